```python
import math
import jax, jax.numpy as jnp
from jax import lax
import numpy as np

D_MODEL = 1024
BATCH = 8
SEQ = 4096
DEPTH = 1

N_Q_HEADS = 8
N_KV_HEADS = 2
HEAD_DIM = 64
ATTN_WIDTH = N_Q_HEADS * HEAD_DIM
KV_WIDTH = N_KV_HEADS * HEAD_DIM
WINDOW = 128
BLOCK = 128
ROPE_DIM = HEAD_DIM // 4
ROPE_THETA = 500000.0
SSM_WIDTH = D_MODEL // 2
SSM_GROUP = 16
N_SSM_GROUPS = SSM_WIDTH // SSM_GROUP
SSM_STATE = 64
N_DIRS = 2
DT_MIN = 1e-3
DT_MAX = 1e-1
MIX_WIDTH = ATTN_WIDTH + SSM_WIDTH
IN_WIDTH = ATTN_WIDTH + 2 * KV_WIDTH + SSM_WIDTH
D_FF = 2816
CONV_WIDTH = 3
EPS = 1e-6

kernel_name = "hymba_s5_swa_convffn_encoder"

F32 = jnp.float32


def rms_norm(x, g):
    xf = x.astype(F32)
    y = xf * lax.rsqrt(jnp.mean(xf * xf, axis=-1, keepdims=True) + EPS)
    return (y * g.astype(F32)).astype(x.dtype)


def partial_rope(t, pos):
    half = ROPE_DIM // 2
    inv_freq = jnp.power(ROPE_THETA, -jnp.arange(half, dtype=F32) / half)
    ang = pos.astype(F32)[:, None] * inv_freq[None, :]
    cos = jnp.cos(ang)[None, :, None, :]
    sin = jnp.sin(ang)[None, :, None, :]
    tf = t.astype(F32)
    t1 = tf[..., :half]
    t2 = tf[..., half:ROPE_DIM]
    rest = tf[..., ROPE_DIM:]
    out = jnp.concatenate([t1 * cos - t2 * sin, t2 * cos + t1 * sin, rest], axis=-1)
    return out.astype(t.dtype)


def window_attention(q, k, v, sink):
    b, l = q.shape[0], q.shape[1]
    nb = l // BLOCK
    grp = N_Q_HEADS // N_KV_HEADS
    qb = q.astype(F32).reshape(b, nb, BLOCK, N_KV_HEADS, grp, HEAD_DIM)

    def band(t):
        tp = jnp.pad(t.astype(F32), ((0, 0), (BLOCK, BLOCK), (0, 0), (0, 0)))
        tb = tp.reshape(b, nb + 2, BLOCK, N_KV_HEADS, HEAD_DIM)
        return jnp.concatenate([tb[:, :-2], tb[:, 1:-1], tb[:, 2:]], axis=2)

    kw = band(k)
    vw = band(v)
    s = jnp.einsum('bnqhgd,bnkhd->bnhgqk', qb, kw) * (HEAD_DIM ** -0.5)
    qpos = jnp.arange(nb)[:, None] * BLOCK + jnp.arange(BLOCK)[None, :]
    kpos = (jnp.arange(nb)[:, None] - 1) * BLOCK + jnp.arange(3 * BLOCK)[None, :]
    rel = kpos[:, None, :] - qpos[:, :, None]
    valid = (jnp.abs(rel) <= WINDOW) & (kpos[:, None, :] >= 0) & (kpos[:, None, :] < l)
    s = jnp.where(valid[None, :, None, None], s, -jnp.inf)
    sink_l = sink.astype(F32).reshape(N_KV_HEADS, grp)[None, None, :, :, None, None]
    m = jnp.maximum(jnp.max(s, axis=-1, keepdims=True), sink_l)
    p = jnp.exp(s - m)
    denom = jnp.sum(p, axis=-1, keepdims=True) + jnp.exp(sink_l - m)
    o = jnp.einsum('bnhgqk,bnkhd->bnqhgd', p / denom, vw)
    return o.reshape(b, l, ATTN_WIDTH)


def _scan_op(e1, e2):
    a1, x1 = e1
    a2, x2 = e2
    return a1 * a2, a2 * x1 + x2


def s5_bidirectional(u, a_re, a_im, log_step, b_re, b_im, c_re, c_im, d_skip):
    bsz, l = u.shape[0], u.shape[1]
    ug = u.astype(F32).reshape(bsz, l, N_SSM_GROUPS, SSM_GROUP)
    lam = lax.complex(a_re.astype(F32), a_im.astype(F32))
    step = jnp.exp(log_step.astype(F32))[..., None]
    lam_bar = jnp.exp(lam * step)
    bmat = lax.complex(b_re.astype(F32), b_im.astype(F32))
    b_bar = ((lam_bar - 1.0) / lam)[..., None] * bmat
    cmat = lax.complex(c_re.astype(F32), c_im.astype(F32))
    y = d_skip.astype(F32)[None, None] * ug
    for direction, rev in ((0, False), (1, True)):
        bu = jnp.einsum('blgc,gpc->blgp', ug, b_bar[direction])
        a = jnp.broadcast_to(lam_bar[direction][None, None], bu.shape)
        _, states = lax.associative_scan(_scan_op, (a, bu), reverse=rev, axis=1)
        y = y + jnp.real(jnp.einsum('gcp,blgp->blgc', cmat[direction], states))
    return y.reshape(bsz, l, SSM_WIDTH)


def depthwise_conv(t, w, bias):
    c = t.shape[-1]
    pad = CONV_WIDTH // 2
    out = lax.conv_general_dilated(t, w[:, None, :].astype(t.dtype), window_strides=(1,),
                                   padding=((pad, pad),), dimension_numbers=('NWC', 'WIO', 'NWC'),
                                   feature_group_count=c)
    return out + bias.astype(out.dtype)


def _fwd_setup_inputs(seed: int = 0) -> dict:
    key = jax.random.key(seed)
    ks = jax.random.split(key, 24)
    G, P, C = N_SSM_GROUPS, SSM_STATE, SSM_GROUP
    nrm = lambda k, shape: jax.random.normal(k, shape, dtype=F32)
    x = nrm(ks[0], (BATCH, SEQ, D_MODEL))
    norm_mix_g = 1.0 + 0.02 * nrm(ks[1], (DEPTH, D_MODEL))
    w_in = nrm(ks[2], (DEPTH, D_MODEL, IN_WIDTH)) * D_MODEL ** -0.5
    n_idx = jnp.arange(P, dtype=F32)
    a_re = -0.5 + 0.01 * nrm(ks[3], (DEPTH, N_DIRS, G, P))
    a_im = math.pi * n_idx + 0.01 * nrm(ks[4], (DEPTH, N_DIRS, G, P))
    log_step = jax.random.uniform(ks[5], (DEPTH, N_DIRS, G), dtype=F32,
                                  minval=math.log(DT_MIN), maxval=math.log(DT_MAX))
    b_re = nrm(ks[6], (DEPTH, N_DIRS, G, P, C)) * (2.0 * C) ** -0.5
    b_im = nrm(ks[7], (DEPTH, N_DIRS, G, P, C)) * (2.0 * C) ** -0.5
    c_re = nrm(ks[8], (DEPTH, N_DIRS, G, C, P)) * P ** -0.5
    c_im = nrm(ks[9], (DEPTH, N_DIRS, G, C, P)) * P ** -0.5
    d_skip = 0.5 * nrm(ks[10], (DEPTH, G, C))
    w_glu = nrm(ks[11], (DEPTH, SSM_WIDTH, SSM_WIDTH)) * SSM_WIDTH ** -0.5
    sink = 0.5 * nrm(ks[12], (DEPTH, N_Q_HEADS))
    norm_attn_g = 1.0 + 0.02 * nrm(ks[13], (DEPTH, ATTN_WIDTH))
    norm_ssm_g = 1.0 + 0.02 * nrm(ks[14], (DEPTH, SSM_WIDTH))
    w_out = nrm(ks[15], (DEPTH, MIX_WIDTH, D_MODEL)) * MIX_WIDTH ** -0.5
    norm_ffn_g = 1.0 + 0.02 * nrm(ks[16], (DEPTH, D_MODEL))
    w_up = nrm(ks[17], (DEPTH, D_MODEL, 2 * D_FF)) * D_MODEL ** -0.5
    conv_w = nrm(ks[18], (DEPTH, CONV_WIDTH, 2 * D_FF)) * CONV_WIDTH ** -0.5
    conv_b = 0.02 * nrm(ks[19], (DEPTH, 2 * D_FF))
    w_down = nrm(ks[20], (DEPTH, D_FF, D_MODEL)) * D_FF ** -0.5
    norm_final_g = 1.0 + 0.02 * nrm(ks[21], (D_MODEL,))
    return {"x": x, "norm_mix_g": norm_mix_g, "w_in": w_in, "a_re": a_re, "a_im": a_im,
            "log_step": log_step, "b_re": b_re, "b_im": b_im, "c_re": c_re, "c_im": c_im,
            "d_skip": d_skip, "w_glu": w_glu, "sink": sink, "norm_attn_g": norm_attn_g,
            "norm_ssm_g": norm_ssm_g, "w_out": w_out, "norm_ffn_g": norm_ffn_g, "w_up": w_up,
            "conv_w": conv_w, "conv_b": conv_b, "w_down": w_down, "norm_final_g": norm_final_g}


def _fwd_reference(x, norm_mix_g, w_in, a_re, a_im, log_step, b_re, b_im, c_re, c_im, d_skip, w_glu,
              sink, norm_attn_g, norm_ssm_g, w_out, norm_ffn_g, w_up, conv_w, conv_b, w_down,
              norm_final_g):
    bsz, l = x.shape[0], x.shape[1]
    pos = jnp.arange(l)
    for i in range(DEPTH):
        h = rms_norm(x, norm_mix_g[i])
        proj = h @ w_in[i]
        q = proj[..., :ATTN_WIDTH].reshape(bsz, l, N_Q_HEADS, HEAD_DIM)
        k = proj[..., ATTN_WIDTH:ATTN_WIDTH + KV_WIDTH].reshape(bsz, l, N_KV_HEADS, HEAD_DIM)
        v = proj[..., ATTN_WIDTH + KV_WIDTH:ATTN_WIDTH + 2 * KV_WIDTH].reshape(bsz, l, N_KV_HEADS, HEAD_DIM)
        u = proj[..., ATTN_WIDTH + 2 * KV_WIDTH:]
        q = partial_rope(q, pos)
        k = partial_rope(k, pos)
        attn = window_attention(q, k, v, sink[i])
        ys = s5_bidirectional(u, a_re[i], a_im[i], log_step[i], b_re[i], b_im[i],
                              c_re[i], c_im[i], d_skip[i])
        ys = jax.nn.gelu(ys, approximate=False)
        ys = ys * jax.nn.sigmoid(ys @ w_glu[i].astype(F32))
        mixed = jnp.concatenate([rms_norm(attn, norm_attn_g[i]), rms_norm(ys, norm_ssm_g[i])], axis=-1)
        x = x + (mixed.astype(x.dtype) @ w_out[i]).astype(x.dtype)
        h = rms_norm(x, norm_ffn_g[i])
        up = depthwise_conv(h @ w_up[i], conv_w[i], conv_b[i])
        gate = up[..., :D_FF]
        val = up[..., D_FF:]
        x = x + ((jax.nn.silu(gate) * val) @ w_down[i]).astype(x.dtype)
    return rms_norm(x, norm_final_g)


import jax as _jax
import jax.numpy as _jnp

TWIN_FORMAT = 'train_step'
FWD_PARAMS = ['x', 'norm_mix_g', 'w_in', 'a_re', 'a_im', 'log_step', 'b_re', 'b_im', 'c_re', 'c_im', 'd_skip', 'w_glu', 'sink', 'norm_attn_g', 'norm_ssm_g', 'w_out', 'norm_ffn_g', 'w_up', 'conv_w', 'conv_b', 'w_down', 'norm_final_g']
TWIN_WEIGHTS = ['norm_mix_g', 'w_in', 'a_re', 'a_im', 'log_step', 'b_re', 'b_im', 'c_re', 'c_im', 'd_skip', 'w_glu', 'sink', 'norm_attn_g', 'norm_ssm_g', 'w_out', 'norm_ffn_g', 'w_up', 'conv_w', 'conv_b', 'w_down', 'norm_final_g']
TWIN_DIFF_INPUT = 'x'
TWIN_INPUTS = ['x', 'norm_mix_g', 'w_in', 'a_re', 'a_im', 'log_step', 'b_re', 'b_im', 'c_re', 'c_im', 'd_skip', 'w_glu', 'sink', 'norm_attn_g', 'norm_ssm_g', 'w_out', 'norm_ffn_g', 'w_up', 'conv_w', 'conv_b', 'w_down', 'norm_final_g', 'loss_target', 'm_norm_mix_g', 'm_w_in', 'm_a_re', 'm_a_im', 'm_log_step', 'm_b_re', 'm_b_im', 'm_c_re', 'm_c_im', 'm_d_skip', 'm_w_glu', 'm_sink', 'm_norm_attn_g', 'm_norm_ssm_g', 'm_w_out', 'm_norm_ffn_g', 'm_w_up', 'm_conv_w', 'm_conv_b', 'm_w_down', 'm_norm_final_g', 'v_norm_mix_g', 'v_w_in', 'v_a_re', 'v_a_im', 'v_log_step', 'v_b_re', 'v_b_im', 'v_c_re', 'v_c_im', 'v_d_skip', 'v_w_glu', 'v_sink', 'v_norm_attn_g', 'v_norm_ssm_g', 'v_w_out', 'v_norm_ffn_g', 'v_w_up', 'v_conv_w', 'v_conv_b', 'v_w_down', 'v_norm_final_g']
TWIN_OUTPUTS = ['loss', 'grad_x', 'grad_norm_mix_g', 'grad_w_in', 'grad_a_re', 'grad_a_im', 'grad_log_step', 'grad_b_re', 'grad_b_im', 'grad_c_re', 'grad_c_im', 'grad_d_skip', 'grad_w_glu', 'grad_sink', 'grad_norm_attn_g', 'grad_norm_ssm_g', 'grad_w_out', 'grad_norm_ffn_g', 'grad_w_up', 'grad_conv_w', 'grad_conv_b', 'grad_w_down', 'grad_norm_final_g', 'delta_norm_mix_g', 'delta_w_in', 'delta_a_re', 'delta_a_im', 'delta_log_step', 'delta_b_re', 'delta_b_im', 'delta_c_re', 'delta_c_im', 'delta_d_skip', 'delta_w_glu', 'delta_sink', 'delta_norm_attn_g', 'delta_norm_ssm_g', 'delta_w_out', 'delta_norm_ffn_g', 'delta_w_up', 'delta_conv_w', 'delta_conv_b', 'delta_w_down', 'delta_norm_final_g', 'new_m_norm_mix_g', 'new_m_w_in', 'new_m_a_re', 'new_m_a_im', 'new_m_log_step', 'new_m_b_re', 'new_m_b_im', 'new_m_c_re', 'new_m_c_im', 'new_m_d_skip', 'new_m_w_glu', 'new_m_sink', 'new_m_norm_attn_g', 'new_m_norm_ssm_g', 'new_m_w_out', 'new_m_norm_ffn_g', 'new_m_w_up', 'new_m_conv_w', 'new_m_conv_b', 'new_m_w_down', 'new_m_norm_final_g', 'new_v_norm_mix_g', 'new_v_w_in', 'new_v_a_re', 'new_v_a_im', 'new_v_log_step', 'new_v_b_re', 'new_v_b_im', 'new_v_c_re', 'new_v_c_im', 'new_v_d_skip', 'new_v_w_glu', 'new_v_sink', 'new_v_norm_attn_g', 'new_v_norm_ssm_g', 'new_v_w_out', 'new_v_norm_ffn_g', 'new_v_w_up', 'new_v_conv_w', 'new_v_conv_b', 'new_v_w_down', 'new_v_norm_final_g']
TWIN_LEAF_KINDS = {'loss': 'loss', 'grad_x': 'grad_x', 'grad_norm_mix_g': 'grad_w', 'grad_w_in': 'grad_w', 'grad_a_re': 'grad_w', 'grad_a_im': 'grad_w', 'grad_log_step': 'grad_w', 'grad_b_re': 'grad_w', 'grad_b_im': 'grad_w', 'grad_c_re': 'grad_w', 'grad_c_im': 'grad_w', 'grad_d_skip': 'grad_w', 'grad_w_glu': 'grad_w', 'grad_sink': 'grad_w', 'grad_norm_attn_g': 'grad_w', 'grad_norm_ssm_g': 'grad_w', 'grad_w_out': 'grad_w', 'grad_norm_ffn_g': 'grad_w', 'grad_w_up': 'grad_w', 'grad_conv_w': 'grad_w', 'grad_conv_b': 'grad_w', 'grad_w_down': 'grad_w', 'grad_norm_final_g': 'grad_w', 'delta_norm_mix_g': 'delta_w', 'delta_w_in': 'delta_w', 'delta_a_re': 'delta_w', 'delta_a_im': 'delta_w', 'delta_log_step': 'delta_w', 'delta_b_re': 'delta_w', 'delta_b_im': 'delta_w', 'delta_c_re': 'delta_w', 'delta_c_im': 'delta_w', 'delta_d_skip': 'delta_w', 'delta_w_glu': 'delta_w', 'delta_sink': 'delta_w', 'delta_norm_attn_g': 'delta_w', 'delta_norm_ssm_g': 'delta_w', 'delta_w_out': 'delta_w', 'delta_norm_ffn_g': 'delta_w', 'delta_w_up': 'delta_w', 'delta_conv_w': 'delta_w', 'delta_conv_b': 'delta_w', 'delta_w_down': 'delta_w', 'delta_norm_final_g': 'delta_w', 'new_m_norm_mix_g': 'new_m', 'new_m_w_in': 'new_m', 'new_m_a_re': 'new_m', 'new_m_a_im': 'new_m', 'new_m_log_step': 'new_m', 'new_m_b_re': 'new_m', 'new_m_b_im': 'new_m', 'new_m_c_re': 'new_m', 'new_m_c_im': 'new_m', 'new_m_d_skip': 'new_m', 'new_m_w_glu': 'new_m', 'new_m_sink': 'new_m', 'new_m_norm_attn_g': 'new_m', 'new_m_norm_ssm_g': 'new_m', 'new_m_w_out': 'new_m', 'new_m_norm_ffn_g': 'new_m', 'new_m_w_up': 'new_m', 'new_m_conv_w': 'new_m', 'new_m_conv_b': 'new_m', 'new_m_w_down': 'new_m', 'new_m_norm_final_g': 'new_m', 'new_v_norm_mix_g': 'new_v', 'new_v_w_in': 'new_v', 'new_v_a_re': 'new_v', 'new_v_a_im': 'new_v', 'new_v_log_step': 'new_v', 'new_v_b_re': 'new_v', 'new_v_b_im': 'new_v', 'new_v_c_re': 'new_v', 'new_v_c_im': 'new_v', 'new_v_d_skip': 'new_v', 'new_v_w_glu': 'new_v', 'new_v_sink': 'new_v', 'new_v_norm_attn_g': 'new_v', 'new_v_norm_ssm_g': 'new_v', 'new_v_w_out': 'new_v', 'new_v_norm_ffn_g': 'new_v', 'new_v_w_up': 'new_v', 'new_v_conv_w': 'new_v', 'new_v_conv_b': 'new_v', 'new_v_w_down': 'new_v', 'new_v_norm_final_g': 'new_v'}


def _forward(args):
    return _fwd_reference(*[args[k] for k in FWD_PARAMS])


def _output_shape():
    def fwd():
        inp = _fwd_setup_inputs(0)
        return _fwd_reference(*[inp[k] for k in FWD_PARAMS])
    out = _jax.eval_shape(fwd)
    return out.shape, out.dtype

N_MICROBATCH = 1
ADAM_LR = 0.001
ADAM_B1 = 0.9
ADAM_B2 = 0.999
ADAM_EPS = 1e-08
ADAM_WD = 0.01
ADAM_STEP = 10
PER_EXAMPLE_BATCH_AXIS = {'x': 0, 'loss_target': 0}
SHARED_INPUTS = []
_WEIGHT_DTYPES = {'norm_mix_g': _jnp.float32, 'w_in': _jnp.float32, 'a_re': _jnp.float32, 'a_im': _jnp.float32, 'log_step': _jnp.float32, 'b_re': _jnp.float32, 'b_im': _jnp.float32, 'c_re': _jnp.float32, 'c_im': _jnp.float32, 'd_skip': _jnp.float32, 'w_glu': _jnp.float32, 'sink': _jnp.float32, 'norm_attn_g': _jnp.float32, 'norm_ssm_g': _jnp.float32, 'w_out': _jnp.float32, 'norm_ffn_g': _jnp.float32, 'w_up': _jnp.float32, 'conv_w': _jnp.float32, 'conv_b': _jnp.float32, 'w_down': _jnp.float32, 'norm_final_g': _jnp.float32}
MOMENT_SCALE = {'norm_mix_g': 2.191902e-01, 'w_in': 1.740062e-01, 'a_re': 2.049251e-02, 'a_im': 1.839727e-02, 'log_step': 8.356995e+00, 'b_re': 1.274983e-02, 'b_im': 1.286561e-02, 'c_re': 1.836186e-02, 'c_im': 1.783418e-02, 'd_skip': 2.904266e-01, 'w_glu': 2.110848e-02, 'sink': 2.617174e-03, 'norm_attn_g': 1.436272e-01, 'norm_ssm_g': 1.446507e-01, 'w_out': 1.396038e-01, 'norm_ffn_g': 9.641647e-02, 'w_up': 4.232713e-02, 'conv_w': 4.327661e-02, 'conv_b': 4.362053e-02, 'w_down': 6.920310e-02, 'norm_final_g': 3.204666e+01}


def _to_microbatches(a, axis):
    t = _jnp.moveaxis(a, axis, 0)
    t = t.reshape((N_MICROBATCH, t.shape[0] // N_MICROBATCH) + t.shape[1:])
    return _jnp.moveaxis(t, 1, axis + 1)


def setup_inputs(seed: int = 0) -> dict:
    inp = _fwd_setup_inputs(seed)
    key = _jax.random.fold_in(_jax.random.key(seed), 7919)
    shape, _ = _output_shape()
    out = dict(inp)
    out["loss_target"] = _jax.random.normal(_jax.random.fold_in(key, 0), shape, _jnp.float32)
    for i, name in enumerate(TWIN_WEIGHTS):
        w = inp[name].astype(_jnp.float32)
        if MOMENT_SCALE is None:
            s = _jnp.sqrt(_jnp.mean(_jnp.square(w)) + 1e-30)
        else:
            s = MOMENT_SCALE[name]
        km, kv = _jax.random.split(_jax.random.fold_in(key, i + 1))
        out[name] = w
        out["m_" + name] = s * _jax.random.normal(km, w.shape, _jnp.float32)
        out["v_" + name] = (s * s) * _jax.random.uniform(kv, w.shape, _jnp.float32, 0.5, 1.5)
    if N_MICROBATCH > 1:
        for name, axis in PER_EXAMPLE_BATCH_AXIS.items():
            out[name] = _to_microbatches(out[name], axis)
    return {'x': out['x'], 'norm_mix_g': out['norm_mix_g'], 'w_in': out['w_in'], 'a_re': out['a_re'], 'a_im': out['a_im'], 'log_step': out['log_step'], 'b_re': out['b_re'], 'b_im': out['b_im'], 'c_re': out['c_re'], 'c_im': out['c_im'], 'd_skip': out['d_skip'], 'w_glu': out['w_glu'], 'sink': out['sink'], 'norm_attn_g': out['norm_attn_g'], 'norm_ssm_g': out['norm_ssm_g'], 'w_out': out['w_out'], 'norm_ffn_g': out['norm_ffn_g'], 'w_up': out['w_up'], 'conv_w': out['conv_w'], 'conv_b': out['conv_b'], 'w_down': out['w_down'], 'norm_final_g': out['norm_final_g'], 'loss_target': out['loss_target'], 'm_norm_mix_g': out['m_norm_mix_g'], 'm_w_in': out['m_w_in'], 'm_a_re': out['m_a_re'], 'm_a_im': out['m_a_im'], 'm_log_step': out['m_log_step'], 'm_b_re': out['m_b_re'], 'm_b_im': out['m_b_im'], 'm_c_re': out['m_c_re'], 'm_c_im': out['m_c_im'], 'm_d_skip': out['m_d_skip'], 'm_w_glu': out['m_w_glu'], 'm_sink': out['m_sink'], 'm_norm_attn_g': out['m_norm_attn_g'], 'm_norm_ssm_g': out['m_norm_ssm_g'], 'm_w_out': out['m_w_out'], 'm_norm_ffn_g': out['m_norm_ffn_g'], 'm_w_up': out['m_w_up'], 'm_conv_w': out['m_conv_w'], 'm_conv_b': out['m_conv_b'], 'm_w_down': out['m_w_down'], 'm_norm_final_g': out['m_norm_final_g'], 'v_norm_mix_g': out['v_norm_mix_g'], 'v_w_in': out['v_w_in'], 'v_a_re': out['v_a_re'], 'v_a_im': out['v_a_im'], 'v_log_step': out['v_log_step'], 'v_b_re': out['v_b_re'], 'v_b_im': out['v_b_im'], 'v_c_re': out['v_c_re'], 'v_c_im': out['v_c_im'], 'v_d_skip': out['v_d_skip'], 'v_w_glu': out['v_w_glu'], 'v_sink': out['v_sink'], 'v_norm_attn_g': out['v_norm_attn_g'], 'v_norm_ssm_g': out['v_norm_ssm_g'], 'v_w_out': out['v_w_out'], 'v_norm_ffn_g': out['v_norm_ffn_g'], 'v_w_up': out['v_w_up'], 'v_conv_w': out['v_conv_w'], 'v_conv_b': out['v_conv_b'], 'v_w_down': out['v_w_down'], 'v_norm_final_g': out['v_norm_final_g']}


def _loss(weights, diff, rest, loss_target):
    with _jax.named_scope("forward"):
        args = {**rest, TWIN_DIFF_INPUT: diff, **{k: w.astype(_WEIGHT_DTYPES[k]) for k, w in weights.items()}}
        y = _forward(args)
    with _jax.named_scope("loss_head"):
        err = _jnp.square(y.astype(_jnp.float32) - loss_target)
        return 0.5 * _jnp.sum(_jnp.mean(err, axis=-1)) if err.ndim else 0.5 * err


def _adamw(w, g, m, v):
    m = ADAM_B1 * m + (1.0 - ADAM_B1) * g
    v = ADAM_B2 * v + (1.0 - ADAM_B2) * _jnp.square(g)
    m_hat = m / (1.0 - ADAM_B1 ** ADAM_STEP)
    v_hat = v / (1.0 - ADAM_B2 ** ADAM_STEP)
    delta = -ADAM_LR * (m_hat / (_jnp.sqrt(v_hat) + ADAM_EPS) + ADAM_WD * w)
    return delta, m, v


def reference(x, norm_mix_g, w_in, a_re, a_im, log_step, b_re, b_im, c_re, c_im, d_skip, w_glu, sink, norm_attn_g, norm_ssm_g, w_out, norm_ffn_g, w_up, conv_w, conv_b, w_down, norm_final_g, loss_target, m_norm_mix_g, m_w_in, m_a_re, m_a_im, m_log_step, m_b_re, m_b_im, m_c_re, m_c_im, m_d_skip, m_w_glu, m_sink, m_norm_attn_g, m_norm_ssm_g, m_w_out, m_norm_ffn_g, m_w_up, m_conv_w, m_conv_b, m_w_down, m_norm_final_g, v_norm_mix_g, v_w_in, v_a_re, v_a_im, v_log_step, v_b_re, v_b_im, v_c_re, v_c_im, v_d_skip, v_w_glu, v_sink, v_norm_attn_g, v_norm_ssm_g, v_w_out, v_norm_ffn_g, v_w_up, v_conv_w, v_conv_b, v_w_down, v_norm_final_g):
    given = dict(x=x, norm_mix_g=norm_mix_g, w_in=w_in, a_re=a_re, a_im=a_im, log_step=log_step, b_re=b_re, b_im=b_im, c_re=c_re, c_im=c_im, d_skip=d_skip, w_glu=w_glu, sink=sink, norm_attn_g=norm_attn_g, norm_ssm_g=norm_ssm_g, w_out=w_out, norm_ffn_g=norm_ffn_g, w_up=w_up, conv_w=conv_w, conv_b=conv_b, w_down=w_down, norm_final_g=norm_final_g, loss_target=loss_target, m_norm_mix_g=m_norm_mix_g, m_w_in=m_w_in, m_a_re=m_a_re, m_a_im=m_a_im, m_log_step=m_log_step, m_b_re=m_b_re, m_b_im=m_b_im, m_c_re=m_c_re, m_c_im=m_c_im, m_d_skip=m_d_skip, m_w_glu=m_w_glu, m_sink=m_sink, m_norm_attn_g=m_norm_attn_g, m_norm_ssm_g=m_norm_ssm_g, m_w_out=m_w_out, m_norm_ffn_g=m_norm_ffn_g, m_w_up=m_w_up, m_conv_w=m_conv_w, m_conv_b=m_conv_b, m_w_down=m_w_down, m_norm_final_g=m_norm_final_g, v_norm_mix_g=v_norm_mix_g, v_w_in=v_w_in, v_a_re=v_a_re, v_a_im=v_a_im, v_log_step=v_log_step, v_b_re=v_b_re, v_b_im=v_b_im, v_c_re=v_c_re, v_c_im=v_c_im, v_d_skip=v_d_skip, v_w_glu=v_w_glu, v_sink=v_sink, v_norm_attn_g=v_norm_attn_g, v_norm_ssm_g=v_norm_ssm_g, v_w_out=v_w_out, v_norm_ffn_g=v_norm_ffn_g, v_w_up=v_w_up, v_conv_w=v_conv_w, v_conv_b=v_conv_b, v_w_down=v_w_down, v_norm_final_g=v_norm_final_g)
    weights = {n: given[n] for n in TWIN_WEIGHTS}
    shared = {n: given[n] for n in SHARED_INPUTS}
    per_example = {n: given[n] for n in ['x']}
    grad_fn = _jax.value_and_grad(_loss, argnums=(0, 1))

    def one_microbatch(ex, loss_target):
        ex = dict(ex)
        diff = ex.pop(TWIN_DIFF_INPUT)
        return grad_fn(weights, diff, {**shared, **ex}, loss_target)

    if N_MICROBATCH == 1:
        loss, (grad_w, grad_x) = one_microbatch(per_example, given["loss_target"])
    else:
        def body(carry, xs):
            loss_sum, grad_sum = carry
            l_k, (gw_k, gx_k) = one_microbatch(xs[0], xs[1])
            with _jax.named_scope("update"):
                return (loss_sum + l_k, _jax.tree.map(_jnp.add, grad_sum, gw_k)), gx_k

        init = (_jnp.zeros((), _jnp.float32), _jax.tree.map(_jnp.zeros_like, weights))
        (loss, grad_w), grad_x = _jax.lax.scan(body, init, (per_example, given["loss_target"]))
    with _jax.named_scope("update"):
        delta_w, new_m, new_v = {}, {}, {}
        for n in TWIN_WEIGHTS:
            delta_w[n], new_m[n], new_v[n] = _adamw(weights[n], grad_w[n], given["m_" + n], given["v_" + n])
    return (loss, grad_x, *[grad_w[n] for n in TWIN_WEIGHTS], *[delta_w[n] for n in TWIN_WEIGHTS],
            *[new_m[n] for n in TWIN_WEIGHTS], *[new_v[n] for n in TWIN_WEIGHTS])
```

```python
import functools
import math

import jax
import jax.numpy as jnp
from jax import lax
from jax.experimental import pallas as pl
from jax.experimental.pallas import tpu as pltpu

F32 = jnp.float32
BF16 = jnp.bfloat16

L = 4096
D = 1024
NQ, NKV, HD = 8, 2, 64
AW = NQ * HD
KVW = NKV * HD
SW = 512
NG, GC, NP = 32, 16, 64
INW = AW + 2 * KVW + SW
DFF = 2816
BLK = 128
WIN = 3 * BLK
EPS = 1e-6
ROPE_THETA = 500000.0
NSEG = 8
TSEG = L // NSEG
SBW = 256
NSB = NG * NP // SBW
NDEV = 8
MESH_AXES = ("x", "y", "c")

LR, B1, B2, AEPS, WD, STEP = 0.001, 0.9, 0.999, 1e-08, 0.01, 10
C1 = 1.0 - B1 ** STEP
C2 = 1.0 - B2 ** STEP

VMEM_LIMIT = 56 * 1024 * 1024


def _pick(n, target, mult):
    best = None
    for t in range(mult, min(n, target) + 1, mult):
        if n % t == 0:
            best = t
    return best if best is not None else n


def _cp(sem):
    return pltpu.CompilerParams(dimension_semantics=sem, vmem_limit_bytes=VMEM_LIMIT)


def _mm(a, b, *, ta=False, tb=False, out_dtype=F32, add=None, name, tm=512, tn=512, tk=512):
    m, k = (a.shape[1], a.shape[0]) if ta else a.shape
    n = b.shape[0] if tb else b.shape[1]
    assert k == (b.shape[1] if tb else b.shape[0])
    tm, tn, tk = _pick(m, tm, 128), _pick(n, tn, 128), _pick(k, tk, 128)
    nk = k // tk
    dn = (((0 if ta else 1,), (1 if tb else 0,)), ((), ()))

    def body(a_ref, b_ref, *rest):
        if add is None:
            o_ref, acc_ref = rest
        else:
            add_ref, o_ref, acc_ref = rest
        kk = pl.program_id(2)

        @pl.when(kk == 0)
        def _():
            acc_ref[...] = jnp.zeros_like(acc_ref)

        acc_ref[...] += lax.dot_general(a_ref[...].astype(BF16), b_ref[...].astype(BF16), dn,
                                        preferred_element_type=F32)

        @pl.when(kk == nk - 1)
        def _():
            r = acc_ref[...]
            if add is not None:
                r = r + add_ref[...]
            o_ref[...] = r.astype(out_dtype)

    a_spec = pl.BlockSpec((tk, tm), lambda i, j, kk: (kk, i)) if ta else pl.BlockSpec((tm, tk), lambda i, j, kk: (i, kk))
    b_spec = pl.BlockSpec((tn, tk), lambda i, j, kk: (j, kk)) if tb else pl.BlockSpec((tk, tn), lambda i, j, kk: (kk, j))
    in_specs = [a_spec, b_spec]
    args = [a, b]
    if add is not None:
        in_specs.append(pl.BlockSpec((tm, tn), lambda i, j, kk: (i, j)))
        args.append(add)
    return pl.pallas_call(
        body, name=name, grid=(m // tm, n // tn, nk),
        in_specs=in_specs, out_specs=pl.BlockSpec((tm, tn), lambda i, j, kk: (i, j)),
        out_shape=jax.ShapeDtypeStruct((m, n), out_dtype),
        scratch_shapes=[pltpu.VMEM((tm, tn), F32)],
        compiler_params=_cp(("parallel", "parallel", "arbitrary")),
    )(*args)


TL = 512


def _rms_fwd(x, g, name):
    d = x.shape[1]

    def body(x_ref, g_ref, o_ref):
        xv = x_ref[...]
        r = lax.rsqrt(jnp.mean(xv * xv, axis=-1, keepdims=True) + EPS)
        o_ref[...] = (xv * r * g_ref[...]).astype(BF16)

    return pl.pallas_call(
        body, name=name, grid=(L // TL,),
        in_specs=[pl.BlockSpec((TL, d), lambda i: (i, 0)), pl.BlockSpec((1, d), lambda i: (0, 0))],
        out_specs=pl.BlockSpec((TL, d), lambda i: (i, 0)),
        out_shape=jax.ShapeDtypeStruct((L, d), BF16),
        compiler_params=_cp(("parallel",)),
    )(x, g)


def _rms_bwd_tile(xv, gv, dh):
    r = lax.rsqrt(jnp.mean(xv * xv, axis=-1, keepdims=True) + EPS)
    a = dh * gv
    dx = r * a - xv * (r * r * r) * jnp.mean(a * xv, axis=-1, keepdims=True)
    dg = jnp.sum(dh * xv * r, axis=0, keepdims=True)
    return dx, dg


def _rms_bwd(x, g, dh, dres, name):
    d = x.shape[1]

    def body(x_ref, g_ref, dh_ref, dres_ref, dx_ref, dg_ref):
        dx, dg = _rms_bwd_tile(x_ref[...], g_ref[...], dh_ref[...])
        dx_ref[...] = dx + dres_ref[...]

        @pl.when(pl.program_id(0) == 0)
        def _():
            dg_ref[...] = jnp.zeros_like(dg_ref)

        dg_ref[...] += dg

    row = pl.BlockSpec((TL, d), lambda i: (i, 0))
    vec = pl.BlockSpec((1, d), lambda i: (0, 0))
    return pl.pallas_call(
        body, name=name, grid=(L // TL,),
        in_specs=[row, vec, row, row], out_specs=[row, vec],
        out_shape=[jax.ShapeDtypeStruct((L, d), F32), jax.ShapeDtypeStruct((1, d), F32)],
        compiler_params=_cp(("arbitrary",)),
    )(x, g, dh, dres)


def _mixnorm_fwd(attn, ysg, ga, gs):
    def body(a_ref, s_ref, ga_ref, gs_ref, o_ref):
        for src, gr, lo in ((a_ref, ga_ref, 0), (s_ref, gs_ref, AW)):
            xv = src[...]
            r = lax.rsqrt(jnp.mean(xv * xv, axis=-1, keepdims=True) + EPS)
            o_ref[:, lo:lo + 512] = (xv * r * gr[...]).astype(BF16)

    row = pl.BlockSpec((TL, 512), lambda i: (i, 0))
    vec = pl.BlockSpec((1, 512), lambda i: (0, 0))
    return pl.pallas_call(
        body, name="mixnorm_fwd", grid=(L // TL,),
        in_specs=[row, row, vec, vec], out_specs=pl.BlockSpec((TL, 1024), lambda i: (i, 0)),
        out_shape=jax.ShapeDtypeStruct((L, 1024), BF16),
        compiler_params=_cp(("parallel",)),
    )(attn, ysg, ga, gs)


def _mixnorm_bwd(attn, ysg, ga, gs, dmixed):
    def body(a_ref, s_ref, ga_ref, gs_ref, dm_ref, da_ref, ds_ref, dga_ref, dgs_ref):
        @pl.when(pl.program_id(0) == 0)
        def _():
            dga_ref[...] = jnp.zeros_like(dga_ref)
            dgs_ref[...] = jnp.zeros_like(dgs_ref)

        dxa, dga = _rms_bwd_tile(a_ref[...], ga_ref[...], dm_ref[:, 0:AW])
        da_ref[...] = dxa
        dga_ref[...] += dga
        dxs, dgs = _rms_bwd_tile(s_ref[...], gs_ref[...], dm_ref[:, AW:AW + SW])
        ds_ref[...] = dxs
        dgs_ref[...] += dgs

    row = pl.BlockSpec((TL, 512), lambda i: (i, 0))
    vec = pl.BlockSpec((1, 512), lambda i: (0, 0))
    return pl.pallas_call(
        body, name="mixnorm_bwd", grid=(L // TL,),
        in_specs=[row, row, vec, vec, pl.BlockSpec((TL, 1024), lambda i: (i, 0))],
        out_specs=[row, row, vec, vec],
        out_shape=[jax.ShapeDtypeStruct((L, 512), F32), jax.ShapeDtypeStruct((L, 512), F32),
                   jax.ShapeDtypeStruct((1, 512), F32), jax.ShapeDtypeStruct((1, 512), F32)],
        compiler_params=_cp(("arbitrary",)),
    )(attn, ysg, ga, gs, dmixed)


def _final_loss(x2, tgt, g):
    def body(x_ref, t_ref, g_ref, loss_ref, dx_ref, dg_ref):
        @pl.when(pl.program_id(0) == 0)
        def _():
            loss_ref[...] = jnp.zeros_like(loss_ref)
            dg_ref[...] = jnp.zeros_like(dg_ref)

        xv = x_ref[...]
        gv = g_ref[...]
        r = lax.rsqrt(jnp.mean(xv * xv, axis=-1, keepdims=True) + EPS)
        e = xv * r * gv - t_ref[...]
        loss_ref[...] += 0.5 * jnp.sum(jnp.mean(e * e, axis=-1, keepdims=True), axis=0, keepdims=True)
        dy = e * (1.0 / D)
        a = dy * gv
        dx_ref[...] = r * a - xv * (r * r * r) * jnp.mean(a * xv, axis=-1, keepdims=True)
        dg_ref[...] += jnp.sum(dy * xv * r, axis=0, keepdims=True)

    row = pl.BlockSpec((TL, D), lambda i: (i, 0))
    vec = pl.BlockSpec((1, D), lambda i: (0, 0))
    return pl.pallas_call(
        body, name="final_loss", grid=(L // TL,),
        in_specs=[row, row, vec],
        out_specs=[pl.BlockSpec((1, 1), lambda i: (0, 0)), row, vec],
        out_shape=[jax.ShapeDtypeStruct((1, 1), F32), jax.ShapeDtypeStruct((L, D), F32),
                   jax.ShapeDtypeStruct((1, D), F32)],
        compiler_params=_cp(("arbitrary",)),
    )(x2, tgt, g)


def _rope_tables():
    half = HD // 8
    inv_freq = jnp.power(ROPE_THETA, -jnp.arange(half, dtype=F32) / half)
    ang = jnp.arange(L, dtype=F32)[:, None] * inv_freq[None, :]
    cos, sin = jnp.cos(ang), jnp.sin(ang)
    one = jnp.ones((L, HD - 2 * half), F32)
    zero = jnp.zeros((L, HD - 2 * half), F32)
    zh = jnp.zeros((L, half), F32)
    cos64 = jnp.concatenate([cos, cos, one], axis=1)
    sa64 = jnp.concatenate([-sin, zh, zero], axis=1)
    sb64 = jnp.concatenate([zh, sin, zero], axis=1)
    return [jnp.tile(t, (1, 2)) for t in (cos64, sa64, sb64)]


def _rope(xv, cosv, sav, sbv, sign):
    return xv * cosv + sign * (pltpu.roll(xv, 120, 1) * sav + pltpu.roll(xv, 8, 1) * sbv)


def _rope_fwd(proj, tabs):
    def body(p_ref, c_ref, sa_ref, sb_ref, o_ref):
        xv = p_ref[...]
        roped = _rope(xv, c_ref[...], sa_ref[...], sb_ref[...], 1.0)
        o_ref[...] = jnp.where(pl.program_id(1) < 5, roped, xv).astype(BF16)

    blk = pl.BlockSpec((TL, 128), lambda i, j: (i, j))
    tab = pl.BlockSpec((TL, 128), lambda i, j: (i, 0))
    return pl.pallas_call(
        body, name="rope_fwd", grid=(L // TL, 6),
        in_specs=[blk, tab, tab, tab], out_specs=blk,
        out_shape=jax.ShapeDtypeStruct((L, AW + 2 * KVW), BF16),
        compiler_params=_cp(("parallel", "parallel")),
    )(proj, *tabs)


def _rope_bwd(dq, dk, dv, du, tabs):
    def body(dq_ref, dk_ref, dv_ref, du_ref, c_ref, sa_ref, sb_ref, o_ref):
        j = pl.program_id(1)

        @pl.when(j < 4)
        def _():
            o_ref[...] = _rope(dq_ref[...], c_ref[...], sa_ref[...], sb_ref[...], -1.0).astype(BF16)

        @pl.when(j == 4)
        def _():
            o_ref[...] = _rope(dk_ref[...], c_ref[...], sa_ref[...], sb_ref[...], -1.0).astype(BF16)

        @pl.when(j == 5)
        def _():
            o_ref[...] = dv_ref[...].astype(BF16)

        @pl.when(j > 5)
        def _():
            o_ref[...] = du_ref[...].astype(BF16)

    tab = pl.BlockSpec((TL, 128), lambda i, j: (i, 0))
    return pl.pallas_call(
        body, name="rope_bwd", grid=(L // TL, 10),
        in_specs=[pl.BlockSpec((TL, 128), lambda i, j: (i, jnp.minimum(j, 3))), tab, tab,
                  pl.BlockSpec((TL, 128), lambda i, j: (i, jnp.clip(j - 6, 0, 3))), tab, tab, tab],
        out_specs=pl.BlockSpec((TL, 128), lambda i, j: (i, j)),
        out_shape=jax.ShapeDtypeStruct((L, INW), BF16),
        compiler_params=_cp(("parallel", "parallel")),
    )(dq, dk, dv, du, *tabs)


def _attn_window(n):
    start = pl.multiple_of(jnp.clip((n - 1) * BLK, 0, L - WIN), BLK)
    qpos = n * BLK + lax.broadcasted_iota(jnp.int32, (BLK, WIN), 0)
    kpos = start + lax.broadcasted_iota(jnp.int32, (BLK, WIN), 1)
    return start, jnp.abs(kpos - qpos) <= BLK


_NT = (((1,), (1,)), ((), ()))
_TN = (((0,), (0,)), ((), ()))
NEG = -1e30


def _attn_fwd(qkv, sink):
    def body(sink_ref, q_ref, k_ref, v_ref, o_ref, lse_ref):
        n = pl.program_id(0)
        start, valid = _attn_window(n)
        kw = k_ref[pl.ds(start, WIN), :]
        vw = v_ref[pl.ds(start, WIN), :]
        for h in range(NQ):
            kv = h // (NQ // NKV)
            qh = q_ref[:, h * HD:(h + 1) * HD]
            kh = kw[:, kv * HD:(kv + 1) * HD]
            vh = vw[:, kv * HD:(kv + 1) * HD]
            s = lax.dot_general(qh, kh, _NT, preferred_element_type=F32) * (HD ** -0.5)
            s = jnp.where(valid, s, NEG)
            sk = sink_ref[h]
            m = jnp.maximum(jnp.max(s, axis=-1, keepdims=True), sk)
            p = jnp.exp(s - m)
            den = jnp.sum(p, axis=-1, keepdims=True) + jnp.exp(sk - m)
            o_ref[:, h * HD:(h + 1) * HD] = jnp.dot((p / den).astype(BF16), vh, preferred_element_type=F32)
            lse_ref[:, h:h + 1] = m + jnp.log(den)

    return pl.pallas_call(
        body, name="attn_fwd", grid=(L // BLK,),
        in_specs=[pl.BlockSpec(memory_space=pltpu.SMEM),
                  pl.BlockSpec((BLK, AW), lambda n: (n, 0)),
                  pl.BlockSpec((L, KVW), lambda n: (0, AW // KVW)),
                  pl.BlockSpec((L, KVW), lambda n: (0, AW // KVW + 1))],
        out_specs=[pl.BlockSpec((BLK, AW), lambda n: (n, 0)), pl.BlockSpec((BLK, NQ), lambda n: (n, 0))],
        out_shape=[jax.ShapeDtypeStruct((L, AW), F32), jax.ShapeDtypeStruct((L, NQ), F32)],
        compiler_params=_cp(("parallel",)),
    )(sink, qkv, qkv, qkv)


def _attn_bwd(qkv, sink, attn, lse, dattn):
    def body(sink_ref, q_ref, k_ref, v_ref, o_ref, lse_ref, do_ref, dq_ref, dk_ref, dv_ref, dsink_ref):
        n = pl.program_id(0)

        @pl.when(n == 0)
        def _():
            dk_ref[...] = jnp.zeros_like(dk_ref)
            dv_ref[...] = jnp.zeros_like(dv_ref)
            dsink_ref[...] = jnp.zeros_like(dsink_ref)

        start, valid = _attn_window(n)
        kw = k_ref[pl.ds(start, WIN), :]
        vw = v_ref[pl.ds(start, WIN), :]
        for kv in range(NKV):
            kh = kw[:, kv * HD:(kv + 1) * HD]
            vh = vw[:, kv * HD:(kv + 1) * HD]
            dk_acc = jnp.zeros((WIN, HD), F32)
            dv_acc = jnp.zeros((WIN, HD), F32)
            for h in range(kv * (NQ // NKV), (kv + 1) * (NQ // NKV)):
                qh = q_ref[:, h * HD:(h + 1) * HD]
                doh = do_ref[:, h * HD:(h + 1) * HD]
                dd = jnp.sum(doh * o_ref[:, h * HD:(h + 1) * HD], axis=-1, keepdims=True)
                lse_h = lse_ref[:, h:h + 1]
                s = lax.dot_general(qh, kh, _NT, preferred_element_type=F32) * (HD ** -0.5)
                p = jnp.where(valid, jnp.exp(s - lse_h), 0.0)
                dob = doh.astype(BF16)
                dp = lax.dot_general(dob, vh, _NT, preferred_element_type=F32)
                ds = (p * (dp - dd) * (HD ** -0.5)).astype(BF16)
                dq_ref[:, h * HD:(h + 1) * HD] = jnp.dot(ds, kh, preferred_element_type=F32)
                dk_acc += lax.dot_general(ds, qh, _TN, preferred_element_type=F32)
                dv_acc += lax.dot_general(p.astype(BF16), dob, _TN, preferred_element_type=F32)
                psink = jnp.exp(sink_ref[h] - lse_h)
                dsk = -jnp.sum(psink * dd, axis=0, keepdims=True)
                dsink_ref[h:h + 1, :] += jnp.broadcast_to(dsk, (1, 128))
            dk_ref[pl.ds(start, WIN), kv * HD:(kv + 1) * HD] += dk_acc
            dv_ref[pl.ds(start, WIN), kv * HD:(kv + 1) * HD] += dv_acc

    qblk = pl.BlockSpec((BLK, AW), lambda n: (n, 0))
    full = pl.BlockSpec((L, KVW), lambda n: (0, 0))
    return pl.pallas_call(
        body, name="attn_bwd", grid=(L // BLK,),
        in_specs=[pl.BlockSpec(memory_space=pltpu.SMEM), qblk,
                  pl.BlockSpec((L, KVW), lambda n: (0, AW // KVW)),
                  pl.BlockSpec((L, KVW), lambda n: (0, AW // KVW + 1)),
                  qblk, pl.BlockSpec((BLK, NQ), lambda n: (n, 0)), qblk],
        out_specs=[qblk, full, full, pl.BlockSpec((NQ, 128), lambda n: (0, 0))],
        out_shape=[jax.ShapeDtypeStruct((L, AW), F32), jax.ShapeDtypeStruct((L, KVW), F32),
                   jax.ShapeDtypeStruct((L, KVW), F32), jax.ShapeDtypeStruct((NQ, 128), F32)],
        compiler_params=_cp(("arbitrary",)),
    )(sink, qkv, qkv, qkv, attn, lse, dattn)


def _perm(a):
    return a.reshape(NSEG, TSEG, a.shape[1]).transpose(1, 0, 2).reshape(L, a.shape[1])


def _unperm(a):
    return a.reshape(TSEG, NSEG, a.shape[1]).transpose(1, 0, 2).reshape(L, a.shape[1])


def _cmul(ar, ai, br, bi):
    return ar * br - ai * bi, ar * bi + ai * br


def _scan_inplace(re_ref, im_ref, lr, li, rev):
    n = lr.shape[1]
    lr8 = jnp.broadcast_to(lr, (NSEG, n))
    li8 = jnp.broadcast_to(li, (NSEG, n))

    def rows(k):
        return pl.ds(pl.multiple_of(jnp.where(rev, TSEG - 1 - k, k) * NSEG, NSEG), NSEG)

    def step(k, c, store):
        sr, si = c
        rs = rows(k)
        pr, pi = _cmul(lr8, li8, sr, si)
        nr = pr + re_ref[rs, :]
        ni = pi + im_ref[rs, :]
        if store:
            re_ref[rs, :] = nr
            im_ref[rs, :] = ni
        return nr, ni

    z = jnp.zeros((NSEG, n), F32)
    er, ei = lax.fori_loop(0, TSEG, functools.partial(step, store=False), (z, z))
    pr, pi = lr, li
    for _ in range(int(math.log2(TSEG))):
        pr, pi = _cmul(pr, pi, pr, pi)

    seg = lax.broadcasted_iota(jnp.int32, (NSEG, n), 0)

    def chain(order):
        cr = jnp.zeros((1, n), F32)
        ci = jnp.zeros((1, n), F32)
        outr = jnp.zeros((NSEG, n), F32)
        outi = jnp.zeros((NSEG, n), F32)
        for s in order:
            outr = jnp.where(seg == s, cr, outr)
            outi = jnp.where(seg == s, ci, outi)
            mr, mi = _cmul(pr, pi, cr, ci)
            cr, ci = mr + er[s:s + 1], mi + ei[s:s + 1]
        return outr, outi

    fr, fi = chain(range(NSEG))
    rr, ri = chain(range(NSEG - 1, -1, -1))
    cin_r = jnp.where(rev, rr, fr)
    cin_i = jnp.where(rev, ri, fi)
    lax.fori_loop(0, TSEG, functools.partial(step, store=True), (cin_r, cin_i))
    return cin_r, cin_i


S5_RC = 512


def _s5_specs():
    u_spec = pl.BlockSpec((L, 128), lambda cb, h, d: (0, cb))
    b_spec = pl.BlockSpec((None, None, 128, SBW), lambda cb, h, d: (d, cb * 2 + h, 0, 0))
    c_spec = pl.BlockSpec((None, None, SBW, 128), lambda cb, h, d: (d, cb * 2 + h, 0, 0))
    l_spec = pl.BlockSpec((None, None, 1, SBW), lambda cb, h, d: (d, cb * 2 + h, 0, 0))
    d_spec = pl.BlockSpec((1, 128), lambda cb, h, d: (0, cb))
    return u_spec, b_spec, c_spec, l_spec, d_spec


def _s5_input_states(u_ref, bre_ref, bim_ref, lr_ref, li_ref, sre, sim, rev):
    def proj(i, _):
        rs = pl.ds(pl.multiple_of(i * S5_RC, S5_RC), S5_RC)
        ub = u_ref[rs, :].astype(BF16)
        sre[rs, :] = jnp.dot(ub, bre_ref[...], preferred_element_type=F32)
        sim[rs, :] = jnp.dot(ub, bim_ref[...], preferred_element_type=F32)
        return 0

    lax.fori_loop(0, L // S5_RC, proj, 0)
    return _scan_inplace(sre, sim, lr_ref[...], li_ref[...], rev)


def _s5_fwd(u_p, bbd_re, bbd_im, cbd_re, cbd_im, lam_re, lam_im, dskip):
    def body(u_ref, bre_ref, bim_ref, cre_ref, cim_ref, lr_ref, li_ref, d_ref, y_ref, sre, sim):
        first = (pl.program_id(1) == 0) & (pl.program_id(2) == 0)
        _s5_input_states(u_ref, bre_ref, bim_ref, lr_ref, li_ref, sre, sim, pl.program_id(2) == 1)

        def out(i, _):
            rs = pl.ds(pl.multiple_of(i * S5_RC, S5_RC), S5_RC)
            yv = (jnp.dot(sre[rs, :].astype(BF16), cre_ref[...], preferred_element_type=F32)
                  - jnp.dot(sim[rs, :].astype(BF16), cim_ref[...], preferred_element_type=F32))

            @pl.when(first)
            def _():
                y_ref[rs, :] = d_ref[...] * u_ref[rs, :] + yv

            @pl.when(jnp.logical_not(first))
            def _():
                y_ref[rs, :] += yv

            return 0

        lax.fori_loop(0, L // S5_RC, out, 0)

    u_spec, b_spec, c_spec, l_spec, d_spec = _s5_specs()
    return pl.pallas_call(
        body, name="s5_fwd", grid=(SW // 128, 2, 2),
        in_specs=[u_spec, b_spec, b_spec, c_spec, c_spec, l_spec, l_spec, d_spec],
        out_specs=u_spec, out_shape=jax.ShapeDtypeStruct((L, SW), F32),
        scratch_shapes=[pltpu.VMEM((L, SBW), F32), pltpu.VMEM((L, SBW), F32)],
        compiler_params=_cp(("parallel", "arbitrary", "arbitrary")),
    )(u_p, bbd_re, bbd_im, cbd_re, cbd_im, lam_re, lam_im, dskip)


def _s5_bwd(u_p, dy_p, bbd_re, bbd_im, cbd_re, cbd_im, lam_re, lam_im, dskip):
    def body(u_ref, dy_ref, bre_ref, bim_ref, cre_ref, cim_ref, lr_ref, li_ref, d_ref,
             du_ref, dbre_ref, dbim_ref, dcre_ref, dcim_ref, dlr_ref, dli_ref, dd_ref, sre, sim, gre, gim):
        first = (pl.program_id(1) == 0) & (pl.program_id(2) == 0)
        rev = pl.program_id(2) == 1
        cin_r, cin_i = _s5_input_states(u_ref, bre_ref, bim_ref, lr_ref, li_ref, sre, sim, rev)

        def dstate(i, _):
            rs = pl.ds(pl.multiple_of(i * S5_RC, S5_RC), S5_RC)
            dyb = dy_ref[rs, :].astype(BF16)
            gre[rs, :] = lax.dot_general(dyb, cre_ref[...], _NT, preferred_element_type=F32)
            gim[rs, :] = -lax.dot_general(dyb, cim_ref[...], _NT, preferred_element_type=F32)
            return 0

        lax.fori_loop(0, L // S5_RC, dstate, 0)
        _scan_inplace(gre, gim, lr_ref[...], -li_ref[...], jnp.logical_not(rev))

        def dlam(k, c):
            ar, ai = c
            tg = jnp.where(rev, k, k + 1)
            ts = jnp.where(rev, k + 1, k)
            rg = pl.ds(pl.multiple_of(tg * NSEG, NSEG), NSEG)
            rs = pl.ds(pl.multiple_of(ts * NSEG, NSEG), NSEG)
            gr, gi, sr, si = gre[rg, :], gim[rg, :], sre[rs, :], sim[rs, :]
            return ar + gr * sr + gi * si, ai + gi * sr - gr * si

        z = jnp.zeros((NSEG, SBW), F32)
        ar, ai = lax.fori_loop(0, TSEG - 1, dlam, (z, z))
        rb = pl.ds(pl.multiple_of(jnp.where(rev, TSEG - 1, 0) * NSEG, NSEG), NSEG)
        gr, gi = gre[rb, :], gim[rb, :]
        ar = ar + gr * cin_r + gi * cin_i
        ai = ai + gi * cin_r - gr * cin_i
        dlr_ref[...] = jnp.sum(ar, axis=0, keepdims=True)
        dli_ref[...] = jnp.sum(ai, axis=0, keepdims=True)

        dbre_ref[...] = jnp.zeros_like(dbre_ref)
        dbim_ref[...] = jnp.zeros_like(dbim_ref)
        dcre_ref[...] = jnp.zeros_like(dcre_ref)
        dcim_ref[...] = jnp.zeros_like(dcim_ref)

        @pl.when(first)
        def _():
            dd_ref[...] = jnp.zeros_like(dd_ref)

        def grads(i, _):
            rs = pl.ds(pl.multiple_of(i * S5_RC, S5_RC), S5_RC)
            uv = u_ref[rs, :]
            dyv = dy_ref[rs, :]
            ub, dyb = uv.astype(BF16), dyv.astype(BF16)
            grb, gib = gre[rs, :].astype(BF16), gim[rs, :].astype(BF16)
            srb, sib = sre[rs, :].astype(BF16), sim[rs, :].astype(BF16)
            dbre_ref[...] += lax.dot_general(ub, grb, _TN, preferred_element_type=F32)
            dbim_ref[...] += lax.dot_general(ub, gib, _TN, preferred_element_type=F32)
            dcre_ref[...] += lax.dot_general(srb, dyb, _TN, preferred_element_type=F32)
            dcim_ref[...] -= lax.dot_general(sib, dyb, _TN, preferred_element_type=F32)
            duv = (lax.dot_general(grb, bre_ref[...], _NT, preferred_element_type=F32)
                   + lax.dot_general(gib, bim_ref[...], _NT, preferred_element_type=F32))

            @pl.when(first)
            def _():
                du_ref[rs, :] = d_ref[...] * dyv + duv
                dd_ref[...] += jnp.sum(dyv * uv, axis=0, keepdims=True)

            @pl.when(jnp.logical_not(first))
            def _():
                du_ref[rs, :] += duv

            return 0

        lax.fori_loop(0, L // S5_RC, grads, 0)

    u_spec, b_spec, c_spec, l_spec, d_spec = _s5_specs()
    scr = pltpu.VMEM((L, SBW), F32)
    return pl.pallas_call(
        body, name="s5_bwd", grid=(SW // 128, 2, 2),
        in_specs=[u_spec, u_spec, b_spec, b_spec, c_spec, c_spec, l_spec, l_spec, d_spec],
        out_specs=[u_spec, b_spec, b_spec, c_spec, c_spec, l_spec, l_spec, d_spec],
        out_shape=[jax.ShapeDtypeStruct((L, SW), F32),
                   jax.ShapeDtypeStruct((2, NSB, 128, SBW), F32), jax.ShapeDtypeStruct((2, NSB, 128, SBW), F32),
                   jax.ShapeDtypeStruct((2, NSB, SBW, 128), F32), jax.ShapeDtypeStruct((2, NSB, SBW, 128), F32),
                   jax.ShapeDtypeStruct((2, NSB, 1, SBW), F32), jax.ShapeDtypeStruct((2, NSB, 1, SBW), F32),
                   jax.ShapeDtypeStruct((1, SW), F32)],
        scratch_shapes=[scr, scr, scr, scr],
        compiler_params=_cp(("parallel", "arbitrary", "arbitrary")),
    )(u_p, dy_p, bbd_re, bbd_im, cbd_re, cbd_im, lam_re, lam_im, dskip)


def _s5_params(a_re, a_im, log_step, b_re, b_im):
    lam = lax.complex(a_re, a_im)
    step = jnp.exp(log_step)[..., None]
    lam_bar = jnp.exp(lam * step)
    b_bar = ((lam_bar - 1.0) / lam)[..., None] * lax.complex(b_re, b_im)
    return jnp.real(lam_bar), jnp.imag(lam_bar), jnp.real(b_bar), jnp.imag(b_bar)


def _sel():
    i = jnp.arange(8)[None, :, None]
    j = jnp.arange(4)[None, None, :]
    r = jnp.arange(2)[:, None, None]
    return (i == r * 4 + j).astype(F32)


def _to_bbd(bb):
    return jnp.einsum('dkrjpc,rij->dkricjp', bb.reshape(2, 4, 2, 4, NP, GC), _sel()).reshape(2, NSB, 128, SBW)


def _from_bbd(dbbd):
    return jnp.einsum('dkricjp,rij->dkrjpc', dbbd.reshape(2, 4, 2, 8, GC, 4, NP), _sel()).reshape(2, NG, NP, GC)


def _to_cbd(cc):
    return jnp.einsum('dkrjcp,rij->dkrjpic', cc.reshape(2, 4, 2, 4, GC, NP), _sel()).reshape(2, NSB, SBW, 128)


def _from_cbd(dcbd):
    return jnp.einsum('dkrjpic,rij->dkrjcp', dcbd.reshape(2, 4, 2, 4, NP, 8, GC), _sel()).reshape(2, NG, GC, NP)


def _gelu(y):
    return 0.5 * y * (1.0 + lax.erf(y * (2.0 ** -0.5)))


def _gelu_grad(y):
    return 0.5 * (1.0 + lax.erf(y * (2.0 ** -0.5))) + y * jnp.exp(-0.5 * y * y) * ((2.0 * math.pi) ** -0.5)


def _sigmoid(z):
    return 1.0 / (1.0 + jnp.exp(-z))


def _glu_fwd(y, wg):
    def body(y_ref, w_ref, o_ref, z_ref):
        ys = _gelu(y_ref[...])
        z = jnp.dot(ys.astype(BF16), w_ref[...], preferred_element_type=F32)
        z_ref[...] = z
        o_ref[...] = ys * _sigmoid(z)

    row = pl.BlockSpec((TL, SW), lambda i: (i, 0))
    return pl.pallas_call(
        body, name="glu_fwd", grid=(L // TL,),
        in_specs=[row, pl.BlockSpec((SW, SW), lambda i: (0, 0))], out_specs=[row, row],
        out_shape=[jax.ShapeDtypeStruct((L, SW), F32), jax.ShapeDtypeStruct((L, SW), F32)],
        compiler_params=_cp(("parallel",)),
    )(y, wg)


def _glu_bwd(y, z, dout, wg):
    def body(y_ref, z_ref, do_ref, w_ref, dy_ref, dw_ref):
        @pl.when(pl.program_id(0) == 0)
        def _():
            dw_ref[...] = jnp.zeros_like(dw_ref)

        yv = y_ref[...]
        ys = _gelu(yv)
        sg = _sigmoid(z_ref[...])
        dov = do_ref[...]
        dz = (dov * ys * sg * (1.0 - sg)).astype(BF16)
        dys = dov * sg + lax.dot_general(dz, w_ref[...], _NT, preferred_element_type=F32)
        dy_ref[...] = dys * _gelu_grad(yv)
        dw_ref[...] += lax.dot_general(ys.astype(BF16), dz, _TN, preferred_element_type=F32)

    row = pl.BlockSpec((TL, SW), lambda i: (i, 0))
    wsp = pl.BlockSpec((SW, SW), lambda i: (0, 0))
    return pl.pallas_call(
        body, name="glu_bwd", grid=(L // TL,),
        in_specs=[row, row, row, wsp], out_specs=[row, wsp],
        out_shape=[jax.ShapeDtypeStruct((L, SW), F32), jax.ShapeDtypeStruct((SW, SW), F32)],
        compiler_params=_cp(("arbitrary",)),
    )(y, z, dout, wg)


CT = 256
CR = 128
NCT = DFF // CT


def _shifted(ref, r):
    cur = ref[pl.ds(r, CR), :]
    before = ref[pl.ds(pl.multiple_of(jnp.maximum(r - 8, 0), 8), 8), :][7:8, :]
    after = ref[pl.ds(pl.multiple_of(jnp.minimum(r + CR, L - 8), 8), 8), :][0:1, :]
    before = jnp.where(r > 0, before, 0.0)
    after = jnp.where(r + CR < L, after, 0.0)
    row = lax.broadcasted_iota(jnp.int32, cur.shape, 0)
    prev = jnp.where(row == 0, before, pltpu.roll(cur, 1, 0))
    nxt = jnp.where(row == CR - 1, after, pltpu.roll(cur, CR - 1, 0))
    return prev, cur, nxt


def _conv3(ref, r, w_ref, b_ref):
    prev, cur, nxt = _shifted(ref, r)
    return w_ref[0:1, :] * prev + w_ref[1:2, :] * cur + w_ref[2:3, :] * nxt + b_ref[...]


def _convact_fwd(up, conv_w, conv_b):
    def body(ug_ref, uv_ref, wg_ref, wv_ref, bg_ref, bv_ref, o_ref):
        def chunk(i, _):
            r = pl.multiple_of(i * CR, CR)
            g = _conv3(ug_ref, r, wg_ref, bg_ref)
            v = _conv3(uv_ref, r, wv_ref, bv_ref)
            o_ref[pl.ds(r, CR), :] = (g * _sigmoid(g) * v).astype(BF16)
            return 0

        lax.fori_loop(0, L // CR, chunk, 0)

    gcol = pl.BlockSpec((L, CT), lambda j: (0, j))
    vcol = pl.BlockSpec((L, CT), lambda j: (0, j + NCT))
    return pl.pallas_call(
        body, name="convact_fwd", grid=(NCT,),
        in_specs=[gcol, vcol,
                  pl.BlockSpec((3, CT), lambda j: (0, j)), pl.BlockSpec((3, CT), lambda j: (0, j + NCT)),
                  pl.BlockSpec((1, CT), lambda j: (0, j)), pl.BlockSpec((1, CT), lambda j: (0, j + NCT))],
        out_specs=gcol, out_shape=jax.ShapeDtypeStruct((L, DFF), BF16),
        compiler_params=_cp(("parallel",)),
    )(up, up, conv_w, conv_w, conv_b, conv_b)


def _convact_bwd(up, dact, conv_w, conv_b):
    def body(ug_ref, uv_ref, da_ref, wg_ref, wv_ref, bg_ref, bv_ref,
             dug_ref, duv_ref, dwg_ref, dwv_ref, dbg_ref, dbv_ref, dgs, dvs):
        dwg_ref[...] = jnp.zeros_like(dwg_ref)
        dwv_ref[...] = jnp.zeros_like(dwv_ref)
        dbg_ref[...] = jnp.zeros_like(dbg_ref)
        dbv_ref[...] = jnp.zeros_like(dbv_ref)

        def chunk1(i, _):
            r = pl.multiple_of(i * CR, CR)
            rs = pl.ds(r, CR)
            pg, cg, ng = _shifted(ug_ref, r)
            pv, cv, nv = _shifted(uv_ref, r)
            g = wg_ref[0:1, :] * pg + wg_ref[1:2, :] * cg + wg_ref[2:3, :] * ng + bg_ref[...]
            v = wv_ref[0:1, :] * pv + wv_ref[1:2, :] * cv + wv_ref[2:3, :] * nv + bv_ref[...]
            sg = _sigmoid(g)
            da = da_ref[rs, :]
            dv = da * g * sg
            dg = da * v * sg * (1.0 + g * (1.0 - sg))
            dgs[rs, :] = dg
            dvs[rs, :] = dv
            for dref, dval, taps, bref in ((dwg_ref, dg, (pg, cg, ng), dbg_ref), (dwv_ref, dv, (pv, cv, nv), dbv_ref)):
                for k in range(3):
                    dref[k:k + 1, :] += jnp.sum(dval * taps[k], axis=0, keepdims=True)
                bref[...] += jnp.sum(dval, axis=0, keepdims=True)
            return 0

        lax.fori_loop(0, L // CR, chunk1, 0)

        def chunk2(i, _):
            r = pl.multiple_of(i * CR, CR)
            for src, w_ref, dst in ((dgs, wg_ref, dug_ref), (dvs, wv_ref, duv_ref)):
                prev, cur, nxt = _shifted(src, r)
                dst[pl.ds(r, CR), :] = (w_ref[0:1, :] * nxt + w_ref[1:2, :] * cur + w_ref[2:3, :] * prev).astype(BF16)
            return 0

        lax.fori_loop(0, L // CR, chunk2, 0)

    gcol = pl.BlockSpec((L, CT), lambda j: (0, j))
    vcol = pl.BlockSpec((L, CT), lambda j: (0, j + NCT))
    wg = pl.BlockSpec((3, CT), lambda j: (0, j))
    wv = pl.BlockSpec((3, CT), lambda j: (0, j + NCT))
    bg = pl.BlockSpec((1, CT), lambda j: (0, j))
    bv = pl.BlockSpec((1, CT), lambda j: (0, j + NCT))
    outs = pl.pallas_call(
        body, name="convact_bwd", grid=(NCT,),
        in_specs=[gcol, vcol, gcol, wg, wv, bg, bv],
        out_specs=[gcol, gcol, wg, wg, bg, bg],
        out_shape=[jax.ShapeDtypeStruct((L, DFF), BF16), jax.ShapeDtypeStruct((L, DFF), BF16),
                   jax.ShapeDtypeStruct((3, DFF), F32), jax.ShapeDtypeStruct((3, DFF), F32),
                   jax.ShapeDtypeStruct((1, DFF), F32), jax.ShapeDtypeStruct((1, DFF), F32)],
        scratch_shapes=[pltpu.VMEM((L, CT), F32), pltpu.VMEM((L, CT), F32)],
        compiler_params=_cp(("parallel",)),
    )(up, up, dact, conv_w, conv_w, conv_b, conv_b)
    dug, duv, dwg_, dwv_, dbg_, dbv_ = outs
    return (jnp.concatenate([dug, duv], axis=1), jnp.concatenate([dwg_, dwv_], axis=1),
            jnp.concatenate([dbg_, dbv_], axis=1))


def _local_step(x, tgt, w_in_t, w_glu, w_out, w_up_t, conv_w, w_down, p):
    tabs = _rope_tables()
    lam_re, lam_im, bb_re, bb_im = _s5_params(p["a_re"], p["a_im"], p["log_step"], p["b_re"], p["b_im"])
    bbd_re, bbd_im = _to_bbd(bb_re).astype(BF16), _to_bbd(bb_im).astype(BF16)
    cbd_re, cbd_im = _to_cbd(p["c_re"]).astype(BF16), _to_cbd(p["c_im"]).astype(BF16)
    lam_re4, lam_im4 = lam_re.reshape(2, NSB, 1, SBW), lam_im.reshape(2, NSB, 1, SBW)
    dskip = p["d_skip"].reshape(1, SW)
    g_mix, g_ffn, g_fin = p["norm_mix_g"].reshape(1, D), p["norm_ffn_g"].reshape(1, D), p["norm_final_g"].reshape(1, D)
    g_attn, g_ssm = p["norm_attn_g"].reshape(1, AW), p["norm_ssm_g"].reshape(1, SW)
    sink = p["sink"].reshape(NQ)
    conv_b = p["conv_b"].reshape(1, 2 * DFF)

    h1 = _rms_fwd(x, g_mix, "norm_mix_fwd")
    proj = _mm(h1, w_in_t, tb=True, name="in_proj", tn=640)
    qkv = _rope_fwd(proj, tabs)
    attn, lse = _attn_fwd(qkv, sink)
    u_p = _perm(proj[:, AW + 2 * KVW:])
    y_p = _s5_fwd(u_p, bbd_re, bbd_im, cbd_re, cbd_im, lam_re4, lam_im4, dskip)
    ysg_p, z_p = _glu_fwd(y_p, w_glu)
    ysg = _unperm(ysg_p)
    mixed = _mixnorm_fwd(attn, ysg, g_attn, g_ssm)
    x1 = _mm(mixed, w_out, add=x, name="out_proj")
    h2 = _rms_fwd(x1, g_ffn, "norm_ffn_fwd")
    up = _mm(h2, w_up_t, tb=True, name="ffn_up")
    act = _convact_fwd(up, conv_w, conv_b)
    x2 = _mm(act, w_down, add=x1, name="ffn_down", tk=704)
    loss, dx2, dg_fin = _final_loss(x2, tgt, g_fin)

    dact = _mm(dx2, w_down, tb=True, name="ffn_down_dx", tn=704)
    dw_down = _mm(act, dx2, ta=True, name="ffn_down_dw", tm=704)
    dup, dconv_w, dconv_b = _convact_bwd(up, dact, conv_w, conv_b)
    dw_up_t = _mm(dup, h2, ta=True, name="ffn_up_dw")
    dh2 = _mm(dup, w_up_t, name="ffn_up_dx")
    dx1, dg_ffn = _rms_bwd(x1, g_ffn, dh2, dx2, "norm_ffn_bwd")
    dmixed = _mm(dx1, w_out, tb=True, name="out_proj_dx")
    dw_out = _mm(mixed, dx1, ta=True, name="out_proj_dw")
    dattn, dysg, dg_attn, dg_ssm = _mixnorm_bwd(attn, ysg, g_attn, g_ssm, dmixed)
    dy_p, dw_glu = _glu_bwd(y_p, z_p, _perm(dysg), w_glu)
    du_p, dbbd_re, dbbd_im, dcbd_re, dcbd_im, dlam_re, dlam_im, dd = _s5_bwd(
        u_p, dy_p, bbd_re, bbd_im, cbd_re, cbd_im, lam_re4, lam_im4, dskip)
    dq, dk, dv, dsink = _attn_bwd(qkv, sink, attn, lse, dattn)
    dproj = _rope_bwd(dq, dk, dv, _unperm(du_p), tabs)
    dw_in_t = _mm(dproj, h1, ta=True, name="in_proj_dw", tm=640)
    dh1 = _mm(dproj, w_in_t, name="in_proj_dx", tk=640)
    grad_x, dg_mix = _rms_bwd(x, g_mix, dh1, dx1, "norm_mix_bwd")

    big = dict(w_in_t=dw_in_t, w_glu=dw_glu, w_out=dw_out, w_up_t=dw_up_t, w_down=dw_down)
    small = dict(norm_mix_g=dg_mix, norm_attn_g=dg_attn, norm_ssm_g=dg_ssm, norm_ffn_g=dg_ffn, norm_final_g=dg_fin,
                 sink=dsink[:, 0], conv_b=dconv_b, d_skip=dd, conv_w=dconv_w,
                 lam_re=dlam_re.reshape(2, NG, NP), lam_im=dlam_im.reshape(2, NG, NP),
                 bb_re=_from_bbd(dbbd_re), bb_im=_from_bbd(dbbd_im),
                 c_re=_from_cbd(dcbd_re), c_im=_from_cbd(dcbd_im))
    return loss[0, 0], grad_x, big, small


ANY = pl.BlockSpec(memory_space=pl.ANY)


def _coords():
    return lax.axis_index("x"), lax.axis_index("y"), lax.axis_index("c")


def _flip(v, b):
    return v + b - 2 * v * b if b else v


def _all_gather(shards, name):
    n = len(shards)

    def body(*refs):
        ins, outs = refs[:n], refs[n:2 * n]
        send_sems, recv_sems, local_sems = refs[2 * n:]
        x, y, c = _coords()
        me, sibling = (x, y, c), (x, y, 1 - c)
        chips = [(1 - x, y), (x, 1 - y), (1 - x, 1 - y)]
        waits = []
        for a in range(n):
            r = ins[a].shape[0]

            def rows(px, py, pc, a=a, r=r):
                return outs[a].at[pl.ds(pl.multiple_of((4 * px + 2 * py + pc) * r, 8), r), :]

            def copy(k, block, to, src=None, a=a, rows=rows):
                return pltpu.make_async_remote_copy(
                    src_ref=rows(*block) if src is None else src, dst_ref=rows(*block),
                    send_sem=send_sems.at[a * 7 + k], recv_sem=recv_sems.at[a * 7 + k],
                    device_id=to, device_id_type=pl.DeviceIdType.MESH)

            mine = pltpu.make_async_copy(ins[a], rows(*me), local_sems.at[a])
            mine.start()
            first = [copy(0, me, sibling, src=ins[a])]
            first += [copy(1 + j, me, (*chip, c), src=ins[a]) for j, chip in enumerate(chips)]
            for cp in first:
                cp.start()
            waits.append((mine, first, copy))
        for a in range(n):
            mine, first, copy = waits[a]
            passed = [copy(4 + j, (*chip, c), sibling) for j, chip in enumerate(chips)]
            for j, chip in enumerate(chips):
                copy(1 + j, (*chip, c), me).wait_recv()
                passed[j].start()
            waits[a] = (mine, first, copy, passed)
        for a in range(n):
            mine, first, copy, passed = waits[a]
            copy(0, sibling, me).wait_recv()
            for j, chip in enumerate(chips):
                copy(4 + j, (*chip, 1 - c), me).wait_recv()
            for cp in first + passed:
                cp.wait_send()
            mine.wait()

    return pl.pallas_call(
        body, name=name,
        in_specs=[ANY] * n, out_specs=[ANY] * n,
        out_shape=[jax.ShapeDtypeStruct((NDEV * s.shape[0], s.shape[1]), s.dtype) for s in shards],
        scratch_shapes=[pltpu.SemaphoreType.DMA((7 * n,)), pltpu.SemaphoreType.DMA((7 * n,)),
                        pltpu.SemaphoreType.DMA((n,))],
    )(*shards)


def _exchange(parts, name):
    n = len(parts)

    def body(*refs):
        ins, outs = refs[:n], refs[n:2 * n]
        send_sems, recv_sems, local_sems = refs[2 * n:]
        x, y, c = _coords()
        me = 4 * x + 2 * y + c
        copies, locs = [], []
        for a in range(n):
            lc = pltpu.make_async_copy(ins[a].at[me], outs[a].at[me], local_sems.at[a])
            lc.start()
            locs.append(lc)
            for k in range(1, NDEV):
                px, py, pc = _flip(x, (k >> 2) & 1), _flip(y, (k >> 1) & 1), _flip(c, k & 1)
                cp = pltpu.make_async_remote_copy(
                    src_ref=ins[a].at[4 * px + 2 * py + pc], dst_ref=outs[a].at[me],
                    send_sem=send_sems.at[a * 7 + k - 1], recv_sem=recv_sems.at[a * 7 + k - 1],
                    device_id=(px, py, pc), device_id_type=pl.DeviceIdType.MESH)
                cp.start()
                copies.append(cp)
        for cp in copies:
            cp.wait_recv()
        for cp in copies:
            cp.wait_send()
        for lc in locs:
            lc.wait()

    return pl.pallas_call(
        body, name=name,
        in_specs=[ANY] * n, out_specs=[ANY] * n,
        out_shape=[jax.ShapeDtypeStruct(s.shape, s.dtype) for s in parts],
        scratch_shapes=[pltpu.SemaphoreType.DMA((7 * n,)), pltpu.SemaphoreType.DMA((7 * n,)),
                        pltpu.SemaphoreType.DMA((n,))],
    )(*parts)


def _sum_slots(slots, name):
    _, r, c = slots.shape
    tr = _pick(r, 256, 8)

    def body(s_ref, o_ref):
        acc = s_ref[0].astype(F32)
        for q in range(1, NDEV):
            acc = acc + s_ref[q].astype(F32)
        o_ref[...] = acc

    return pl.pallas_call(
        body, name=name, grid=(r // tr,),
        in_specs=[pl.BlockSpec((NDEV, tr, c), lambda i: (0, i, 0))],
        out_specs=pl.BlockSpec((tr, c), lambda i: (i, 0)),
        out_shape=jax.ShapeDtypeStruct((r, c), F32),
        compiler_params=_cp(("parallel",)),
    )(slots)


def _adamw(w, g, m, v, name):
    r, c = w.shape
    tr = _pick(r, 256, 8)

    def body(w_ref, g_ref, m_ref, v_ref, d_ref, nm_ref, nv_ref):
        gv = g_ref[...]
        nm = B1 * m_ref[...] + (1.0 - B1) * gv
        nv = B2 * v_ref[...] + (1.0 - B2) * (gv * gv)
        nm_ref[...] = nm
        nv_ref[...] = nv
        d_ref[...] = -LR * ((nm / C1) / (jnp.sqrt(nv / C2) + AEPS) + WD * w_ref[...])

    blk = pl.BlockSpec((tr, c), lambda i: (i, 0))
    return pl.pallas_call(
        body, name=name, grid=(r // tr,),
        in_specs=[blk] * 4, out_specs=[blk] * 3,
        out_shape=[jax.ShapeDtypeStruct((r, c), F32)] * 3,
        compiler_params=_cp(("parallel",)),
    )(w, g, m, v)


SMALL_ORDER = ["norm_mix_g", "a_re", "a_im", "log_step", "b_re", "b_im", "c_re", "c_im", "d_skip", "sink",
               "norm_attn_g", "norm_ssm_g", "norm_ffn_g", "conv_b", "norm_final_g"]
PACK_W = 1024


def _pack(arrs, rows):
    flat = jnp.concatenate([a.reshape(-1).astype(F32) for a in arrs])
    return jnp.pad(flat, (0, rows * PACK_W - flat.shape[0])).reshape(rows, PACK_W)


def _unpack(packed, shapes):
    flat = packed.reshape(-1)
    out, off = [], 0
    for s in shapes:
        size = math.prod(s)
        out.append(flat[off:off + size].reshape(s))
        off += size
    return out


def kernel(x, norm_mix_g, w_in, a_re, a_im, log_step, b_re, b_im, c_re, c_im, d_skip, w_glu, sink, norm_attn_g, norm_ssm_g, w_out, norm_ffn_g, w_up, conv_w, conv_b, w_down, norm_final_g, loss_target, m_norm_mix_g, m_w_in, m_a_re, m_a_im, m_log_step, m_b_re, m_b_im, m_c_re, m_c_im, m_d_skip, m_w_glu, m_sink, m_norm_attn_g, m_norm_ssm_g, m_w_out, m_norm_ffn_g, m_w_up, m_conv_w, m_conv_b, m_w_down, m_norm_final_g, v_norm_mix_g, v_w_in, v_a_re, v_a_im, v_log_step, v_b_re, v_b_im, v_c_re, v_c_im, v_d_skip, v_w_glu, v_sink, v_norm_attn_g, v_norm_ssm_g, v_w_out, v_norm_ffn_g, v_w_up, v_conv_w, v_conv_b, v_w_down, v_norm_final_g):
    args = dict(locals())
    names = ["norm_mix_g", "w_in", "a_re", "a_im", "log_step", "b_re", "b_im", "c_re", "c_im", "d_skip", "w_glu",
             "sink", "norm_attn_g", "norm_ssm_g", "w_out", "norm_ffn_g", "w_up", "conv_w", "conv_b", "w_down",
             "norm_final_g"]
    w = {k: args[k] for k in names}
    m = {k: args["m_" + k] for k in names}
    v = {k: args["v_" + k] for k in names}

    shards = [w_in[0].T.astype(BF16), w_glu[0].astype(BF16), w_out[0].astype(BF16), w_up[0].T.astype(BF16),
              w_down[0].astype(BF16), jnp.pad(conv_w[0], ((0, 5), (0, 0)))]
    w_in_t, w_glu_f, w_out_f, w_up_t, w_down_f, conv_w_g = _all_gather(shards, "gather_weights")
    conv_w_f = conv_w_g.reshape(NDEV, 8, 2 * DFF // NDEV)[:, :3].transpose(1, 0, 2).reshape(3, 2 * DFF)

    p = {k: w[k][0] for k in SMALL_ORDER if k != "norm_final_g"}
    p["norm_final_g"] = norm_final_g
    loss, grad_x, big, small = _local_step(x[0], loss_target[0], w_in_t, w_glu_f, w_out_f, w_up_t, conv_w_f,
                                           w_down_f, p)

    small_names = list(small.keys())
    small_shapes = [small[k].shape for k in small_names]
    n_small = sum(math.prod(s) for s in small_shapes)
    rows_dev = -(-n_small // (PACK_W * NDEV * 8)) * 8
    spack = _pack([small[k] for k in small_names], rows_dev * NDEV)
    big_names = ["w_in_t", "w_glu", "w_out", "w_up_t", "w_down"]
    parts = [big[k].reshape(NDEV, big[k].shape[0] // NDEV, big[k].shape[1]) for k in big_names]
    parts.append(spack.reshape(NDEV, rows_dev, PACK_W))
    slots = _exchange(parts, "exchange_grads")
    red = {k: _sum_slots(s, "sum_" + k) for k, s in zip(big_names + ["small"], slots)}
    (small_full,) = _all_gather([red["small"]], "gather_small")
    sm = dict(zip(small_names, _unpack(small_full, small_shapes)))

    me = 4 * lax.axis_index("x") + 2 * lax.axis_index("y") + lax.axis_index("c")
    _, s5_vjp = jax.vjp(_s5_params, a_re[0], a_im[0], log_step[0], b_re[0], b_im[0])
    da_re, da_im, dlog_step, db_re, db_im = s5_vjp((sm["lam_re"], sm["lam_im"], sm["bb_re"], sm["bb_im"]))
    grads = {
        "norm_mix_g": sm["norm_mix_g"], "a_re": da_re[None], "a_im": da_im[None], "log_step": dlog_step[None],
        "b_re": db_re[None], "b_im": db_im[None], "c_re": sm["c_re"][None], "c_im": sm["c_im"][None],
        "d_skip": sm["d_skip"].reshape(1, NG, GC), "sink": sm["sink"][None], "norm_attn_g": sm["norm_attn_g"],
        "norm_ssm_g": sm["norm_ssm_g"], "norm_ffn_g": sm["norm_ffn_g"], "conv_b": sm["conv_b"],
        "norm_final_g": sm["norm_final_g"].reshape(D),
        "w_in": red["w_in_t"].T[None], "w_glu": red["w_glu"][None], "w_out": red["w_out"][None],
        "w_up": red["w_up_t"].T[None], "w_down": red["w_down"][None],
        "conv_w": lax.dynamic_slice_in_dim(sm["conv_w"], me * (2 * DFF // NDEV), 2 * DFF // NDEV, axis=1)[None],
    }

    delta, new_m, new_v = {}, {}, {}
    for k in ["w_in", "w_glu", "w_out", "w_up", "w_down"]:
        shp = w[k].shape
        d_, m_, v_ = _adamw(w[k][0], grads[k][0], m[k][0], v[k][0], "adamw_" + k)
        delta[k], new_m[k], new_v[k] = d_.reshape(shp), m_.reshape(shp), v_.reshape(shp)
    rest = SMALL_ORDER + ["conv_w"]
    shapes = [w[k].shape for k in rest]
    prow = -(-sum(math.prod(s) for s in shapes) // (PACK_W * 8)) * 8
    packs = [_pack([src[k] for k in rest], prow) for src in (w, grads, m, v)]
    outs = _adamw(*packs, "adamw_small")
    for dst, packed in zip((delta, new_m, new_v), outs):
        dst.update(dict(zip(rest, _unpack(packed, shapes))))

    total = lax.psum(loss, MESH_AXES)
    return (total, grad_x[None], *[grads[k] for k in names], *[delta[k] for k in names],
            *[new_m[k] for k in names], *[new_v[k] for k in names])
```

```python
import functools
import math

import jax
import jax.numpy as jnp
from jax import lax
from jax.experimental import pallas as pl
from jax.experimental.pallas import tpu as pltpu

F32 = jnp.float32
BF16 = jnp.bfloat16

L = 4096
D = 1024
NQ, NKV, HD = 8, 2, 64
AW = NQ * HD
KVW = NKV * HD
SW = 512
NG, GC, NP = 32, 16, 64
INW = AW + 2 * KVW + SW
DFF = 2816
BLK = 128
WIN = 3 * BLK
EPS = 1e-6
ROPE_THETA = 500000.0
NSEG = 32
TSEG = L // NSEG
SBW = 256
NSB = NG * NP // SBW
NDEV = 8
MESH_AXES = ("x", "y", "c")

LR, B1, B2, AEPS, WD, STEP = 0.001, 0.9, 0.999, 1e-08, 0.01, 10
C1 = 1.0 - B1 ** STEP
C2 = 1.0 - B2 ** STEP

VMEM_LIMIT = 56 * 1024 * 1024


def _pick(n, target, mult):
    best = None
    for t in range(mult, min(n, target) + 1, mult):
        if n % t == 0:
            best = t
    return best if best is not None else n


def _cp(sem):
    return pltpu.CompilerParams(dimension_semantics=sem, vmem_limit_bytes=VMEM_LIMIT)


def _mm(a, b, *, ta=False, tb=False, out_dtype=F32, add=None, name, tm=1024, tn=1024, tk=1024):
    m, k = (a.shape[1], a.shape[0]) if ta else a.shape
    n = b.shape[0] if tb else b.shape[1]
    assert k == (b.shape[1] if tb else b.shape[0])
    tm, tn, tk = _pick(m, tm, 128), _pick(n, tn, 128), _pick(k, tk, 128)
    nk = k // tk
    dn = (((0 if ta else 1,), (1 if tb else 0,)), ((), ()))

    def body(a_ref, b_ref, *rest):
        if add is None:
            o_ref, acc_ref = rest
        else:
            add_ref, o_ref, acc_ref = rest
        kk = pl.program_id(2)
        prod = lax.dot_general(a_ref[...].astype(BF16), b_ref[...].astype(BF16), dn, preferred_element_type=F32)

        def finish(r):
            if add is not None:
                r = r + add_ref[...]
            o_ref[...] = r.astype(out_dtype)

        if nk == 1:
            finish(prod)
            return

        @pl.when(kk == 0)
        def _():
            acc_ref[...] = prod

        @pl.when((kk > 0) & (kk < nk - 1))
        def _():
            acc_ref[...] += prod

        @pl.when(kk == nk - 1)
        def _():
            finish(acc_ref[...] + prod)

    a_spec = pl.BlockSpec((tk, tm), lambda i, j, kk: (kk, i)) if ta else pl.BlockSpec((tm, tk), lambda i, j, kk: (i, kk))
    b_spec = pl.BlockSpec((tn, tk), lambda i, j, kk: (j, kk)) if tb else pl.BlockSpec((tk, tn), lambda i, j, kk: (kk, j))
    in_specs = [a_spec, b_spec]
    args = [a, b]
    if add is not None:
        in_specs.append(pl.BlockSpec((tm, tn), lambda i, j, kk: (i, j)))
        args.append(add)
    return pl.pallas_call(
        body, name=name, grid=(m // tm, n // tn, nk),
        in_specs=in_specs, out_specs=pl.BlockSpec((tm, tn), lambda i, j, kk: (i, j)),
        out_shape=jax.ShapeDtypeStruct((m, n), out_dtype),
        scratch_shapes=[pltpu.VMEM((tm, tn) if nk > 1 else (8, 128), F32)],
        compiler_params=_cp(("parallel", "parallel", "arbitrary")),
    )(*args)


TL = 512


def _rms_fwd(x, g, name):
    d = x.shape[1]

    def body(x_ref, g_ref, o_ref):
        xv = x_ref[...]
        r = lax.rsqrt(jnp.mean(xv * xv, axis=-1, keepdims=True) + EPS)
        o_ref[...] = (xv * r * g_ref[...]).astype(BF16)

    return pl.pallas_call(
        body, name=name, grid=(L // TL,),
        in_specs=[pl.BlockSpec((TL, d), lambda i: (i, 0)), pl.BlockSpec((1, d), lambda i: (0, 0))],
        out_specs=pl.BlockSpec((TL, d), lambda i: (i, 0)),
        out_shape=jax.ShapeDtypeStruct((L, d), BF16),
        compiler_params=_cp(("parallel",)),
    )(x, g)


def _rms_bwd_tile(xv, gv, dh):
    r = lax.rsqrt(jnp.mean(xv * xv, axis=-1, keepdims=True) + EPS)
    a = dh * gv
    dx = r * a - xv * (r * r * r) * jnp.mean(a * xv, axis=-1, keepdims=True)
    dg = jnp.sum(dh * xv * r, axis=0, keepdims=True)
    return dx, dg


def _rms_bwd(x, g, dh, dres, name):
    d = x.shape[1]

    def body(x_ref, g_ref, dh_ref, dres_ref, dx_ref, dg_ref):
        dx, dg = _rms_bwd_tile(x_ref[...], g_ref[...], dh_ref[...])
        dx_ref[...] = dx + dres_ref[...]

        @pl.when(pl.program_id(0) == 0)
        def _():
            dg_ref[...] = jnp.zeros_like(dg_ref)

        dg_ref[...] += dg

    row = pl.BlockSpec((TL, d), lambda i: (i, 0))
    vec = pl.BlockSpec((1, d), lambda i: (0, 0))
    return pl.pallas_call(
        body, name=name, grid=(L // TL,),
        in_specs=[row, vec, row, row], out_specs=[row, vec],
        out_shape=[jax.ShapeDtypeStruct((L, d), F32), jax.ShapeDtypeStruct((1, d), F32)],
        compiler_params=_cp(("arbitrary",)),
    )(x, g, dh, dres)


def _mixnorm_fwd(attn, ysg, ga, gs):
    def body(a_ref, s_ref, ga_ref, gs_ref, o_ref):
        for src, gr, lo in ((a_ref, ga_ref, 0), (s_ref, gs_ref, AW)):
            xv = src[...]
            r = lax.rsqrt(jnp.mean(xv * xv, axis=-1, keepdims=True) + EPS)
            o_ref[:, lo:lo + 512] = (xv * r * gr[...]).astype(BF16)

    row = pl.BlockSpec((TL, 512), lambda i: (i, 0))
    vec = pl.BlockSpec((1, 512), lambda i: (0, 0))
    return pl.pallas_call(
        body, name="mixnorm_fwd", grid=(L // TL,),
        in_specs=[row, row, vec, vec], out_specs=pl.BlockSpec((TL, 1024), lambda i: (i, 0)),
        out_shape=jax.ShapeDtypeStruct((L, 1024), BF16),
        compiler_params=_cp(("parallel",)),
    )(attn, ysg, ga, gs)


def _mixnorm_bwd(attn, ysg, ga, gs, dmixed):
    def body(a_ref, s_ref, ga_ref, gs_ref, dm_ref, da_ref, ds_ref, dga_ref, dgs_ref):
        @pl.when(pl.program_id(0) == 0)
        def _():
            dga_ref[...] = jnp.zeros_like(dga_ref)
            dgs_ref[...] = jnp.zeros_like(dgs_ref)

        dxa, dga = _rms_bwd_tile(a_ref[...], ga_ref[...], dm_ref[:, 0:AW])
        da_ref[...] = dxa
        dga_ref[...] += dga
        dxs, dgs = _rms_bwd_tile(s_ref[...], gs_ref[...], dm_ref[:, AW:AW + SW])
        ds_ref[...] = dxs
        dgs_ref[...] += dgs

    row = pl.BlockSpec((TL, 512), lambda i: (i, 0))
    vec = pl.BlockSpec((1, 512), lambda i: (0, 0))
    return pl.pallas_call(
        body, name="mixnorm_bwd", grid=(L // TL,),
        in_specs=[row, row, vec, vec, pl.BlockSpec((TL, 1024), lambda i: (i, 0))],
        out_specs=[row, row, vec, vec],
        out_shape=[jax.ShapeDtypeStruct((L, 512), F32), jax.ShapeDtypeStruct((L, 512), F32),
                   jax.ShapeDtypeStruct((1, 512), F32), jax.ShapeDtypeStruct((1, 512), F32)],
        compiler_params=_cp(("arbitrary",)),
    )(attn, ysg, ga, gs, dmixed)


def _final_loss(x2, tgt, g):
    def body(x_ref, t_ref, g_ref, loss_ref, dx_ref, dg_ref):
        @pl.when(pl.program_id(0) == 0)
        def _():
            loss_ref[...] = jnp.zeros_like(loss_ref)
            dg_ref[...] = jnp.zeros_like(dg_ref)

        xv = x_ref[...]
        gv = g_ref[...]
        r = lax.rsqrt(jnp.mean(xv * xv, axis=-1, keepdims=True) + EPS)
        e = xv * r * gv - t_ref[...]
        loss_ref[...] += 0.5 * jnp.sum(jnp.mean(e * e, axis=-1, keepdims=True), axis=0, keepdims=True)
        dy = e * (1.0 / D)
        a = dy * gv
        dx_ref[...] = r * a - xv * (r * r * r) * jnp.mean(a * xv, axis=-1, keepdims=True)
        dg_ref[...] += jnp.sum(dy * xv * r, axis=0, keepdims=True)

    row = pl.BlockSpec((TL, D), lambda i: (i, 0))
    vec = pl.BlockSpec((1, D), lambda i: (0, 0))
    return pl.pallas_call(
        body, name="final_loss", grid=(L // TL,),
        in_specs=[row, row, vec],
        out_specs=[pl.BlockSpec((1, 1), lambda i: (0, 0)), row, vec],
        out_shape=[jax.ShapeDtypeStruct((1, 1), F32), jax.ShapeDtypeStruct((L, D), F32),
                   jax.ShapeDtypeStruct((1, D), F32)],
        compiler_params=_cp(("arbitrary",)),
    )(x2, tgt, g)


def _rope_tables():
    half = HD // 8
    inv_freq = jnp.power(ROPE_THETA, -jnp.arange(half, dtype=F32) / half)
    ang = jnp.arange(L, dtype=F32)[:, None] * inv_freq[None, :]
    cos, sin = jnp.cos(ang), jnp.sin(ang)
    one = jnp.ones((L, HD - 2 * half), F32)
    zero = jnp.zeros((L, HD - 2 * half), F32)
    zh = jnp.zeros((L, half), F32)
    cos64 = jnp.concatenate([cos, cos, one], axis=1)
    sa64 = jnp.concatenate([-sin, zh, zero], axis=1)
    sb64 = jnp.concatenate([zh, sin, zero], axis=1)
    return [jnp.tile(t, (1, 2)) for t in (cos64, sa64, sb64)]


def _rope(xv, cosv, sav, sbv, sign):
    return xv * cosv + sign * (pltpu.roll(xv, 120, 1) * sav + pltpu.roll(xv, 8, 1) * sbv)


def _rope_fwd(proj, tabs):
    def body(p_ref, c_ref, sa_ref, sb_ref, o_ref):
        xv = p_ref[...]
        roped = _rope(xv, c_ref[...], sa_ref[...], sb_ref[...], 1.0)
        o_ref[...] = jnp.where(pl.program_id(1) < 5, roped, xv).astype(BF16)

    blk = pl.BlockSpec((TL, 128), lambda i, j: (i, j))
    tab = pl.BlockSpec((TL, 128), lambda i, j: (i, 0))
    return pl.pallas_call(
        body, name="rope_fwd", grid=(L // TL, 6),
        in_specs=[blk, tab, tab, tab], out_specs=blk,
        out_shape=jax.ShapeDtypeStruct((L, AW + 2 * KVW), BF16),
        compiler_params=_cp(("parallel", "parallel")),
    )(proj, *tabs)


def _rope_bwd(dq, dk, dv, du, tabs):
    def body(dq_ref, dk_ref, dv_ref, du_ref, c_ref, sa_ref, sb_ref, o_ref):
        j = pl.program_id(1)

        @pl.when(j < 4)
        def _():
            o_ref[...] = _rope(dq_ref[...], c_ref[...], sa_ref[...], sb_ref[...], -1.0).astype(BF16)

        @pl.when(j == 4)
        def _():
            o_ref[...] = _rope(dk_ref[...], c_ref[...], sa_ref[...], sb_ref[...], -1.0).astype(BF16)

        @pl.when(j == 5)
        def _():
            o_ref[...] = dv_ref[...].astype(BF16)

        @pl.when(j > 5)
        def _():
            o_ref[...] = du_ref[...].astype(BF16)

    tab = pl.BlockSpec((TL, 128), lambda i, j: (i, 0))
    return pl.pallas_call(
        body, name="rope_bwd", grid=(L // TL, 10),
        in_specs=[pl.BlockSpec((TL, 128), lambda i, j: (i, jnp.minimum(j, 3))), tab, tab,
                  pl.BlockSpec((TL, 128), lambda i, j: (i, jnp.clip(j - 6, 0, 3))), tab, tab, tab],
        out_specs=pl.BlockSpec((TL, 128), lambda i, j: (i, j)),
        out_shape=jax.ShapeDtypeStruct((L, INW), BF16),
        compiler_params=_cp(("parallel", "parallel")),
    )(dq, dk, dv, du, *tabs)


def _attn_window(n):
    start = pl.multiple_of(jnp.clip((n - 1) * BLK, 0, L - WIN), BLK)
    qpos = n * BLK + lax.broadcasted_iota(jnp.int32, (BLK, WIN), 0)
    kpos = start + lax.broadcasted_iota(jnp.int32, (BLK, WIN), 1)
    return start, jnp.abs(kpos - qpos) <= BLK


_NT = (((1,), (1,)), ((), ()))
_TN = (((0,), (0,)), ((), ()))
NEG = -1e30


def _attn_fwd(qkv, sink):
    def body(sink_ref, q_ref, k_ref, v_ref, o_ref, lse_ref):
        n = pl.program_id(0)
        start, valid = _attn_window(n)
        kw = k_ref[pl.ds(start, WIN), :]
        vw = v_ref[pl.ds(start, WIN), :]
        for h in range(NQ):
            kv = h // (NQ // NKV)
            qh = q_ref[:, h * HD:(h + 1) * HD]
            kh = kw[:, kv * HD:(kv + 1) * HD]
            vh = vw[:, kv * HD:(kv + 1) * HD]
            s = lax.dot_general(qh, kh, _NT, preferred_element_type=F32) * (HD ** -0.5)
            s = jnp.where(valid, s, NEG)
            sk = sink_ref[h]
            m = jnp.maximum(jnp.max(s, axis=-1, keepdims=True), sk)
            p = jnp.exp(s - m)
            den = jnp.sum(p, axis=-1, keepdims=True) + jnp.exp(sk - m)
            o_ref[:, h * HD:(h + 1) * HD] = jnp.dot((p / den).astype(BF16), vh, preferred_element_type=F32)
            lse_ref[:, h:h + 1] = m + jnp.log(den)

    return pl.pallas_call(
        body, name="attn_fwd", grid=(L // BLK,),
        in_specs=[pl.BlockSpec(memory_space=pltpu.SMEM),
                  pl.BlockSpec((BLK, AW), lambda n: (n, 0)),
                  pl.BlockSpec((L, KVW), lambda n: (0, AW // KVW)),
                  pl.BlockSpec((L, KVW), lambda n: (0, AW // KVW + 1))],
        out_specs=[pl.BlockSpec((BLK, AW), lambda n: (n, 0)), pl.BlockSpec((BLK, NQ), lambda n: (n, 0))],
        out_shape=[jax.ShapeDtypeStruct((L, AW), F32), jax.ShapeDtypeStruct((L, NQ), F32)],
        compiler_params=_cp(("parallel",)),
    )(sink, qkv, qkv, qkv)


def _attn_bwd(qkv, sink, attn, lse, dattn):
    def body(sink_ref, q_ref, k_ref, v_ref, o_ref, lse_ref, do_ref, dq_ref, dk_ref, dv_ref, dsink_ref):
        n = pl.program_id(0)

        @pl.when(n == 0)
        def _():
            dk_ref[...] = jnp.zeros_like(dk_ref)
            dv_ref[...] = jnp.zeros_like(dv_ref)
            dsink_ref[...] = jnp.zeros_like(dsink_ref)

        start, valid = _attn_window(n)
        kw = k_ref[pl.ds(start, WIN), :]
        vw = v_ref[pl.ds(start, WIN), :]
        for kv in range(NKV):
            kh = kw[:, kv * HD:(kv + 1) * HD]
            vh = vw[:, kv * HD:(kv + 1) * HD]
            dk_acc = jnp.zeros((WIN, HD), F32)
            dv_acc = jnp.zeros((WIN, HD), F32)
            for h in range(kv * (NQ // NKV), (kv + 1) * (NQ // NKV)):
                qh = q_ref[:, h * HD:(h + 1) * HD]
                doh = do_ref[:, h * HD:(h + 1) * HD]
                dd = jnp.sum(doh * o_ref[:, h * HD:(h + 1) * HD], axis=-1, keepdims=True)
                lse_h = lse_ref[:, h:h + 1]
                s = lax.dot_general(qh, kh, _NT, preferred_element_type=F32) * (HD ** -0.5)
                p = jnp.where(valid, jnp.exp(s - lse_h), 0.0)
                dob = doh.astype(BF16)
                dp = lax.dot_general(dob, vh, _NT, preferred_element_type=F32)
                ds = (p * (dp - dd) * (HD ** -0.5)).astype(BF16)
                dq_ref[:, h * HD:(h + 1) * HD] = jnp.dot(ds, kh, preferred_element_type=F32)
                dk_acc += lax.dot_general(ds, qh, _TN, preferred_element_type=F32)
                dv_acc += lax.dot_general(p.astype(BF16), dob, _TN, preferred_element_type=F32)
                psink = jnp.exp(sink_ref[h] - lse_h)
                dsk = -jnp.sum(psink * dd, axis=0, keepdims=True)
                dsink_ref[h:h + 1, :] += jnp.broadcast_to(dsk, (1, 128))
            dk_ref[pl.ds(start, WIN), kv * HD:(kv + 1) * HD] += dk_acc
            dv_ref[pl.ds(start, WIN), kv * HD:(kv + 1) * HD] += dv_acc

    qblk = pl.BlockSpec((BLK, AW), lambda n: (n, 0))
    full = pl.BlockSpec((L, KVW), lambda n: (0, 0))
    return pl.pallas_call(
        body, name="attn_bwd", grid=(L // BLK,),
        in_specs=[pl.BlockSpec(memory_space=pltpu.SMEM), qblk,
                  pl.BlockSpec((L, KVW), lambda n: (0, AW // KVW)),
                  pl.BlockSpec((L, KVW), lambda n: (0, AW // KVW + 1)),
                  qblk, pl.BlockSpec((BLK, NQ), lambda n: (n, 0)), qblk],
        out_specs=[qblk, full, full, pl.BlockSpec((NQ, 128), lambda n: (0, 0))],
        out_shape=[jax.ShapeDtypeStruct((L, AW), F32), jax.ShapeDtypeStruct((L, KVW), F32),
                   jax.ShapeDtypeStruct((L, KVW), F32), jax.ShapeDtypeStruct((NQ, 128), F32)],
        compiler_params=_cp(("arbitrary",)),
    )(sink, qkv, qkv, qkv, attn, lse, dattn)


def _perm(a):
    return a.reshape(NSEG, TSEG, a.shape[1]).transpose(1, 0, 2).reshape(L, a.shape[1])


def _unperm(a):
    return a.reshape(TSEG, NSEG, a.shape[1]).transpose(1, 0, 2).reshape(L, a.shape[1])


def _cmul(ar, ai, br, bi):
    return ar * br - ai * bi, ar * bi + ai * br


def _scan_inplace(re_ref, im_ref, lr, li, rev):
    n = lr.shape[1]
    lr8 = jnp.broadcast_to(lr, (NSEG, n))
    li8 = jnp.broadcast_to(li, (NSEG, n))

    def rows(k):
        return pl.ds(pl.multiple_of(jnp.where(rev, TSEG - 1 - k, k) * NSEG, NSEG), NSEG)

    def step(k, c, store):
        sr, si = c
        rs = rows(k)
        pr, pi = _cmul(lr8, li8, sr, si)
        nr = pr + re_ref[rs, :]
        ni = pi + im_ref[rs, :]
        if store:
            re_ref[rs, :] = nr
            im_ref[rs, :] = ni
        return nr, ni

    z = jnp.zeros((NSEG, n), F32)
    er, ei = lax.fori_loop(0, TSEG, functools.partial(step, store=False), (z, z))
    pr, pi = lr, li
    for _ in range(int(math.log2(TSEG))):
        pr, pi = _cmul(pr, pi, pr, pi)

    seg = lax.broadcasted_iota(jnp.int32, (NSEG, n), 0)

    def chain(order):
        cr = jnp.zeros((1, n), F32)
        ci = jnp.zeros((1, n), F32)
        outr = jnp.zeros((NSEG, n), F32)
        outi = jnp.zeros((NSEG, n), F32)
        for s in order:
            outr = jnp.where(seg == s, cr, outr)
            outi = jnp.where(seg == s, ci, outi)
            mr, mi = _cmul(pr, pi, cr, ci)
            cr, ci = mr + er[s:s + 1], mi + ei[s:s + 1]
        return outr, outi

    fr, fi = chain(range(NSEG))
    rr, ri = chain(range(NSEG - 1, -1, -1))
    cin_r = jnp.where(rev, rr, fr)
    cin_i = jnp.where(rev, ri, fi)
    lax.fori_loop(0, TSEG, functools.partial(step, store=True), (cin_r, cin_i))
    return cin_r, cin_i


S5_RC = 512


def _s5_specs():
    u_spec = pl.BlockSpec((L, 128), lambda cb, h, d: (0, cb))
    b_spec = pl.BlockSpec((None, None, 128, SBW), lambda cb, h, d: (d, cb * 2 + h, 0, 0))
    c_spec = pl.BlockSpec((None, None, SBW, 128), lambda cb, h, d: (d, cb * 2 + h, 0, 0))
    l_spec = pl.BlockSpec((None, None, 1, SBW), lambda cb, h, d: (d, cb * 2 + h, 0, 0))
    d_spec = pl.BlockSpec((1, 128), lambda cb, h, d: (0, cb))
    return u_spec, b_spec, c_spec, l_spec, d_spec


def _s5_input_states(u_ref, bre_ref, bim_ref, lr_ref, li_ref, sre, sim, rev):
    def proj(i, _):
        rs = pl.ds(pl.multiple_of(i * S5_RC, S5_RC), S5_RC)
        ub = u_ref[rs, :].astype(BF16)
        sre[rs, :] = jnp.dot(ub, bre_ref[...], preferred_element_type=F32)
        sim[rs, :] = jnp.dot(ub, bim_ref[...], preferred_element_type=F32)
        return 0

    lax.fori_loop(0, L // S5_RC, proj, 0)
    return _scan_inplace(sre, sim, lr_ref[...], li_ref[...], rev)


def _s5_fwd(u_p, bbd_re, bbd_im, cbd_re, cbd_im, lam_re, lam_im, dskip):
    def body(u_ref, bre_ref, bim_ref, cre_ref, cim_ref, lr_ref, li_ref, d_ref, y_ref, sre, sim):
        first = (pl.program_id(1) == 0) & (pl.program_id(2) == 0)
        _s5_input_states(u_ref, bre_ref, bim_ref, lr_ref, li_ref, sre, sim, pl.program_id(2) == 1)

        def out(i, _):
            rs = pl.ds(pl.multiple_of(i * S5_RC, S5_RC), S5_RC)
            yv = (jnp.dot(sre[rs, :].astype(BF16), cre_ref[...], preferred_element_type=F32)
                  - jnp.dot(sim[rs, :].astype(BF16), cim_ref[...], preferred_element_type=F32))

            @pl.when(first)
            def _():
                y_ref[rs, :] = d_ref[...] * u_ref[rs, :] + yv

            @pl.when(jnp.logical_not(first))
            def _():
                y_ref[rs, :] += yv

            return 0

        lax.fori_loop(0, L // S5_RC, out, 0)

    u_spec, b_spec, c_spec, l_spec, d_spec = _s5_specs()
    return pl.pallas_call(
        body, name="s5_fwd", grid=(SW // 128, 2, 2),
        in_specs=[u_spec, b_spec, b_spec, c_spec, c_spec, l_spec, l_spec, d_spec],
        out_specs=u_spec, out_shape=jax.ShapeDtypeStruct((L, SW), F32),
        scratch_shapes=[pltpu.VMEM((L, SBW), F32), pltpu.VMEM((L, SBW), F32)],
        compiler_params=_cp(("parallel", "arbitrary", "arbitrary")),
    )(u_p, bbd_re, bbd_im, cbd_re, cbd_im, lam_re, lam_im, dskip)


def _s5_bwd(u_p, dy_p, bbd_re, bbd_im, cbd_re, cbd_im, lam_re, lam_im, dskip):
    def body(u_ref, dy_ref, bre_ref, bim_ref, cre_ref, cim_ref, lr_ref, li_ref, d_ref,
             du_ref, dbre_ref, dbim_ref, dcre_ref, dcim_ref, dlr_ref, dli_ref, dd_ref, sre, sim, gre, gim):
        first = (pl.program_id(1) == 0) & (pl.program_id(2) == 0)
        rev = pl.program_id(2) == 1
        cin_r, cin_i = _s5_input_states(u_ref, bre_ref, bim_ref, lr_ref, li_ref, sre, sim, rev)

        def dstate(i, _):
            rs = pl.ds(pl.multiple_of(i * S5_RC, S5_RC), S5_RC)
            dyb = dy_ref[rs, :].astype(BF16)
            gre[rs, :] = lax.dot_general(dyb, cre_ref[...], _NT, preferred_element_type=F32)
            gim[rs, :] = -lax.dot_general(dyb, cim_ref[...], _NT, preferred_element_type=F32)
            return 0

        lax.fori_loop(0, L // S5_RC, dstate, 0)
        _scan_inplace(gre, gim, lr_ref[...], -li_ref[...], jnp.logical_not(rev))

        def dlam(k, c):
            ar, ai = c
            tg = jnp.where(rev, k, k + 1)
            ts = jnp.where(rev, k + 1, k)
            rg = pl.ds(pl.multiple_of(tg * NSEG, NSEG), NSEG)
            rs = pl.ds(pl.multiple_of(ts * NSEG, NSEG), NSEG)
            gr, gi, sr, si = gre[rg, :], gim[rg, :], sre[rs, :], sim[rs, :]
            return ar + gr * sr + gi * si, ai + gi * sr - gr * si

        z = jnp.zeros((NSEG, SBW), F32)
        ar, ai = lax.fori_loop(0, TSEG - 1, dlam, (z, z))
        rb = pl.ds(pl.multiple_of(jnp.where(rev, TSEG - 1, 0) * NSEG, NSEG), NSEG)
        gr, gi = gre[rb, :], gim[rb, :]
        ar = ar + gr * cin_r + gi * cin_i
        ai = ai + gi * cin_r - gr * cin_i
        dlr_ref[...] = jnp.sum(ar, axis=0, keepdims=True)
        dli_ref[...] = jnp.sum(ai, axis=0, keepdims=True)

        dbre_ref[...] = jnp.zeros_like(dbre_ref)
        dbim_ref[...] = jnp.zeros_like(dbim_ref)
        dcre_ref[...] = jnp.zeros_like(dcre_ref)
        dcim_ref[...] = jnp.zeros_like(dcim_ref)

        @pl.when(first)
        def _():
            dd_ref[...] = jnp.zeros_like(dd_ref)

        def grads(i, _):
            rs = pl.ds(pl.multiple_of(i * S5_RC, S5_RC), S5_RC)
            uv = u_ref[rs, :]
            dyv = dy_ref[rs, :]
            ub, dyb = uv.astype(BF16), dyv.astype(BF16)
            grb, gib = gre[rs, :].astype(BF16), gim[rs, :].astype(BF16)
            srb, sib = sre[rs, :].astype(BF16), sim[rs, :].astype(BF16)
            dbre_ref[...] += lax.dot_general(ub, grb, _TN, preferred_element_type=F32)
            dbim_ref[...] += lax.dot_general(ub, gib, _TN, preferred_element_type=F32)
            dcre_ref[...] += lax.dot_general(srb, dyb, _TN, preferred_element_type=F32)
            dcim_ref[...] -= lax.dot_general(sib, dyb, _TN, preferred_element_type=F32)
            duv = (lax.dot_general(grb, bre_ref[...], _NT, preferred_element_type=F32)
                   + lax.dot_general(gib, bim_ref[...], _NT, preferred_element_type=F32))

            @pl.when(first)
            def _():
                du_ref[rs, :] = d_ref[...] * dyv + duv
                dd_ref[...] += jnp.sum(dyv * uv, axis=0, keepdims=True)

            @pl.when(jnp.logical_not(first))
            def _():
                du_ref[rs, :] += duv

            return 0

        lax.fori_loop(0, L // S5_RC, grads, 0)

    u_spec, b_spec, c_spec, l_spec, d_spec = _s5_specs()
    scr = pltpu.VMEM((L, SBW), F32)
    return pl.pallas_call(
        body, name="s5_bwd", grid=(SW // 128, 2, 2),
        in_specs=[u_spec, u_spec, b_spec, b_spec, c_spec, c_spec, l_spec, l_spec, d_spec],
        out_specs=[u_spec, b_spec, b_spec, c_spec, c_spec, l_spec, l_spec, d_spec],
        out_shape=[jax.ShapeDtypeStruct((L, SW), F32),
                   jax.ShapeDtypeStruct((2, NSB, 128, SBW), F32), jax.ShapeDtypeStruct((2, NSB, 128, SBW), F32),
                   jax.ShapeDtypeStruct((2, NSB, SBW, 128), F32), jax.ShapeDtypeStruct((2, NSB, SBW, 128), F32),
                   jax.ShapeDtypeStruct((2, NSB, 1, SBW), F32), jax.ShapeDtypeStruct((2, NSB, 1, SBW), F32),
                   jax.ShapeDtypeStruct((1, SW), F32)],
        scratch_shapes=[scr, scr, scr, scr],
        compiler_params=_cp(("parallel", "arbitrary", "arbitrary")),
    )(u_p, dy_p, bbd_re, bbd_im, cbd_re, cbd_im, lam_re, lam_im, dskip)


def _s5_params(a_re, a_im, log_step, b_re, b_im):
    lam = lax.complex(a_re, a_im)
    step = jnp.exp(log_step)[..., None]
    lam_bar = jnp.exp(lam * step)
    b_bar = ((lam_bar - 1.0) / lam)[..., None] * lax.complex(b_re, b_im)
    return jnp.real(lam_bar), jnp.imag(lam_bar), jnp.real(b_bar), jnp.imag(b_bar)


def _sel():
    i = jnp.arange(8)[None, :, None]
    j = jnp.arange(4)[None, None, :]
    r = jnp.arange(2)[:, None, None]
    return (i == r * 4 + j).astype(F32)


def _to_bbd(bb):
    return jnp.einsum('dkrjpc,rij->dkricjp', bb.reshape(2, 4, 2, 4, NP, GC), _sel()).reshape(2, NSB, 128, SBW)


def _from_bbd(dbbd):
    return jnp.einsum('dkricjp,rij->dkrjpc', dbbd.reshape(2, 4, 2, 8, GC, 4, NP), _sel()).reshape(2, NG, NP, GC)


def _to_cbd(cc):
    return jnp.einsum('dkrjcp,rij->dkrjpic', cc.reshape(2, 4, 2, 4, GC, NP), _sel()).reshape(2, NSB, SBW, 128)


def _from_cbd(dcbd):
    return jnp.einsum('dkrjpic,rij->dkrjcp', dcbd.reshape(2, 4, 2, 4, NP, 8, GC), _sel()).reshape(2, NG, GC, NP)


def _gelu(y):
    return 0.5 * y * (1.0 + lax.erf(y * (2.0 ** -0.5)))


def _gelu_grad(y):
    return 0.5 * (1.0 + lax.erf(y * (2.0 ** -0.5))) + y * jnp.exp(-0.5 * y * y) * ((2.0 * math.pi) ** -0.5)


def _sigmoid(z):
    return 1.0 / (1.0 + jnp.exp(-z))


def _glu_fwd(y, wg):
    def body(y_ref, w_ref, o_ref, z_ref):
        ys = _gelu(y_ref[...])
        z = jnp.dot(ys.astype(BF16), w_ref[...], preferred_element_type=F32)
        z_ref[...] = z
        o_ref[...] = ys * _sigmoid(z)

    row = pl.BlockSpec((TL, SW), lambda i: (i, 0))
    return pl.pallas_call(
        body, name="glu_fwd", grid=(L // TL,),
        in_specs=[row, pl.BlockSpec((SW, SW), lambda i: (0, 0))], out_specs=[row, row],
        out_shape=[jax.ShapeDtypeStruct((L, SW), F32), jax.ShapeDtypeStruct((L, SW), F32)],
        compiler_params=_cp(("parallel",)),
    )(y, wg)


def _glu_bwd(y, z, dout, wg):
    def body(y_ref, z_ref, do_ref, w_ref, dy_ref, dw_ref):
        @pl.when(pl.program_id(0) == 0)
        def _():
            dw_ref[...] = jnp.zeros_like(dw_ref)

        yv = y_ref[...]
        ys = _gelu(yv)
        sg = _sigmoid(z_ref[...])
        dov = do_ref[...]
        dz = (dov * ys * sg * (1.0 - sg)).astype(BF16)
        dys = dov * sg + lax.dot_general(dz, w_ref[...], _NT, preferred_element_type=F32)
        dy_ref[...] = dys * _gelu_grad(yv)
        dw_ref[...] += lax.dot_general(ys.astype(BF16), dz, _TN, preferred_element_type=F32)

    row = pl.BlockSpec((TL, SW), lambda i: (i, 0))
    wsp = pl.BlockSpec((SW, SW), lambda i: (0, 0))
    return pl.pallas_call(
        body, name="glu_bwd", grid=(L // TL,),
        in_specs=[row, row, row, wsp], out_specs=[row, wsp],
        out_shape=[jax.ShapeDtypeStruct((L, SW), F32), jax.ShapeDtypeStruct((SW, SW), F32)],
        compiler_params=_cp(("arbitrary",)),
    )(y, z, dout, wg)


CT = 256
CR = 128
NCT = DFF // CT


def _shifted(ref, r):
    cur = ref[pl.ds(r, CR), :]
    before = ref[pl.ds(pl.multiple_of(jnp.maximum(r - 8, 0), 8), 8), :][7:8, :]
    after = ref[pl.ds(pl.multiple_of(jnp.minimum(r + CR, L - 8), 8), 8), :][0:1, :]
    before = jnp.where(r > 0, before, 0.0)
    after = jnp.where(r + CR < L, after, 0.0)
    row = lax.broadcasted_iota(jnp.int32, cur.shape, 0)
    prev = jnp.where(row == 0, before, pltpu.roll(cur, 1, 0))
    nxt = jnp.where(row == CR - 1, after, pltpu.roll(cur, CR - 1, 0))
    return prev, cur, nxt


def _conv3(ref, r, w_ref, b_ref):
    prev, cur, nxt = _shifted(ref, r)
    return w_ref[0:1, :] * prev + w_ref[1:2, :] * cur + w_ref[2:3, :] * nxt + b_ref[...]


def _convact_fwd(up, conv_w, conv_b):
    def body(ug_ref, uv_ref, wg_ref, wv_ref, bg_ref, bv_ref, o_ref):
        def chunk(i, _):
            r = pl.multiple_of(i * CR, CR)
            g = _conv3(ug_ref, r, wg_ref, bg_ref)
            v = _conv3(uv_ref, r, wv_ref, bv_ref)
            o_ref[pl.ds(r, CR), :] = (g * _sigmoid(g) * v).astype(BF16)
            return 0

        lax.fori_loop(0, L // CR, chunk, 0)

    gcol = pl.BlockSpec((L, CT), lambda j: (0, j))
    vcol = pl.BlockSpec((L, CT), lambda j: (0, j + NCT))
    return pl.pallas_call(
        body, name="convact_fwd", grid=(NCT,),
        in_specs=[gcol, vcol,
                  pl.BlockSpec((3, CT), lambda j: (0, j)), pl.BlockSpec((3, CT), lambda j: (0, j + NCT)),
                  pl.BlockSpec((1, CT), lambda j: (0, j)), pl.BlockSpec((1, CT), lambda j: (0, j + NCT))],
        out_specs=gcol, out_shape=jax.ShapeDtypeStruct((L, DFF), BF16),
        compiler_params=_cp(("parallel",)),
    )(up, up, conv_w, conv_w, conv_b, conv_b)


def _convact_bwd(up, dact, conv_w, conv_b):
    def body(ug_ref, uv_ref, da_ref, wg_ref, wv_ref, bg_ref, bv_ref, du_ref, dw_ref, db_ref, dgs, dvs, dwv, dbv):
        half = pl.program_id(1)

        def transpose_conv(src, w_ref):
            def chunk(i, _):
                r = pl.multiple_of(i * CR, CR)
                prev, cur, nxt = _shifted(src, r)
                du_ref[pl.ds(r, CR), :] = (w_ref[0:1, :] * nxt + w_ref[1:2, :] * cur
                                           + w_ref[2:3, :] * prev).astype(BF16)
                return 0

            lax.fori_loop(0, L // CR, chunk, 0)

        @pl.when(half == 0)
        def _():
            dw_ref[...] = jnp.zeros_like(dw_ref)
            db_ref[...] = jnp.zeros_like(db_ref)
            dwv[...] = jnp.zeros_like(dwv)
            dbv[...] = jnp.zeros_like(dbv)

            def chunk1(i, _):
                r = pl.multiple_of(i * CR, CR)
                rs = pl.ds(r, CR)
                pg, cg, ng = _shifted(ug_ref, r)
                pv, cv, nv = _shifted(uv_ref, r)
                g = wg_ref[0:1, :] * pg + wg_ref[1:2, :] * cg + wg_ref[2:3, :] * ng + bg_ref[...]
                v = wv_ref[0:1, :] * pv + wv_ref[1:2, :] * cv + wv_ref[2:3, :] * nv + bv_ref[...]
                sg = _sigmoid(g)
                da = da_ref[rs, :]
                dv = da * g * sg
                dg = da * v * sg * (1.0 + g * (1.0 - sg))
                dgs[rs, :] = dg
                dvs[rs, :] = dv
                for dref, dval, taps, bref in ((dw_ref, dg, (pg, cg, ng), db_ref), (dwv, dv, (pv, cv, nv), dbv)):
                    for k in range(3):
                        dref[k:k + 1, :] += jnp.sum(dval * taps[k], axis=0, keepdims=True)
                    bref[0:1, :] += jnp.sum(dval, axis=0, keepdims=True)
                return 0

            lax.fori_loop(0, L // CR, chunk1, 0)
            transpose_conv(dgs, wg_ref)

        @pl.when(half == 1)
        def _():
            dw_ref[...] = dwv[0:3, :]
            db_ref[...] = dbv[0:1, :]
            transpose_conv(dvs, wv_ref)

    def col(rows, off):
        return pl.BlockSpec((rows, CT), lambda j, h: (0, j + off))

    def out(rows):
        return pl.BlockSpec((rows, CT), lambda j, h: (0, j + h * NCT))

    return pl.pallas_call(
        body, name="convact_bwd", grid=(NCT, 2),
        in_specs=[col(L, 0), col(L, NCT), col(L, 0), col(3, 0), col(3, NCT), col(1, 0), col(1, NCT)],
        out_specs=[out(L), out(3), out(1)],
        out_shape=[jax.ShapeDtypeStruct((L, 2 * DFF), BF16), jax.ShapeDtypeStruct((3, 2 * DFF), F32),
                   jax.ShapeDtypeStruct((1, 2 * DFF), F32)],
        scratch_shapes=[pltpu.VMEM((L, CT), F32), pltpu.VMEM((L, CT), F32),
                        pltpu.VMEM((8, CT), F32), pltpu.VMEM((8, CT), F32)],
        compiler_params=_cp(("parallel", "arbitrary")),
    )(up, up, dact, conv_w, conv_w, conv_b, conv_b)


def _local_step(x, tgt, w_in_t, w_glu, w_out, w_up_t, conv_w, w_down, p):
    tabs = _rope_tables()
    lam_re, lam_im, bb_re, bb_im = _s5_params(p["a_re"], p["a_im"], p["log_step"], p["b_re"], p["b_im"])
    bbd_re, bbd_im = _to_bbd(bb_re).astype(BF16), _to_bbd(bb_im).astype(BF16)
    cbd_re, cbd_im = _to_cbd(p["c_re"]).astype(BF16), _to_cbd(p["c_im"]).astype(BF16)
    lam_re4, lam_im4 = lam_re.reshape(2, NSB, 1, SBW), lam_im.reshape(2, NSB, 1, SBW)
    dskip = p["d_skip"].reshape(1, SW)
    g_mix, g_ffn, g_fin = p["norm_mix_g"].reshape(1, D), p["norm_ffn_g"].reshape(1, D), p["norm_final_g"].reshape(1, D)
    g_attn, g_ssm = p["norm_attn_g"].reshape(1, AW), p["norm_ssm_g"].reshape(1, SW)
    sink = p["sink"].reshape(NQ)
    conv_b = p["conv_b"].reshape(1, 2 * DFF)

    h1 = _rms_fwd(x, g_mix, "norm_mix_fwd")
    proj = _mm(h1, w_in_t, tb=True, name="in_proj", tn=1280)
    qkv = _rope_fwd(proj, tabs)
    attn, lse = _attn_fwd(qkv, sink)
    u_p = _perm(proj[:, AW + 2 * KVW:])
    y_p = _s5_fwd(u_p, bbd_re, bbd_im, cbd_re, cbd_im, lam_re4, lam_im4, dskip)
    ysg_p, z_p = _glu_fwd(y_p, w_glu)
    ysg = _unperm(ysg_p)
    mixed = _mixnorm_fwd(attn, ysg, g_attn, g_ssm)
    x1 = _mm(mixed, w_out, add=x, name="out_proj")
    h2 = _rms_fwd(x1, g_ffn, "norm_ffn_fwd")
    up = _mm(h2, w_up_t, tb=True, name="ffn_up", tn=1408)
    act = _convact_fwd(up, conv_w, conv_b)
    x2 = _mm(act, w_down, add=x1, name="ffn_down", tk=1408)
    loss, dx2, dg_fin = _final_loss(x2, tgt, g_fin)

    dact = _mm(dx2, w_down, tb=True, name="ffn_down_dx", tn=1408)
    dw_down = _mm(act, dx2, ta=True, name="ffn_down_dw", tm=1408)
    dup, dconv_w, dconv_b = _convact_bwd(up, dact, conv_w, conv_b)
    dw_up_t = _mm(dup, h2, ta=True, name="ffn_up_dw", tm=1408)
    dh2 = _mm(dup, w_up_t, name="ffn_up_dx", tk=1408)
    dx1, dg_ffn = _rms_bwd(x1, g_ffn, dh2, dx2, "norm_ffn_bwd")
    dmixed = _mm(dx1, w_out, tb=True, name="out_proj_dx")
    dw_out = _mm(mixed, dx1, ta=True, name="out_proj_dw")
    dattn, dysg, dg_attn, dg_ssm = _mixnorm_bwd(attn, ysg, g_attn, g_ssm, dmixed)
    dy_p, dw_glu = _glu_bwd(y_p, z_p, _perm(dysg), w_glu)
    du_p, dbbd_re, dbbd_im, dcbd_re, dcbd_im, dlam_re, dlam_im, dd = _s5_bwd(
        u_p, dy_p, bbd_re, bbd_im, cbd_re, cbd_im, lam_re4, lam_im4, dskip)
    dq, dk, dv, dsink = _attn_bwd(qkv, sink, attn, lse, dattn)
    dproj = _rope_bwd(dq, dk, dv, _unperm(du_p), tabs)
    dw_in_t = _mm(dproj, h1, ta=True, name="in_proj_dw", tm=1280)
    dh1 = _mm(dproj, w_in_t, name="in_proj_dx", tk=1280)
    grad_x, dg_mix = _rms_bwd(x, g_mix, dh1, dx1, "norm_mix_bwd")

    big = dict(w_in_t=dw_in_t, w_glu=dw_glu, w_out=dw_out, w_up_t=dw_up_t, w_down=dw_down)
    small = dict(norm_mix_g=dg_mix, norm_attn_g=dg_attn, norm_ssm_g=dg_ssm, norm_ffn_g=dg_ffn, norm_final_g=dg_fin,
                 sink=dsink[:, 0], conv_b=dconv_b, d_skip=dd, conv_w=dconv_w,
                 lam_re=dlam_re.reshape(2, NG, NP), lam_im=dlam_im.reshape(2, NG, NP),
                 bb_re=_from_bbd(dbbd_re), bb_im=_from_bbd(dbbd_im),
                 c_re=_from_cbd(dcbd_re), c_im=_from_cbd(dcbd_im))
    return loss[0, 0], grad_x, big, small


ANY = pl.BlockSpec(memory_space=pl.ANY)


def _coords():
    return lax.axis_index("x"), lax.axis_index("y"), lax.axis_index("c")


def _flip(v, b):
    return v + b - 2 * v * b if b else v


def _all_gather(shards, name):
    n = len(shards)

    def body(*refs):
        ins, outs = refs[:n], refs[n:2 * n]
        send_sems, recv_sems, local_sems = refs[2 * n:]
        x, y, c = _coords()
        me, sibling = (x, y, c), (x, y, 1 - c)
        chips = [(1 - x, y), (x, 1 - y), (1 - x, 1 - y)]
        waits = []
        for a in range(n):
            r = ins[a].shape[0]

            def rows(px, py, pc, a=a, r=r):
                return outs[a].at[pl.ds(pl.multiple_of((4 * px + 2 * py + pc) * r, 8), r), :]

            def copy(k, block, to, src=None, a=a, rows=rows):
                return pltpu.make_async_remote_copy(
                    src_ref=rows(*block) if src is None else src, dst_ref=rows(*block),
                    send_sem=send_sems.at[a * 7 + k], recv_sem=recv_sems.at[a * 7 + k],
                    device_id=to, device_id_type=pl.DeviceIdType.MESH)

            mine = pltpu.make_async_copy(ins[a], rows(*me), local_sems.at[a])
            mine.start()
            first = [copy(0, me, sibling, src=ins[a])]
            first += [copy(1 + j, me, (*chip, c), src=ins[a]) for j, chip in enumerate(chips)]
            for cp in first:
                cp.start()
            waits.append((mine, first, copy))
        for a in range(n):
            mine, first, copy = waits[a]
            passed = [copy(4 + j, (*chip, c), sibling) for j, chip in enumerate(chips)]
            for j, chip in enumerate(chips):
                copy(1 + j, (*chip, c), me).wait_recv()
                passed[j].start()
            waits[a] = (mine, first, copy, passed)
        for a in range(n):
            mine, first, copy, passed = waits[a]
            copy(0, sibling, me).wait_recv()
            for j, chip in enumerate(chips):
                copy(4 + j, (*chip, 1 - c), me).wait_recv()
            for cp in first + passed:
                cp.wait_send()
            mine.wait()

    return pl.pallas_call(
        body, name=name,
        in_specs=[ANY] * n, out_specs=[ANY] * n,
        out_shape=[jax.ShapeDtypeStruct((NDEV * s.shape[0], s.shape[1]), s.dtype) for s in shards],
        scratch_shapes=[pltpu.SemaphoreType.DMA((7 * n,)), pltpu.SemaphoreType.DMA((7 * n,)),
                        pltpu.SemaphoreType.DMA((n,))],
    )(*shards)


NCHIP = 4
CHIP_FLIPS = ((1, 0), (0, 1), (1, 1))


def _swap_remote(ins, outs, send_sems, recv_sems, plan):
    copies = []
    for k, (a, src, dst, to) in enumerate(plan):
        cp = pltpu.make_async_remote_copy(
            src_ref=ins[a].at[src], dst_ref=outs[a].at[dst], send_sem=send_sems.at[k], recv_sem=recv_sems.at[k],
            device_id=to, device_id_type=pl.DeviceIdType.MESH)
        cp.start()
        copies.append(cp)
    for cp in copies:
        cp.wait_recv()
    for cp in copies:
        cp.wait_send()


def _exchange_cores(parts, name):
    n = len(parts)

    def body(*refs):
        ins, outs = refs[:n], refs[n:2 * n]
        x, y, c = _coords()
        plan = [(a, 2 * q + 1 - c, q, (x, y, 1 - c)) for a in range(n) for q in range(NCHIP)]
        _swap_remote(ins, outs, *refs[2 * n:], plan)

    return pl.pallas_call(
        body, name=name, in_specs=[ANY] * n, out_specs=[ANY] * n,
        out_shape=[jax.ShapeDtypeStruct((NCHIP,) + s.shape[1:], s.dtype) for s in parts],
        scratch_shapes=[pltpu.SemaphoreType.DMA((NCHIP * n,)), pltpu.SemaphoreType.DMA((NCHIP * n,))],
    )(*parts)


def _exchange_chips(parts, name):
    n = len(parts)

    def body(*refs):
        ins, outs = refs[:n], refs[n:2 * n]
        x, y, c = _coords()
        plan = []
        for a in range(n):
            for j, (fx, fy) in enumerate(CHIP_FLIPS):
                px, py = _flip(x, fx), _flip(y, fy)
                plan.append((a, 2 * px + py, j, (px, py, c)))
        _swap_remote(ins, outs, *refs[2 * n:], plan)

    return pl.pallas_call(
        body, name=name, in_specs=[ANY] * n, out_specs=[ANY] * n,
        out_shape=[jax.ShapeDtypeStruct((3,) + s.shape[1:], s.dtype) for s in parts],
        scratch_shapes=[pltpu.SemaphoreType.DMA((3 * n,)), pltpu.SemaphoreType.DMA((3 * n,))],
    )(*parts)


def _pair_sum(where, part, recv, wire_dtype, name):
    _, r, c = part.shape
    tr = _pick(r, 256, 16)

    def body(w_ref, p_ref, r_ref, pb_ref, own_ref):
        s = p_ref[...] + r_ref[...]
        pb_ref[...] = s.astype(wire_dtype)

        @pl.when(pl.program_id(1) == w_ref[1])
        def _():
            own_ref[...] = s

    return pl.pallas_call(
        body, name=name,
        grid_spec=pltpu.PrefetchScalarGridSpec(
            num_scalar_prefetch=1, grid=(r // tr, NCHIP),
            in_specs=[pl.BlockSpec((None, tr, c), lambda i, q, w: (2 * q + w[0], i, 0)),
                      pl.BlockSpec((None, tr, c), lambda i, q, w: (q, i, 0))],
            out_specs=[pl.BlockSpec((None, tr, c), lambda i, q, w: (q, i, 0)),
                       pl.BlockSpec((tr, c), lambda i, q, w: (i, 0))]),
        out_shape=[jax.ShapeDtypeStruct((NCHIP, r, c), wire_dtype), jax.ShapeDtypeStruct((r, c), F32)],
        compiler_params=_cp(("parallel", "arbitrary")),
    )(where, part, recv)


def _chip_sum(own, recv, name):
    r, c = own.shape
    tr = _pick(r, 256, 16)

    def body(o_ref, r_ref, out_ref):
        acc = o_ref[...]
        for j in range(3):
            acc = acc + r_ref[j].astype(F32)
        out_ref[...] = acc

    return pl.pallas_call(
        body, name=name, grid=(r // tr,),
        in_specs=[pl.BlockSpec((tr, c), lambda i: (i, 0)), pl.BlockSpec((3, tr, c), lambda i: (0, i, 0))],
        out_specs=pl.BlockSpec((tr, c), lambda i: (i, 0)),
        out_shape=jax.ShapeDtypeStruct((r, c), F32),
        compiler_params=_cp(("parallel",)),
    )(own, recv)


def _adamw(w, g, m, v, name):
    r, c = w.shape
    tr = _pick(r, 256, 8)

    def body(w_ref, g_ref, m_ref, v_ref, d_ref, nm_ref, nv_ref):
        gv = g_ref[...]
        nm = B1 * m_ref[...] + (1.0 - B1) * gv
        nv = B2 * v_ref[...] + (1.0 - B2) * (gv * gv)
        nm_ref[...] = nm
        nv_ref[...] = nv
        d_ref[...] = -LR * ((nm / C1) / (jnp.sqrt(nv / C2) + AEPS) + WD * w_ref[...])

    blk = pl.BlockSpec((tr, c), lambda i: (i, 0))
    return pl.pallas_call(
        body, name=name, grid=(r // tr,),
        in_specs=[blk] * 4, out_specs=[blk] * 3,
        out_shape=[jax.ShapeDtypeStruct((r, c), F32)] * 3,
        compiler_params=_cp(("parallel",)),
    )(w, g, m, v)


SMALL_ORDER = ["norm_mix_g", "a_re", "a_im", "log_step", "b_re", "b_im", "c_re", "c_im", "d_skip", "sink",
               "norm_attn_g", "norm_ssm_g", "norm_ffn_g", "conv_b", "norm_final_g"]
PACK_W = 1024


def _pack(arrs, rows):
    flat = jnp.concatenate([a.reshape(-1).astype(F32) for a in arrs])
    return jnp.pad(flat, (0, rows * PACK_W - flat.shape[0])).reshape(rows, PACK_W)


def _unpack(packed, shapes):
    flat = packed.reshape(-1)
    out, off = [], 0
    for s in shapes:
        size = math.prod(s)
        out.append(flat[off:off + size].reshape(s))
        off += size
    return out


def kernel(x, norm_mix_g, w_in, a_re, a_im, log_step, b_re, b_im, c_re, c_im, d_skip, w_glu, sink, norm_attn_g, norm_ssm_g, w_out, norm_ffn_g, w_up, conv_w, conv_b, w_down, norm_final_g, loss_target, m_norm_mix_g, m_w_in, m_a_re, m_a_im, m_log_step, m_b_re, m_b_im, m_c_re, m_c_im, m_d_skip, m_w_glu, m_sink, m_norm_attn_g, m_norm_ssm_g, m_w_out, m_norm_ffn_g, m_w_up, m_conv_w, m_conv_b, m_w_down, m_norm_final_g, v_norm_mix_g, v_w_in, v_a_re, v_a_im, v_log_step, v_b_re, v_b_im, v_c_re, v_c_im, v_d_skip, v_w_glu, v_sink, v_norm_attn_g, v_norm_ssm_g, v_w_out, v_norm_ffn_g, v_w_up, v_conv_w, v_conv_b, v_w_down, v_norm_final_g):
    args = dict(locals())
    names = ["norm_mix_g", "w_in", "a_re", "a_im", "log_step", "b_re", "b_im", "c_re", "c_im", "d_skip", "w_glu",
             "sink", "norm_attn_g", "norm_ssm_g", "w_out", "norm_ffn_g", "w_up", "conv_w", "conv_b", "w_down",
             "norm_final_g"]
    w = {k: args[k] for k in names}
    m = {k: args["m_" + k] for k in names}
    v = {k: args["v_" + k] for k in names}

    shards = [w_in[0].T.astype(BF16), w_glu[0].astype(BF16), w_out[0].astype(BF16), w_up[0].T.astype(BF16),
              w_down[0].astype(BF16), jnp.pad(conv_w[0], ((0, 5), (0, 0)))]
    w_in_t, w_glu_f, w_out_f, w_up_t, w_down_f, conv_w_g = _all_gather(shards, "gather_weights")
    conv_w_f = conv_w_g.reshape(NDEV, 8, 2 * DFF // NDEV)[:, :3].transpose(1, 0, 2).reshape(3, 2 * DFF)

    p = {k: w[k][0] for k in SMALL_ORDER if k != "norm_final_g"}
    p["norm_final_g"] = norm_final_g
    loss, grad_x, big, small = _local_step(x[0], loss_target[0], w_in_t, w_glu_f, w_out_f, w_up_t, conv_w_f,
                                           w_down_f, p)

    small["loss"] = loss.reshape(1)
    small_names = list(small.keys())
    small_shapes = [small[k].shape for k in small_names]
    n_small = sum(math.prod(s) for s in small_shapes)
    rows_dev = -(-n_small // (PACK_W * NDEV * 16)) * 16
    spack = _pack([small[k] for k in small_names], rows_dev * NDEV)
    big_names = ["w_in_t", "w_glu", "w_out", "w_up_t", "w_down"]
    all_names = big_names + ["small"]
    parts = [big[k].reshape(NDEV, big[k].shape[0] // NDEV, big[k].shape[1]) for k in big_names]
    parts.append(spack.reshape(NDEV, rows_dev, PACK_W))
    ax, ay, ac = _coords()
    me = 4 * ax + 2 * ay + ac
    where = jnp.stack([ac, 2 * ax + ay]).astype(jnp.int32)
    from_core = _exchange_cores(parts, "exchange_cores")
    sums = [_pair_sum(where, p_, r_, F32 if k == "small" else BF16, "pair_sum_" + k)
            for k, p_, r_ in zip(all_names, parts, from_core)]
    from_chips = _exchange_chips([s[0] for s in sums], "exchange_chips")
    red = {k: _chip_sum(s[1], r_, "chip_sum_" + k) for k, s, r_ in zip(all_names, sums, from_chips)}
    (small_full,) = _all_gather([red["small"]], "gather_small")
    sm = dict(zip(small_names, _unpack(small_full, small_shapes)))

    _, s5_vjp = jax.vjp(_s5_params, a_re[0], a_im[0], log_step[0], b_re[0], b_im[0])
    da_re, da_im, dlog_step, db_re, db_im = s5_vjp((sm["lam_re"], sm["lam_im"], sm["bb_re"], sm["bb_im"]))
    grads = {
        "norm_mix_g": sm["norm_mix_g"], "a_re": da_re[None], "a_im": da_im[None], "log_step": dlog_step[None],
        "b_re": db_re[None], "b_im": db_im[None], "c_re": sm["c_re"][None], "c_im": sm["c_im"][None],
        "d_skip": sm["d_skip"].reshape(1, NG, GC), "sink": sm["sink"][None], "norm_attn_g": sm["norm_attn_g"],
        "norm_ssm_g": sm["norm_ssm_g"], "norm_ffn_g": sm["norm_ffn_g"], "conv_b": sm["conv_b"],
        "norm_final_g": sm["norm_final_g"].reshape(D),
        "w_in": red["w_in_t"].T[None], "w_glu": red["w_glu"][None], "w_out": red["w_out"][None],
        "w_up": red["w_up_t"].T[None], "w_down": red["w_down"][None],
        "conv_w": lax.dynamic_slice_in_dim(sm["conv_w"], me * (2 * DFF // NDEV), 2 * DFF // NDEV, axis=1)[None],
    }

    delta, new_m, new_v = {}, {}, {}
    for k in ["w_in", "w_glu", "w_out", "w_up", "w_down"]:
        shp = w[k].shape
        d_, m_, v_ = _adamw(w[k][0], grads[k][0], m[k][0], v[k][0], "adamw_" + k)
        delta[k], new_m[k], new_v[k] = d_.reshape(shp), m_.reshape(shp), v_.reshape(shp)
    rest = SMALL_ORDER + ["conv_w"]
    shapes = [w[k].shape for k in rest]
    prow = -(-sum(math.prod(s) for s in shapes) // (PACK_W * 8)) * 8
    packs = [_pack([src[k] for k in rest], prow) for src in (w, grads, m, v)]
    outs = _adamw(*packs, "adamw_small")
    for dst, packed in zip((delta, new_m, new_v), outs):
        dst.update(dict(zip(rest, _unpack(packed, shapes))))

    return (sm["loss"][0], grad_x[None], *[grads[k] for k in names], *[delta[k] for k in names],
            *[new_m[k] for k in names], *[new_v[k] for k in names])
```

```python
import functools
import math

import jax
import jax.numpy as jnp
from jax import lax
from jax.experimental import pallas as pl
from jax.experimental.pallas import tpu as pltpu

F32 = jnp.float32
BF16 = jnp.bfloat16

L = 4096
D = 1024
NQ, NKV, HD = 8, 2, 64
AW = NQ * HD
KVW = NKV * HD
SW = 512
NG, GC, NP = 32, 16, 64
INW = AW + 2 * KVW + SW
DFF = 2816
BLK = 128
WIN = 3 * BLK
EPS = 1e-6
ROPE_THETA = 500000.0
NSEG = 32
TSEG = L // NSEG
SBW = 256
NSB = NG * NP // SBW
NDEV = 8
MESH_AXES = ("x", "y", "c")

LR, B1, B2, AEPS, WD, STEP = 0.001, 0.9, 0.999, 1e-08, 0.01, 10
C1 = 1.0 - B1 ** STEP
C2 = 1.0 - B2 ** STEP

VMEM_LIMIT = 56 * 1024 * 1024


def _pick(n, target, mult):
    best = None
    for t in range(mult, min(n, target) + 1, mult):
        if n % t == 0:
            best = t
    return best if best is not None else n


def _cp(sem):
    return pltpu.CompilerParams(dimension_semantics=sem, vmem_limit_bytes=VMEM_LIMIT)


def _mm(a, b, *, ta=False, tb=False, out_dtype=F32, add=None, name, tm=1024, tn=1024, tk=1024):
    m, k = (a.shape[1], a.shape[0]) if ta else a.shape
    n = b.shape[0] if tb else b.shape[1]
    assert k == (b.shape[1] if tb else b.shape[0])
    tm, tn, tk = _pick(m, tm, 128), _pick(n, tn, 128), _pick(k, tk, 128)
    nk = k // tk
    dn = (((0 if ta else 1,), (1 if tb else 0,)), ((), ()))

    def body(a_ref, b_ref, *rest):
        if add is None:
            o_ref, acc_ref = rest
        else:
            add_ref, o_ref, acc_ref = rest
        kk = pl.program_id(2)
        prod = lax.dot_general(a_ref[...].astype(BF16), b_ref[...].astype(BF16), dn, preferred_element_type=F32)

        def finish(r):
            if add is not None:
                r = r + add_ref[...]
            o_ref[...] = r.astype(out_dtype)

        if nk == 1:
            finish(prod)
            return

        @pl.when(kk == 0)
        def _():
            acc_ref[...] = prod

        @pl.when((kk > 0) & (kk < nk - 1))
        def _():
            acc_ref[...] += prod

        @pl.when(kk == nk - 1)
        def _():
            finish(acc_ref[...] + prod)

    a_spec = pl.BlockSpec((tk, tm), lambda i, j, kk: (kk, i)) if ta else pl.BlockSpec((tm, tk), lambda i, j, kk: (i, kk))
    b_spec = pl.BlockSpec((tn, tk), lambda i, j, kk: (j, kk)) if tb else pl.BlockSpec((tk, tn), lambda i, j, kk: (kk, j))
    in_specs = [a_spec, b_spec]
    args = [a, b]
    if add is not None:
        in_specs.append(pl.BlockSpec((tm, tn), lambda i, j, kk: (i, j)))
        args.append(add)
    return pl.pallas_call(
        body, name=name, grid=(m // tm, n // tn, nk),
        in_specs=in_specs, out_specs=pl.BlockSpec((tm, tn), lambda i, j, kk: (i, j)),
        out_shape=jax.ShapeDtypeStruct((m, n), out_dtype),
        scratch_shapes=[pltpu.VMEM((tm, tn) if nk > 1 else (8, 128), F32)],
        compiler_params=_cp(("parallel", "parallel", "arbitrary")),
    )(*args)


TL = 512


def _rms_fwd(x, g, name):
    d = x.shape[1]

    def body(x_ref, g_ref, o_ref):
        xv = x_ref[...]
        r = lax.rsqrt(jnp.mean(xv * xv, axis=-1, keepdims=True) + EPS)
        o_ref[...] = (xv * r * g_ref[...]).astype(BF16)

    return pl.pallas_call(
        body, name=name, grid=(L // TL,),
        in_specs=[pl.BlockSpec((TL, d), lambda i: (i, 0)), pl.BlockSpec((1, d), lambda i: (0, 0))],
        out_specs=pl.BlockSpec((TL, d), lambda i: (i, 0)),
        out_shape=jax.ShapeDtypeStruct((L, d), BF16),
        compiler_params=_cp(("parallel",)),
    )(x, g)


def _rms_bwd_tile(xv, gv, dh):
    r = lax.rsqrt(jnp.mean(xv * xv, axis=-1, keepdims=True) + EPS)
    a = dh * gv
    dx = r * a - xv * (r * r * r) * jnp.mean(a * xv, axis=-1, keepdims=True)
    dg = jnp.sum(dh * xv * r, axis=0, keepdims=True)
    return dx, dg


def _rms_bwd(x, g, dh, dres, name):
    d = x.shape[1]

    def body(x_ref, g_ref, dh_ref, dres_ref, dx_ref, dg_ref):
        dx, dg = _rms_bwd_tile(x_ref[...], g_ref[...], dh_ref[...])
        dx_ref[...] = dx + dres_ref[...]

        @pl.when(pl.program_id(0) == 0)
        def _():
            dg_ref[...] = jnp.zeros_like(dg_ref)

        dg_ref[...] += dg

    row = pl.BlockSpec((TL, d), lambda i: (i, 0))
    vec = pl.BlockSpec((1, d), lambda i: (0, 0))
    return pl.pallas_call(
        body, name=name, grid=(L // TL,),
        in_specs=[row, vec, row, row], out_specs=[row, vec],
        out_shape=[jax.ShapeDtypeStruct((L, d), F32), jax.ShapeDtypeStruct((1, d), F32)],
        compiler_params=_cp(("arbitrary",)),
    )(x, g, dh, dres)


def _mixnorm_fwd(attn, ysg, ga, gs):
    def body(a_ref, s_ref, ga_ref, gs_ref, o_ref):
        for src, gr, lo in ((a_ref, ga_ref, 0), (s_ref, gs_ref, AW)):
            xv = src[...]
            r = lax.rsqrt(jnp.mean(xv * xv, axis=-1, keepdims=True) + EPS)
            o_ref[:, lo:lo + 512] = (xv * r * gr[...]).astype(BF16)

    row = pl.BlockSpec((TL, 512), lambda i: (i, 0))
    vec = pl.BlockSpec((1, 512), lambda i: (0, 0))
    return pl.pallas_call(
        body, name="mixnorm_fwd", grid=(L // TL,),
        in_specs=[row, row, vec, vec], out_specs=pl.BlockSpec((TL, 1024), lambda i: (i, 0)),
        out_shape=jax.ShapeDtypeStruct((L, 1024), BF16),
        compiler_params=_cp(("parallel",)),
    )(attn, ysg, ga, gs)


def _mixnorm_bwd(attn, ysg, ga, gs, dmixed):
    def body(a_ref, s_ref, ga_ref, gs_ref, dm_ref, da_ref, ds_ref, dga_ref, dgs_ref):
        @pl.when(pl.program_id(0) == 0)
        def _():
            dga_ref[...] = jnp.zeros_like(dga_ref)
            dgs_ref[...] = jnp.zeros_like(dgs_ref)

        dxa, dga = _rms_bwd_tile(a_ref[...], ga_ref[...], dm_ref[:, 0:AW])
        da_ref[...] = dxa
        dga_ref[...] += dga
        dxs, dgs = _rms_bwd_tile(s_ref[...], gs_ref[...], dm_ref[:, AW:AW + SW])
        ds_ref[...] = dxs
        dgs_ref[...] += dgs

    row = pl.BlockSpec((TL, 512), lambda i: (i, 0))
    vec = pl.BlockSpec((1, 512), lambda i: (0, 0))
    return pl.pallas_call(
        body, name="mixnorm_bwd", grid=(L // TL,),
        in_specs=[row, row, vec, vec, pl.BlockSpec((TL, 1024), lambda i: (i, 0))],
        out_specs=[row, row, vec, vec],
        out_shape=[jax.ShapeDtypeStruct((L, 512), F32), jax.ShapeDtypeStruct((L, 512), F32),
                   jax.ShapeDtypeStruct((1, 512), F32), jax.ShapeDtypeStruct((1, 512), F32)],
        compiler_params=_cp(("arbitrary",)),
    )(attn, ysg, ga, gs, dmixed)


def _final_loss(x2, tgt, g):
    def body(x_ref, t_ref, g_ref, loss_ref, dx_ref, dg_ref):
        @pl.when(pl.program_id(0) == 0)
        def _():
            loss_ref[...] = jnp.zeros_like(loss_ref)
            dg_ref[...] = jnp.zeros_like(dg_ref)

        xv = x_ref[...]
        gv = g_ref[...]
        r = lax.rsqrt(jnp.mean(xv * xv, axis=-1, keepdims=True) + EPS)
        e = xv * r * gv - t_ref[...]
        loss_ref[...] += 0.5 * jnp.sum(jnp.mean(e * e, axis=-1, keepdims=True), axis=0, keepdims=True)
        dy = e * (1.0 / D)
        a = dy * gv
        dx_ref[...] = r * a - xv * (r * r * r) * jnp.mean(a * xv, axis=-1, keepdims=True)
        dg_ref[...] += jnp.sum(dy * xv * r, axis=0, keepdims=True)

    row = pl.BlockSpec((TL, D), lambda i: (i, 0))
    vec = pl.BlockSpec((1, D), lambda i: (0, 0))
    return pl.pallas_call(
        body, name="final_loss", grid=(L // TL,),
        in_specs=[row, row, vec],
        out_specs=[pl.BlockSpec((1, 1), lambda i: (0, 0)), row, vec],
        out_shape=[jax.ShapeDtypeStruct((1, 1), F32), jax.ShapeDtypeStruct((L, D), F32),
                   jax.ShapeDtypeStruct((1, D), F32)],
        compiler_params=_cp(("arbitrary",)),
    )(x2, tgt, g)


def _rope_tables():
    half = HD // 8
    inv_freq = jnp.power(ROPE_THETA, -jnp.arange(half, dtype=F32) / half)
    ang = jnp.arange(L, dtype=F32)[:, None] * inv_freq[None, :]
    cos, sin = jnp.cos(ang), jnp.sin(ang)
    one = jnp.ones((L, HD - 2 * half), F32)
    zero = jnp.zeros((L, HD - 2 * half), F32)
    zh = jnp.zeros((L, half), F32)
    cos64 = jnp.concatenate([cos, cos, one], axis=1)
    sa64 = jnp.concatenate([-sin, zh, zero], axis=1)
    sb64 = jnp.concatenate([zh, sin, zero], axis=1)
    return [jnp.tile(t, (1, 2)) for t in (cos64, sa64, sb64)]


def _rope(xv, cosv, sav, sbv, sign):
    return xv * cosv + sign * (pltpu.roll(xv, 120, 1) * sav + pltpu.roll(xv, 8, 1) * sbv)


def _rope_fwd(proj, tabs):
    qkw = AW + 2 * KVW

    def body(p_ref, c_ref, sa_ref, sb_ref, o_ref):
        for j in range(qkw // 128):
            cols = slice(j * 128, (j + 1) * 128)
            xv = p_ref[:, cols]
            if j < (AW + KVW) // 128:
                xv = _rope(xv, c_ref[...], sa_ref[...], sb_ref[...], 1.0)
            o_ref[:, cols] = xv.astype(BF16)

    blk = pl.BlockSpec((TL, qkw), lambda i: (i, 0))
    tab = pl.BlockSpec((TL, 128), lambda i: (i, 0))
    return pl.pallas_call(
        body, name="rope_fwd", grid=(L // TL,),
        in_specs=[blk, tab, tab, tab], out_specs=blk,
        out_shape=jax.ShapeDtypeStruct((L, qkw), BF16),
        compiler_params=_cp(("parallel",)),
    )(proj, *tabs)


def _rope_bwd(dq, dk, dv, du, tabs):
    def body(dq_ref, dk_ref, dv_ref, du_ref, c_ref, sa_ref, sb_ref, o_ref):
        for j in range(AW // 128):
            cols = slice(j * 128, (j + 1) * 128)
            o_ref[:, cols] = _rope(dq_ref[:, cols], c_ref[...], sa_ref[...], sb_ref[...], -1.0).astype(BF16)
        o_ref[:, AW:AW + KVW] = _rope(dk_ref[...], c_ref[...], sa_ref[...], sb_ref[...], -1.0).astype(BF16)
        o_ref[:, AW + KVW:AW + 2 * KVW] = dv_ref[...].astype(BF16)
        o_ref[:, AW + 2 * KVW:] = du_ref[...].astype(BF16)

    def row(width):
        return pl.BlockSpec((TL, width), lambda i: (i, 0))

    return pl.pallas_call(
        body, name="rope_bwd", grid=(L // TL,),
        in_specs=[row(AW), row(KVW), row(KVW), row(SW), row(128), row(128), row(128)],
        out_specs=row(INW), out_shape=jax.ShapeDtypeStruct((L, INW), BF16),
        compiler_params=_cp(("parallel",)),
    )(dq, dk, dv, du, *tabs)


def _attn_window(n):
    start = pl.multiple_of(jnp.clip((n - 1) * BLK, 0, L - WIN), BLK)
    qpos = n * BLK + lax.broadcasted_iota(jnp.int32, (BLK, WIN), 0)
    kpos = start + lax.broadcasted_iota(jnp.int32, (BLK, WIN), 1)
    return start, jnp.abs(kpos - qpos) <= BLK


_NT = (((1,), (1,)), ((), ()))
_TN = (((0,), (0,)), ((), ()))
NEG = -1e30


def _attn_fwd(qkv, sink):
    def body(sink_ref, q_ref, k_ref, v_ref, o_ref, lse_ref):
        n = pl.program_id(0)
        start, valid = _attn_window(n)
        kw = k_ref[pl.ds(start, WIN), :]
        vw = v_ref[pl.ds(start, WIN), :]
        for h in range(NQ):
            kv = h // (NQ // NKV)
            qh = q_ref[:, h * HD:(h + 1) * HD]
            kh = kw[:, kv * HD:(kv + 1) * HD]
            vh = vw[:, kv * HD:(kv + 1) * HD]
            s = lax.dot_general(qh, kh, _NT, preferred_element_type=F32) * (HD ** -0.5)
            s = jnp.where(valid, s, NEG)
            sk = sink_ref[h]
            m = jnp.maximum(jnp.max(s, axis=-1, keepdims=True), sk)
            p = jnp.exp(s - m)
            den = jnp.sum(p, axis=-1, keepdims=True) + jnp.exp(sk - m)
            o_ref[:, h * HD:(h + 1) * HD] = jnp.dot((p / den).astype(BF16), vh, preferred_element_type=F32)
            lse_ref[:, h:h + 1] = m + jnp.log(den)

    return pl.pallas_call(
        body, name="attn_fwd", grid=(L // BLK,),
        in_specs=[pl.BlockSpec(memory_space=pltpu.SMEM),
                  pl.BlockSpec((BLK, AW), lambda n: (n, 0)),
                  pl.BlockSpec((L, KVW), lambda n: (0, AW // KVW)),
                  pl.BlockSpec((L, KVW), lambda n: (0, AW // KVW + 1))],
        out_specs=[pl.BlockSpec((BLK, AW), lambda n: (n, 0)), pl.BlockSpec((BLK, NQ), lambda n: (n, 0))],
        out_shape=[jax.ShapeDtypeStruct((L, AW), F32), jax.ShapeDtypeStruct((L, NQ), F32)],
        compiler_params=_cp(("parallel",)),
    )(sink, qkv, qkv, qkv)


def _attn_bwd(qkv, sink, attn, lse, dattn):
    def body(sink_ref, q_ref, k_ref, v_ref, o_ref, lse_ref, do_ref, dq_ref, dk_ref, dv_ref, dsink_ref):
        n = pl.program_id(0)

        @pl.when(n == 0)
        def _():
            dk_ref[...] = jnp.zeros_like(dk_ref)
            dv_ref[...] = jnp.zeros_like(dv_ref)
            dsink_ref[...] = jnp.zeros_like(dsink_ref)

        start, valid = _attn_window(n)
        kw = k_ref[pl.ds(start, WIN), :]
        vw = v_ref[pl.ds(start, WIN), :]
        for kv in range(NKV):
            kh = kw[:, kv * HD:(kv + 1) * HD]
            vh = vw[:, kv * HD:(kv + 1) * HD]
            dk_acc = jnp.zeros((WIN, HD), F32)
            dv_acc = jnp.zeros((WIN, HD), F32)
            for h in range(kv * (NQ // NKV), (kv + 1) * (NQ // NKV)):
                qh = q_ref[:, h * HD:(h + 1) * HD]
                doh = do_ref[:, h * HD:(h + 1) * HD]
                dd = jnp.sum(doh * o_ref[:, h * HD:(h + 1) * HD], axis=-1, keepdims=True)
                lse_h = lse_ref[:, h:h + 1]
                s = lax.dot_general(qh, kh, _NT, preferred_element_type=F32) * (HD ** -0.5)
                p = jnp.where(valid, jnp.exp(s - lse_h), 0.0)
                dob = doh.astype(BF16)
                dp = lax.dot_general(dob, vh, _NT, preferred_element_type=F32)
                ds = (p * (dp - dd) * (HD ** -0.5)).astype(BF16)
                dq_ref[:, h * HD:(h + 1) * HD] = jnp.dot(ds, kh, preferred_element_type=F32)
                dk_acc += lax.dot_general(ds, qh, _TN, preferred_element_type=F32)
                dv_acc += lax.dot_general(p.astype(BF16), dob, _TN, preferred_element_type=F32)
                psink = jnp.exp(sink_ref[h] - lse_h)
                dsk = -jnp.sum(psink * dd, axis=0, keepdims=True)
                dsink_ref[h:h + 1, :] += jnp.broadcast_to(dsk, (1, 128))
            dk_ref[pl.ds(start, WIN), kv * HD:(kv + 1) * HD] += dk_acc
            dv_ref[pl.ds(start, WIN), kv * HD:(kv + 1) * HD] += dv_acc

    qblk = pl.BlockSpec((BLK, AW), lambda n: (n, 0))
    full = pl.BlockSpec((L, KVW), lambda n: (0, 0))
    return pl.pallas_call(
        body, name="attn_bwd", grid=(L // BLK,),
        in_specs=[pl.BlockSpec(memory_space=pltpu.SMEM), qblk,
                  pl.BlockSpec((L, KVW), lambda n: (0, AW // KVW)),
                  pl.BlockSpec((L, KVW), lambda n: (0, AW // KVW + 1)),
                  qblk, pl.BlockSpec((BLK, NQ), lambda n: (n, 0)), qblk],
        out_specs=[qblk, full, full, pl.BlockSpec((NQ, 128), lambda n: (0, 0))],
        out_shape=[jax.ShapeDtypeStruct((L, AW), F32), jax.ShapeDtypeStruct((L, KVW), F32),
                   jax.ShapeDtypeStruct((L, KVW), F32), jax.ShapeDtypeStruct((NQ, 128), F32)],
        compiler_params=_cp(("arbitrary",)),
    )(sink, qkv, qkv, qkv, attn, lse, dattn)


def _perm(a):
    return a.reshape(NSEG, TSEG, a.shape[1]).transpose(1, 0, 2).reshape(L, a.shape[1])


def _unperm(a):
    return a.reshape(TSEG, NSEG, a.shape[1]).transpose(1, 0, 2).reshape(L, a.shape[1])


def _cmul(ar, ai, br, bi):
    return ar * br - ai * bi, ar * bi + ai * br


def _scan_inplace(s_ref, lr, li, rev):
    n = lr.shape[1]
    lr8 = jnp.broadcast_to(lr, (NSEG, n))
    li8 = jnp.broadcast_to(li, (NSEG, n))

    def rows(k):
        return pl.ds(pl.multiple_of(jnp.where(rev, TSEG - 1 - k, k) * NSEG, NSEG), NSEG)

    def step(k, c, store):
        sr, si = c
        rs = rows(k)
        pr, pi = _cmul(lr8, li8, sr, si)
        nr = pr + s_ref[rs, 0:n]
        ni = pi + s_ref[rs, n:2 * n]
        if store:
            s_ref[rs, 0:n] = nr
            s_ref[rs, n:2 * n] = ni
        return nr, ni

    z = jnp.zeros((NSEG, n), F32)
    er, ei = lax.fori_loop(0, TSEG, functools.partial(step, store=False), (z, z))
    pr, pi = lr, li
    for _ in range(int(math.log2(TSEG))):
        pr, pi = _cmul(pr, pi, pr, pi)

    seg = lax.broadcasted_iota(jnp.int32, (NSEG, n), 0)

    def chain(order):
        cr = jnp.zeros((1, n), F32)
        ci = jnp.zeros((1, n), F32)
        outr = jnp.zeros((NSEG, n), F32)
        outi = jnp.zeros((NSEG, n), F32)
        for s in order:
            outr = jnp.where(seg == s, cr, outr)
            outi = jnp.where(seg == s, ci, outi)
            mr, mi = _cmul(pr, pi, cr, ci)
            cr, ci = mr + er[s:s + 1], mi + ei[s:s + 1]
        return outr, outi

    fr, fi = chain(range(NSEG))
    rr, ri = chain(range(NSEG - 1, -1, -1))
    cin_r = jnp.where(rev, rr, fr)
    cin_i = jnp.where(rev, ri, fi)
    lax.fori_loop(0, TSEG, functools.partial(step, store=True), (cin_r, cin_i))
    return cin_r, cin_i


S5_RC = 512


def _s5_specs():
    u_spec = pl.BlockSpec((L, 128), lambda cb, h, d: (0, cb))
    b_spec = pl.BlockSpec((None, None, 128, 2 * SBW), lambda cb, h, d: (d, cb * 2 + h, 0, 0))
    c_spec = pl.BlockSpec((None, None, 2 * SBW, 128), lambda cb, h, d: (d, cb * 2 + h, 0, 0))
    l_spec = pl.BlockSpec((None, None, 1, SBW), lambda cb, h, d: (d, cb * 2 + h, 0, 0))
    d_spec = pl.BlockSpec((1, 128), lambda cb, h, d: (0, cb))
    return u_spec, b_spec, c_spec, l_spec, d_spec


def _s5_input_states(u_ref, b_ref, lr_ref, li_ref, s_scr, rev):
    def proj(i, _):
        rs = pl.ds(pl.multiple_of(i * S5_RC, S5_RC), S5_RC)
        s_scr[rs, :] = jnp.dot(u_ref[rs, :].astype(BF16), b_ref[...], preferred_element_type=F32)
        return 0

    lax.fori_loop(0, L // S5_RC, proj, 0)
    return _scan_inplace(s_scr, lr_ref[...], li_ref[...], rev)


def _s5_fwd(u_p, bcat, ccat, lam_re, lam_im, dskip):
    def body(u_ref, b_ref, c_ref, lr_ref, li_ref, d_ref, y_ref, s_scr):
        first = (pl.program_id(1) == 0) & (pl.program_id(2) == 0)
        _s5_input_states(u_ref, b_ref, lr_ref, li_ref, s_scr, pl.program_id(2) == 1)

        def out(i, _):
            rs = pl.ds(pl.multiple_of(i * S5_RC, S5_RC), S5_RC)
            yv = jnp.dot(s_scr[rs, :].astype(BF16), c_ref[...], preferred_element_type=F32)

            @pl.when(first)
            def _():
                y_ref[rs, :] = d_ref[...] * u_ref[rs, :] + yv

            @pl.when(jnp.logical_not(first))
            def _():
                y_ref[rs, :] += yv

            return 0

        lax.fori_loop(0, L // S5_RC, out, 0)

    u_spec, b_spec, c_spec, l_spec, d_spec = _s5_specs()
    return pl.pallas_call(
        body, name="s5_fwd", grid=(SW // 128, 2, 2),
        in_specs=[u_spec, b_spec, c_spec, l_spec, l_spec, d_spec],
        out_specs=u_spec, out_shape=jax.ShapeDtypeStruct((L, SW), F32),
        scratch_shapes=[pltpu.VMEM((L, 2 * SBW), F32)],
        compiler_params=_cp(("parallel", "arbitrary", "arbitrary")),
    )(u_p, bcat, ccat, lam_re, lam_im, dskip)


def _s5_bwd(u_p, dy_p, bcat, ccat, lam_re, lam_im, dskip):
    def body(u_ref, dy_ref, b_ref, c_ref, lr_ref, li_ref, d_ref,
             du_ref, db_ref, dc_ref, dlr_ref, dli_ref, dd_ref, s_scr, g_scr):
        first = (pl.program_id(1) == 0) & (pl.program_id(2) == 0)
        rev = pl.program_id(2) == 1
        cin_r, cin_i = _s5_input_states(u_ref, b_ref, lr_ref, li_ref, s_scr, rev)

        def dstate(i, _):
            rs = pl.ds(pl.multiple_of(i * S5_RC, S5_RC), S5_RC)
            g_scr[rs, :] = lax.dot_general(dy_ref[rs, :].astype(BF16), c_ref[...], _NT, preferred_element_type=F32)
            return 0

        lax.fori_loop(0, L // S5_RC, dstate, 0)
        _scan_inplace(g_scr, lr_ref[...], -li_ref[...], jnp.logical_not(rev))

        def dlam(k, c):
            ar, ai = c
            tg = jnp.where(rev, k, k + 1)
            ts = jnp.where(rev, k + 1, k)
            rg = pl.ds(pl.multiple_of(tg * NSEG, NSEG), NSEG)
            rs = pl.ds(pl.multiple_of(ts * NSEG, NSEG), NSEG)
            gr, gi = g_scr[rg, 0:SBW], g_scr[rg, SBW:2 * SBW]
            sr, si = s_scr[rs, 0:SBW], s_scr[rs, SBW:2 * SBW]
            return ar + gr * sr + gi * si, ai + gi * sr - gr * si

        z = jnp.zeros((NSEG, SBW), F32)
        ar, ai = lax.fori_loop(0, TSEG - 1, dlam, (z, z))
        rb = pl.ds(pl.multiple_of(jnp.where(rev, TSEG - 1, 0) * NSEG, NSEG), NSEG)
        gr, gi = g_scr[rb, 0:SBW], g_scr[rb, SBW:2 * SBW]
        ar = ar + gr * cin_r + gi * cin_i
        ai = ai + gi * cin_r - gr * cin_i
        dlr_ref[...] = jnp.sum(ar, axis=0, keepdims=True)
        dli_ref[...] = jnp.sum(ai, axis=0, keepdims=True)

        db_ref[...] = jnp.zeros_like(db_ref)
        dc_ref[...] = jnp.zeros_like(dc_ref)

        @pl.when(first)
        def _():
            dd_ref[...] = jnp.zeros_like(dd_ref)

        def grads(i, _):
            rs = pl.ds(pl.multiple_of(i * S5_RC, S5_RC), S5_RC)
            uv = u_ref[rs, :]
            dyv = dy_ref[rs, :]
            gb = g_scr[rs, :].astype(BF16)
            db_ref[...] += lax.dot_general(uv.astype(BF16), gb, _TN, preferred_element_type=F32)
            dc_ref[...] += lax.dot_general(s_scr[rs, :].astype(BF16), dyv.astype(BF16), _TN,
                                           preferred_element_type=F32)
            duv = lax.dot_general(gb, b_ref[...], _NT, preferred_element_type=F32)

            @pl.when(first)
            def _():
                du_ref[rs, :] = d_ref[...] * dyv + duv
                dd_ref[...] += jnp.sum(dyv * uv, axis=0, keepdims=True)

            @pl.when(jnp.logical_not(first))
            def _():
                du_ref[rs, :] += duv

            return 0

        lax.fori_loop(0, L // S5_RC, grads, 0)

    u_spec, b_spec, c_spec, l_spec, d_spec = _s5_specs()
    scr = pltpu.VMEM((L, 2 * SBW), F32)
    return pl.pallas_call(
        body, name="s5_bwd", grid=(SW // 128, 2, 2),
        in_specs=[u_spec, u_spec, b_spec, c_spec, l_spec, l_spec, d_spec],
        out_specs=[u_spec, b_spec, c_spec, l_spec, l_spec, d_spec],
        out_shape=[jax.ShapeDtypeStruct((L, SW), F32),
                   jax.ShapeDtypeStruct((2, NSB, 128, 2 * SBW), F32), jax.ShapeDtypeStruct((2, NSB, 2 * SBW, 128), F32),
                   jax.ShapeDtypeStruct((2, NSB, 1, SBW), F32), jax.ShapeDtypeStruct((2, NSB, 1, SBW), F32),
                   jax.ShapeDtypeStruct((1, SW), F32)],
        scratch_shapes=[scr, scr],
        compiler_params=_cp(("parallel", "arbitrary", "arbitrary")),
    )(u_p, dy_p, bcat, ccat, lam_re, lam_im, dskip)


def _s5_params(a_re, a_im, log_step, bt_re, bt_im):
    lam = lax.complex(a_re, a_im)
    step = jnp.exp(log_step)[..., None]
    lam_bar = jnp.exp(lam * step)
    b_bar = ((lam_bar - 1.0) / lam)[..., None, :] * lax.complex(bt_re, bt_im)
    return jnp.real(lam_bar), jnp.imag(lam_bar), jnp.real(b_bar), jnp.imag(b_bar)


def _sel():
    i = jnp.arange(8)[None, :, None]
    j = jnp.arange(4)[None, None, :]
    r = jnp.arange(2)[:, None, None]
    return (i == r * 4 + j).astype(F32)


def _to_bcat(bt_re, bt_im):
    def one(bt):
        return jnp.einsum('dkrjcp,rij->dkricjp', bt.reshape(2, 4, 2, 4, GC, NP), _sel()).reshape(2, NSB, 128, SBW)
    return jnp.concatenate([one(bt_re), one(bt_im)], axis=-1)


def _from_bcat(dbcat):
    def one(dbbd):
        return jnp.einsum('dkricjp,rij->dkrjcp', dbbd.reshape(2, 4, 2, 8, GC, 4, NP), _sel()).reshape(2, NG, GC, NP)
    return one(dbcat[..., :SBW]), one(dbcat[..., SBW:])


def _to_ccat(c_re, c_im):
    def one(cc):
        return jnp.einsum('dkrjcp,rij->dkrjpic', cc.reshape(2, 4, 2, 4, GC, NP), _sel()).reshape(2, NSB, SBW, 128)
    return jnp.concatenate([one(c_re), -one(c_im)], axis=-2)


def _from_ccat(dccat):
    def one(dcbd):
        return jnp.einsum('dkrjpic,rij->dkrjcp', dcbd.reshape(2, 4, 2, 4, NP, 8, GC), _sel()).reshape(2, NG, GC, NP)
    return one(dccat[:, :, :SBW]), -one(dccat[:, :, SBW:])


def _gelu(y):
    return 0.5 * y * (1.0 + lax.erf(y * (2.0 ** -0.5)))


def _gelu_grad(y):
    return 0.5 * (1.0 + lax.erf(y * (2.0 ** -0.5))) + y * jnp.exp(-0.5 * y * y) * ((2.0 * math.pi) ** -0.5)


def _sigmoid(z):
    return 0.5 * jnp.tanh(0.5 * z) + 0.5


def _glu_fwd(y, wg):
    def body(y_ref, w_ref, o_ref, z_ref):
        ys = _gelu(y_ref[...])
        z = jnp.dot(ys.astype(BF16), w_ref[...], preferred_element_type=F32)
        z_ref[...] = z
        o_ref[...] = ys * _sigmoid(z)

    row = pl.BlockSpec((TL, SW), lambda i: (i, 0))
    return pl.pallas_call(
        body, name="glu_fwd", grid=(L // TL,),
        in_specs=[row, pl.BlockSpec((SW, SW), lambda i: (0, 0))], out_specs=[row, row],
        out_shape=[jax.ShapeDtypeStruct((L, SW), F32), jax.ShapeDtypeStruct((L, SW), F32)],
        compiler_params=_cp(("parallel",)),
    )(y, wg)


def _glu_bwd(y, z, dout, wg):
    def body(y_ref, z_ref, do_ref, w_ref, dy_ref, dw_ref):
        @pl.when(pl.program_id(0) == 0)
        def _():
            dw_ref[...] = jnp.zeros_like(dw_ref)

        yv = y_ref[...]
        ys = _gelu(yv)
        sg = _sigmoid(z_ref[...])
        dov = do_ref[...]
        dz = (dov * ys * sg * (1.0 - sg)).astype(BF16)
        dys = dov * sg + lax.dot_general(dz, w_ref[...], _NT, preferred_element_type=F32)
        dy_ref[...] = dys * _gelu_grad(yv)
        dw_ref[...] += lax.dot_general(ys.astype(BF16), dz, _TN, preferred_element_type=F32)

    row = pl.BlockSpec((TL, SW), lambda i: (i, 0))
    wsp = pl.BlockSpec((SW, SW), lambda i: (0, 0))
    return pl.pallas_call(
        body, name="glu_bwd", grid=(L // TL,),
        in_specs=[row, row, row, wsp], out_specs=[row, wsp],
        out_shape=[jax.ShapeDtypeStruct((L, SW), F32), jax.ShapeDtypeStruct((SW, SW), F32)],
        compiler_params=_cp(("arbitrary",)),
    )(y, z, dout, wg)


CT = 256
CR = 128
NCT = DFF // CT


def _shifted(ref, r):
    cur = ref[pl.ds(r, CR), :]
    before = ref[pl.ds(pl.multiple_of(jnp.maximum(r - 8, 0), 8), 8), :][7:8, :]
    after = ref[pl.ds(pl.multiple_of(jnp.minimum(r + CR, L - 8), 8), 8), :][0:1, :]
    before = jnp.where(r > 0, before, 0.0)
    after = jnp.where(r + CR < L, after, 0.0)
    row = lax.broadcasted_iota(jnp.int32, cur.shape, 0)
    prev = jnp.where(row == 0, before, pltpu.roll(cur, 1, 0))
    nxt = jnp.where(row == CR - 1, after, pltpu.roll(cur, CR - 1, 0))
    return prev, cur, nxt


def _conv3(ref, r, w_ref, b_ref):
    prev, cur, nxt = _shifted(ref, r)
    return w_ref[0:1, :] * prev + w_ref[1:2, :] * cur + w_ref[2:3, :] * nxt + b_ref[...]


def _convact_fwd(up, conv_w, conv_b):
    def body(ug_ref, uv_ref, wg_ref, wv_ref, bg_ref, bv_ref, o_ref):
        def chunk(i, _):
            r = pl.multiple_of(i * CR, CR)
            g = _conv3(ug_ref, r, wg_ref, bg_ref)
            v = _conv3(uv_ref, r, wv_ref, bv_ref)
            o_ref[pl.ds(r, CR), :] = (g * _sigmoid(g) * v).astype(BF16)
            return 0

        lax.fori_loop(0, L // CR, chunk, 0)

    gcol = pl.BlockSpec((L, CT), lambda j: (0, j))
    vcol = pl.BlockSpec((L, CT), lambda j: (0, j + NCT))
    return pl.pallas_call(
        body, name="convact_fwd", grid=(NCT,),
        in_specs=[gcol, vcol,
                  pl.BlockSpec((3, CT), lambda j: (0, j)), pl.BlockSpec((3, CT), lambda j: (0, j + NCT)),
                  pl.BlockSpec((1, CT), lambda j: (0, j)), pl.BlockSpec((1, CT), lambda j: (0, j + NCT))],
        out_specs=gcol, out_shape=jax.ShapeDtypeStruct((L, DFF), BF16),
        compiler_params=_cp(("parallel",)),
    )(up, up, conv_w, conv_w, conv_b, conv_b)


def _convact_bwd(up, dact, conv_w, conv_b):
    def body(ug_ref, uv_ref, da_ref, wg_ref, wv_ref, bg_ref, bv_ref, du_ref, dw_ref, db_ref, dgs, dvs, dwv, dbv):
        half = pl.program_id(1)

        def transpose_conv(src, w_ref):
            def chunk(i, _):
                r = pl.multiple_of(i * CR, CR)
                prev, cur, nxt = _shifted(src, r)
                du_ref[pl.ds(r, CR), :] = (w_ref[0:1, :] * nxt + w_ref[1:2, :] * cur
                                           + w_ref[2:3, :] * prev).astype(BF16)
                return 0

            lax.fori_loop(0, L // CR, chunk, 0)

        @pl.when(half == 0)
        def _():
            dw_ref[...] = jnp.zeros_like(dw_ref)
            db_ref[...] = jnp.zeros_like(db_ref)
            dwv[...] = jnp.zeros_like(dwv)
            dbv[...] = jnp.zeros_like(dbv)

            def chunk1(i, _):
                r = pl.multiple_of(i * CR, CR)
                rs = pl.ds(r, CR)
                pg, cg, ng = _shifted(ug_ref, r)
                pv, cv, nv = _shifted(uv_ref, r)
                g = wg_ref[0:1, :] * pg + wg_ref[1:2, :] * cg + wg_ref[2:3, :] * ng + bg_ref[...]
                v = wv_ref[0:1, :] * pv + wv_ref[1:2, :] * cv + wv_ref[2:3, :] * nv + bv_ref[...]
                sg = _sigmoid(g)
                da = da_ref[rs, :]
                dv = da * g * sg
                dg = da * v * sg * (1.0 + g * (1.0 - sg))
                dgs[rs, :] = dg
                dvs[rs, :] = dv
                for dref, dval, taps, bref in ((dw_ref, dg, (pg, cg, ng), db_ref), (dwv, dv, (pv, cv, nv), dbv)):
                    for k in range(3):
                        dref[k:k + 1, :] += jnp.sum(dval * taps[k], axis=0, keepdims=True)
                    bref[0:1, :] += jnp.sum(dval, axis=0, keepdims=True)
                return 0

            lax.fori_loop(0, L // CR, chunk1, 0)
            transpose_conv(dgs, wg_ref)

        @pl.when(half == 1)
        def _():
            dw_ref[...] = dwv[0:3, :]
            db_ref[...] = dbv[0:1, :]
            transpose_conv(dvs, wv_ref)

    def col(rows, off):
        return pl.BlockSpec((rows, CT), lambda j, h: (0, j + off))

    def out(rows):
        return pl.BlockSpec((rows, CT), lambda j, h: (0, j + h * NCT))

    return pl.pallas_call(
        body, name="convact_bwd", grid=(NCT, 2),
        in_specs=[col(L, 0), col(L, NCT), col(L, 0), col(3, 0), col(3, NCT), col(1, 0), col(1, NCT)],
        out_specs=[out(L), out(3), out(1)],
        out_shape=[jax.ShapeDtypeStruct((L, 2 * DFF), BF16), jax.ShapeDtypeStruct((3, 2 * DFF), F32),
                   jax.ShapeDtypeStruct((1, 2 * DFF), F32)],
        scratch_shapes=[pltpu.VMEM((L, CT), F32), pltpu.VMEM((L, CT), F32),
                        pltpu.VMEM((8, CT), F32), pltpu.VMEM((8, CT), F32)],
        compiler_params=_cp(("parallel", "arbitrary")),
    )(up, up, dact, conv_w, conv_w, conv_b, conv_b)


def _local_step(x, tgt, w_in_t, w_glu, w_out, w_up_t, conv_w, w_down, p):
    tabs = _rope_tables()
    lam_re, lam_im, bb_re, bb_im = _s5_params(p["a_re"], p["a_im"], p["log_step"], p["bt_re"], p["bt_im"])
    bcat = _to_bcat(bb_re, bb_im).astype(BF16)
    ccat = _to_ccat(p["c_re"], p["c_im"]).astype(BF16)
    lam_re4, lam_im4 = lam_re.reshape(2, NSB, 1, SBW), lam_im.reshape(2, NSB, 1, SBW)
    dskip = p["d_skip"].reshape(1, SW)
    g_mix, g_ffn, g_fin = p["norm_mix_g"].reshape(1, D), p["norm_ffn_g"].reshape(1, D), p["norm_final_g"].reshape(1, D)
    g_attn, g_ssm = p["norm_attn_g"].reshape(1, AW), p["norm_ssm_g"].reshape(1, SW)
    sink = p["sink"].reshape(NQ)
    conv_b = p["conv_b"].reshape(1, 2 * DFF)

    h1 = _rms_fwd(x, g_mix, "norm_mix_fwd")
    proj = _mm(h1, w_in_t, tb=True, name="in_proj", tn=1280)
    qkv = _rope_fwd(proj, tabs)
    attn, lse = _attn_fwd(qkv, sink)
    u_p = _perm(proj[:, AW + 2 * KVW:])
    y_p = _s5_fwd(u_p, bcat, ccat, lam_re4, lam_im4, dskip)
    ysg_p, z_p = _glu_fwd(y_p, w_glu)
    ysg = _unperm(ysg_p)
    mixed = _mixnorm_fwd(attn, ysg, g_attn, g_ssm)
    x1 = _mm(mixed, w_out, add=x, name="out_proj")
    h2 = _rms_fwd(x1, g_ffn, "norm_ffn_fwd")
    up = _mm(h2, w_up_t, tb=True, name="ffn_up", tn=1408)
    act = _convact_fwd(up, conv_w, conv_b)
    x2 = _mm(act, w_down, add=x1, name="ffn_down", tk=1408)
    loss, dx2, dg_fin = _final_loss(x2, tgt, g_fin)

    dact = _mm(dx2, w_down, tb=True, name="ffn_down_dx", tn=1408)
    dw_down = _mm(act, dx2, ta=True, name="ffn_down_dw", tm=1408)
    dup, dconv_w, dconv_b = _convact_bwd(up, dact, conv_w, conv_b)
    dw_up_t = _mm(dup, h2, ta=True, name="ffn_up_dw", tm=1408)
    dh2 = _mm(dup, w_up_t, name="ffn_up_dx", tk=1408)
    dx1, dg_ffn = _rms_bwd(x1, g_ffn, dh2, dx2, "norm_ffn_bwd")
    dmixed = _mm(dx1, w_out, tb=True, name="out_proj_dx")
    dw_out = _mm(mixed, dx1, ta=True, name="out_proj_dw")
    dattn, dysg, dg_attn, dg_ssm = _mixnorm_bwd(attn, ysg, g_attn, g_ssm, dmixed)
    dy_p, dw_glu = _glu_bwd(y_p, z_p, _perm(dysg), w_glu)
    du_p, dbcat, dccat, dlam_re, dlam_im, dd = _s5_bwd(u_p, dy_p, bcat, ccat, lam_re4, lam_im4, dskip)
    dbb_re, dbb_im = _from_bcat(dbcat)
    dc_re, dc_im = _from_ccat(dccat)
    dq, dk, dv, dsink = _attn_bwd(qkv, sink, attn, lse, dattn)
    dproj = _rope_bwd(dq, dk, dv, _unperm(du_p), tabs)
    dw_in_t = _mm(dproj, h1, ta=True, name="in_proj_dw", tm=1280)
    dh1 = _mm(dproj, w_in_t, name="in_proj_dx", tk=1280)
    grad_x, dg_mix = _rms_bwd(x, g_mix, dh1, dx1, "norm_mix_bwd")

    big = dict(w_in_t=dw_in_t, w_glu=dw_glu, w_out=dw_out, w_up_t=dw_up_t, w_down=dw_down)
    small = dict(norm_mix_g=dg_mix, norm_attn_g=dg_attn, norm_ssm_g=dg_ssm, norm_ffn_g=dg_ffn, norm_final_g=dg_fin,
                 sink=dsink[:, 0], conv_b=dconv_b, d_skip=dd, conv_w=dconv_w,
                 lam_re=dlam_re.reshape(2, NG, NP), lam_im=dlam_im.reshape(2, NG, NP),
                 bb_re=dbb_re, bb_im=dbb_im, c_re=dc_re, c_im=dc_im, loss=loss.reshape(1))
    return grad_x, big, small


ANY = pl.BlockSpec(memory_space=pl.ANY)


def _coords():
    return lax.axis_index("x"), lax.axis_index("y"), lax.axis_index("c")


def _flip(v, b):
    return v + b - 2 * v * b if b else v


def _all_gather(shards, name):
    n = len(shards)

    def body(*refs):
        ins, outs = refs[:n], refs[n:2 * n]
        send_sems, recv_sems, local_sems = refs[2 * n:]
        x, y, c = _coords()
        me, sibling = (x, y, c), (x, y, 1 - c)
        chips = [(1 - x, y), (x, 1 - y), (1 - x, 1 - y)]
        waits = []
        for a in range(n):
            r = ins[a].shape[0]

            def rows(px, py, pc, a=a, r=r):
                return outs[a].at[pl.ds(pl.multiple_of((4 * px + 2 * py + pc) * r, 8), r), :]

            def copy(k, block, to, src=None, a=a, rows=rows):
                return pltpu.make_async_remote_copy(
                    src_ref=rows(*block) if src is None else src, dst_ref=rows(*block),
                    send_sem=send_sems.at[a * 7 + k], recv_sem=recv_sems.at[a * 7 + k],
                    device_id=to, device_id_type=pl.DeviceIdType.MESH)

            mine = pltpu.make_async_copy(ins[a], rows(*me), local_sems.at[a])
            mine.start()
            first = [copy(0, me, sibling, src=ins[a])]
            first += [copy(1 + j, me, (*chip, c), src=ins[a]) for j, chip in enumerate(chips)]
            for cp in first:
                cp.start()
            waits.append((mine, first, copy))
        for a in range(n):
            mine, first, copy = waits[a]
            passed = [copy(4 + j, (*chip, c), sibling) for j, chip in enumerate(chips)]
            for j, chip in enumerate(chips):
                copy(1 + j, (*chip, c), me).wait_recv()
                passed[j].start()
            waits[a] = (mine, first, copy, passed)
        for a in range(n):
            mine, first, copy, passed = waits[a]
            copy(0, sibling, me).wait_recv()
            for j, chip in enumerate(chips):
                copy(4 + j, (*chip, 1 - c), me).wait_recv()
            for cp in first + passed:
                cp.wait_send()
            mine.wait()

    return pl.pallas_call(
        body, name=name,
        in_specs=[ANY] * n, out_specs=[ANY] * n,
        out_shape=[jax.ShapeDtypeStruct((NDEV * s.shape[0], s.shape[1]), s.dtype) for s in shards],
        scratch_shapes=[pltpu.SemaphoreType.DMA((7 * n,)), pltpu.SemaphoreType.DMA((7 * n,)),
                        pltpu.SemaphoreType.DMA((n,))],
    )(*shards)


NCHIP = 4
CHIP_FLIPS = ((1, 0), (0, 1), (1, 1))


def _swap_remote(ins, outs, send_sems, recv_sems, plan):
    copies = []
    for k, (a, src, dst, to) in enumerate(plan):
        cp = pltpu.make_async_remote_copy(
            src_ref=ins[a].at[src], dst_ref=outs[a].at[dst], send_sem=send_sems.at[k], recv_sem=recv_sems.at[k],
            device_id=to, device_id_type=pl.DeviceIdType.MESH)
        cp.start()
        copies.append(cp)
    for cp in copies:
        cp.wait_recv()
    for cp in copies:
        cp.wait_send()


def _exchange_cores(parts, name):
    n = len(parts)

    def body(*refs):
        ins, outs = refs[:n], refs[n:2 * n]
        x, y, c = _coords()
        plan = [(a, 2 * q + 1 - c, q, (x, y, 1 - c)) for a in range(n) for q in range(NCHIP)]
        _swap_remote(ins, outs, *refs[2 * n:], plan)

    return pl.pallas_call(
        body, name=name, in_specs=[ANY] * n, out_specs=[ANY] * n,
        out_shape=[jax.ShapeDtypeStruct((NCHIP,) + s.shape[1:], s.dtype) for s in parts],
        scratch_shapes=[pltpu.SemaphoreType.DMA((NCHIP * n,)), pltpu.SemaphoreType.DMA((NCHIP * n,))],
    )(*parts)


def _exchange_chips(parts, name):
    n = len(parts)

    def body(*refs):
        ins, outs = refs[:n], refs[n:2 * n]
        x, y, c = _coords()
        plan = []
        for a in range(n):
            for j, (fx, fy) in enumerate(CHIP_FLIPS):
                px, py = _flip(x, fx), _flip(y, fy)
                plan.append((a, 2 * px + py, j, (px, py, c)))
        _swap_remote(ins, outs, *refs[2 * n:], plan)

    return pl.pallas_call(
        body, name=name, in_specs=[ANY] * n, out_specs=[ANY] * n,
        out_shape=[jax.ShapeDtypeStruct((3,) + s.shape[1:], s.dtype) for s in parts],
        scratch_shapes=[pltpu.SemaphoreType.DMA((3 * n,)), pltpu.SemaphoreType.DMA((3 * n,))],
    )(*parts)


def _pair_sum(where, part, recv, wire_dtype, name):
    _, r, c = part.shape
    tr = _pick(r, 256, 16)

    def body(w_ref, p_ref, r_ref, pb_ref, own_ref):
        s = p_ref[...] + r_ref[...]
        pb_ref[...] = s.astype(wire_dtype)

        @pl.when(pl.program_id(1) == w_ref[1])
        def _():
            own_ref[...] = s

    return pl.pallas_call(
        body, name=name,
        grid_spec=pltpu.PrefetchScalarGridSpec(
            num_scalar_prefetch=1, grid=(r // tr, NCHIP),
            in_specs=[pl.BlockSpec((None, tr, c), lambda i, q, w: (2 * q + w[0], i, 0)),
                      pl.BlockSpec((None, tr, c), lambda i, q, w: (q, i, 0))],
            out_specs=[pl.BlockSpec((None, tr, c), lambda i, q, w: (q, i, 0)),
                       pl.BlockSpec((tr, c), lambda i, q, w: (i, 0))]),
        out_shape=[jax.ShapeDtypeStruct((NCHIP, r, c), wire_dtype), jax.ShapeDtypeStruct((r, c), F32)],
        compiler_params=_cp(("parallel", "arbitrary")),
    )(where, part, recv)


def _chip_sum(own, recv, name):
    r, c = own.shape
    tr = _pick(r, 256, 16)

    def body(o_ref, r_ref, out_ref):
        acc = o_ref[...]
        for j in range(3):
            acc = acc + r_ref[j].astype(F32)
        out_ref[...] = acc

    return pl.pallas_call(
        body, name=name, grid=(r // tr,),
        in_specs=[pl.BlockSpec((tr, c), lambda i: (i, 0)), pl.BlockSpec((3, tr, c), lambda i: (0, i, 0))],
        out_specs=pl.BlockSpec((tr, c), lambda i: (i, 0)),
        out_shape=jax.ShapeDtypeStruct((r, c), F32),
        compiler_params=_cp(("parallel",)),
    )(own, recv)


def _adamw(w, g, m, v, name):
    r, c = w.shape
    tr = _pick(r, 256, 8)

    def body(w_ref, g_ref, m_ref, v_ref, d_ref, nm_ref, nv_ref):
        _adamw_refs(w_ref, g_ref, m_ref, v_ref, d_ref, nm_ref, nv_ref)

    blk = pl.BlockSpec((tr, c), lambda i: (i, 0))
    return pl.pallas_call(
        body, name=name, grid=(r // tr,),
        in_specs=[blk] * 4, out_specs=[blk] * 3,
        out_shape=[jax.ShapeDtypeStruct((r, c), F32)] * 3,
        compiler_params=_cp(("parallel",)),
    )(w, g, m, v)


def _adamw_refs(w_ref, g_ref, m_ref, v_ref, d_ref, nm_ref, nv_ref):
    gv = g_ref[...]
    nm = B1 * m_ref[...] + (1.0 - B1) * gv
    nv = B2 * v_ref[...] + (1.0 - B2) * (gv * gv)
    nm_ref[...] = nm
    nv_ref[...] = nv
    d_ref[...] = -LR * ((nm / C1) / (jnp.sqrt(nv / C2) + AEPS) + WD * w_ref[...])


def _adamw_small(ws, gs, ms, vs, name):
    n = len(ws)

    def body(*refs):
        groups = [refs[i * n:(i + 1) * n] for i in range(7)]
        for per_param in zip(*groups):
            _adamw_refs(*per_param)

    vm = pl.BlockSpec(memory_space=pltpu.VMEM)
    outs = pl.pallas_call(
        body, name=name, in_specs=[vm] * (4 * n), out_specs=[vm] * (3 * n),
        out_shape=[jax.ShapeDtypeStruct(a.shape, F32) for a in ws] * 3,
    )(*ws, *gs, *ms, *vs)
    return outs[:n], outs[n:2 * n], outs[2 * n:]


def _swap(a):
    return jnp.swapaxes(a, -1, -2)


VIEWS = {
    "w_in": (lambda a: a[0].T, lambda u: u.T[None]),
    "w_up": (lambda a: a[0].T, lambda u: u.T[None]),
    "w_glu": (lambda a: a[0], lambda u: u[None]),
    "w_out": (lambda a: a[0], lambda u: u[None]),
    "w_down": (lambda a: a[0], lambda u: u[None]),
    "conv_w": (lambda a: a[0], lambda u: u[None]),
    "norm_mix_g": (lambda a: a, lambda u: u),
    "norm_attn_g": (lambda a: a, lambda u: u),
    "norm_ssm_g": (lambda a: a, lambda u: u),
    "norm_ffn_g": (lambda a: a, lambda u: u),
    "norm_final_g": (lambda a: a[None], lambda u: u[0]),
    "conv_b": (lambda a: a, lambda u: u),
    "sink": (lambda a: a, lambda u: u),
    "a_re": (lambda a: a.reshape(2 * NG, NP), lambda u: u.reshape(1, 2, NG, NP)),
    "a_im": (lambda a: a.reshape(2 * NG, NP), lambda u: u.reshape(1, 2, NG, NP)),
    "log_step": (lambda a: a[0], lambda u: u[None]),
    "b_re": (lambda a: _swap(a[0]).reshape(2 * NG * GC, NP), lambda u: _swap(u.reshape(2, NG, GC, NP))[None]),
    "b_im": (lambda a: _swap(a[0]).reshape(2 * NG * GC, NP), lambda u: _swap(u.reshape(2, NG, GC, NP))[None]),
    "c_re": (lambda a: a.reshape(2 * NG * GC, NP), lambda u: u.reshape(1, 2, NG, GC, NP)),
    "c_im": (lambda a: a.reshape(2 * NG * GC, NP), lambda u: u.reshape(1, 2, NG, GC, NP)),
    "d_skip": (lambda a: a[0].T, lambda u: u.T[None]),
}
BIG = ["w_in", "w_glu", "w_out", "w_up", "w_down"]
PACK_W = 1024


def _pack(arrs, rows):
    flat = jnp.concatenate([a.reshape(-1).astype(F32) for a in arrs])
    return jnp.pad(flat, (0, rows * PACK_W - flat.shape[0])).reshape(rows, PACK_W)


def _unpack(packed, shapes):
    flat = packed.reshape(-1)
    out, off = [], 0
    for s in shapes:
        size = math.prod(s)
        out.append(flat[off:off + size].reshape(s))
        off += size
    return out


def kernel(x, norm_mix_g, w_in, a_re, a_im, log_step, b_re, b_im, c_re, c_im, d_skip, w_glu, sink, norm_attn_g, norm_ssm_g, w_out, norm_ffn_g, w_up, conv_w, conv_b, w_down, norm_final_g, loss_target, m_norm_mix_g, m_w_in, m_a_re, m_a_im, m_log_step, m_b_re, m_b_im, m_c_re, m_c_im, m_d_skip, m_w_glu, m_sink, m_norm_attn_g, m_norm_ssm_g, m_w_out, m_norm_ffn_g, m_w_up, m_conv_w, m_conv_b, m_w_down, m_norm_final_g, v_norm_mix_g, v_w_in, v_a_re, v_a_im, v_log_step, v_b_re, v_b_im, v_c_re, v_c_im, v_d_skip, v_w_glu, v_sink, v_norm_attn_g, v_norm_ssm_g, v_w_out, v_norm_ffn_g, v_w_up, v_conv_w, v_conv_b, v_w_down, v_norm_final_g):
    args = dict(locals())
    names = ["norm_mix_g", "w_in", "a_re", "a_im", "log_step", "b_re", "b_im", "c_re", "c_im", "d_skip", "w_glu",
             "sink", "norm_attn_g", "norm_ssm_g", "w_out", "norm_ffn_g", "w_up", "conv_w", "conv_b", "w_down",
             "norm_final_g"]
    w = {k: args[k] for k in names}
    m = {k: args["m_" + k] for k in names}
    v = {k: args["v_" + k] for k in names}

    shards = [w_in[0].T.astype(BF16), w_glu[0].astype(BF16), w_out[0].astype(BF16), w_up[0].T.astype(BF16),
              w_down[0].astype(BF16), jnp.pad(conv_w[0], ((0, 5), (0, 0)))]
    w_in_t, w_glu_f, w_out_f, w_up_t, w_down_f, conv_w_g = _all_gather(shards, "gather_weights")
    conv_w_f = conv_w_g.reshape(NDEV, 8, 2 * DFF // NDEV)[:, :3].transpose(1, 0, 2).reshape(3, 2 * DFF)

    p = {k: w[k][0] for k in ("norm_mix_g", "a_re", "a_im", "log_step", "c_re", "c_im", "d_skip", "sink",
                              "norm_attn_g", "norm_ssm_g", "norm_ffn_g", "conv_b")}
    p["norm_final_g"] = norm_final_g
    p["bt_re"], p["bt_im"] = _swap(b_re[0]), _swap(b_im[0])
    grad_x, big, small = _local_step(x[0], loss_target[0], w_in_t, w_glu_f, w_out_f, w_up_t, conv_w_f, w_down_f, p)

    small_names = list(small.keys())
    small_shapes = [small[k].shape for k in small_names]
    n_small = sum(math.prod(s) for s in small_shapes)
    rows_dev = -(-n_small // (PACK_W * NDEV * 16)) * 16
    spack = _pack([small[k] for k in small_names], rows_dev * NDEV)
    big_names = ["w_in_t", "w_glu", "w_out", "w_up_t", "w_down"]
    all_names = big_names + ["small"]
    parts = [big[k].reshape(NDEV, big[k].shape[0] // NDEV, big[k].shape[1]) for k in big_names]
    parts.append(spack.reshape(NDEV, rows_dev, PACK_W))
    ax, ay, ac = _coords()
    me = 4 * ax + 2 * ay + ac
    where = jnp.stack([ac, 2 * ax + ay]).astype(jnp.int32)
    from_core = _exchange_cores(parts, "exchange_cores")
    sums = [_pair_sum(where, p_, r_, F32 if k == "small" else BF16, "pair_sum_" + k)
            for k, p_, r_ in zip(all_names, parts, from_core)]
    from_chips = _exchange_chips([s[0] for s in sums], "exchange_chips")
    red = {k: _chip_sum(s[1], r_, "chip_sum_" + k) for k, s, r_ in zip(all_names, sums, from_chips)}
    (small_full,) = _all_gather([red["small"]], "gather_small")
    sm = dict(zip(small_names, _unpack(small_full, small_shapes)))

    _, s5_vjp = jax.vjp(_s5_params, a_re[0], a_im[0], log_step[0], p["bt_re"], p["bt_im"])
    da_re, da_im, dlog_step, dbt_re, dbt_im = s5_vjp((sm["lam_re"], sm["lam_im"], sm["bb_re"], sm["bb_im"]))
    gview = {
        "norm_mix_g": sm["norm_mix_g"], "norm_attn_g": sm["norm_attn_g"], "norm_ssm_g": sm["norm_ssm_g"],
        "norm_ffn_g": sm["norm_ffn_g"], "norm_final_g": sm["norm_final_g"], "conv_b": sm["conv_b"],
        "sink": sm["sink"][None], "a_re": da_re.reshape(2 * NG, NP), "a_im": da_im.reshape(2 * NG, NP),
        "log_step": dlog_step, "b_re": dbt_re.reshape(2 * NG * GC, NP), "b_im": dbt_im.reshape(2 * NG * GC, NP),
        "c_re": sm["c_re"].reshape(2 * NG * GC, NP), "c_im": sm["c_im"].reshape(2 * NG * GC, NP),
        "d_skip": sm["d_skip"].reshape(NG, GC).T,
        "w_in": red["w_in_t"], "w_glu": red["w_glu"], "w_out": red["w_out"], "w_up": red["w_up_t"],
        "w_down": red["w_down"],
        "conv_w": lax.dynamic_slice_in_dim(sm["conv_w"], me * (2 * DFF // NDEV), 2 * DFF // NDEV, axis=1),
    }

    dview, mview, vview = {}, {}, {}
    for k in BIG:
        to = VIEWS[k][0]
        dview[k], mview[k], vview[k] = _adamw(to(w[k]), gview[k], to(m[k]), to(v[k]), "adamw_" + k)
    rest = [k for k in names if k not in BIG]
    outs = _adamw_small([VIEWS[k][0](w[k]) for k in rest], [gview[k] for k in rest],
                        [VIEWS[k][0](m[k]) for k in rest], [VIEWS[k][0](v[k]) for k in rest], "adamw_small")
    for dst, vals in zip((dview, mview, vview), outs):
        dst.update(dict(zip(rest, vals)))

    def back(views):
        return [VIEWS[k][1](views[k]) for k in names]

    return (sm["loss"][0], grad_x[None], *back(gview), *back(dview), *back(mview), *back(vview))
```

```python
import functools
import math

import jax
import jax.numpy as jnp
from jax import lax
from jax.experimental import pallas as pl
from jax.experimental.pallas import tpu as pltpu

F32 = jnp.float32
BF16 = jnp.bfloat16

L = 4096
D = 1024
NQ, NKV, HD = 8, 2, 64
AW = NQ * HD
KVW = NKV * HD
SW = 512
NG, GC, NP = 32, 16, 64
INW = AW + 2 * KVW + SW
DFF = 2816
BLK = 128
WIN = 3 * BLK
EPS = 1e-6
ROPE_THETA = 500000.0
NSEG = 32
TSEG = L // NSEG
SBW = 256
NSB = NG * NP // SBW
NDEV = 8
MESH_AXES = ("x", "y", "c")

LR, B1, B2, AEPS, WD, STEP = 0.001, 0.9, 0.999, 1e-08, 0.01, 10
C1 = 1.0 - B1 ** STEP
C2 = 1.0 - B2 ** STEP

VMEM_LIMIT = 56 * 1024 * 1024


def _pick(n, target, mult):
    best = None
    for t in range(mult, min(n, target) + 1, mult):
        if n % t == 0:
            best = t
    return best if best is not None else n


def _cp(sem):
    return pltpu.CompilerParams(dimension_semantics=sem, vmem_limit_bytes=VMEM_LIMIT)


def _mm(a, b, *, ta=False, tb=False, out_dtype=F32, add=None, ride=(), name, tm=1024, tn=1024, tk=1024):
    m, k = (a.shape[1], a.shape[0]) if ta else a.shape
    n = b.shape[0] if tb else b.shape[1]
    assert k == (b.shape[1] if tb else b.shape[0])
    tm, tn, tk = _pick(m, tm, 128), _pick(n, tn, 128), _pick(k, tk, 128)
    grid = (m // tm, n // tn, k // tk)
    nk = grid[2]
    dn = (((0 if ta else 1,), (1 if tb else 0,)), ((), ()))
    n_in = 2 + (add is not None)
    nr = len(ride)

    def body(*refs):
        a_ref, b_ref = refs[0], refs[1]
        o_ref = refs[n_in + nr]
        acc_ref = refs[n_in + 2 * nr + 1]
        step = [pl.program_id(d) for d in range(3)]
        kk = step[2]
        if nr:
            riders = (refs[n_in:n_in + nr], refs[n_in + nr + 1:n_in + 2 * nr + 1], *refs[n_in + 2 * nr + 2:])

            @pl.when((step[0] == 0) & (step[1] == 0) & (kk == 0))
            def _():
                _start_all(_chips_copies(*riders))

        prod = lax.dot_general(a_ref[...].astype(BF16), b_ref[...].astype(BF16), dn, preferred_element_type=F32)

        def finish(r):
            if add is not None:
                r = r + refs[2][...]
            o_ref[...] = r.astype(out_dtype)

        if nk == 1:
            finish(prod)
        else:
            @pl.when(kk == 0)
            def _():
                acc_ref[...] = prod

            @pl.when((kk > 0) & (kk < nk - 1))
            def _():
                acc_ref[...] += prod

            @pl.when(kk == nk - 1)
            def _():
                finish(acc_ref[...] + prod)

        if nr:
            @pl.when((step[0] == grid[0] - 1) & (step[1] == grid[1] - 1) & (kk == nk - 1))
            def _():
                _wait_all(_chips_copies(*riders))

    a_spec = pl.BlockSpec((tk, tm), lambda i, j, kk: (kk, i)) if ta else pl.BlockSpec((tm, tk), lambda i, j, kk: (i, kk))
    b_spec = pl.BlockSpec((tn, tk), lambda i, j, kk: (j, kk)) if tb else pl.BlockSpec((tk, tn), lambda i, j, kk: (kk, j))
    in_specs = [a_spec, b_spec]
    args = [a, b]
    if add is not None:
        in_specs.append(pl.BlockSpec((tm, tn), lambda i, j, kk: (i, j)))
        args.append(add)
    outs = pl.pallas_call(
        body, name=name, grid=grid,
        in_specs=in_specs + [ANY] * nr,
        out_specs=[pl.BlockSpec((tm, tn), lambda i, j, kk: (i, j))] + [ANY] * nr,
        out_shape=[jax.ShapeDtypeStruct((m, n), out_dtype)] + _chips_shapes(ride),
        scratch_shapes=[pltpu.VMEM((tm, tn) if nk > 1 else (8, 128), F32)] + (_chips_sems(nr) if nr else []),
        compiler_params=_cp(("arbitrary",) * 3 if nr else ("parallel", "parallel", "arbitrary")),
    )(*args, *ride)
    return (outs[0], list(outs[1:])) if nr else outs[0]


TL = 512


def _rms_fwd(x, g, name):
    d = x.shape[1]

    def body(x_ref, g_ref, o_ref):
        xv = x_ref[...]
        r = lax.rsqrt(jnp.mean(xv * xv, axis=-1, keepdims=True) + EPS)
        o_ref[...] = (xv * r * g_ref[...]).astype(BF16)

    return pl.pallas_call(
        body, name=name, grid=(L // TL,),
        in_specs=[pl.BlockSpec((TL, d), lambda i: (i, 0)), pl.BlockSpec((1, d), lambda i: (0, 0))],
        out_specs=pl.BlockSpec((TL, d), lambda i: (i, 0)),
        out_shape=jax.ShapeDtypeStruct((L, d), BF16),
        compiler_params=_cp(("parallel",)),
    )(x, g)


def _rms_bwd_tile(xv, gv, dh):
    r = lax.rsqrt(jnp.mean(xv * xv, axis=-1, keepdims=True) + EPS)
    a = dh * gv
    dx = r * a - xv * (r * r * r) * jnp.mean(a * xv, axis=-1, keepdims=True)
    dg = jnp.sum(dh * xv * r, axis=0, keepdims=True)
    return dx, dg


def _rms_bwd(x, g, dh, dres, name):
    d = x.shape[1]

    def body(x_ref, g_ref, dh_ref, dres_ref, dx_ref, dg_ref):
        dx, dg = _rms_bwd_tile(x_ref[...], g_ref[...], dh_ref[...])
        dx_ref[...] = dx + dres_ref[...]

        @pl.when(pl.program_id(0) == 0)
        def _():
            dg_ref[...] = jnp.zeros_like(dg_ref)

        dg_ref[...] += dg

    row = pl.BlockSpec((TL, d), lambda i: (i, 0))
    vec = pl.BlockSpec((1, d), lambda i: (0, 0))
    return pl.pallas_call(
        body, name=name, grid=(L // TL,),
        in_specs=[row, vec, row, row], out_specs=[row, vec],
        out_shape=[jax.ShapeDtypeStruct((L, d), F32), jax.ShapeDtypeStruct((1, d), F32)],
        compiler_params=_cp(("arbitrary",)),
    )(x, g, dh, dres)


def _mixnorm_fwd(attn, ysg, ga, gs):
    def body(a_ref, s_ref, ga_ref, gs_ref, o_ref):
        for src, gr, lo in ((a_ref, ga_ref, 0), (s_ref, gs_ref, AW)):
            xv = src[...]
            r = lax.rsqrt(jnp.mean(xv * xv, axis=-1, keepdims=True) + EPS)
            o_ref[:, lo:lo + 512] = (xv * r * gr[...]).astype(BF16)

    row = pl.BlockSpec((TL, 512), lambda i: (i, 0))
    vec = pl.BlockSpec((1, 512), lambda i: (0, 0))
    return pl.pallas_call(
        body, name="mixnorm_fwd", grid=(L // TL,),
        in_specs=[row, row, vec, vec], out_specs=pl.BlockSpec((TL, 1024), lambda i: (i, 0)),
        out_shape=jax.ShapeDtypeStruct((L, 1024), BF16),
        compiler_params=_cp(("parallel",)),
    )(attn, ysg, ga, gs)


def _mixnorm_bwd(attn, ysg, ga, gs, dmixed):
    def body(a_ref, s_ref, ga_ref, gs_ref, dm_ref, da_ref, ds_ref, dga_ref, dgs_ref):
        @pl.when(pl.program_id(0) == 0)
        def _():
            dga_ref[...] = jnp.zeros_like(dga_ref)
            dgs_ref[...] = jnp.zeros_like(dgs_ref)

        dxa, dga = _rms_bwd_tile(a_ref[...], ga_ref[...], dm_ref[:, 0:AW])
        da_ref[...] = dxa
        dga_ref[...] += dga
        dxs, dgs = _rms_bwd_tile(s_ref[...], gs_ref[...], dm_ref[:, AW:AW + SW])
        ds_ref[...] = dxs
        dgs_ref[...] += dgs

    row = pl.BlockSpec((TL, 512), lambda i: (i, 0))
    vec = pl.BlockSpec((1, 512), lambda i: (0, 0))
    return pl.pallas_call(
        body, name="mixnorm_bwd", grid=(L // TL,),
        in_specs=[row, row, vec, vec, pl.BlockSpec((TL, 1024), lambda i: (i, 0))],
        out_specs=[row, row, vec, vec],
        out_shape=[jax.ShapeDtypeStruct((L, 512), F32), jax.ShapeDtypeStruct((L, 512), F32),
                   jax.ShapeDtypeStruct((1, 512), F32), jax.ShapeDtypeStruct((1, 512), F32)],
        compiler_params=_cp(("arbitrary",)),
    )(attn, ysg, ga, gs, dmixed)


def _final_loss(x2, tgt, g):
    def body(x_ref, t_ref, g_ref, loss_ref, dx_ref, dg_ref):
        @pl.when(pl.program_id(0) == 0)
        def _():
            loss_ref[...] = jnp.zeros_like(loss_ref)
            dg_ref[...] = jnp.zeros_like(dg_ref)

        xv = x_ref[...]
        gv = g_ref[...]
        r = lax.rsqrt(jnp.mean(xv * xv, axis=-1, keepdims=True) + EPS)
        e = xv * r * gv - t_ref[...]
        loss_ref[...] += 0.5 * jnp.sum(jnp.mean(e * e, axis=-1, keepdims=True), axis=0, keepdims=True)
        dy = e * (1.0 / D)
        a = dy * gv
        dx_ref[...] = r * a - xv * (r * r * r) * jnp.mean(a * xv, axis=-1, keepdims=True)
        dg_ref[...] += jnp.sum(dy * xv * r, axis=0, keepdims=True)

    row = pl.BlockSpec((TL, D), lambda i: (i, 0))
    vec = pl.BlockSpec((1, D), lambda i: (0, 0))
    return pl.pallas_call(
        body, name="final_loss", grid=(L // TL,),
        in_specs=[row, row, vec],
        out_specs=[pl.BlockSpec((1, 1), lambda i: (0, 0)), row, vec],
        out_shape=[jax.ShapeDtypeStruct((1, 1), F32), jax.ShapeDtypeStruct((L, D), F32),
                   jax.ShapeDtypeStruct((1, D), F32)],
        compiler_params=_cp(("arbitrary",)),
    )(x2, tgt, g)


def _rope_tables():
    half = HD // 8
    inv_freq = jnp.power(ROPE_THETA, -jnp.arange(half, dtype=F32) / half)
    ang = jnp.arange(L, dtype=F32)[:, None] * inv_freq[None, :]
    cos, sin = jnp.cos(ang), jnp.sin(ang)
    one = jnp.ones((L, HD - 2 * half), F32)
    zero = jnp.zeros((L, HD - 2 * half), F32)
    zh = jnp.zeros((L, half), F32)
    cos64 = jnp.concatenate([cos, cos, one], axis=1)
    sa64 = jnp.concatenate([-sin, zh, zero], axis=1)
    sb64 = jnp.concatenate([zh, sin, zero], axis=1)
    return [jnp.tile(t, (1, 2)) for t in (cos64, sa64, sb64)]


def _rope(xv, cosv, sav, sbv, sign):
    return xv * cosv + sign * (pltpu.roll(xv, 120, 1) * sav + pltpu.roll(xv, 8, 1) * sbv)


def _rope_fwd(proj, tabs):
    qkw = AW + 2 * KVW

    def body(p_ref, c_ref, sa_ref, sb_ref, o_ref):
        for j in range(qkw // 128):
            cols = slice(j * 128, (j + 1) * 128)
            xv = p_ref[:, cols]
            if j < (AW + KVW) // 128:
                xv = _rope(xv, c_ref[...], sa_ref[...], sb_ref[...], 1.0)
            o_ref[:, cols] = xv.astype(BF16)

    blk = pl.BlockSpec((TL, qkw), lambda i: (i, 0))
    tab = pl.BlockSpec((TL, 128), lambda i: (i, 0))
    return pl.pallas_call(
        body, name="rope_fwd", grid=(L // TL,),
        in_specs=[blk, tab, tab, tab], out_specs=blk,
        out_shape=jax.ShapeDtypeStruct((L, qkw), BF16),
        compiler_params=_cp(("parallel",)),
    )(proj, *tabs)


def _rope_bwd(dq, dk, dv, du, tabs):
    def body(dq_ref, dk_ref, dv_ref, du_ref, c_ref, sa_ref, sb_ref, o_ref):
        for j in range(AW // 128):
            cols = slice(j * 128, (j + 1) * 128)
            o_ref[:, cols] = _rope(dq_ref[:, cols], c_ref[...], sa_ref[...], sb_ref[...], -1.0).astype(BF16)
        o_ref[:, AW:AW + KVW] = _rope(dk_ref[...], c_ref[...], sa_ref[...], sb_ref[...], -1.0).astype(BF16)
        o_ref[:, AW + KVW:AW + 2 * KVW] = dv_ref[...].astype(BF16)
        o_ref[:, AW + 2 * KVW:] = du_ref[...].astype(BF16)

    def row(width):
        return pl.BlockSpec((TL, width), lambda i: (i, 0))

    return pl.pallas_call(
        body, name="rope_bwd", grid=(L // TL,),
        in_specs=[row(AW), row(KVW), row(KVW), row(SW), row(128), row(128), row(128)],
        out_specs=row(INW), out_shape=jax.ShapeDtypeStruct((L, INW), BF16),
        compiler_params=_cp(("parallel",)),
    )(dq, dk, dv, du, *tabs)


def _attn_window(n):
    start = pl.multiple_of(jnp.clip((n - 1) * BLK, 0, L - WIN), BLK)
    qpos = n * BLK + lax.broadcasted_iota(jnp.int32, (BLK, WIN), 0)
    kpos = start + lax.broadcasted_iota(jnp.int32, (BLK, WIN), 1)
    return start, jnp.abs(kpos - qpos) <= BLK


_NT = (((1,), (1,)), ((), ()))
_TN = (((0,), (0,)), ((), ()))
NEG = -1e30


def _attn_fwd(qkv, sink, gather=()):
    ng = len(gather)

    def body(sink_ref, q_ref, k_ref, v_ref, *rest):
        o_ref, lse_ref = rest[ng], rest[ng + 1]
        n = pl.program_id(0)
        if ng:
            travellers = (rest[:ng], rest[ng + 2:2 * ng + 2], *rest[2 * ng + 2:])

            @pl.when(n == 0)
            def _():
                _gather_start(*travellers)

            @pl.when(n == L // BLK - 1)
            def _():
                _gather_finish(*travellers)

        start, valid = _attn_window(n)
        kw = k_ref[pl.ds(start, WIN), :]
        vw = v_ref[pl.ds(start, WIN), :]
        for h in range(NQ):
            kv = h // (NQ // NKV)
            qh = q_ref[:, h * HD:(h + 1) * HD]
            kh = kw[:, kv * HD:(kv + 1) * HD]
            vh = vw[:, kv * HD:(kv + 1) * HD]
            s = lax.dot_general(qh, kh, _NT, preferred_element_type=F32) * (HD ** -0.5)
            s = jnp.where(valid, s, NEG)
            sk = sink_ref[h]
            m = jnp.maximum(jnp.max(s, axis=-1, keepdims=True), sk)
            p = jnp.exp(s - m)
            den = jnp.sum(p, axis=-1, keepdims=True) + jnp.exp(sk - m)
            o_ref[:, h * HD:(h + 1) * HD] = jnp.dot((p / den).astype(BF16), vh, preferred_element_type=F32)
            lse_ref[:, h:h + 1] = m + jnp.log(den)

    outs = pl.pallas_call(
        body, name="attn_fwd", grid=(L // BLK,),
        in_specs=[pl.BlockSpec(memory_space=pltpu.SMEM),
                  pl.BlockSpec((BLK, AW), lambda n: (n, 0)),
                  pl.BlockSpec((L, KVW), lambda n: (0, AW // KVW)),
                  pl.BlockSpec((L, KVW), lambda n: (0, AW // KVW + 1))] + [ANY] * ng,
        out_specs=[pl.BlockSpec((BLK, AW), lambda n: (n, 0)), pl.BlockSpec((BLK, NQ), lambda n: (n, 0))] + [ANY] * ng,
        out_shape=[jax.ShapeDtypeStruct((L, AW), F32), jax.ShapeDtypeStruct((L, NQ), F32)] + _gather_shapes(gather),
        scratch_shapes=_gather_sems(ng) if ng else [],
        compiler_params=_cp(("arbitrary",) if ng else ("parallel",)),
    )(sink, qkv, qkv, qkv, *gather)
    return outs[0], outs[1], list(outs[2:])


def _attn_bwd(qkv, sink, attn, lse, dattn):
    def body(sink_ref, q_ref, k_ref, v_ref, o_ref, lse_ref, do_ref, dq_ref, dk_ref, dv_ref, dsink_ref):
        n = pl.program_id(0)

        @pl.when(n == 0)
        def _():
            dk_ref[...] = jnp.zeros_like(dk_ref)
            dv_ref[...] = jnp.zeros_like(dv_ref)
            dsink_ref[...] = jnp.zeros_like(dsink_ref)

        start, valid = _attn_window(n)
        kw = k_ref[pl.ds(start, WIN), :]
        vw = v_ref[pl.ds(start, WIN), :]
        for kv in range(NKV):
            kh = kw[:, kv * HD:(kv + 1) * HD]
            vh = vw[:, kv * HD:(kv + 1) * HD]
            dk_acc = jnp.zeros((WIN, HD), F32)
            dv_acc = jnp.zeros((WIN, HD), F32)
            for h in range(kv * (NQ // NKV), (kv + 1) * (NQ // NKV)):
                qh = q_ref[:, h * HD:(h + 1) * HD]
                doh = do_ref[:, h * HD:(h + 1) * HD]
                dd = jnp.sum(doh * o_ref[:, h * HD:(h + 1) * HD], axis=-1, keepdims=True)
                lse_h = lse_ref[:, h:h + 1]
                s = lax.dot_general(qh, kh, _NT, preferred_element_type=F32) * (HD ** -0.5)
                p = jnp.where(valid, jnp.exp(s - lse_h), 0.0)
                dob = doh.astype(BF16)
                dp = lax.dot_general(dob, vh, _NT, preferred_element_type=F32)
                ds = (p * (dp - dd) * (HD ** -0.5)).astype(BF16)
                dq_ref[:, h * HD:(h + 1) * HD] = jnp.dot(ds, kh, preferred_element_type=F32)
                dk_acc += lax.dot_general(ds, qh, _TN, preferred_element_type=F32)
                dv_acc += lax.dot_general(p.astype(BF16), dob, _TN, preferred_element_type=F32)
                psink = jnp.exp(sink_ref[h] - lse_h)
                dsk = -jnp.sum(psink * dd, axis=0, keepdims=True)
                dsink_ref[h:h + 1, :] += jnp.broadcast_to(dsk, (1, 128))
            dk_ref[pl.ds(start, WIN), kv * HD:(kv + 1) * HD] += dk_acc
            dv_ref[pl.ds(start, WIN), kv * HD:(kv + 1) * HD] += dv_acc

    qblk = pl.BlockSpec((BLK, AW), lambda n: (n, 0))
    full = pl.BlockSpec((L, KVW), lambda n: (0, 0))
    return pl.pallas_call(
        body, name="attn_bwd", grid=(L // BLK,),
        in_specs=[pl.BlockSpec(memory_space=pltpu.SMEM), qblk,
                  pl.BlockSpec((L, KVW), lambda n: (0, AW // KVW)),
                  pl.BlockSpec((L, KVW), lambda n: (0, AW // KVW + 1)),
                  qblk, pl.BlockSpec((BLK, NQ), lambda n: (n, 0)), qblk],
        out_specs=[qblk, full, full, pl.BlockSpec((NQ, 128), lambda n: (0, 0))],
        out_shape=[jax.ShapeDtypeStruct((L, AW), F32), jax.ShapeDtypeStruct((L, KVW), F32),
                   jax.ShapeDtypeStruct((L, KVW), F32), jax.ShapeDtypeStruct((NQ, 128), F32)],
        compiler_params=_cp(("arbitrary",)),
    )(sink, qkv, qkv, qkv, attn, lse, dattn)


def _perm(a):
    return a.reshape(NSEG, TSEG, a.shape[1]).transpose(1, 0, 2).reshape(L, a.shape[1])


def _unperm(a):
    return a.reshape(TSEG, NSEG, a.shape[1]).transpose(1, 0, 2).reshape(L, a.shape[1])


def _cmul(ar, ai, br, bi):
    return ar * br - ai * bi, ar * bi + ai * br


def _scan_inplace(s_ref, lr, li, rev):
    n = lr.shape[1]
    lr8 = jnp.broadcast_to(lr, (NSEG, n))
    li8 = jnp.broadcast_to(li, (NSEG, n))

    def rows(k):
        return pl.ds(pl.multiple_of(jnp.where(rev, TSEG - 1 - k, k) * NSEG, NSEG), NSEG)

    def step(k, c, store):
        sr, si = c
        rs = rows(k)
        pr, pi = _cmul(lr8, li8, sr, si)
        nr = pr + s_ref[rs, 0:n]
        ni = pi + s_ref[rs, n:2 * n]
        if store:
            s_ref[rs, 0:n] = nr
            s_ref[rs, n:2 * n] = ni
        return nr, ni

    z = jnp.zeros((NSEG, n), F32)
    er, ei = lax.fori_loop(0, TSEG, functools.partial(step, store=False), (z, z))
    pr, pi = lr, li
    for _ in range(int(math.log2(TSEG))):
        pr, pi = _cmul(pr, pi, pr, pi)

    seg = lax.broadcasted_iota(jnp.int32, (NSEG, n), 0)

    def chain(order):
        cr = jnp.zeros((1, n), F32)
        ci = jnp.zeros((1, n), F32)
        outr = jnp.zeros((NSEG, n), F32)
        outi = jnp.zeros((NSEG, n), F32)
        for s in order:
            outr = jnp.where(seg == s, cr, outr)
            outi = jnp.where(seg == s, ci, outi)
            mr, mi = _cmul(pr, pi, cr, ci)
            cr, ci = mr + er[s:s + 1], mi + ei[s:s + 1]
        return outr, outi

    fr, fi = chain(range(NSEG))
    rr, ri = chain(range(NSEG - 1, -1, -1))
    cin_r = jnp.where(rev, rr, fr)
    cin_i = jnp.where(rev, ri, fi)
    lax.fori_loop(0, TSEG, functools.partial(step, store=True), (cin_r, cin_i))
    return cin_r, cin_i


S5_RC = 512


def _s5_specs():
    u_spec = pl.BlockSpec((L, 128), lambda cb, h, d: (0, cb))
    b_spec = pl.BlockSpec((None, None, 128, 2 * SBW), lambda cb, h, d: (d, cb * 2 + h, 0, 0))
    c_spec = pl.BlockSpec((None, None, 2 * SBW, 128), lambda cb, h, d: (d, cb * 2 + h, 0, 0))
    l_spec = pl.BlockSpec((None, None, 1, SBW), lambda cb, h, d: (d, cb * 2 + h, 0, 0))
    d_spec = pl.BlockSpec((1, 128), lambda cb, h, d: (0, cb))
    return u_spec, b_spec, c_spec, l_spec, d_spec


def _s5_input_states(u_ref, b_ref, lr_ref, li_ref, s_scr, rev):
    def proj(i, _):
        rs = pl.ds(pl.multiple_of(i * S5_RC, S5_RC), S5_RC)
        s_scr[rs, :] = jnp.dot(u_ref[rs, :].astype(BF16), b_ref[...], preferred_element_type=F32)
        return 0

    lax.fori_loop(0, L // S5_RC, proj, 0)
    return _scan_inplace(s_scr, lr_ref[...], li_ref[...], rev)


def _s5_fwd(u_p, bcat, ccat, lam_re, lam_im, dskip):
    def body(u_ref, b_ref, c_ref, lr_ref, li_ref, d_ref, y_ref, s_scr):
        first = (pl.program_id(1) == 0) & (pl.program_id(2) == 0)
        _s5_input_states(u_ref, b_ref, lr_ref, li_ref, s_scr, pl.program_id(2) == 1)

        def out(i, _):
            rs = pl.ds(pl.multiple_of(i * S5_RC, S5_RC), S5_RC)
            yv = jnp.dot(s_scr[rs, :].astype(BF16), c_ref[...], preferred_element_type=F32)

            @pl.when(first)
            def _():
                y_ref[rs, :] = d_ref[...] * u_ref[rs, :] + yv

            @pl.when(jnp.logical_not(first))
            def _():
                y_ref[rs, :] += yv

            return 0

        lax.fori_loop(0, L // S5_RC, out, 0)

    u_spec, b_spec, c_spec, l_spec, d_spec = _s5_specs()
    return pl.pallas_call(
        body, name="s5_fwd", grid=(SW // 128, 2, 2),
        in_specs=[u_spec, b_spec, c_spec, l_spec, l_spec, d_spec],
        out_specs=u_spec, out_shape=jax.ShapeDtypeStruct((L, SW), F32),
        scratch_shapes=[pltpu.VMEM((L, 2 * SBW), F32)],
        compiler_params=_cp(("parallel", "arbitrary", "arbitrary")),
    )(u_p, bcat, ccat, lam_re, lam_im, dskip)


def _s5_bwd(u_p, dy_p, bcat, ccat, lam_re, lam_im, dskip):
    def body(u_ref, dy_ref, b_ref, c_ref, lr_ref, li_ref, d_ref,
             du_ref, db_ref, dc_ref, dlr_ref, dli_ref, dd_ref, s_scr, g_scr):
        first = (pl.program_id(1) == 0) & (pl.program_id(2) == 0)
        rev = pl.program_id(2) == 1
        cin_r, cin_i = _s5_input_states(u_ref, b_ref, lr_ref, li_ref, s_scr, rev)

        def dstate(i, _):
            rs = pl.ds(pl.multiple_of(i * S5_RC, S5_RC), S5_RC)
            g_scr[rs, :] = lax.dot_general(dy_ref[rs, :].astype(BF16), c_ref[...], _NT, preferred_element_type=F32)
            return 0

        lax.fori_loop(0, L // S5_RC, dstate, 0)
        _scan_inplace(g_scr, lr_ref[...], -li_ref[...], jnp.logical_not(rev))

        def dlam(k, c):
            ar, ai = c
            tg = jnp.where(rev, k, k + 1)
            ts = jnp.where(rev, k + 1, k)
            rg = pl.ds(pl.multiple_of(tg * NSEG, NSEG), NSEG)
            rs = pl.ds(pl.multiple_of(ts * NSEG, NSEG), NSEG)
            gr, gi = g_scr[rg, 0:SBW], g_scr[rg, SBW:2 * SBW]
            sr, si = s_scr[rs, 0:SBW], s_scr[rs, SBW:2 * SBW]
            return ar + gr * sr + gi * si, ai + gi * sr - gr * si

        z = jnp.zeros((NSEG, SBW), F32)
        ar, ai = lax.fori_loop(0, TSEG - 1, dlam, (z, z))
        rb = pl.ds(pl.multiple_of(jnp.where(rev, TSEG - 1, 0) * NSEG, NSEG), NSEG)
        gr, gi = g_scr[rb, 0:SBW], g_scr[rb, SBW:2 * SBW]
        ar = ar + gr * cin_r + gi * cin_i
        ai = ai + gi * cin_r - gr * cin_i
        dlr_ref[...] = jnp.sum(ar, axis=0, keepdims=True)
        dli_ref[...] = jnp.sum(ai, axis=0, keepdims=True)

        db_ref[...] = jnp.zeros_like(db_ref)
        dc_ref[...] = jnp.zeros_like(dc_ref)

        @pl.when(first)
        def _():
            dd_ref[...] = jnp.zeros_like(dd_ref)

        def grads(i, _):
            rs = pl.ds(pl.multiple_of(i * S5_RC, S5_RC), S5_RC)
            uv = u_ref[rs, :]
            dyv = dy_ref[rs, :]
            gb = g_scr[rs, :].astype(BF16)
            db_ref[...] += lax.dot_general(uv.astype(BF16), gb, _TN, preferred_element_type=F32)
            dc_ref[...] += lax.dot_general(s_scr[rs, :].astype(BF16), dyv.astype(BF16), _TN,
                                           preferred_element_type=F32)
            duv = lax.dot_general(gb, b_ref[...], _NT, preferred_element_type=F32)

            @pl.when(first)
            def _():
                du_ref[rs, :] = d_ref[...] * dyv + duv
                dd_ref[...] += jnp.sum(dyv * uv, axis=0, keepdims=True)

            @pl.when(jnp.logical_not(first))
            def _():
                du_ref[rs, :] += duv

            return 0

        lax.fori_loop(0, L // S5_RC, grads, 0)

    u_spec, b_spec, c_spec, l_spec, d_spec = _s5_specs()
    scr = pltpu.VMEM((L, 2 * SBW), F32)
    return pl.pallas_call(
        body, name="s5_bwd", grid=(SW // 128, 2, 2),
        in_specs=[u_spec, u_spec, b_spec, c_spec, l_spec, l_spec, d_spec],
        out_specs=[u_spec, b_spec, c_spec, l_spec, l_spec, d_spec],
        out_shape=[jax.ShapeDtypeStruct((L, SW), F32),
                   jax.ShapeDtypeStruct((2, NSB, 128, 2 * SBW), F32), jax.ShapeDtypeStruct((2, NSB, 2 * SBW, 128), F32),
                   jax.ShapeDtypeStruct((2, NSB, 1, SBW), F32), jax.ShapeDtypeStruct((2, NSB, 1, SBW), F32),
                   jax.ShapeDtypeStruct((1, SW), F32)],
        scratch_shapes=[scr, scr],
        compiler_params=_cp(("parallel", "arbitrary", "arbitrary")),
    )(u_p, dy_p, bcat, ccat, lam_re, lam_im, dskip)


def _s5_params(a_re, a_im, log_step, bt_re, bt_im):
    lam = lax.complex(a_re, a_im)
    step = jnp.exp(log_step)[..., None]
    lam_bar = jnp.exp(lam * step)
    b_bar = ((lam_bar - 1.0) / lam)[..., None, :] * lax.complex(bt_re, bt_im)
    return jnp.real(lam_bar), jnp.imag(lam_bar), jnp.real(b_bar), jnp.imag(b_bar)


def _sel():
    i = jnp.arange(8)[None, :, None]
    j = jnp.arange(4)[None, None, :]
    r = jnp.arange(2)[:, None, None]
    return (i == r * 4 + j).astype(F32)


def _to_bcat(bt_re, bt_im):
    def one(bt):
        return jnp.einsum('dkrjcp,rij->dkricjp', bt.reshape(2, 4, 2, 4, GC, NP), _sel()).reshape(2, NSB, 128, SBW)
    return jnp.concatenate([one(bt_re), one(bt_im)], axis=-1)


def _from_bcat(dbcat):
    def one(dbbd):
        return jnp.einsum('dkricjp,rij->dkrjcp', dbbd.reshape(2, 4, 2, 8, GC, 4, NP), _sel()).reshape(2, NG, GC, NP)
    return one(dbcat[..., :SBW]), one(dbcat[..., SBW:])


def _to_ccat(c_re, c_im):
    def one(cc):
        return jnp.einsum('dkrjcp,rij->dkrjpic', cc.reshape(2, 4, 2, 4, GC, NP), _sel()).reshape(2, NSB, SBW, 128)
    return jnp.concatenate([one(c_re), -one(c_im)], axis=-2)


def _from_ccat(dccat):
    def one(dcbd):
        return jnp.einsum('dkrjpic,rij->dkrjcp', dcbd.reshape(2, 4, 2, 4, NP, 8, GC), _sel()).reshape(2, NG, GC, NP)
    return one(dccat[:, :, :SBW]), -one(dccat[:, :, SBW:])


def _gelu(y):
    return 0.5 * y * (1.0 + lax.erf(y * (2.0 ** -0.5)))


def _gelu_grad(y):
    return 0.5 * (1.0 + lax.erf(y * (2.0 ** -0.5))) + y * jnp.exp(-0.5 * y * y) * ((2.0 * math.pi) ** -0.5)


def _sigmoid(z):
    return 0.5 * jnp.tanh(0.5 * z) + 0.5


def _glu_fwd(y, wg):
    def body(y_ref, w_ref, o_ref, z_ref):
        ys = _gelu(y_ref[...])
        z = jnp.dot(ys.astype(BF16), w_ref[...], preferred_element_type=F32)
        z_ref[...] = z
        o_ref[...] = ys * _sigmoid(z)

    row = pl.BlockSpec((TL, SW), lambda i: (i, 0))
    return pl.pallas_call(
        body, name="glu_fwd", grid=(L // TL,),
        in_specs=[row, pl.BlockSpec((SW, SW), lambda i: (0, 0))], out_specs=[row, row],
        out_shape=[jax.ShapeDtypeStruct((L, SW), F32), jax.ShapeDtypeStruct((L, SW), F32)],
        compiler_params=_cp(("parallel",)),
    )(y, wg)


def _glu_bwd(y, z, dout, wg):
    def body(y_ref, z_ref, do_ref, w_ref, dy_ref, dw_ref):
        @pl.when(pl.program_id(0) == 0)
        def _():
            dw_ref[...] = jnp.zeros_like(dw_ref)

        yv = y_ref[...]
        ys = _gelu(yv)
        sg = _sigmoid(z_ref[...])
        dov = do_ref[...]
        dz = (dov * ys * sg * (1.0 - sg)).astype(BF16)
        dys = dov * sg + lax.dot_general(dz, w_ref[...], _NT, preferred_element_type=F32)
        dy_ref[...] = dys * _gelu_grad(yv)
        dw_ref[...] += lax.dot_general(ys.astype(BF16), dz, _TN, preferred_element_type=F32)

    row = pl.BlockSpec((TL, SW), lambda i: (i, 0))
    wsp = pl.BlockSpec((SW, SW), lambda i: (0, 0))
    return pl.pallas_call(
        body, name="glu_bwd", grid=(L // TL,),
        in_specs=[row, row, row, wsp], out_specs=[row, wsp],
        out_shape=[jax.ShapeDtypeStruct((L, SW), F32), jax.ShapeDtypeStruct((SW, SW), F32)],
        compiler_params=_cp(("arbitrary",)),
    )(y, z, dout, wg)


CT = 256
CR = 128
NCT = DFF // CT


def _shifted(ref, r):
    cur = ref[pl.ds(r, CR), :]
    before = ref[pl.ds(pl.multiple_of(jnp.maximum(r - 8, 0), 8), 8), :][7:8, :]
    after = ref[pl.ds(pl.multiple_of(jnp.minimum(r + CR, L - 8), 8), 8), :][0:1, :]
    before = jnp.where(r > 0, before, 0.0)
    after = jnp.where(r + CR < L, after, 0.0)
    row = lax.broadcasted_iota(jnp.int32, cur.shape, 0)
    prev = jnp.where(row == 0, before, pltpu.roll(cur, 1, 0))
    nxt = jnp.where(row == CR - 1, after, pltpu.roll(cur, CR - 1, 0))
    return prev, cur, nxt


def _conv3(ref, r, w_ref, b_ref):
    prev, cur, nxt = _shifted(ref, r)
    return w_ref[0:1, :] * prev + w_ref[1:2, :] * cur + w_ref[2:3, :] * nxt + b_ref[...]


def _convact_fwd(up, conv_w, conv_b):
    def body(ug_ref, uv_ref, wg_ref, wv_ref, bg_ref, bv_ref, o_ref):
        def chunk(i, _):
            r = pl.multiple_of(i * CR, CR)
            g = _conv3(ug_ref, r, wg_ref, bg_ref)
            v = _conv3(uv_ref, r, wv_ref, bv_ref)
            o_ref[pl.ds(r, CR), :] = (g * _sigmoid(g) * v).astype(BF16)
            return 0

        lax.fori_loop(0, L // CR, chunk, 0)

    gcol = pl.BlockSpec((L, CT), lambda j: (0, j))
    vcol = pl.BlockSpec((L, CT), lambda j: (0, j + NCT))
    return pl.pallas_call(
        body, name="convact_fwd", grid=(NCT,),
        in_specs=[gcol, vcol,
                  pl.BlockSpec((3, CT), lambda j: (0, j)), pl.BlockSpec((3, CT), lambda j: (0, j + NCT)),
                  pl.BlockSpec((1, CT), lambda j: (0, j)), pl.BlockSpec((1, CT), lambda j: (0, j + NCT))],
        out_specs=gcol, out_shape=jax.ShapeDtypeStruct((L, DFF), BF16),
        compiler_params=_cp(("parallel",)),
    )(up, up, conv_w, conv_w, conv_b, conv_b)


def _convact_bwd(up, dact, conv_w, conv_b):
    def body(ug_ref, uv_ref, da_ref, wg_ref, wv_ref, bg_ref, bv_ref, du_ref, dw_ref, db_ref, dgs, dvs, dwv, dbv):
        half = pl.program_id(1)

        def transpose_conv(src, w_ref):
            def chunk(i, _):
                r = pl.multiple_of(i * CR, CR)
                prev, cur, nxt = _shifted(src, r)
                du_ref[pl.ds(r, CR), :] = (w_ref[0:1, :] * nxt + w_ref[1:2, :] * cur
                                           + w_ref[2:3, :] * prev).astype(BF16)
                return 0

            lax.fori_loop(0, L // CR, chunk, 0)

        @pl.when(half == 0)
        def _():
            dw_ref[...] = jnp.zeros_like(dw_ref)
            db_ref[...] = jnp.zeros_like(db_ref)
            dwv[...] = jnp.zeros_like(dwv)
            dbv[...] = jnp.zeros_like(dbv)

            def chunk1(i, _):
                r = pl.multiple_of(i * CR, CR)
                rs = pl.ds(r, CR)
                pg, cg, ng = _shifted(ug_ref, r)
                pv, cv, nv = _shifted(uv_ref, r)
                g = wg_ref[0:1, :] * pg + wg_ref[1:2, :] * cg + wg_ref[2:3, :] * ng + bg_ref[...]
                v = wv_ref[0:1, :] * pv + wv_ref[1:2, :] * cv + wv_ref[2:3, :] * nv + bv_ref[...]
                sg = _sigmoid(g)
                da = da_ref[rs, :]
                dv = da * g * sg
                dg = da * v * sg * (1.0 + g * (1.0 - sg))
                dgs[rs, :] = dg
                dvs[rs, :] = dv
                for dref, dval, taps, bref in ((dw_ref, dg, (pg, cg, ng), db_ref), (dwv, dv, (pv, cv, nv), dbv)):
                    for k in range(3):
                        dref[k:k + 1, :] += jnp.sum(dval * taps[k], axis=0, keepdims=True)
                    bref[0:1, :] += jnp.sum(dval, axis=0, keepdims=True)
                return 0

            lax.fori_loop(0, L // CR, chunk1, 0)
            transpose_conv(dgs, wg_ref)

        @pl.when(half == 1)
        def _():
            dw_ref[...] = dwv[0:3, :]
            db_ref[...] = dbv[0:1, :]
            transpose_conv(dvs, wv_ref)

    def col(rows, off):
        return pl.BlockSpec((rows, CT), lambda j, h: (0, j + off))

    def out(rows):
        return pl.BlockSpec((rows, CT), lambda j, h: (0, j + h * NCT))

    return pl.pallas_call(
        body, name="convact_bwd", grid=(NCT, 2),
        in_specs=[col(L, 0), col(L, NCT), col(L, 0), col(3, 0), col(3, NCT), col(1, 0), col(1, NCT)],
        out_specs=[out(L), out(3), out(1)],
        out_shape=[jax.ShapeDtypeStruct((L, 2 * DFF), BF16), jax.ShapeDtypeStruct((3, 2 * DFF), F32),
                   jax.ShapeDtypeStruct((1, 2 * DFF), F32)],
        scratch_shapes=[pltpu.VMEM((L, CT), F32), pltpu.VMEM((L, CT), F32),
                        pltpu.VMEM((8, CT), F32), pltpu.VMEM((8, CT), F32)],
        compiler_params=_cp(("parallel", "arbitrary")),
    )(up, up, dact, conv_w, conv_w, conv_b, conv_b)


def _local_step(x, tgt, w_in_t, w_glu, w_out, conv_w, p, attend, reduce_early):
    tabs = _rope_tables()
    lam_re, lam_im, bb_re, bb_im = _s5_params(p["a_re"], p["a_im"], p["log_step"], p["bt_re"], p["bt_im"])
    bcat = _to_bcat(bb_re, bb_im).astype(BF16)
    ccat = _to_ccat(p["c_re"], p["c_im"]).astype(BF16)
    lam_re4, lam_im4 = lam_re.reshape(2, NSB, 1, SBW), lam_im.reshape(2, NSB, 1, SBW)
    dskip = p["d_skip"].reshape(1, SW)
    g_mix, g_ffn, g_fin = p["norm_mix_g"].reshape(1, D), p["norm_ffn_g"].reshape(1, D), p["norm_final_g"].reshape(1, D)
    g_attn, g_ssm = p["norm_attn_g"].reshape(1, AW), p["norm_ssm_g"].reshape(1, SW)
    sink = p["sink"].reshape(NQ)
    conv_b = p["conv_b"].reshape(1, 2 * DFF)

    h1 = _rms_fwd(x, g_mix, "norm_mix_fwd")
    proj = _mm(h1, w_in_t, tb=True, name="in_proj", tn=1280)
    qkv = _rope_fwd(proj, tabs)
    attn, lse, w_up_t, w_down = attend(qkv, sink)
    u_p = _perm(proj[:, AW + 2 * KVW:])
    y_p = _s5_fwd(u_p, bcat, ccat, lam_re4, lam_im4, dskip)
    ysg_p, z_p = _glu_fwd(y_p, w_glu)
    ysg = _unperm(ysg_p)
    mixed = _mixnorm_fwd(attn, ysg, g_attn, g_ssm)
    x1 = _mm(mixed, w_out, add=x, name="out_proj")
    h2 = _rms_fwd(x1, g_ffn, "norm_ffn_fwd")
    up = _mm(h2, w_up_t, tb=True, name="ffn_up", tn=1408)
    act = _convact_fwd(up, conv_w, conv_b)
    x2 = _mm(act, w_down, add=x1, name="ffn_down", tk=1408)
    loss, dx2, dg_fin = _final_loss(x2, tgt, g_fin)

    dact = _mm(dx2, w_down, tb=True, name="ffn_down_dx", tn=1408)
    dw_down = _mm(act, dx2, ta=True, name="ffn_down_dw", tm=1408)
    ride = reduce_early("w_down", dw_down)
    dup, dconv_w, dconv_b = _convact_bwd(up, dact, conv_w, conv_b)
    res = _mm(dup, h2, ta=True, name="ffn_up_dw", tm=1408, ride=ride)
    dw_up_t, got_down = res if ride else (res, [])
    ride = reduce_early("w_up_t", dw_up_t)
    res = _mm(dup, w_up_t, name="ffn_up_dx", tk=1408, ride=ride)
    dh2, got_up = res if ride else (res, [])
    dx1, dg_ffn = _rms_bwd(x1, g_ffn, dh2, dx2, "norm_ffn_bwd")
    dmixed = _mm(dx1, w_out, tb=True, name="out_proj_dx")
    dw_out = _mm(mixed, dx1, ta=True, name="out_proj_dw")
    dattn, dysg, dg_attn, dg_ssm = _mixnorm_bwd(attn, ysg, g_attn, g_ssm, dmixed)
    dy_p, dw_glu = _glu_bwd(y_p, z_p, _perm(dysg), w_glu)
    du_p, dbcat, dccat, dlam_re, dlam_im, dd = _s5_bwd(u_p, dy_p, bcat, ccat, lam_re4, lam_im4, dskip)
    dbb_re, dbb_im = _from_bcat(dbcat)
    dc_re, dc_im = _from_ccat(dccat)
    dq, dk, dv, dsink = _attn_bwd(qkv, sink, attn, lse, dattn)
    dproj = _rope_bwd(dq, dk, dv, _unperm(du_p), tabs)
    dw_in_t = _mm(dproj, h1, ta=True, name="in_proj_dw", tm=1280)
    dh1 = _mm(dproj, w_in_t, name="in_proj_dx", tk=1280)
    grad_x, dg_mix = _rms_bwd(x, g_mix, dh1, dx1, "norm_mix_bwd")

    big = dict(w_in_t=dw_in_t, w_glu=dw_glu, w_out=dw_out)
    early = dict(w_down=got_down, w_up_t=got_up)
    small = dict(norm_mix_g=dg_mix, norm_attn_g=dg_attn, norm_ssm_g=dg_ssm, norm_ffn_g=dg_ffn, norm_final_g=dg_fin,
                 sink=dsink[:, 0], conv_b=dconv_b, d_skip=dd, conv_w=dconv_w,
                 lam_re=dlam_re.reshape(2, NG, NP), lam_im=dlam_im.reshape(2, NG, NP),
                 bb_re=dbb_re, bb_im=dbb_im, c_re=dc_re, c_im=dc_im, loss=loss.reshape(1))
    return grad_x, big, small, early


ANY = pl.BlockSpec(memory_space=pl.ANY)


def _coords():
    return lax.axis_index("x"), lax.axis_index("y"), lax.axis_index("c")


def _flip(v, b):
    return v + b - 2 * v * b if b else v


def _all_gather(shards, name):
    n = len(shards)

    def body(*refs):
        _gather_start(refs[:n], refs[n:2 * n], *refs[2 * n:])
        _gather_finish(refs[:n], refs[n:2 * n], *refs[2 * n:])

    return pl.pallas_call(
        body, name=name,
        in_specs=[ANY] * n, out_specs=[ANY] * n,
        out_shape=_gather_shapes(shards), scratch_shapes=_gather_sems(n),
    )(*shards)


def _gather_shapes(shards):
    return [jax.ShapeDtypeStruct((NDEV * s.shape[0], s.shape[1]), s.dtype) for s in shards]


def _gather_sems(n):
    return [pltpu.SemaphoreType.DMA((7 * n,)), pltpu.SemaphoreType.DMA((7 * n,)), pltpu.SemaphoreType.DMA((n,))]


def _gather_copies(ins, outs, send_sems, recv_sems, local_sems, a):
    x, y, c = _coords()
    me, sibling = (x, y, c), (x, y, 1 - c)
    chips = [(1 - x, y), (x, 1 - y), (1 - x, 1 - y)]
    r = ins[a].shape[0]

    def rows(px, py, pc):
        return outs[a].at[pl.ds(pl.multiple_of((4 * px + 2 * py + pc) * r, 8), r), :]

    def copy(k, block, to, src=None):
        return pltpu.make_async_remote_copy(
            src_ref=rows(*block) if src is None else src, dst_ref=rows(*block),
            send_sem=send_sems.at[a * 7 + k], recv_sem=recv_sems.at[a * 7 + k],
            device_id=to, device_id_type=pl.DeviceIdType.MESH)

    mine = pltpu.make_async_copy(ins[a], rows(*me), local_sems.at[a])
    first = [copy(0, me, sibling, src=ins[a])]
    first += [copy(1 + j, me, (*chip, c), src=ins[a]) for j, chip in enumerate(chips)]
    passed = [copy(4 + j, (*chip, c), sibling) for j, chip in enumerate(chips)]
    arrivals = [copy(1 + j, (*chip, c), me) for j, chip in enumerate(chips)]
    from_sibling = [copy(0, sibling, me)] + [copy(4 + j, (*chip, 1 - c), me) for j, chip in enumerate(chips)]
    return mine, first, passed, arrivals, from_sibling


def _gather_start(ins, outs, send_sems, recv_sems, local_sems):
    for a in range(len(ins)):
        mine, first, _, _, _ = _gather_copies(ins, outs, send_sems, recv_sems, local_sems, a)
        mine.start()
        for cp in first:
            cp.start()


def _gather_finish(ins, outs, send_sems, recv_sems, local_sems):
    n = len(ins)
    parts = [_gather_copies(ins, outs, send_sems, recv_sems, local_sems, a) for a in range(n)]
    for mine, first, passed, arrivals, from_sibling in parts:
        for arrived, onward in zip(arrivals, passed):
            arrived.wait_recv()
            onward.start()
    for mine, first, passed, arrivals, from_sibling in parts:
        for cp in from_sibling:
            cp.wait_recv()
        for cp in first + passed:
            cp.wait_send()
        mine.wait()


NCHIP = 4
CHIP_FLIPS = ((1, 0), (0, 1), (1, 1))


def _planned_copies(ins, outs, send_sems, recv_sems, plan):
    return [pltpu.make_async_remote_copy(
        src_ref=ins[a].at[src], dst_ref=outs[a].at[dst], send_sem=send_sems.at[k], recv_sem=recv_sems.at[k],
        device_id=to, device_id_type=pl.DeviceIdType.MESH) for k, (a, src, dst, to) in enumerate(plan)]


def _start_all(copies):
    for cp in copies:
        cp.start()


def _wait_all(copies):
    for cp in copies:
        cp.wait_recv()
    for cp in copies:
        cp.wait_send()


def _exchange_cores(parts, name):
    n = len(parts)

    def body(*refs):
        x, y, c = _coords()
        plan = [(a, 2 * q + 1 - c, q, (x, y, 1 - c)) for a in range(n) for q in range(NCHIP)]
        copies = _planned_copies(refs[:n], refs[n:2 * n], *refs[2 * n:], plan)
        _start_all(copies)
        _wait_all(copies)

    return pl.pallas_call(
        body, name=name, in_specs=[ANY] * n, out_specs=[ANY] * n,
        out_shape=[jax.ShapeDtypeStruct((NCHIP,) + s.shape[1:], s.dtype) for s in parts],
        scratch_shapes=[pltpu.SemaphoreType.DMA((NCHIP * n,)), pltpu.SemaphoreType.DMA((NCHIP * n,))],
    )(*parts)


def _chips_copies(ins, outs, send_sems, recv_sems):
    x, y, c = _coords()
    plan = []
    for a in range(len(ins)):
        for j, (fx, fy) in enumerate(CHIP_FLIPS):
            px, py = _flip(x, fx), _flip(y, fy)
            plan.append((a, 2 * px + py, j, (px, py, c)))
    return _planned_copies(ins, outs, send_sems, recv_sems, plan)


def _chips_shapes(parts):
    return [jax.ShapeDtypeStruct((3,) + s.shape[1:], s.dtype) for s in parts]


def _chips_sems(n):
    return [pltpu.SemaphoreType.DMA((3 * n,)), pltpu.SemaphoreType.DMA((3 * n,))]


def _exchange_chips(parts, name):
    n = len(parts)

    def body(*refs):
        copies = _chips_copies(refs[:n], refs[n:2 * n], *refs[2 * n:])
        _start_all(copies)
        _wait_all(copies)

    return pl.pallas_call(
        body, name=name, in_specs=[ANY] * n, out_specs=[ANY] * n,
        out_shape=_chips_shapes(parts), scratch_shapes=_chips_sems(n),
    )(*parts)


def _pair_sum(where, part, recv, wire_dtype, name):
    _, r, c = part.shape
    tr = _pick(r, 256, 16)

    def body(w_ref, p_ref, r_ref, pb_ref, own_ref):
        s = p_ref[...] + r_ref[...]
        pb_ref[...] = s.astype(wire_dtype)

        @pl.when(pl.program_id(1) == w_ref[1])
        def _():
            own_ref[...] = s

    return pl.pallas_call(
        body, name=name,
        grid_spec=pltpu.PrefetchScalarGridSpec(
            num_scalar_prefetch=1, grid=(r // tr, NCHIP),
            in_specs=[pl.BlockSpec((None, tr, c), lambda i, q, w: (2 * q + w[0], i, 0)),
                      pl.BlockSpec((None, tr, c), lambda i, q, w: (q, i, 0))],
            out_specs=[pl.BlockSpec((None, tr, c), lambda i, q, w: (q, i, 0)),
                       pl.BlockSpec((tr, c), lambda i, q, w: (i, 0))]),
        out_shape=[jax.ShapeDtypeStruct((NCHIP, r, c), wire_dtype), jax.ShapeDtypeStruct((r, c), F32)],
        compiler_params=_cp(("parallel", "arbitrary")),
    )(where, part, recv)


def _chip_sum(own, recv, name):
    r, c = own.shape
    tr = _pick(r, 256, 16)

    def body(o_ref, r_ref, out_ref):
        acc = o_ref[...]
        for j in range(3):
            acc = acc + r_ref[j].astype(F32)
        out_ref[...] = acc

    return pl.pallas_call(
        body, name=name, grid=(r // tr,),
        in_specs=[pl.BlockSpec((tr, c), lambda i: (i, 0)), pl.BlockSpec((3, tr, c), lambda i: (0, i, 0))],
        out_specs=pl.BlockSpec((tr, c), lambda i: (i, 0)),
        out_shape=jax.ShapeDtypeStruct((r, c), F32),
        compiler_params=_cp(("parallel",)),
    )(own, recv)


def _adamw(w, g, m, v, name):
    r, c = w.shape
    tr = _pick(r, 256, 8)

    def body(w_ref, g_ref, m_ref, v_ref, d_ref, nm_ref, nv_ref):
        _adamw_refs(w_ref, g_ref, m_ref, v_ref, d_ref, nm_ref, nv_ref)

    blk = pl.BlockSpec((tr, c), lambda i: (i, 0))
    return pl.pallas_call(
        body, name=name, grid=(r // tr,),
        in_specs=[blk] * 4, out_specs=[blk] * 3,
        out_shape=[jax.ShapeDtypeStruct((r, c), F32)] * 3,
        compiler_params=_cp(("parallel",)),
    )(w, g, m, v)


def _adamw_refs(w_ref, g_ref, m_ref, v_ref, d_ref, nm_ref, nv_ref):
    gv = g_ref[...]
    nm = B1 * m_ref[...] + (1.0 - B1) * gv
    nv = B2 * v_ref[...] + (1.0 - B2) * (gv * gv)
    nm_ref[...] = nm
    nv_ref[...] = nv
    d_ref[...] = -LR * ((nm / C1) / (jnp.sqrt(nv / C2) + AEPS) + WD * w_ref[...])


def _adamw_small(ws, gs, ms, vs, name):
    n = len(ws)

    def body(*refs):
        groups = [refs[i * n:(i + 1) * n] for i in range(7)]
        for per_param in zip(*groups):
            _adamw_refs(*per_param)

    vm = pl.BlockSpec(memory_space=pltpu.VMEM)
    outs = pl.pallas_call(
        body, name=name, in_specs=[vm] * (4 * n), out_specs=[vm] * (3 * n),
        out_shape=[jax.ShapeDtypeStruct(a.shape, F32) for a in ws] * 3,
    )(*ws, *gs, *ms, *vs)
    return outs[:n], outs[n:2 * n], outs[2 * n:]


def _swap(a):
    return jnp.swapaxes(a, -1, -2)


VIEWS = {
    "w_in": (lambda a: a[0].T, lambda u: u.T[None]),
    "w_up": (lambda a: a[0].T, lambda u: u.T[None]),
    "w_glu": (lambda a: a[0], lambda u: u[None]),
    "w_out": (lambda a: a[0], lambda u: u[None]),
    "w_down": (lambda a: a[0], lambda u: u[None]),
    "conv_w": (lambda a: a[0], lambda u: u[None]),
    "norm_mix_g": (lambda a: a, lambda u: u),
    "norm_attn_g": (lambda a: a, lambda u: u),
    "norm_ssm_g": (lambda a: a, lambda u: u),
    "norm_ffn_g": (lambda a: a, lambda u: u),
    "norm_final_g": (lambda a: a[None], lambda u: u[0]),
    "conv_b": (lambda a: a, lambda u: u),
    "sink": (lambda a: a, lambda u: u),
    "a_re": (lambda a: a.reshape(2 * NG, NP), lambda u: u.reshape(1, 2, NG, NP)),
    "a_im": (lambda a: a.reshape(2 * NG, NP), lambda u: u.reshape(1, 2, NG, NP)),
    "log_step": (lambda a: a[0], lambda u: u[None]),
    "b_re": (lambda a: _swap(a[0]).reshape(2 * NG * GC, NP), lambda u: _swap(u.reshape(2, NG, GC, NP))[None]),
    "b_im": (lambda a: _swap(a[0]).reshape(2 * NG * GC, NP), lambda u: _swap(u.reshape(2, NG, GC, NP))[None]),
    "c_re": (lambda a: a.reshape(2 * NG * GC, NP), lambda u: u.reshape(1, 2, NG, GC, NP)),
    "c_im": (lambda a: a.reshape(2 * NG * GC, NP), lambda u: u.reshape(1, 2, NG, GC, NP)),
    "d_skip": (lambda a: a[0].T, lambda u: u.T[None]),
}
BIG = ["w_in", "w_glu", "w_out", "w_up", "w_down"]
PACK_W = 1024


def _pack(arrs, rows):
    flat = jnp.concatenate([a.reshape(-1).astype(F32) for a in arrs])
    return jnp.pad(flat, (0, rows * PACK_W - flat.shape[0])).reshape(rows, PACK_W)


def _unpack(packed, shapes):
    flat = packed.reshape(-1)
    out, off = [], 0
    for s in shapes:
        size = math.prod(s)
        out.append(flat[off:off + size].reshape(s))
        off += size
    return out


def kernel(x, norm_mix_g, w_in, a_re, a_im, log_step, b_re, b_im, c_re, c_im, d_skip, w_glu, sink, norm_attn_g, norm_ssm_g, w_out, norm_ffn_g, w_up, conv_w, conv_b, w_down, norm_final_g, loss_target, m_norm_mix_g, m_w_in, m_a_re, m_a_im, m_log_step, m_b_re, m_b_im, m_c_re, m_c_im, m_d_skip, m_w_glu, m_sink, m_norm_attn_g, m_norm_ssm_g, m_w_out, m_norm_ffn_g, m_w_up, m_conv_w, m_conv_b, m_w_down, m_norm_final_g, v_norm_mix_g, v_w_in, v_a_re, v_a_im, v_log_step, v_b_re, v_b_im, v_c_re, v_c_im, v_d_skip, v_w_glu, v_sink, v_norm_attn_g, v_norm_ssm_g, v_w_out, v_norm_ffn_g, v_w_up, v_conv_w, v_conv_b, v_w_down, v_norm_final_g):
    args = dict(locals())
    names = ["norm_mix_g", "w_in", "a_re", "a_im", "log_step", "b_re", "b_im", "c_re", "c_im", "d_skip", "w_glu",
             "sink", "norm_attn_g", "norm_ssm_g", "w_out", "norm_ffn_g", "w_up", "conv_w", "conv_b", "w_down",
             "norm_final_g"]
    w = {k: args[k] for k in names}
    m = {k: args["m_" + k] for k in names}
    v = {k: args["v_" + k] for k in names}

    shards = [w_in[0].T.astype(BF16), w_glu[0].astype(BF16), w_out[0].astype(BF16),
              jnp.pad(conv_w[0], ((0, 5), (0, 0)))]
    w_in_t, w_glu_f, w_out_f, conv_w_g = _all_gather(shards, "gather_weights")
    conv_w_f = conv_w_g.reshape(NDEV, 8, 2 * DFF // NDEV)[:, :3].transpose(1, 0, 2).reshape(3, 2 * DFF)
    ffn_shards = [w_up[0].T.astype(BF16), w_down[0].astype(BF16)]

    ax, ay, ac = _coords()
    me = 4 * ax + 2 * ay + ac
    where = jnp.stack([ac, 2 * ax + ay]).astype(jnp.int32)
    own = {}

    def to_chip_sums(named_parts, tag):
        ks = [k for k, _ in named_parts]
        parts = [part for _, part in named_parts]
        from_core = _exchange_cores(parts, "exchange_cores_" + tag)
        wire = []
        for k, part, got in zip(ks, parts, from_core):
            per_chip, own[k] = _pair_sum(where, part, got, F32 if k == "small" else BF16, "pair_sum_" + k)
            wire.append(per_chip)
        return wire

    def split8(g):
        return g.reshape(NDEV, g.shape[0] // NDEV, g.shape[1])

    def attend(qkv, sink_):
        attn, lse, (w_up_t, w_down_f) = _attn_fwd(qkv, sink_, gather=ffn_shards)
        return attn, lse, w_up_t, w_down_f

    def reduce_early(k, dw):
        return to_chip_sums([(k, split8(dw))], k)

    p = {k: w[k][0] for k in ("norm_mix_g", "a_re", "a_im", "log_step", "c_re", "c_im", "d_skip", "sink",
                              "norm_attn_g", "norm_ssm_g", "norm_ffn_g", "conv_b")}
    p["norm_final_g"] = norm_final_g
    p["bt_re"], p["bt_im"] = _swap(b_re[0]), _swap(b_im[0])
    grad_x, big, small, early = _local_step(x[0], loss_target[0], w_in_t, w_glu_f, w_out_f, conv_w_f, p,
                                            attend, reduce_early)

    small_names = list(small.keys())
    small_shapes = [small[k].shape for k in small_names]
    n_small = sum(math.prod(s) for s in small_shapes)
    rows_dev = -(-n_small // (PACK_W * NDEV * 16)) * 16
    spack = _pack([small[k] for k in small_names], rows_dev * NDEV)
    late_names = ["w_in_t", "w_glu", "w_out", "small"]
    late_parts = [split8(big[k]) for k in late_names[:-1]] + [spack.reshape(NDEV, rows_dev, PACK_W)]
    from_chips = _exchange_chips(to_chip_sums(list(zip(late_names, late_parts)), "late"), "exchange_chips")
    got = dict(zip(late_names, from_chips), w_down=early["w_down"][0], w_up_t=early["w_up_t"][0])
    red = {k: _chip_sum(own[k], got[k], "chip_sum_" + k) for k in got}
    (small_full,) = _all_gather([red["small"]], "gather_small")
    sm = dict(zip(small_names, _unpack(small_full, small_shapes)))

    _, s5_vjp = jax.vjp(_s5_params, a_re[0], a_im[0], log_step[0], p["bt_re"], p["bt_im"])
    da_re, da_im, dlog_step, dbt_re, dbt_im = s5_vjp((sm["lam_re"], sm["lam_im"], sm["bb_re"], sm["bb_im"]))
    gview = {
        "norm_mix_g": sm["norm_mix_g"], "norm_attn_g": sm["norm_attn_g"], "norm_ssm_g": sm["norm_ssm_g"],
        "norm_ffn_g": sm["norm_ffn_g"], "norm_final_g": sm["norm_final_g"], "conv_b": sm["conv_b"],
        "sink": sm["sink"][None], "a_re": da_re.reshape(2 * NG, NP), "a_im": da_im.reshape(2 * NG, NP),
        "log_step": dlog_step, "b_re": dbt_re.reshape(2 * NG * GC, NP), "b_im": dbt_im.reshape(2 * NG * GC, NP),
        "c_re": sm["c_re"].reshape(2 * NG * GC, NP), "c_im": sm["c_im"].reshape(2 * NG * GC, NP),
        "d_skip": sm["d_skip"].reshape(NG, GC).T,
        "w_in": red["w_in_t"], "w_glu": red["w_glu"], "w_out": red["w_out"], "w_up": red["w_up_t"],
        "w_down": red["w_down"],
        "conv_w": lax.dynamic_slice_in_dim(sm["conv_w"], me * (2 * DFF // NDEV), 2 * DFF // NDEV, axis=1),
    }

    dview, mview, vview = {}, {}, {}
    for k in BIG:
        to = VIEWS[k][0]
        dview[k], mview[k], vview[k] = _adamw(to(w[k]), gview[k], to(m[k]), to(v[k]), "adamw_" + k)
    rest = [k for k in names if k not in BIG]
    outs = _adamw_small([VIEWS[k][0](w[k]) for k in rest], [gview[k] for k in rest],
                        [VIEWS[k][0](m[k]) for k in rest], [VIEWS[k][0](v[k]) for k in rest], "adamw_small")
    for dst, vals in zip((dview, mview, vview), outs):
        dst.update(dict(zip(rest, vals)))

    def back(views):
        return [VIEWS[k][1](views[k]) for k in names]

    return (sm["loss"][0], grad_x[None], *back(gview), *back(dview), *back(mview), *back(vview))
```

```python
import functools
import math

import jax
import jax.numpy as jnp
from jax import lax
from jax.experimental import pallas as pl
from jax.experimental.pallas import tpu as pltpu

F32 = jnp.float32
BF16 = jnp.bfloat16

L = 4096
D = 1024
NQ, NKV, HD = 8, 2, 64
AW = NQ * HD
KVW = NKV * HD
SW = 512
NG, GC, NP = 32, 16, 64
INW = AW + 2 * KVW + SW
DFF = 2816
BLK = 128
WIN = 3 * BLK
EPS = 1e-6
ROPE_THETA = 500000.0
NSEG = 32
TSEG = L // NSEG
SBW = 256
NSB = NG * NP // SBW
NDEV = 8
MESH_AXES = ("x", "y", "c")

LR, B1, B2, AEPS, WD, STEP = 0.001, 0.9, 0.999, 1e-08, 0.01, 10
C1 = 1.0 - B1 ** STEP
C2 = 1.0 - B2 ** STEP

VMEM_LIMIT = 56 * 1024 * 1024


def _pick(n, target, mult):
    best = None
    for t in range(mult, min(n, target) + 1, mult):
        if n % t == 0:
            best = t
    return best if best is not None else n


def _cp(sem):
    return pltpu.CompilerParams(dimension_semantics=sem, vmem_limit_bytes=VMEM_LIMIT)


def _mm(a, b, *, ta=False, tb=False, out_dtype=F32, add=None, ride=(), name, tm=1024, tn=1024, tk=1024):
    m, k = (a.shape[1], a.shape[0]) if ta else a.shape
    n = b.shape[0] if tb else b.shape[1]
    assert k == (b.shape[1] if tb else b.shape[0])
    tm, tn, tk = _pick(m, tm, 128), _pick(n, tn, 128), _pick(k, tk, 128)
    grid = (m // tm, n // tn, k // tk)
    nk = grid[2]
    dn = (((0 if ta else 1,), (1 if tb else 0,)), ((), ()))
    n_in = 2 + (add is not None)
    nr = len(ride)

    def body(*refs):
        a_ref, b_ref = refs[0], refs[1]
        o_ref = refs[n_in + nr]
        acc_ref = refs[n_in + 2 * nr + 1]
        step = [pl.program_id(d) for d in range(3)]
        kk = step[2]
        if nr:
            riders = (refs[n_in:n_in + nr], refs[n_in + nr + 1:n_in + 2 * nr + 1], *refs[n_in + 2 * nr + 2:])

            @pl.when((step[0] == 0) & (step[1] == 0) & (kk == 0))
            def _():
                _start_all(_chips_copies(*riders))

        prod = lax.dot_general(a_ref[...].astype(BF16), b_ref[...].astype(BF16), dn, preferred_element_type=F32)

        def finish(r):
            if add is not None:
                r = r + refs[2][...]
            o_ref[...] = r.astype(out_dtype)

        if nk == 1:
            finish(prod)
        else:
            @pl.when(kk == 0)
            def _():
                acc_ref[...] = prod

            @pl.when((kk > 0) & (kk < nk - 1))
            def _():
                acc_ref[...] += prod

            @pl.when(kk == nk - 1)
            def _():
                finish(acc_ref[...] + prod)

        if nr:
            @pl.when((step[0] == grid[0] - 1) & (step[1] == grid[1] - 1) & (kk == nk - 1))
            def _():
                _wait_all(_chips_copies(*riders))

    a_spec = pl.BlockSpec((tk, tm), lambda i, j, kk: (kk, i)) if ta else pl.BlockSpec((tm, tk), lambda i, j, kk: (i, kk))
    b_spec = pl.BlockSpec((tn, tk), lambda i, j, kk: (j, kk)) if tb else pl.BlockSpec((tk, tn), lambda i, j, kk: (kk, j))
    in_specs = [a_spec, b_spec]
    args = [a, b]
    if add is not None:
        in_specs.append(pl.BlockSpec((tm, tn), lambda i, j, kk: (i, j)))
        args.append(add)
    outs = pl.pallas_call(
        body, name=name, grid=grid,
        in_specs=in_specs + [ANY] * nr,
        out_specs=[pl.BlockSpec((tm, tn), lambda i, j, kk: (i, j))] + [ANY] * nr,
        out_shape=[jax.ShapeDtypeStruct((m, n), out_dtype)] + _chips_shapes(ride),
        scratch_shapes=[pltpu.VMEM((tm, tn) if nk > 1 else (8, 128), F32)] + (_chips_sems(nr) if nr else []),
        compiler_params=_cp(("arbitrary",) * 3 if nr else ("parallel", "parallel", "arbitrary")),
    )(*args, *ride)
    return (outs[0], list(outs[1:])) if nr else outs[0]


TL = 512


def _rms_fwd(x, g, name):
    d = x.shape[1]

    def body(x_ref, g_ref, o_ref):
        xv = x_ref[...]
        r = lax.rsqrt(jnp.mean(xv * xv, axis=-1, keepdims=True) + EPS)
        o_ref[...] = (xv * r * g_ref[...]).astype(BF16)

    return pl.pallas_call(
        body, name=name, grid=(L // TL,),
        in_specs=[pl.BlockSpec((TL, d), lambda i: (i, 0)), pl.BlockSpec((1, d), lambda i: (0, 0))],
        out_specs=pl.BlockSpec((TL, d), lambda i: (i, 0)),
        out_shape=jax.ShapeDtypeStruct((L, d), BF16),
        compiler_params=_cp(("parallel",)),
    )(x, g)


def _rms_bwd_tile(xv, gv, dh):
    r = lax.rsqrt(jnp.mean(xv * xv, axis=-1, keepdims=True) + EPS)
    a = dh * gv
    dx = r * a - xv * (r * r * r) * jnp.mean(a * xv, axis=-1, keepdims=True)
    dg = jnp.sum(dh * xv * r, axis=0, keepdims=True)
    return dx, dg


def _rms_bwd(x, g, dh, dres, name):
    d = x.shape[1]

    def body(x_ref, g_ref, dh_ref, dres_ref, dx_ref, dg_ref):
        dx, dg = _rms_bwd_tile(x_ref[...], g_ref[...], dh_ref[...])
        dx_ref[...] = dx + dres_ref[...]

        @pl.when(pl.program_id(0) == 0)
        def _():
            dg_ref[...] = jnp.zeros_like(dg_ref)

        dg_ref[...] += dg

    row = pl.BlockSpec((TL, d), lambda i: (i, 0))
    vec = pl.BlockSpec((1, d), lambda i: (0, 0))
    return pl.pallas_call(
        body, name=name, grid=(L // TL,),
        in_specs=[row, vec, row, row], out_specs=[row, vec],
        out_shape=[jax.ShapeDtypeStruct((L, d), F32), jax.ShapeDtypeStruct((1, d), F32)],
        compiler_params=_cp(("arbitrary",)),
    )(x, g, dh, dres)


def _mixnorm_fwd(attn, ysg, ga, gs):
    def body(a_ref, s_ref, ga_ref, gs_ref, o_ref):
        for src, gr, lo in ((a_ref, ga_ref, 0), (s_ref, gs_ref, AW)):
            xv = src[...]
            r = lax.rsqrt(jnp.mean(xv * xv, axis=-1, keepdims=True) + EPS)
            o_ref[:, lo:lo + 512] = (xv * r * gr[...]).astype(BF16)

    row = pl.BlockSpec((TL, 512), lambda i: (i, 0))
    vec = pl.BlockSpec((1, 512), lambda i: (0, 0))
    return pl.pallas_call(
        body, name="mixnorm_fwd", grid=(L // TL,),
        in_specs=[row, row, vec, vec], out_specs=pl.BlockSpec((TL, 1024), lambda i: (i, 0)),
        out_shape=jax.ShapeDtypeStruct((L, 1024), BF16),
        compiler_params=_cp(("parallel",)),
    )(attn, ysg, ga, gs)


def _mixnorm_bwd(attn, ysg, ga, gs, dmixed):
    def body(a_ref, s_ref, ga_ref, gs_ref, dm_ref, da_ref, ds_ref, dga_ref, dgs_ref):
        @pl.when(pl.program_id(0) == 0)
        def _():
            dga_ref[...] = jnp.zeros_like(dga_ref)
            dgs_ref[...] = jnp.zeros_like(dgs_ref)

        dxa, dga = _rms_bwd_tile(a_ref[...], ga_ref[...], dm_ref[:, 0:AW])
        da_ref[...] = dxa
        dga_ref[...] += dga
        dxs, dgs = _rms_bwd_tile(s_ref[...], gs_ref[...], dm_ref[:, AW:AW + SW])
        ds_ref[...] = dxs
        dgs_ref[...] += dgs

    row = pl.BlockSpec((TL, 512), lambda i: (i, 0))
    vec = pl.BlockSpec((1, 512), lambda i: (0, 0))
    return pl.pallas_call(
        body, name="mixnorm_bwd", grid=(L // TL,),
        in_specs=[row, row, vec, vec, pl.BlockSpec((TL, 1024), lambda i: (i, 0))],
        out_specs=[row, row, vec, vec],
        out_shape=[jax.ShapeDtypeStruct((L, 512), F32), jax.ShapeDtypeStruct((L, 512), F32),
                   jax.ShapeDtypeStruct((1, 512), F32), jax.ShapeDtypeStruct((1, 512), F32)],
        compiler_params=_cp(("arbitrary",)),
    )(attn, ysg, ga, gs, dmixed)


def _final_loss(x2, tgt, g):
    def body(x_ref, t_ref, g_ref, loss_ref, dx_ref, dg_ref):
        @pl.when(pl.program_id(0) == 0)
        def _():
            loss_ref[...] = jnp.zeros_like(loss_ref)
            dg_ref[...] = jnp.zeros_like(dg_ref)

        xv = x_ref[...]
        gv = g_ref[...]
        r = lax.rsqrt(jnp.mean(xv * xv, axis=-1, keepdims=True) + EPS)
        e = xv * r * gv - t_ref[...]
        loss_ref[...] += 0.5 * jnp.sum(jnp.mean(e * e, axis=-1, keepdims=True), axis=0, keepdims=True)
        dy = e * (1.0 / D)
        a = dy * gv
        dx_ref[...] = r * a - xv * (r * r * r) * jnp.mean(a * xv, axis=-1, keepdims=True)
        dg_ref[...] += jnp.sum(dy * xv * r, axis=0, keepdims=True)

    row = pl.BlockSpec((TL, D), lambda i: (i, 0))
    vec = pl.BlockSpec((1, D), lambda i: (0, 0))
    return pl.pallas_call(
        body, name="final_loss", grid=(L // TL,),
        in_specs=[row, row, vec],
        out_specs=[pl.BlockSpec((1, 1), lambda i: (0, 0)), row, vec],
        out_shape=[jax.ShapeDtypeStruct((1, 1), F32), jax.ShapeDtypeStruct((L, D), F32),
                   jax.ShapeDtypeStruct((1, D), F32)],
        compiler_params=_cp(("arbitrary",)),
    )(x2, tgt, g)


def _rope_tables():
    half = HD // 8
    inv_freq = jnp.power(ROPE_THETA, -jnp.arange(half, dtype=F32) / half)
    ang = jnp.arange(L, dtype=F32)[:, None] * inv_freq[None, :]
    cos, sin = jnp.cos(ang), jnp.sin(ang)
    one = jnp.ones((L, HD - 2 * half), F32)
    zero = jnp.zeros((L, HD - 2 * half), F32)
    zh = jnp.zeros((L, half), F32)
    cos64 = jnp.concatenate([cos, cos, one], axis=1)
    sa64 = jnp.concatenate([-sin, zh, zero], axis=1)
    sb64 = jnp.concatenate([zh, sin, zero], axis=1)
    return [jnp.tile(t, (1, 2)) for t in (cos64, sa64, sb64)]


def _rope(xv, cosv, sav, sbv, sign):
    return xv * cosv + sign * (pltpu.roll(xv, 120, 1) * sav + pltpu.roll(xv, 8, 1) * sbv)


def _rope_fwd(proj, tabs):
    qkw = AW + 2 * KVW

    def body(p_ref, c_ref, sa_ref, sb_ref, o_ref):
        for j in range(qkw // 128):
            cols = slice(j * 128, (j + 1) * 128)
            xv = p_ref[:, cols]
            if j < (AW + KVW) // 128:
                xv = _rope(xv, c_ref[...], sa_ref[...], sb_ref[...], 1.0)
            o_ref[:, cols] = xv.astype(BF16)

    blk = pl.BlockSpec((TL, qkw), lambda i: (i, 0))
    tab = pl.BlockSpec((TL, 128), lambda i: (i, 0))
    return pl.pallas_call(
        body, name="rope_fwd", grid=(L // TL,),
        in_specs=[blk, tab, tab, tab], out_specs=blk,
        out_shape=jax.ShapeDtypeStruct((L, qkw), BF16),
        compiler_params=_cp(("parallel",)),
    )(proj, *tabs)


def _rope_bwd(dq, dk, dv, du, tabs):
    def body(dq_ref, dk_ref, dv_ref, du_ref, c_ref, sa_ref, sb_ref, o_ref):
        for j in range(AW // 128):
            cols = slice(j * 128, (j + 1) * 128)
            o_ref[:, cols] = _rope(dq_ref[:, cols], c_ref[...], sa_ref[...], sb_ref[...], -1.0).astype(BF16)
        o_ref[:, AW:AW + KVW] = _rope(dk_ref[...], c_ref[...], sa_ref[...], sb_ref[...], -1.0).astype(BF16)
        o_ref[:, AW + KVW:AW + 2 * KVW] = dv_ref[...].astype(BF16)
        o_ref[:, AW + 2 * KVW:] = du_ref[...].astype(BF16)

    def row(width):
        return pl.BlockSpec((TL, width), lambda i: (i, 0))

    return pl.pallas_call(
        body, name="rope_bwd", grid=(L // TL,),
        in_specs=[row(AW), row(KVW), row(KVW), row(SW), row(128), row(128), row(128)],
        out_specs=row(INW), out_shape=jax.ShapeDtypeStruct((L, INW), BF16),
        compiler_params=_cp(("parallel",)),
    )(dq, dk, dv, du, *tabs)


def _attn_window(n):
    start = pl.multiple_of(jnp.clip((n - 1) * BLK, 0, L - WIN), BLK)
    qpos = n * BLK + lax.broadcasted_iota(jnp.int32, (BLK, WIN), 0)
    kpos = start + lax.broadcasted_iota(jnp.int32, (BLK, WIN), 1)
    return start, jnp.abs(kpos - qpos) <= BLK


_NT = (((1,), (1,)), ((), ()))
_TN = (((0,), (0,)), ((), ()))
NEG = -1e30


def _attn_fwd(qkv, sink, gather=()):
    ng = len(gather)

    def body(sink_ref, q_ref, k_ref, v_ref, *rest):
        o_ref, lse_ref = rest[ng], rest[ng + 1]
        n = pl.program_id(0)
        if ng:
            travellers = (rest[:ng], rest[ng + 2:2 * ng + 2], *rest[2 * ng + 2:])

            @pl.when(n == 0)
            def _():
                _gather_start(*travellers)

            @pl.when(n == L // BLK - 1)
            def _():
                _gather_finish(*travellers)

        start, valid = _attn_window(n)
        kw = k_ref[pl.ds(start, WIN), :]
        vw = v_ref[pl.ds(start, WIN), :]
        for h in range(NQ):
            kv = h // (NQ // NKV)
            qh = q_ref[:, h * HD:(h + 1) * HD]
            kh = kw[:, kv * HD:(kv + 1) * HD]
            vh = vw[:, kv * HD:(kv + 1) * HD]
            s = lax.dot_general(qh, kh, _NT, preferred_element_type=F32) * (HD ** -0.5)
            s = jnp.where(valid, s, NEG)
            sk = sink_ref[h]
            m = jnp.maximum(jnp.max(s, axis=-1, keepdims=True), sk)
            p = jnp.exp(s - m)
            den = jnp.sum(p, axis=-1, keepdims=True) + jnp.exp(sk - m)
            o_ref[:, h * HD:(h + 1) * HD] = jnp.dot((p / den).astype(BF16), vh, preferred_element_type=F32)
            lse_ref[:, h:h + 1] = m + jnp.log(den)

    outs = pl.pallas_call(
        body, name="attn_fwd", grid=(L // BLK,),
        in_specs=[pl.BlockSpec(memory_space=pltpu.SMEM),
                  pl.BlockSpec((BLK, AW), lambda n: (n, 0)),
                  pl.BlockSpec((L, KVW), lambda n: (0, AW // KVW)),
                  pl.BlockSpec((L, KVW), lambda n: (0, AW // KVW + 1))] + [ANY] * ng,
        out_specs=[pl.BlockSpec((BLK, AW), lambda n: (n, 0)), pl.BlockSpec((BLK, NQ), lambda n: (n, 0))] + [ANY] * ng,
        out_shape=[jax.ShapeDtypeStruct((L, AW), F32), jax.ShapeDtypeStruct((L, NQ), F32)] + _gather_shapes(gather),
        scratch_shapes=_gather_sems(ng) if ng else [],
        compiler_params=_cp(("arbitrary",) if ng else ("parallel",)),
    )(sink, qkv, qkv, qkv, *gather)
    return outs[0], outs[1], list(outs[2:])


def _attn_bwd(qkv, sink, attn, lse, dattn):
    def body(sink_ref, q_ref, k_ref, v_ref, o_ref, lse_ref, do_ref, dq_ref, dk_ref, dv_ref, dsink_ref):
        n = pl.program_id(0)

        @pl.when(n == 0)
        def _():
            dk_ref[...] = jnp.zeros_like(dk_ref)
            dv_ref[...] = jnp.zeros_like(dv_ref)
            dsink_ref[...] = jnp.zeros_like(dsink_ref)

        start, valid = _attn_window(n)
        kw = k_ref[pl.ds(start, WIN), :]
        vw = v_ref[pl.ds(start, WIN), :]
        for kv in range(NKV):
            kh = kw[:, kv * HD:(kv + 1) * HD]
            vh = vw[:, kv * HD:(kv + 1) * HD]
            dk_acc = jnp.zeros((WIN, HD), F32)
            dv_acc = jnp.zeros((WIN, HD), F32)
            for h in range(kv * (NQ // NKV), (kv + 1) * (NQ // NKV)):
                qh = q_ref[:, h * HD:(h + 1) * HD]
                doh = do_ref[:, h * HD:(h + 1) * HD]
                dd = jnp.sum(doh * o_ref[:, h * HD:(h + 1) * HD], axis=-1, keepdims=True)
                lse_h = lse_ref[:, h:h + 1]
                s = lax.dot_general(qh, kh, _NT, preferred_element_type=F32) * (HD ** -0.5)
                p = jnp.where(valid, jnp.exp(s - lse_h), 0.0)
                dob = doh.astype(BF16)
                dp = lax.dot_general(dob, vh, _NT, preferred_element_type=F32)
                ds = (p * (dp - dd) * (HD ** -0.5)).astype(BF16)
                dq_ref[:, h * HD:(h + 1) * HD] = jnp.dot(ds, kh, preferred_element_type=F32)
                dk_acc += lax.dot_general(ds, qh, _TN, preferred_element_type=F32)
                dv_acc += lax.dot_general(p.astype(BF16), dob, _TN, preferred_element_type=F32)
                psink = jnp.exp(sink_ref[h] - lse_h)
                dsk = -jnp.sum(psink * dd, axis=0, keepdims=True)
                dsink_ref[h:h + 1, :] += jnp.broadcast_to(dsk, (1, 128))
            dk_ref[pl.ds(start, WIN), kv * HD:(kv + 1) * HD] += dk_acc
            dv_ref[pl.ds(start, WIN), kv * HD:(kv + 1) * HD] += dv_acc

    qblk = pl.BlockSpec((BLK, AW), lambda n: (n, 0))
    full = pl.BlockSpec((L, KVW), lambda n: (0, 0))
    return pl.pallas_call(
        body, name="attn_bwd", grid=(L // BLK,),
        in_specs=[pl.BlockSpec(memory_space=pltpu.SMEM), qblk,
                  pl.BlockSpec((L, KVW), lambda n: (0, AW // KVW)),
                  pl.BlockSpec((L, KVW), lambda n: (0, AW // KVW + 1)),
                  qblk, pl.BlockSpec((BLK, NQ), lambda n: (n, 0)), qblk],
        out_specs=[qblk, full, full, pl.BlockSpec((NQ, 128), lambda n: (0, 0))],
        out_shape=[jax.ShapeDtypeStruct((L, AW), F32), jax.ShapeDtypeStruct((L, KVW), F32),
                   jax.ShapeDtypeStruct((L, KVW), F32), jax.ShapeDtypeStruct((NQ, 128), F32)],
        compiler_params=_cp(("arbitrary",)),
    )(sink, qkv, qkv, qkv, attn, lse, dattn)


def _perm(a):
    return a.reshape(NSEG, TSEG, a.shape[1]).transpose(1, 0, 2).reshape(L, a.shape[1])


def _unperm(a):
    return a.reshape(TSEG, NSEG, a.shape[1]).transpose(1, 0, 2).reshape(L, a.shape[1])


def _cmul(ar, ai, br, bi):
    return ar * br - ai * bi, ar * bi + ai * br


def _scan_inplace(s_ref, lr, li, rev, visit=None, carried=()):
    n = lr.shape[1]
    lr8 = jnp.broadcast_to(lr, (NSEG, n))
    li8 = jnp.broadcast_to(li, (NSEG, n))

    def rows(k):
        return pl.ds(pl.multiple_of(jnp.where(rev, TSEG - 1 - k, k) * NSEG, NSEG), NSEG)

    def step(k, c, store):
        sr, si = c
        rs = rows(k)
        pr, pi = _cmul(lr8, li8, sr, si)
        nr = pr + s_ref[rs, 0:n]
        ni = pi + s_ref[rs, n:2 * n]
        if store:
            s_ref[rs, 0:n] = nr
            s_ref[rs, n:2 * n] = ni
        return nr, ni

    z = jnp.zeros((NSEG, n), F32)
    er, ei = lax.fori_loop(0, TSEG, functools.partial(step, store=False), (z, z))
    pr, pi = lr, li
    for _ in range(int(math.log2(TSEG))):
        pr, pi = _cmul(pr, pi, pr, pi)

    seg = lax.broadcasted_iota(jnp.int32, (NSEG, n), 0)

    def chain(order):
        cr = jnp.zeros((1, n), F32)
        ci = jnp.zeros((1, n), F32)
        outr = jnp.zeros((NSEG, n), F32)
        outi = jnp.zeros((NSEG, n), F32)
        for s in order:
            outr = jnp.where(seg == s, cr, outr)
            outi = jnp.where(seg == s, ci, outi)
            mr, mi = _cmul(pr, pi, cr, ci)
            cr, ci = mr + er[s:s + 1], mi + ei[s:s + 1]
        return outr, outi

    fr, fi = chain(range(NSEG))
    rr, ri = chain(range(NSEG - 1, -1, -1))
    cin_r = jnp.where(rev, rr, fr)
    cin_i = jnp.where(rev, ri, fi)
    if visit is None:
        lax.fori_loop(0, TSEG, functools.partial(step, store=True), (cin_r, cin_i))
        return cin_r, cin_i

    def visited(k, c):
        nr, ni = step(k, c[:2], True)
        return (nr, ni) + tuple(visit(k, nr, ni, c[2:]))

    fin = lax.fori_loop(0, TSEG - 1, visited, (cin_r, cin_i) + tuple(carried))
    last_r, last_i = step(TSEG - 1, fin[:2], True)
    return last_r, last_i, fin[2:]


S5_RC = 512


def _s5_specs():
    u_spec = pl.BlockSpec((L, 128), lambda cb, h, d: (0, cb))
    b_spec = pl.BlockSpec((None, None, 128, 2 * SBW), lambda cb, h, d: (d, cb * 2 + h, 0, 0))
    c_spec = pl.BlockSpec((None, None, 2 * SBW, 128), lambda cb, h, d: (d, cb * 2 + h, 0, 0))
    l_spec = pl.BlockSpec((None, None, 1, SBW), lambda cb, h, d: (d, cb * 2 + h, 0, 0))
    d_spec = pl.BlockSpec((1, 128), lambda cb, h, d: (0, cb))
    return u_spec, b_spec, c_spec, l_spec, d_spec


def _s5_input_states(u_ref, b_ref, lr_ref, li_ref, s_scr, rev):
    def proj(i, _):
        rs = pl.ds(pl.multiple_of(i * S5_RC, S5_RC), S5_RC)
        s_scr[rs, :] = jnp.dot(u_ref[rs, :].astype(BF16), b_ref[...], preferred_element_type=F32)
        return 0

    lax.fori_loop(0, L // S5_RC, proj, 0)
    return _scan_inplace(s_scr, lr_ref[...], li_ref[...], rev)


def _s5_fwd(u_p, bcat, ccat, lam_re, lam_im, dskip):
    def body(u_ref, b_ref, c_ref, lr_ref, li_ref, d_ref, y_ref, sb_ref, s_scr):
        first = (pl.program_id(1) == 0) & (pl.program_id(2) == 0)
        _s5_input_states(u_ref, b_ref, lr_ref, li_ref, s_scr, pl.program_id(2) == 1)

        def out(i, _):
            rs = pl.ds(pl.multiple_of(i * S5_RC, S5_RC), S5_RC)
            sb = s_scr[rs, :].astype(BF16)
            sb_ref[rs, :] = sb
            yv = jnp.dot(sb, c_ref[...], preferred_element_type=F32)

            @pl.when(first)
            def _():
                y_ref[rs, :] = d_ref[...] * u_ref[rs, :] + yv

            @pl.when(jnp.logical_not(first))
            def _():
                y_ref[rs, :] += yv

            return 0

        lax.fori_loop(0, L // S5_RC, out, 0)

    u_spec, b_spec, c_spec, l_spec, d_spec = _s5_specs()
    return pl.pallas_call(
        body, name="s5_fwd", grid=(SW // 128, 2, 2),
        in_specs=[u_spec, b_spec, c_spec, l_spec, l_spec, d_spec],
        out_specs=[u_spec, _s5_state_spec()],
        out_shape=[jax.ShapeDtypeStruct((L, SW), F32), jax.ShapeDtypeStruct((2, NSB, L, 2 * SBW), BF16)],
        scratch_shapes=[pltpu.VMEM((L, 2 * SBW), F32)],
        compiler_params=_cp(("parallel", "arbitrary", "arbitrary")),
    )(u_p, bcat, ccat, lam_re, lam_im, dskip)


def _s5_state_spec():
    return pl.BlockSpec((None, None, L, 2 * SBW), lambda cb, h, d: (d, cb * 2 + h, 0, 0))


def _s5_bwd(u_p, dy_p, states, bcat, ccat, lam_re, lam_im, dskip):
    def body(u_ref, dy_ref, s_ref, b_ref, c_ref, lr_ref, li_ref, d_ref,
             du_ref, db_ref, dc_ref, dlr_ref, dli_ref, dd_ref, g_scr):
        first = (pl.program_id(1) == 0) & (pl.program_id(2) == 0)
        rev = pl.program_id(2) == 1

        def dstate(i, _):
            rs = pl.ds(pl.multiple_of(i * S5_RC, S5_RC), S5_RC)
            g_scr[rs, :] = lax.dot_general(dy_ref[rs, :].astype(BF16), c_ref[...], _NT, preferred_element_type=F32)
            return 0

        lax.fori_loop(0, L // S5_RC, dstate, 0)

        def before(rows):
            sv = s_ref[rows, :].astype(F32)
            return sv[:, 0:SBW], sv[:, SBW:2 * SBW]

        def dlam(gr, gi, sr, si, ar, ai):
            return ar + gr * sr + gi * si, ai + gi * sr - gr * si

        def visit(k, gr, gi, acc):
            ts = jnp.where(rev, k + 1, TSEG - 2 - k)
            sr, si = before(pl.ds(pl.multiple_of(ts * NSEG, NSEG), NSEG))
            return dlam(gr, gi, sr, si, *acc)

        z = jnp.zeros((NSEG, SBW), F32)
        gr, gi, acc = _scan_inplace(g_scr, lr_ref[...], -li_ref[...], jnp.logical_not(rev), visit, (z, z))
        edge_r, edge_i = before(pl.ds(pl.multiple_of(jnp.where(rev, 0, TSEG - 1) * NSEG, NSEG), NSEG))
        seg = lax.broadcasted_iota(jnp.int32, (NSEG, SBW), 0)
        keep = seg != jnp.where(rev, NSEG - 1, 0)

        def neighbour(e):
            return jnp.where(keep, jnp.where(rev, pltpu.roll(e, NSEG - 1, 0), pltpu.roll(e, 1, 0)), 0.0)

        ar, ai = dlam(gr, gi, neighbour(edge_r), neighbour(edge_i), *acc)
        dlr_ref[...] = jnp.sum(ar, axis=0, keepdims=True)
        dli_ref[...] = jnp.sum(ai, axis=0, keepdims=True)

        db_ref[...] = jnp.zeros_like(db_ref)
        dc_ref[...] = jnp.zeros_like(dc_ref)

        @pl.when(first)
        def _():
            dd_ref[...] = jnp.zeros_like(dd_ref)

        def grads(i, _):
            rs = pl.ds(pl.multiple_of(i * S5_RC, S5_RC), S5_RC)
            uv = u_ref[rs, :]
            dyv = dy_ref[rs, :]
            gb = g_scr[rs, :].astype(BF16)
            db_ref[...] += lax.dot_general(uv.astype(BF16), gb, _TN, preferred_element_type=F32)
            dc_ref[...] += lax.dot_general(s_ref[rs, :], dyv.astype(BF16), _TN, preferred_element_type=F32)
            duv = lax.dot_general(gb, b_ref[...], _NT, preferred_element_type=F32)

            @pl.when(first)
            def _():
                du_ref[rs, :] = d_ref[...] * dyv + duv
                dd_ref[...] += jnp.sum(dyv * uv, axis=0, keepdims=True)

            @pl.when(jnp.logical_not(first))
            def _():
                du_ref[rs, :] += duv

            return 0

        lax.fori_loop(0, L // S5_RC, grads, 0)

    u_spec, b_spec, c_spec, l_spec, d_spec = _s5_specs()
    return pl.pallas_call(
        body, name="s5_bwd", grid=(SW // 128, 2, 2),
        in_specs=[u_spec, u_spec, _s5_state_spec(), b_spec, c_spec, l_spec, l_spec, d_spec],
        out_specs=[u_spec, b_spec, c_spec, l_spec, l_spec, d_spec],
        out_shape=[jax.ShapeDtypeStruct((L, SW), F32),
                   jax.ShapeDtypeStruct((2, NSB, 128, 2 * SBW), F32), jax.ShapeDtypeStruct((2, NSB, 2 * SBW, 128), F32),
                   jax.ShapeDtypeStruct((2, NSB, 1, SBW), F32), jax.ShapeDtypeStruct((2, NSB, 1, SBW), F32),
                   jax.ShapeDtypeStruct((1, SW), F32)],
        scratch_shapes=[pltpu.VMEM((L, 2 * SBW), F32)],
        compiler_params=_cp(("parallel", "arbitrary", "arbitrary")),
    )(u_p, dy_p, states, bcat, ccat, lam_re, lam_im, dskip)


def _s5_params(a_re, a_im, log_step, bt_re, bt_im):
    lam = lax.complex(a_re, a_im)
    step = jnp.exp(log_step)[..., None]
    lam_bar = jnp.exp(lam * step)
    b_bar = ((lam_bar - 1.0) / lam)[..., None, :] * lax.complex(bt_re, bt_im)
    return jnp.real(lam_bar), jnp.imag(lam_bar), jnp.real(b_bar), jnp.imag(b_bar)


def _sel():
    i = jnp.arange(8)[None, :, None]
    j = jnp.arange(4)[None, None, :]
    r = jnp.arange(2)[:, None, None]
    return (i == r * 4 + j).astype(F32)


def _to_bcat(bt_re, bt_im):
    def one(bt):
        return jnp.einsum('dkrjcp,rij->dkricjp', bt.reshape(2, 4, 2, 4, GC, NP), _sel()).reshape(2, NSB, 128, SBW)
    return jnp.concatenate([one(bt_re), one(bt_im)], axis=-1)


def _from_bcat(dbcat):
    def one(dbbd):
        return jnp.einsum('dkricjp,rij->dkrjcp', dbbd.reshape(2, 4, 2, 8, GC, 4, NP), _sel()).reshape(2, NG, GC, NP)
    return one(dbcat[..., :SBW]), one(dbcat[..., SBW:])


def _to_ccat(c_re, c_im):
    def one(cc):
        return jnp.einsum('dkrjcp,rij->dkrjpic', cc.reshape(2, 4, 2, 4, GC, NP), _sel()).reshape(2, NSB, SBW, 128)
    return jnp.concatenate([one(c_re), -one(c_im)], axis=-2)


def _from_ccat(dccat):
    def one(dcbd):
        return jnp.einsum('dkrjpic,rij->dkrjcp', dcbd.reshape(2, 4, 2, 4, NP, 8, GC), _sel()).reshape(2, NG, GC, NP)
    return one(dccat[:, :, :SBW]), -one(dccat[:, :, SBW:])


def _gelu(y):
    return 0.5 * y * (1.0 + lax.erf(y * (2.0 ** -0.5)))


def _gelu_grad(y):
    return 0.5 * (1.0 + lax.erf(y * (2.0 ** -0.5))) + y * jnp.exp(-0.5 * y * y) * ((2.0 * math.pi) ** -0.5)


def _sigmoid(z):
    return 0.5 * jnp.tanh(0.5 * z) + 0.5


def _glu_fwd(y, wg):
    def body(y_ref, w_ref, o_ref, z_ref):
        ys = _gelu(y_ref[...])
        z = jnp.dot(ys.astype(BF16), w_ref[...], preferred_element_type=F32)
        z_ref[...] = z
        o_ref[...] = ys * _sigmoid(z)

    row = pl.BlockSpec((TL, SW), lambda i: (i, 0))
    return pl.pallas_call(
        body, name="glu_fwd", grid=(L // TL,),
        in_specs=[row, pl.BlockSpec((SW, SW), lambda i: (0, 0))], out_specs=[row, row],
        out_shape=[jax.ShapeDtypeStruct((L, SW), F32), jax.ShapeDtypeStruct((L, SW), F32)],
        compiler_params=_cp(("parallel",)),
    )(y, wg)


def _glu_bwd(y, z, dout, wg):
    def body(y_ref, z_ref, do_ref, w_ref, dy_ref, dw_ref):
        @pl.when(pl.program_id(0) == 0)
        def _():
            dw_ref[...] = jnp.zeros_like(dw_ref)

        yv = y_ref[...]
        ys = _gelu(yv)
        sg = _sigmoid(z_ref[...])
        dov = do_ref[...]
        dz = (dov * ys * sg * (1.0 - sg)).astype(BF16)
        dys = dov * sg + lax.dot_general(dz, w_ref[...], _NT, preferred_element_type=F32)
        dy_ref[...] = dys * _gelu_grad(yv)
        dw_ref[...] += lax.dot_general(ys.astype(BF16), dz, _TN, preferred_element_type=F32)

    row = pl.BlockSpec((TL, SW), lambda i: (i, 0))
    wsp = pl.BlockSpec((SW, SW), lambda i: (0, 0))
    return pl.pallas_call(
        body, name="glu_bwd", grid=(L // TL,),
        in_specs=[row, row, row, wsp], out_specs=[row, wsp],
        out_shape=[jax.ShapeDtypeStruct((L, SW), F32), jax.ShapeDtypeStruct((SW, SW), F32)],
        compiler_params=_cp(("arbitrary",)),
    )(y, z, dout, wg)


CT = 256
CR = 128
NCT = DFF // CT


def _shifted(ref, r):
    cur = ref[pl.ds(r, CR), :]
    before = ref[pl.ds(pl.multiple_of(jnp.maximum(r - 8, 0), 8), 8), :][7:8, :]
    after = ref[pl.ds(pl.multiple_of(jnp.minimum(r + CR, L - 8), 8), 8), :][0:1, :]
    before = jnp.where(r > 0, before, 0.0)
    after = jnp.where(r + CR < L, after, 0.0)
    row = lax.broadcasted_iota(jnp.int32, cur.shape, 0)
    prev = jnp.where(row == 0, before, pltpu.roll(cur, 1, 0))
    nxt = jnp.where(row == CR - 1, after, pltpu.roll(cur, CR - 1, 0))
    return prev, cur, nxt


def _conv3(ref, r, w_ref, b_ref):
    prev, cur, nxt = _shifted(ref, r)
    return w_ref[0:1, :] * prev + w_ref[1:2, :] * cur + w_ref[2:3, :] * nxt + b_ref[...]


def _convact_fwd(up, conv_w, conv_b):
    def body(ug_ref, uv_ref, wg_ref, wv_ref, bg_ref, bv_ref, o_ref, g_ref, v_ref):
        def chunk(i, _):
            r = pl.multiple_of(i * CR, CR)
            rs = pl.ds(r, CR)
            g = _conv3(ug_ref, r, wg_ref, bg_ref)
            v = _conv3(uv_ref, r, wv_ref, bv_ref)
            o_ref[rs, :] = (g * _sigmoid(g) * v).astype(BF16)
            g_ref[rs, :] = g.astype(BF16)
            v_ref[rs, :] = v.astype(BF16)
            return 0

        lax.fori_loop(0, L // CR, chunk, 0)

    gcol = pl.BlockSpec((L, CT), lambda j: (0, j))
    vcol = pl.BlockSpec((L, CT), lambda j: (0, j + NCT))
    return pl.pallas_call(
        body, name="convact_fwd", grid=(NCT,),
        in_specs=[gcol, vcol,
                  pl.BlockSpec((3, CT), lambda j: (0, j)), pl.BlockSpec((3, CT), lambda j: (0, j + NCT)),
                  pl.BlockSpec((1, CT), lambda j: (0, j)), pl.BlockSpec((1, CT), lambda j: (0, j + NCT))],
        out_specs=[gcol, gcol, gcol], out_shape=[jax.ShapeDtypeStruct((L, DFF), BF16)] * 3,
        compiler_params=_cp(("parallel",)),
    )(up, up, conv_w, conv_w, conv_b, conv_b)


def _convact_bwd(up, gq, vq, dact, conv_w):
    def body(ug_ref, uv_ref, g_ref, v_ref, da_ref, wg_ref, wv_ref, du_ref, dw_ref, db_ref, dgs, dvs, dbv):
        half = pl.program_id(1)

        def transpose_conv(src, u_ref, w_ref):
            dw_ref[...] = jnp.zeros_like(dw_ref)

            def chunk(i, _):
                r = pl.multiple_of(i * CR, CR)
                rs = pl.ds(r, CR)
                prev, cur, nxt = _shifted(src, r)
                du_ref[rs, :] = (w_ref[0:1, :] * nxt + w_ref[1:2, :] * cur + w_ref[2:3, :] * prev).astype(BF16)
                uv = u_ref[rs, :]
                for k, d in enumerate((nxt, cur, prev)):
                    dw_ref[k:k + 1, :] += jnp.sum(d * uv, axis=0, keepdims=True)
                return 0

            lax.fori_loop(0, L // CR, chunk, 0)

        @pl.when(half == 0)
        def _():
            db_ref[...] = jnp.zeros_like(db_ref)
            dbv[...] = jnp.zeros_like(dbv)

            def chunk1(i, _):
                rs = pl.ds(pl.multiple_of(i * CR, CR), CR)
                g = g_ref[rs, :].astype(F32)
                v = v_ref[rs, :].astype(F32)
                sg = _sigmoid(g)
                da = da_ref[rs, :]
                dv = da * g * sg
                dg = da * v * sg * (1.0 + g * (1.0 - sg))
                dgs[rs, :] = dg
                dvs[rs, :] = dv
                db_ref[...] += jnp.sum(dg, axis=0, keepdims=True)
                dbv[0:1, :] += jnp.sum(dv, axis=0, keepdims=True)
                return 0

            lax.fori_loop(0, L // CR, chunk1, 0)
            transpose_conv(dgs, ug_ref, wg_ref)

        @pl.when(half == 1)
        def _():
            db_ref[...] = dbv[0:1, :]
            transpose_conv(dvs, uv_ref, wv_ref)

    def col(rows, off):
        return pl.BlockSpec((rows, CT), lambda j, h: (0, j + off))

    def out(rows):
        return pl.BlockSpec((rows, CT), lambda j, h: (0, j + h * NCT))

    return pl.pallas_call(
        body, name="convact_bwd", grid=(NCT, 2),
        in_specs=[col(L, 0), col(L, NCT), col(L, 0), col(L, 0), col(L, 0), col(3, 0), col(3, NCT)],
        out_specs=[out(L), out(3), out(1)],
        out_shape=[jax.ShapeDtypeStruct((L, 2 * DFF), BF16), jax.ShapeDtypeStruct((3, 2 * DFF), F32),
                   jax.ShapeDtypeStruct((1, 2 * DFF), F32)],
        scratch_shapes=[pltpu.VMEM((L, CT), F32), pltpu.VMEM((L, CT), F32), pltpu.VMEM((8, CT), F32)],
        compiler_params=_cp(("parallel", "arbitrary")),
    )(up, up, gq, vq, dact, conv_w, conv_w)


def _local_step(x, tgt, w_in_t, w_glu, w_out, conv_w, p, attend, reduce_early):
    tabs = _rope_tables()
    lam_re, lam_im, bb_re, bb_im = _s5_params(p["a_re"], p["a_im"], p["log_step"], p["bt_re"], p["bt_im"])
    bcat = _to_bcat(bb_re, bb_im).astype(BF16)
    ccat = _to_ccat(p["c_re"], p["c_im"]).astype(BF16)
    lam_re4, lam_im4 = lam_re.reshape(2, NSB, 1, SBW), lam_im.reshape(2, NSB, 1, SBW)
    dskip = p["d_skip"].reshape(1, SW)
    g_mix, g_ffn, g_fin = p["norm_mix_g"].reshape(1, D), p["norm_ffn_g"].reshape(1, D), p["norm_final_g"].reshape(1, D)
    g_attn, g_ssm = p["norm_attn_g"].reshape(1, AW), p["norm_ssm_g"].reshape(1, SW)
    sink = p["sink"].reshape(NQ)
    conv_b = p["conv_b"].reshape(1, 2 * DFF)

    h1 = _rms_fwd(x, g_mix, "norm_mix_fwd")
    proj = _mm(h1, w_in_t, tb=True, name="in_proj", tn=1280)
    qkv = _rope_fwd(proj, tabs)
    attn, lse, w_up_t, w_down = attend(qkv, sink)
    u_p = _perm(proj[:, AW + 2 * KVW:])
    y_p, states = _s5_fwd(u_p, bcat, ccat, lam_re4, lam_im4, dskip)
    ysg_p, z_p = _glu_fwd(y_p, w_glu)
    ysg = _unperm(ysg_p)
    mixed = _mixnorm_fwd(attn, ysg, g_attn, g_ssm)
    x1 = _mm(mixed, w_out, add=x, name="out_proj")
    h2 = _rms_fwd(x1, g_ffn, "norm_ffn_fwd")
    up = _mm(h2, w_up_t, tb=True, name="ffn_up", tn=1408)
    act, gq, vq = _convact_fwd(up, conv_w, conv_b)
    x2 = _mm(act, w_down, add=x1, name="ffn_down", tk=1408)
    loss, dx2, dg_fin = _final_loss(x2, tgt, g_fin)

    dact = _mm(dx2, w_down, tb=True, name="ffn_down_dx", tn=1408)
    dw_down = _mm(act, dx2, ta=True, name="ffn_down_dw", tm=1408)
    ride = reduce_early("w_down", dw_down)
    dup, dconv_w, dconv_b = _convact_bwd(up, gq, vq, dact, conv_w)
    res = _mm(dup, h2, ta=True, name="ffn_up_dw", tm=1408, ride=ride)
    dw_up_t, got_down = res if ride else (res, [])
    ride = reduce_early("w_up_t", dw_up_t)
    res = _mm(dup, w_up_t, name="ffn_up_dx", tk=1408, ride=ride)
    dh2, got_up = res if ride else (res, [])
    dx1, dg_ffn = _rms_bwd(x1, g_ffn, dh2, dx2, "norm_ffn_bwd")
    dmixed = _mm(dx1, w_out, tb=True, name="out_proj_dx")
    dw_out = _mm(mixed, dx1, ta=True, name="out_proj_dw")
    dattn, dysg, dg_attn, dg_ssm = _mixnorm_bwd(attn, ysg, g_attn, g_ssm, dmixed)
    dy_p, dw_glu = _glu_bwd(y_p, z_p, _perm(dysg), w_glu)
    du_p, dbcat, dccat, dlam_re, dlam_im, dd = _s5_bwd(u_p, dy_p, states, bcat, ccat, lam_re4, lam_im4, dskip)
    dbb_re, dbb_im = _from_bcat(dbcat)
    dc_re, dc_im = _from_ccat(dccat)
    dq, dk, dv, dsink = _attn_bwd(qkv, sink, attn, lse, dattn)
    dproj = _rope_bwd(dq, dk, dv, _unperm(du_p), tabs)
    dw_in_t = _mm(dproj, h1, ta=True, name="in_proj_dw", tm=1280)
    dh1 = _mm(dproj, w_in_t, name="in_proj_dx", tk=1280)
    grad_x, dg_mix = _rms_bwd(x, g_mix, dh1, dx1, "norm_mix_bwd")

    big = dict(w_in_t=dw_in_t, w_glu=dw_glu, w_out=dw_out)
    early = dict(w_down=got_down, w_up_t=got_up)
    small = dict(norm_mix_g=dg_mix, norm_attn_g=dg_attn, norm_ssm_g=dg_ssm, norm_ffn_g=dg_ffn, norm_final_g=dg_fin,
                 sink=dsink[:, 0], conv_b=dconv_b, d_skip=dd, conv_w=dconv_w,
                 lam_re=dlam_re.reshape(2, NG, NP), lam_im=dlam_im.reshape(2, NG, NP),
                 bb_re=dbb_re, bb_im=dbb_im, c_re=dc_re, c_im=dc_im, loss=loss.reshape(1))
    return grad_x, big, small, early


ANY = pl.BlockSpec(memory_space=pl.ANY)


def _coords():
    return lax.axis_index("x"), lax.axis_index("y"), lax.axis_index("c")


def _flip(v, b):
    return v + b - 2 * v * b if b else v


def _all_gather(shards, name):
    n = len(shards)

    def body(*refs):
        _gather_start(refs[:n], refs[n:2 * n], *refs[2 * n:])
        _gather_finish(refs[:n], refs[n:2 * n], *refs[2 * n:])

    return pl.pallas_call(
        body, name=name,
        in_specs=[ANY] * n, out_specs=[ANY] * n,
        out_shape=_gather_shapes(shards), scratch_shapes=_gather_sems(n),
    )(*shards)


def _gather_shapes(shards):
    return [jax.ShapeDtypeStruct((NDEV * s.shape[0], s.shape[1]), s.dtype) for s in shards]


def _gather_sems(n):
    return [pltpu.SemaphoreType.DMA((7 * n,)), pltpu.SemaphoreType.DMA((7 * n,)), pltpu.SemaphoreType.DMA((n,))]


def _gather_copies(ins, outs, send_sems, recv_sems, local_sems, a):
    x, y, c = _coords()
    me, sibling = (x, y, c), (x, y, 1 - c)
    chips = [(1 - x, y), (x, 1 - y), (1 - x, 1 - y)]
    r = ins[a].shape[0]

    def rows(px, py, pc):
        return outs[a].at[pl.ds(pl.multiple_of((4 * px + 2 * py + pc) * r, 8), r), :]

    def copy(k, block, to, src=None):
        return pltpu.make_async_remote_copy(
            src_ref=rows(*block) if src is None else src, dst_ref=rows(*block),
            send_sem=send_sems.at[a * 7 + k], recv_sem=recv_sems.at[a * 7 + k],
            device_id=to, device_id_type=pl.DeviceIdType.MESH)

    mine = pltpu.make_async_copy(ins[a], rows(*me), local_sems.at[a])
    first = [copy(0, me, sibling, src=ins[a])]
    first += [copy(1 + j, me, (*chip, c), src=ins[a]) for j, chip in enumerate(chips)]
    passed = [copy(4 + j, (*chip, c), sibling) for j, chip in enumerate(chips)]
    arrivals = [copy(1 + j, (*chip, c), me) for j, chip in enumerate(chips)]
    from_sibling = [copy(0, sibling, me)] + [copy(4 + j, (*chip, 1 - c), me) for j, chip in enumerate(chips)]
    return mine, first, passed, arrivals, from_sibling


def _gather_start(ins, outs, send_sems, recv_sems, local_sems):
    for a in range(len(ins)):
        mine, first, _, _, _ = _gather_copies(ins, outs, send_sems, recv_sems, local_sems, a)
        mine.start()
        for cp in first:
            cp.start()


def _gather_finish(ins, outs, send_sems, recv_sems, local_sems):
    n = len(ins)
    parts = [_gather_copies(ins, outs, send_sems, recv_sems, local_sems, a) for a in range(n)]
    for mine, first, passed, arrivals, from_sibling in parts:
        for arrived, onward in zip(arrivals, passed):
            arrived.wait_recv()
            onward.start()
    for mine, first, passed, arrivals, from_sibling in parts:
        for cp in from_sibling:
            cp.wait_recv()
        for cp in first + passed:
            cp.wait_send()
        mine.wait()


NCHIP = 4
CHIP_FLIPS = ((1, 0), (0, 1), (1, 1))


def _planned_copies(ins, outs, send_sems, recv_sems, plan):
    return [pltpu.make_async_remote_copy(
        src_ref=ins[a].at[src], dst_ref=outs[a].at[dst], send_sem=send_sems.at[k], recv_sem=recv_sems.at[k],
        device_id=to, device_id_type=pl.DeviceIdType.MESH) for k, (a, src, dst, to) in enumerate(plan)]


def _start_all(copies):
    for cp in copies:
        cp.start()


def _wait_all(copies):
    for cp in copies:
        cp.wait_recv()
    for cp in copies:
        cp.wait_send()


def _exchange_cores(parts, name):
    n = len(parts)

    def body(*refs):
        x, y, c = _coords()
        plan = [(a, 2 * q + 1 - c, q, (x, y, 1 - c)) for a in range(n) for q in range(NCHIP)]
        copies = _planned_copies(refs[:n], refs[n:2 * n], *refs[2 * n:], plan)
        _start_all(copies)
        _wait_all(copies)

    return pl.pallas_call(
        body, name=name, in_specs=[ANY] * n, out_specs=[ANY] * n,
        out_shape=[jax.ShapeDtypeStruct((NCHIP,) + s.shape[1:], s.dtype) for s in parts],
        scratch_shapes=[pltpu.SemaphoreType.DMA((NCHIP * n,)), pltpu.SemaphoreType.DMA((NCHIP * n,))],
    )(*parts)


def _chips_copies(ins, outs, send_sems, recv_sems):
    x, y, c = _coords()
    plan = []
    for a in range(len(ins)):
        for j, (fx, fy) in enumerate(CHIP_FLIPS):
            px, py = _flip(x, fx), _flip(y, fy)
            plan.append((a, 2 * px + py, j, (px, py, c)))
    return _planned_copies(ins, outs, send_sems, recv_sems, plan)


def _chips_shapes(parts):
    return [jax.ShapeDtypeStruct((3,) + s.shape[1:], s.dtype) for s in parts]


def _chips_sems(n):
    return [pltpu.SemaphoreType.DMA((3 * n,)), pltpu.SemaphoreType.DMA((3 * n,))]


def _exchange_chips(parts, name):
    n = len(parts)

    def body(*refs):
        copies = _chips_copies(refs[:n], refs[n:2 * n], *refs[2 * n:])
        _start_all(copies)
        _wait_all(copies)

    return pl.pallas_call(
        body, name=name, in_specs=[ANY] * n, out_specs=[ANY] * n,
        out_shape=_chips_shapes(parts), scratch_shapes=_chips_sems(n),
    )(*parts)


def _pair_sum(where, part, recv, wire_dtype, name):
    _, r, c = part.shape
    tr = _pick(r, 256, 16)

    def body(w_ref, p_ref, r_ref, pb_ref, own_ref):
        s = p_ref[...] + r_ref[...]
        pb_ref[...] = s.astype(wire_dtype)

        @pl.when(pl.program_id(1) == w_ref[1])
        def _():
            own_ref[...] = s

    return pl.pallas_call(
        body, name=name,
        grid_spec=pltpu.PrefetchScalarGridSpec(
            num_scalar_prefetch=1, grid=(r // tr, NCHIP),
            in_specs=[pl.BlockSpec((None, tr, c), lambda i, q, w: (2 * q + w[0], i, 0)),
                      pl.BlockSpec((None, tr, c), lambda i, q, w: (q, i, 0))],
            out_specs=[pl.BlockSpec((None, tr, c), lambda i, q, w: (q, i, 0)),
                       pl.BlockSpec((tr, c), lambda i, q, w: (i, 0))]),
        out_shape=[jax.ShapeDtypeStruct((NCHIP, r, c), wire_dtype), jax.ShapeDtypeStruct((r, c), F32)],
        compiler_params=_cp(("parallel", "arbitrary")),
    )(where, part, recv)


def _chip_sum(own, recv, name):
    r, c = own.shape
    tr = _pick(r, 256, 16)

    def body(o_ref, r_ref, out_ref):
        acc = o_ref[...]
        for j in range(3):
            acc = acc + r_ref[j].astype(F32)
        out_ref[...] = acc

    return pl.pallas_call(
        body, name=name, grid=(r // tr,),
        in_specs=[pl.BlockSpec((tr, c), lambda i: (i, 0)), pl.BlockSpec((3, tr, c), lambda i: (0, i, 0))],
        out_specs=pl.BlockSpec((tr, c), lambda i: (i, 0)),
        out_shape=jax.ShapeDtypeStruct((r, c), F32),
        compiler_params=_cp(("parallel",)),
    )(own, recv)


def _adamw(w, g, m, v, name):
    r, c = w.shape
    tr = _pick(r, 256, 8)

    def body(w_ref, g_ref, m_ref, v_ref, d_ref, nm_ref, nv_ref):
        _adamw_refs(w_ref, g_ref, m_ref, v_ref, d_ref, nm_ref, nv_ref)

    blk = pl.BlockSpec((tr, c), lambda i: (i, 0))
    return pl.pallas_call(
        body, name=name, grid=(r // tr,),
        in_specs=[blk] * 4, out_specs=[blk] * 3,
        out_shape=[jax.ShapeDtypeStruct((r, c), F32)] * 3,
        compiler_params=_cp(("parallel",)),
    )(w, g, m, v)


def _adamw_refs(w_ref, g_ref, m_ref, v_ref, d_ref, nm_ref, nv_ref):
    gv = g_ref[...]
    nm = B1 * m_ref[...] + (1.0 - B1) * gv
    nv = B2 * v_ref[...] + (1.0 - B2) * (gv * gv)
    nm_ref[...] = nm
    nv_ref[...] = nv
    d_ref[...] = -LR * ((nm / C1) / (jnp.sqrt(nv / C2) + AEPS) + WD * w_ref[...])


def _adamw_small(ws, gs, ms, vs, name):
    n = len(ws)

    def body(*refs):
        groups = [refs[i * n:(i + 1) * n] for i in range(7)]
        for per_param in zip(*groups):
            _adamw_refs(*per_param)

    vm = pl.BlockSpec(memory_space=pltpu.VMEM)
    outs = pl.pallas_call(
        body, name=name, in_specs=[vm] * (4 * n), out_specs=[vm] * (3 * n),
        out_shape=[jax.ShapeDtypeStruct(a.shape, F32) for a in ws] * 3,
    )(*ws, *gs, *ms, *vs)
    return outs[:n], outs[n:2 * n], outs[2 * n:]


def _swap(a):
    return jnp.swapaxes(a, -1, -2)


VIEWS = {
    "w_in": (lambda a: a[0].T, lambda u: u.T[None]),
    "w_up": (lambda a: a[0].T, lambda u: u.T[None]),
    "w_glu": (lambda a: a[0], lambda u: u[None]),
    "w_out": (lambda a: a[0], lambda u: u[None]),
    "w_down": (lambda a: a[0], lambda u: u[None]),
    "conv_w": (lambda a: a[0], lambda u: u[None]),
    "norm_mix_g": (lambda a: a, lambda u: u),
    "norm_attn_g": (lambda a: a, lambda u: u),
    "norm_ssm_g": (lambda a: a, lambda u: u),
    "norm_ffn_g": (lambda a: a, lambda u: u),
    "norm_final_g": (lambda a: a[None], lambda u: u[0]),
    "conv_b": (lambda a: a, lambda u: u),
    "sink": (lambda a: a, lambda u: u),
    "a_re": (lambda a: a.reshape(2 * NG, NP), lambda u: u.reshape(1, 2, NG, NP)),
    "a_im": (lambda a: a.reshape(2 * NG, NP), lambda u: u.reshape(1, 2, NG, NP)),
    "log_step": (lambda a: a[0], lambda u: u[None]),
    "b_re": (lambda a: _swap(a[0]).reshape(2 * NG * GC, NP), lambda u: _swap(u.reshape(2, NG, GC, NP))[None]),
    "b_im": (lambda a: _swap(a[0]).reshape(2 * NG * GC, NP), lambda u: _swap(u.reshape(2, NG, GC, NP))[None]),
    "c_re": (lambda a: a.reshape(2 * NG * GC, NP), lambda u: u.reshape(1, 2, NG, GC, NP)),
    "c_im": (lambda a: a.reshape(2 * NG * GC, NP), lambda u: u.reshape(1, 2, NG, GC, NP)),
    "d_skip": (lambda a: a[0].T, lambda u: u.T[None]),
}
BIG = ["w_in", "w_glu", "w_out", "w_up", "w_down"]
PACK_W = 1024


def _pack(arrs, rows):
    flat = jnp.concatenate([a.reshape(-1).astype(F32) for a in arrs])
    return jnp.pad(flat, (0, rows * PACK_W - flat.shape[0])).reshape(rows, PACK_W)


def _unpack(packed, shapes):
    flat = packed.reshape(-1)
    out, off = [], 0
    for s in shapes:
        size = math.prod(s)
        out.append(flat[off:off + size].reshape(s))
        off += size
    return out


def kernel(x, norm_mix_g, w_in, a_re, a_im, log_step, b_re, b_im, c_re, c_im, d_skip, w_glu, sink, norm_attn_g, norm_ssm_g, w_out, norm_ffn_g, w_up, conv_w, conv_b, w_down, norm_final_g, loss_target, m_norm_mix_g, m_w_in, m_a_re, m_a_im, m_log_step, m_b_re, m_b_im, m_c_re, m_c_im, m_d_skip, m_w_glu, m_sink, m_norm_attn_g, m_norm_ssm_g, m_w_out, m_norm_ffn_g, m_w_up, m_conv_w, m_conv_b, m_w_down, m_norm_final_g, v_norm_mix_g, v_w_in, v_a_re, v_a_im, v_log_step, v_b_re, v_b_im, v_c_re, v_c_im, v_d_skip, v_w_glu, v_sink, v_norm_attn_g, v_norm_ssm_g, v_w_out, v_norm_ffn_g, v_w_up, v_conv_w, v_conv_b, v_w_down, v_norm_final_g):
    args = dict(locals())
    names = ["norm_mix_g", "w_in", "a_re", "a_im", "log_step", "b_re", "b_im", "c_re", "c_im", "d_skip", "w_glu",
             "sink", "norm_attn_g", "norm_ssm_g", "w_out", "norm_ffn_g", "w_up", "conv_w", "conv_b", "w_down",
             "norm_final_g"]
    w = {k: args[k] for k in names}
    m = {k: args["m_" + k] for k in names}
    v = {k: args["v_" + k] for k in names}

    shards = [w_in[0].T.astype(BF16), w_glu[0].astype(BF16), w_out[0].astype(BF16),
              jnp.pad(conv_w[0], ((0, 5), (0, 0)))]
    w_in_t, w_glu_f, w_out_f, conv_w_g = _all_gather(shards, "gather_weights")
    conv_w_f = conv_w_g.reshape(NDEV, 8, 2 * DFF // NDEV)[:, :3].transpose(1, 0, 2).reshape(3, 2 * DFF)
    ffn_shards = [w_up[0].T.astype(BF16), w_down[0].astype(BF16)]

    ax, ay, ac = _coords()
    me = 4 * ax + 2 * ay + ac
    where = jnp.stack([ac, 2 * ax + ay]).astype(jnp.int32)
    own = {}

    def to_chip_sums(named_parts, tag):
        ks = [k for k, _ in named_parts]
        parts = [part for _, part in named_parts]
        from_core = _exchange_cores(parts, "exchange_cores_" + tag)
        wire = []
        for k, part, got in zip(ks, parts, from_core):
            per_chip, own[k] = _pair_sum(where, part, got, F32 if k == "small" else BF16, "pair_sum_" + k)
            wire.append(per_chip)
        return wire

    def split8(g):
        return g.reshape(NDEV, g.shape[0] // NDEV, g.shape[1])

    def attend(qkv, sink_):
        attn, lse, (w_up_t, w_down_f) = _attn_fwd(qkv, sink_, gather=ffn_shards)
        return attn, lse, w_up_t, w_down_f

    def reduce_early(k, dw):
        return to_chip_sums([(k, split8(dw))], k)

    p = {k: w[k][0] for k in ("norm_mix_g", "a_re", "a_im", "log_step", "c_re", "c_im", "d_skip", "sink",
                              "norm_attn_g", "norm_ssm_g", "norm_ffn_g", "conv_b")}
    p["norm_final_g"] = norm_final_g
    p["bt_re"], p["bt_im"] = _swap(b_re[0]), _swap(b_im[0])
    grad_x, big, small, early = _local_step(x[0], loss_target[0], w_in_t, w_glu_f, w_out_f, conv_w_f, p,
                                            attend, reduce_early)

    small_names = list(small.keys())
    small_shapes = [small[k].shape for k in small_names]
    n_small = sum(math.prod(s) for s in small_shapes)
    rows_dev = -(-n_small // (PACK_W * NDEV * 16)) * 16
    spack = _pack([small[k] for k in small_names], rows_dev * NDEV)
    late_names = ["w_in_t", "w_glu", "w_out", "small"]
    late_parts = [split8(big[k]) for k in late_names[:-1]] + [spack.reshape(NDEV, rows_dev, PACK_W)]
    from_chips = _exchange_chips(to_chip_sums(list(zip(late_names, late_parts)), "late"), "exchange_chips")
    got = dict(zip(late_names, from_chips), w_down=early["w_down"][0], w_up_t=early["w_up_t"][0])
    red = {k: _chip_sum(own[k], got[k], "chip_sum_" + k) for k in got}
    (small_full,) = _all_gather([red["small"]], "gather_small")
    sm = dict(zip(small_names, _unpack(small_full, small_shapes)))

    _, s5_vjp = jax.vjp(_s5_params, a_re[0], a_im[0], log_step[0], p["bt_re"], p["bt_im"])
    da_re, da_im, dlog_step, dbt_re, dbt_im = s5_vjp((sm["lam_re"], sm["lam_im"], sm["bb_re"], sm["bb_im"]))
    gview = {
        "norm_mix_g": sm["norm_mix_g"], "norm_attn_g": sm["norm_attn_g"], "norm_ssm_g": sm["norm_ssm_g"],
        "norm_ffn_g": sm["norm_ffn_g"], "norm_final_g": sm["norm_final_g"], "conv_b": sm["conv_b"],
        "sink": sm["sink"][None], "a_re": da_re.reshape(2 * NG, NP), "a_im": da_im.reshape(2 * NG, NP),
        "log_step": dlog_step, "b_re": dbt_re.reshape(2 * NG * GC, NP), "b_im": dbt_im.reshape(2 * NG * GC, NP),
        "c_re": sm["c_re"].reshape(2 * NG * GC, NP), "c_im": sm["c_im"].reshape(2 * NG * GC, NP),
        "d_skip": sm["d_skip"].reshape(NG, GC).T,
        "w_in": red["w_in_t"], "w_glu": red["w_glu"], "w_out": red["w_out"], "w_up": red["w_up_t"],
        "w_down": red["w_down"],
        "conv_w": lax.dynamic_slice_in_dim(sm["conv_w"], me * (2 * DFF // NDEV), 2 * DFF // NDEV, axis=1),
    }

    dview, mview, vview = {}, {}, {}
    for k in BIG:
        to = VIEWS[k][0]
        dview[k], mview[k], vview[k] = _adamw(to(w[k]), gview[k], to(m[k]), to(v[k]), "adamw_" + k)
    rest = [k for k in names if k not in BIG]
    outs = _adamw_small([VIEWS[k][0](w[k]) for k in rest], [gview[k] for k in rest],
                        [VIEWS[k][0](m[k]) for k in rest], [VIEWS[k][0](v[k]) for k in rest], "adamw_small")
    for dst, vals in zip((dview, mview, vview), outs):
        dst.update(dict(zip(rest, vals)))

    def back(views):
        return [VIEWS[k][1](views[k]) for k in names]

    return (sm["loss"][0], grad_x[None], *back(gview), *back(dview), *back(mview), *back(vview))
```

```python
import functools
import math

import jax
import jax.numpy as jnp
from jax import lax
from jax.experimental import pallas as pl
from jax.experimental.pallas import tpu as pltpu

F32 = jnp.float32
BF16 = jnp.bfloat16

L = 4096
D = 1024
NQ, NKV, HD = 8, 2, 64
AW = NQ * HD
KVW = NKV * HD
SW = 512
NG, GC, NP = 32, 16, 64
INW = AW + 2 * KVW + SW
DFF = 2816
BLK = 128
WIN = 3 * BLK
EPS = 1e-6
ROPE_THETA = 500000.0
NSEG = 32
TSEG = L // NSEG
SBW = 256
NSB = NG * NP // SBW
NDEV = 8
MESH_AXES = ("x", "y", "c")

LR, B1, B2, AEPS, WD, STEP = 0.001, 0.9, 0.999, 1e-08, 0.01, 10
C1 = 1.0 - B1 ** STEP
C2 = 1.0 - B2 ** STEP

VMEM_LIMIT = 56 * 1024 * 1024


def _pick(n, target, mult):
    best = None
    for t in range(mult, min(n, target) + 1, mult):
        if n % t == 0:
            best = t
    return best if best is not None else n


def _cp(sem):
    return pltpu.CompilerParams(dimension_semantics=sem, vmem_limit_bytes=VMEM_LIMIT)


def _mm(a, b, *, ta=False, tb=False, out_dtype=F32, add=None, ride=(), post=None, name, tm=1024, tn=1024, tk=1024):
    m, k = (a.shape[1], a.shape[0]) if ta else a.shape
    n = b.shape[0] if tb else b.shape[1]
    assert k == (b.shape[1] if tb else b.shape[0])
    tm, tn, tk = _pick(m, tm, 128), _pick(n, tn, 128), _pick(k, tk, 128)
    grid = (m // tm, n // tn, k // tk)
    nk = grid[2]
    dn = (((0 if ta else 1,), (1 if tb else 0,)), ((), ()))
    n_in = 2 + (add is not None)
    nr = len(ride)
    post_fn, post_ins, post_outs = post if post is not None else (None, (), ())
    n_pi = len(post_ins)
    n_out = len(post_outs) if post is not None else 1
    assert post is None or grid[1] == 1

    def body(*refs):
        a_ref, b_ref = refs[0], refs[1]
        pin = refs[n_in:n_in + n_pi]
        base = n_in + n_pi + nr
        o_refs = refs[base:base + n_out]
        acc_ref = refs[base + n_out + nr]
        step = [pl.program_id(d) for d in range(3)]
        kk = step[2]
        if nr:
            riders = (refs[n_in + n_pi:base], refs[base + n_out:base + n_out + nr], *refs[base + n_out + nr + 1:])

            @pl.when((step[0] == 0) & (step[1] == 0) & (kk == 0))
            def _():
                _start_all(_chips_copies(*riders))

        prod = lax.dot_general(a_ref[...].astype(BF16), b_ref[...].astype(BF16), dn, preferred_element_type=F32)

        def finish(r):
            if add is not None:
                r = r + refs[2][...]
            if post_fn is None:
                o_refs[0][...] = r.astype(out_dtype)
            else:
                post_fn(r, step[0], pin, o_refs)

        if nk == 1:
            finish(prod)
        else:
            @pl.when(kk == 0)
            def _():
                acc_ref[...] = prod

            @pl.when((kk > 0) & (kk < nk - 1))
            def _():
                acc_ref[...] += prod

            @pl.when(kk == nk - 1)
            def _():
                finish(acc_ref[...] + prod)

        if nr:
            @pl.when((step[0] == grid[0] - 1) & (step[1] == grid[1] - 1) & (kk == nk - 1))
            def _():
                _wait_all(_chips_copies(*riders))

    a_spec = pl.BlockSpec((tk, tm), lambda i, j, kk: (kk, i)) if ta else pl.BlockSpec((tm, tk), lambda i, j, kk: (i, kk))
    b_spec = pl.BlockSpec((tn, tk), lambda i, j, kk: (j, kk)) if tb else pl.BlockSpec((tk, tn), lambda i, j, kk: (kk, j))
    def row_spec(shape):
        return pl.BlockSpec((tm if shape[0] == m else shape[0], shape[1]),
                            (lambda i, j, kk: (i, 0)) if shape[0] == m else (lambda i, j, kk: (0, 0)))

    in_specs = [a_spec, b_spec]
    args = [a, b]
    if add is not None:
        in_specs.append(pl.BlockSpec((tm, tn), lambda i, j, kk: (i, j)))
        args.append(add)
    if post is None:
        main_specs = [pl.BlockSpec((tm, tn), lambda i, j, kk: (i, j))]
        main_shapes = [jax.ShapeDtypeStruct((m, n), out_dtype)]
    else:
        main_specs = [row_spec(s.shape) for s in post_outs]
        main_shapes = list(post_outs)
    outs = pl.pallas_call(
        body, name=name, grid=grid,
        in_specs=in_specs + [row_spec(p.shape) for p in post_ins] + [ANY] * nr,
        out_specs=main_specs + [ANY] * nr,
        out_shape=main_shapes + _chips_shapes(ride),
        scratch_shapes=[pltpu.VMEM((tm, tn) if nk > 1 else (8, 128), F32)] + (_chips_sems(nr) if nr else []),
        compiler_params=_cp(("arbitrary",) * 3 if (nr or post is not None) else ("parallel", "parallel", "arbitrary")),
    )(*args, *post_ins, *ride)
    main = outs[0] if post is None else list(outs[:n_out])
    return (main, list(outs[n_out:])) if nr else main


TL = 512


def _rms(xv, gv):
    return xv * lax.rsqrt(jnp.mean(xv * xv, axis=-1, keepdims=True) + EPS) * gv


def _rows(width):
    return pl.BlockSpec((TL, width), lambda i: (i, 0))


def _whole(shape):
    return pl.BlockSpec(shape, lambda i: (0,) * len(shape))


def _in_proj(x, g, w_in_t, tabs):
    qkw = AW + 2 * KVW

    def body(x_ref, g_ref, w_ref, c_ref, sa_ref, sb_ref, h_ref, qkv_ref, u_ref):
        h = _rms(x_ref[...], g_ref[...]).astype(BF16)
        h_ref[...] = h
        proj = lax.dot_general(h, w_ref[...], _NT, preferred_element_type=F32)
        for j in range(qkw // 128):
            cols = slice(j * 128, (j + 1) * 128)
            xv = proj[:, cols]
            if j < (AW + KVW) // 128:
                xv = _rope(xv, c_ref[...], sa_ref[...], sb_ref[...], 1.0)
            qkv_ref[:, cols] = xv.astype(BF16)
        u_ref[...] = proj[:, qkw:]

    return pl.pallas_call(
        body, name="in_proj", grid=(L // TL,),
        in_specs=[_rows(D), _whole((1, D)), _whole((INW, D)), _rows(128), _rows(128), _rows(128)],
        out_specs=[_rows(D), _rows(qkw), _rows(SW)],
        out_shape=[jax.ShapeDtypeStruct((L, D), BF16), jax.ShapeDtypeStruct((L, qkw), BF16),
                   jax.ShapeDtypeStruct((L, SW), F32)],
        compiler_params=_cp(("parallel",)),
    )(x, g, w_in_t, *tabs)


def _in_proj_dx(dproj, w_in_t, x, g, dres):
    def body(dp_ref, w_ref, x_ref, g_ref, dres_ref, dx_ref, dg_ref):
        dh = jnp.dot(dp_ref[...], w_ref[...], preferred_element_type=F32)
        dx, dg = _rms_bwd_tile(x_ref[...], g_ref[...], dh)
        dx_ref[...] = dx + dres_ref[...]

        @pl.when(pl.program_id(0) == 0)
        def _():
            dg_ref[...] = jnp.zeros_like(dg_ref)

        dg_ref[...] += dg

    return pl.pallas_call(
        body, name="in_proj_dx", grid=(L // TL,),
        in_specs=[_rows(INW), _whole((INW, D)), _rows(D), _whole((1, D)), _rows(D)],
        out_specs=[_rows(D), _whole((1, D))],
        out_shape=[jax.ShapeDtypeStruct((L, D), F32), jax.ShapeDtypeStruct((1, D), F32)],
        compiler_params=_cp(("arbitrary",)),
    )(dproj, w_in_t, x, g, dres)


def _rms_bwd_tile(xv, gv, dh):
    r = lax.rsqrt(jnp.mean(xv * xv, axis=-1, keepdims=True) + EPS)
    a = dh * gv
    dx = r * a - xv * (r * r * r) * jnp.mean(a * xv, axis=-1, keepdims=True)
    dg = jnp.sum(dh * xv * r, axis=0, keepdims=True)
    return dx, dg


def _rms_bwd_post(dh, i, ins, outs):
    x_ref, dres_ref, g_ref = ins
    dx_ref, dg_ref = outs
    dx, dg = _rms_bwd_tile(x_ref[...], g_ref[...], dh)
    dx_ref[...] = dx + dres_ref[...]

    @pl.when(i == 0)
    def _():
        dg_ref[...] = jnp.zeros_like(dg_ref)

    dg_ref[...] += dg


def _final_post(xv, i, ins, outs):
    t_ref, g_ref = ins
    loss_ref, dx_ref, dg_ref = outs

    @pl.when(i == 0)
    def _():
        loss_ref[...] = jnp.zeros_like(loss_ref)
        dg_ref[...] = jnp.zeros_like(dg_ref)

    gv = g_ref[...]
    r = lax.rsqrt(jnp.mean(xv * xv, axis=-1, keepdims=True) + EPS)
    e = xv * r * gv - t_ref[...]
    loss_ref[...] += 0.5 * jnp.sum(jnp.mean(e * e, axis=-1, keepdims=True), axis=0, keepdims=True)
    dy = e * (1.0 / D)
    a = dy * gv
    dx_ref[...] = r * a - xv * (r * r * r) * jnp.mean(a * xv, axis=-1, keepdims=True)
    dg_ref[...] += jnp.sum(dy * xv * r, axis=0, keepdims=True)


def _out_proj(attn, ysg, ga, gs, w_out, x, gf):
    def body(a_ref, s_ref, ga_ref, gs_ref, w_ref, x_ref, gf_ref, m_ref, x1_ref, h2_ref):
        m_ref[:, 0:AW] = _rms(a_ref[...], ga_ref[...]).astype(BF16)
        m_ref[:, AW:AW + SW] = _rms(s_ref[...], gs_ref[...]).astype(BF16)
        x1 = jnp.dot(m_ref[...], w_ref[...], preferred_element_type=F32) + x_ref[...]
        x1_ref[...] = x1
        h2_ref[...] = _rms(x1, gf_ref[...]).astype(BF16)

    return pl.pallas_call(
        body, name="out_proj", grid=(L // TL,),
        in_specs=[_rows(AW), _rows(SW), _whole((1, AW)), _whole((1, SW)), _whole((D, D)), _rows(D), _whole((1, D))],
        out_specs=[_rows(D), _rows(D), _rows(D)],
        out_shape=[jax.ShapeDtypeStruct((L, D), BF16), jax.ShapeDtypeStruct((L, D), F32),
                   jax.ShapeDtypeStruct((L, D), BF16)],
        compiler_params=_cp(("parallel",)),
    )(attn, ysg, ga, gs, w_out, x, gf)


def _out_proj_dx(dx1, w_out, attn, ysg, ga, gs):
    def body(dx_ref, w_ref, a_ref, s_ref, ga_ref, gs_ref, da_ref, ds_ref, dga_ref, dgs_ref):
        @pl.when(pl.program_id(0) == 0)
        def _():
            dga_ref[...] = jnp.zeros_like(dga_ref)
            dgs_ref[...] = jnp.zeros_like(dgs_ref)

        dm = lax.dot_general(dx_ref[...].astype(BF16), w_ref[...], _NT, preferred_element_type=F32)
        dxa, dga = _rms_bwd_tile(a_ref[...], ga_ref[...], dm[:, 0:AW])
        da_ref[...] = dxa
        dga_ref[...] += dga
        dxs, dgs = _rms_bwd_tile(s_ref[...], gs_ref[...], dm[:, AW:AW + SW])
        ds_ref[...] = dxs
        dgs_ref[...] += dgs

    return pl.pallas_call(
        body, name="out_proj_dx", grid=(L // TL,),
        in_specs=[_rows(D), _whole((D, D)), _rows(AW), _rows(SW), _whole((1, AW)), _whole((1, SW))],
        out_specs=[_rows(AW), _rows(SW), _whole((1, AW)), _whole((1, SW))],
        out_shape=[jax.ShapeDtypeStruct((L, AW), F32), jax.ShapeDtypeStruct((L, SW), F32),
                   jax.ShapeDtypeStruct((1, AW), F32), jax.ShapeDtypeStruct((1, SW), F32)],
        compiler_params=_cp(("arbitrary",)),
    )(dx1, w_out, attn, ysg, ga, gs)


def _rope_tables():
    half = HD // 8
    inv_freq = jnp.power(ROPE_THETA, -jnp.arange(half, dtype=F32) / half)
    ang = jnp.arange(L, dtype=F32)[:, None] * inv_freq[None, :]
    cos, sin = jnp.cos(ang), jnp.sin(ang)
    one = jnp.ones((L, HD - 2 * half), F32)
    zero = jnp.zeros((L, HD - 2 * half), F32)
    zh = jnp.zeros((L, half), F32)
    cos64 = jnp.concatenate([cos, cos, one], axis=1)
    sa64 = jnp.concatenate([-sin, zh, zero], axis=1)
    sb64 = jnp.concatenate([zh, sin, zero], axis=1)
    return [jnp.tile(t, (1, 2)) for t in (cos64, sa64, sb64)]


def _rope(xv, cosv, sav, sbv, sign):
    return xv * cosv + sign * (pltpu.roll(xv, 120, 1) * sav + pltpu.roll(xv, 8, 1) * sbv)


def _rope_bwd(dq, dk, dv, du, tabs):
    def body(dq_ref, dk_ref, dv_ref, du_ref, c_ref, sa_ref, sb_ref, o_ref):
        for j in range(AW // 128):
            cols = slice(j * 128, (j + 1) * 128)
            o_ref[:, cols] = _rope(dq_ref[:, cols], c_ref[...], sa_ref[...], sb_ref[...], -1.0).astype(BF16)
        o_ref[:, AW:AW + KVW] = _rope(dk_ref[...], c_ref[...], sa_ref[...], sb_ref[...], -1.0).astype(BF16)
        o_ref[:, AW + KVW:AW + 2 * KVW] = dv_ref[...].astype(BF16)
        o_ref[:, AW + 2 * KVW:] = du_ref[...].astype(BF16)

    def row(width):
        return pl.BlockSpec((TL, width), lambda i: (i, 0))

    return pl.pallas_call(
        body, name="rope_bwd", grid=(L // TL,),
        in_specs=[row(AW), row(KVW), row(KVW), row(SW), row(128), row(128), row(128)],
        out_specs=row(INW), out_shape=jax.ShapeDtypeStruct((L, INW), BF16),
        compiler_params=_cp(("parallel",)),
    )(dq, dk, dv, du, *tabs)


def _attn_window(n):
    start = pl.multiple_of(jnp.clip((n - 1) * BLK, 0, L - WIN), BLK)
    qpos = n * BLK + lax.broadcasted_iota(jnp.int32, (BLK, WIN), 0)
    kpos = start + lax.broadcasted_iota(jnp.int32, (BLK, WIN), 1)
    return start, jnp.abs(kpos - qpos) <= BLK


_NT = (((1,), (1,)), ((), ()))
_TN = (((0,), (0,)), ((), ()))
NEG = -1e30


def _attn_fwd(qkv, sink, gather=()):
    ng = len(gather)

    def body(sink_ref, q_ref, k_ref, v_ref, *rest):
        o_ref, lse_ref = rest[ng], rest[ng + 1]
        n = pl.program_id(0)
        if ng:
            travellers = (rest[:ng], rest[ng + 2:2 * ng + 2], *rest[2 * ng + 2:])

            @pl.when(n == 0)
            def _():
                _gather_start(*travellers)

            @pl.when(n == L // BLK - 1)
            def _():
                _gather_finish(*travellers)

        start, valid = _attn_window(n)
        kw = k_ref[pl.ds(start, WIN), :]
        vw = v_ref[pl.ds(start, WIN), :]
        for h in range(NQ):
            kv = h // (NQ // NKV)
            qh = q_ref[:, h * HD:(h + 1) * HD]
            kh = kw[:, kv * HD:(kv + 1) * HD]
            vh = vw[:, kv * HD:(kv + 1) * HD]
            s = lax.dot_general(qh, kh, _NT, preferred_element_type=F32) * (HD ** -0.5)
            s = jnp.where(valid, s, NEG)
            sk = sink_ref[h]
            m = jnp.maximum(jnp.max(s, axis=-1, keepdims=True), sk)
            p = jnp.exp(s - m)
            den = jnp.sum(p, axis=-1, keepdims=True) + jnp.exp(sk - m)
            o_ref[:, h * HD:(h + 1) * HD] = jnp.dot((p / den).astype(BF16), vh, preferred_element_type=F32)
            lse_ref[:, h:h + 1] = m + jnp.log(den)

    outs = pl.pallas_call(
        body, name="attn_fwd", grid=(L // BLK,),
        in_specs=[pl.BlockSpec(memory_space=pltpu.SMEM),
                  pl.BlockSpec((BLK, AW), lambda n: (n, 0)),
                  pl.BlockSpec((L, KVW), lambda n: (0, AW // KVW)),
                  pl.BlockSpec((L, KVW), lambda n: (0, AW // KVW + 1))] + [ANY] * ng,
        out_specs=[pl.BlockSpec((BLK, AW), lambda n: (n, 0)), pl.BlockSpec((BLK, NQ), lambda n: (n, 0))] + [ANY] * ng,
        out_shape=[jax.ShapeDtypeStruct((L, AW), F32), jax.ShapeDtypeStruct((L, NQ), F32)] + _gather_shapes(gather),
        scratch_shapes=_gather_sems(ng) if ng else [],
        compiler_params=_cp(("arbitrary",) if ng else ("parallel",)),
    )(sink, qkv, qkv, qkv, *gather)
    return outs[0], outs[1], list(outs[2:])


def _attn_bwd(qkv, sink, attn, lse, dattn):
    def body(sink_ref, q_ref, k_ref, v_ref, o_ref, lse_ref, do_ref, dq_ref, dk_ref, dv_ref, dsink_ref):
        n = pl.program_id(0)

        @pl.when(n == 0)
        def _():
            dk_ref[...] = jnp.zeros_like(dk_ref)
            dv_ref[...] = jnp.zeros_like(dv_ref)
            dsink_ref[...] = jnp.zeros_like(dsink_ref)

        start, valid = _attn_window(n)
        kw = k_ref[pl.ds(start, WIN), :]
        vw = v_ref[pl.ds(start, WIN), :]
        for kv in range(NKV):
            kh = kw[:, kv * HD:(kv + 1) * HD]
            vh = vw[:, kv * HD:(kv + 1) * HD]
            dk_acc = jnp.zeros((WIN, HD), F32)
            dv_acc = jnp.zeros((WIN, HD), F32)
            for h in range(kv * (NQ // NKV), (kv + 1) * (NQ // NKV)):
                qh = q_ref[:, h * HD:(h + 1) * HD]
                doh = do_ref[:, h * HD:(h + 1) * HD]
                dd = jnp.sum(doh * o_ref[:, h * HD:(h + 1) * HD], axis=-1, keepdims=True)
                lse_h = lse_ref[:, h:h + 1]
                s = lax.dot_general(qh, kh, _NT, preferred_element_type=F32) * (HD ** -0.5)
                p = jnp.where(valid, jnp.exp(s - lse_h), 0.0)
                dob = doh.astype(BF16)
                dp = lax.dot_general(dob, vh, _NT, preferred_element_type=F32)
                ds = (p * (dp - dd) * (HD ** -0.5)).astype(BF16)
                dq_ref[:, h * HD:(h + 1) * HD] = jnp.dot(ds, kh, preferred_element_type=F32)
                dk_acc += lax.dot_general(ds, qh, _TN, preferred_element_type=F32)
                dv_acc += lax.dot_general(p.astype(BF16), dob, _TN, preferred_element_type=F32)
                psink = jnp.exp(sink_ref[h] - lse_h)
                dsk = -jnp.sum(psink * dd, axis=0, keepdims=True)
                dsink_ref[h:h + 1, :] += jnp.broadcast_to(dsk, (1, 128))
            dk_ref[pl.ds(start, WIN), kv * HD:(kv + 1) * HD] += dk_acc
            dv_ref[pl.ds(start, WIN), kv * HD:(kv + 1) * HD] += dv_acc

    qblk = pl.BlockSpec((BLK, AW), lambda n: (n, 0))
    full = pl.BlockSpec((L, KVW), lambda n: (0, 0))
    return pl.pallas_call(
        body, name="attn_bwd", grid=(L // BLK,),
        in_specs=[pl.BlockSpec(memory_space=pltpu.SMEM), qblk,
                  pl.BlockSpec((L, KVW), lambda n: (0, AW // KVW)),
                  pl.BlockSpec((L, KVW), lambda n: (0, AW // KVW + 1)),
                  qblk, pl.BlockSpec((BLK, NQ), lambda n: (n, 0)), qblk],
        out_specs=[qblk, full, full, pl.BlockSpec((NQ, 128), lambda n: (0, 0))],
        out_shape=[jax.ShapeDtypeStruct((L, AW), F32), jax.ShapeDtypeStruct((L, KVW), F32),
                   jax.ShapeDtypeStruct((L, KVW), F32), jax.ShapeDtypeStruct((NQ, 128), F32)],
        compiler_params=_cp(("arbitrary",)),
    )(sink, qkv, qkv, qkv, attn, lse, dattn)


def _perm(a):
    return a.reshape(NSEG, TSEG, a.shape[1]).transpose(1, 0, 2).reshape(L, a.shape[1])


def _unperm(a):
    return a.reshape(TSEG, NSEG, a.shape[1]).transpose(1, 0, 2).reshape(L, a.shape[1])


def _cmul(ar, ai, br, bi):
    return ar * br - ai * bi, ar * bi + ai * br


def _scan_inplace(s_ref, lr, li, rev, visit=None, carried=()):
    n = lr.shape[1]
    lr8 = jnp.broadcast_to(lr, (NSEG, n))
    li8 = jnp.broadcast_to(li, (NSEG, n))

    def rows(k):
        return pl.ds(pl.multiple_of(jnp.where(rev, TSEG - 1 - k, k) * NSEG, NSEG), NSEG)

    def step(k, c, store):
        sr, si = c
        rs = rows(k)
        pr, pi = _cmul(lr8, li8, sr, si)
        nr = pr + s_ref[rs, 0:n]
        ni = pi + s_ref[rs, n:2 * n]
        if store:
            s_ref[rs, 0:n] = nr
            s_ref[rs, n:2 * n] = ni
        return nr, ni

    z = jnp.zeros((NSEG, n), F32)
    er, ei = lax.fori_loop(0, TSEG, functools.partial(step, store=False), (z, z))
    pr, pi = lr, li
    for _ in range(int(math.log2(TSEG))):
        pr, pi = _cmul(pr, pi, pr, pi)

    seg = lax.broadcasted_iota(jnp.int32, (NSEG, n), 0)

    def chain(order):
        cr = jnp.zeros((1, n), F32)
        ci = jnp.zeros((1, n), F32)
        outr = jnp.zeros((NSEG, n), F32)
        outi = jnp.zeros((NSEG, n), F32)
        for s in order:
            outr = jnp.where(seg == s, cr, outr)
            outi = jnp.where(seg == s, ci, outi)
            mr, mi = _cmul(pr, pi, cr, ci)
            cr, ci = mr + er[s:s + 1], mi + ei[s:s + 1]
        return outr, outi

    fr, fi = chain(range(NSEG))
    rr, ri = chain(range(NSEG - 1, -1, -1))
    cin_r = jnp.where(rev, rr, fr)
    cin_i = jnp.where(rev, ri, fi)
    if visit is None:
        lax.fori_loop(0, TSEG, functools.partial(step, store=True), (cin_r, cin_i))
        return cin_r, cin_i

    def visited(k, c):
        nr, ni = step(k, c[:2], True)
        return (nr, ni) + tuple(visit(k, nr, ni, c[2:]))

    fin = lax.fori_loop(0, TSEG - 1, visited, (cin_r, cin_i) + tuple(carried))
    last_r, last_i = step(TSEG - 1, fin[:2], True)
    return last_r, last_i, fin[2:]


S5_RC = 512


def _s5_specs():
    u_spec = pl.BlockSpec((L, 128), lambda cb, h, d: (0, cb))
    b_spec = pl.BlockSpec((None, None, 128, 2 * SBW), lambda cb, h, d: (d, cb * 2 + h, 0, 0))
    c_spec = pl.BlockSpec((None, None, 2 * SBW, 128), lambda cb, h, d: (d, cb * 2 + h, 0, 0))
    l_spec = pl.BlockSpec((None, None, 1, SBW), lambda cb, h, d: (d, cb * 2 + h, 0, 0))
    d_spec = pl.BlockSpec((1, 128), lambda cb, h, d: (0, cb))
    return u_spec, b_spec, c_spec, l_spec, d_spec


def _s5_input_states(u_ref, b_ref, lr_ref, li_ref, s_scr, rev):
    def proj(i, _):
        rs = pl.ds(pl.multiple_of(i * S5_RC, S5_RC), S5_RC)
        s_scr[rs, :] = jnp.dot(u_ref[rs, :].astype(BF16), b_ref[...], preferred_element_type=F32)
        return 0

    lax.fori_loop(0, L // S5_RC, proj, 0)
    return _scan_inplace(s_scr, lr_ref[...], li_ref[...], rev)


def _s5_fwd(u_p, bcat, ccat, lam_re, lam_im, dskip):
    def body(u_ref, b_ref, c_ref, lr_ref, li_ref, d_ref, y_ref, sb_ref, s_scr):
        first = (pl.program_id(1) == 0) & (pl.program_id(2) == 0)
        _s5_input_states(u_ref, b_ref, lr_ref, li_ref, s_scr, pl.program_id(2) == 1)

        def out(i, _):
            rs = pl.ds(pl.multiple_of(i * S5_RC, S5_RC), S5_RC)
            sb = s_scr[rs, :].astype(BF16)
            sb_ref[rs, :] = sb
            yv = jnp.dot(sb, c_ref[...], preferred_element_type=F32)

            @pl.when(first)
            def _():
                y_ref[rs, :] = d_ref[...] * u_ref[rs, :] + yv

            @pl.when(jnp.logical_not(first))
            def _():
                y_ref[rs, :] += yv

            return 0

        lax.fori_loop(0, L // S5_RC, out, 0)

    u_spec, b_spec, c_spec, l_spec, d_spec = _s5_specs()
    return pl.pallas_call(
        body, name="s5_fwd", grid=(SW // 128, 2, 2),
        in_specs=[u_spec, b_spec, c_spec, l_spec, l_spec, d_spec],
        out_specs=[u_spec, _s5_state_spec()],
        out_shape=[jax.ShapeDtypeStruct((L, SW), F32), jax.ShapeDtypeStruct((2, NSB, L, 2 * SBW), BF16)],
        scratch_shapes=[pltpu.VMEM((L, 2 * SBW), F32)],
        compiler_params=_cp(("parallel", "arbitrary", "arbitrary")),
    )(u_p, bcat, ccat, lam_re, lam_im, dskip)


def _s5_state_spec():
    return pl.BlockSpec((None, None, L, 2 * SBW), lambda cb, h, d: (d, cb * 2 + h, 0, 0))


def _s5_bwd(u_p, dy_p, states, bcat, ccat, lam_re, lam_im, dskip):
    def body(u_ref, dy_ref, s_ref, b_ref, c_ref, lr_ref, li_ref, d_ref,
             du_ref, db_ref, dc_ref, dlr_ref, dli_ref, dd_ref, g_scr):
        first = (pl.program_id(1) == 0) & (pl.program_id(2) == 0)
        rev = pl.program_id(2) == 1

        def dstate(i, _):
            rs = pl.ds(pl.multiple_of(i * S5_RC, S5_RC), S5_RC)
            g_scr[rs, :] = lax.dot_general(dy_ref[rs, :].astype(BF16), c_ref[...], _NT, preferred_element_type=F32)
            return 0

        lax.fori_loop(0, L // S5_RC, dstate, 0)

        def before(rows):
            sv = s_ref[rows, :].astype(F32)
            return sv[:, 0:SBW], sv[:, SBW:2 * SBW]

        def dlam(gr, gi, sr, si, ar, ai):
            return ar + gr * sr + gi * si, ai + gi * sr - gr * si

        def visit(k, gr, gi, acc):
            ts = jnp.where(rev, k + 1, TSEG - 2 - k)
            sr, si = before(pl.ds(pl.multiple_of(ts * NSEG, NSEG), NSEG))
            return dlam(gr, gi, sr, si, *acc)

        z = jnp.zeros((NSEG, SBW), F32)
        gr, gi, acc = _scan_inplace(g_scr, lr_ref[...], -li_ref[...], jnp.logical_not(rev), visit, (z, z))
        edge_r, edge_i = before(pl.ds(pl.multiple_of(jnp.where(rev, 0, TSEG - 1) * NSEG, NSEG), NSEG))
        seg = lax.broadcasted_iota(jnp.int32, (NSEG, SBW), 0)
        keep = seg != jnp.where(rev, NSEG - 1, 0)

        def neighbour(e):
            return jnp.where(keep, jnp.where(rev, pltpu.roll(e, NSEG - 1, 0), pltpu.roll(e, 1, 0)), 0.0)

        ar, ai = dlam(gr, gi, neighbour(edge_r), neighbour(edge_i), *acc)
        dlr_ref[...] = jnp.sum(ar, axis=0, keepdims=True)
        dli_ref[...] = jnp.sum(ai, axis=0, keepdims=True)

        db_ref[...] = jnp.zeros_like(db_ref)
        dc_ref[...] = jnp.zeros_like(dc_ref)

        @pl.when(first)
        def _():
            dd_ref[...] = jnp.zeros_like(dd_ref)

        def grads(i, _):
            rs = pl.ds(pl.multiple_of(i * S5_RC, S5_RC), S5_RC)
            uv = u_ref[rs, :]
            dyv = dy_ref[rs, :]
            gb = g_scr[rs, :].astype(BF16)
            db_ref[...] += lax.dot_general(uv.astype(BF16), gb, _TN, preferred_element_type=F32)
            dc_ref[...] += lax.dot_general(dyv.astype(BF16), s_ref[rs, :], _TN, preferred_element_type=F32)
            duv = lax.dot_general(gb, b_ref[...], _NT, preferred_element_type=F32)

            @pl.when(first)
            def _():
                du_ref[rs, :] = d_ref[...] * dyv + duv
                dd_ref[...] += jnp.sum(dyv * uv, axis=0, keepdims=True)

            @pl.when(jnp.logical_not(first))
            def _():
                du_ref[rs, :] += duv

            return 0

        lax.fori_loop(0, L // S5_RC, grads, 0)

    u_spec, b_spec, c_spec, l_spec, d_spec = _s5_specs()
    return pl.pallas_call(
        body, name="s5_bwd", grid=(SW // 128, 2, 2),
        in_specs=[u_spec, u_spec, _s5_state_spec(), b_spec, c_spec, l_spec, l_spec, d_spec],
        out_specs=[u_spec, b_spec, b_spec, l_spec, l_spec, d_spec],
        out_shape=[jax.ShapeDtypeStruct((L, SW), F32),
                   jax.ShapeDtypeStruct((2, NSB, 128, 2 * SBW), F32), jax.ShapeDtypeStruct((2, NSB, 128, 2 * SBW), F32),
                   jax.ShapeDtypeStruct((2, NSB, 1, SBW), F32), jax.ShapeDtypeStruct((2, NSB, 1, SBW), F32),
                   jax.ShapeDtypeStruct((1, SW), F32)],
        scratch_shapes=[pltpu.VMEM((L, 2 * SBW), F32)],
        compiler_params=_cp(("parallel", "arbitrary", "arbitrary")),
    )(u_p, dy_p, states, bcat, ccat, lam_re, lam_im, dskip)


def _s5_params(a_re, a_im, log_step, bt_re, bt_im):
    lam = lax.complex(a_re, a_im)
    step = jnp.exp(log_step)[..., None]
    lam_bar = jnp.exp(lam * step)
    b_bar = ((lam_bar - 1.0) / lam)[..., None, :] * lax.complex(bt_re, bt_im)
    return jnp.real(lam_bar), jnp.imag(lam_bar), jnp.real(b_bar), jnp.imag(b_bar)


def _sel():
    i = jnp.arange(8)[None, :, None]
    j = jnp.arange(4)[None, None, :]
    r = jnp.arange(2)[:, None, None]
    return (i == r * 4 + j).astype(F32)


def _to_bcat(bt_re, bt_im):
    def one(bt):
        return jnp.einsum('dkrjcp,rij->dkricjp', bt.reshape(2, 4, 2, 4, GC, NP), _sel()).reshape(2, NSB, 128, SBW)
    return jnp.concatenate([one(bt_re), one(bt_im)], axis=-1)


def _from_bcat(dbcat):
    def one(dbbd):
        return jnp.einsum('dkricjp,rij->dkrjcp', dbbd.reshape(2, 4, 2, 8, GC, 4, NP), _sel()).reshape(2, NG, GC, NP)
    return one(dbcat[..., :SBW]), one(dbcat[..., SBW:])


def _to_ccat(c_re, c_im):
    def one(cc):
        return jnp.einsum('dkrjcp,rij->dkrjpic', cc.reshape(2, 4, 2, 4, GC, NP), _sel()).reshape(2, NSB, SBW, 128)
    return jnp.concatenate([one(c_re), -one(c_im)], axis=-2)


def _from_ccat(dccat):
    def one(dcbd):
        return jnp.einsum('dkrjpic,rij->dkrjcp', dcbd.reshape(2, 4, 2, 4, NP, 8, GC), _sel()).reshape(2, NG, GC, NP)
    return one(dccat[:, :, :SBW]), -one(dccat[:, :, SBW:])


def _gelu(y):
    return 0.5 * y * (1.0 + lax.erf(y * (2.0 ** -0.5)))


def _gelu_grad(y):
    return 0.5 * (1.0 + lax.erf(y * (2.0 ** -0.5))) + y * jnp.exp(-0.5 * y * y) * ((2.0 * math.pi) ** -0.5)


def _sigmoid(z):
    return 0.5 * jnp.tanh(0.5 * z) + 0.5


def _glu_fwd(y, wg):
    def body(y_ref, w_ref, o_ref, z_ref):
        ys = _gelu(y_ref[...])
        z = jnp.dot(ys.astype(BF16), w_ref[...], preferred_element_type=F32)
        z_ref[...] = z
        o_ref[...] = ys * _sigmoid(z)

    row = pl.BlockSpec((TL, SW), lambda i: (i, 0))
    return pl.pallas_call(
        body, name="glu_fwd", grid=(L // TL,),
        in_specs=[row, pl.BlockSpec((SW, SW), lambda i: (0, 0))], out_specs=[row, row],
        out_shape=[jax.ShapeDtypeStruct((L, SW), F32), jax.ShapeDtypeStruct((L, SW), F32)],
        compiler_params=_cp(("parallel",)),
    )(y, wg)


def _glu_bwd(y, z, dout, wg):
    def body(y_ref, z_ref, do_ref, w_ref, dy_ref, dw_ref):
        @pl.when(pl.program_id(0) == 0)
        def _():
            dw_ref[...] = jnp.zeros_like(dw_ref)

        yv = y_ref[...]
        ys = _gelu(yv)
        sg = _sigmoid(z_ref[...])
        dov = do_ref[...]
        dz = (dov * ys * sg * (1.0 - sg)).astype(BF16)
        dys = dov * sg + lax.dot_general(dz, w_ref[...], _NT, preferred_element_type=F32)
        dy_ref[...] = dys * _gelu_grad(yv)
        dw_ref[...] += lax.dot_general(ys.astype(BF16), dz, _TN, preferred_element_type=F32)

    row = pl.BlockSpec((TL, SW), lambda i: (i, 0))
    wsp = pl.BlockSpec((SW, SW), lambda i: (0, 0))
    return pl.pallas_call(
        body, name="glu_bwd", grid=(L // TL,),
        in_specs=[row, row, row, wsp], out_specs=[row, wsp],
        out_shape=[jax.ShapeDtypeStruct((L, SW), F32), jax.ShapeDtypeStruct((SW, SW), F32)],
        compiler_params=_cp(("arbitrary",)),
    )(y, z, dout, wg)


CT = 256
CR = 128
NCT = DFF // CT


def _shifted(ref, r):
    cur = ref[pl.ds(r, CR), :]
    before = ref[pl.ds(pl.multiple_of(jnp.maximum(r - 8, 0), 8), 8), :][7:8, :]
    after = ref[pl.ds(pl.multiple_of(jnp.minimum(r + CR, L - 8), 8), 8), :][0:1, :]
    before = jnp.where(r > 0, before, 0.0)
    after = jnp.where(r + CR < L, after, 0.0)
    row = lax.broadcasted_iota(jnp.int32, cur.shape, 0)
    prev = jnp.where(row == 0, before, pltpu.roll(cur, 1, 0))
    nxt = jnp.where(row == CR - 1, after, pltpu.roll(cur, CR - 1, 0))
    return prev, cur, nxt


def _conv3(ref, r, w_ref, b_ref):
    prev, cur, nxt = _shifted(ref, r)
    return w_ref[0:1, :] * prev + w_ref[1:2, :] * cur + w_ref[2:3, :] * nxt + b_ref[...]


def _convact_fwd(up, conv_w, conv_b):
    def body(ug_ref, uv_ref, wg_ref, wv_ref, bg_ref, bv_ref, o_ref, g_ref, v_ref):
        def chunk(i, _):
            r = pl.multiple_of(i * CR, CR)
            rs = pl.ds(r, CR)
            g = _conv3(ug_ref, r, wg_ref, bg_ref)
            v = _conv3(uv_ref, r, wv_ref, bv_ref)
            o_ref[rs, :] = (g * _sigmoid(g) * v).astype(BF16)
            g_ref[rs, :] = g.astype(BF16)
            v_ref[rs, :] = v.astype(BF16)
            return 0

        lax.fori_loop(0, L // CR, chunk, 0)

    gcol = pl.BlockSpec((L, CT), lambda j: (0, j))
    vcol = pl.BlockSpec((L, CT), lambda j: (0, j + NCT))
    return pl.pallas_call(
        body, name="convact_fwd", grid=(NCT,),
        in_specs=[gcol, vcol,
                  pl.BlockSpec((3, CT), lambda j: (0, j)), pl.BlockSpec((3, CT), lambda j: (0, j + NCT)),
                  pl.BlockSpec((1, CT), lambda j: (0, j)), pl.BlockSpec((1, CT), lambda j: (0, j + NCT))],
        out_specs=[gcol, gcol, gcol], out_shape=[jax.ShapeDtypeStruct((L, DFF), BF16)] * 3,
        compiler_params=_cp(("parallel",)),
    )(up, up, conv_w, conv_w, conv_b, conv_b)


def _convact_bwd(up, gq, vq, dact, conv_w):
    def body(ug_ref, uv_ref, g_ref, v_ref, da_ref, wg_ref, wv_ref, du_ref, dw_ref, db_ref, dgs, dvs, dbv):
        half = pl.program_id(1)

        def transpose_conv(src, u_ref, w_ref):
            dw_ref[...] = jnp.zeros_like(dw_ref)

            def chunk(i, _):
                r = pl.multiple_of(i * CR, CR)
                rs = pl.ds(r, CR)
                prev, cur, nxt = _shifted(src, r)
                du_ref[rs, :] = (w_ref[0:1, :] * nxt + w_ref[1:2, :] * cur + w_ref[2:3, :] * prev).astype(BF16)
                uv = u_ref[rs, :]
                for k, d in enumerate((nxt, cur, prev)):
                    dw_ref[k:k + 1, :] += jnp.sum(d * uv, axis=0, keepdims=True)
                return 0

            lax.fori_loop(0, L // CR, chunk, 0)

        @pl.when(half == 0)
        def _():
            db_ref[...] = jnp.zeros_like(db_ref)
            dbv[...] = jnp.zeros_like(dbv)

            def chunk1(i, _):
                rs = pl.ds(pl.multiple_of(i * CR, CR), CR)
                g = g_ref[rs, :].astype(F32)
                v = v_ref[rs, :].astype(F32)
                sg = _sigmoid(g)
                da = da_ref[rs, :]
                dv = da * g * sg
                dg = da * v * sg * (1.0 + g * (1.0 - sg))
                dgs[rs, :] = dg
                dvs[rs, :] = dv
                db_ref[...] += jnp.sum(dg, axis=0, keepdims=True)
                dbv[0:1, :] += jnp.sum(dv, axis=0, keepdims=True)
                return 0

            lax.fori_loop(0, L // CR, chunk1, 0)
            transpose_conv(dgs, ug_ref, wg_ref)

        @pl.when(half == 1)
        def _():
            db_ref[...] = dbv[0:1, :]
            transpose_conv(dvs, uv_ref, wv_ref)

    def col(rows, off):
        return pl.BlockSpec((rows, CT), lambda j, h: (0, j + off))

    def out(rows):
        return pl.BlockSpec((rows, CT), lambda j, h: (0, j + h * NCT))

    return pl.pallas_call(
        body, name="convact_bwd", grid=(NCT, 2),
        in_specs=[col(L, 0), col(L, NCT), col(L, 0), col(L, 0), col(L, 0), col(3, 0), col(3, NCT)],
        out_specs=[out(L), out(3), out(1)],
        out_shape=[jax.ShapeDtypeStruct((L, 2 * DFF), BF16), jax.ShapeDtypeStruct((3, 2 * DFF), F32),
                   jax.ShapeDtypeStruct((1, 2 * DFF), F32)],
        scratch_shapes=[pltpu.VMEM((L, CT), F32), pltpu.VMEM((L, CT), F32), pltpu.VMEM((8, CT), F32)],
        compiler_params=_cp(("parallel", "arbitrary")),
    )(up, up, gq, vq, dact, conv_w, conv_w)


def _local_step(x, tgt, w_in_t, w_glu, w_out, conv_w, p, attend, reduce_early):
    tabs = _rope_tables()
    lam_re, lam_im, bb_re, bb_im = _s5_params(p["a_re"], p["a_im"], p["log_step"], p["bt_re"], p["bt_im"])
    bcat = _to_bcat(bb_re, bb_im).astype(BF16)
    ccat = _to_ccat(p["c_re"], p["c_im"]).astype(BF16)
    lam_re4, lam_im4 = lam_re.reshape(2, NSB, 1, SBW), lam_im.reshape(2, NSB, 1, SBW)
    dskip = p["d_skip"].reshape(1, SW)
    g_mix, g_ffn, g_fin = p["norm_mix_g"].reshape(1, D), p["norm_ffn_g"].reshape(1, D), p["norm_final_g"].reshape(1, D)
    g_attn, g_ssm = p["norm_attn_g"].reshape(1, AW), p["norm_ssm_g"].reshape(1, SW)
    sink = p["sink"].reshape(NQ)
    conv_b = p["conv_b"].reshape(1, 2 * DFF)

    rows, gain = jax.ShapeDtypeStruct((L, D), F32), jax.ShapeDtypeStruct((1, D), F32)
    h1, qkv, u = _in_proj(x, g_mix, w_in_t, tabs)
    attn, lse, w_up_t, w_down = attend(qkv, sink)
    u_p = _perm(u)
    y_p, states = _s5_fwd(u_p, bcat, ccat, lam_re4, lam_im4, dskip)
    ysg_p, z_p = _glu_fwd(y_p, w_glu)
    ysg = _unperm(ysg_p)
    mixed, x1, h2 = _out_proj(attn, ysg, g_attn, g_ssm, w_out, x, g_ffn)
    up = _mm(h2, w_up_t, tb=True, name="ffn_up", tn=1408)
    act, gq, vq = _convact_fwd(up, conv_w, conv_b)
    loss, dx2, dg_fin = _mm(act, w_down, add=x1, name="ffn_down", tk=1408,
                            post=(_final_post, [tgt, g_fin], [jax.ShapeDtypeStruct((1, 1), F32), rows, gain]))

    dact = _mm(dx2, w_down, tb=True, name="ffn_down_dx", tn=1408)
    dw_down = _mm(act, dx2, ta=True, name="ffn_down_dw", tm=1408)
    ride = reduce_early("w_down", dw_down)
    dup, dconv_w, dconv_b = _convact_bwd(up, gq, vq, dact, conv_w)
    res = _mm(dup, h2, ta=True, name="ffn_up_dw", tm=1408, ride=ride)
    dw_up_t, got_down = res if ride else (res, [])
    ride = reduce_early("w_up_t", dw_up_t)
    res = _mm(dup, w_up_t, name="ffn_up_dx", tk=1408, ride=ride,
              post=(_rms_bwd_post, [x1, dx2, g_ffn], [rows, gain]))
    (dx1, dg_ffn), got_up = res if ride else (res, [])
    dattn, dysg, dg_attn, dg_ssm = _out_proj_dx(dx1, w_out, attn, ysg, g_attn, g_ssm)
    dw_out = _mm(mixed, dx1, ta=True, name="out_proj_dw")
    dy_p, dw_glu = _glu_bwd(y_p, z_p, _perm(dysg), w_glu)
    du_p, dbcat, dccat, dlam_re, dlam_im, dd = _s5_bwd(u_p, dy_p, states, bcat, ccat, lam_re4, lam_im4, dskip)
    dbb_re, dbb_im = _from_bcat(dbcat)
    dc_re, dc_im = _from_ccat(_swap(dccat))
    dq, dk, dv, dsink = _attn_bwd(qkv, sink, attn, lse, dattn)
    dproj = _rope_bwd(dq, dk, dv, _unperm(du_p), tabs)
    dw_in_t = _mm(dproj, h1, ta=True, name="in_proj_dw", tm=1280)
    grad_x, dg_mix = _in_proj_dx(dproj, w_in_t, x, g_mix, dx1)

    big = dict(w_in_t=dw_in_t, w_glu=dw_glu, w_out=dw_out)
    early = dict(w_down=got_down, w_up_t=got_up)
    small = dict(norm_mix_g=dg_mix, norm_attn_g=dg_attn, norm_ssm_g=dg_ssm, norm_ffn_g=dg_ffn, norm_final_g=dg_fin,
                 sink=dsink[:, 0], conv_b=dconv_b, d_skip=dd, conv_w=dconv_w,
                 lam_re=dlam_re.reshape(2, NG, NP), lam_im=dlam_im.reshape(2, NG, NP),
                 bb_re=dbb_re, bb_im=dbb_im, c_re=dc_re, c_im=dc_im, loss=loss.reshape(1))
    return grad_x, big, small, early


ANY = pl.BlockSpec(memory_space=pl.ANY)


def _coords():
    return lax.axis_index("x"), lax.axis_index("y"), lax.axis_index("c")


def _flip(v, b):
    return v + b - 2 * v * b if b else v


def _all_gather(shards, name):
    n = len(shards)

    def body(*refs):
        _gather_start(refs[:n], refs[n:2 * n], *refs[2 * n:])
        _gather_finish(refs[:n], refs[n:2 * n], *refs[2 * n:])

    return pl.pallas_call(
        body, name=name,
        in_specs=[ANY] * n, out_specs=[ANY] * n,
        out_shape=_gather_shapes(shards), scratch_shapes=_gather_sems(n),
    )(*shards)


def _gather_shapes(shards):
    return [jax.ShapeDtypeStruct((NDEV * s.shape[0], s.shape[1]), s.dtype) for s in shards]


def _gather_sems(n):
    return [pltpu.SemaphoreType.DMA((7 * n,)), pltpu.SemaphoreType.DMA((7 * n,)), pltpu.SemaphoreType.DMA((n,))]


def _gather_copies(ins, outs, send_sems, recv_sems, local_sems, a):
    x, y, c = _coords()
    me, sibling = (x, y, c), (x, y, 1 - c)
    chips = [(1 - x, y), (x, 1 - y), (1 - x, 1 - y)]
    r = ins[a].shape[0]

    def rows(px, py, pc):
        return outs[a].at[pl.ds(pl.multiple_of((4 * px + 2 * py + pc) * r, 8), r), :]

    def copy(k, block, to, src=None):
        return pltpu.make_async_remote_copy(
            src_ref=rows(*block) if src is None else src, dst_ref=rows(*block),
            send_sem=send_sems.at[a * 7 + k], recv_sem=recv_sems.at[a * 7 + k],
            device_id=to, device_id_type=pl.DeviceIdType.MESH)

    mine = pltpu.make_async_copy(ins[a], rows(*me), local_sems.at[a])
    first = [copy(0, me, sibling, src=ins[a])]
    first += [copy(1 + j, me, (*chip, c), src=ins[a]) for j, chip in enumerate(chips)]
    passed = [copy(4 + j, (*chip, c), sibling) for j, chip in enumerate(chips)]
    arrivals = [copy(1 + j, (*chip, c), me) for j, chip in enumerate(chips)]
    from_sibling = [copy(0, sibling, me)] + [copy(4 + j, (*chip, 1 - c), me) for j, chip in enumerate(chips)]
    return mine, first, passed, arrivals, from_sibling


def _gather_start(ins, outs, send_sems, recv_sems, local_sems):
    for a in range(len(ins)):
        mine, first, _, _, _ = _gather_copies(ins, outs, send_sems, recv_sems, local_sems, a)
        mine.start()
        for cp in first:
            cp.start()


def _gather_finish(ins, outs, send_sems, recv_sems, local_sems):
    n = len(ins)
    parts = [_gather_copies(ins, outs, send_sems, recv_sems, local_sems, a) for a in range(n)]
    for mine, first, passed, arrivals, from_sibling in parts:
        for arrived, onward in zip(arrivals, passed):
            arrived.wait_recv()
            onward.start()
    for mine, first, passed, arrivals, from_sibling in parts:
        for cp in from_sibling:
            cp.wait_recv()
        for cp in first + passed:
            cp.wait_send()
        mine.wait()


NCHIP = 4
CHIP_FLIPS = ((1, 0), (0, 1), (1, 1))


def _planned_copies(ins, outs, send_sems, recv_sems, plan):
    return [pltpu.make_async_remote_copy(
        src_ref=ins[a].at[src], dst_ref=outs[a].at[dst], send_sem=send_sems.at[k], recv_sem=recv_sems.at[k],
        device_id=to, device_id_type=pl.DeviceIdType.MESH) for k, (a, src, dst, to) in enumerate(plan)]


def _start_all(copies):
    for cp in copies:
        cp.start()


def _wait_all(copies):
    for cp in copies:
        cp.wait_recv()
    for cp in copies:
        cp.wait_send()


def _exchange_cores(parts, name):
    n = len(parts)

    def body(*refs):
        x, y, c = _coords()
        plan = [(a, 2 * q + 1 - c, q, (x, y, 1 - c)) for a in range(n) for q in range(NCHIP)]
        copies = _planned_copies(refs[:n], refs[n:2 * n], *refs[2 * n:], plan)
        _start_all(copies)
        _wait_all(copies)

    return pl.pallas_call(
        body, name=name, in_specs=[ANY] * n, out_specs=[ANY] * n,
        out_shape=[jax.ShapeDtypeStruct((NCHIP,) + s.shape[1:], s.dtype) for s in parts],
        scratch_shapes=[pltpu.SemaphoreType.DMA((NCHIP * n,)), pltpu.SemaphoreType.DMA((NCHIP * n,))],
    )(*parts)


def _chips_copies(ins, outs, send_sems, recv_sems):
    x, y, c = _coords()
    plan = []
    for a in range(len(ins)):
        for j, (fx, fy) in enumerate(CHIP_FLIPS):
            px, py = _flip(x, fx), _flip(y, fy)
            plan.append((a, 2 * px + py, j, (px, py, c)))
    return _planned_copies(ins, outs, send_sems, recv_sems, plan)


def _chips_shapes(parts):
    return [jax.ShapeDtypeStruct((3,) + s.shape[1:], s.dtype) for s in parts]


def _chips_sems(n):
    return [pltpu.SemaphoreType.DMA((3 * n,)), pltpu.SemaphoreType.DMA((3 * n,))]


def _exchange_chips(parts, name):
    n = len(parts)

    def body(*refs):
        copies = _chips_copies(refs[:n], refs[n:2 * n], *refs[2 * n:])
        _start_all(copies)
        _wait_all(copies)

    return pl.pallas_call(
        body, name=name, in_specs=[ANY] * n, out_specs=[ANY] * n,
        out_shape=_chips_shapes(parts), scratch_shapes=_chips_sems(n),
    )(*parts)


def _pair_sum(where, part, recv, wire_dtype, name):
    _, r, c = part.shape
    tr = _pick(r, 256, 16)

    def body(w_ref, p_ref, r_ref, pb_ref, own_ref):
        s = p_ref[...] + r_ref[...]
        pb_ref[...] = s.astype(wire_dtype)

        @pl.when(pl.program_id(1) == w_ref[1])
        def _():
            own_ref[...] = s

    return pl.pallas_call(
        body, name=name,
        grid_spec=pltpu.PrefetchScalarGridSpec(
            num_scalar_prefetch=1, grid=(r // tr, NCHIP),
            in_specs=[pl.BlockSpec((None, tr, c), lambda i, q, w: (2 * q + w[0], i, 0)),
                      pl.BlockSpec((None, tr, c), lambda i, q, w: (q, i, 0))],
            out_specs=[pl.BlockSpec((None, tr, c), lambda i, q, w: (q, i, 0)),
                       pl.BlockSpec((tr, c), lambda i, q, w: (i, 0))]),
        out_shape=[jax.ShapeDtypeStruct((NCHIP, r, c), wire_dtype), jax.ShapeDtypeStruct((r, c), F32)],
        compiler_params=_cp(("parallel", "arbitrary")),
    )(where, part, recv)


def _chip_sum(own, recv, name):
    r, c = own.shape
    tr = _pick(r, 256, 16)

    def body(o_ref, r_ref, out_ref):
        acc = o_ref[...]
        for j in range(3):
            acc = acc + r_ref[j].astype(F32)
        out_ref[...] = acc

    return pl.pallas_call(
        body, name=name, grid=(r // tr,),
        in_specs=[pl.BlockSpec((tr, c), lambda i: (i, 0)), pl.BlockSpec((3, tr, c), lambda i: (0, i, 0))],
        out_specs=pl.BlockSpec((tr, c), lambda i: (i, 0)),
        out_shape=jax.ShapeDtypeStruct((r, c), F32),
        compiler_params=_cp(("parallel",)),
    )(own, recv)


def _adamw(w, g, m, v, name):
    r, c = w.shape
    tr = _pick(r, 256, 8)

    def body(w_ref, g_ref, m_ref, v_ref, d_ref, nm_ref, nv_ref):
        _adamw_refs(w_ref, g_ref, m_ref, v_ref, d_ref, nm_ref, nv_ref)

    blk = pl.BlockSpec((tr, c), lambda i: (i, 0))
    return pl.pallas_call(
        body, name=name, grid=(r // tr,),
        in_specs=[blk] * 4, out_specs=[blk] * 3,
        out_shape=[jax.ShapeDtypeStruct((r, c), F32)] * 3,
        compiler_params=_cp(("parallel",)),
    )(w, g, m, v)


def _adamw_refs(w_ref, g_ref, m_ref, v_ref, d_ref, nm_ref, nv_ref):
    gv = g_ref[...]
    nm = B1 * m_ref[...] + (1.0 - B1) * gv
    nv = B2 * v_ref[...] + (1.0 - B2) * (gv * gv)
    nm_ref[...] = nm
    nv_ref[...] = nv
    d_ref[...] = -LR * ((nm / C1) / (jnp.sqrt(nv / C2) + AEPS) + WD * w_ref[...])


def _adamw_small(ws, gs, ms, vs, name):
    n = len(ws)

    def body(*refs):
        groups = [refs[i * n:(i + 1) * n] for i in range(7)]
        for per_param in zip(*groups):
            _adamw_refs(*per_param)

    vm = pl.BlockSpec(memory_space=pltpu.VMEM)
    outs = pl.pallas_call(
        body, name=name, in_specs=[vm] * (4 * n), out_specs=[vm] * (3 * n),
        out_shape=[jax.ShapeDtypeStruct(a.shape, F32) for a in ws] * 3,
    )(*ws, *gs, *ms, *vs)
    return outs[:n], outs[n:2 * n], outs[2 * n:]


def _swap(a):
    return jnp.swapaxes(a, -1, -2)


VIEWS = {
    "w_in": (lambda a: a[0].T, lambda u: u.T[None]),
    "w_up": (lambda a: a[0].T, lambda u: u.T[None]),
    "w_glu": (lambda a: a[0], lambda u: u[None]),
    "w_out": (lambda a: a[0], lambda u: u[None]),
    "w_down": (lambda a: a[0], lambda u: u[None]),
    "conv_w": (lambda a: a[0], lambda u: u[None]),
    "norm_mix_g": (lambda a: a, lambda u: u),
    "norm_attn_g": (lambda a: a, lambda u: u),
    "norm_ssm_g": (lambda a: a, lambda u: u),
    "norm_ffn_g": (lambda a: a, lambda u: u),
    "norm_final_g": (lambda a: a[None], lambda u: u[0]),
    "conv_b": (lambda a: a, lambda u: u),
    "sink": (lambda a: a, lambda u: u),
    "a_re": (lambda a: a.reshape(2 * NG, NP), lambda u: u.reshape(1, 2, NG, NP)),
    "a_im": (lambda a: a.reshape(2 * NG, NP), lambda u: u.reshape(1, 2, NG, NP)),
    "log_step": (lambda a: a[0], lambda u: u[None]),
    "b_re": (lambda a: _swap(a[0]).reshape(2 * NG * GC, NP), lambda u: _swap(u.reshape(2, NG, GC, NP))[None]),
    "b_im": (lambda a: _swap(a[0]).reshape(2 * NG * GC, NP), lambda u: _swap(u.reshape(2, NG, GC, NP))[None]),
    "c_re": (lambda a: a.reshape(2 * NG * GC, NP), lambda u: u.reshape(1, 2, NG, GC, NP)),
    "c_im": (lambda a: a.reshape(2 * NG * GC, NP), lambda u: u.reshape(1, 2, NG, GC, NP)),
    "d_skip": (lambda a: a[0].T, lambda u: u.T[None]),
}
BIG = ["w_in", "w_glu", "w_out", "w_up", "w_down"]
PACK_W = 1024


def _pack(arrs, rows):
    flat = jnp.concatenate([a.reshape(-1).astype(F32) for a in arrs])
    return jnp.pad(flat, (0, rows * PACK_W - flat.shape[0])).reshape(rows, PACK_W)


def _unpack(packed, shapes):
    flat = packed.reshape(-1)
    out, off = [], 0
    for s in shapes:
        size = math.prod(s)
        out.append(flat[off:off + size].reshape(s))
        off += size
    return out


def kernel(x, norm_mix_g, w_in, a_re, a_im, log_step, b_re, b_im, c_re, c_im, d_skip, w_glu, sink, norm_attn_g, norm_ssm_g, w_out, norm_ffn_g, w_up, conv_w, conv_b, w_down, norm_final_g, loss_target, m_norm_mix_g, m_w_in, m_a_re, m_a_im, m_log_step, m_b_re, m_b_im, m_c_re, m_c_im, m_d_skip, m_w_glu, m_sink, m_norm_attn_g, m_norm_ssm_g, m_w_out, m_norm_ffn_g, m_w_up, m_conv_w, m_conv_b, m_w_down, m_norm_final_g, v_norm_mix_g, v_w_in, v_a_re, v_a_im, v_log_step, v_b_re, v_b_im, v_c_re, v_c_im, v_d_skip, v_w_glu, v_sink, v_norm_attn_g, v_norm_ssm_g, v_w_out, v_norm_ffn_g, v_w_up, v_conv_w, v_conv_b, v_w_down, v_norm_final_g):
    args = dict(locals())
    names = ["norm_mix_g", "w_in", "a_re", "a_im", "log_step", "b_re", "b_im", "c_re", "c_im", "d_skip", "w_glu",
             "sink", "norm_attn_g", "norm_ssm_g", "w_out", "norm_ffn_g", "w_up", "conv_w", "conv_b", "w_down",
             "norm_final_g"]
    w = {k: args[k] for k in names}
    m = {k: args["m_" + k] for k in names}
    v = {k: args["v_" + k] for k in names}

    shards = [w_in[0].T.astype(BF16), w_glu[0].astype(BF16), w_out[0].astype(BF16),
              jnp.pad(conv_w[0], ((0, 5), (0, 0)))]
    w_in_t, w_glu_f, w_out_f, conv_w_g = _all_gather(shards, "gather_weights")
    conv_w_f = conv_w_g.reshape(NDEV, 8, 2 * DFF // NDEV)[:, :3].transpose(1, 0, 2).reshape(3, 2 * DFF)
    ffn_shards = [w_up[0].T.astype(BF16), w_down[0].astype(BF16)]

    ax, ay, ac = _coords()
    me = 4 * ax + 2 * ay + ac
    where = jnp.stack([ac, 2 * ax + ay]).astype(jnp.int32)
    own = {}

    def to_chip_sums(named_parts, tag):
        ks = [k for k, _ in named_parts]
        parts = [part for _, part in named_parts]
        from_core = _exchange_cores(parts, "exchange_cores_" + tag)
        wire = []
        for k, part, got in zip(ks, parts, from_core):
            per_chip, own[k] = _pair_sum(where, part, got, F32 if k == "small" else BF16, "pair_sum_" + k)
            wire.append(per_chip)
        return wire

    def split8(g):
        return g.reshape(NDEV, g.shape[0] // NDEV, g.shape[1])

    def attend(qkv, sink_):
        attn, lse, (w_up_t, w_down_f) = _attn_fwd(qkv, sink_, gather=ffn_shards)
        return attn, lse, w_up_t, w_down_f

    def reduce_early(k, dw):
        return to_chip_sums([(k, split8(dw))], k)

    p = {k: w[k][0] for k in ("norm_mix_g", "a_re", "a_im", "log_step", "c_re", "c_im", "d_skip", "sink",
                              "norm_attn_g", "norm_ssm_g", "norm_ffn_g", "conv_b")}
    p["norm_final_g"] = norm_final_g
    p["bt_re"], p["bt_im"] = _swap(b_re[0]), _swap(b_im[0])
    grad_x, big, small, early = _local_step(x[0], loss_target[0], w_in_t, w_glu_f, w_out_f, conv_w_f, p,
                                            attend, reduce_early)

    small_names = list(small.keys())
    small_shapes = [small[k].shape for k in small_names]
    n_small = sum(math.prod(s) for s in small_shapes)
    rows_dev = -(-n_small // (PACK_W * NDEV * 16)) * 16
    spack = _pack([small[k] for k in small_names], rows_dev * NDEV)
    late_names = ["w_in_t", "w_glu", "w_out", "small"]
    late_parts = [split8(big[k]) for k in late_names[:-1]] + [spack.reshape(NDEV, rows_dev, PACK_W)]
    from_chips = _exchange_chips(to_chip_sums(list(zip(late_names, late_parts)), "late"), "exchange_chips")
    got = dict(zip(late_names, from_chips), w_down=early["w_down"][0], w_up_t=early["w_up_t"][0])
    red = {k: _chip_sum(own[k], got[k], "chip_sum_" + k) for k in got}
    (small_full,) = _all_gather([red["small"]], "gather_small")
    sm = dict(zip(small_names, _unpack(small_full, small_shapes)))

    _, s5_vjp = jax.vjp(_s5_params, a_re[0], a_im[0], log_step[0], p["bt_re"], p["bt_im"])
    da_re, da_im, dlog_step, dbt_re, dbt_im = s5_vjp((sm["lam_re"], sm["lam_im"], sm["bb_re"], sm["bb_im"]))
    gview = {
        "norm_mix_g": sm["norm_mix_g"], "norm_attn_g": sm["norm_attn_g"], "norm_ssm_g": sm["norm_ssm_g"],
        "norm_ffn_g": sm["norm_ffn_g"], "norm_final_g": sm["norm_final_g"], "conv_b": sm["conv_b"],
        "sink": sm["sink"][None], "a_re": da_re.reshape(2 * NG, NP), "a_im": da_im.reshape(2 * NG, NP),
        "log_step": dlog_step, "b_re": dbt_re.reshape(2 * NG * GC, NP), "b_im": dbt_im.reshape(2 * NG * GC, NP),
        "c_re": sm["c_re"].reshape(2 * NG * GC, NP), "c_im": sm["c_im"].reshape(2 * NG * GC, NP),
        "d_skip": sm["d_skip"].reshape(NG, GC).T,
        "w_in": red["w_in_t"], "w_glu": red["w_glu"], "w_out": red["w_out"], "w_up": red["w_up_t"],
        "w_down": red["w_down"],
        "conv_w": lax.dynamic_slice_in_dim(sm["conv_w"], me * (2 * DFF // NDEV), 2 * DFF // NDEV, axis=1),
    }

    dview, mview, vview = {}, {}, {}
    for k in BIG:
        to = VIEWS[k][0]
        dview[k], mview[k], vview[k] = _adamw(to(w[k]), gview[k], to(m[k]), to(v[k]), "adamw_" + k)
    rest = [k for k in names if k not in BIG]
    outs = _adamw_small([VIEWS[k][0](w[k]) for k in rest], [gview[k] for k in rest],
                        [VIEWS[k][0](m[k]) for k in rest], [VIEWS[k][0](v[k]) for k in rest], "adamw_small")
    for dst, vals in zip((dview, mview, vview), outs):
        dst.update(dict(zip(rest, vals)))

    def back(views):
        return [VIEWS[k][1](views[k]) for k in names]

    return (sm["loss"][0], grad_x[None], *back(gview), *back(dview), *back(mview), *back(vview))
```

```python
import functools
import math

import jax
import jax.numpy as jnp
from jax import lax
from jax.experimental import pallas as pl
from jax.experimental.pallas import tpu as pltpu

F32 = jnp.float32
BF16 = jnp.bfloat16

L = 4096
D = 1024
NQ, NKV, HD = 8, 2, 64
AW = NQ * HD
KVW = NKV * HD
SW = 512
NG, GC, NP = 32, 16, 64
INW = AW + 2 * KVW + SW
DFF = 2816
BLK = 128
WIN = 3 * BLK
EPS = 1e-6
ROPE_THETA = 500000.0
NSEG = 32
TSEG = L // NSEG
SBW = 256
NSB = NG * NP // SBW
NDEV = 8
MESH_AXES = ("x", "y", "c")

LR, B1, B2, AEPS, WD, STEP = 0.001, 0.9, 0.999, 1e-08, 0.01, 10
C1 = 1.0 - B1 ** STEP
C2 = 1.0 - B2 ** STEP

VMEM_LIMIT = 56 * 1024 * 1024


def _pick(n, target, mult):
    best = None
    for t in range(mult, min(n, target) + 1, mult):
        if n % t == 0:
            best = t
    return best if best is not None else n


def _cp(sem):
    return pltpu.CompilerParams(dimension_semantics=sem, vmem_limit_bytes=VMEM_LIMIT)


def _mm(a, b, *, ta=False, tb=False, out_dtype=F32, add=None, ride=(), post=None, name, tm=1024, tn=1024, tk=1024):
    m, k = (a.shape[1], a.shape[0]) if ta else a.shape
    n = b.shape[0] if tb else b.shape[1]
    assert k == (b.shape[1] if tb else b.shape[0])
    tm, tn, tk = _pick(m, tm, 128), _pick(n, tn, 128), _pick(k, tk, 128)
    grid = (m // tm, n // tn, k // tk)
    nk = grid[2]
    dn = (((0 if ta else 1,), (1 if tb else 0,)), ((), ()))
    n_in = 2 + (add is not None)
    nr = len(ride)
    post_fn, post_ins, post_outs = post if post is not None else (None, (), ())
    n_pi = len(post_ins)
    n_out = len(post_outs) if post is not None else 1
    assert post is None or grid[1] == 1

    def body(*refs):
        a_ref, b_ref = refs[0], refs[1]
        pin = refs[n_in:n_in + n_pi]
        base = n_in + n_pi + nr
        o_refs = refs[base:base + n_out]
        acc_ref = refs[base + n_out + nr]
        step = [pl.program_id(d) for d in range(3)]
        kk = step[2]
        if nr:
            riders = (refs[n_in + n_pi:base], refs[base + n_out:base + n_out + nr], *refs[base + n_out + nr + 1:])

            @pl.when((step[0] == 0) & (step[1] == 0) & (kk == 0))
            def _():
                _start_all(_chips_copies(*riders))

        prod = lax.dot_general(a_ref[...].astype(BF16), b_ref[...].astype(BF16), dn, preferred_element_type=F32)

        def finish(r):
            if add is not None:
                r = r + refs[2][...]
            if post_fn is None:
                o_refs[0][...] = r.astype(out_dtype)
            else:
                post_fn(r, step[0], pin, o_refs)

        if nk == 1:
            finish(prod)
        else:
            @pl.when(kk == 0)
            def _():
                acc_ref[...] = prod

            @pl.when((kk > 0) & (kk < nk - 1))
            def _():
                acc_ref[...] += prod

            @pl.when(kk == nk - 1)
            def _():
                finish(acc_ref[...] + prod)

        if nr:
            @pl.when((step[0] == grid[0] - 1) & (step[1] == grid[1] - 1) & (kk == nk - 1))
            def _():
                _wait_all(_chips_copies(*riders))

    a_spec = pl.BlockSpec((tk, tm), lambda i, j, kk: (kk, i)) if ta else pl.BlockSpec((tm, tk), lambda i, j, kk: (i, kk))
    b_spec = pl.BlockSpec((tn, tk), lambda i, j, kk: (j, kk)) if tb else pl.BlockSpec((tk, tn), lambda i, j, kk: (kk, j))
    def row_spec(shape):
        return pl.BlockSpec((tm if shape[0] == m else shape[0], shape[1]),
                            (lambda i, j, kk: (i, 0)) if shape[0] == m else (lambda i, j, kk: (0, 0)))

    in_specs = [a_spec, b_spec]
    args = [a, b]
    if add is not None:
        in_specs.append(pl.BlockSpec((tm, tn), lambda i, j, kk: (i, j)))
        args.append(add)
    if post is None:
        main_specs = [pl.BlockSpec((tm, tn), lambda i, j, kk: (i, j))]
        main_shapes = [jax.ShapeDtypeStruct((m, n), out_dtype)]
    else:
        main_specs = [row_spec(s.shape) for s in post_outs]
        main_shapes = list(post_outs)
    outs = pl.pallas_call(
        body, name=name, grid=grid,
        in_specs=in_specs + [row_spec(p.shape) for p in post_ins] + [ANY] * nr,
        out_specs=main_specs + [ANY] * nr,
        out_shape=main_shapes + _chips_shapes(ride),
        scratch_shapes=[pltpu.VMEM((tm, tn) if nk > 1 else (8, 128), F32)] + (_chips_sems(nr) if nr else []),
        compiler_params=_cp(("arbitrary",) * 3 if (nr or post is not None) else ("parallel", "parallel", "arbitrary")),
    )(*args, *post_ins, *ride)
    main = outs[0] if post is None else list(outs[:n_out])
    return (main, list(outs[n_out:])) if nr else main


TL = 512


def _rms(xv, gv):
    return xv * lax.rsqrt(jnp.mean(xv * xv, axis=-1, keepdims=True) + EPS) * gv


def _rows(width):
    return pl.BlockSpec((TL, width), lambda i: (i, 0))


def _whole(shape):
    return pl.BlockSpec(shape, lambda i: (0,) * len(shape))


def _in_proj(x, g, w_in_t, tabs):
    qkw = AW + 2 * KVW

    def body(x_ref, g_ref, w_ref, c_ref, sa_ref, sb_ref, h_ref, qkv_ref, u_ref):
        h = _rms(x_ref[...], g_ref[...]).astype(BF16)
        h_ref[...] = h
        proj = lax.dot_general(h, w_ref[...], _NT, preferred_element_type=F32)
        for j in range(qkw // 128):
            cols = slice(j * 128, (j + 1) * 128)
            xv = proj[:, cols]
            if j < (AW + KVW) // 128:
                xv = _rope(xv, c_ref[...], sa_ref[...], sb_ref[...], 1.0)
            qkv_ref[:, cols] = xv.astype(BF16)
        u_ref[...] = proj[:, qkw:]

    return pl.pallas_call(
        body, name="in_proj", grid=(L // TL,),
        in_specs=[_rows(D), _whole((1, D)), _whole((INW, D)), _rows(128), _rows(128), _rows(128)],
        out_specs=[_rows(D), _rows(qkw), _rows(SW)],
        out_shape=[jax.ShapeDtypeStruct((L, D), BF16), jax.ShapeDtypeStruct((L, qkw), BF16),
                   jax.ShapeDtypeStruct((L, SW), F32)],
        compiler_params=_cp(("parallel",)),
    )(x, g, w_in_t, *tabs)


def _in_proj_dx(dproj, w_in_t, x, g, dres):
    def body(dp_ref, w_ref, x_ref, g_ref, dres_ref, dx_ref, dg_ref):
        dh = jnp.dot(dp_ref[...], w_ref[...], preferred_element_type=F32)
        dx, dg = _rms_bwd_tile(x_ref[...], g_ref[...], dh)
        dx_ref[...] = dx + dres_ref[...]

        @pl.when(pl.program_id(0) == 0)
        def _():
            dg_ref[...] = jnp.zeros_like(dg_ref)

        dg_ref[...] += dg

    return pl.pallas_call(
        body, name="in_proj_dx", grid=(L // TL,),
        in_specs=[_rows(INW), _whole((INW, D)), _rows(D), _whole((1, D)), _rows(D)],
        out_specs=[_rows(D), _whole((1, D))],
        out_shape=[jax.ShapeDtypeStruct((L, D), F32), jax.ShapeDtypeStruct((1, D), F32)],
        compiler_params=_cp(("arbitrary",)),
    )(dproj, w_in_t, x, g, dres)


def _rms_bwd_tile(xv, gv, dh):
    r = lax.rsqrt(jnp.mean(xv * xv, axis=-1, keepdims=True) + EPS)
    a = dh * gv
    dx = r * a - xv * (r * r * r) * jnp.mean(a * xv, axis=-1, keepdims=True)
    dg = jnp.sum(dh * xv * r, axis=0, keepdims=True)
    return dx, dg


def _rms_bwd_post(dh, i, ins, outs):
    x_ref, dres_ref, g_ref = ins
    dx_ref, dg_ref = outs
    dx, dg = _rms_bwd_tile(x_ref[...], g_ref[...], dh)
    dx_ref[...] = dx + dres_ref[...]

    @pl.when(i == 0)
    def _():
        dg_ref[...] = jnp.zeros_like(dg_ref)

    dg_ref[...] += dg


def _final_post(xv, i, ins, outs):
    t_ref, g_ref = ins
    loss_ref, dx_ref, dg_ref = outs

    @pl.when(i == 0)
    def _():
        loss_ref[...] = jnp.zeros_like(loss_ref)
        dg_ref[...] = jnp.zeros_like(dg_ref)

    gv = g_ref[...]
    r = lax.rsqrt(jnp.mean(xv * xv, axis=-1, keepdims=True) + EPS)
    e = xv * r * gv - t_ref[...]
    loss_ref[...] += 0.5 * jnp.sum(jnp.mean(e * e, axis=-1, keepdims=True), axis=0, keepdims=True)
    dy = e * (1.0 / D)
    a = dy * gv
    dx_ref[...] = r * a - xv * (r * r * r) * jnp.mean(a * xv, axis=-1, keepdims=True)
    dg_ref[...] += jnp.sum(dy * xv * r, axis=0, keepdims=True)


def _out_proj(attn, ysg, ga, gs, w_out, x, gf):
    def body(a_ref, s_ref, ga_ref, gs_ref, w_ref, x_ref, gf_ref, m_ref, x1_ref, h2_ref):
        m_ref[:, 0:AW] = _rms(a_ref[...], ga_ref[...]).astype(BF16)
        m_ref[:, AW:AW + SW] = _rms(s_ref[...], gs_ref[...]).astype(BF16)
        x1 = jnp.dot(m_ref[...], w_ref[...], preferred_element_type=F32) + x_ref[...]
        x1_ref[...] = x1
        h2_ref[...] = _rms(x1, gf_ref[...]).astype(BF16)

    return pl.pallas_call(
        body, name="out_proj", grid=(L // TL,),
        in_specs=[_rows(AW), _rows(SW), _whole((1, AW)), _whole((1, SW)), _whole((D, D)), _rows(D), _whole((1, D))],
        out_specs=[_rows(D), _rows(D), _rows(D)],
        out_shape=[jax.ShapeDtypeStruct((L, D), BF16), jax.ShapeDtypeStruct((L, D), F32),
                   jax.ShapeDtypeStruct((L, D), BF16)],
        compiler_params=_cp(("parallel",)),
    )(attn, ysg, ga, gs, w_out, x, gf)


def _out_proj_dx(dx1, w_out, attn, ysg, ga, gs):
    def body(dx_ref, w_ref, a_ref, s_ref, ga_ref, gs_ref, da_ref, ds_ref, dga_ref, dgs_ref):
        @pl.when(pl.program_id(0) == 0)
        def _():
            dga_ref[...] = jnp.zeros_like(dga_ref)
            dgs_ref[...] = jnp.zeros_like(dgs_ref)

        dm = lax.dot_general(dx_ref[...].astype(BF16), w_ref[...], _NT, preferred_element_type=F32)
        dxa, dga = _rms_bwd_tile(a_ref[...], ga_ref[...], dm[:, 0:AW])
        da_ref[...] = dxa
        dga_ref[...] += dga
        dxs, dgs = _rms_bwd_tile(s_ref[...], gs_ref[...], dm[:, AW:AW + SW])
        ds_ref[...] = dxs
        dgs_ref[...] += dgs

    return pl.pallas_call(
        body, name="out_proj_dx", grid=(L // TL,),
        in_specs=[_rows(D), _whole((D, D)), _rows(AW), _rows(SW), _whole((1, AW)), _whole((1, SW))],
        out_specs=[_rows(AW), _rows(SW), _whole((1, AW)), _whole((1, SW))],
        out_shape=[jax.ShapeDtypeStruct((L, AW), F32), jax.ShapeDtypeStruct((L, SW), F32),
                   jax.ShapeDtypeStruct((1, AW), F32), jax.ShapeDtypeStruct((1, SW), F32)],
        compiler_params=_cp(("arbitrary",)),
    )(dx1, w_out, attn, ysg, ga, gs)


def _rope_tables():
    half = HD // 8
    inv_freq = jnp.power(ROPE_THETA, -jnp.arange(half, dtype=F32) / half)
    ang = jnp.arange(L, dtype=F32)[:, None] * inv_freq[None, :]
    cos, sin = jnp.cos(ang), jnp.sin(ang)
    one = jnp.ones((L, HD - 2 * half), F32)
    zero = jnp.zeros((L, HD - 2 * half), F32)
    zh = jnp.zeros((L, half), F32)
    cos64 = jnp.concatenate([cos, cos, one], axis=1)
    sa64 = jnp.concatenate([-sin, zh, zero], axis=1)
    sb64 = jnp.concatenate([zh, sin, zero], axis=1)
    return [jnp.tile(t, (1, 2)) for t in (cos64, sa64, sb64)]


def _rope(xv, cosv, sav, sbv, sign):
    return xv * cosv + sign * (pltpu.roll(xv, 120, 1) * sav + pltpu.roll(xv, 8, 1) * sbv)


def _rope_bwd(dq, dk, dv, du, tabs):
    def body(dq_ref, dk_ref, dv_ref, du_ref, c_ref, sa_ref, sb_ref, o_ref):
        for j in range(AW // 128):
            cols = slice(j * 128, (j + 1) * 128)
            o_ref[:, cols] = _rope(dq_ref[:, cols], c_ref[...], sa_ref[...], sb_ref[...], -1.0).astype(BF16)
        o_ref[:, AW:AW + KVW] = _rope(dk_ref[...], c_ref[...], sa_ref[...], sb_ref[...], -1.0).astype(BF16)
        o_ref[:, AW + KVW:AW + 2 * KVW] = dv_ref[...].astype(BF16)
        o_ref[:, AW + 2 * KVW:] = du_ref[...].astype(BF16)

    def row(width):
        return pl.BlockSpec((TL, width), lambda i: (i, 0))

    return pl.pallas_call(
        body, name="rope_bwd", grid=(L // TL,),
        in_specs=[row(AW), row(KVW), row(KVW), row(SW), row(128), row(128), row(128)],
        out_specs=row(INW), out_shape=jax.ShapeDtypeStruct((L, INW), BF16),
        compiler_params=_cp(("parallel",)),
    )(dq, dk, dv, du, *tabs)


def _attn_window(n):
    start = pl.multiple_of(jnp.clip((n - 1) * BLK, 0, L - WIN), BLK)
    qpos = n * BLK + lax.broadcasted_iota(jnp.int32, (BLK, WIN), 0)
    kpos = start + lax.broadcasted_iota(jnp.int32, (BLK, WIN), 1)
    return start, jnp.abs(kpos - qpos) <= BLK


_NT = (((1,), (1,)), ((), ()))
_TN = (((0,), (0,)), ((), ()))
NEG = -1e30


def _attn_fwd(qkv, sink, gather=()):
    ng = len(gather)

    def body(sink_ref, q_ref, k_ref, v_ref, *rest):
        o_ref, lse_ref = rest[ng], rest[ng + 1]
        n = pl.program_id(0)
        if ng:
            travellers = (rest[:ng], rest[ng + 2:2 * ng + 2], *rest[2 * ng + 2:])

            @pl.when(n == 0)
            def _():
                _gather_start(*travellers)

            @pl.when(n == L // BLK - 1)
            def _():
                _gather_finish(*travellers)

        start, valid = _attn_window(n)
        kw = k_ref[pl.ds(start, WIN), :]
        vw = v_ref[pl.ds(start, WIN), :]
        for h in range(NQ):
            kv = h // (NQ // NKV)
            qh = q_ref[:, h * HD:(h + 1) * HD]
            kh = kw[:, kv * HD:(kv + 1) * HD]
            vh = vw[:, kv * HD:(kv + 1) * HD]
            s = lax.dot_general(qh, kh, _NT, preferred_element_type=F32) * (HD ** -0.5)
            s = jnp.where(valid, s, NEG)
            sk = sink_ref[h]
            m = jnp.maximum(jnp.max(s, axis=-1, keepdims=True), sk)
            p = jnp.exp(s - m)
            den = jnp.sum(p, axis=-1, keepdims=True) + jnp.exp(sk - m)
            o_ref[:, h * HD:(h + 1) * HD] = jnp.dot((p / den).astype(BF16), vh, preferred_element_type=F32)
            lse_ref[:, h:h + 1] = m + jnp.log(den)

    outs = pl.pallas_call(
        body, name="attn_fwd", grid=(L // BLK,),
        in_specs=[pl.BlockSpec(memory_space=pltpu.SMEM),
                  pl.BlockSpec((BLK, AW), lambda n: (n, 0)),
                  pl.BlockSpec((L, KVW), lambda n: (0, AW // KVW)),
                  pl.BlockSpec((L, KVW), lambda n: (0, AW // KVW + 1))] + [ANY] * ng,
        out_specs=[pl.BlockSpec((BLK, AW), lambda n: (n, 0)), pl.BlockSpec((BLK, NQ), lambda n: (n, 0))] + [ANY] * ng,
        out_shape=[jax.ShapeDtypeStruct((L, AW), F32), jax.ShapeDtypeStruct((L, NQ), F32)] + _gather_shapes(gather),
        scratch_shapes=_gather_sems(ng) if ng else [],
        compiler_params=_cp(("arbitrary",) if ng else ("parallel",)),
    )(sink, qkv, qkv, qkv, *gather)
    return outs[0], outs[1], list(outs[2:])


def _attn_bwd(qkv, sink, attn, lse, dattn):
    def body(sink_ref, q_ref, k_ref, v_ref, o_ref, lse_ref, do_ref, dq_ref, dk_ref, dv_ref, dsink_ref):
        n = pl.program_id(0)

        @pl.when(n == 0)
        def _():
            dk_ref[...] = jnp.zeros_like(dk_ref)
            dv_ref[...] = jnp.zeros_like(dv_ref)
            dsink_ref[...] = jnp.zeros_like(dsink_ref)

        start, valid = _attn_window(n)
        kw = k_ref[pl.ds(start, WIN), :]
        vw = v_ref[pl.ds(start, WIN), :]
        for kv in range(NKV):
            kh = kw[:, kv * HD:(kv + 1) * HD]
            vh = vw[:, kv * HD:(kv + 1) * HD]
            dk_acc = jnp.zeros((WIN, HD), F32)
            dv_acc = jnp.zeros((WIN, HD), F32)
            for h in range(kv * (NQ // NKV), (kv + 1) * (NQ // NKV)):
                qh = q_ref[:, h * HD:(h + 1) * HD]
                doh = do_ref[:, h * HD:(h + 1) * HD]
                dd = jnp.sum(doh * o_ref[:, h * HD:(h + 1) * HD], axis=-1, keepdims=True)
                lse_h = lse_ref[:, h:h + 1]
                s = lax.dot_general(qh, kh, _NT, preferred_element_type=F32) * (HD ** -0.5)
                p = jnp.where(valid, jnp.exp(s - lse_h), 0.0)
                dob = doh.astype(BF16)
                dp = lax.dot_general(dob, vh, _NT, preferred_element_type=F32)
                ds = (p * (dp - dd) * (HD ** -0.5)).astype(BF16)
                dq_ref[:, h * HD:(h + 1) * HD] = jnp.dot(ds, kh, preferred_element_type=F32)
                dk_acc += lax.dot_general(ds, qh, _TN, preferred_element_type=F32)
                dv_acc += lax.dot_general(p.astype(BF16), dob, _TN, preferred_element_type=F32)
                psink = jnp.exp(sink_ref[h] - lse_h)
                dsk = -jnp.sum(psink * dd, axis=0, keepdims=True)
                dsink_ref[h:h + 1, :] += jnp.broadcast_to(dsk, (1, 128))
            dk_ref[pl.ds(start, WIN), kv * HD:(kv + 1) * HD] += dk_acc
            dv_ref[pl.ds(start, WIN), kv * HD:(kv + 1) * HD] += dv_acc

    qblk = pl.BlockSpec((BLK, AW), lambda n: (n, 0))
    full = pl.BlockSpec((L, KVW), lambda n: (0, 0))
    return pl.pallas_call(
        body, name="attn_bwd", grid=(L // BLK,),
        in_specs=[pl.BlockSpec(memory_space=pltpu.SMEM), qblk,
                  pl.BlockSpec((L, KVW), lambda n: (0, AW // KVW)),
                  pl.BlockSpec((L, KVW), lambda n: (0, AW // KVW + 1)),
                  qblk, pl.BlockSpec((BLK, NQ), lambda n: (n, 0)), qblk],
        out_specs=[qblk, full, full, pl.BlockSpec((NQ, 128), lambda n: (0, 0))],
        out_shape=[jax.ShapeDtypeStruct((L, AW), F32), jax.ShapeDtypeStruct((L, KVW), F32),
                   jax.ShapeDtypeStruct((L, KVW), F32), jax.ShapeDtypeStruct((NQ, 128), F32)],
        compiler_params=_cp(("arbitrary",)),
    )(sink, qkv, qkv, qkv, attn, lse, dattn)


def _perm(a):
    return a.reshape(NSEG, TSEG, a.shape[1]).transpose(1, 0, 2).reshape(L, a.shape[1])


def _unperm(a):
    return a.reshape(TSEG, NSEG, a.shape[1]).transpose(1, 0, 2).reshape(L, a.shape[1])


def _cmul(ar, ai, br, bi):
    return ar * br - ai * bi, ar * bi + ai * br


def _scan_inplace(s_ref, lr, li, rev, visit=None, carried=(), out_ref=None):
    n = lr.shape[1]
    lr8 = jnp.broadcast_to(lr, (NSEG, n))
    li8 = jnp.broadcast_to(li, (NSEG, n))

    def rows(k):
        return pl.ds(pl.multiple_of(jnp.where(rev, TSEG - 1 - k, k) * NSEG, NSEG), NSEG)

    def step(k, c, store):
        sr, si = c
        rs = rows(k)
        pr, pi = _cmul(lr8, li8, sr, si)
        nr = pr + s_ref[rs, 0:n]
        ni = pi + s_ref[rs, n:2 * n]
        if store:
            dst = s_ref if out_ref is None else out_ref
            dst[rs, 0:n] = nr.astype(dst.dtype)
            dst[rs, n:2 * n] = ni.astype(dst.dtype)
        return nr, ni

    z = jnp.zeros((NSEG, n), F32)
    er, ei = lax.fori_loop(0, TSEG, functools.partial(step, store=False), (z, z))
    pr, pi = lr, li
    for _ in range(int(math.log2(TSEG))):
        pr, pi = _cmul(pr, pi, pr, pi)

    seg = lax.broadcasted_iota(jnp.int32, (NSEG, n), 0)

    def chain(order):
        cr = jnp.zeros((1, n), F32)
        ci = jnp.zeros((1, n), F32)
        outr = jnp.zeros((NSEG, n), F32)
        outi = jnp.zeros((NSEG, n), F32)
        for s in order:
            outr = jnp.where(seg == s, cr, outr)
            outi = jnp.where(seg == s, ci, outi)
            mr, mi = _cmul(pr, pi, cr, ci)
            cr, ci = mr + er[s:s + 1], mi + ei[s:s + 1]
        return outr, outi

    fr, fi = chain(range(NSEG))
    rr, ri = chain(range(NSEG - 1, -1, -1))
    cin_r = jnp.where(rev, rr, fr)
    cin_i = jnp.where(rev, ri, fi)
    if visit is None:
        lax.fori_loop(0, TSEG, functools.partial(step, store=True), (cin_r, cin_i))
        return cin_r, cin_i

    def visited(k, c):
        nr, ni = step(k, c[:2], True)
        return (nr, ni) + tuple(visit(k, nr, ni, c[2:]))

    fin = lax.fori_loop(0, TSEG - 1, visited, (cin_r, cin_i) + tuple(carried))
    last_r, last_i = step(TSEG - 1, fin[:2], True)
    return last_r, last_i, fin[2:]


S5_RC = 512


def _s5_specs():
    u_spec = pl.BlockSpec((L, 128), lambda cb, h, d: (0, cb))
    b_spec = pl.BlockSpec((None, None, 128, 2 * SBW), lambda cb, h, d: (d, cb * 2 + h, 0, 0))
    c_spec = pl.BlockSpec((None, None, 2 * SBW, 128), lambda cb, h, d: (d, cb * 2 + h, 0, 0))
    l_spec = pl.BlockSpec((None, None, 1, SBW), lambda cb, h, d: (d, cb * 2 + h, 0, 0))
    d_spec = pl.BlockSpec((1, 128), lambda cb, h, d: (0, cb))
    return u_spec, b_spec, c_spec, l_spec, d_spec


def _s5_fwd(u_p, bcat, ccat, lam_re, lam_im, dskip):
    def body(u_ref, b_ref, c_ref, lr_ref, li_ref, d_ref, y_ref, sb_ref, s_scr):
        first = (pl.program_id(1) == 0) & (pl.program_id(2) == 0)

        def proj(i, _):
            rs = pl.ds(pl.multiple_of(i * S5_RC, S5_RC), S5_RC)
            s_scr[rs, :] = jnp.dot(u_ref[rs, :].astype(BF16), b_ref[...], preferred_element_type=F32)
            return 0

        lax.fori_loop(0, L // S5_RC, proj, 0)
        _scan_inplace(s_scr, lr_ref[...], li_ref[...], pl.program_id(2) == 1, out_ref=sb_ref)

        def out(i, _):
            rs = pl.ds(pl.multiple_of(i * S5_RC, S5_RC), S5_RC)
            yv = jnp.dot(sb_ref[rs, :], c_ref[...], preferred_element_type=F32)

            @pl.when(first)
            def _():
                y_ref[rs, :] = d_ref[...] * u_ref[rs, :] + yv

            @pl.when(jnp.logical_not(first))
            def _():
                y_ref[rs, :] += yv

            return 0

        lax.fori_loop(0, L // S5_RC, out, 0)

    u_spec, b_spec, c_spec, l_spec, d_spec = _s5_specs()
    return pl.pallas_call(
        body, name="s5_fwd", grid=(SW // 128, 2, 2),
        in_specs=[u_spec, b_spec, c_spec, l_spec, l_spec, d_spec],
        out_specs=[u_spec, _s5_state_spec()],
        out_shape=[jax.ShapeDtypeStruct((L, SW), F32), jax.ShapeDtypeStruct((2, NSB, L, 2 * SBW), BF16)],
        scratch_shapes=[pltpu.VMEM((L, 2 * SBW), F32)],
        compiler_params=_cp(("parallel", "arbitrary", "arbitrary")),
    )(u_p, bcat, ccat, lam_re, lam_im, dskip)


def _s5_state_spec():
    return pl.BlockSpec((None, None, L, 2 * SBW), lambda cb, h, d: (d, cb * 2 + h, 0, 0))


def _s5_bwd(u_p, dy_p, states, bcat, ccat, lam_re, lam_im, dskip):
    def body(u_ref, dy_ref, s_ref, b_ref, c_ref, lr_ref, li_ref, d_ref,
             du_ref, db_ref, dc_ref, dlr_ref, dli_ref, dd_ref, g_scr, gb_scr):
        first = (pl.program_id(1) == 0) & (pl.program_id(2) == 0)
        rev = pl.program_id(2) == 1

        def dstate(i, _):
            rs = pl.ds(pl.multiple_of(i * S5_RC, S5_RC), S5_RC)
            g_scr[rs, :] = lax.dot_general(dy_ref[rs, :].astype(BF16), c_ref[...], _NT, preferred_element_type=F32)
            return 0

        lax.fori_loop(0, L // S5_RC, dstate, 0)

        def before(rows):
            sv = s_ref[rows, :].astype(F32)
            return sv[:, 0:SBW], sv[:, SBW:2 * SBW]

        def dlam(gr, gi, sr, si, ar, ai):
            return ar + gr * sr + gi * si, ai + gi * sr - gr * si

        def visit(k, gr, gi, acc):
            ts = jnp.where(rev, k + 1, TSEG - 2 - k)
            sr, si = before(pl.ds(pl.multiple_of(ts * NSEG, NSEG), NSEG))
            return dlam(gr, gi, sr, si, *acc)

        z = jnp.zeros((NSEG, SBW), F32)
        gr, gi, acc = _scan_inplace(g_scr, lr_ref[...], -li_ref[...], jnp.logical_not(rev), visit, (z, z), gb_scr)
        edge_r, edge_i = before(pl.ds(pl.multiple_of(jnp.where(rev, 0, TSEG - 1) * NSEG, NSEG), NSEG))
        seg = lax.broadcasted_iota(jnp.int32, (NSEG, SBW), 0)
        keep = seg != jnp.where(rev, NSEG - 1, 0)

        def neighbour(e):
            return jnp.where(keep, jnp.where(rev, pltpu.roll(e, NSEG - 1, 0), pltpu.roll(e, 1, 0)), 0.0)

        ar, ai = dlam(gr, gi, neighbour(edge_r), neighbour(edge_i), *acc)
        dlr_ref[...] = jnp.sum(ar, axis=0, keepdims=True)
        dli_ref[...] = jnp.sum(ai, axis=0, keepdims=True)

        db_ref[...] = jnp.zeros_like(db_ref)
        dc_ref[...] = jnp.zeros_like(dc_ref)

        @pl.when(first)
        def _():
            dd_ref[...] = jnp.zeros_like(dd_ref)

        def grads(i, _):
            rs = pl.ds(pl.multiple_of(i * S5_RC, S5_RC), S5_RC)
            uv = u_ref[rs, :]
            dyv = dy_ref[rs, :]
            gb = gb_scr[rs, :]
            db_ref[...] += lax.dot_general(uv.astype(BF16), gb, _TN, preferred_element_type=F32)
            dc_ref[...] += lax.dot_general(dyv.astype(BF16), s_ref[rs, :], _TN, preferred_element_type=F32)
            duv = lax.dot_general(gb, b_ref[...], _NT, preferred_element_type=F32)

            @pl.when(first)
            def _():
                du_ref[rs, :] = d_ref[...] * dyv + duv
                dd_ref[...] += jnp.sum(dyv * uv, axis=0, keepdims=True)

            @pl.when(jnp.logical_not(first))
            def _():
                du_ref[rs, :] += duv

            return 0

        lax.fori_loop(0, L // S5_RC, grads, 0)

    u_spec, b_spec, c_spec, l_spec, d_spec = _s5_specs()
    return pl.pallas_call(
        body, name="s5_bwd", grid=(SW // 128, 2, 2),
        in_specs=[u_spec, u_spec, _s5_state_spec(), b_spec, c_spec, l_spec, l_spec, d_spec],
        out_specs=[u_spec, b_spec, b_spec, l_spec, l_spec, d_spec],
        out_shape=[jax.ShapeDtypeStruct((L, SW), F32),
                   jax.ShapeDtypeStruct((2, NSB, 128, 2 * SBW), F32), jax.ShapeDtypeStruct((2, NSB, 128, 2 * SBW), F32),
                   jax.ShapeDtypeStruct((2, NSB, 1, SBW), F32), jax.ShapeDtypeStruct((2, NSB, 1, SBW), F32),
                   jax.ShapeDtypeStruct((1, SW), F32)],
        scratch_shapes=[pltpu.VMEM((L, 2 * SBW), F32), pltpu.VMEM((L, 2 * SBW), BF16)],
        compiler_params=_cp(("parallel", "arbitrary", "arbitrary")),
    )(u_p, dy_p, states, bcat, ccat, lam_re, lam_im, dskip)


def _s5_params(a_re, a_im, log_step, bt_re, bt_im):
    lam = lax.complex(a_re, a_im)
    step = jnp.exp(log_step)[..., None]
    lam_bar = jnp.exp(lam * step)
    b_bar = ((lam_bar - 1.0) / lam)[..., None, :] * lax.complex(bt_re, bt_im)
    return jnp.real(lam_bar), jnp.imag(lam_bar), jnp.real(b_bar), jnp.imag(b_bar)


def _sel():
    i = jnp.arange(8)[None, :, None]
    j = jnp.arange(4)[None, None, :]
    r = jnp.arange(2)[:, None, None]
    return (i == r * 4 + j).astype(F32)


def _to_bcat(bt_re, bt_im):
    def one(bt):
        return jnp.einsum('dkrjcp,rij->dkricjp', bt.reshape(2, 4, 2, 4, GC, NP), _sel()).reshape(2, NSB, 128, SBW)
    return jnp.concatenate([one(bt_re), one(bt_im)], axis=-1)


def _from_bcat(dbcat):
    def one(dbbd):
        return jnp.einsum('dkricjp,rij->dkrjcp', dbbd.reshape(2, 4, 2, 8, GC, 4, NP), _sel()).reshape(2, NG, GC, NP)
    return one(dbcat[..., :SBW]), one(dbcat[..., SBW:])


def _to_ccat(c_re, c_im):
    def one(cc):
        return jnp.einsum('dkrjcp,rij->dkrjpic', cc.reshape(2, 4, 2, 4, GC, NP), _sel()).reshape(2, NSB, SBW, 128)
    return jnp.concatenate([one(c_re), -one(c_im)], axis=-2)


def _from_ccat(dccat):
    def one(dcbd):
        return jnp.einsum('dkrjpic,rij->dkrjcp', dcbd.reshape(2, 4, 2, 4, NP, 8, GC), _sel()).reshape(2, NG, GC, NP)
    return one(dccat[:, :, :SBW]), -one(dccat[:, :, SBW:])


def _gelu(y):
    return 0.5 * y * (1.0 + lax.erf(y * (2.0 ** -0.5)))


def _gelu_grad(y):
    return 0.5 * (1.0 + lax.erf(y * (2.0 ** -0.5))) + y * jnp.exp(-0.5 * y * y) * ((2.0 * math.pi) ** -0.5)


def _sigmoid(z):
    return 0.5 * jnp.tanh(0.5 * z) + 0.5


def _glu_fwd(y, wg):
    def body(y_ref, w_ref, o_ref, z_ref):
        ys = _gelu(y_ref[...])
        z = jnp.dot(ys.astype(BF16), w_ref[...], preferred_element_type=F32)
        z_ref[...] = z
        o_ref[...] = ys * _sigmoid(z)

    row = pl.BlockSpec((TL, SW), lambda i: (i, 0))
    return pl.pallas_call(
        body, name="glu_fwd", grid=(L // TL,),
        in_specs=[row, pl.BlockSpec((SW, SW), lambda i: (0, 0))], out_specs=[row, row],
        out_shape=[jax.ShapeDtypeStruct((L, SW), F32), jax.ShapeDtypeStruct((L, SW), F32)],
        compiler_params=_cp(("parallel",)),
    )(y, wg)


def _glu_bwd(y, z, dout, wg):
    def body(y_ref, z_ref, do_ref, w_ref, dy_ref, dw_ref):
        @pl.when(pl.program_id(0) == 0)
        def _():
            dw_ref[...] = jnp.zeros_like(dw_ref)

        yv = y_ref[...]
        ys = _gelu(yv)
        sg = _sigmoid(z_ref[...])
        dov = do_ref[...]
        dz = (dov * ys * sg * (1.0 - sg)).astype(BF16)
        dys = dov * sg + lax.dot_general(dz, w_ref[...], _NT, preferred_element_type=F32)
        dy_ref[...] = dys * _gelu_grad(yv)
        dw_ref[...] += lax.dot_general(ys.astype(BF16), dz, _TN, preferred_element_type=F32)

    row = pl.BlockSpec((TL, SW), lambda i: (i, 0))
    wsp = pl.BlockSpec((SW, SW), lambda i: (0, 0))
    return pl.pallas_call(
        body, name="glu_bwd", grid=(L // TL,),
        in_specs=[row, row, row, wsp], out_specs=[row, wsp],
        out_shape=[jax.ShapeDtypeStruct((L, SW), F32), jax.ShapeDtypeStruct((SW, SW), F32)],
        compiler_params=_cp(("arbitrary",)),
    )(y, z, dout, wg)


CT = 256
CR = 128
NCT = DFF // CT


def _shifted(ref, r):
    h = 8 * (4 // ref.dtype.itemsize)
    cur = ref[pl.ds(r, CR), :].astype(F32)
    before = ref[pl.ds(pl.multiple_of(jnp.maximum(r - h, 0), h), h), :][h - 1:h, :].astype(F32)
    after = ref[pl.ds(pl.multiple_of(jnp.minimum(r + CR, L - h), h), h), :][0:1, :].astype(F32)
    before = jnp.where(r > 0, before, 0.0)
    after = jnp.where(r + CR < L, after, 0.0)
    row = lax.broadcasted_iota(jnp.int32, cur.shape, 0)
    prev = jnp.where(row == 0, before, pltpu.roll(cur, 1, 0))
    nxt = jnp.where(row == CR - 1, after, pltpu.roll(cur, CR - 1, 0))
    return prev, cur, nxt


def _conv3(ref, r, w_ref, b_ref):
    prev, cur, nxt = _shifted(ref, r)
    return w_ref[0:1, :] * prev + w_ref[1:2, :] * cur + w_ref[2:3, :] * nxt + b_ref[...]


def _convact_fwd(up, conv_w, conv_b):
    def body(ug_ref, uv_ref, wg_ref, wv_ref, bg_ref, bv_ref, o_ref, g_ref, v_ref):
        def chunk(i, _):
            r = pl.multiple_of(i * CR, CR)
            rs = pl.ds(r, CR)
            g = _conv3(ug_ref, r, wg_ref, bg_ref)
            v = _conv3(uv_ref, r, wv_ref, bv_ref)
            o_ref[rs, :] = (g * _sigmoid(g) * v).astype(BF16)
            g_ref[rs, :] = g.astype(BF16)
            v_ref[rs, :] = v.astype(BF16)
            return 0

        lax.fori_loop(0, L // CR, chunk, 0)

    gcol = pl.BlockSpec((L, CT), lambda j: (0, j))
    vcol = pl.BlockSpec((L, CT), lambda j: (0, j + NCT))
    return pl.pallas_call(
        body, name="convact_fwd", grid=(NCT,),
        in_specs=[gcol, vcol,
                  pl.BlockSpec((3, CT), lambda j: (0, j)), pl.BlockSpec((3, CT), lambda j: (0, j + NCT)),
                  pl.BlockSpec((1, CT), lambda j: (0, j)), pl.BlockSpec((1, CT), lambda j: (0, j + NCT))],
        out_specs=[gcol, gcol, gcol], out_shape=[jax.ShapeDtypeStruct((L, DFF), BF16)] * 3,
        compiler_params=_cp(("parallel",)),
    )(up, up, conv_w, conv_w, conv_b, conv_b)


def _convact_bwd(up, gq, vq, dact, conv_w):
    def body(ug_ref, uv_ref, g_ref, v_ref, da_ref, wg_ref, wv_ref, du_ref, dw_ref, db_ref, dgs, dvs, dbv):
        half = pl.program_id(1)

        def transpose_conv(src, u_ref, w_ref):
            dw_ref[...] = jnp.zeros_like(dw_ref)

            def chunk(i, _):
                r = pl.multiple_of(i * CR, CR)
                rs = pl.ds(r, CR)
                prev, cur, nxt = _shifted(src, r)
                du_ref[rs, :] = (w_ref[0:1, :] * nxt + w_ref[1:2, :] * cur + w_ref[2:3, :] * prev).astype(BF16)
                uv = u_ref[rs, :].astype(F32)
                for k, d in enumerate((nxt, cur, prev)):
                    dw_ref[k:k + 1, :] += jnp.sum(d * uv, axis=0, keepdims=True)
                return 0

            lax.fori_loop(0, L // CR, chunk, 0)

        @pl.when(half == 0)
        def _():
            db_ref[...] = jnp.zeros_like(db_ref)
            dbv[...] = jnp.zeros_like(dbv)

            def chunk1(i, _):
                rs = pl.ds(pl.multiple_of(i * CR, CR), CR)
                g = g_ref[rs, :].astype(F32)
                v = v_ref[rs, :].astype(F32)
                sg = _sigmoid(g)
                da = da_ref[rs, :].astype(F32)
                dv = da * g * sg
                dg = da * v * sg * (1.0 + g * (1.0 - sg))
                dgs[rs, :] = dg
                dvs[rs, :] = dv
                db_ref[...] += jnp.sum(dg, axis=0, keepdims=True)
                dbv[0:1, :] += jnp.sum(dv, axis=0, keepdims=True)
                return 0

            lax.fori_loop(0, L // CR, chunk1, 0)
            transpose_conv(dgs, ug_ref, wg_ref)

        @pl.when(half == 1)
        def _():
            db_ref[...] = dbv[0:1, :]
            transpose_conv(dvs, uv_ref, wv_ref)

    def col(rows, off):
        return pl.BlockSpec((rows, CT), lambda j, h: (0, j + off))

    def out(rows):
        return pl.BlockSpec((rows, CT), lambda j, h: (0, j + h * NCT))

    return pl.pallas_call(
        body, name="convact_bwd", grid=(NCT, 2),
        in_specs=[col(L, 0), col(L, NCT), col(L, 0), col(L, 0), col(L, 0), col(3, 0), col(3, NCT)],
        out_specs=[out(L), out(3), out(1)],
        out_shape=[jax.ShapeDtypeStruct((L, 2 * DFF), BF16), jax.ShapeDtypeStruct((3, 2 * DFF), F32),
                   jax.ShapeDtypeStruct((1, 2 * DFF), F32)],
        scratch_shapes=[pltpu.VMEM((L, CT), F32), pltpu.VMEM((L, CT), F32), pltpu.VMEM((8, CT), F32)],
        compiler_params=_cp(("parallel", "arbitrary")),
    )(up, up, gq, vq, dact, conv_w, conv_w)


def _local_step(x, tgt, w_in_t, w_glu, w_out, conv_w, p, attend, reduce_early):
    tabs = _rope_tables()
    lam_re, lam_im, bb_re, bb_im = _s5_params(p["a_re"], p["a_im"], p["log_step"], p["bt_re"], p["bt_im"])
    bcat = _to_bcat(bb_re, bb_im).astype(BF16)
    ccat = _to_ccat(p["c_re"], p["c_im"]).astype(BF16)
    lam_re4, lam_im4 = lam_re.reshape(2, NSB, 1, SBW), lam_im.reshape(2, NSB, 1, SBW)
    dskip = p["d_skip"].reshape(1, SW)
    g_mix, g_ffn, g_fin = p["norm_mix_g"].reshape(1, D), p["norm_ffn_g"].reshape(1, D), p["norm_final_g"].reshape(1, D)
    g_attn, g_ssm = p["norm_attn_g"].reshape(1, AW), p["norm_ssm_g"].reshape(1, SW)
    sink = p["sink"].reshape(NQ)
    conv_b = p["conv_b"].reshape(1, 2 * DFF)

    rows, gain = jax.ShapeDtypeStruct((L, D), F32), jax.ShapeDtypeStruct((1, D), F32)
    h1, qkv, u = _in_proj(x, g_mix, w_in_t, tabs)
    attn, lse, w_up_t, w_down = attend(qkv, sink)
    u_p = _perm(u)
    y_p, states = _s5_fwd(u_p, bcat, ccat, lam_re4, lam_im4, dskip)
    ysg_p, z_p = _glu_fwd(y_p, w_glu)
    ysg = _unperm(ysg_p)
    mixed, x1, h2 = _out_proj(attn, ysg, g_attn, g_ssm, w_out, x, g_ffn)
    up = _mm(h2, w_up_t, tb=True, name="ffn_up", tn=1408, out_dtype=BF16)
    act, gq, vq = _convact_fwd(up, conv_w, conv_b)
    loss, dx2, dg_fin = _mm(act, w_down, add=x1, name="ffn_down", tk=1408,
                            post=(_final_post, [tgt, g_fin], [jax.ShapeDtypeStruct((1, 1), F32), rows, gain]))

    dact = _mm(dx2, w_down, tb=True, name="ffn_down_dx", tn=1408, out_dtype=BF16)
    dw_down = _mm(act, dx2, ta=True, name="ffn_down_dw", tm=1408)
    ride = reduce_early("w_down", dw_down)
    dup, dconv_w, dconv_b = _convact_bwd(up, gq, vq, dact, conv_w)
    res = _mm(dup, h2, ta=True, name="ffn_up_dw", tm=1408, ride=ride)
    dw_up_t, got_down = res if ride else (res, [])
    ride = reduce_early("w_up_t", dw_up_t)
    res = _mm(dup, w_up_t, name="ffn_up_dx", tk=1408, ride=ride,
              post=(_rms_bwd_post, [x1, dx2, g_ffn], [rows, gain]))
    (dx1, dg_ffn), got_up = res if ride else (res, [])
    dattn, dysg, dg_attn, dg_ssm = _out_proj_dx(dx1, w_out, attn, ysg, g_attn, g_ssm)
    dw_out = _mm(mixed, dx1, ta=True, name="out_proj_dw")
    dy_p, dw_glu = _glu_bwd(y_p, z_p, _perm(dysg), w_glu)
    du_p, dbcat, dccat, dlam_re, dlam_im, dd = _s5_bwd(u_p, dy_p, states, bcat, ccat, lam_re4, lam_im4, dskip)
    dbb_re, dbb_im = _from_bcat(dbcat)
    dc_re, dc_im = _from_ccat(_swap(dccat))
    dq, dk, dv, dsink = _attn_bwd(qkv, sink, attn, lse, dattn)
    dproj = _rope_bwd(dq, dk, dv, _unperm(du_p), tabs)
    dw_in_t = _mm(dproj, h1, ta=True, name="in_proj_dw", tm=1280)
    grad_x, dg_mix = _in_proj_dx(dproj, w_in_t, x, g_mix, dx1)

    big = dict(w_in_t=dw_in_t, w_glu=dw_glu, w_out=dw_out)
    early = dict(w_down=got_down, w_up_t=got_up)
    small = dict(norm_mix_g=dg_mix, norm_attn_g=dg_attn, norm_ssm_g=dg_ssm, norm_ffn_g=dg_ffn, norm_final_g=dg_fin,
                 sink=dsink[:, 0], conv_b=dconv_b, d_skip=dd, conv_w=dconv_w,
                 lam_re=dlam_re.reshape(2, NG, NP), lam_im=dlam_im.reshape(2, NG, NP),
                 bb_re=dbb_re, bb_im=dbb_im, c_re=dc_re, c_im=dc_im, loss=loss.reshape(1))
    return grad_x, big, small, early


ANY = pl.BlockSpec(memory_space=pl.ANY)


def _coords():
    return lax.axis_index("x"), lax.axis_index("y"), lax.axis_index("c")


def _flip(v, b):
    return v + b - 2 * v * b if b else v


def _all_gather(shards, name):
    n = len(shards)

    def body(*refs):
        _gather_start(refs[:n], refs[n:2 * n], *refs[2 * n:])
        _gather_finish(refs[:n], refs[n:2 * n], *refs[2 * n:])

    return pl.pallas_call(
        body, name=name,
        in_specs=[ANY] * n, out_specs=[ANY] * n,
        out_shape=_gather_shapes(shards), scratch_shapes=_gather_sems(n),
    )(*shards)


def _gather_shapes(shards):
    return [jax.ShapeDtypeStruct((NDEV * s.shape[0], s.shape[1]), s.dtype) for s in shards]


def _gather_sems(n):
    return [pltpu.SemaphoreType.DMA((7 * n,)), pltpu.SemaphoreType.DMA((7 * n,)), pltpu.SemaphoreType.DMA((n,))]


def _gather_copies(ins, outs, send_sems, recv_sems, local_sems, a):
    x, y, c = _coords()
    me, sibling = (x, y, c), (x, y, 1 - c)
    chips = [(1 - x, y), (x, 1 - y), (1 - x, 1 - y)]
    r = ins[a].shape[0]

    def rows(px, py, pc):
        return outs[a].at[pl.ds(pl.multiple_of((4 * px + 2 * py + pc) * r, 8), r), :]

    def copy(k, block, to, src=None):
        return pltpu.make_async_remote_copy(
            src_ref=rows(*block) if src is None else src, dst_ref=rows(*block),
            send_sem=send_sems.at[a * 7 + k], recv_sem=recv_sems.at[a * 7 + k],
            device_id=to, device_id_type=pl.DeviceIdType.MESH)

    mine = pltpu.make_async_copy(ins[a], rows(*me), local_sems.at[a])
    first = [copy(0, me, sibling, src=ins[a])]
    first += [copy(1 + j, me, (*chip, c), src=ins[a]) for j, chip in enumerate(chips)]
    passed = [copy(4 + j, (*chip, c), sibling) for j, chip in enumerate(chips)]
    arrivals = [copy(1 + j, (*chip, c), me) for j, chip in enumerate(chips)]
    from_sibling = [copy(0, sibling, me)] + [copy(4 + j, (*chip, 1 - c), me) for j, chip in enumerate(chips)]
    return mine, first, passed, arrivals, from_sibling


def _gather_start(ins, outs, send_sems, recv_sems, local_sems):
    for a in range(len(ins)):
        mine, first, _, _, _ = _gather_copies(ins, outs, send_sems, recv_sems, local_sems, a)
        mine.start()
        for cp in first:
            cp.start()


def _gather_finish(ins, outs, send_sems, recv_sems, local_sems):
    n = len(ins)
    parts = [_gather_copies(ins, outs, send_sems, recv_sems, local_sems, a) for a in range(n)]
    for mine, first, passed, arrivals, from_sibling in parts:
        for arrived, onward in zip(arrivals, passed):
            arrived.wait_recv()
            onward.start()
    for mine, first, passed, arrivals, from_sibling in parts:
        for cp in from_sibling:
            cp.wait_recv()
        for cp in first + passed:
            cp.wait_send()
        mine.wait()


NCHIP = 4
CHIP_FLIPS = ((1, 0), (0, 1), (1, 1))


def _planned_copies(ins, outs, send_sems, recv_sems, plan):
    return [pltpu.make_async_remote_copy(
        src_ref=ins[a].at[src], dst_ref=outs[a].at[dst], send_sem=send_sems.at[k], recv_sem=recv_sems.at[k],
        device_id=to, device_id_type=pl.DeviceIdType.MESH) for k, (a, src, dst, to) in enumerate(plan)]


def _start_all(copies):
    for cp in copies:
        cp.start()


def _wait_all(copies):
    for cp in copies:
        cp.wait_recv()
    for cp in copies:
        cp.wait_send()


def _exchange_cores(parts, name):
    n = len(parts)

    def body(*refs):
        x, y, c = _coords()
        plan = [(a, 2 * q + 1 - c, q, (x, y, 1 - c)) for a in range(n) for q in range(NCHIP)]
        copies = _planned_copies(refs[:n], refs[n:2 * n], *refs[2 * n:], plan)
        _start_all(copies)
        _wait_all(copies)

    return pl.pallas_call(
        body, name=name, in_specs=[ANY] * n, out_specs=[ANY] * n,
        out_shape=[jax.ShapeDtypeStruct((NCHIP,) + s.shape[1:], s.dtype) for s in parts],
        scratch_shapes=[pltpu.SemaphoreType.DMA((NCHIP * n,)), pltpu.SemaphoreType.DMA((NCHIP * n,))],
    )(*parts)


def _chips_copies(ins, outs, send_sems, recv_sems):
    x, y, c = _coords()
    plan = []
    for a in range(len(ins)):
        for j, (fx, fy) in enumerate(CHIP_FLIPS):
            px, py = _flip(x, fx), _flip(y, fy)
            plan.append((a, 2 * px + py, j, (px, py, c)))
    return _planned_copies(ins, outs, send_sems, recv_sems, plan)


def _chips_shapes(parts):
    return [jax.ShapeDtypeStruct((3,) + s.shape[1:], s.dtype) for s in parts]


def _chips_sems(n):
    return [pltpu.SemaphoreType.DMA((3 * n,)), pltpu.SemaphoreType.DMA((3 * n,))]


def _exchange_chips(parts, name):
    n = len(parts)

    def body(*refs):
        copies = _chips_copies(refs[:n], refs[n:2 * n], *refs[2 * n:])
        _start_all(copies)
        _wait_all(copies)

    return pl.pallas_call(
        body, name=name, in_specs=[ANY] * n, out_specs=[ANY] * n,
        out_shape=_chips_shapes(parts), scratch_shapes=_chips_sems(n),
    )(*parts)


def _pair_sum(where, part, recv, wire_dtype, name):
    _, r, c = part.shape
    tr = _pick(r, 256, 16)

    def body(w_ref, p_ref, r_ref, pb_ref, own_ref):
        s = p_ref[...] + r_ref[...]
        pb_ref[...] = s.astype(wire_dtype)

        @pl.when(pl.program_id(1) == w_ref[1])
        def _():
            own_ref[...] = s

    return pl.pallas_call(
        body, name=name,
        grid_spec=pltpu.PrefetchScalarGridSpec(
            num_scalar_prefetch=1, grid=(r // tr, NCHIP),
            in_specs=[pl.BlockSpec((None, tr, c), lambda i, q, w: (2 * q + w[0], i, 0)),
                      pl.BlockSpec((None, tr, c), lambda i, q, w: (q, i, 0))],
            out_specs=[pl.BlockSpec((None, tr, c), lambda i, q, w: (q, i, 0)),
                       pl.BlockSpec((tr, c), lambda i, q, w: (i, 0))]),
        out_shape=[jax.ShapeDtypeStruct((NCHIP, r, c), wire_dtype), jax.ShapeDtypeStruct((r, c), F32)],
        compiler_params=_cp(("parallel", "arbitrary")),
    )(where, part, recv)


def _chip_sum(own, recv, name):
    r, c = own.shape
    tr = _pick(r, 256, 16)

    def body(o_ref, r_ref, out_ref):
        acc = o_ref[...]
        for j in range(3):
            acc = acc + r_ref[j].astype(F32)
        out_ref[...] = acc

    return pl.pallas_call(
        body, name=name, grid=(r // tr,),
        in_specs=[pl.BlockSpec((tr, c), lambda i: (i, 0)), pl.BlockSpec((3, tr, c), lambda i: (0, i, 0))],
        out_specs=pl.BlockSpec((tr, c), lambda i: (i, 0)),
        out_shape=jax.ShapeDtypeStruct((r, c), F32),
        compiler_params=_cp(("parallel",)),
    )(own, recv)


def _adamw(w, own, recv, m, v, name):
    r, c = w.shape
    tr = _pick(r, 256, 16)

    def body(w_ref, o_ref, r_ref, m_ref, v_ref, g_ref, d_ref, nm_ref, nv_ref):
        acc = o_ref[...]
        for j in range(3):
            acc = acc + r_ref[j].astype(F32)
        g_ref[...] = acc
        _adamw_refs(w_ref, g_ref, m_ref, v_ref, d_ref, nm_ref, nv_ref)

    blk = pl.BlockSpec((tr, c), lambda i: (i, 0))
    return pl.pallas_call(
        body, name=name, grid=(r // tr,),
        in_specs=[blk, blk, pl.BlockSpec((3, tr, c), lambda i: (0, i, 0)), blk, blk], out_specs=[blk] * 4,
        out_shape=[jax.ShapeDtypeStruct((r, c), F32)] * 4,
        compiler_params=_cp(("parallel",)),
    )(w, own, recv, m, v)


def _adamw_refs(w_ref, g_ref, m_ref, v_ref, d_ref, nm_ref, nv_ref):
    gv = g_ref[...]
    nm = B1 * m_ref[...] + (1.0 - B1) * gv
    nv = B2 * v_ref[...] + (1.0 - B2) * (gv * gv)
    nm_ref[...] = nm
    nv_ref[...] = nv
    d_ref[...] = -LR * ((nm / C1) / (jnp.sqrt(nv / C2) + AEPS) + WD * w_ref[...])


def _adamw_small(ws, gs, ms, vs, name):
    n = len(ws)

    def body(*refs):
        groups = [refs[i * n:(i + 1) * n] for i in range(7)]
        for per_param in zip(*groups):
            _adamw_refs(*per_param)

    vm = pl.BlockSpec(memory_space=pltpu.VMEM)
    outs = pl.pallas_call(
        body, name=name, in_specs=[vm] * (4 * n), out_specs=[vm] * (3 * n),
        out_shape=[jax.ShapeDtypeStruct(a.shape, F32) for a in ws] * 3,
    )(*ws, *gs, *ms, *vs)
    return outs[:n], outs[n:2 * n], outs[2 * n:]


def _swap(a):
    return jnp.swapaxes(a, -1, -2)


VIEWS = {
    "w_in": (lambda a: a[0].T, lambda u: u.T[None]),
    "w_up": (lambda a: a[0].T, lambda u: u.T[None]),
    "w_glu": (lambda a: a[0], lambda u: u[None]),
    "w_out": (lambda a: a[0], lambda u: u[None]),
    "w_down": (lambda a: a[0], lambda u: u[None]),
    "conv_w": (lambda a: a[0], lambda u: u[None]),
    "norm_mix_g": (lambda a: a, lambda u: u),
    "norm_attn_g": (lambda a: a, lambda u: u),
    "norm_ssm_g": (lambda a: a, lambda u: u),
    "norm_ffn_g": (lambda a: a, lambda u: u),
    "norm_final_g": (lambda a: a[None], lambda u: u[0]),
    "conv_b": (lambda a: a, lambda u: u),
    "sink": (lambda a: a, lambda u: u),
    "a_re": (lambda a: a.reshape(2 * NG, NP), lambda u: u.reshape(1, 2, NG, NP)),
    "a_im": (lambda a: a.reshape(2 * NG, NP), lambda u: u.reshape(1, 2, NG, NP)),
    "log_step": (lambda a: a[0], lambda u: u[None]),
    "b_re": (lambda a: _swap(a[0]).reshape(2 * NG * GC, NP), lambda u: _swap(u.reshape(2, NG, GC, NP))[None]),
    "b_im": (lambda a: _swap(a[0]).reshape(2 * NG * GC, NP), lambda u: _swap(u.reshape(2, NG, GC, NP))[None]),
    "c_re": (lambda a: a.reshape(2 * NG * GC, NP), lambda u: u.reshape(1, 2, NG, GC, NP)),
    "c_im": (lambda a: a.reshape(2 * NG * GC, NP), lambda u: u.reshape(1, 2, NG, GC, NP)),
    "d_skip": (lambda a: a[0].T, lambda u: u.T[None]),
}
BIG = ["w_in", "w_glu", "w_out", "w_up", "w_down"]
PACK_W = 1024


def _pack(arrs, rows):
    flat = jnp.concatenate([a.reshape(-1).astype(F32) for a in arrs])
    return jnp.pad(flat, (0, rows * PACK_W - flat.shape[0])).reshape(rows, PACK_W)


def _unpack(packed, shapes):
    flat = packed.reshape(-1)
    out, off = [], 0
    for s in shapes:
        size = math.prod(s)
        out.append(flat[off:off + size].reshape(s))
        off += size
    return out


def kernel(x, norm_mix_g, w_in, a_re, a_im, log_step, b_re, b_im, c_re, c_im, d_skip, w_glu, sink, norm_attn_g, norm_ssm_g, w_out, norm_ffn_g, w_up, conv_w, conv_b, w_down, norm_final_g, loss_target, m_norm_mix_g, m_w_in, m_a_re, m_a_im, m_log_step, m_b_re, m_b_im, m_c_re, m_c_im, m_d_skip, m_w_glu, m_sink, m_norm_attn_g, m_norm_ssm_g, m_w_out, m_norm_ffn_g, m_w_up, m_conv_w, m_conv_b, m_w_down, m_norm_final_g, v_norm_mix_g, v_w_in, v_a_re, v_a_im, v_log_step, v_b_re, v_b_im, v_c_re, v_c_im, v_d_skip, v_w_glu, v_sink, v_norm_attn_g, v_norm_ssm_g, v_w_out, v_norm_ffn_g, v_w_up, v_conv_w, v_conv_b, v_w_down, v_norm_final_g):
    args = dict(locals())
    names = ["norm_mix_g", "w_in", "a_re", "a_im", "log_step", "b_re", "b_im", "c_re", "c_im", "d_skip", "w_glu",
             "sink", "norm_attn_g", "norm_ssm_g", "w_out", "norm_ffn_g", "w_up", "conv_w", "conv_b", "w_down",
             "norm_final_g"]
    w = {k: args[k] for k in names}
    m = {k: args["m_" + k] for k in names}
    v = {k: args["v_" + k] for k in names}

    shards = [w_in[0].T.astype(BF16), w_glu[0].astype(BF16), w_out[0].astype(BF16),
              jnp.pad(conv_w[0], ((0, 5), (0, 0)))]
    w_in_t, w_glu_f, w_out_f, conv_w_g = _all_gather(shards, "gather_weights")
    conv_w_f = conv_w_g.reshape(NDEV, 8, 2 * DFF // NDEV)[:, :3].transpose(1, 0, 2).reshape(3, 2 * DFF)
    ffn_shards = [w_up[0].T.astype(BF16), w_down[0].astype(BF16)]

    ax, ay, ac = _coords()
    me = 4 * ax + 2 * ay + ac
    where = jnp.stack([ac, 2 * ax + ay]).astype(jnp.int32)
    own = {}

    def to_chip_sums(named_parts, tag):
        ks = [k for k, _ in named_parts]
        parts = [part for _, part in named_parts]
        from_core = _exchange_cores(parts, "exchange_cores_" + tag)
        wire = []
        for k, part, got in zip(ks, parts, from_core):
            per_chip, own[k] = _pair_sum(where, part, got, F32 if k == "small" else BF16, "pair_sum_" + k)
            wire.append(per_chip)
        return wire

    def split8(g):
        return g.reshape(NDEV, g.shape[0] // NDEV, g.shape[1])

    def attend(qkv, sink_):
        attn, lse, (w_up_t, w_down_f) = _attn_fwd(qkv, sink_, gather=ffn_shards)
        return attn, lse, w_up_t, w_down_f

    def reduce_early(k, dw):
        return to_chip_sums([(k, split8(dw))], k)

    p = {k: w[k][0] for k in ("norm_mix_g", "a_re", "a_im", "log_step", "c_re", "c_im", "d_skip", "sink",
                              "norm_attn_g", "norm_ssm_g", "norm_ffn_g", "conv_b")}
    p["norm_final_g"] = norm_final_g
    p["bt_re"], p["bt_im"] = _swap(b_re[0]), _swap(b_im[0])
    grad_x, big, small, early = _local_step(x[0], loss_target[0], w_in_t, w_glu_f, w_out_f, conv_w_f, p,
                                            attend, reduce_early)

    small_names = list(small.keys())
    small_shapes = [small[k].shape for k in small_names]
    n_small = sum(math.prod(s) for s in small_shapes)
    rows_dev = -(-n_small // (PACK_W * NDEV * 16)) * 16
    spack = _pack([small[k] for k in small_names], rows_dev * NDEV)
    late_names = ["w_in_t", "w_glu", "w_out", "small"]
    late_parts = [split8(big[k]) for k in late_names[:-1]] + [spack.reshape(NDEV, rows_dev, PACK_W)]
    from_chips = _exchange_chips(to_chip_sums(list(zip(late_names, late_parts)), "late"), "exchange_chips")
    got = dict(zip(late_names, from_chips), w_down=early["w_down"][0], w_up_t=early["w_up_t"][0])
    (small_full,) = _all_gather([_chip_sum(own["small"], got["small"], "chip_sum_small")], "gather_small")
    sm = dict(zip(small_names, _unpack(small_full, small_shapes)))

    _, s5_vjp = jax.vjp(_s5_params, a_re[0], a_im[0], log_step[0], p["bt_re"], p["bt_im"])
    da_re, da_im, dlog_step, dbt_re, dbt_im = s5_vjp((sm["lam_re"], sm["lam_im"], sm["bb_re"], sm["bb_im"]))
    gview = {
        "norm_mix_g": sm["norm_mix_g"], "norm_attn_g": sm["norm_attn_g"], "norm_ssm_g": sm["norm_ssm_g"],
        "norm_ffn_g": sm["norm_ffn_g"], "norm_final_g": sm["norm_final_g"], "conv_b": sm["conv_b"],
        "sink": sm["sink"][None], "a_re": da_re.reshape(2 * NG, NP), "a_im": da_im.reshape(2 * NG, NP),
        "log_step": dlog_step, "b_re": dbt_re.reshape(2 * NG * GC, NP), "b_im": dbt_im.reshape(2 * NG * GC, NP),
        "c_re": sm["c_re"].reshape(2 * NG * GC, NP), "c_im": sm["c_im"].reshape(2 * NG * GC, NP),
        "d_skip": sm["d_skip"].reshape(NG, GC).T,
        "conv_w": lax.dynamic_slice_in_dim(sm["conv_w"], me * (2 * DFF // NDEV), 2 * DFF // NDEV, axis=1),
    }

    dview, mview, vview = {}, {}, {}
    for k, kg in zip(BIG, ("w_in_t", "w_glu", "w_out", "w_up_t", "w_down")):
        to = VIEWS[k][0]
        gview[k], dview[k], mview[k], vview[k] = _adamw(to(w[k]), own[kg], got[kg], to(m[k]), to(v[k]), "adamw_" + k)
    rest = [k for k in names if k not in BIG]
    outs = _adamw_small([VIEWS[k][0](w[k]) for k in rest], [gview[k] for k in rest],
                        [VIEWS[k][0](m[k]) for k in rest], [VIEWS[k][0](v[k]) for k in rest], "adamw_small")
    for dst, vals in zip((dview, mview, vview), outs):
        dst.update(dict(zip(rest, vals)))

    def back(views):
        return [VIEWS[k][1](views[k]) for k in names]

    return (sm["loss"][0], grad_x[None], *back(gview), *back(dview), *back(mview), *back(vview))
```

```python
import functools
import math

import jax
import jax.numpy as jnp
from jax import lax
from jax.experimental import pallas as pl
from jax.experimental.pallas import tpu as pltpu

F32 = jnp.float32
BF16 = jnp.bfloat16

L = 4096
D = 1024
NQ, NKV, HD = 8, 2, 64
AW = NQ * HD
KVW = NKV * HD
SW = 512
NG, GC, NP = 32, 16, 64
INW = AW + 2 * KVW + SW
DFF = 2816
BLK = 128
WIN = 3 * BLK
EPS = 1e-6
ROPE_THETA = 500000.0
NSEG = 32
TSEG = L // NSEG
SBW = 256
NSB = NG * NP // SBW
NDEV = 8
MESH_AXES = ("x", "y", "c")

LR, B1, B2, AEPS, WD, STEP = 0.001, 0.9, 0.999, 1e-08, 0.01, 10
C1 = 1.0 - B1 ** STEP
C2 = 1.0 - B2 ** STEP

VMEM_LIMIT = 56 * 1024 * 1024


def _pick(n, target, mult):
    best = None
    for t in range(mult, min(n, target) + 1, mult):
        if n % t == 0:
            best = t
    return best if best is not None else n


def _cp(sem):
    return pltpu.CompilerParams(dimension_semantics=sem, vmem_limit_bytes=VMEM_LIMIT)


def _mm(a, b, *, ta=False, tb=False, out_dtype=F32, add=None, ride=(), post=None, name, tm=1024, tn=1024, tk=1024):
    m, k = (a.shape[1], a.shape[0]) if ta else a.shape
    n = b.shape[0] if tb else b.shape[1]
    assert k == (b.shape[1] if tb else b.shape[0])
    tm, tn, tk = _pick(m, tm, 128), _pick(n, tn, 128), _pick(k, tk, 128)
    grid = (m // tm, n // tn, k // tk)
    nk = grid[2]
    dn = (((0 if ta else 1,), (1 if tb else 0,)), ((), ()))
    n_in = 2 + (add is not None)
    kind, riding = ride if ride else (None, ())
    nr = len(riding)
    post_fn, post_ins, post_outs = post if post is not None else (None, (), ())
    n_pi = len(post_ins)
    n_out = len(post_outs) if post is not None else 1
    assert post is None or grid[1] == 1

    def body(*refs):
        a_ref, b_ref = refs[0], refs[1]
        pin = refs[n_in:n_in + n_pi]
        base = n_in + n_pi + nr
        o_refs = refs[base:base + n_out]
        acc_ref = refs[base + n_out + nr]
        step = [pl.program_id(d) for d in range(3)]
        kk = step[2]
        if nr:
            riders = (refs[n_in + n_pi:base], refs[base + n_out:base + n_out + nr], *refs[base + n_out + nr + 1:])

            @pl.when((step[0] == 0) & (step[1] == 0) & (kk == 0))
            def _():
                _start_all(_exchange_copies(kind, *riders))

        prod = lax.dot_general(a_ref[...].astype(BF16), b_ref[...].astype(BF16), dn, preferred_element_type=F32)

        def finish(r):
            if add is not None:
                r = r + refs[2][...]
            if post_fn is None:
                o_refs[0][...] = r.astype(out_dtype)
            else:
                post_fn(r, step[0], pin, o_refs)

        if nk == 1:
            finish(prod)
        else:
            @pl.when(kk == 0)
            def _():
                acc_ref[...] = prod

            @pl.when((kk > 0) & (kk < nk - 1))
            def _():
                acc_ref[...] += prod

            @pl.when(kk == nk - 1)
            def _():
                finish(acc_ref[...] + prod)

        if nr:
            @pl.when((step[0] == grid[0] - 1) & (step[1] == grid[1] - 1) & (kk == nk - 1))
            def _():
                _wait_all(_exchange_copies(kind, *riders))

    a_spec = pl.BlockSpec((tk, tm), lambda i, j, kk: (kk, i)) if ta else pl.BlockSpec((tm, tk), lambda i, j, kk: (i, kk))
    b_spec = pl.BlockSpec((tn, tk), lambda i, j, kk: (j, kk)) if tb else pl.BlockSpec((tk, tn), lambda i, j, kk: (kk, j))
    def row_spec(shape):
        return pl.BlockSpec((tm if shape[0] == m else shape[0], shape[1]),
                            (lambda i, j, kk: (i, 0)) if shape[0] == m else (lambda i, j, kk: (0, 0)))

    in_specs = [a_spec, b_spec]
    args = [a, b]
    if add is not None:
        in_specs.append(pl.BlockSpec((tm, tn), lambda i, j, kk: (i, j)))
        args.append(add)
    if post is None:
        main_specs = [pl.BlockSpec((tm, tn), lambda i, j, kk: (i, j))]
        main_shapes = [jax.ShapeDtypeStruct((m, n), out_dtype)]
    else:
        main_specs = [row_spec(s.shape) for s in post_outs]
        main_shapes = list(post_outs)
    outs = pl.pallas_call(
        body, name=name, grid=grid,
        in_specs=in_specs + [row_spec(p.shape) for p in post_ins] + [ANY] * nr,
        out_specs=main_specs + [ANY] * nr,
        out_shape=main_shapes + (_exchange_shapes(kind, riding) if nr else []),
        scratch_shapes=[pltpu.VMEM((tm, tn) if nk > 1 else (8, 128), F32)] + (_exchange_sems(kind, nr) if nr else []),
        compiler_params=_cp(("arbitrary",) * 3 if (nr or post is not None) else ("parallel", "parallel", "arbitrary")),
    )(*args, *post_ins, *riding)
    main = outs[0] if post is None else list(outs[:n_out])
    return (main, list(outs[n_out:])) if nr else main


TL = 512


def _rms(xv, gv):
    return xv * lax.rsqrt(jnp.mean(xv * xv, axis=-1, keepdims=True) + EPS) * gv


def _rows(width):
    return pl.BlockSpec((TL, width), lambda i: (i, 0))


def _whole(shape):
    return pl.BlockSpec(shape, lambda i: (0,) * len(shape))


def _in_proj(x, g, w_in_t, tabs):
    qkw = AW + 2 * KVW

    def body(x_ref, g_ref, w_ref, c_ref, sa_ref, sb_ref, h_ref, qkv_ref, u_ref):
        h = _rms(x_ref[...], g_ref[...]).astype(BF16)
        h_ref[...] = h
        proj = lax.dot_general(h, w_ref[...], _NT, preferred_element_type=F32)
        for j in range(qkw // 128):
            cols = slice(j * 128, (j + 1) * 128)
            xv = proj[:, cols]
            if j < (AW + KVW) // 128:
                xv = _rope(xv, c_ref[...], sa_ref[...], sb_ref[...], 1.0)
            qkv_ref[:, cols] = xv.astype(BF16)
        u_ref[...] = proj[:, qkw:]

    return pl.pallas_call(
        body, name="in_proj", grid=(L // TL,),
        in_specs=[_rows(D), _whole((1, D)), _whole((INW, D)), _rows(128), _rows(128), _rows(128)],
        out_specs=[_rows(D), _rows(qkw), _rows(SW)],
        out_shape=[jax.ShapeDtypeStruct((L, D), BF16), jax.ShapeDtypeStruct((L, qkw), BF16),
                   jax.ShapeDtypeStruct((L, SW), F32)],
        compiler_params=_cp(("parallel",)),
    )(x, g, w_in_t, *tabs)


def _in_proj_dx(dproj, w_in_t, x, g, dres):
    def body(dp_ref, w_ref, x_ref, g_ref, dres_ref, dx_ref, dg_ref):
        dh = jnp.dot(dp_ref[...], w_ref[...], preferred_element_type=F32)
        dx, dg = _rms_bwd_tile(x_ref[...], g_ref[...], dh)
        dx_ref[...] = dx + dres_ref[...]

        @pl.when(pl.program_id(0) == 0)
        def _():
            dg_ref[...] = jnp.zeros_like(dg_ref)

        dg_ref[...] += dg

    return pl.pallas_call(
        body, name="in_proj_dx", grid=(L // TL,),
        in_specs=[_rows(INW), _whole((INW, D)), _rows(D), _whole((1, D)), _rows(D)],
        out_specs=[_rows(D), _whole((1, D))],
        out_shape=[jax.ShapeDtypeStruct((L, D), F32), jax.ShapeDtypeStruct((1, D), F32)],
        compiler_params=_cp(("arbitrary",)),
    )(dproj, w_in_t, x, g, dres)


def _rms_bwd_tile(xv, gv, dh):
    r = lax.rsqrt(jnp.mean(xv * xv, axis=-1, keepdims=True) + EPS)
    a = dh * gv
    dx = r * a - xv * (r * r * r) * jnp.mean(a * xv, axis=-1, keepdims=True)
    dg = jnp.sum(dh * xv * r, axis=0, keepdims=True)
    return dx, dg


def _rms_bwd_post(dh, i, ins, outs):
    x_ref, dres_ref, g_ref = ins
    dx_ref, dxb_ref, dg_ref = outs
    dx, dg = _rms_bwd_tile(x_ref[...], g_ref[...], dh)
    dx = dx + dres_ref[...]
    dx_ref[...] = dx
    dxb_ref[...] = dx.astype(BF16)

    @pl.when(i == 0)
    def _():
        dg_ref[...] = jnp.zeros_like(dg_ref)

    dg_ref[...] += dg


def _final_post(xv, i, ins, outs):
    t_ref, g_ref = ins
    loss_ref, dx_ref, dxb_ref, dg_ref = outs

    @pl.when(i == 0)
    def _():
        loss_ref[...] = jnp.zeros_like(loss_ref)
        dg_ref[...] = jnp.zeros_like(dg_ref)

    gv = g_ref[...]
    r = lax.rsqrt(jnp.mean(xv * xv, axis=-1, keepdims=True) + EPS)
    e = xv * r * gv - t_ref[...]
    loss_ref[...] += 0.5 * jnp.sum(jnp.mean(e * e, axis=-1, keepdims=True), axis=0, keepdims=True)
    dy = e * (1.0 / D)
    a = dy * gv
    dx = r * a - xv * (r * r * r) * jnp.mean(a * xv, axis=-1, keepdims=True)
    dx_ref[...] = dx
    dxb_ref[...] = dx.astype(BF16)
    dg_ref[...] += jnp.sum(dy * xv * r, axis=0, keepdims=True)


def _out_proj(attn, ysg, ga, gs, w_out, x, gf):
    def body(a_ref, s_ref, ga_ref, gs_ref, w_ref, x_ref, gf_ref, m_ref, x1_ref, h2_ref):
        m_ref[:, 0:AW] = _rms(a_ref[...], ga_ref[...]).astype(BF16)
        m_ref[:, AW:AW + SW] = _rms(s_ref[...], gs_ref[...]).astype(BF16)
        x1 = jnp.dot(m_ref[...], w_ref[...], preferred_element_type=F32) + x_ref[...]
        x1_ref[...] = x1
        h2_ref[...] = _rms(x1, gf_ref[...]).astype(BF16)

    return pl.pallas_call(
        body, name="out_proj", grid=(L // TL,),
        in_specs=[_rows(AW), _rows(SW), _whole((1, AW)), _whole((1, SW)), _whole((D, D)), _rows(D), _whole((1, D))],
        out_specs=[_rows(D), _rows(D), _rows(D)],
        out_shape=[jax.ShapeDtypeStruct((L, D), BF16), jax.ShapeDtypeStruct((L, D), F32),
                   jax.ShapeDtypeStruct((L, D), BF16)],
        compiler_params=_cp(("parallel",)),
    )(attn, ysg, ga, gs, w_out, x, gf)


def _out_proj_dx(dx1, w_out, attn, ysg, ga, gs):
    def body(dx_ref, w_ref, a_ref, s_ref, ga_ref, gs_ref, da_ref, ds_ref, dga_ref, dgs_ref):
        @pl.when(pl.program_id(0) == 0)
        def _():
            dga_ref[...] = jnp.zeros_like(dga_ref)
            dgs_ref[...] = jnp.zeros_like(dgs_ref)

        dm = lax.dot_general(dx_ref[...].astype(BF16), w_ref[...], _NT, preferred_element_type=F32)
        dxa, dga = _rms_bwd_tile(a_ref[...], ga_ref[...], dm[:, 0:AW])
        da_ref[...] = dxa
        dga_ref[...] += dga
        dxs, dgs = _rms_bwd_tile(s_ref[...], gs_ref[...], dm[:, AW:AW + SW])
        ds_ref[...] = dxs
        dgs_ref[...] += dgs

    return pl.pallas_call(
        body, name="out_proj_dx", grid=(L // TL,),
        in_specs=[_rows(D), _whole((D, D)), _rows(AW), _rows(SW), _whole((1, AW)), _whole((1, SW))],
        out_specs=[_rows(AW), _rows(SW), _whole((1, AW)), _whole((1, SW))],
        out_shape=[jax.ShapeDtypeStruct((L, AW), F32), jax.ShapeDtypeStruct((L, SW), F32),
                   jax.ShapeDtypeStruct((1, AW), F32), jax.ShapeDtypeStruct((1, SW), F32)],
        compiler_params=_cp(("arbitrary",)),
    )(dx1, w_out, attn, ysg, ga, gs)


def _rope_tables():
    half = HD // 8
    inv_freq = jnp.power(ROPE_THETA, -jnp.arange(half, dtype=F32) / half)
    ang = jnp.arange(L, dtype=F32)[:, None] * inv_freq[None, :]
    cos, sin = jnp.cos(ang), jnp.sin(ang)
    one = jnp.ones((L, HD - 2 * half), F32)
    zero = jnp.zeros((L, HD - 2 * half), F32)
    zh = jnp.zeros((L, half), F32)
    cos64 = jnp.concatenate([cos, cos, one], axis=1)
    sa64 = jnp.concatenate([-sin, zh, zero], axis=1)
    sb64 = jnp.concatenate([zh, sin, zero], axis=1)
    return [jnp.tile(t, (1, 2)) for t in (cos64, sa64, sb64)]


def _rope(xv, cosv, sav, sbv, sign):
    return xv * cosv + sign * (pltpu.roll(xv, 120, 1) * sav + pltpu.roll(xv, 8, 1) * sbv)


def _rope_bwd(dq, dk, dv, du, tabs):
    def body(dq_ref, dk_ref, dv_ref, du_ref, c_ref, sa_ref, sb_ref, o_ref):
        for j in range(AW // 128):
            cols = slice(j * 128, (j + 1) * 128)
            o_ref[:, cols] = _rope(dq_ref[:, cols], c_ref[...], sa_ref[...], sb_ref[...], -1.0).astype(BF16)
        o_ref[:, AW:AW + KVW] = _rope(dk_ref[...], c_ref[...], sa_ref[...], sb_ref[...], -1.0).astype(BF16)
        o_ref[:, AW + KVW:AW + 2 * KVW] = dv_ref[...].astype(BF16)
        o_ref[:, AW + 2 * KVW:] = du_ref[...].astype(BF16)

    def row(width):
        return pl.BlockSpec((TL, width), lambda i: (i, 0))

    return pl.pallas_call(
        body, name="rope_bwd", grid=(L // TL,),
        in_specs=[row(AW), row(KVW), row(KVW), row(SW), row(128), row(128), row(128)],
        out_specs=row(INW), out_shape=jax.ShapeDtypeStruct((L, INW), BF16),
        compiler_params=_cp(("parallel",)),
    )(dq, dk, dv, du, *tabs)


def _attn_window(n):
    start = pl.multiple_of(jnp.clip((n - 1) * BLK, 0, L - WIN), BLK)
    qpos = n * BLK + lax.broadcasted_iota(jnp.int32, (BLK, WIN), 0)
    kpos = start + lax.broadcasted_iota(jnp.int32, (BLK, WIN), 1)
    return start, jnp.abs(kpos - qpos) <= BLK


_NT = (((1,), (1,)), ((), ()))
_TN = (((0,), (0,)), ((), ()))
NEG = -1e30


def _attn_fwd(qkv, sink, gather=()):
    ng = len(gather)

    def body(sink_ref, q_ref, k_ref, v_ref, *rest):
        o_ref, lse_ref = rest[ng], rest[ng + 1]
        n = pl.program_id(0)
        if ng:
            travellers = (rest[:ng], rest[ng + 2:2 * ng + 2], *rest[2 * ng + 2:])

            @pl.when(n == 0)
            def _():
                _gather_start(*travellers)

            @pl.when(n == L // BLK - 1)
            def _():
                _gather_finish(*travellers)

        start, valid = _attn_window(n)
        kw = k_ref[pl.ds(start, WIN), :]
        vw = v_ref[pl.ds(start, WIN), :]
        for h in range(NQ):
            kv = h // (NQ // NKV)
            qh = q_ref[:, h * HD:(h + 1) * HD]
            kh = kw[:, kv * HD:(kv + 1) * HD]
            vh = vw[:, kv * HD:(kv + 1) * HD]
            s = lax.dot_general(qh, kh, _NT, preferred_element_type=F32) * (HD ** -0.5)
            s = jnp.where(valid, s, NEG)
            sk = sink_ref[h]
            m = jnp.maximum(jnp.max(s, axis=-1, keepdims=True), sk)
            p = jnp.exp(s - m)
            den = jnp.sum(p, axis=-1, keepdims=True) + jnp.exp(sk - m)
            o_ref[:, h * HD:(h + 1) * HD] = jnp.dot((p / den).astype(BF16), vh, preferred_element_type=F32)
            lse_ref[:, h:h + 1] = m + jnp.log(den)

    outs = pl.pallas_call(
        body, name="attn_fwd", grid=(L // BLK,),
        in_specs=[pl.BlockSpec(memory_space=pltpu.SMEM),
                  pl.BlockSpec((BLK, AW), lambda n: (n, 0)),
                  pl.BlockSpec((L, KVW), lambda n: (0, AW // KVW)),
                  pl.BlockSpec((L, KVW), lambda n: (0, AW // KVW + 1))] + [ANY] * ng,
        out_specs=[pl.BlockSpec((BLK, AW), lambda n: (n, 0)), pl.BlockSpec((BLK, NQ), lambda n: (n, 0))] + [ANY] * ng,
        out_shape=[jax.ShapeDtypeStruct((L, AW), F32), jax.ShapeDtypeStruct((L, NQ), F32)] + _gather_shapes(gather),
        scratch_shapes=_gather_sems(ng) if ng else [],
        compiler_params=_cp(("arbitrary",) if ng else ("parallel",)),
    )(sink, qkv, qkv, qkv, *gather)
    return outs[0], outs[1], list(outs[2:])


def _attn_bwd(qkv, sink, attn, lse, dattn):
    def body(sink_ref, q_ref, k_ref, v_ref, o_ref, lse_ref, do_ref, dq_ref, dk_ref, dv_ref, dsink_ref):
        n = pl.program_id(0)

        @pl.when(n == 0)
        def _():
            dk_ref[...] = jnp.zeros_like(dk_ref)
            dv_ref[...] = jnp.zeros_like(dv_ref)
            dsink_ref[...] = jnp.zeros_like(dsink_ref)

        start, valid = _attn_window(n)
        kw = k_ref[pl.ds(start, WIN), :]
        vw = v_ref[pl.ds(start, WIN), :]
        for kv in range(NKV):
            kh = kw[:, kv * HD:(kv + 1) * HD]
            vh = vw[:, kv * HD:(kv + 1) * HD]
            dk_acc = jnp.zeros((WIN, HD), F32)
            dv_acc = jnp.zeros((WIN, HD), F32)
            for h in range(kv * (NQ // NKV), (kv + 1) * (NQ // NKV)):
                qh = q_ref[:, h * HD:(h + 1) * HD]
                doh = do_ref[:, h * HD:(h + 1) * HD]
                dd = jnp.sum(doh * o_ref[:, h * HD:(h + 1) * HD], axis=-1, keepdims=True)
                lse_h = lse_ref[:, h:h + 1]
                s = lax.dot_general(qh, kh, _NT, preferred_element_type=F32) * (HD ** -0.5)
                p = jnp.where(valid, jnp.exp(s - lse_h), 0.0)
                dob = doh.astype(BF16)
                dp = lax.dot_general(dob, vh, _NT, preferred_element_type=F32)
                ds = (p * (dp - dd) * (HD ** -0.5)).astype(BF16)
                dq_ref[:, h * HD:(h + 1) * HD] = jnp.dot(ds, kh, preferred_element_type=F32)
                dk_acc += lax.dot_general(ds, qh, _TN, preferred_element_type=F32)
                dv_acc += lax.dot_general(p.astype(BF16), dob, _TN, preferred_element_type=F32)
                psink = jnp.exp(sink_ref[h] - lse_h)
                dsk = -jnp.sum(psink * dd, axis=0, keepdims=True)
                dsink_ref[h:h + 1, :] += jnp.broadcast_to(dsk, (1, 128))
            dk_ref[pl.ds(start, WIN), kv * HD:(kv + 1) * HD] += dk_acc
            dv_ref[pl.ds(start, WIN), kv * HD:(kv + 1) * HD] += dv_acc

    qblk = pl.BlockSpec((BLK, AW), lambda n: (n, 0))
    full = pl.BlockSpec((L, KVW), lambda n: (0, 0))
    return pl.pallas_call(
        body, name="attn_bwd", grid=(L // BLK,),
        in_specs=[pl.BlockSpec(memory_space=pltpu.SMEM), qblk,
                  pl.BlockSpec((L, KVW), lambda n: (0, AW // KVW)),
                  pl.BlockSpec((L, KVW), lambda n: (0, AW // KVW + 1)),
                  qblk, pl.BlockSpec((BLK, NQ), lambda n: (n, 0)), qblk],
        out_specs=[qblk, full, full, pl.BlockSpec((NQ, 128), lambda n: (0, 0))],
        out_shape=[jax.ShapeDtypeStruct((L, AW), F32), jax.ShapeDtypeStruct((L, KVW), F32),
                   jax.ShapeDtypeStruct((L, KVW), F32), jax.ShapeDtypeStruct((NQ, 128), F32)],
        compiler_params=_cp(("arbitrary",)),
    )(sink, qkv, qkv, qkv, attn, lse, dattn)


def _perm(a):
    return a.reshape(NSEG, TSEG, a.shape[1]).transpose(1, 0, 2).reshape(L, a.shape[1])


def _unperm(a):
    return a.reshape(TSEG, NSEG, a.shape[1]).transpose(1, 0, 2).reshape(L, a.shape[1])


def _cmul(ar, ai, br, bi):
    return ar * br - ai * bi, ar * bi + ai * br


def _scan_inplace(s_ref, lr, li, rev, visit=None, carried=(), out_ref=None):
    n = lr.shape[1]
    lr8 = jnp.broadcast_to(lr, (NSEG, n))
    li8 = jnp.broadcast_to(li, (NSEG, n))

    def rows(k):
        return pl.ds(pl.multiple_of(jnp.where(rev, TSEG - 1 - k, k) * NSEG, NSEG), NSEG)

    def step(k, c, store):
        sr, si = c
        rs = rows(k)
        pr, pi = _cmul(lr8, li8, sr, si)
        nr = pr + s_ref[rs, 0:n]
        ni = pi + s_ref[rs, n:2 * n]
        if store:
            dst = s_ref if out_ref is None else out_ref
            dst[rs, 0:n] = nr.astype(dst.dtype)
            dst[rs, n:2 * n] = ni.astype(dst.dtype)
        return nr, ni

    z = jnp.zeros((NSEG, n), F32)
    er, ei = lax.fori_loop(0, TSEG, functools.partial(step, store=False), (z, z))
    pr, pi = lr, li
    for _ in range(int(math.log2(TSEG))):
        pr, pi = _cmul(pr, pi, pr, pi)

    seg = lax.broadcasted_iota(jnp.int32, (NSEG, n), 0)

    def chain(order):
        cr = jnp.zeros((1, n), F32)
        ci = jnp.zeros((1, n), F32)
        outr = jnp.zeros((NSEG, n), F32)
        outi = jnp.zeros((NSEG, n), F32)
        for s in order:
            outr = jnp.where(seg == s, cr, outr)
            outi = jnp.where(seg == s, ci, outi)
            mr, mi = _cmul(pr, pi, cr, ci)
            cr, ci = mr + er[s:s + 1], mi + ei[s:s + 1]
        return outr, outi

    fr, fi = chain(range(NSEG))
    rr, ri = chain(range(NSEG - 1, -1, -1))
    cin_r = jnp.where(rev, rr, fr)
    cin_i = jnp.where(rev, ri, fi)
    if visit is None:
        lax.fori_loop(0, TSEG, functools.partial(step, store=True), (cin_r, cin_i))
        return cin_r, cin_i

    def visited(k, c):
        nr, ni = step(k, c[:2], True)
        return (nr, ni) + tuple(visit(k, nr, ni, c[2:]))

    fin = lax.fori_loop(0, TSEG - 1, visited, (cin_r, cin_i) + tuple(carried))
    last_r, last_i = step(TSEG - 1, fin[:2], True)
    return last_r, last_i, fin[2:]


S5_RC = 512


def _s5_specs():
    u_spec = pl.BlockSpec((L, 128), lambda cb, h, d: (0, cb))
    b_spec = pl.BlockSpec((None, None, 128, 2 * SBW), lambda cb, h, d: (d, cb * 2 + h, 0, 0))
    c_spec = pl.BlockSpec((None, None, 2 * SBW, 128), lambda cb, h, d: (d, cb * 2 + h, 0, 0))
    l_spec = pl.BlockSpec((None, None, 1, SBW), lambda cb, h, d: (d, cb * 2 + h, 0, 0))
    d_spec = pl.BlockSpec((1, 128), lambda cb, h, d: (0, cb))
    return u_spec, b_spec, c_spec, l_spec, d_spec


def _s5_fwd(u_p, bcat, ccat, lam_re, lam_im, dskip):
    def body(u_ref, b_ref, c_ref, lr_ref, li_ref, d_ref, y_ref, sb_ref, s_scr):
        first = (pl.program_id(1) == 0) & (pl.program_id(2) == 0)

        def proj(i, _):
            rs = pl.ds(pl.multiple_of(i * S5_RC, S5_RC), S5_RC)
            s_scr[rs, :] = jnp.dot(u_ref[rs, :].astype(BF16), b_ref[...], preferred_element_type=F32)
            return 0

        lax.fori_loop(0, L // S5_RC, proj, 0)
        _scan_inplace(s_scr, lr_ref[...], li_ref[...], pl.program_id(2) == 1, out_ref=sb_ref)

        def out(i, _):
            rs = pl.ds(pl.multiple_of(i * S5_RC, S5_RC), S5_RC)
            yv = jnp.dot(sb_ref[rs, :], c_ref[...], preferred_element_type=F32)

            @pl.when(first)
            def _():
                y_ref[rs, :] = d_ref[...] * u_ref[rs, :] + yv

            @pl.when(jnp.logical_not(first))
            def _():
                y_ref[rs, :] += yv

            return 0

        lax.fori_loop(0, L // S5_RC, out, 0)

    u_spec, b_spec, c_spec, l_spec, d_spec = _s5_specs()
    return pl.pallas_call(
        body, name="s5_fwd", grid=(SW // 128, 2, 2),
        in_specs=[u_spec, b_spec, c_spec, l_spec, l_spec, d_spec],
        out_specs=[u_spec, _s5_state_spec()],
        out_shape=[jax.ShapeDtypeStruct((L, SW), F32), jax.ShapeDtypeStruct((2, NSB, L, 2 * SBW), BF16)],
        scratch_shapes=[pltpu.VMEM((L, 2 * SBW), F32)],
        compiler_params=_cp(("parallel", "arbitrary", "arbitrary")),
    )(u_p, bcat, ccat, lam_re, lam_im, dskip)


def _s5_state_spec():
    return pl.BlockSpec((None, None, L, 2 * SBW), lambda cb, h, d: (d, cb * 2 + h, 0, 0))


def _s5_bwd(u_p, dy_p, states, bcat, ccat, lam_re, lam_im, dskip, ride=()):
    kind, riding = ride if ride else (None, ())
    nr = len(riding)
    grid = (SW // 128, 2, 2)

    def body(*refs):
        work = refs[:8] + refs[8 + nr:14 + nr] + refs[14 + 2 * nr:16 + 2 * nr]
        if nr:
            riders = (refs[8:8 + nr], refs[14 + nr:14 + 2 * nr], *refs[16 + 2 * nr:])
            step = [pl.program_id(d) for d in range(3)]

            @pl.when((step[0] == 0) & (step[1] == 0) & (step[2] == 0))
            def _():
                _start_all(_exchange_copies(kind, *riders))

        compute(*work)
        if nr:
            @pl.when((step[0] == grid[0] - 1) & (step[1] == grid[1] - 1) & (step[2] == grid[2] - 1))
            def _():
                _wait_all(_exchange_copies(kind, *riders))

    def compute(u_ref, dy_ref, s_ref, b_ref, c_ref, lr_ref, li_ref, d_ref,
                du_ref, db_ref, dc_ref, dlr_ref, dli_ref, dd_ref, g_scr, gb_scr):
        first = (pl.program_id(1) == 0) & (pl.program_id(2) == 0)
        rev = pl.program_id(2) == 1

        def dstate(i, _):
            rs = pl.ds(pl.multiple_of(i * S5_RC, S5_RC), S5_RC)
            g_scr[rs, :] = lax.dot_general(dy_ref[rs, :].astype(BF16), c_ref[...], _NT, preferred_element_type=F32)
            return 0

        lax.fori_loop(0, L // S5_RC, dstate, 0)

        def before(rows):
            sv = s_ref[rows, :].astype(F32)
            return sv[:, 0:SBW], sv[:, SBW:2 * SBW]

        def dlam(gr, gi, sr, si, ar, ai):
            return ar + gr * sr + gi * si, ai + gi * sr - gr * si

        def visit(k, gr, gi, acc):
            ts = jnp.where(rev, k + 1, TSEG - 2 - k)
            sr, si = before(pl.ds(pl.multiple_of(ts * NSEG, NSEG), NSEG))
            return dlam(gr, gi, sr, si, *acc)

        z = jnp.zeros((NSEG, SBW), F32)
        gr, gi, acc = _scan_inplace(g_scr, lr_ref[...], -li_ref[...], jnp.logical_not(rev), visit, (z, z), gb_scr)
        edge_r, edge_i = before(pl.ds(pl.multiple_of(jnp.where(rev, 0, TSEG - 1) * NSEG, NSEG), NSEG))
        seg = lax.broadcasted_iota(jnp.int32, (NSEG, SBW), 0)
        keep = seg != jnp.where(rev, NSEG - 1, 0)

        def neighbour(e):
            return jnp.where(keep, jnp.where(rev, pltpu.roll(e, NSEG - 1, 0), pltpu.roll(e, 1, 0)), 0.0)

        ar, ai = dlam(gr, gi, neighbour(edge_r), neighbour(edge_i), *acc)
        dlr_ref[...] = jnp.sum(ar, axis=0, keepdims=True)
        dli_ref[...] = jnp.sum(ai, axis=0, keepdims=True)

        db_ref[...] = jnp.zeros_like(db_ref)
        dc_ref[...] = jnp.zeros_like(dc_ref)

        @pl.when(first)
        def _():
            dd_ref[...] = jnp.zeros_like(dd_ref)

        def grads(i, _):
            rs = pl.ds(pl.multiple_of(i * S5_RC, S5_RC), S5_RC)
            uv = u_ref[rs, :]
            dyv = dy_ref[rs, :]
            gb = gb_scr[rs, :]
            db_ref[...] += lax.dot_general(uv.astype(BF16), gb, _TN, preferred_element_type=F32)
            dc_ref[...] += lax.dot_general(dyv.astype(BF16), s_ref[rs, :], _TN, preferred_element_type=F32)
            duv = lax.dot_general(gb, b_ref[...], _NT, preferred_element_type=F32)

            @pl.when(first)
            def _():
                du_ref[rs, :] = d_ref[...] * dyv + duv
                dd_ref[...] += jnp.sum(dyv * uv, axis=0, keepdims=True)

            @pl.when(jnp.logical_not(first))
            def _():
                du_ref[rs, :] += duv

            return 0

        lax.fori_loop(0, L // S5_RC, grads, 0)

    u_spec, b_spec, c_spec, l_spec, d_spec = _s5_specs()
    outs = pl.pallas_call(
        body, name="s5_bwd", grid=grid,
        in_specs=[u_spec, u_spec, _s5_state_spec(), b_spec, c_spec, l_spec, l_spec, d_spec] + [ANY] * nr,
        out_specs=[u_spec, b_spec, b_spec, l_spec, l_spec, d_spec] + [ANY] * nr,
        out_shape=[jax.ShapeDtypeStruct((L, SW), F32),
                   jax.ShapeDtypeStruct((2, NSB, 128, 2 * SBW), F32), jax.ShapeDtypeStruct((2, NSB, 128, 2 * SBW), F32),
                   jax.ShapeDtypeStruct((2, NSB, 1, SBW), F32), jax.ShapeDtypeStruct((2, NSB, 1, SBW), F32),
                   jax.ShapeDtypeStruct((1, SW), F32)] + (_exchange_shapes(kind, riding) if nr else []),
        scratch_shapes=[pltpu.VMEM((L, 2 * SBW), F32), pltpu.VMEM((L, 2 * SBW), BF16)]
        + (_exchange_sems(kind, nr) if nr else []),
        compiler_params=_cp(("arbitrary",) * 3 if nr else ("parallel", "arbitrary", "arbitrary")),
    )(u_p, dy_p, states, bcat, ccat, lam_re, lam_im, dskip, *riding)
    return list(outs[:6]), list(outs[6:])


def _s5_params(a_re, a_im, log_step, bt_re, bt_im):
    lam = lax.complex(a_re, a_im)
    step = jnp.exp(log_step)[..., None]
    lam_bar = jnp.exp(lam * step)
    b_bar = ((lam_bar - 1.0) / lam)[..., None, :] * lax.complex(bt_re, bt_im)
    return jnp.real(lam_bar), jnp.imag(lam_bar), jnp.real(b_bar), jnp.imag(b_bar)


def _sel():
    i = jnp.arange(8)[None, :, None]
    j = jnp.arange(4)[None, None, :]
    r = jnp.arange(2)[:, None, None]
    return (i == r * 4 + j).astype(F32)


def _to_bcat(bt_re, bt_im):
    def one(bt):
        return jnp.einsum('dkrjcp,rij->dkricjp', bt.reshape(2, 4, 2, 4, GC, NP), _sel()).reshape(2, NSB, 128, SBW)
    return jnp.concatenate([one(bt_re), one(bt_im)], axis=-1)


def _from_bcat(dbcat):
    def one(dbbd):
        return jnp.einsum('dkricjp,rij->dkrjcp', dbbd.reshape(2, 4, 2, 8, GC, 4, NP), _sel()).reshape(2, NG, GC, NP)
    return one(dbcat[..., :SBW]), one(dbcat[..., SBW:])


def _to_ccat(c_re, c_im):
    def one(cc):
        return jnp.einsum('dkrjcp,rij->dkrjpic', cc.reshape(2, 4, 2, 4, GC, NP), _sel()).reshape(2, NSB, SBW, 128)
    return jnp.concatenate([one(c_re), -one(c_im)], axis=-2)


def _from_ccat(dccat):
    def one(dcbd):
        return jnp.einsum('dkrjpic,rij->dkrjcp', dcbd.reshape(2, 4, 2, 4, NP, 8, GC), _sel()).reshape(2, NG, GC, NP)
    return one(dccat[:, :, :SBW]), -one(dccat[:, :, SBW:])


def _gelu(y):
    return 0.5 * y * (1.0 + lax.erf(y * (2.0 ** -0.5)))


def _gelu_grad(y):
    return 0.5 * (1.0 + lax.erf(y * (2.0 ** -0.5))) + y * jnp.exp(-0.5 * y * y) * ((2.0 * math.pi) ** -0.5)


def _sigmoid(z):
    return 0.5 * jnp.tanh(0.5 * z) + 0.5


def _glu_fwd(y, wg):
    def body(y_ref, w_ref, o_ref, z_ref):
        ys = _gelu(y_ref[...])
        z = jnp.dot(ys.astype(BF16), w_ref[...], preferred_element_type=F32)
        z_ref[...] = z
        o_ref[...] = ys * _sigmoid(z)

    row = pl.BlockSpec((TL, SW), lambda i: (i, 0))
    return pl.pallas_call(
        body, name="glu_fwd", grid=(L // TL,),
        in_specs=[row, pl.BlockSpec((SW, SW), lambda i: (0, 0))], out_specs=[row, row],
        out_shape=[jax.ShapeDtypeStruct((L, SW), F32), jax.ShapeDtypeStruct((L, SW), F32)],
        compiler_params=_cp(("parallel",)),
    )(y, wg)


def _glu_bwd(y, z, dout, wg):
    def body(y_ref, z_ref, do_ref, w_ref, dy_ref, dw_ref):
        @pl.when(pl.program_id(0) == 0)
        def _():
            dw_ref[...] = jnp.zeros_like(dw_ref)

        yv = y_ref[...]
        ys = _gelu(yv)
        sg = _sigmoid(z_ref[...])
        dov = do_ref[...]
        dz = (dov * ys * sg * (1.0 - sg)).astype(BF16)
        dys = dov * sg + lax.dot_general(dz, w_ref[...], _NT, preferred_element_type=F32)
        dy_ref[...] = dys * _gelu_grad(yv)
        dw_ref[...] += lax.dot_general(ys.astype(BF16), dz, _TN, preferred_element_type=F32)

    row = pl.BlockSpec((TL, SW), lambda i: (i, 0))
    wsp = pl.BlockSpec((SW, SW), lambda i: (0, 0))
    return pl.pallas_call(
        body, name="glu_bwd", grid=(L // TL,),
        in_specs=[row, row, row, wsp], out_specs=[row, wsp],
        out_shape=[jax.ShapeDtypeStruct((L, SW), F32), jax.ShapeDtypeStruct((SW, SW), F32)],
        compiler_params=_cp(("arbitrary",)),
    )(y, z, dout, wg)


CT = 256
CR = 128
NCT = DFF // CT


def _shifted(ref, r):
    h = 8 * (4 // ref.dtype.itemsize)
    cur = ref[pl.ds(r, CR), :].astype(F32)
    before = ref[pl.ds(pl.multiple_of(jnp.maximum(r - h, 0), h), h), :][h - 1:h, :].astype(F32)
    after = ref[pl.ds(pl.multiple_of(jnp.minimum(r + CR, L - h), h), h), :][0:1, :].astype(F32)
    before = jnp.where(r > 0, before, 0.0)
    after = jnp.where(r + CR < L, after, 0.0)
    row = lax.broadcasted_iota(jnp.int32, cur.shape, 0)
    prev = jnp.where(row == 0, before, pltpu.roll(cur, 1, 0))
    nxt = jnp.where(row == CR - 1, after, pltpu.roll(cur, CR - 1, 0))
    return prev, cur, nxt


def _conv3(ref, r, w_ref, b_ref):
    prev, cur, nxt = _shifted(ref, r)
    return w_ref[0:1, :] * prev + w_ref[1:2, :] * cur + w_ref[2:3, :] * nxt + b_ref[...]


def _convact_fwd(up, conv_w, conv_b):
    def body(ug_ref, uv_ref, wg_ref, wv_ref, bg_ref, bv_ref, o_ref, g_ref, v_ref):
        def chunk(i, _):
            r = pl.multiple_of(i * CR, CR)
            rs = pl.ds(r, CR)
            g = _conv3(ug_ref, r, wg_ref, bg_ref)
            v = _conv3(uv_ref, r, wv_ref, bv_ref)
            o_ref[rs, :] = (g * _sigmoid(g) * v).astype(BF16)
            g_ref[rs, :] = g.astype(BF16)
            v_ref[rs, :] = v.astype(BF16)
            return 0

        lax.fori_loop(0, L // CR, chunk, 0)

    gcol = pl.BlockSpec((L, CT), lambda j: (0, j))
    vcol = pl.BlockSpec((L, CT), lambda j: (0, j + NCT))
    return pl.pallas_call(
        body, name="convact_fwd", grid=(NCT,),
        in_specs=[gcol, vcol,
                  pl.BlockSpec((3, CT), lambda j: (0, j)), pl.BlockSpec((3, CT), lambda j: (0, j + NCT)),
                  pl.BlockSpec((1, CT), lambda j: (0, j)), pl.BlockSpec((1, CT), lambda j: (0, j + NCT))],
        out_specs=[gcol, gcol, gcol], out_shape=[jax.ShapeDtypeStruct((L, DFF), BF16)] * 3,
        compiler_params=_cp(("parallel",)),
    )(up, up, conv_w, conv_w, conv_b, conv_b)


def _convact_bwd(up, gq, vq, dact, conv_w):
    def body(ug_ref, uv_ref, g_ref, v_ref, da_ref, wg_ref, wv_ref, du_ref, dw_ref, db_ref, dgs, dvs, dbv):
        half = pl.program_id(1)

        def transpose_conv(src, u_ref, w_ref):
            dw_ref[...] = jnp.zeros_like(dw_ref)

            def chunk(i, _):
                r = pl.multiple_of(i * CR, CR)
                rs = pl.ds(r, CR)
                prev, cur, nxt = _shifted(src, r)
                du_ref[rs, :] = (w_ref[0:1, :] * nxt + w_ref[1:2, :] * cur + w_ref[2:3, :] * prev).astype(BF16)
                uv = u_ref[rs, :].astype(F32)
                for k, d in enumerate((nxt, cur, prev)):
                    dw_ref[k:k + 1, :] += jnp.sum(d * uv, axis=0, keepdims=True)
                return 0

            lax.fori_loop(0, L // CR, chunk, 0)

        @pl.when(half == 0)
        def _():
            db_ref[...] = jnp.zeros_like(db_ref)
            dbv[...] = jnp.zeros_like(dbv)

            def chunk1(i, _):
                rs = pl.ds(pl.multiple_of(i * CR, CR), CR)
                g = g_ref[rs, :].astype(F32)
                v = v_ref[rs, :].astype(F32)
                sg = _sigmoid(g)
                da = da_ref[rs, :].astype(F32)
                dv = da * g * sg
                dg = da * v * sg * (1.0 + g * (1.0 - sg))
                dgs[rs, :] = dg
                dvs[rs, :] = dv
                db_ref[...] += jnp.sum(dg, axis=0, keepdims=True)
                dbv[0:1, :] += jnp.sum(dv, axis=0, keepdims=True)
                return 0

            lax.fori_loop(0, L // CR, chunk1, 0)
            transpose_conv(dgs, ug_ref, wg_ref)

        @pl.when(half == 1)
        def _():
            db_ref[...] = dbv[0:1, :]
            transpose_conv(dvs, uv_ref, wv_ref)

    def col(rows, off):
        return pl.BlockSpec((rows, CT), lambda j, h: (0, j + off))

    def out(rows):
        return pl.BlockSpec((rows, CT), lambda j, h: (0, j + h * NCT))

    return pl.pallas_call(
        body, name="convact_bwd", grid=(NCT, 2),
        in_specs=[col(L, 0), col(L, NCT), col(L, 0), col(L, 0), col(L, 0), col(3, 0), col(3, NCT)],
        out_specs=[out(L), out(3), out(1)],
        out_shape=[jax.ShapeDtypeStruct((L, 2 * DFF), BF16), jax.ShapeDtypeStruct((3, 2 * DFF), F32),
                   jax.ShapeDtypeStruct((1, 2 * DFF), F32)],
        scratch_shapes=[pltpu.VMEM((L, CT), F32), pltpu.VMEM((L, CT), F32), pltpu.VMEM((8, CT), F32)],
        compiler_params=_cp(("parallel", "arbitrary")),
    )(up, up, gq, vq, dact, conv_w, conv_w)


def _local_step(x, tgt, w_in_t, p, attend, stage):
    tabs = _rope_tables()
    lam_re, lam_im, bb_re, bb_im = _s5_params(p["a_re"], p["a_im"], p["log_step"], p["bt_re"], p["bt_im"])
    bcat = _to_bcat(bb_re, bb_im).astype(BF16)
    ccat = _to_ccat(p["c_re"], p["c_im"]).astype(BF16)
    lam_re4, lam_im4 = lam_re.reshape(2, NSB, 1, SBW), lam_im.reshape(2, NSB, 1, SBW)
    dskip = p["d_skip"].reshape(1, SW)
    g_mix, g_ffn, g_fin = p["norm_mix_g"].reshape(1, D), p["norm_ffn_g"].reshape(1, D), p["norm_final_g"].reshape(1, D)
    g_attn, g_ssm = p["norm_attn_g"].reshape(1, AW), p["norm_ssm_g"].reshape(1, SW)
    sink = p["sink"].reshape(NQ)
    conv_b = p["conv_b"].reshape(1, 2 * DFF)

    rows, gain = jax.ShapeDtypeStruct((L, D), F32), jax.ShapeDtypeStruct((1, D), F32)
    rows16 = jax.ShapeDtypeStruct((L, D), BF16)
    h1, qkv, u = _in_proj(x, g_mix, w_in_t, tabs)
    attn, lse, wts = attend(qkv, sink)
    w_glu, w_out, w_up_t, w_down, conv_w = (wts[k] for k in ("w_glu", "w_out", "w_up_t", "w_down", "conv_w"))
    u_p = _perm(u)
    y_p, states = _s5_fwd(u_p, bcat, ccat, lam_re4, lam_im4, dskip)
    ysg_p, z_p = _glu_fwd(y_p, w_glu)
    ysg = _unperm(ysg_p)
    mixed, x1, h2 = _out_proj(attn, ysg, g_attn, g_ssm, w_out, x, g_ffn)
    up = _mm(h2, w_up_t, tb=True, name="ffn_up", tn=1408, out_dtype=BF16)
    act, gq, vq = _convact_fwd(up, conv_w, conv_b)
    loss, dx2, dx2b, dg_fin = _mm(
        act, w_down, add=x1, name="ffn_down", tm=512, tk=DFF,
        post=(_final_post, [tgt, g_fin], [jax.ShapeDtypeStruct((1, 1), F32), rows, rows16, gain]))

    def riding(res, ride):
        return res if ride else (res, None)

    dw_down = _mm(act, dx2b, ta=True, name="ffn_down_dw", tm=256, tk=L)
    ride = stage("w_down", "cores", dw_down)
    dact, got = riding(_mm(dx2b, w_down, tb=True, name="ffn_down_dx", tn=1408, out_dtype=BF16, ride=ride), ride)
    ride = stage("w_down", "chips", got)
    dup, dconv_w, dconv_b = _convact_bwd(up, gq, vq, dact, conv_w)
    dw_up_t, got = riding(_mm(dup, h2, ta=True, name="ffn_up_dw", tm=512, tk=L, ride=ride), ride)
    stage("w_down", "done", got)
    ride = stage("w_up_t", "cores", dw_up_t)
    (dx1, dx1b, dg_ffn), got = riding(
        _mm(dup, w_up_t, name="ffn_up_dx", tm=512, tk=2 * DFF, ride=ride,
            post=(_rms_bwd_post, [x1, dx2, g_ffn], [rows, rows16, gain])), ride)
    ride = stage("w_up_t", "chips", got)
    dattn, dysg, dg_attn, dg_ssm = _out_proj_dx(dx1b, w_out, attn, ysg, g_attn, g_ssm)
    dw_out = _mm(mixed, dx1b, ta=True, name="out_proj_dw", tm=512, tk=L)
    dy_p, dw_glu = _glu_bwd(y_p, z_p, _perm(dysg), w_glu)
    (du_p, dbcat, dccat, dlam_re, dlam_im, dd), got = _s5_bwd(u_p, dy_p, states, bcat, ccat, lam_re4, lam_im4, dskip,
                                                              ride=ride)
    stage("w_up_t", "done", got)
    dbb_re, dbb_im = _from_bcat(dbcat)
    dc_re, dc_im = _from_ccat(_swap(dccat))
    dq, dk, dv, dsink = _attn_bwd(qkv, sink, attn, lse, dattn)
    dproj = _rope_bwd(dq, dk, dv, _unperm(du_p), tabs)
    dw_in_t = _mm(dproj, h1, ta=True, name="in_proj_dw", tm=640, tk=L)
    grad_x, dg_mix = _in_proj_dx(dproj, w_in_t, x, g_mix, dx1)

    big = dict(w_in_t=dw_in_t, w_glu=dw_glu, w_out=dw_out)
    small = dict(norm_mix_g=dg_mix, norm_attn_g=dg_attn, norm_ssm_g=dg_ssm, norm_ffn_g=dg_ffn, norm_final_g=dg_fin,
                 sink=dsink[:, 0], conv_b=dconv_b, d_skip=dd, conv_w=dconv_w,
                 lam_re=dlam_re.reshape(2, NG, NP), lam_im=dlam_im.reshape(2, NG, NP),
                 bb_re=dbb_re, bb_im=dbb_im, c_re=dc_re, c_im=dc_im, loss=loss.reshape(1))
    return grad_x, big, small


ANY = pl.BlockSpec(memory_space=pl.ANY)


def _coords():
    return lax.axis_index("x"), lax.axis_index("y"), lax.axis_index("c")


def _flip(v, b):
    return v + b - 2 * v * b if b else v


def _all_gather(shards, name):
    n = len(shards)

    def body(*refs):
        _gather_start(refs[:n], refs[n:2 * n], *refs[2 * n:])
        _gather_finish(refs[:n], refs[n:2 * n], *refs[2 * n:])

    return pl.pallas_call(
        body, name=name,
        in_specs=[ANY] * n, out_specs=[ANY] * n,
        out_shape=_gather_shapes(shards), scratch_shapes=_gather_sems(n),
    )(*shards)


def _gather_shapes(shards):
    return [jax.ShapeDtypeStruct((NDEV * s.shape[0], s.shape[1]), s.dtype) for s in shards]


def _gather_sems(n):
    return [pltpu.SemaphoreType.DMA((7 * n,)), pltpu.SemaphoreType.DMA((7 * n,)), pltpu.SemaphoreType.DMA((n,))]


def _gather_copies(ins, outs, send_sems, recv_sems, local_sems, a):
    x, y, c = _coords()
    me, sibling = (x, y, c), (x, y, 1 - c)
    chips = [(1 - x, y), (x, 1 - y), (1 - x, 1 - y)]
    r = ins[a].shape[0]

    def rows(px, py, pc):
        return outs[a].at[pl.ds(pl.multiple_of((4 * px + 2 * py + pc) * r, 8), r), :]

    def copy(k, block, to, src=None):
        return pltpu.make_async_remote_copy(
            src_ref=rows(*block) if src is None else src, dst_ref=rows(*block),
            send_sem=send_sems.at[a * 7 + k], recv_sem=recv_sems.at[a * 7 + k],
            device_id=to, device_id_type=pl.DeviceIdType.MESH)

    mine = pltpu.make_async_copy(ins[a], rows(*me), local_sems.at[a])
    first = [copy(0, me, sibling, src=ins[a])]
    first += [copy(1 + j, me, (*chip, c), src=ins[a]) for j, chip in enumerate(chips)]
    passed = [copy(4 + j, (*chip, c), sibling) for j, chip in enumerate(chips)]
    arrivals = [copy(1 + j, (*chip, c), me) for j, chip in enumerate(chips)]
    from_sibling = [copy(0, sibling, me)] + [copy(4 + j, (*chip, 1 - c), me) for j, chip in enumerate(chips)]
    return mine, first, passed, arrivals, from_sibling


def _gather_start(ins, outs, send_sems, recv_sems, local_sems):
    for a in range(len(ins)):
        mine, first, _, _, _ = _gather_copies(ins, outs, send_sems, recv_sems, local_sems, a)
        mine.start()
        for cp in first:
            cp.start()


def _gather_finish(ins, outs, send_sems, recv_sems, local_sems):
    n = len(ins)
    parts = [_gather_copies(ins, outs, send_sems, recv_sems, local_sems, a) for a in range(n)]
    for mine, first, passed, arrivals, from_sibling in parts:
        for arrived, onward in zip(arrivals, passed):
            arrived.wait_recv()
            onward.start()
    for mine, first, passed, arrivals, from_sibling in parts:
        for cp in from_sibling:
            cp.wait_recv()
        for cp in first + passed:
            cp.wait_send()
        mine.wait()


NCHIP = 4
CHIP_FLIPS = ((1, 0), (0, 1), (1, 1))


def _planned_copies(ins, outs, send_sems, recv_sems, plan):
    return [pltpu.make_async_remote_copy(
        src_ref=ins[a].at[src], dst_ref=outs[a].at[dst], send_sem=send_sems.at[k], recv_sem=recv_sems.at[k],
        device_id=to, device_id_type=pl.DeviceIdType.MESH) for k, (a, src, dst, to) in enumerate(plan)]


def _start_all(copies):
    for cp in copies:
        cp.start()


def _wait_all(copies):
    for cp in copies:
        cp.wait_recv()
    for cp in copies:
        cp.wait_send()


SLOTS = {"cores": NCHIP, "chips": 3}


def _exchange_copies(kind, ins, outs, send_sems, recv_sems):
    x, y, c = _coords()
    plan = []
    for a in range(len(ins)):
        if kind == "cores":
            plan += [(a, 2 * q + 1 - c, q, (x, y, 1 - c)) for q in range(NCHIP)]
        else:
            for j, (fx, fy) in enumerate(CHIP_FLIPS):
                px, py = _flip(x, fx), _flip(y, fy)
                plan.append((a, 2 * px + py, j, (px, py, c)))
    return _planned_copies(ins, outs, send_sems, recv_sems, plan)


def _exchange_shapes(kind, parts):
    return [jax.ShapeDtypeStruct((SLOTS[kind],) + s.shape[1:], s.dtype) for s in parts]


def _exchange_sems(kind, n):
    return [pltpu.SemaphoreType.DMA((SLOTS[kind] * n,)), pltpu.SemaphoreType.DMA((SLOTS[kind] * n,))]


def _exchange(kind, parts, name):
    n = len(parts)

    def body(*refs):
        copies = _exchange_copies(kind, refs[:n], refs[n:2 * n], *refs[2 * n:])
        _start_all(copies)
        _wait_all(copies)

    return pl.pallas_call(
        body, name=name, in_specs=[ANY] * n, out_specs=[ANY] * n,
        out_shape=_exchange_shapes(kind, parts), scratch_shapes=_exchange_sems(kind, n),
    )(*parts)


def _pair_sum(where, part, recv, wire_dtype, name):
    _, r, c = part.shape
    tr = _pick(r, 256, 16)

    def body(w_ref, p_ref, r_ref, pb_ref, own_ref):
        s = p_ref[...] + r_ref[...]
        pb_ref[...] = s.astype(wire_dtype)

        @pl.when(pl.program_id(1) == w_ref[1])
        def _():
            own_ref[...] = s

    return pl.pallas_call(
        body, name=name,
        grid_spec=pltpu.PrefetchScalarGridSpec(
            num_scalar_prefetch=1, grid=(r // tr, NCHIP),
            in_specs=[pl.BlockSpec((None, tr, c), lambda i, q, w: (2 * q + w[0], i, 0)),
                      pl.BlockSpec((None, tr, c), lambda i, q, w: (q, i, 0))],
            out_specs=[pl.BlockSpec((None, tr, c), lambda i, q, w: (q, i, 0)),
                       pl.BlockSpec((tr, c), lambda i, q, w: (i, 0))]),
        out_shape=[jax.ShapeDtypeStruct((NCHIP, r, c), wire_dtype), jax.ShapeDtypeStruct((r, c), F32)],
        compiler_params=_cp(("parallel", "arbitrary")),
    )(where, part, recv)


def _chip_sum(own, recv, name):
    r, c = own.shape
    tr = _pick(r, 256, 16)

    def body(o_ref, r_ref, out_ref):
        acc = o_ref[...]
        for j in range(3):
            acc = acc + r_ref[j].astype(F32)
        out_ref[...] = acc

    return pl.pallas_call(
        body, name=name, grid=(r // tr,),
        in_specs=[pl.BlockSpec((tr, c), lambda i: (i, 0)), pl.BlockSpec((3, tr, c), lambda i: (0, i, 0))],
        out_specs=pl.BlockSpec((tr, c), lambda i: (i, 0)),
        out_shape=jax.ShapeDtypeStruct((r, c), F32),
        compiler_params=_cp(("parallel",)),
    )(own, recv)


def _adamw(w, own, recv, m, v, name):
    r, c = w.shape
    tr = _pick(r, 256, 16)

    def body(w_ref, o_ref, r_ref, m_ref, v_ref, g_ref, d_ref, nm_ref, nv_ref):
        acc = o_ref[...]
        for j in range(3):
            acc = acc + r_ref[j].astype(F32)
        g_ref[...] = acc
        _adamw_refs(w_ref, g_ref, m_ref, v_ref, d_ref, nm_ref, nv_ref)

    blk = pl.BlockSpec((tr, c), lambda i: (i, 0))
    return pl.pallas_call(
        body, name=name, grid=(r // tr,),
        in_specs=[blk, blk, pl.BlockSpec((3, tr, c), lambda i: (0, i, 0)), blk, blk], out_specs=[blk] * 4,
        out_shape=[jax.ShapeDtypeStruct((r, c), F32)] * 4,
        compiler_params=_cp(("parallel",)),
    )(w, own, recv, m, v)


def _adamw_refs(w_ref, g_ref, m_ref, v_ref, d_ref, nm_ref, nv_ref):
    gv = g_ref[...]
    nm = B1 * m_ref[...] + (1.0 - B1) * gv
    nv = B2 * v_ref[...] + (1.0 - B2) * (gv * gv)
    nm_ref[...] = nm
    nv_ref[...] = nv
    d_ref[...] = -LR * ((nm / C1) / (jnp.sqrt(nv / C2) + AEPS) + WD * w_ref[...])


def _adamw_small(ws, gs, ms, vs, name):
    n = len(ws)

    def body(*refs):
        groups = [refs[i * n:(i + 1) * n] for i in range(7)]
        for per_param in zip(*groups):
            _adamw_refs(*per_param)

    vm = pl.BlockSpec(memory_space=pltpu.VMEM)
    outs = pl.pallas_call(
        body, name=name, in_specs=[vm] * (4 * n), out_specs=[vm] * (3 * n),
        out_shape=[jax.ShapeDtypeStruct(a.shape, F32) for a in ws] * 3,
    )(*ws, *gs, *ms, *vs)
    return outs[:n], outs[n:2 * n], outs[2 * n:]


def _swap(a):
    return jnp.swapaxes(a, -1, -2)


VIEWS = {
    "w_in": (lambda a: a[0].T, lambda u: u.T[None]),
    "w_up": (lambda a: a[0].T, lambda u: u.T[None]),
    "w_glu": (lambda a: a[0], lambda u: u[None]),
    "w_out": (lambda a: a[0], lambda u: u[None]),
    "w_down": (lambda a: a[0], lambda u: u[None]),
    "conv_w": (lambda a: a[0], lambda u: u[None]),
    "norm_mix_g": (lambda a: a, lambda u: u),
    "norm_attn_g": (lambda a: a, lambda u: u),
    "norm_ssm_g": (lambda a: a, lambda u: u),
    "norm_ffn_g": (lambda a: a, lambda u: u),
    "norm_final_g": (lambda a: a[None], lambda u: u[0]),
    "conv_b": (lambda a: a, lambda u: u),
    "sink": (lambda a: a, lambda u: u),
    "a_re": (lambda a: a.reshape(2 * NG, NP), lambda u: u.reshape(1, 2, NG, NP)),
    "a_im": (lambda a: a.reshape(2 * NG, NP), lambda u: u.reshape(1, 2, NG, NP)),
    "log_step": (lambda a: a[0], lambda u: u[None]),
    "b_re": (lambda a: _swap(a[0]).reshape(2 * NG * GC, NP), lambda u: _swap(u.reshape(2, NG, GC, NP))[None]),
    "b_im": (lambda a: _swap(a[0]).reshape(2 * NG * GC, NP), lambda u: _swap(u.reshape(2, NG, GC, NP))[None]),
    "c_re": (lambda a: a.reshape(2 * NG * GC, NP), lambda u: u.reshape(1, 2, NG, GC, NP)),
    "c_im": (lambda a: a.reshape(2 * NG * GC, NP), lambda u: u.reshape(1, 2, NG, GC, NP)),
    "d_skip": (lambda a: a[0].T, lambda u: u.T[None]),
}
BIG = ["w_in", "w_glu", "w_out", "w_up", "w_down"]
PACK_W = 1024


def _pack(arrs, rows):
    flat = jnp.concatenate([a.reshape(-1).astype(F32) for a in arrs])
    return jnp.pad(flat, (0, rows * PACK_W - flat.shape[0])).reshape(rows, PACK_W)


def _unpack(packed, shapes):
    flat = packed.reshape(-1)
    out, off = [], 0
    for s in shapes:
        size = math.prod(s)
        out.append(flat[off:off + size].reshape(s))
        off += size
    return out


def kernel(x, norm_mix_g, w_in, a_re, a_im, log_step, b_re, b_im, c_re, c_im, d_skip, w_glu, sink, norm_attn_g, norm_ssm_g, w_out, norm_ffn_g, w_up, conv_w, conv_b, w_down, norm_final_g, loss_target, m_norm_mix_g, m_w_in, m_a_re, m_a_im, m_log_step, m_b_re, m_b_im, m_c_re, m_c_im, m_d_skip, m_w_glu, m_sink, m_norm_attn_g, m_norm_ssm_g, m_w_out, m_norm_ffn_g, m_w_up, m_conv_w, m_conv_b, m_w_down, m_norm_final_g, v_norm_mix_g, v_w_in, v_a_re, v_a_im, v_log_step, v_b_re, v_b_im, v_c_re, v_c_im, v_d_skip, v_w_glu, v_sink, v_norm_attn_g, v_norm_ssm_g, v_w_out, v_norm_ffn_g, v_w_up, v_conv_w, v_conv_b, v_w_down, v_norm_final_g):
    args = dict(locals())
    names = ["norm_mix_g", "w_in", "a_re", "a_im", "log_step", "b_re", "b_im", "c_re", "c_im", "d_skip", "w_glu",
             "sink", "norm_attn_g", "norm_ssm_g", "w_out", "norm_ffn_g", "w_up", "conv_w", "conv_b", "w_down",
             "norm_final_g"]
    w = {k: args[k] for k in names}
    m = {k: args["m_" + k] for k in names}
    v = {k: args["v_" + k] for k in names}

    (w_in_t,) = _all_gather([w_in[0].T.astype(BF16)], "gather_w_in")
    later = dict(w_glu=w_glu[0].astype(BF16), w_out=w_out[0].astype(BF16), w_up_t=w_up[0].T.astype(BF16),
                 w_down=w_down[0].astype(BF16), conv_w=jnp.pad(conv_w[0], ((0, 5), (0, 0))))

    ax, ay, ac = _coords()
    me = 4 * ax + 2 * ay + ac
    where = jnp.stack([ac, 2 * ax + ay]).astype(jnp.int32)
    parts, own, got = {}, {}, {}

    def split8(g):
        return g.reshape(NDEV, g.shape[0] // NDEV, g.shape[1])

    def attend(qkv, sink_):
        attn, lse, gathered = _attn_fwd(qkv, sink_, gather=list(later.values()))
        wts = dict(zip(later.keys(), gathered))
        wts["conv_w"] = (wts["conv_w"].reshape(NDEV, 8, 2 * DFF // NDEV)[:, :3].transpose(1, 0, 2)
                         .reshape(3, 2 * DFF))
        return attn, lse, wts

    def pair_sum(k, from_core):
        per_chip, own[k] = _pair_sum(where, parts[k], from_core, F32 if k == "small" else BF16, "pair_sum_" + k)
        return per_chip

    def stage(k, phase, payload):
        if phase == "cores":
            parts[k] = split8(payload)
            return ("cores", [parts[k]])
        if phase == "chips":
            return ("chips", [pair_sum(k, payload[0])])
        got[k] = payload[0]
        return ()

    p = {k: w[k][0] for k in ("norm_mix_g", "a_re", "a_im", "log_step", "c_re", "c_im", "d_skip", "sink",
                              "norm_attn_g", "norm_ssm_g", "norm_ffn_g", "conv_b")}
    p["norm_final_g"] = norm_final_g
    p["bt_re"], p["bt_im"] = _swap(b_re[0]), _swap(b_im[0])
    grad_x, big, small = _local_step(x[0], loss_target[0], w_in_t, p, attend, stage)

    small_names = list(small.keys())
    small_shapes = [small[k].shape for k in small_names]
    n_small = sum(math.prod(s) for s in small_shapes)
    rows_dev = -(-n_small // (PACK_W * NDEV * 16)) * 16
    spack = _pack([small[k] for k in small_names], rows_dev * NDEV)
    late = ["w_in_t", "w_glu", "w_out", "small"]
    parts.update({k: split8(big[k]) for k in late[:-1]}, small=spack.reshape(NDEV, rows_dev, PACK_W))
    from_cores = _exchange("cores", [parts[k] for k in late], "exchange_cores")
    from_chips = _exchange("chips", [pair_sum(k, fc) for k, fc in zip(late, from_cores)], "exchange_chips")
    got.update(zip(late, from_chips))
    (small_full,) = _all_gather([_chip_sum(own["small"], got["small"], "chip_sum_small")], "gather_small")
    sm = dict(zip(small_names, _unpack(small_full, small_shapes)))

    _, s5_vjp = jax.vjp(_s5_params, a_re[0], a_im[0], log_step[0], p["bt_re"], p["bt_im"])
    da_re, da_im, dlog_step, dbt_re, dbt_im = s5_vjp((sm["lam_re"], sm["lam_im"], sm["bb_re"], sm["bb_im"]))
    gview = {
        "norm_mix_g": sm["norm_mix_g"], "norm_attn_g": sm["norm_attn_g"], "norm_ssm_g": sm["norm_ssm_g"],
        "norm_ffn_g": sm["norm_ffn_g"], "norm_final_g": sm["norm_final_g"], "conv_b": sm["conv_b"],
        "sink": sm["sink"][None], "a_re": da_re.reshape(2 * NG, NP), "a_im": da_im.reshape(2 * NG, NP),
        "log_step": dlog_step, "b_re": dbt_re.reshape(2 * NG * GC, NP), "b_im": dbt_im.reshape(2 * NG * GC, NP),
        "c_re": sm["c_re"].reshape(2 * NG * GC, NP), "c_im": sm["c_im"].reshape(2 * NG * GC, NP),
        "d_skip": sm["d_skip"].reshape(NG, GC).T,
        "conv_w": lax.dynamic_slice_in_dim(sm["conv_w"], me * (2 * DFF // NDEV), 2 * DFF // NDEV, axis=1),
    }

    dview, mview, vview = {}, {}, {}
    for k, kg in zip(BIG, ("w_in_t", "w_glu", "w_out", "w_up_t", "w_down")):
        to = VIEWS[k][0]
        gview[k], dview[k], mview[k], vview[k] = _adamw(to(w[k]), own[kg], got[kg], to(m[k]), to(v[k]), "adamw_" + k)
    rest = [k for k in names if k not in BIG]
    outs = _adamw_small([VIEWS[k][0](w[k]) for k in rest], [gview[k] for k in rest],
                        [VIEWS[k][0](m[k]) for k in rest], [VIEWS[k][0](v[k]) for k in rest], "adamw_small")
    for dst, vals in zip((dview, mview, vview), outs):
        dst.update(dict(zip(rest, vals)))

    def back(views):
        return [VIEWS[k][1](views[k]) for k in names]

    return (sm["loss"][0], grad_x[None], *back(gview), *back(dview), *back(mview), *back(vview))
```

```python
import functools
import math

import jax
import jax.numpy as jnp
from jax import lax
from jax.experimental import pallas as pl
from jax.experimental.pallas import tpu as pltpu

F32 = jnp.float32
BF16 = jnp.bfloat16

L = 4096
D = 1024
NQ, NKV, HD = 8, 2, 64
AW = NQ * HD
KVW = NKV * HD
SW = 512
NG, GC, NP = 32, 16, 64
INW = AW + 2 * KVW + SW
DFF = 2816
BLK = 128
WIN = 3 * BLK
EPS = 1e-6
ROPE_THETA = 500000.0
NSEG = 32
TSEG = L // NSEG
SBW = 256
NSB = NG * NP // SBW
NDEV = 8
MESH_AXES = ("x", "y", "c")

LR, B1, B2, AEPS, WD, STEP = 0.001, 0.9, 0.999, 1e-08, 0.01, 10
C1 = 1.0 - B1 ** STEP
C2 = 1.0 - B2 ** STEP

VMEM_LIMIT = 56 * 1024 * 1024


def _pick(n, target, mult):
    best = None
    for t in range(mult, min(n, target) + 1, mult):
        if n % t == 0:
            best = t
    return best if best is not None else n


def _cp(sem):
    return pltpu.CompilerParams(dimension_semantics=sem, vmem_limit_bytes=VMEM_LIMIT)


def _mm(a, b, *, ta=False, tb=False, out_dtype=F32, add=None, ride=(), post=None, name, tm=1024, tn=1024, tk=1024):
    m, k = (a.shape[1], a.shape[0]) if ta else a.shape
    n = b.shape[0] if tb else b.shape[1]
    assert k == (b.shape[1] if tb else b.shape[0])
    tm, tn, tk = _pick(m, tm, 128), _pick(n, tn, 128), _pick(k, tk, 128)
    grid = (m // tm, n // tn, k // tk)
    nk = grid[2]
    dn = (((0 if ta else 1,), (1 if tb else 0,)), ((), ()))
    n_in = 2 + (add is not None)
    kind, riding = ride if ride else (None, ())
    nr = len(riding)
    post_fn, post_ins, post_outs = post if post is not None else (None, (), ())
    n_pi = len(post_ins)
    n_out = len(post_outs) if post is not None else 1
    assert post is None or grid[1] == 1

    def body(*refs):
        a_ref, b_ref = refs[0], refs[1]
        pin = refs[n_in:n_in + n_pi]
        base = n_in + n_pi + nr
        o_refs = refs[base:base + n_out]
        acc_ref = refs[base + n_out + nr]
        step = [pl.program_id(d) for d in range(3)]
        kk = step[2]
        if nr:
            riders = (refs[n_in + n_pi:base], refs[base + n_out:base + n_out + nr], *refs[base + n_out + nr + 1:])

            @pl.when((step[0] == 0) & (step[1] == 0) & (kk == 0))
            def _():
                _start_all(_exchange_copies(kind, *riders))

        prod = lax.dot_general(a_ref[...].astype(BF16), b_ref[...].astype(BF16), dn, preferred_element_type=F32)

        def finish(r):
            if add is not None:
                r = r + refs[2][...]
            if post_fn is None:
                o_refs[0][...] = r.astype(out_dtype)
            else:
                post_fn(r, step[0], pin, o_refs)

        if nk == 1:
            finish(prod)
        else:
            @pl.when(kk == 0)
            def _():
                acc_ref[...] = prod

            @pl.when((kk > 0) & (kk < nk - 1))
            def _():
                acc_ref[...] += prod

            @pl.when(kk == nk - 1)
            def _():
                finish(acc_ref[...] + prod)

        if nr:
            @pl.when((step[0] == grid[0] - 1) & (step[1] == grid[1] - 1) & (kk == nk - 1))
            def _():
                _wait_all(_exchange_copies(kind, *riders))

    a_spec = pl.BlockSpec((tk, tm), lambda i, j, kk: (kk, i)) if ta else pl.BlockSpec((tm, tk), lambda i, j, kk: (i, kk))
    b_spec = pl.BlockSpec((tn, tk), lambda i, j, kk: (j, kk)) if tb else pl.BlockSpec((tk, tn), lambda i, j, kk: (kk, j))
    def row_spec(shape):
        return pl.BlockSpec((tm if shape[0] == m else shape[0], shape[1]),
                            (lambda i, j, kk: (i, 0)) if shape[0] == m else (lambda i, j, kk: (0, 0)))

    in_specs = [a_spec, b_spec]
    args = [a, b]
    if add is not None:
        in_specs.append(pl.BlockSpec((tm, tn), lambda i, j, kk: (i, j)))
        args.append(add)
    if post is None:
        main_specs = [pl.BlockSpec((tm, tn), lambda i, j, kk: (i, j))]
        main_shapes = [jax.ShapeDtypeStruct((m, n), out_dtype)]
    else:
        main_specs = [row_spec(s.shape) for s in post_outs]
        main_shapes = list(post_outs)
    outs = pl.pallas_call(
        body, name=name, grid=grid,
        in_specs=in_specs + [row_spec(p.shape) for p in post_ins] + [ANY] * nr,
        out_specs=main_specs + [ANY] * nr,
        out_shape=main_shapes + (_exchange_shapes(kind, riding) if nr else []),
        scratch_shapes=[pltpu.VMEM((tm, tn) if nk > 1 else (8, 128), F32)] + (_exchange_sems(kind, nr) if nr else []),
        compiler_params=_cp(("arbitrary",) * 3 if (nr or post is not None) else ("parallel", "parallel", "arbitrary")),
    )(*args, *post_ins, *riding)
    main = outs[0] if post is None else list(outs[:n_out])
    return (main, list(outs[n_out:])) if nr else main


TL = 512


def _rms(xv, gv):
    return xv * lax.rsqrt(jnp.mean(xv * xv, axis=-1, keepdims=True) + EPS) * gv


def _rows(width):
    return pl.BlockSpec((TL, width), lambda i: (i, 0))


def _whole(shape):
    return pl.BlockSpec(shape, lambda i: (0,) * len(shape))


def _in_proj(x, g, w_in_t, tabs):
    qkw = AW + 2 * KVW

    def body(x_ref, g_ref, w_ref, c_ref, sa_ref, sb_ref, h_ref, qkv_ref, u_ref):
        h = _rms(x_ref[...], g_ref[...]).astype(BF16)
        h_ref[...] = h
        proj = lax.dot_general(h, w_ref[...], _NT, preferred_element_type=F32)
        for j in range(qkw // 128):
            cols = slice(j * 128, (j + 1) * 128)
            xv = proj[:, cols]
            if j < (AW + KVW) // 128:
                xv = _rope(xv, c_ref[...], sa_ref[...], sb_ref[...], 1.0)
            qkv_ref[:, cols] = xv.astype(BF16)
        u_ref[...] = proj[:, qkw:]

    return pl.pallas_call(
        body, name="in_proj", grid=(L // TL,),
        in_specs=[_rows(D), _whole((1, D)), _whole((INW, D)), _rows(128), _rows(128), _rows(128)],
        out_specs=[_rows(D), _rows(qkw), _rows(SW)],
        out_shape=[jax.ShapeDtypeStruct((L, D), BF16), jax.ShapeDtypeStruct((L, qkw), BF16),
                   jax.ShapeDtypeStruct((L, SW), F32)],
        compiler_params=_cp(("parallel",)),
    )(x, g, w_in_t, *tabs)


def _in_proj_dx(dproj, w_in_t, x, g, dres):
    def body(dp_ref, w_ref, x_ref, g_ref, dres_ref, dx_ref, dg_ref):
        dh = jnp.dot(dp_ref[...], w_ref[...], preferred_element_type=F32)
        dx, dg = _rms_bwd_tile(x_ref[...], g_ref[...], dh)
        dx_ref[...] = dx + dres_ref[...]

        @pl.when(pl.program_id(0) == 0)
        def _():
            dg_ref[...] = jnp.zeros_like(dg_ref)

        dg_ref[...] += dg

    return pl.pallas_call(
        body, name="in_proj_dx", grid=(L // TL,),
        in_specs=[_rows(INW), _whole((INW, D)), _rows(D), _whole((1, D)), _rows(D)],
        out_specs=[_rows(D), _whole((1, D))],
        out_shape=[jax.ShapeDtypeStruct((L, D), F32), jax.ShapeDtypeStruct((1, D), F32)],
        compiler_params=_cp(("arbitrary",)),
    )(dproj, w_in_t, x, g, dres)


def _rms_bwd_tile(xv, gv, dh):
    r = lax.rsqrt(jnp.mean(xv * xv, axis=-1, keepdims=True) + EPS)
    a = dh * gv
    dx = r * a - xv * (r * r * r) * jnp.mean(a * xv, axis=-1, keepdims=True)
    dg = jnp.sum(dh * xv * r, axis=0, keepdims=True)
    return dx, dg


def _rms_bwd_post(dh, i, ins, outs):
    x_ref, dres_ref, g_ref = ins
    dx_ref, dxb_ref, dg_ref = outs
    dx, dg = _rms_bwd_tile(x_ref[...], g_ref[...], dh)
    dx = dx + dres_ref[...]
    dx_ref[...] = dx
    dxb_ref[...] = dx.astype(BF16)

    @pl.when(i == 0)
    def _():
        dg_ref[...] = jnp.zeros_like(dg_ref)

    dg_ref[...] += dg


def _final_post(xv, i, ins, outs):
    t_ref, g_ref = ins
    loss_ref, dx_ref, dxb_ref, dg_ref = outs

    @pl.when(i == 0)
    def _():
        loss_ref[...] = jnp.zeros_like(loss_ref)
        dg_ref[...] = jnp.zeros_like(dg_ref)

    gv = g_ref[...]
    r = lax.rsqrt(jnp.mean(xv * xv, axis=-1, keepdims=True) + EPS)
    e = xv * r * gv - t_ref[...]
    loss_ref[...] += 0.5 * jnp.sum(jnp.mean(e * e, axis=-1, keepdims=True), axis=0, keepdims=True)
    dy = e * (1.0 / D)
    a = dy * gv
    dx = r * a - xv * (r * r * r) * jnp.mean(a * xv, axis=-1, keepdims=True)
    dx_ref[...] = dx
    dxb_ref[...] = dx.astype(BF16)
    dg_ref[...] += jnp.sum(dy * xv * r, axis=0, keepdims=True)


def _out_proj(attn, ysg, ga, gs, w_out, x, gf):
    def body(a_ref, s_ref, ga_ref, gs_ref, w_ref, x_ref, gf_ref, m_ref, x1_ref, h2_ref):
        m_ref[:, 0:AW] = _rms(a_ref[...], ga_ref[...]).astype(BF16)
        m_ref[:, AW:AW + SW] = _rms(s_ref[...], gs_ref[...]).astype(BF16)
        x1 = jnp.dot(m_ref[...], w_ref[...], preferred_element_type=F32) + x_ref[...]
        x1_ref[...] = x1
        h2_ref[...] = _rms(x1, gf_ref[...]).astype(BF16)

    return pl.pallas_call(
        body, name="out_proj", grid=(L // TL,),
        in_specs=[_rows(AW), _rows(SW), _whole((1, AW)), _whole((1, SW)), _whole((D, D)), _rows(D), _whole((1, D))],
        out_specs=[_rows(D), _rows(D), _rows(D)],
        out_shape=[jax.ShapeDtypeStruct((L, D), BF16), jax.ShapeDtypeStruct((L, D), F32),
                   jax.ShapeDtypeStruct((L, D), BF16)],
        compiler_params=_cp(("parallel",)),
    )(attn, ysg, ga, gs, w_out, x, gf)


def _out_proj_dx(dx1, w_out, attn, ysg, ga, gs):
    def body(dx_ref, w_ref, a_ref, s_ref, ga_ref, gs_ref, da_ref, ds_ref, dga_ref, dgs_ref):
        @pl.when(pl.program_id(0) == 0)
        def _():
            dga_ref[...] = jnp.zeros_like(dga_ref)
            dgs_ref[...] = jnp.zeros_like(dgs_ref)

        dm = lax.dot_general(dx_ref[...].astype(BF16), w_ref[...], _NT, preferred_element_type=F32)
        dxa, dga = _rms_bwd_tile(a_ref[...], ga_ref[...], dm[:, 0:AW])
        da_ref[...] = dxa
        dga_ref[...] += dga
        dxs, dgs = _rms_bwd_tile(s_ref[...], gs_ref[...], dm[:, AW:AW + SW])
        ds_ref[...] = dxs
        dgs_ref[...] += dgs

    return pl.pallas_call(
        body, name="out_proj_dx", grid=(L // TL,),
        in_specs=[_rows(D), _whole((D, D)), _rows(AW), _rows(SW), _whole((1, AW)), _whole((1, SW))],
        out_specs=[_rows(AW), _rows(SW), _whole((1, AW)), _whole((1, SW))],
        out_shape=[jax.ShapeDtypeStruct((L, AW), F32), jax.ShapeDtypeStruct((L, SW), F32),
                   jax.ShapeDtypeStruct((1, AW), F32), jax.ShapeDtypeStruct((1, SW), F32)],
        compiler_params=_cp(("arbitrary",)),
    )(dx1, w_out, attn, ysg, ga, gs)


def _rope_tables():
    half = HD // 8
    inv_freq = jnp.power(ROPE_THETA, -jnp.arange(half, dtype=F32) / half)
    ang = jnp.arange(L, dtype=F32)[:, None] * inv_freq[None, :]
    cos, sin = jnp.cos(ang), jnp.sin(ang)
    one = jnp.ones((L, HD - 2 * half), F32)
    zero = jnp.zeros((L, HD - 2 * half), F32)
    zh = jnp.zeros((L, half), F32)
    cos64 = jnp.concatenate([cos, cos, one], axis=1)
    sa64 = jnp.concatenate([-sin, zh, zero], axis=1)
    sb64 = jnp.concatenate([zh, sin, zero], axis=1)
    return [jnp.tile(t, (1, 2)) for t in (cos64, sa64, sb64)]


def _rope(xv, cosv, sav, sbv, sign):
    return xv * cosv + sign * (pltpu.roll(xv, 120, 1) * sav + pltpu.roll(xv, 8, 1) * sbv)


def _rope_bwd(dq, dk, dv, du, tabs):
    def body(dq_ref, dk_ref, dv_ref, du_ref, c_ref, sa_ref, sb_ref, o_ref):
        for j in range(AW // 128):
            cols = slice(j * 128, (j + 1) * 128)
            o_ref[:, cols] = _rope(dq_ref[:, cols], c_ref[...], sa_ref[...], sb_ref[...], -1.0).astype(BF16)
        o_ref[:, AW:AW + KVW] = _rope(dk_ref[...], c_ref[...], sa_ref[...], sb_ref[...], -1.0).astype(BF16)
        o_ref[:, AW + KVW:AW + 2 * KVW] = dv_ref[...].astype(BF16)
        o_ref[:, AW + 2 * KVW:] = du_ref[...].astype(BF16)

    def row(width):
        return pl.BlockSpec((TL, width), lambda i: (i, 0))

    return pl.pallas_call(
        body, name="rope_bwd", grid=(L // TL,),
        in_specs=[row(AW), row(KVW), row(KVW), row(SW), row(128), row(128), row(128)],
        out_specs=row(INW), out_shape=jax.ShapeDtypeStruct((L, INW), BF16),
        compiler_params=_cp(("parallel",)),
    )(dq, dk, dv, du, *tabs)


def _attn_window(n):
    start = pl.multiple_of(jnp.clip((n - 1) * BLK, 0, L - WIN), BLK)
    qpos = n * BLK + lax.broadcasted_iota(jnp.int32, (BLK, WIN), 0)
    kpos = start + lax.broadcasted_iota(jnp.int32, (BLK, WIN), 1)
    return start, jnp.abs(kpos - qpos) <= BLK


_NT = (((1,), (1,)), ((), ()))
_TN = (((0,), (0,)), ((), ()))
NEG = -1e30


def _attn_fwd(qkv, sink, gather=()):
    ng = len(gather)

    def body(sink_ref, q_ref, k_ref, v_ref, *rest):
        o_ref, lse_ref = rest[ng], rest[ng + 1]
        n = pl.program_id(0)
        if ng:
            travellers = (rest[:ng], rest[ng + 2:2 * ng + 2], *rest[2 * ng + 2:])

            @pl.when(n == 0)
            def _():
                _gather_start(*travellers)

            @pl.when(n == L // BLK - 1)
            def _():
                _gather_finish(*travellers)

        start, valid = _attn_window(n)
        kw = k_ref[pl.ds(start, WIN), :]
        vw = v_ref[pl.ds(start, WIN), :]
        for h in range(NQ):
            kv = h // (NQ // NKV)
            qh = q_ref[:, h * HD:(h + 1) * HD]
            kh = kw[:, kv * HD:(kv + 1) * HD]
            vh = vw[:, kv * HD:(kv + 1) * HD]
            s = lax.dot_general(qh, kh, _NT, preferred_element_type=F32) * (HD ** -0.5)
            s = jnp.where(valid, s, NEG)
            sk = sink_ref[h]
            m = jnp.maximum(jnp.max(s, axis=-1, keepdims=True), sk)
            p = jnp.exp(s - m)
            den = jnp.sum(p, axis=-1, keepdims=True) + jnp.exp(sk - m)
            o_ref[:, h * HD:(h + 1) * HD] = jnp.dot((p / den).astype(BF16), vh, preferred_element_type=F32)
            lse_ref[:, h:h + 1] = m + jnp.log(den)

    outs = pl.pallas_call(
        body, name="attn_fwd", grid=(L // BLK,),
        in_specs=[pl.BlockSpec(memory_space=pltpu.SMEM),
                  pl.BlockSpec((BLK, AW), lambda n: (n, 0)),
                  pl.BlockSpec((L, KVW), lambda n: (0, AW // KVW)),
                  pl.BlockSpec((L, KVW), lambda n: (0, AW // KVW + 1))] + [ANY] * ng,
        out_specs=[pl.BlockSpec((BLK, AW), lambda n: (n, 0)), pl.BlockSpec((BLK, NQ), lambda n: (n, 0))] + [ANY] * ng,
        out_shape=[jax.ShapeDtypeStruct((L, AW), F32), jax.ShapeDtypeStruct((L, NQ), F32)] + _gather_shapes(gather),
        scratch_shapes=_gather_sems(ng) if ng else [],
        compiler_params=_cp(("arbitrary",) if ng else ("parallel",)),
    )(sink, qkv, qkv, qkv, *gather)
    return outs[0], outs[1], list(outs[2:])


def _attn_bwd(qkv, sink, attn, lse, dattn, ride=()):
    kind, riding = ride if ride else (None, ())
    nr = len(riding)

    def body(*refs):
        if nr:
            riders = (refs[7:7 + nr], refs[11 + nr:11 + 2 * nr], *refs[11 + 2 * nr:])

            @pl.when(pl.program_id(0) == 0)
            def _():
                _start_all(_exchange_copies(kind, *riders))

        compute(*refs[:7], *refs[7 + nr:11 + nr])
        if nr:
            @pl.when(pl.program_id(0) == L // BLK - 1)
            def _():
                _wait_all(_exchange_copies(kind, *riders))

    def compute(sink_ref, q_ref, k_ref, v_ref, o_ref, lse_ref, do_ref, dq_ref, dk_ref, dv_ref, dsink_ref):
        n = pl.program_id(0)

        @pl.when(n == 0)
        def _():
            dk_ref[...] = jnp.zeros_like(dk_ref)
            dv_ref[...] = jnp.zeros_like(dv_ref)
            dsink_ref[...] = jnp.zeros_like(dsink_ref)

        start, valid = _attn_window(n)
        kw = k_ref[pl.ds(start, WIN), :]
        vw = v_ref[pl.ds(start, WIN), :]
        for kv in range(NKV):
            kh = kw[:, kv * HD:(kv + 1) * HD]
            vh = vw[:, kv * HD:(kv + 1) * HD]
            dk_acc = jnp.zeros((WIN, HD), F32)
            dv_acc = jnp.zeros((WIN, HD), F32)
            for h in range(kv * (NQ // NKV), (kv + 1) * (NQ // NKV)):
                qh = q_ref[:, h * HD:(h + 1) * HD]
                doh = do_ref[:, h * HD:(h + 1) * HD]
                dd = jnp.sum(doh * o_ref[:, h * HD:(h + 1) * HD], axis=-1, keepdims=True)
                lse_h = lse_ref[:, h:h + 1]
                s = lax.dot_general(qh, kh, _NT, preferred_element_type=F32) * (HD ** -0.5)
                p = jnp.where(valid, jnp.exp(s - lse_h), 0.0)
                dob = doh.astype(BF16)
                dp = lax.dot_general(dob, vh, _NT, preferred_element_type=F32)
                ds = (p * (dp - dd) * (HD ** -0.5)).astype(BF16)
                dq_ref[:, h * HD:(h + 1) * HD] = jnp.dot(ds, kh, preferred_element_type=F32)
                dk_acc += lax.dot_general(ds, qh, _TN, preferred_element_type=F32)
                dv_acc += lax.dot_general(p.astype(BF16), dob, _TN, preferred_element_type=F32)
                psink = jnp.exp(sink_ref[h] - lse_h)
                dsk = -jnp.sum(psink * dd, axis=0, keepdims=True)
                dsink_ref[h:h + 1, :] += jnp.broadcast_to(dsk, (1, 128))
            dk_ref[pl.ds(start, WIN), kv * HD:(kv + 1) * HD] += dk_acc
            dv_ref[pl.ds(start, WIN), kv * HD:(kv + 1) * HD] += dv_acc

    qblk = pl.BlockSpec((BLK, AW), lambda n: (n, 0))
    full = pl.BlockSpec((L, KVW), lambda n: (0, 0))
    outs = pl.pallas_call(
        body, name="attn_bwd", grid=(L // BLK,),
        in_specs=[pl.BlockSpec(memory_space=pltpu.SMEM), qblk,
                  pl.BlockSpec((L, KVW), lambda n: (0, AW // KVW)),
                  pl.BlockSpec((L, KVW), lambda n: (0, AW // KVW + 1)),
                  qblk, pl.BlockSpec((BLK, NQ), lambda n: (n, 0)), qblk] + [ANY] * nr,
        out_specs=[qblk, full, full, pl.BlockSpec((NQ, 128), lambda n: (0, 0))] + [ANY] * nr,
        out_shape=[jax.ShapeDtypeStruct((L, AW), F32), jax.ShapeDtypeStruct((L, KVW), F32),
                   jax.ShapeDtypeStruct((L, KVW), F32), jax.ShapeDtypeStruct((NQ, 128), F32)]
        + (_exchange_shapes(kind, riding) if nr else []),
        scratch_shapes=_exchange_sems(kind, nr) if nr else [],
        compiler_params=_cp(("arbitrary",)),
    )(sink, qkv, qkv, qkv, attn, lse, dattn, *riding)
    return list(outs[:4]), list(outs[4:])


def _perm(a):
    return a.reshape(NSEG, TSEG, a.shape[1]).transpose(1, 0, 2).reshape(L, a.shape[1])


def _unperm(a):
    return a.reshape(TSEG, NSEG, a.shape[1]).transpose(1, 0, 2).reshape(L, a.shape[1])


def _cmul(ar, ai, br, bi):
    return ar * br - ai * bi, ar * bi + ai * br


def _scan_inplace(s_ref, lr, li, rev, visit=None, carried=(), out_ref=None):
    n = lr.shape[1]
    lr8 = jnp.broadcast_to(lr, (NSEG, n))
    li8 = jnp.broadcast_to(li, (NSEG, n))

    def rows(k):
        return pl.ds(pl.multiple_of(jnp.where(rev, TSEG - 1 - k, k) * NSEG, NSEG), NSEG)

    def step(k, c, store):
        sr, si = c
        rs = rows(k)
        pr, pi = _cmul(lr8, li8, sr, si)
        nr = pr + s_ref[rs, 0:n]
        ni = pi + s_ref[rs, n:2 * n]
        if store:
            dst = s_ref if out_ref is None else out_ref
            dst[rs, 0:n] = nr.astype(dst.dtype)
            dst[rs, n:2 * n] = ni.astype(dst.dtype)
        return nr, ni

    z = jnp.zeros((NSEG, n), F32)
    er, ei = lax.fori_loop(0, TSEG, functools.partial(step, store=False), (z, z))
    pr, pi = lr, li
    for _ in range(int(math.log2(TSEG))):
        pr, pi = _cmul(pr, pi, pr, pi)

    seg = lax.broadcasted_iota(jnp.int32, (NSEG, n), 0)

    def chain(order):
        cr = jnp.zeros((1, n), F32)
        ci = jnp.zeros((1, n), F32)
        outr = jnp.zeros((NSEG, n), F32)
        outi = jnp.zeros((NSEG, n), F32)
        for s in order:
            outr = jnp.where(seg == s, cr, outr)
            outi = jnp.where(seg == s, ci, outi)
            mr, mi = _cmul(pr, pi, cr, ci)
            cr, ci = mr + er[s:s + 1], mi + ei[s:s + 1]
        return outr, outi

    fr, fi = chain(range(NSEG))
    rr, ri = chain(range(NSEG - 1, -1, -1))
    cin_r = jnp.where(rev, rr, fr)
    cin_i = jnp.where(rev, ri, fi)
    if visit is None:
        lax.fori_loop(0, TSEG, functools.partial(step, store=True), (cin_r, cin_i))
        return cin_r, cin_i

    def visited(k, c):
        nr, ni = step(k, c[:2], True)
        return (nr, ni) + tuple(visit(k, nr, ni, c[2:]))

    fin = lax.fori_loop(0, TSEG - 1, visited, (cin_r, cin_i) + tuple(carried))
    last_r, last_i = step(TSEG - 1, fin[:2], True)
    return last_r, last_i, fin[2:]


S5_RC = 512


def _s5_specs():
    u_spec = pl.BlockSpec((L, 128), lambda cb, h, d: (0, cb))
    b_spec = pl.BlockSpec((None, None, 128, 2 * SBW), lambda cb, h, d: (d, cb * 2 + h, 0, 0))
    c_spec = pl.BlockSpec((None, None, 2 * SBW, 128), lambda cb, h, d: (d, cb * 2 + h, 0, 0))
    l_spec = pl.BlockSpec((None, None, 1, SBW), lambda cb, h, d: (d, cb * 2 + h, 0, 0))
    d_spec = pl.BlockSpec((1, 128), lambda cb, h, d: (0, cb))
    return u_spec, b_spec, c_spec, l_spec, d_spec


def _s5_fwd(u_p, bcat, ccat, lam_re, lam_im, dskip, gather=()):
    ng = len(gather)
    grid = (SW // 128, 2, 2)

    def body(*refs):
        if ng:
            travellers = (refs[6:6 + ng], refs[8 + ng:8 + 2 * ng], *refs[9 + 2 * ng:])
            step = [pl.program_id(d) for d in range(3)]

            @pl.when((step[0] == 0) & (step[1] == 0) & (step[2] == 0))
            def _():
                _gather_start(*travellers)

        compute(*refs[:6], *refs[6 + ng:8 + ng], refs[8 + 2 * ng])
        if ng:
            @pl.when((step[0] == grid[0] - 1) & (step[1] == grid[1] - 1) & (step[2] == grid[2] - 1))
            def _():
                _gather_finish(*travellers)

    def compute(u_ref, b_ref, c_ref, lr_ref, li_ref, d_ref, y_ref, sb_ref, s_scr):
        first = (pl.program_id(1) == 0) & (pl.program_id(2) == 0)

        def proj(i, _):
            rs = pl.ds(pl.multiple_of(i * S5_RC, S5_RC), S5_RC)
            s_scr[rs, :] = jnp.dot(u_ref[rs, :].astype(BF16), b_ref[...], preferred_element_type=F32)
            return 0

        lax.fori_loop(0, L // S5_RC, proj, 0)
        _scan_inplace(s_scr, lr_ref[...], li_ref[...], pl.program_id(2) == 1, out_ref=sb_ref)

        def out(i, _):
            rs = pl.ds(pl.multiple_of(i * S5_RC, S5_RC), S5_RC)
            yv = jnp.dot(sb_ref[rs, :], c_ref[...], preferred_element_type=F32)

            @pl.when(first)
            def _():
                y_ref[rs, :] = d_ref[...] * u_ref[rs, :] + yv

            @pl.when(jnp.logical_not(first))
            def _():
                y_ref[rs, :] += yv

            return 0

        lax.fori_loop(0, L // S5_RC, out, 0)

    u_spec, b_spec, c_spec, l_spec, d_spec = _s5_specs()
    outs = pl.pallas_call(
        body, name="s5_fwd", grid=grid,
        in_specs=[u_spec, b_spec, c_spec, l_spec, l_spec, d_spec] + [ANY] * ng,
        out_specs=[u_spec, _s5_state_spec()] + [ANY] * ng,
        out_shape=[jax.ShapeDtypeStruct((L, SW), F32), jax.ShapeDtypeStruct((2, NSB, L, 2 * SBW), BF16)]
        + _gather_shapes(gather),
        scratch_shapes=[pltpu.VMEM((L, 2 * SBW), F32)] + (_gather_sems(ng) if ng else []),
        compiler_params=_cp(("arbitrary",) * 3 if ng else ("parallel", "arbitrary", "arbitrary")),
    )(u_p, bcat, ccat, lam_re, lam_im, dskip, *gather)
    return outs[0], outs[1], list(outs[2:])


def _s5_state_spec():
    return pl.BlockSpec((None, None, L, 2 * SBW), lambda cb, h, d: (d, cb * 2 + h, 0, 0))


def _s5_bwd(u_p, dy_p, states, bcat, ccat, lam_re, lam_im, dskip, ride=()):
    kind, riding = ride if ride else (None, ())
    nr = len(riding)
    grid = (SW // 128, 2, 2)

    def body(*refs):
        work = refs[:8] + refs[8 + nr:14 + nr] + refs[14 + 2 * nr:16 + 2 * nr]
        if nr:
            riders = (refs[8:8 + nr], refs[14 + nr:14 + 2 * nr], *refs[16 + 2 * nr:])
            step = [pl.program_id(d) for d in range(3)]

            @pl.when((step[0] == 0) & (step[1] == 0) & (step[2] == 0))
            def _():
                _start_all(_exchange_copies(kind, *riders))

        compute(*work)
        if nr:
            @pl.when((step[0] == grid[0] - 1) & (step[1] == grid[1] - 1) & (step[2] == grid[2] - 1))
            def _():
                _wait_all(_exchange_copies(kind, *riders))

    def compute(u_ref, dy_ref, s_ref, b_ref, c_ref, lr_ref, li_ref, d_ref,
                du_ref, db_ref, dc_ref, dlr_ref, dli_ref, dd_ref, g_scr, gb_scr):
        first = (pl.program_id(1) == 0) & (pl.program_id(2) == 0)
        rev = pl.program_id(2) == 1

        def dstate(i, _):
            rs = pl.ds(pl.multiple_of(i * S5_RC, S5_RC), S5_RC)
            g_scr[rs, :] = lax.dot_general(dy_ref[rs, :].astype(BF16), c_ref[...], _NT, preferred_element_type=F32)
            return 0

        lax.fori_loop(0, L // S5_RC, dstate, 0)

        def before(rows):
            sv = s_ref[rows, :].astype(F32)
            return sv[:, 0:SBW], sv[:, SBW:2 * SBW]

        def dlam(gr, gi, sr, si, ar, ai):
            return ar + gr * sr + gi * si, ai + gi * sr - gr * si

        def visit(k, gr, gi, acc):
            ts = jnp.where(rev, k + 1, TSEG - 2 - k)
            sr, si = before(pl.ds(pl.multiple_of(ts * NSEG, NSEG), NSEG))
            return dlam(gr, gi, sr, si, *acc)

        z = jnp.zeros((NSEG, SBW), F32)
        gr, gi, acc = _scan_inplace(g_scr, lr_ref[...], -li_ref[...], jnp.logical_not(rev), visit, (z, z), gb_scr)
        edge_r, edge_i = before(pl.ds(pl.multiple_of(jnp.where(rev, 0, TSEG - 1) * NSEG, NSEG), NSEG))
        seg = lax.broadcasted_iota(jnp.int32, (NSEG, SBW), 0)
        keep = seg != jnp.where(rev, NSEG - 1, 0)

        def neighbour(e):
            return jnp.where(keep, jnp.where(rev, pltpu.roll(e, NSEG - 1, 0), pltpu.roll(e, 1, 0)), 0.0)

        ar, ai = dlam(gr, gi, neighbour(edge_r), neighbour(edge_i), *acc)
        dlr_ref[...] = jnp.sum(ar, axis=0, keepdims=True)
        dli_ref[...] = jnp.sum(ai, axis=0, keepdims=True)

        db_ref[...] = jnp.zeros_like(db_ref)
        dc_ref[...] = jnp.zeros_like(dc_ref)

        @pl.when(first)
        def _():
            dd_ref[...] = jnp.zeros_like(dd_ref)

        def grads(i, _):
            rs = pl.ds(pl.multiple_of(i * S5_RC, S5_RC), S5_RC)
            uv = u_ref[rs, :]
            dyv = dy_ref[rs, :]
            gb = gb_scr[rs, :]
            db_ref[...] += lax.dot_general(uv.astype(BF16), gb, _TN, preferred_element_type=F32)
            dc_ref[...] += lax.dot_general(dyv.astype(BF16), s_ref[rs, :], _TN, preferred_element_type=F32)
            duv = lax.dot_general(gb, b_ref[...], _NT, preferred_element_type=F32)

            @pl.when(first)
            def _():
                du_ref[rs, :] = d_ref[...] * dyv + duv
                dd_ref[...] += jnp.sum(dyv * uv, axis=0, keepdims=True)

            @pl.when(jnp.logical_not(first))
            def _():
                du_ref[rs, :] += duv

            return 0

        lax.fori_loop(0, L // S5_RC, grads, 0)

    u_spec, b_spec, c_spec, l_spec, d_spec = _s5_specs()
    outs = pl.pallas_call(
        body, name="s5_bwd", grid=grid,
        in_specs=[u_spec, u_spec, _s5_state_spec(), b_spec, c_spec, l_spec, l_spec, d_spec] + [ANY] * nr,
        out_specs=[u_spec, b_spec, b_spec, l_spec, l_spec, d_spec] + [ANY] * nr,
        out_shape=[jax.ShapeDtypeStruct((L, SW), F32),
                   jax.ShapeDtypeStruct((2, NSB, 128, 2 * SBW), F32), jax.ShapeDtypeStruct((2, NSB, 128, 2 * SBW), F32),
                   jax.ShapeDtypeStruct((2, NSB, 1, SBW), F32), jax.ShapeDtypeStruct((2, NSB, 1, SBW), F32),
                   jax.ShapeDtypeStruct((1, SW), F32)] + (_exchange_shapes(kind, riding) if nr else []),
        scratch_shapes=[pltpu.VMEM((L, 2 * SBW), F32), pltpu.VMEM((L, 2 * SBW), BF16)]
        + (_exchange_sems(kind, nr) if nr else []),
        compiler_params=_cp(("arbitrary",) * 3 if nr else ("parallel", "arbitrary", "arbitrary")),
    )(u_p, dy_p, states, bcat, ccat, lam_re, lam_im, dskip, *riding)
    return list(outs[:6]), list(outs[6:])


def _s5_params(a_re, a_im, log_step, bt_re, bt_im):
    lam = lax.complex(a_re, a_im)
    step = jnp.exp(log_step)[..., None]
    lam_bar = jnp.exp(lam * step)
    b_bar = ((lam_bar - 1.0) / lam)[..., None, :] * lax.complex(bt_re, bt_im)
    return jnp.real(lam_bar), jnp.imag(lam_bar), jnp.real(b_bar), jnp.imag(b_bar)


def _sel():
    i = jnp.arange(8)[None, :, None]
    j = jnp.arange(4)[None, None, :]
    r = jnp.arange(2)[:, None, None]
    return (i == r * 4 + j).astype(F32)


def _to_bcat(bt_re, bt_im):
    def one(bt):
        return jnp.einsum('dkrjcp,rij->dkricjp', bt.reshape(2, 4, 2, 4, GC, NP), _sel()).reshape(2, NSB, 128, SBW)
    return jnp.concatenate([one(bt_re), one(bt_im)], axis=-1)


def _from_bcat(dbcat):
    def one(dbbd):
        return jnp.einsum('dkricjp,rij->dkrjcp', dbbd.reshape(2, 4, 2, 8, GC, 4, NP), _sel()).reshape(2, NG, GC, NP)
    return one(dbcat[..., :SBW]), one(dbcat[..., SBW:])


def _to_ccat(c_re, c_im):
    def one(cc):
        return jnp.einsum('dkrjcp,rij->dkrjpic', cc.reshape(2, 4, 2, 4, GC, NP), _sel()).reshape(2, NSB, SBW, 128)
    return jnp.concatenate([one(c_re), -one(c_im)], axis=-2)


def _from_ccat(dccat):
    def one(dcbd):
        return jnp.einsum('dkrjpic,rij->dkrjcp', dcbd.reshape(2, 4, 2, 4, NP, 8, GC), _sel()).reshape(2, NG, GC, NP)
    return one(dccat[:, :, :SBW]), -one(dccat[:, :, SBW:])


def _gelu(y):
    return 0.5 * y * (1.0 + lax.erf(y * (2.0 ** -0.5)))


def _gelu_grad(y):
    return 0.5 * (1.0 + lax.erf(y * (2.0 ** -0.5))) + y * jnp.exp(-0.5 * y * y) * ((2.0 * math.pi) ** -0.5)


def _sigmoid(z):
    return 0.5 * jnp.tanh(0.5 * z) + 0.5


def _glu_fwd(y, wg):
    def body(y_ref, w_ref, o_ref, z_ref):
        ys = _gelu(y_ref[...])
        z = jnp.dot(ys.astype(BF16), w_ref[...], preferred_element_type=F32)
        z_ref[...] = z
        o_ref[...] = ys * _sigmoid(z)

    row = pl.BlockSpec((TL, SW), lambda i: (i, 0))
    return pl.pallas_call(
        body, name="glu_fwd", grid=(L // TL,),
        in_specs=[row, pl.BlockSpec((SW, SW), lambda i: (0, 0))], out_specs=[row, row],
        out_shape=[jax.ShapeDtypeStruct((L, SW), F32), jax.ShapeDtypeStruct((L, SW), F32)],
        compiler_params=_cp(("parallel",)),
    )(y, wg)


def _glu_bwd(y, z, dout, wg):
    def body(y_ref, z_ref, do_ref, w_ref, dy_ref, dw_ref):
        @pl.when(pl.program_id(0) == 0)
        def _():
            dw_ref[...] = jnp.zeros_like(dw_ref)

        yv = y_ref[...]
        ys = _gelu(yv)
        sg = _sigmoid(z_ref[...])
        dov = do_ref[...]
        dz = (dov * ys * sg * (1.0 - sg)).astype(BF16)
        dys = dov * sg + lax.dot_general(dz, w_ref[...], _NT, preferred_element_type=F32)
        dy_ref[...] = dys * _gelu_grad(yv)
        dw_ref[...] += lax.dot_general(ys.astype(BF16), dz, _TN, preferred_element_type=F32)

    row = pl.BlockSpec((TL, SW), lambda i: (i, 0))
    wsp = pl.BlockSpec((SW, SW), lambda i: (0, 0))
    return pl.pallas_call(
        body, name="glu_bwd", grid=(L // TL,),
        in_specs=[row, row, row, wsp], out_specs=[row, wsp],
        out_shape=[jax.ShapeDtypeStruct((L, SW), F32), jax.ShapeDtypeStruct((SW, SW), F32)],
        compiler_params=_cp(("arbitrary",)),
    )(y, z, dout, wg)


CT = 256
CR = 128
NCT = DFF // CT


def _shifted(ref, r):
    h = 8 * (4 // ref.dtype.itemsize)
    cur = ref[pl.ds(r, CR), :].astype(F32)
    before = ref[pl.ds(pl.multiple_of(jnp.maximum(r - h, 0), h), h), :][h - 1:h, :].astype(F32)
    after = ref[pl.ds(pl.multiple_of(jnp.minimum(r + CR, L - h), h), h), :][0:1, :].astype(F32)
    before = jnp.where(r > 0, before, 0.0)
    after = jnp.where(r + CR < L, after, 0.0)
    row = lax.broadcasted_iota(jnp.int32, cur.shape, 0)
    prev = jnp.where(row == 0, before, pltpu.roll(cur, 1, 0))
    nxt = jnp.where(row == CR - 1, after, pltpu.roll(cur, CR - 1, 0))
    return prev, cur, nxt


def _conv3(ref, r, w_ref, b_ref):
    prev, cur, nxt = _shifted(ref, r)
    return w_ref[0:1, :] * prev + w_ref[1:2, :] * cur + w_ref[2:3, :] * nxt + b_ref[...]


def _convact_fwd(up, conv_w, conv_b):
    def body(ug_ref, uv_ref, wg_ref, wv_ref, bg_ref, bv_ref, o_ref, g_ref, v_ref):
        def chunk(i, _):
            r = pl.multiple_of(i * CR, CR)
            rs = pl.ds(r, CR)
            g = _conv3(ug_ref, r, wg_ref, bg_ref)
            v = _conv3(uv_ref, r, wv_ref, bv_ref)
            o_ref[rs, :] = (g * _sigmoid(g) * v).astype(BF16)
            g_ref[rs, :] = g.astype(BF16)
            v_ref[rs, :] = v.astype(BF16)
            return 0

        lax.fori_loop(0, L // CR, chunk, 0)

    gcol = pl.BlockSpec((L, CT), lambda j: (0, j))
    vcol = pl.BlockSpec((L, CT), lambda j: (0, j + NCT))
    return pl.pallas_call(
        body, name="convact_fwd", grid=(NCT,),
        in_specs=[gcol, vcol,
                  pl.BlockSpec((3, CT), lambda j: (0, j)), pl.BlockSpec((3, CT), lambda j: (0, j + NCT)),
                  pl.BlockSpec((1, CT), lambda j: (0, j)), pl.BlockSpec((1, CT), lambda j: (0, j + NCT))],
        out_specs=[gcol, gcol, gcol], out_shape=[jax.ShapeDtypeStruct((L, DFF), BF16)] * 3,
        compiler_params=_cp(("parallel",)),
    )(up, up, conv_w, conv_w, conv_b, conv_b)


def _convact_bwd(up, gq, vq, dact, conv_w):
    def body(ug_ref, uv_ref, g_ref, v_ref, da_ref, wg_ref, wv_ref, du_ref, dw_ref, db_ref, dgs, dvs, dbv):
        half = pl.program_id(1)

        def transpose_conv(src, u_ref, w_ref):
            dw_ref[...] = jnp.zeros_like(dw_ref)

            def chunk(i, _):
                r = pl.multiple_of(i * CR, CR)
                rs = pl.ds(r, CR)
                prev, cur, nxt = _shifted(src, r)
                du_ref[rs, :] = (w_ref[0:1, :] * nxt + w_ref[1:2, :] * cur + w_ref[2:3, :] * prev).astype(BF16)
                uv = u_ref[rs, :].astype(F32)
                for k, d in enumerate((nxt, cur, prev)):
                    dw_ref[k:k + 1, :] += jnp.sum(d * uv, axis=0, keepdims=True)
                return 0

            lax.fori_loop(0, L // CR, chunk, 0)

        @pl.when(half == 0)
        def _():
            db_ref[...] = jnp.zeros_like(db_ref)
            dbv[...] = jnp.zeros_like(dbv)

            def chunk1(i, _):
                rs = pl.ds(pl.multiple_of(i * CR, CR), CR)
                g = g_ref[rs, :].astype(F32)
                v = v_ref[rs, :].astype(F32)
                sg = _sigmoid(g)
                da = da_ref[rs, :].astype(F32)
                dv = da * g * sg
                dg = da * v * sg * (1.0 + g * (1.0 - sg))
                dgs[rs, :] = dg
                dvs[rs, :] = dv
                db_ref[...] += jnp.sum(dg, axis=0, keepdims=True)
                dbv[0:1, :] += jnp.sum(dv, axis=0, keepdims=True)
                return 0

            lax.fori_loop(0, L // CR, chunk1, 0)
            transpose_conv(dgs, ug_ref, wg_ref)

        @pl.when(half == 1)
        def _():
            db_ref[...] = dbv[0:1, :]
            transpose_conv(dvs, uv_ref, wv_ref)

    def col(rows, off):
        return pl.BlockSpec((rows, CT), lambda j, h: (0, j + off))

    def out(rows):
        return pl.BlockSpec((rows, CT), lambda j, h: (0, j + h * NCT))

    return pl.pallas_call(
        body, name="convact_bwd", grid=(NCT, 2),
        in_specs=[col(L, 0), col(L, NCT), col(L, 0), col(L, 0), col(L, 0), col(3, 0), col(3, NCT)],
        out_specs=[out(L), out(3), out(1)],
        out_shape=[jax.ShapeDtypeStruct((L, 2 * DFF), BF16), jax.ShapeDtypeStruct((3, 2 * DFF), F32),
                   jax.ShapeDtypeStruct((1, 2 * DFF), F32)],
        scratch_shapes=[pltpu.VMEM((L, CT), F32), pltpu.VMEM((L, CT), F32), pltpu.VMEM((8, CT), F32)],
        compiler_params=_cp(("parallel", "arbitrary")),
    )(up, up, gq, vq, dact, conv_w, conv_w)


def _local_step(x, tgt, w_in_t, p, attend, scan, stage):
    tabs = _rope_tables()
    lam_re, lam_im, bb_re, bb_im = _s5_params(p["a_re"], p["a_im"], p["log_step"], p["bt_re"], p["bt_im"])
    bcat = _to_bcat(bb_re, bb_im).astype(BF16)
    ccat = _to_ccat(p["c_re"], p["c_im"]).astype(BF16)
    lam_re4, lam_im4 = lam_re.reshape(2, NSB, 1, SBW), lam_im.reshape(2, NSB, 1, SBW)
    dskip = p["d_skip"].reshape(1, SW)
    g_mix, g_ffn, g_fin = p["norm_mix_g"].reshape(1, D), p["norm_ffn_g"].reshape(1, D), p["norm_final_g"].reshape(1, D)
    g_attn, g_ssm = p["norm_attn_g"].reshape(1, AW), p["norm_ssm_g"].reshape(1, SW)
    sink = p["sink"].reshape(NQ)
    conv_b = p["conv_b"].reshape(1, 2 * DFF)

    rows, gain = jax.ShapeDtypeStruct((L, D), F32), jax.ShapeDtypeStruct((1, D), F32)
    rows16 = jax.ShapeDtypeStruct((L, D), BF16)
    h1, qkv, u = _in_proj(x, g_mix, w_in_t, tabs)
    attn, lse, wts = attend(qkv, sink)
    u_p = _perm(u)
    y_p, states, more = scan(u_p, bcat, ccat, lam_re4, lam_im4, dskip)
    wts = dict(wts, **more)
    w_glu, w_out, w_up_t, w_down, conv_w = (wts[k] for k in ("w_glu", "w_out", "w_up_t", "w_down", "conv_w"))
    ysg_p, z_p = _glu_fwd(y_p, w_glu)
    ysg = _unperm(ysg_p)
    mixed, x1, h2 = _out_proj(attn, ysg, g_attn, g_ssm, w_out, x, g_ffn)
    up = _mm(h2, w_up_t, tb=True, name="ffn_up", tn=1408, out_dtype=BF16)
    act, gq, vq = _convact_fwd(up, conv_w, conv_b)
    loss, dx2, dx2b, dg_fin = _mm(
        act, w_down, add=x1, name="ffn_down", tm=512, tk=DFF,
        post=(_final_post, [tgt, g_fin], [jax.ShapeDtypeStruct((1, 1), F32), rows, rows16, gain]))

    def riding(res, ride):
        return res if ride else (res, None)

    dw_down = _mm(act, dx2b, ta=True, name="ffn_down_dw", tm=256, tk=L)
    ride = stage(("w_down",), "cores", [dw_down])
    dact, got = riding(_mm(dx2b, w_down, tb=True, name="ffn_down_dx", tn=1408, out_dtype=BF16, ride=ride), ride)
    ride = stage(("w_down",), "chips", got)
    dup, dconv_w, dconv_b = _convact_bwd(up, gq, vq, dact, conv_w)
    dw_up_t, got = riding(_mm(dup, h2, ta=True, name="ffn_up_dw", tm=512, tk=L, ride=ride), ride)
    stage(("w_down",), "done", got)
    ride = stage(("w_up_t",), "cores", [dw_up_t])
    (dx1, dx1b, dg_ffn), got = riding(
        _mm(dup, w_up_t, name="ffn_up_dx", tm=512, tk=2 * DFF, ride=ride,
            post=(_rms_bwd_post, [x1, dx2, g_ffn], [rows, rows16, gain])), ride)
    ride = stage(("w_up_t",), "chips", got)
    dattn, dysg, dg_attn, dg_ssm = _out_proj_dx(dx1b, w_out, attn, ysg, g_attn, g_ssm)
    dw_out = _mm(mixed, dx1b, ta=True, name="out_proj_dw", tm=512, tk=L)
    dy_p, dw_glu = _glu_bwd(y_p, z_p, _perm(dysg), w_glu)
    (du_p, dbcat, dccat, dlam_re, dlam_im, dd), got = _s5_bwd(u_p, dy_p, states, bcat, ccat, lam_re4, lam_im4, dskip,
                                                              ride=ride)
    stage(("w_up_t",), "done", got)
    dbb_re, dbb_im = _from_bcat(dbcat)
    dc_re, dc_im = _from_ccat(_swap(dccat))
    mix = ("w_out", "w_glu")
    ride = stage(mix, "cores", [dw_out, dw_glu])
    (dq, dk, dv, dsink), got = _attn_bwd(qkv, sink, attn, lse, dattn, ride=ride)
    ride = stage(mix, "chips", got)
    dproj = _rope_bwd(dq, dk, dv, _unperm(du_p), tabs)
    dw_in_t, got = riding(_mm(dproj, h1, ta=True, name="in_proj_dw", tm=640, tk=L, ride=ride), ride)
    stage(mix, "done", got)
    grad_x, dg_mix = _in_proj_dx(dproj, w_in_t, x, g_mix, dx1)

    big = dict(w_in_t=dw_in_t)
    small = dict(norm_mix_g=dg_mix, norm_attn_g=dg_attn, norm_ssm_g=dg_ssm, norm_ffn_g=dg_ffn, norm_final_g=dg_fin,
                 sink=dsink[:, 0], conv_b=dconv_b, d_skip=dd, conv_w=dconv_w,
                 lam_re=dlam_re.reshape(2, NG, NP), lam_im=dlam_im.reshape(2, NG, NP),
                 bb_re=dbb_re, bb_im=dbb_im, c_re=dc_re, c_im=dc_im, loss=loss.reshape(1))
    return grad_x, big, small


ANY = pl.BlockSpec(memory_space=pl.ANY)


def _coords():
    return lax.axis_index("x"), lax.axis_index("y"), lax.axis_index("c")


def _flip(v, b):
    return v + b - 2 * v * b if b else v


def _all_gather(shards, name):
    n = len(shards)

    def body(*refs):
        _gather_start(refs[:n], refs[n:2 * n], *refs[2 * n:])
        _gather_finish(refs[:n], refs[n:2 * n], *refs[2 * n:])

    return pl.pallas_call(
        body, name=name,
        in_specs=[ANY] * n, out_specs=[ANY] * n,
        out_shape=_gather_shapes(shards), scratch_shapes=_gather_sems(n),
    )(*shards)


def _gather_shapes(shards):
    return [jax.ShapeDtypeStruct((NDEV * s.shape[0], s.shape[1]), s.dtype) for s in shards]


def _gather_sems(n):
    return [pltpu.SemaphoreType.DMA((7 * n,)), pltpu.SemaphoreType.DMA((7 * n,)), pltpu.SemaphoreType.DMA((n,))]


def _gather_copies(ins, outs, send_sems, recv_sems, local_sems, a):
    x, y, c = _coords()
    me, sibling = (x, y, c), (x, y, 1 - c)
    chips = [(1 - x, y), (x, 1 - y), (1 - x, 1 - y)]
    r = ins[a].shape[0]

    def rows(px, py, pc):
        return outs[a].at[pl.ds(pl.multiple_of((4 * px + 2 * py + pc) * r, 8), r), :]

    def copy(k, block, to, src=None):
        return pltpu.make_async_remote_copy(
            src_ref=rows(*block) if src is None else src, dst_ref=rows(*block),
            send_sem=send_sems.at[a * 7 + k], recv_sem=recv_sems.at[a * 7 + k],
            device_id=to, device_id_type=pl.DeviceIdType.MESH)

    mine = pltpu.make_async_copy(ins[a], rows(*me), local_sems.at[a])
    first = [copy(0, me, sibling, src=ins[a])]
    first += [copy(1 + j, me, (*chip, c), src=ins[a]) for j, chip in enumerate(chips)]
    passed = [copy(4 + j, (*chip, c), sibling) for j, chip in enumerate(chips)]
    arrivals = [copy(1 + j, (*chip, c), me) for j, chip in enumerate(chips)]
    from_sibling = [copy(0, sibling, me)] + [copy(4 + j, (*chip, 1 - c), me) for j, chip in enumerate(chips)]
    return mine, first, passed, arrivals, from_sibling


def _gather_start(ins, outs, send_sems, recv_sems, local_sems):
    for a in range(len(ins)):
        mine, first, _, _, _ = _gather_copies(ins, outs, send_sems, recv_sems, local_sems, a)
        mine.start()
        for cp in first:
            cp.start()


def _gather_finish(ins, outs, send_sems, recv_sems, local_sems):
    n = len(ins)
    parts = [_gather_copies(ins, outs, send_sems, recv_sems, local_sems, a) for a in range(n)]
    for mine, first, passed, arrivals, from_sibling in parts:
        for arrived, onward in zip(arrivals, passed):
            arrived.wait_recv()
            onward.start()
    for mine, first, passed, arrivals, from_sibling in parts:
        for cp in from_sibling:
            cp.wait_recv()
        for cp in first + passed:
            cp.wait_send()
        mine.wait()


NCHIP = 4
CHIP_FLIPS = ((1, 0), (0, 1), (1, 1))


def _planned_copies(ins, outs, send_sems, recv_sems, plan):
    return [pltpu.make_async_remote_copy(
        src_ref=ins[a].at[src], dst_ref=outs[a].at[dst], send_sem=send_sems.at[k], recv_sem=recv_sems.at[k],
        device_id=to, device_id_type=pl.DeviceIdType.MESH) for k, (a, src, dst, to) in enumerate(plan)]


def _start_all(copies):
    for cp in copies:
        cp.start()


def _wait_all(copies):
    for cp in copies:
        cp.wait_recv()
    for cp in copies:
        cp.wait_send()


SLOTS = {"cores": NCHIP, "chips": 3}


def _exchange_copies(kind, ins, outs, send_sems, recv_sems):
    x, y, c = _coords()
    plan = []
    for a in range(len(ins)):
        if kind == "cores":
            plan += [(a, 2 * q + 1 - c, q, (x, y, 1 - c)) for q in range(NCHIP)]
        else:
            for j, (fx, fy) in enumerate(CHIP_FLIPS):
                px, py = _flip(x, fx), _flip(y, fy)
                plan.append((a, 2 * px + py, j, (px, py, c)))
    return _planned_copies(ins, outs, send_sems, recv_sems, plan)


def _exchange_shapes(kind, parts):
    return [jax.ShapeDtypeStruct((SLOTS[kind],) + s.shape[1:], s.dtype) for s in parts]


def _exchange_sems(kind, n):
    return [pltpu.SemaphoreType.DMA((SLOTS[kind] * n,)), pltpu.SemaphoreType.DMA((SLOTS[kind] * n,))]


def _exchange(kind, parts, name):
    n = len(parts)

    def body(*refs):
        copies = _exchange_copies(kind, refs[:n], refs[n:2 * n], *refs[2 * n:])
        _start_all(copies)
        _wait_all(copies)

    return pl.pallas_call(
        body, name=name, in_specs=[ANY] * n, out_specs=[ANY] * n,
        out_shape=_exchange_shapes(kind, parts), scratch_shapes=_exchange_sems(kind, n),
    )(*parts)


def _pair_sum(where, part, recv, wire_dtype, name):
    _, r, c = part.shape
    tr = _pick(r, 256, 16)

    def body(w_ref, p_ref, r_ref, pb_ref, own_ref):
        s = p_ref[...] + r_ref[...]
        pb_ref[...] = s.astype(wire_dtype)

        @pl.when(pl.program_id(1) == w_ref[1])
        def _():
            own_ref[...] = s

    return pl.pallas_call(
        body, name=name,
        grid_spec=pltpu.PrefetchScalarGridSpec(
            num_scalar_prefetch=1, grid=(r // tr, NCHIP),
            in_specs=[pl.BlockSpec((None, tr, c), lambda i, q, w: (2 * q + w[0], i, 0)),
                      pl.BlockSpec((None, tr, c), lambda i, q, w: (q, i, 0))],
            out_specs=[pl.BlockSpec((None, tr, c), lambda i, q, w: (q, i, 0)),
                       pl.BlockSpec((tr, c), lambda i, q, w: (i, 0))]),
        out_shape=[jax.ShapeDtypeStruct((NCHIP, r, c), wire_dtype), jax.ShapeDtypeStruct((r, c), F32)],
        compiler_params=_cp(("parallel", "arbitrary")),
    )(where, part, recv)


def _chip_sum(own, recv, name):
    r, c = own.shape
    tr = _pick(r, 256, 16)

    def body(o_ref, r_ref, out_ref):
        acc = o_ref[...]
        for j in range(3):
            acc = acc + r_ref[j].astype(F32)
        out_ref[...] = acc

    return pl.pallas_call(
        body, name=name, grid=(r // tr,),
        in_specs=[pl.BlockSpec((tr, c), lambda i: (i, 0)), pl.BlockSpec((3, tr, c), lambda i: (0, i, 0))],
        out_specs=pl.BlockSpec((tr, c), lambda i: (i, 0)),
        out_shape=jax.ShapeDtypeStruct((r, c), F32),
        compiler_params=_cp(("parallel",)),
    )(own, recv)


def _adamw(w, own, recv, m, v, name):
    r, c = w.shape
    tr = _pick(r, 256, 16)

    def body(w_ref, o_ref, r_ref, m_ref, v_ref, g_ref, d_ref, nm_ref, nv_ref):
        acc = o_ref[...]
        for j in range(3):
            acc = acc + r_ref[j].astype(F32)
        g_ref[...] = acc
        _adamw_refs(w_ref, g_ref, m_ref, v_ref, d_ref, nm_ref, nv_ref)

    blk = pl.BlockSpec((tr, c), lambda i: (i, 0))
    return pl.pallas_call(
        body, name=name, grid=(r // tr,),
        in_specs=[blk, blk, pl.BlockSpec((3, tr, c), lambda i: (0, i, 0)), blk, blk], out_specs=[blk] * 4,
        out_shape=[jax.ShapeDtypeStruct((r, c), F32)] * 4,
        compiler_params=_cp(("parallel",)),
    )(w, own, recv, m, v)


def _adamw_refs(w_ref, g_ref, m_ref, v_ref, d_ref, nm_ref, nv_ref):
    gv = g_ref[...]
    nm = B1 * m_ref[...] + (1.0 - B1) * gv
    nv = B2 * v_ref[...] + (1.0 - B2) * (gv * gv)
    nm_ref[...] = nm
    nv_ref[...] = nv
    d_ref[...] = -LR * ((nm / C1) / (jnp.sqrt(nv / C2) + AEPS) + WD * w_ref[...])


def _adamw_small(ws, gs, ms, vs, name):
    n = len(ws)

    def body(*refs):
        groups = [refs[i * n:(i + 1) * n] for i in range(7)]
        for per_param in zip(*groups):
            _adamw_refs(*per_param)

    vm = pl.BlockSpec(memory_space=pltpu.VMEM)
    outs = pl.pallas_call(
        body, name=name, in_specs=[vm] * (4 * n), out_specs=[vm] * (3 * n),
        out_shape=[jax.ShapeDtypeStruct(a.shape, F32) for a in ws] * 3,
    )(*ws, *gs, *ms, *vs)
    return outs[:n], outs[n:2 * n], outs[2 * n:]


def _swap(a):
    return jnp.swapaxes(a, -1, -2)


VIEWS = {
    "w_in": (lambda a: a[0].T, lambda u: u.T[None]),
    "w_up": (lambda a: a[0].T, lambda u: u.T[None]),
    "w_glu": (lambda a: a[0], lambda u: u[None]),
    "w_out": (lambda a: a[0], lambda u: u[None]),
    "w_down": (lambda a: a[0], lambda u: u[None]),
    "conv_w": (lambda a: a[0], lambda u: u[None]),
    "norm_mix_g": (lambda a: a, lambda u: u),
    "norm_attn_g": (lambda a: a, lambda u: u),
    "norm_ssm_g": (lambda a: a, lambda u: u),
    "norm_ffn_g": (lambda a: a, lambda u: u),
    "norm_final_g": (lambda a: a[None], lambda u: u[0]),
    "conv_b": (lambda a: a, lambda u: u),
    "sink": (lambda a: a, lambda u: u),
    "a_re": (lambda a: a.reshape(2 * NG, NP), lambda u: u.reshape(1, 2, NG, NP)),
    "a_im": (lambda a: a.reshape(2 * NG, NP), lambda u: u.reshape(1, 2, NG, NP)),
    "log_step": (lambda a: a[0], lambda u: u[None]),
    "b_re": (lambda a: _swap(a[0]).reshape(2 * NG * GC, NP), lambda u: _swap(u.reshape(2, NG, GC, NP))[None]),
    "b_im": (lambda a: _swap(a[0]).reshape(2 * NG * GC, NP), lambda u: _swap(u.reshape(2, NG, GC, NP))[None]),
    "c_re": (lambda a: a.reshape(2 * NG * GC, NP), lambda u: u.reshape(1, 2, NG, GC, NP)),
    "c_im": (lambda a: a.reshape(2 * NG * GC, NP), lambda u: u.reshape(1, 2, NG, GC, NP)),
    "d_skip": (lambda a: a[0].T, lambda u: u.T[None]),
}
BIG = ["w_in", "w_glu", "w_out", "w_up", "w_down"]
PACK_W = 1024


def _pack(arrs, rows):
    flat = jnp.concatenate([a.reshape(-1).astype(F32) for a in arrs])
    return jnp.pad(flat, (0, rows * PACK_W - flat.shape[0])).reshape(rows, PACK_W)


def _unpack(packed, shapes):
    flat = packed.reshape(-1)
    out, off = [], 0
    for s in shapes:
        size = math.prod(s)
        out.append(flat[off:off + size].reshape(s))
        off += size
    return out


def kernel(x, norm_mix_g, w_in, a_re, a_im, log_step, b_re, b_im, c_re, c_im, d_skip, w_glu, sink, norm_attn_g, norm_ssm_g, w_out, norm_ffn_g, w_up, conv_w, conv_b, w_down, norm_final_g, loss_target, m_norm_mix_g, m_w_in, m_a_re, m_a_im, m_log_step, m_b_re, m_b_im, m_c_re, m_c_im, m_d_skip, m_w_glu, m_sink, m_norm_attn_g, m_norm_ssm_g, m_w_out, m_norm_ffn_g, m_w_up, m_conv_w, m_conv_b, m_w_down, m_norm_final_g, v_norm_mix_g, v_w_in, v_a_re, v_a_im, v_log_step, v_b_re, v_b_im, v_c_re, v_c_im, v_d_skip, v_w_glu, v_sink, v_norm_attn_g, v_norm_ssm_g, v_w_out, v_norm_ffn_g, v_w_up, v_conv_w, v_conv_b, v_w_down, v_norm_final_g):
    args = dict(locals())
    names = ["norm_mix_g", "w_in", "a_re", "a_im", "log_step", "b_re", "b_im", "c_re", "c_im", "d_skip", "w_glu",
             "sink", "norm_attn_g", "norm_ssm_g", "w_out", "norm_ffn_g", "w_up", "conv_w", "conv_b", "w_down",
             "norm_final_g"]
    w = {k: args[k] for k in names}
    m = {k: args["m_" + k] for k in names}
    v = {k: args["v_" + k] for k in names}

    (w_in_t,) = _all_gather([w_in[0].T.astype(BF16)], "gather_w_in")
    under_attn = dict(w_glu=w_glu[0].astype(BF16), w_out=w_out[0].astype(BF16),
                      conv_w=jnp.pad(conv_w[0], ((0, 5), (0, 0))))
    under_scan = dict(w_up_t=w_up[0].T.astype(BF16), w_down=w_down[0].astype(BF16))

    ax, ay, ac = _coords()
    me = 4 * ax + 2 * ay + ac
    where = jnp.stack([ac, 2 * ax + ay]).astype(jnp.int32)
    parts, own, got = {}, {}, {}

    def split8(g):
        return g.reshape(NDEV, g.shape[0] // NDEV, g.shape[1])

    def attend(qkv, sink_):
        attn, lse, gathered = _attn_fwd(qkv, sink_, gather=list(under_attn.values()))
        wts = dict(zip(under_attn.keys(), gathered))
        wts["conv_w"] = (wts["conv_w"].reshape(NDEV, 8, 2 * DFF // NDEV)[:, :3].transpose(1, 0, 2)
                         .reshape(3, 2 * DFF))
        return attn, lse, wts

    def scan(*operands):
        y_p, states, gathered = _s5_fwd(*operands, gather=list(under_scan.values()))
        return y_p, states, dict(zip(under_scan.keys(), gathered))

    def pair_sum(k, from_core):
        per_chip, own[k] = _pair_sum(where, parts[k], from_core, F32 if k == "small" else BF16, "pair_sum_" + k)
        return per_chip

    def stage(ks, phase, payload):
        if phase == "cores":
            parts.update({k: split8(g) for k, g in zip(ks, payload)})
            return ("cores", [parts[k] for k in ks])
        if phase == "chips":
            return ("chips", [pair_sum(k, fc) for k, fc in zip(ks, payload)])
        got.update(zip(ks, payload))
        return ()

    p = {k: w[k][0] for k in ("norm_mix_g", "a_re", "a_im", "log_step", "c_re", "c_im", "d_skip", "sink",
                              "norm_attn_g", "norm_ssm_g", "norm_ffn_g", "conv_b")}
    p["norm_final_g"] = norm_final_g
    p["bt_re"], p["bt_im"] = _swap(b_re[0]), _swap(b_im[0])
    grad_x, big, small = _local_step(x[0], loss_target[0], w_in_t, p, attend, scan, stage)

    small_names = list(small.keys())
    small_shapes = [small[k].shape for k in small_names]
    n_small = sum(math.prod(s) for s in small_shapes)
    rows_dev = -(-n_small // (PACK_W * NDEV * 16)) * 16
    spack = _pack([small[k] for k in small_names], rows_dev * NDEV)
    late = ["w_in_t", "small"]
    parts.update(w_in_t=split8(big["w_in_t"]), small=spack.reshape(NDEV, rows_dev, PACK_W))
    from_cores = _exchange("cores", [parts[k] for k in late], "exchange_cores")
    from_chips = _exchange("chips", [pair_sum(k, fc) for k, fc in zip(late, from_cores)], "exchange_chips")
    got.update(zip(late, from_chips))
    (small_full,) = _all_gather([_chip_sum(own["small"], got["small"], "chip_sum_small")], "gather_small")
    sm = dict(zip(small_names, _unpack(small_full, small_shapes)))

    _, s5_vjp = jax.vjp(_s5_params, a_re[0], a_im[0], log_step[0], p["bt_re"], p["bt_im"])
    da_re, da_im, dlog_step, dbt_re, dbt_im = s5_vjp((sm["lam_re"], sm["lam_im"], sm["bb_re"], sm["bb_im"]))
    gview = {
        "norm_mix_g": sm["norm_mix_g"], "norm_attn_g": sm["norm_attn_g"], "norm_ssm_g": sm["norm_ssm_g"],
        "norm_ffn_g": sm["norm_ffn_g"], "norm_final_g": sm["norm_final_g"], "conv_b": sm["conv_b"],
        "sink": sm["sink"][None], "a_re": da_re.reshape(2 * NG, NP), "a_im": da_im.reshape(2 * NG, NP),
        "log_step": dlog_step, "b_re": dbt_re.reshape(2 * NG * GC, NP), "b_im": dbt_im.reshape(2 * NG * GC, NP),
        "c_re": sm["c_re"].reshape(2 * NG * GC, NP), "c_im": sm["c_im"].reshape(2 * NG * GC, NP),
        "d_skip": sm["d_skip"].reshape(NG, GC).T,
        "conv_w": lax.dynamic_slice_in_dim(sm["conv_w"], me * (2 * DFF // NDEV), 2 * DFF // NDEV, axis=1),
    }

    dview, mview, vview = {}, {}, {}
    for k, kg in zip(BIG, ("w_in_t", "w_glu", "w_out", "w_up_t", "w_down")):
        to = VIEWS[k][0]
        gview[k], dview[k], mview[k], vview[k] = _adamw(to(w[k]), own[kg], got[kg], to(m[k]), to(v[k]), "adamw_" + k)
    rest = [k for k in names if k not in BIG]
    outs = _adamw_small([VIEWS[k][0](w[k]) for k in rest], [gview[k] for k in rest],
                        [VIEWS[k][0](m[k]) for k in rest], [VIEWS[k][0](v[k]) for k in rest], "adamw_small")
    for dst, vals in zip((dview, mview, vview), outs):
        dst.update(dict(zip(rest, vals)))

    def back(views):
        return [VIEWS[k][1](views[k]) for k in names]

    return (sm["loss"][0], grad_x[None], *back(gview), *back(dview), *back(mview), *back(vview))
```

```python
import functools
import math

import jax
import jax.numpy as jnp
from jax import lax
from jax.experimental import pallas as pl
from jax.experimental.pallas import tpu as pltpu

F32 = jnp.float32
BF16 = jnp.bfloat16

L = 4096
D = 1024
NQ, NKV, HD = 8, 2, 64
AW = NQ * HD
KVW = NKV * HD
SW = 512
NG, GC, NP = 32, 16, 64
INW = AW + 2 * KVW + SW
DFF = 2816
BLK = 128
WIN = 3 * BLK
EPS = 1e-6
ROPE_THETA = 500000.0
NSEG = 32
TSEG = L // NSEG
SBW = 256
NSB = NG * NP // SBW
NDEV = 8
MESH_AXES = ("x", "y", "c")

LR, B1, B2, AEPS, WD, STEP = 0.001, 0.9, 0.999, 1e-08, 0.01, 10
C1 = 1.0 - B1 ** STEP
C2 = 1.0 - B2 ** STEP

VMEM_LIMIT = 56 * 1024 * 1024


def _pick(n, target, mult):
    best = None
    for t in range(mult, min(n, target) + 1, mult):
        if n % t == 0:
            best = t
    return best if best is not None else n


def _cp(sem):
    return pltpu.CompilerParams(dimension_semantics=sem, vmem_limit_bytes=VMEM_LIMIT)


def _mm(a, b, *, ta=False, tb=False, out_dtype=F32, add=None, ride=(), post=None, name, tm=1024, tn=1024, tk=1024):
    m, k = (a.shape[1], a.shape[0]) if ta else a.shape
    n = b.shape[0] if tb else b.shape[1]
    assert k == (b.shape[1] if tb else b.shape[0])
    tm, tn, tk = _pick(m, tm, 128), _pick(n, tn, 128), _pick(k, tk, 128)
    grid = (m // tm, n // tn, k // tk)
    nk = grid[2]
    dn = (((0 if ta else 1,), (1 if tb else 0,)), ((), ()))
    n_in = 2 + (add is not None)
    kind, riding = ride if ride else (None, ())
    nr = len(riding)
    post_fn, post_ins, post_outs = post if post is not None else (None, (), ())
    n_pi = len(post_ins)
    n_out = len(post_outs) if post is not None else 1
    assert post is None or grid[1] == 1

    def body(*refs):
        a_ref, b_ref = refs[0], refs[1]
        pin = refs[n_in:n_in + n_pi]
        base = n_in + n_pi + nr
        o_refs = refs[base:base + n_out]
        acc_ref = refs[base + n_out + nr]
        step = [pl.program_id(d) for d in range(3)]
        kk = step[2]
        if nr:
            riders = (refs[n_in + n_pi:base], refs[base + n_out:base + n_out + nr], *refs[base + n_out + nr + 1:])

            @pl.when((step[0] == 0) & (step[1] == 0) & (kk == 0))
            def _():
                _start_all(_exchange_copies(kind, *riders))

        prod = lax.dot_general(a_ref[...].astype(BF16), b_ref[...].astype(BF16), dn, preferred_element_type=F32)

        def finish(r):
            if add is not None:
                r = r + refs[2][...]
            if post_fn is None:
                o_refs[0][...] = r.astype(out_dtype)
            else:
                post_fn(r, step[0], pin, o_refs)

        if nk == 1:
            finish(prod)
        else:
            @pl.when(kk == 0)
            def _():
                acc_ref[...] = prod

            @pl.when((kk > 0) & (kk < nk - 1))
            def _():
                acc_ref[...] += prod

            @pl.when(kk == nk - 1)
            def _():
                finish(acc_ref[...] + prod)

        if nr:
            @pl.when((step[0] == grid[0] - 1) & (step[1] == grid[1] - 1) & (kk == nk - 1))
            def _():
                _wait_all(_exchange_copies(kind, *riders))

    a_spec = pl.BlockSpec((tk, tm), lambda i, j, kk: (kk, i)) if ta else pl.BlockSpec((tm, tk), lambda i, j, kk: (i, kk))
    b_spec = pl.BlockSpec((tn, tk), lambda i, j, kk: (j, kk)) if tb else pl.BlockSpec((tk, tn), lambda i, j, kk: (kk, j))
    def row_spec(shape):
        return pl.BlockSpec((tm if shape[0] == m else shape[0], shape[1]),
                            (lambda i, j, kk: (i, 0)) if shape[0] == m else (lambda i, j, kk: (0, 0)))

    in_specs = [a_spec, b_spec]
    args = [a, b]
    if add is not None:
        in_specs.append(pl.BlockSpec((tm, tn), lambda i, j, kk: (i, j)))
        args.append(add)
    if post is None:
        main_specs = [pl.BlockSpec((tm, tn), lambda i, j, kk: (i, j))]
        main_shapes = [jax.ShapeDtypeStruct((m, n), out_dtype)]
    else:
        main_specs = [row_spec(s.shape) for s in post_outs]
        main_shapes = list(post_outs)
    outs = pl.pallas_call(
        body, name=name, grid=grid,
        in_specs=in_specs + [row_spec(p.shape) for p in post_ins] + [ANY] * nr,
        out_specs=main_specs + [ANY] * nr,
        out_shape=main_shapes + (_exchange_shapes(kind, riding) if nr else []),
        scratch_shapes=[pltpu.VMEM((tm, tn) if nk > 1 else (8, 128), F32)] + (_exchange_sems(kind, nr) if nr else []),
        compiler_params=_cp(("arbitrary",) * 3 if (nr or post is not None) else ("parallel", "parallel", "arbitrary")),
    )(*args, *post_ins, *riding)
    main = outs[0] if post is None else list(outs[:n_out])
    return (main, list(outs[n_out:])) if nr else main


TL = 512


def _rms(xv, gv):
    return xv * lax.rsqrt(jnp.mean(xv * xv, axis=-1, keepdims=True) + EPS) * gv


def _rows(width):
    return pl.BlockSpec((TL, width), lambda i: (i, 0))


def _whole(shape):
    return pl.BlockSpec(shape, lambda i: (0,) * len(shape))


def _in_proj(x, g, w_in_t, tabs):
    qkw = AW + 2 * KVW

    def body(x_ref, g_ref, w_ref, c_ref, sa_ref, sb_ref, h_ref, qkv_ref, u_ref):
        h = _rms(x_ref[...], g_ref[...]).astype(BF16)
        h_ref[...] = h
        proj = lax.dot_general(h, w_ref[...], _NT, preferred_element_type=F32)
        for j in range(qkw // 128):
            cols = slice(j * 128, (j + 1) * 128)
            xv = proj[:, cols]
            if j < (AW + KVW) // 128:
                xv = _rope(xv, c_ref[...], sa_ref[...], sb_ref[...], 1.0)
            qkv_ref[:, cols] = xv.astype(BF16)
        u_ref[...] = proj[:, qkw:]

    return pl.pallas_call(
        body, name="in_proj", grid=(L // TL,),
        in_specs=[_rows(D), _whole((1, D)), _whole((INW, D)), _rows(128), _rows(128), _rows(128)],
        out_specs=[_rows(D), _rows(qkw), _rows(SW)],
        out_shape=[jax.ShapeDtypeStruct((L, D), BF16), jax.ShapeDtypeStruct((L, qkw), BF16),
                   jax.ShapeDtypeStruct((L, SW), F32)],
        compiler_params=_cp(("parallel",)),
    )(x, g, w_in_t, *tabs)


def _in_proj_dx(dproj, w_in_t, x, g, dres):
    def body(dp_ref, w_ref, x_ref, g_ref, dres_ref, dx_ref, dg_ref):
        dh = jnp.dot(dp_ref[...], w_ref[...], preferred_element_type=F32)
        dx, dg = _rms_bwd_tile(x_ref[...], g_ref[...], dh)
        dx_ref[...] = dx + dres_ref[...]

        @pl.when(pl.program_id(0) == 0)
        def _():
            dg_ref[...] = jnp.zeros_like(dg_ref)

        dg_ref[...] += dg

    return pl.pallas_call(
        body, name="in_proj_dx", grid=(L // TL,),
        in_specs=[_rows(INW), _whole((INW, D)), _rows(D), _whole((1, D)), _rows(D)],
        out_specs=[_rows(D), _whole((1, D))],
        out_shape=[jax.ShapeDtypeStruct((L, D), F32), jax.ShapeDtypeStruct((1, D), F32)],
        compiler_params=_cp(("arbitrary",)),
    )(dproj, w_in_t, x, g, dres)


def _rms_bwd_tile(xv, gv, dh):
    r = lax.rsqrt(jnp.mean(xv * xv, axis=-1, keepdims=True) + EPS)
    a = dh * gv
    dx = r * a - xv * (r * r * r) * jnp.mean(a * xv, axis=-1, keepdims=True)
    dg = jnp.sum(dh * xv * r, axis=0, keepdims=True)
    return dx, dg


def _rms_bwd_post(dh, i, ins, outs):
    x_ref, dres_ref, g_ref = ins
    dx_ref, dxb_ref, dg_ref = outs
    dx, dg = _rms_bwd_tile(x_ref[...], g_ref[...], dh)
    dx = dx + dres_ref[...]
    dx_ref[...] = dx
    dxb_ref[...] = dx.astype(BF16)

    @pl.when(i == 0)
    def _():
        dg_ref[...] = jnp.zeros_like(dg_ref)

    dg_ref[...] += dg


def _final_post(xv, i, ins, outs):
    t_ref, g_ref = ins
    loss_ref, dx_ref, dxb_ref, dg_ref = outs

    @pl.when(i == 0)
    def _():
        loss_ref[...] = jnp.zeros_like(loss_ref)
        dg_ref[...] = jnp.zeros_like(dg_ref)

    gv = g_ref[...]
    r = lax.rsqrt(jnp.mean(xv * xv, axis=-1, keepdims=True) + EPS)
    e = xv * r * gv - t_ref[...]
    loss_ref[...] += 0.5 * jnp.sum(jnp.mean(e * e, axis=-1, keepdims=True), axis=0, keepdims=True)
    dy = e * (1.0 / D)
    a = dy * gv
    dx = r * a - xv * (r * r * r) * jnp.mean(a * xv, axis=-1, keepdims=True)
    dx_ref[...] = dx
    dxb_ref[...] = dx.astype(BF16)
    dg_ref[...] += jnp.sum(dy * xv * r, axis=0, keepdims=True)


def _out_proj(attn, ysg, ga, gs, w_out, x, gf):
    def body(a_ref, s_ref, ga_ref, gs_ref, w_ref, x_ref, gf_ref, m_ref, x1_ref, h2_ref):
        m_ref[:, 0:AW] = _rms(a_ref[...], ga_ref[...]).astype(BF16)
        m_ref[:, AW:AW + SW] = _rms(s_ref[...], gs_ref[...]).astype(BF16)
        x1 = jnp.dot(m_ref[...], w_ref[...], preferred_element_type=F32) + x_ref[...]
        x1_ref[...] = x1
        h2_ref[...] = _rms(x1, gf_ref[...]).astype(BF16)

    return pl.pallas_call(
        body, name="out_proj", grid=(L // TL,),
        in_specs=[_rows(AW), _rows(SW), _whole((1, AW)), _whole((1, SW)), _whole((D, D)), _rows(D), _whole((1, D))],
        out_specs=[_rows(D), _rows(D), _rows(D)],
        out_shape=[jax.ShapeDtypeStruct((L, D), BF16), jax.ShapeDtypeStruct((L, D), F32),
                   jax.ShapeDtypeStruct((L, D), BF16)],
        compiler_params=_cp(("parallel",)),
    )(attn, ysg, ga, gs, w_out, x, gf)


def _out_proj_dx(dx1, w_out, attn, ysg, ga, gs):
    def body(dx_ref, w_ref, a_ref, s_ref, ga_ref, gs_ref, da_ref, ds_ref, dga_ref, dgs_ref):
        @pl.when(pl.program_id(0) == 0)
        def _():
            dga_ref[...] = jnp.zeros_like(dga_ref)
            dgs_ref[...] = jnp.zeros_like(dgs_ref)

        dm = lax.dot_general(dx_ref[...].astype(BF16), w_ref[...], _NT, preferred_element_type=F32)
        dxa, dga = _rms_bwd_tile(a_ref[...], ga_ref[...], dm[:, 0:AW])
        da_ref[...] = dxa
        dga_ref[...] += dga
        dxs, dgs = _rms_bwd_tile(s_ref[...], gs_ref[...], dm[:, AW:AW + SW])
        ds_ref[...] = dxs
        dgs_ref[...] += dgs

    return pl.pallas_call(
        body, name="out_proj_dx", grid=(L // TL,),
        in_specs=[_rows(D), _whole((D, D)), _rows(AW), _rows(SW), _whole((1, AW)), _whole((1, SW))],
        out_specs=[_rows(AW), _rows(SW), _whole((1, AW)), _whole((1, SW))],
        out_shape=[jax.ShapeDtypeStruct((L, AW), F32), jax.ShapeDtypeStruct((L, SW), F32),
                   jax.ShapeDtypeStruct((1, AW), F32), jax.ShapeDtypeStruct((1, SW), F32)],
        compiler_params=_cp(("arbitrary",)),
    )(dx1, w_out, attn, ysg, ga, gs)


def _rope_tables():
    half = HD // 8
    inv_freq = jnp.power(ROPE_THETA, -jnp.arange(half, dtype=F32) / half)
    ang = jnp.arange(L, dtype=F32)[:, None] * inv_freq[None, :]
    cos, sin = jnp.cos(ang), jnp.sin(ang)
    one = jnp.ones((L, HD - 2 * half), F32)
    zero = jnp.zeros((L, HD - 2 * half), F32)
    zh = jnp.zeros((L, half), F32)
    cos64 = jnp.concatenate([cos, cos, one], axis=1)
    sa64 = jnp.concatenate([-sin, zh, zero], axis=1)
    sb64 = jnp.concatenate([zh, sin, zero], axis=1)
    return [jnp.tile(t, (1, 2)) for t in (cos64, sa64, sb64)]


def _rope(xv, cosv, sav, sbv, sign):
    return xv * cosv + sign * (pltpu.roll(xv, 120, 1) * sav + pltpu.roll(xv, 8, 1) * sbv)


def _rope_bwd(dq, dk, dv, du, tabs):
    def body(dq_ref, dk_ref, dv_ref, du_ref, c_ref, sa_ref, sb_ref, o_ref):
        for j in range(AW // 128):
            cols = slice(j * 128, (j + 1) * 128)
            o_ref[:, cols] = _rope(dq_ref[:, cols], c_ref[...], sa_ref[...], sb_ref[...], -1.0).astype(BF16)
        o_ref[:, AW:AW + KVW] = _rope(dk_ref[...], c_ref[...], sa_ref[...], sb_ref[...], -1.0).astype(BF16)
        o_ref[:, AW + KVW:AW + 2 * KVW] = dv_ref[...].astype(BF16)
        o_ref[:, AW + 2 * KVW:] = du_ref[...].astype(BF16)

    def row(width):
        return pl.BlockSpec((TL, width), lambda i: (i, 0))

    return pl.pallas_call(
        body, name="rope_bwd", grid=(L // TL,),
        in_specs=[row(AW), row(KVW), row(KVW), row(SW), row(128), row(128), row(128)],
        out_specs=row(INW), out_shape=jax.ShapeDtypeStruct((L, INW), BF16),
        compiler_params=_cp(("parallel",)),
    )(dq, dk, dv, du, *tabs)


def _attn_window(n):
    start = pl.multiple_of(jnp.clip((n - 1) * BLK, 0, L - WIN), BLK)
    qpos = n * BLK + lax.broadcasted_iota(jnp.int32, (BLK, WIN), 0)
    kpos = start + lax.broadcasted_iota(jnp.int32, (BLK, WIN), 1)
    return start, jnp.abs(kpos - qpos) <= BLK


_NT = (((1,), (1,)), ((), ()))
_TN = (((0,), (0,)), ((), ()))
NEG = -1e30


def _attn_fwd(qkv, sink, gather=()):
    ng = len(gather)

    def body(sink_ref, q_ref, k_ref, v_ref, *rest):
        o_ref, lse_ref = rest[ng], rest[ng + 1]
        s_scr, p_scr = rest[2 * ng + 2], rest[2 * ng + 3]
        n = pl.program_id(0)
        if ng:
            travellers = (rest[:ng], rest[ng + 2:2 * ng + 2], *rest[2 * ng + 4:])

            @pl.when(n == 0)
            def _():
                _gather_start(*travellers)

            @pl.when(n == L // BLK - 1)
            def _():
                _gather_finish(*travellers)

        start, valid = _attn_window(n)
        kw = k_ref[pl.ds(start, WIN), :]
        vw = v_ref[pl.ds(start, WIN), :]
        for h in range(NQ):
            kv = h // (NQ // NKV)
            s_scr[h] = lax.dot_general(q_ref[:, h * HD:(h + 1) * HD], kw[:, kv * HD:(kv + 1) * HD], _NT,
                                       preferred_element_type=F32)
        for h in range(NQ):
            s = jnp.where(valid, s_scr[h] * (HD ** -0.5), NEG)
            sk = sink_ref[h]
            m = jnp.maximum(jnp.max(s, axis=-1, keepdims=True), sk)
            p = jnp.exp(s - m)
            den = jnp.sum(p, axis=-1, keepdims=True) + jnp.exp(sk - m)
            p_scr[h] = (p / den).astype(BF16)
            lse_ref[:, h:h + 1] = m + jnp.log(den)
        for h in range(NQ):
            kv = h // (NQ // NKV)
            o_ref[:, h * HD:(h + 1) * HD] = jnp.dot(p_scr[h], vw[:, kv * HD:(kv + 1) * HD],
                                                    preferred_element_type=F32)

    outs = pl.pallas_call(
        body, name="attn_fwd", grid=(L // BLK,),
        in_specs=[pl.BlockSpec(memory_space=pltpu.SMEM),
                  pl.BlockSpec((BLK, AW), lambda n: (n, 0)),
                  pl.BlockSpec((L, KVW), lambda n: (0, AW // KVW)),
                  pl.BlockSpec((L, KVW), lambda n: (0, AW // KVW + 1))] + [ANY] * ng,
        out_specs=[pl.BlockSpec((BLK, AW), lambda n: (n, 0)), pl.BlockSpec((BLK, NQ), lambda n: (n, 0))] + [ANY] * ng,
        out_shape=[jax.ShapeDtypeStruct((L, AW), F32), jax.ShapeDtypeStruct((L, NQ), F32)] + _gather_shapes(gather),
        scratch_shapes=[pltpu.VMEM((NQ, BLK, WIN), F32), pltpu.VMEM((NQ, BLK, WIN), BF16)]
        + (_gather_sems(ng) if ng else []),
        compiler_params=_cp(("arbitrary",) if ng else ("parallel",)),
    )(sink, qkv, qkv, qkv, *gather)
    return outs[0], outs[1], list(outs[2:])


def _attn_bwd(qkv, sink, attn, lse, dattn, ride=()):
    kind, riding = ride if ride else (None, ())
    nr = len(riding)

    def body(*refs):
        if nr:
            riders = (refs[7:7 + nr], refs[11 + nr:11 + 2 * nr], *refs[15 + 2 * nr:])

            @pl.when(pl.program_id(0) == 0)
            def _():
                _start_all(_exchange_copies(kind, *riders))

        compute(*refs[:7], *refs[7 + nr:11 + nr], *refs[11 + 2 * nr:15 + 2 * nr])
        if nr:
            @pl.when(pl.program_id(0) == L // BLK - 1)
            def _():
                _wait_all(_exchange_copies(kind, *riders))

    def compute(sink_ref, q_ref, k_ref, v_ref, o_ref, lse_ref, do_ref, dq_ref, dk_ref, dv_ref, dsink_ref,
                s_scr, dp_scr, p_scr, ds_scr):
        n = pl.program_id(0)

        @pl.when(n == 0)
        def _():
            dk_ref[...] = jnp.zeros_like(dk_ref)
            dv_ref[...] = jnp.zeros_like(dv_ref)
            dsink_ref[...] = jnp.zeros_like(dsink_ref)

        start, valid = _attn_window(n)
        kw = k_ref[pl.ds(start, WIN), :]
        vw = v_ref[pl.ds(start, WIN), :]
        group = NQ // NKV
        for h in range(NQ):
            kv = h // group
            s_scr[h] = lax.dot_general(q_ref[:, h * HD:(h + 1) * HD], kw[:, kv * HD:(kv + 1) * HD], _NT,
                                       preferred_element_type=F32)
            dp_scr[h] = lax.dot_general(do_ref[:, h * HD:(h + 1) * HD].astype(BF16), vw[:, kv * HD:(kv + 1) * HD],
                                        _NT, preferred_element_type=F32)
        for h in range(NQ):
            dd = jnp.sum(do_ref[:, h * HD:(h + 1) * HD] * o_ref[:, h * HD:(h + 1) * HD], axis=-1, keepdims=True)
            lse_h = lse_ref[:, h:h + 1]
            p = jnp.where(valid, jnp.exp(s_scr[h] * (HD ** -0.5) - lse_h), 0.0)
            p_scr[h] = p.astype(BF16)
            ds_scr[h] = (p * (dp_scr[h] - dd) * (HD ** -0.5)).astype(BF16)
            dsk = -jnp.sum(jnp.exp(sink_ref[h] - lse_h) * dd, axis=0, keepdims=True)
            dsink_ref[h:h + 1, :] += jnp.broadcast_to(dsk, (1, 128))
        for kv in range(NKV):
            kh = kw[:, kv * HD:(kv + 1) * HD]
            dk_acc = jnp.zeros((WIN, HD), F32)
            dv_acc = jnp.zeros((WIN, HD), F32)
            for h in range(kv * group, (kv + 1) * group):
                dq_ref[:, h * HD:(h + 1) * HD] = jnp.dot(ds_scr[h], kh, preferred_element_type=F32)
                dk_acc += lax.dot_general(ds_scr[h], q_ref[:, h * HD:(h + 1) * HD], _TN, preferred_element_type=F32)
                dv_acc += lax.dot_general(p_scr[h], do_ref[:, h * HD:(h + 1) * HD].astype(BF16), _TN,
                                          preferred_element_type=F32)
            dk_ref[pl.ds(start, WIN), kv * HD:(kv + 1) * HD] += dk_acc
            dv_ref[pl.ds(start, WIN), kv * HD:(kv + 1) * HD] += dv_acc

    qblk = pl.BlockSpec((BLK, AW), lambda n: (n, 0))
    full = pl.BlockSpec((L, KVW), lambda n: (0, 0))
    outs = pl.pallas_call(
        body, name="attn_bwd", grid=(L // BLK,),
        in_specs=[pl.BlockSpec(memory_space=pltpu.SMEM), qblk,
                  pl.BlockSpec((L, KVW), lambda n: (0, AW // KVW)),
                  pl.BlockSpec((L, KVW), lambda n: (0, AW // KVW + 1)),
                  qblk, pl.BlockSpec((BLK, NQ), lambda n: (n, 0)), qblk] + [ANY] * nr,
        out_specs=[qblk, full, full, pl.BlockSpec((NQ, 128), lambda n: (0, 0))] + [ANY] * nr,
        out_shape=[jax.ShapeDtypeStruct((L, AW), F32), jax.ShapeDtypeStruct((L, KVW), F32),
                   jax.ShapeDtypeStruct((L, KVW), F32), jax.ShapeDtypeStruct((NQ, 128), F32)]
        + (_exchange_shapes(kind, riding) if nr else []),
        scratch_shapes=[pltpu.VMEM((NQ, BLK, WIN), F32), pltpu.VMEM((NQ, BLK, WIN), F32),
                        pltpu.VMEM((NQ, BLK, WIN), BF16), pltpu.VMEM((NQ, BLK, WIN), BF16)]
        + (_exchange_sems(kind, nr) if nr else []),
        compiler_params=_cp(("arbitrary",)),
    )(sink, qkv, qkv, qkv, attn, lse, dattn, *riding)
    return list(outs[:4]), list(outs[4:])


def _perm(a):
    return a.reshape(NSEG, TSEG, a.shape[1]).transpose(1, 0, 2).reshape(L, a.shape[1])


def _unperm(a):
    return a.reshape(TSEG, NSEG, a.shape[1]).transpose(1, 0, 2).reshape(L, a.shape[1])


def _cmul(ar, ai, br, bi):
    return ar * br - ai * bi, ar * bi + ai * br


def _scan_inplace(s_ref, lr, li, rev, visit=None, carried=(), out_ref=None):
    n = lr.shape[1]
    lr8 = jnp.broadcast_to(lr, (NSEG, n))
    li8 = jnp.broadcast_to(li, (NSEG, n))

    def rows(k):
        return pl.ds(pl.multiple_of(jnp.where(rev, TSEG - 1 - k, k) * NSEG, NSEG), NSEG)

    def step(k, c, store):
        sr, si = c
        rs = rows(k)
        pr, pi = _cmul(lr8, li8, sr, si)
        nr = pr + s_ref[rs, 0:n]
        ni = pi + s_ref[rs, n:2 * n]
        if store:
            dst = s_ref if out_ref is None else out_ref
            dst[rs, 0:n] = nr.astype(dst.dtype)
            dst[rs, n:2 * n] = ni.astype(dst.dtype)
        return nr, ni

    z = jnp.zeros((NSEG, n), F32)
    er, ei = lax.fori_loop(0, TSEG, functools.partial(step, store=False), (z, z))
    pr, pi = lr, li
    for _ in range(int(math.log2(TSEG))):
        pr, pi = _cmul(pr, pi, pr, pi)

    seg = lax.broadcasted_iota(jnp.int32, (NSEG, n), 0)

    def chain(order):
        cr = jnp.zeros((1, n), F32)
        ci = jnp.zeros((1, n), F32)
        outr = jnp.zeros((NSEG, n), F32)
        outi = jnp.zeros((NSEG, n), F32)
        for s in order:
            outr = jnp.where(seg == s, cr, outr)
            outi = jnp.where(seg == s, ci, outi)
            mr, mi = _cmul(pr, pi, cr, ci)
            cr, ci = mr + er[s:s + 1], mi + ei[s:s + 1]
        return outr, outi

    fr, fi = chain(range(NSEG))
    rr, ri = chain(range(NSEG - 1, -1, -1))
    cin_r = jnp.where(rev, rr, fr)
    cin_i = jnp.where(rev, ri, fi)
    if visit is None:
        lax.fori_loop(0, TSEG, functools.partial(step, store=True), (cin_r, cin_i))
        return cin_r, cin_i

    def visited(k, c):
        nr, ni = step(k, c[:2], True)
        return (nr, ni) + tuple(visit(k, nr, ni, c[2:]))

    fin = lax.fori_loop(0, TSEG - 1, visited, (cin_r, cin_i) + tuple(carried))
    last_r, last_i = step(TSEG - 1, fin[:2], True)
    return last_r, last_i, fin[2:]


S5_RC = 512


def _s5_specs():
    u_spec = pl.BlockSpec((L, 128), lambda cb, h, d: (0, cb))
    b_spec = pl.BlockSpec((None, None, 128, 2 * SBW), lambda cb, h, d: (d, cb * 2 + h, 0, 0))
    c_spec = pl.BlockSpec((None, None, 2 * SBW, 128), lambda cb, h, d: (d, cb * 2 + h, 0, 0))
    l_spec = pl.BlockSpec((None, None, 1, SBW), lambda cb, h, d: (d, cb * 2 + h, 0, 0))
    d_spec = pl.BlockSpec((1, 128), lambda cb, h, d: (0, cb))
    return u_spec, b_spec, c_spec, l_spec, d_spec


def _s5_fwd(u_p, bcat, ccat, lam_re, lam_im, dskip, gather=()):
    ng = len(gather)
    grid = (SW // 128, 2, 2)

    def body(*refs):
        if ng:
            travellers = (refs[6:6 + ng], refs[8 + ng:8 + 2 * ng], *refs[9 + 2 * ng:])
            step = [pl.program_id(d) for d in range(3)]

            @pl.when((step[0] == 0) & (step[1] == 0) & (step[2] == 0))
            def _():
                _gather_start(*travellers)

        compute(*refs[:6], *refs[6 + ng:8 + ng], refs[8 + 2 * ng])
        if ng:
            @pl.when((step[0] == grid[0] - 1) & (step[1] == grid[1] - 1) & (step[2] == grid[2] - 1))
            def _():
                _gather_finish(*travellers)

    def compute(u_ref, b_ref, c_ref, lr_ref, li_ref, d_ref, y_ref, sb_ref, s_scr):
        first = (pl.program_id(1) == 0) & (pl.program_id(2) == 0)

        def proj(i, _):
            rs = pl.ds(pl.multiple_of(i * S5_RC, S5_RC), S5_RC)
            s_scr[rs, :] = jnp.dot(u_ref[rs, :].astype(BF16), b_ref[...], preferred_element_type=F32)
            return 0

        lax.fori_loop(0, L // S5_RC, proj, 0)
        _scan_inplace(s_scr, lr_ref[...], li_ref[...], pl.program_id(2) == 1, out_ref=sb_ref)

        def out(i, _):
            rs = pl.ds(pl.multiple_of(i * S5_RC, S5_RC), S5_RC)
            yv = jnp.dot(sb_ref[rs, :], c_ref[...], preferred_element_type=F32)

            @pl.when(first)
            def _():
                y_ref[rs, :] = d_ref[...] * u_ref[rs, :] + yv

            @pl.when(jnp.logical_not(first))
            def _():
                y_ref[rs, :] += yv

            return 0

        lax.fori_loop(0, L // S5_RC, out, 0)

    u_spec, b_spec, c_spec, l_spec, d_spec = _s5_specs()
    outs = pl.pallas_call(
        body, name="s5_fwd", grid=grid,
        in_specs=[u_spec, b_spec, c_spec, l_spec, l_spec, d_spec] + [ANY] * ng,
        out_specs=[u_spec, _s5_state_spec()] + [ANY] * ng,
        out_shape=[jax.ShapeDtypeStruct((L, SW), F32), jax.ShapeDtypeStruct((2, NSB, L, 2 * SBW), BF16)]
        + _gather_shapes(gather),
        scratch_shapes=[pltpu.VMEM((L, 2 * SBW), F32)] + (_gather_sems(ng) if ng else []),
        compiler_params=_cp(("arbitrary",) * 3 if ng else ("parallel", "arbitrary", "arbitrary")),
    )(u_p, bcat, ccat, lam_re, lam_im, dskip, *gather)
    return outs[0], outs[1], list(outs[2:])


def _s5_state_spec():
    return pl.BlockSpec((None, None, L, 2 * SBW), lambda cb, h, d: (d, cb * 2 + h, 0, 0))


def _s5_bwd(u_p, dy_p, states, bcat, ccat, lam_re, lam_im, dskip, ride=()):
    kind, riding = ride if ride else (None, ())
    nr = len(riding)
    grid = (SW // 128, 2, 2)

    def body(*refs):
        work = refs[:8] + refs[8 + nr:14 + nr] + refs[14 + 2 * nr:16 + 2 * nr]
        if nr:
            riders = (refs[8:8 + nr], refs[14 + nr:14 + 2 * nr], *refs[16 + 2 * nr:])
            step = [pl.program_id(d) for d in range(3)]

            @pl.when((step[0] == 0) & (step[1] == 0) & (step[2] == 0))
            def _():
                _start_all(_exchange_copies(kind, *riders))

        compute(*work)
        if nr:
            @pl.when((step[0] == grid[0] - 1) & (step[1] == grid[1] - 1) & (step[2] == grid[2] - 1))
            def _():
                _wait_all(_exchange_copies(kind, *riders))

    def compute(u_ref, dy_ref, s_ref, b_ref, c_ref, lr_ref, li_ref, d_ref,
                du_ref, db_ref, dc_ref, dlr_ref, dli_ref, dd_ref, g_scr, gb_scr):
        first = (pl.program_id(1) == 0) & (pl.program_id(2) == 0)
        rev = pl.program_id(2) == 1

        def dstate(i, _):
            rs = pl.ds(pl.multiple_of(i * S5_RC, S5_RC), S5_RC)
            g_scr[rs, :] = lax.dot_general(dy_ref[rs, :].astype(BF16), c_ref[...], _NT, preferred_element_type=F32)
            return 0

        lax.fori_loop(0, L // S5_RC, dstate, 0)

        def before(rows):
            sv = s_ref[rows, :].astype(F32)
            return sv[:, 0:SBW], sv[:, SBW:2 * SBW]

        def dlam(gr, gi, sr, si, ar, ai):
            return ar + gr * sr + gi * si, ai + gi * sr - gr * si

        def visit(k, gr, gi, acc):
            ts = jnp.where(rev, k + 1, TSEG - 2 - k)
            sr, si = before(pl.ds(pl.multiple_of(ts * NSEG, NSEG), NSEG))
            return dlam(gr, gi, sr, si, *acc)

        z = jnp.zeros((NSEG, SBW), F32)
        gr, gi, acc = _scan_inplace(g_scr, lr_ref[...], -li_ref[...], jnp.logical_not(rev), visit, (z, z), gb_scr)
        edge_r, edge_i = before(pl.ds(pl.multiple_of(jnp.where(rev, 0, TSEG - 1) * NSEG, NSEG), NSEG))
        seg = lax.broadcasted_iota(jnp.int32, (NSEG, SBW), 0)
        keep = seg != jnp.where(rev, NSEG - 1, 0)

        def neighbour(e):
            return jnp.where(keep, jnp.where(rev, pltpu.roll(e, NSEG - 1, 0), pltpu.roll(e, 1, 0)), 0.0)

        ar, ai = dlam(gr, gi, neighbour(edge_r), neighbour(edge_i), *acc)
        dlr_ref[...] = jnp.sum(ar, axis=0, keepdims=True)
        dli_ref[...] = jnp.sum(ai, axis=0, keepdims=True)

        db_ref[...] = jnp.zeros_like(db_ref)
        dc_ref[...] = jnp.zeros_like(dc_ref)

        @pl.when(first)
        def _():
            dd_ref[...] = jnp.zeros_like(dd_ref)

        def grads(i, _):
            rs = pl.ds(pl.multiple_of(i * S5_RC, S5_RC), S5_RC)
            uv = u_ref[rs, :]
            dyv = dy_ref[rs, :]
            gb = gb_scr[rs, :]
            db_ref[...] += lax.dot_general(uv.astype(BF16), gb, _TN, preferred_element_type=F32)
            dc_ref[...] += lax.dot_general(dyv.astype(BF16), s_ref[rs, :], _TN, preferred_element_type=F32)
            duv = lax.dot_general(gb, b_ref[...], _NT, preferred_element_type=F32)

            @pl.when(first)
            def _():
                du_ref[rs, :] = d_ref[...] * dyv + duv
                dd_ref[...] += jnp.sum(dyv * uv, axis=0, keepdims=True)

            @pl.when(jnp.logical_not(first))
            def _():
                du_ref[rs, :] += duv

            return 0

        lax.fori_loop(0, L // S5_RC, grads, 0)

    u_spec, b_spec, c_spec, l_spec, d_spec = _s5_specs()
    outs = pl.pallas_call(
        body, name="s5_bwd", grid=grid,
        in_specs=[u_spec, u_spec, _s5_state_spec(), b_spec, c_spec, l_spec, l_spec, d_spec] + [ANY] * nr,
        out_specs=[u_spec, b_spec, b_spec, l_spec, l_spec, d_spec] + [ANY] * nr,
        out_shape=[jax.ShapeDtypeStruct((L, SW), F32),
                   jax.ShapeDtypeStruct((2, NSB, 128, 2 * SBW), F32), jax.ShapeDtypeStruct((2, NSB, 128, 2 * SBW), F32),
                   jax.ShapeDtypeStruct((2, NSB, 1, SBW), F32), jax.ShapeDtypeStruct((2, NSB, 1, SBW), F32),
                   jax.ShapeDtypeStruct((1, SW), F32)] + (_exchange_shapes(kind, riding) if nr else []),
        scratch_shapes=[pltpu.VMEM((L, 2 * SBW), F32), pltpu.VMEM((L, 2 * SBW), BF16)]
        + (_exchange_sems(kind, nr) if nr else []),
        compiler_params=_cp(("arbitrary",) * 3 if nr else ("parallel", "arbitrary", "arbitrary")),
    )(u_p, dy_p, states, bcat, ccat, lam_re, lam_im, dskip, *riding)
    return list(outs[:6]), list(outs[6:])


def _s5_params(a_re, a_im, log_step, bt_re, bt_im):
    lam = lax.complex(a_re, a_im)
    step = jnp.exp(log_step)[..., None]
    lam_bar = jnp.exp(lam * step)
    b_bar = ((lam_bar - 1.0) / lam)[..., None, :] * lax.complex(bt_re, bt_im)
    return jnp.real(lam_bar), jnp.imag(lam_bar), jnp.real(b_bar), jnp.imag(b_bar)


def _sel():
    i = jnp.arange(8)[None, :, None]
    j = jnp.arange(4)[None, None, :]
    r = jnp.arange(2)[:, None, None]
    return (i == r * 4 + j).astype(F32)


def _to_bcat(bt_re, bt_im):
    def one(bt):
        return jnp.einsum('dkrjcp,rij->dkricjp', bt.reshape(2, 4, 2, 4, GC, NP), _sel()).reshape(2, NSB, 128, SBW)
    return jnp.concatenate([one(bt_re), one(bt_im)], axis=-1)


def _from_bcat(dbcat):
    def one(dbbd):
        return jnp.einsum('dkricjp,rij->dkrjcp', dbbd.reshape(2, 4, 2, 8, GC, 4, NP), _sel()).reshape(2, NG, GC, NP)
    return one(dbcat[..., :SBW]), one(dbcat[..., SBW:])


def _to_ccat(c_re, c_im):
    def one(cc):
        return jnp.einsum('dkrjcp,rij->dkrjpic', cc.reshape(2, 4, 2, 4, GC, NP), _sel()).reshape(2, NSB, SBW, 128)
    return jnp.concatenate([one(c_re), -one(c_im)], axis=-2)


def _from_ccat(dccat):
    def one(dcbd):
        return jnp.einsum('dkrjpic,rij->dkrjcp', dcbd.reshape(2, 4, 2, 4, NP, 8, GC), _sel()).reshape(2, NG, GC, NP)
    return one(dccat[:, :, :SBW]), -one(dccat[:, :, SBW:])


def _gelu(y):
    return 0.5 * y * (1.0 + lax.erf(y * (2.0 ** -0.5)))


def _gelu_grad(y):
    return 0.5 * (1.0 + lax.erf(y * (2.0 ** -0.5))) + y * jnp.exp(-0.5 * y * y) * ((2.0 * math.pi) ** -0.5)


def _sigmoid(z):
    return 0.5 * jnp.tanh(0.5 * z) + 0.5


def _glu_fwd(y, wg):
    def body(y_ref, w_ref, o_ref, z_ref):
        ys = _gelu(y_ref[...])
        z = jnp.dot(ys.astype(BF16), w_ref[...], preferred_element_type=F32)
        z_ref[...] = z
        o_ref[...] = ys * _sigmoid(z)

    row = pl.BlockSpec((TL, SW), lambda i: (i, 0))
    return pl.pallas_call(
        body, name="glu_fwd", grid=(L // TL,),
        in_specs=[row, pl.BlockSpec((SW, SW), lambda i: (0, 0))], out_specs=[row, row],
        out_shape=[jax.ShapeDtypeStruct((L, SW), F32), jax.ShapeDtypeStruct((L, SW), F32)],
        compiler_params=_cp(("parallel",)),
    )(y, wg)


def _glu_bwd(y, z, dout, wg):
    def body(y_ref, z_ref, do_ref, w_ref, dy_ref, dw_ref):
        @pl.when(pl.program_id(0) == 0)
        def _():
            dw_ref[...] = jnp.zeros_like(dw_ref)

        yv = y_ref[...]
        ys = _gelu(yv)
        sg = _sigmoid(z_ref[...])
        dov = do_ref[...]
        dz = (dov * ys * sg * (1.0 - sg)).astype(BF16)
        dys = dov * sg + lax.dot_general(dz, w_ref[...], _NT, preferred_element_type=F32)
        dy_ref[...] = dys * _gelu_grad(yv)
        dw_ref[...] += lax.dot_general(ys.astype(BF16), dz, _TN, preferred_element_type=F32)

    row = pl.BlockSpec((TL, SW), lambda i: (i, 0))
    wsp = pl.BlockSpec((SW, SW), lambda i: (0, 0))
    return pl.pallas_call(
        body, name="glu_bwd", grid=(L // TL,),
        in_specs=[row, row, row, wsp], out_specs=[row, wsp],
        out_shape=[jax.ShapeDtypeStruct((L, SW), F32), jax.ShapeDtypeStruct((SW, SW), F32)],
        compiler_params=_cp(("arbitrary",)),
    )(y, z, dout, wg)


CT = 256
CR = 128
NCT = DFF // CT


def _shifted(ref, r):
    h = 8 * (4 // ref.dtype.itemsize)
    cur = ref[pl.ds(r, CR), :].astype(F32)
    before = ref[pl.ds(pl.multiple_of(jnp.maximum(r - h, 0), h), h), :][h - 1:h, :].astype(F32)
    after = ref[pl.ds(pl.multiple_of(jnp.minimum(r + CR, L - h), h), h), :][0:1, :].astype(F32)
    before = jnp.where(r > 0, before, 0.0)
    after = jnp.where(r + CR < L, after, 0.0)
    row = lax.broadcasted_iota(jnp.int32, cur.shape, 0)
    prev = jnp.where(row == 0, before, pltpu.roll(cur, 1, 0))
    nxt = jnp.where(row == CR - 1, after, pltpu.roll(cur, CR - 1, 0))
    return prev, cur, nxt


def _conv3(ref, r, w_ref, b_ref):
    prev, cur, nxt = _shifted(ref, r)
    return w_ref[0:1, :] * prev + w_ref[1:2, :] * cur + w_ref[2:3, :] * nxt + b_ref[...]


def _convact_fwd(up, conv_w, conv_b):
    def body(ug_ref, uv_ref, wg_ref, wv_ref, bg_ref, bv_ref, o_ref, g_ref, v_ref):
        def chunk(i, _):
            r = pl.multiple_of(i * CR, CR)
            rs = pl.ds(r, CR)
            g = _conv3(ug_ref, r, wg_ref, bg_ref)
            v = _conv3(uv_ref, r, wv_ref, bv_ref)
            o_ref[rs, :] = (g * _sigmoid(g) * v).astype(BF16)
            g_ref[rs, :] = g.astype(BF16)
            v_ref[rs, :] = v.astype(BF16)
            return 0

        lax.fori_loop(0, L // CR, chunk, 0)

    gcol = pl.BlockSpec((L, CT), lambda j: (0, j))
    vcol = pl.BlockSpec((L, CT), lambda j: (0, j + NCT))
    return pl.pallas_call(
        body, name="convact_fwd", grid=(NCT,),
        in_specs=[gcol, vcol,
                  pl.BlockSpec((3, CT), lambda j: (0, j)), pl.BlockSpec((3, CT), lambda j: (0, j + NCT)),
                  pl.BlockSpec((1, CT), lambda j: (0, j)), pl.BlockSpec((1, CT), lambda j: (0, j + NCT))],
        out_specs=[gcol, gcol, gcol], out_shape=[jax.ShapeDtypeStruct((L, DFF), BF16)] * 3,
        compiler_params=_cp(("parallel",)),
    )(up, up, conv_w, conv_w, conv_b, conv_b)


def _convact_bwd(up, gq, vq, dact, conv_w):
    def body(ug_ref, uv_ref, g_ref, v_ref, da_ref, wg_ref, wv_ref, du_ref, dw_ref, db_ref, dgs, dvs, dbv):
        half = pl.program_id(1)

        def transpose_conv(src, u_ref, w_ref):
            dw_ref[...] = jnp.zeros_like(dw_ref)

            def chunk(i, _):
                r = pl.multiple_of(i * CR, CR)
                rs = pl.ds(r, CR)
                prev, cur, nxt = _shifted(src, r)
                du_ref[rs, :] = (w_ref[0:1, :] * nxt + w_ref[1:2, :] * cur + w_ref[2:3, :] * prev).astype(BF16)
                uv = u_ref[rs, :].astype(F32)
                for k, d in enumerate((nxt, cur, prev)):
                    dw_ref[k:k + 1, :] += jnp.sum(d * uv, axis=0, keepdims=True)
                return 0

            lax.fori_loop(0, L // CR, chunk, 0)

        @pl.when(half == 0)
        def _():
            db_ref[...] = jnp.zeros_like(db_ref)
            dbv[...] = jnp.zeros_like(dbv)

            def chunk1(i, _):
                rs = pl.ds(pl.multiple_of(i * CR, CR), CR)
                g = g_ref[rs, :].astype(F32)
                v = v_ref[rs, :].astype(F32)
                sg = _sigmoid(g)
                da = da_ref[rs, :].astype(F32)
                dv = da * g * sg
                dg = da * v * sg * (1.0 + g * (1.0 - sg))
                dgs[rs, :] = dg
                dvs[rs, :] = dv
                db_ref[...] += jnp.sum(dg, axis=0, keepdims=True)
                dbv[0:1, :] += jnp.sum(dv, axis=0, keepdims=True)
                return 0

            lax.fori_loop(0, L // CR, chunk1, 0)
            transpose_conv(dgs, ug_ref, wg_ref)

        @pl.when(half == 1)
        def _():
            db_ref[...] = dbv[0:1, :]
            transpose_conv(dvs, uv_ref, wv_ref)

    def col(rows, off):
        return pl.BlockSpec((rows, CT), lambda j, h: (0, j + off))

    def out(rows):
        return pl.BlockSpec((rows, CT), lambda j, h: (0, j + h * NCT))

    return pl.pallas_call(
        body, name="convact_bwd", grid=(NCT, 2),
        in_specs=[col(L, 0), col(L, NCT), col(L, 0), col(L, 0), col(L, 0), col(3, 0), col(3, NCT)],
        out_specs=[out(L), out(3), out(1)],
        out_shape=[jax.ShapeDtypeStruct((L, 2 * DFF), BF16), jax.ShapeDtypeStruct((3, 2 * DFF), F32),
                   jax.ShapeDtypeStruct((1, 2 * DFF), F32)],
        scratch_shapes=[pltpu.VMEM((L, CT), F32), pltpu.VMEM((L, CT), F32), pltpu.VMEM((8, CT), F32)],
        compiler_params=_cp(("parallel", "arbitrary")),
    )(up, up, gq, vq, dact, conv_w, conv_w)


def _local_step(x, tgt, w_in_t, p, attend, scan, stage):
    tabs = _rope_tables()
    lam_re, lam_im, bb_re, bb_im = _s5_params(p["a_re"], p["a_im"], p["log_step"], p["bt_re"], p["bt_im"])
    bcat = _to_bcat(bb_re, bb_im).astype(BF16)
    ccat = _to_ccat(p["c_re"], p["c_im"]).astype(BF16)
    lam_re4, lam_im4 = lam_re.reshape(2, NSB, 1, SBW), lam_im.reshape(2, NSB, 1, SBW)
    dskip = p["d_skip"].reshape(1, SW)
    g_mix, g_ffn, g_fin = p["norm_mix_g"].reshape(1, D), p["norm_ffn_g"].reshape(1, D), p["norm_final_g"].reshape(1, D)
    g_attn, g_ssm = p["norm_attn_g"].reshape(1, AW), p["norm_ssm_g"].reshape(1, SW)
    sink = p["sink"].reshape(NQ)
    conv_b = p["conv_b"].reshape(1, 2 * DFF)

    rows, gain = jax.ShapeDtypeStruct((L, D), F32), jax.ShapeDtypeStruct((1, D), F32)
    rows16 = jax.ShapeDtypeStruct((L, D), BF16)
    h1, qkv, u = _in_proj(x, g_mix, w_in_t, tabs)
    attn, lse, wts = attend(qkv, sink)
    u_p = _perm(u)
    y_p, states, more = scan(u_p, bcat, ccat, lam_re4, lam_im4, dskip)
    wts = dict(wts, **more)
    w_glu, w_out, w_up_t, w_down, conv_w = (wts[k] for k in ("w_glu", "w_out", "w_up_t", "w_down", "conv_w"))
    ysg_p, z_p = _glu_fwd(y_p, w_glu)
    ysg = _unperm(ysg_p)
    mixed, x1, h2 = _out_proj(attn, ysg, g_attn, g_ssm, w_out, x, g_ffn)
    up = _mm(h2, w_up_t, tb=True, name="ffn_up", tn=1408, out_dtype=BF16)
    act, gq, vq = _convact_fwd(up, conv_w, conv_b)
    loss, dx2, dx2b, dg_fin = _mm(
        act, w_down, add=x1, name="ffn_down", tm=512, tk=DFF,
        post=(_final_post, [tgt, g_fin], [jax.ShapeDtypeStruct((1, 1), F32), rows, rows16, gain]))

    def riding(res, ride):
        return res if ride else (res, None)

    dw_down = _mm(act, dx2b, ta=True, name="ffn_down_dw", tm=256, tk=L)
    ride = stage(("w_down",), "cores", [dw_down])
    dact, got = riding(_mm(dx2b, w_down, tb=True, name="ffn_down_dx", tn=1408, out_dtype=BF16, ride=ride), ride)
    ride = stage(("w_down",), "chips", got)
    dup, dconv_w, dconv_b = _convact_bwd(up, gq, vq, dact, conv_w)
    dw_up_t, got = riding(_mm(dup, h2, ta=True, name="ffn_up_dw", tm=512, tk=L, ride=ride), ride)
    stage(("w_down",), "done", got)
    ride = stage(("w_up_t",), "cores", [dw_up_t])
    (dx1, dx1b, dg_ffn), got = riding(
        _mm(dup, w_up_t, name="ffn_up_dx", tm=512, tk=2 * DFF, ride=ride,
            post=(_rms_bwd_post, [x1, dx2, g_ffn], [rows, rows16, gain])), ride)
    ride = stage(("w_up_t",), "chips", got)
    dattn, dysg, dg_attn, dg_ssm = _out_proj_dx(dx1b, w_out, attn, ysg, g_attn, g_ssm)
    dw_out = _mm(mixed, dx1b, ta=True, name="out_proj_dw", tm=512, tk=L)
    dy_p, dw_glu = _glu_bwd(y_p, z_p, _perm(dysg), w_glu)
    (du_p, dbcat, dccat, dlam_re, dlam_im, dd), got = _s5_bwd(u_p, dy_p, states, bcat, ccat, lam_re4, lam_im4, dskip,
                                                              ride=ride)
    stage(("w_up_t",), "done", got)
    dbb_re, dbb_im = _from_bcat(dbcat)
    dc_re, dc_im = _from_ccat(_swap(dccat))
    mix = ("w_out", "w_glu")
    ride = stage(mix, "cores", [dw_out, dw_glu])
    (dq, dk, dv, dsink), got = _attn_bwd(qkv, sink, attn, lse, dattn, ride=ride)
    ride = stage(mix, "chips", got)
    dproj = _rope_bwd(dq, dk, dv, _unperm(du_p), tabs)
    dw_in_t, got = riding(_mm(dproj, h1, ta=True, name="in_proj_dw", tm=640, tk=L, ride=ride), ride)
    stage(mix, "done", got)
    grad_x, dg_mix = _in_proj_dx(dproj, w_in_t, x, g_mix, dx1)

    big = dict(w_in_t=dw_in_t)
    small = dict(norm_mix_g=dg_mix, norm_attn_g=dg_attn, norm_ssm_g=dg_ssm, norm_ffn_g=dg_ffn, norm_final_g=dg_fin,
                 sink=dsink[:, 0], conv_b=dconv_b, d_skip=dd, conv_w=dconv_w,
                 lam_re=dlam_re.reshape(2, NG, NP), lam_im=dlam_im.reshape(2, NG, NP),
                 bb_re=dbb_re, bb_im=dbb_im, c_re=dc_re, c_im=dc_im, loss=loss.reshape(1))
    return grad_x, big, small


ANY = pl.BlockSpec(memory_space=pl.ANY)


def _coords():
    return lax.axis_index("x"), lax.axis_index("y"), lax.axis_index("c")


def _flip(v, b):
    return v + b - 2 * v * b if b else v


def _all_gather(shards, name):
    n = len(shards)

    def body(*refs):
        _gather_start(refs[:n], refs[n:2 * n], *refs[2 * n:])
        _gather_finish(refs[:n], refs[n:2 * n], *refs[2 * n:])

    return pl.pallas_call(
        body, name=name,
        in_specs=[ANY] * n, out_specs=[ANY] * n,
        out_shape=_gather_shapes(shards), scratch_shapes=_gather_sems(n),
    )(*shards)


def _gather_shapes(shards):
    return [jax.ShapeDtypeStruct((NDEV * s.shape[0], s.shape[1]), s.dtype) for s in shards]


def _gather_sems(n):
    return [pltpu.SemaphoreType.DMA((7 * n,)), pltpu.SemaphoreType.DMA((7 * n,)), pltpu.SemaphoreType.DMA((n,))]


def _gather_copies(ins, outs, send_sems, recv_sems, local_sems, a):
    x, y, c = _coords()
    me, sibling = (x, y, c), (x, y, 1 - c)
    chips = [(1 - x, y), (x, 1 - y), (1 - x, 1 - y)]
    r = ins[a].shape[0]

    def rows(px, py, pc):
        return outs[a].at[pl.ds(pl.multiple_of((4 * px + 2 * py + pc) * r, 8), r), :]

    def copy(k, block, to, src=None):
        return pltpu.make_async_remote_copy(
            src_ref=rows(*block) if src is None else src, dst_ref=rows(*block),
            send_sem=send_sems.at[a * 7 + k], recv_sem=recv_sems.at[a * 7 + k],
            device_id=to, device_id_type=pl.DeviceIdType.MESH)

    mine = pltpu.make_async_copy(ins[a], rows(*me), local_sems.at[a])
    first = [copy(0, me, sibling, src=ins[a])]
    first += [copy(1 + j, me, (*chip, c), src=ins[a]) for j, chip in enumerate(chips)]
    passed = [copy(4 + j, (*chip, c), sibling) for j, chip in enumerate(chips)]
    arrivals = [copy(1 + j, (*chip, c), me) for j, chip in enumerate(chips)]
    from_sibling = [copy(0, sibling, me)] + [copy(4 + j, (*chip, 1 - c), me) for j, chip in enumerate(chips)]
    return mine, first, passed, arrivals, from_sibling


def _gather_start(ins, outs, send_sems, recv_sems, local_sems):
    for a in range(len(ins)):
        mine, first, _, _, _ = _gather_copies(ins, outs, send_sems, recv_sems, local_sems, a)
        mine.start()
        for cp in first:
            cp.start()


def _gather_finish(ins, outs, send_sems, recv_sems, local_sems):
    n = len(ins)
    parts = [_gather_copies(ins, outs, send_sems, recv_sems, local_sems, a) for a in range(n)]
    for mine, first, passed, arrivals, from_sibling in parts:
        for arrived, onward in zip(arrivals, passed):
            arrived.wait_recv()
            onward.start()
    for mine, first, passed, arrivals, from_sibling in parts:
        for cp in from_sibling:
            cp.wait_recv()
        for cp in first + passed:
            cp.wait_send()
        mine.wait()


NCHIP = 4
CHIP_FLIPS = ((1, 0), (0, 1), (1, 1))


def _planned_copies(ins, outs, send_sems, recv_sems, plan):
    return [pltpu.make_async_remote_copy(
        src_ref=ins[a].at[src], dst_ref=outs[a].at[dst], send_sem=send_sems.at[k], recv_sem=recv_sems.at[k],
        device_id=to, device_id_type=pl.DeviceIdType.MESH) for k, (a, src, dst, to) in enumerate(plan)]


def _start_all(copies):
    for cp in copies:
        cp.start()


def _wait_all(copies):
    for cp in copies:
        cp.wait_recv()
    for cp in copies:
        cp.wait_send()


SLOTS = {"cores": NCHIP, "chips": 3}


def _exchange_copies(kind, ins, outs, send_sems, recv_sems):
    x, y, c = _coords()
    plan = []
    for a in range(len(ins)):
        if kind == "cores":
            plan += [(a, 2 * q + 1 - c, q, (x, y, 1 - c)) for q in range(NCHIP)]
        else:
            for j, (fx, fy) in enumerate(CHIP_FLIPS):
                px, py = _flip(x, fx), _flip(y, fy)
                plan.append((a, 2 * px + py, j, (px, py, c)))
    return _planned_copies(ins, outs, send_sems, recv_sems, plan)


def _exchange_shapes(kind, parts):
    return [jax.ShapeDtypeStruct((SLOTS[kind],) + s.shape[1:], s.dtype) for s in parts]


def _exchange_sems(kind, n):
    return [pltpu.SemaphoreType.DMA((SLOTS[kind] * n,)), pltpu.SemaphoreType.DMA((SLOTS[kind] * n,))]


def _exchange(kind, parts, name):
    n = len(parts)

    def body(*refs):
        copies = _exchange_copies(kind, refs[:n], refs[n:2 * n], *refs[2 * n:])
        _start_all(copies)
        _wait_all(copies)

    return pl.pallas_call(
        body, name=name, in_specs=[ANY] * n, out_specs=[ANY] * n,
        out_shape=_exchange_shapes(kind, parts), scratch_shapes=_exchange_sems(kind, n),
    )(*parts)


def _pair_sum(where, part, recv, wire_dtype, name):
    _, r, c = part.shape
    tr = _pick(r, 256, 16)

    def body(w_ref, p_ref, r_ref, pb_ref, own_ref):
        s = p_ref[...] + r_ref[...]
        pb_ref[...] = s.astype(wire_dtype)

        @pl.when(pl.program_id(1) == w_ref[1])
        def _():
            own_ref[...] = s

    return pl.pallas_call(
        body, name=name,
        grid_spec=pltpu.PrefetchScalarGridSpec(
            num_scalar_prefetch=1, grid=(r // tr, NCHIP),
            in_specs=[pl.BlockSpec((None, tr, c), lambda i, q, w: (2 * q + w[0], i, 0)),
                      pl.BlockSpec((None, tr, c), lambda i, q, w: (q, i, 0))],
            out_specs=[pl.BlockSpec((None, tr, c), lambda i, q, w: (q, i, 0)),
                       pl.BlockSpec((tr, c), lambda i, q, w: (i, 0))]),
        out_shape=[jax.ShapeDtypeStruct((NCHIP, r, c), wire_dtype), jax.ShapeDtypeStruct((r, c), F32)],
        compiler_params=_cp(("parallel", "arbitrary")),
    )(where, part, recv)


def _chip_sum(own, recv, name):
    r, c = own.shape
    tr = _pick(r, 256, 16)

    def body(o_ref, r_ref, out_ref):
        acc = o_ref[...]
        for j in range(3):
            acc = acc + r_ref[j].astype(F32)
        out_ref[...] = acc

    return pl.pallas_call(
        body, name=name, grid=(r // tr,),
        in_specs=[pl.BlockSpec((tr, c), lambda i: (i, 0)), pl.BlockSpec((3, tr, c), lambda i: (0, i, 0))],
        out_specs=pl.BlockSpec((tr, c), lambda i: (i, 0)),
        out_shape=jax.ShapeDtypeStruct((r, c), F32),
        compiler_params=_cp(("parallel",)),
    )(own, recv)


def _adamw(w, own, recv, m, v, name):
    r, c = w.shape
    tr = _pick(r, 256, 16)

    def body(w_ref, o_ref, r_ref, m_ref, v_ref, g_ref, d_ref, nm_ref, nv_ref):
        acc = o_ref[...]
        for j in range(3):
            acc = acc + r_ref[j].astype(F32)
        g_ref[...] = acc
        _adamw_refs(w_ref, g_ref, m_ref, v_ref, d_ref, nm_ref, nv_ref)

    blk = pl.BlockSpec((tr, c), lambda i: (i, 0))
    return pl.pallas_call(
        body, name=name, grid=(r // tr,),
        in_specs=[blk, blk, pl.BlockSpec((3, tr, c), lambda i: (0, i, 0)), blk, blk], out_specs=[blk] * 4,
        out_shape=[jax.ShapeDtypeStruct((r, c), F32)] * 4,
        compiler_params=_cp(("parallel",)),
    )(w, own, recv, m, v)


def _adamw_refs(w_ref, g_ref, m_ref, v_ref, d_ref, nm_ref, nv_ref):
    gv = g_ref[...]
    nm = B1 * m_ref[...] + (1.0 - B1) * gv
    nv = B2 * v_ref[...] + (1.0 - B2) * (gv * gv)
    nm_ref[...] = nm
    nv_ref[...] = nv
    d_ref[...] = -LR * ((nm / C1) / (jnp.sqrt(nv / C2) + AEPS) + WD * w_ref[...])


def _adamw_small(ws, gs, ms, vs, name):
    n = len(ws)

    def body(*refs):
        groups = [refs[i * n:(i + 1) * n] for i in range(7)]
        for per_param in zip(*groups):
            _adamw_refs(*per_param)

    vm = pl.BlockSpec(memory_space=pltpu.VMEM)
    outs = pl.pallas_call(
        body, name=name, in_specs=[vm] * (4 * n), out_specs=[vm] * (3 * n),
        out_shape=[jax.ShapeDtypeStruct(a.shape, F32) for a in ws] * 3,
    )(*ws, *gs, *ms, *vs)
    return outs[:n], outs[n:2 * n], outs[2 * n:]


def _swap(a):
    return jnp.swapaxes(a, -1, -2)


VIEWS = {
    "w_in": (lambda a: a[0].T, lambda u: u.T[None]),
    "w_up": (lambda a: a[0].T, lambda u: u.T[None]),
    "w_glu": (lambda a: a[0], lambda u: u[None]),
    "w_out": (lambda a: a[0], lambda u: u[None]),
    "w_down": (lambda a: a[0], lambda u: u[None]),
    "conv_w": (lambda a: a[0], lambda u: u[None]),
    "norm_mix_g": (lambda a: a, lambda u: u),
    "norm_attn_g": (lambda a: a, lambda u: u),
    "norm_ssm_g": (lambda a: a, lambda u: u),
    "norm_ffn_g": (lambda a: a, lambda u: u),
    "norm_final_g": (lambda a: a[None], lambda u: u[0]),
    "conv_b": (lambda a: a, lambda u: u),
    "sink": (lambda a: a, lambda u: u),
    "a_re": (lambda a: a.reshape(2 * NG, NP), lambda u: u.reshape(1, 2, NG, NP)),
    "a_im": (lambda a: a.reshape(2 * NG, NP), lambda u: u.reshape(1, 2, NG, NP)),
    "log_step": (lambda a: a[0], lambda u: u[None]),
    "b_re": (lambda a: _swap(a[0]).reshape(2 * NG * GC, NP), lambda u: _swap(u.reshape(2, NG, GC, NP))[None]),
    "b_im": (lambda a: _swap(a[0]).reshape(2 * NG * GC, NP), lambda u: _swap(u.reshape(2, NG, GC, NP))[None]),
    "c_re": (lambda a: a.reshape(2 * NG * GC, NP), lambda u: u.reshape(1, 2, NG, GC, NP)),
    "c_im": (lambda a: a.reshape(2 * NG * GC, NP), lambda u: u.reshape(1, 2, NG, GC, NP)),
    "d_skip": (lambda a: a[0].T, lambda u: u.T[None]),
}
BIG = ["w_in", "w_glu", "w_out", "w_up", "w_down"]
PACK_W = 1024


def _pack(arrs, rows):
    flat = jnp.concatenate([a.reshape(-1).astype(F32) for a in arrs])
    return jnp.pad(flat, (0, rows * PACK_W - flat.shape[0])).reshape(rows, PACK_W)


def _unpack(packed, shapes):
    flat = packed.reshape(-1)
    out, off = [], 0
    for s in shapes:
        size = math.prod(s)
        out.append(flat[off:off + size].reshape(s))
        off += size
    return out


def kernel(x, norm_mix_g, w_in, a_re, a_im, log_step, b_re, b_im, c_re, c_im, d_skip, w_glu, sink, norm_attn_g, norm_ssm_g, w_out, norm_ffn_g, w_up, conv_w, conv_b, w_down, norm_final_g, loss_target, m_norm_mix_g, m_w_in, m_a_re, m_a_im, m_log_step, m_b_re, m_b_im, m_c_re, m_c_im, m_d_skip, m_w_glu, m_sink, m_norm_attn_g, m_norm_ssm_g, m_w_out, m_norm_ffn_g, m_w_up, m_conv_w, m_conv_b, m_w_down, m_norm_final_g, v_norm_mix_g, v_w_in, v_a_re, v_a_im, v_log_step, v_b_re, v_b_im, v_c_re, v_c_im, v_d_skip, v_w_glu, v_sink, v_norm_attn_g, v_norm_ssm_g, v_w_out, v_norm_ffn_g, v_w_up, v_conv_w, v_conv_b, v_w_down, v_norm_final_g):
    args = dict(locals())
    names = ["norm_mix_g", "w_in", "a_re", "a_im", "log_step", "b_re", "b_im", "c_re", "c_im", "d_skip", "w_glu",
             "sink", "norm_attn_g", "norm_ssm_g", "w_out", "norm_ffn_g", "w_up", "conv_w", "conv_b", "w_down",
             "norm_final_g"]
    w = {k: args[k] for k in names}
    m = {k: args["m_" + k] for k in names}
    v = {k: args["v_" + k] for k in names}

    (w_in_t,) = _all_gather([w_in[0].T.astype(BF16)], "gather_w_in")
    under_attn = dict(w_glu=w_glu[0].astype(BF16), w_out=w_out[0].astype(BF16),
                      conv_w=jnp.pad(conv_w[0], ((0, 5), (0, 0))))
    under_scan = dict(w_up_t=w_up[0].T.astype(BF16), w_down=w_down[0].astype(BF16))

    ax, ay, ac = _coords()
    me = 4 * ax + 2 * ay + ac
    where = jnp.stack([ac, 2 * ax + ay]).astype(jnp.int32)
    parts, own, got = {}, {}, {}

    def split8(g):
        return g.reshape(NDEV, g.shape[0] // NDEV, g.shape[1])

    def attend(qkv, sink_):
        attn, lse, gathered = _attn_fwd(qkv, sink_, gather=list(under_attn.values()))
        wts = dict(zip(under_attn.keys(), gathered))
        wts["conv_w"] = (wts["conv_w"].reshape(NDEV, 8, 2 * DFF // NDEV)[:, :3].transpose(1, 0, 2)
                         .reshape(3, 2 * DFF))
        return attn, lse, wts

    def scan(*operands):
        y_p, states, gathered = _s5_fwd(*operands, gather=list(under_scan.values()))
        return y_p, states, dict(zip(under_scan.keys(), gathered))

    def pair_sum(k, from_core):
        per_chip, own[k] = _pair_sum(where, parts[k], from_core, F32 if k == "small" else BF16, "pair_sum_" + k)
        return per_chip

    def stage(ks, phase, payload):
        if phase == "cores":
            parts.update({k: split8(g) for k, g in zip(ks, payload)})
            return ("cores", [parts[k] for k in ks])
        if phase == "chips":
            return ("chips", [pair_sum(k, fc) for k, fc in zip(ks, payload)])
        got.update(zip(ks, payload))
        return ()

    p = {k: w[k][0] for k in ("norm_mix_g", "a_re", "a_im", "log_step", "c_re", "c_im", "d_skip", "sink",
                              "norm_attn_g", "norm_ssm_g", "norm_ffn_g", "conv_b")}
    p["norm_final_g"] = norm_final_g
    p["bt_re"], p["bt_im"] = _swap(b_re[0]), _swap(b_im[0])
    grad_x, big, small = _local_step(x[0], loss_target[0], w_in_t, p, attend, scan, stage)

    small_names = list(small.keys())
    small_shapes = [small[k].shape for k in small_names]
    n_small = sum(math.prod(s) for s in small_shapes)
    rows_dev = -(-n_small // (PACK_W * NDEV * 16)) * 16
    spack = _pack([small[k] for k in small_names], rows_dev * NDEV)
    late = ["w_in_t", "small"]
    parts.update(w_in_t=split8(big["w_in_t"]), small=spack.reshape(NDEV, rows_dev, PACK_W))
    from_cores = _exchange("cores", [parts[k] for k in late], "exchange_cores")
    from_chips = _exchange("chips", [pair_sum(k, fc) for k, fc in zip(late, from_cores)], "exchange_chips")
    got.update(zip(late, from_chips))
    (small_full,) = _all_gather([_chip_sum(own["small"], got["small"], "chip_sum_small")], "gather_small")
    sm = dict(zip(small_names, _unpack(small_full, small_shapes)))

    _, s5_vjp = jax.vjp(_s5_params, a_re[0], a_im[0], log_step[0], p["bt_re"], p["bt_im"])
    da_re, da_im, dlog_step, dbt_re, dbt_im = s5_vjp((sm["lam_re"], sm["lam_im"], sm["bb_re"], sm["bb_im"]))
    gview = {
        "norm_mix_g": sm["norm_mix_g"], "norm_attn_g": sm["norm_attn_g"], "norm_ssm_g": sm["norm_ssm_g"],
        "norm_ffn_g": sm["norm_ffn_g"], "norm_final_g": sm["norm_final_g"], "conv_b": sm["conv_b"],
        "sink": sm["sink"][None], "a_re": da_re.reshape(2 * NG, NP), "a_im": da_im.reshape(2 * NG, NP),
        "log_step": dlog_step, "b_re": dbt_re.reshape(2 * NG * GC, NP), "b_im": dbt_im.reshape(2 * NG * GC, NP),
        "c_re": sm["c_re"].reshape(2 * NG * GC, NP), "c_im": sm["c_im"].reshape(2 * NG * GC, NP),
        "d_skip": sm["d_skip"].reshape(NG, GC).T,
        "conv_w": lax.dynamic_slice_in_dim(sm["conv_w"], me * (2 * DFF // NDEV), 2 * DFF // NDEV, axis=1),
    }

    dview, mview, vview = {}, {}, {}
    for k, kg in zip(BIG, ("w_in_t", "w_glu", "w_out", "w_up_t", "w_down")):
        to = VIEWS[k][0]
        gview[k], dview[k], mview[k], vview[k] = _adamw(to(w[k]), own[kg], got[kg], to(m[k]), to(v[k]), "adamw_" + k)
    rest = [k for k in names if k not in BIG]
    outs = _adamw_small([VIEWS[k][0](w[k]) for k in rest], [gview[k] for k in rest],
                        [VIEWS[k][0](m[k]) for k in rest], [VIEWS[k][0](v[k]) for k in rest], "adamw_small")
    for dst, vals in zip((dview, mview, vview), outs):
        dst.update(dict(zip(rest, vals)))

    def back(views):
        return [VIEWS[k][1](views[k]) for k in names]

    return (sm["loss"][0], grad_x[None], *back(gview), *back(dview), *back(mview), *back(vview))
```

```python
import functools
import math

import jax
import jax.numpy as jnp
from jax import lax
from jax.experimental import pallas as pl
from jax.experimental.pallas import tpu as pltpu

F32 = jnp.float32
BF16 = jnp.bfloat16

L = 4096
D = 1024
NQ, NKV, HD = 8, 2, 64
AW = NQ * HD
KVW = NKV * HD
SW = 512
NG, GC, NP = 32, 16, 64
INW = AW + 2 * KVW + SW
DFF = 2816
BLK = 128
WIN = 3 * BLK
EPS = 1e-6
ROPE_THETA = 500000.0
NSEG = 32
TSEG = L // NSEG
SBW = 256
NSB = NG * NP // SBW
NDEV = 8
MESH_AXES = ("x", "y", "c")

LR, B1, B2, AEPS, WD, STEP = 0.001, 0.9, 0.999, 1e-08, 0.01, 10
C1 = 1.0 - B1 ** STEP
C2 = 1.0 - B2 ** STEP

VMEM_LIMIT = 56 * 1024 * 1024


def _pick(n, target, mult):
    best = None
    for t in range(mult, min(n, target) + 1, mult):
        if n % t == 0:
            best = t
    return best if best is not None else n


def _cp(sem):
    return pltpu.CompilerParams(dimension_semantics=sem, vmem_limit_bytes=VMEM_LIMIT)


def _mm(a, b, *, ta=False, tb=False, out_dtype=F32, add=None, ride=(), post=None, name, tm=1024, tn=1024, tk=1024):
    m, k = (a.shape[1], a.shape[0]) if ta else a.shape
    n = b.shape[0] if tb else b.shape[1]
    assert k == (b.shape[1] if tb else b.shape[0])
    tm, tn, tk = _pick(m, tm, 128), _pick(n, tn, 128), _pick(k, tk, 128)
    grid = (m // tm, n // tn, k // tk)
    nk = grid[2]
    dn = (((0 if ta else 1,), (1 if tb else 0,)), ((), ()))
    n_in = 2 + (add is not None)
    kind, riding = ride if ride else (None, ())
    nr = len(riding)
    post_fn, post_ins, post_outs = post if post is not None else (None, (), ())
    n_pi = len(post_ins)
    n_out = len(post_outs) if post is not None else 1
    assert post is None or grid[1] == 1

    def body(*refs):
        a_ref, b_ref = refs[0], refs[1]
        pin = refs[n_in:n_in + n_pi]
        base = n_in + n_pi + nr
        o_refs = refs[base:base + n_out]
        acc_ref = refs[base + n_out + nr]
        step = [pl.program_id(d) for d in range(3)]
        kk = step[2]
        if nr:
            riders = (refs[n_in + n_pi:base], refs[base + n_out:base + n_out + nr], *refs[base + n_out + nr + 1:])

            @pl.when((step[0] == 0) & (step[1] == 0) & (kk == 0))
            def _():
                _start_all(_exchange_copies(kind, *riders))

        prod = lax.dot_general(a_ref[...].astype(BF16), b_ref[...].astype(BF16), dn, preferred_element_type=F32)

        def finish(r):
            if add is not None:
                r = r + refs[2][...]
            if post_fn is None:
                o_refs[0][...] = r.astype(out_dtype)
            else:
                post_fn(r, step[0], pin, o_refs)

        if nk == 1:
            finish(prod)
        else:
            @pl.when(kk == 0)
            def _():
                acc_ref[...] = prod

            @pl.when((kk > 0) & (kk < nk - 1))
            def _():
                acc_ref[...] += prod

            @pl.when(kk == nk - 1)
            def _():
                finish(acc_ref[...] + prod)

        if nr:
            @pl.when((step[0] == grid[0] - 1) & (step[1] == grid[1] - 1) & (kk == nk - 1))
            def _():
                _wait_all(_exchange_copies(kind, *riders))

    a_spec = pl.BlockSpec((tk, tm), lambda i, j, kk: (kk, i)) if ta else pl.BlockSpec((tm, tk), lambda i, j, kk: (i, kk))
    b_spec = pl.BlockSpec((tn, tk), lambda i, j, kk: (j, kk)) if tb else pl.BlockSpec((tk, tn), lambda i, j, kk: (kk, j))
    def row_spec(shape):
        return pl.BlockSpec((tm if shape[0] == m else shape[0], shape[1]),
                            (lambda i, j, kk: (i, 0)) if shape[0] == m else (lambda i, j, kk: (0, 0)))

    in_specs = [a_spec, b_spec]
    args = [a, b]
    if add is not None:
        in_specs.append(pl.BlockSpec((tm, tn), lambda i, j, kk: (i, j)))
        args.append(add)
    if post is None:
        main_specs = [pl.BlockSpec((tm, tn), lambda i, j, kk: (i, j))]
        main_shapes = [jax.ShapeDtypeStruct((m, n), out_dtype)]
    else:
        main_specs = [row_spec(s.shape) for s in post_outs]
        main_shapes = list(post_outs)
    outs = pl.pallas_call(
        body, name=name, grid=grid,
        in_specs=in_specs + [row_spec(p.shape) for p in post_ins] + [ANY] * nr,
        out_specs=main_specs + [ANY] * nr,
        out_shape=main_shapes + (_exchange_shapes(kind, riding) if nr else []),
        scratch_shapes=[pltpu.VMEM((tm, tn) if nk > 1 else (8, 128), F32)] + (_exchange_sems(kind, nr) if nr else []),
        compiler_params=_cp(("arbitrary",) * 3 if (nr or post is not None) else ("parallel", "parallel", "arbitrary")),
    )(*args, *post_ins, *riding)
    main = outs[0] if post is None else list(outs[:n_out])
    return (main, list(outs[n_out:])) if nr else main


TL = 512


def _rms(xv, gv):
    return xv * lax.rsqrt(jnp.mean(xv * xv, axis=-1, keepdims=True) + EPS) * gv


def _rows(width):
    return pl.BlockSpec((TL, width), lambda i: (i, 0))


def _whole(shape):
    return pl.BlockSpec(shape, lambda i: (0,) * len(shape))


def _in_proj(x, g, w_in_t, tabs):
    qkw = AW + 2 * KVW

    def body(x_ref, g_ref, w_ref, c_ref, sa_ref, sb_ref, h_ref, qkv_ref, u_ref):
        h = _rms(x_ref[...], g_ref[...]).astype(BF16)
        h_ref[...] = h
        proj = lax.dot_general(h, w_ref[...], _NT, preferred_element_type=F32)
        for j in range(qkw // 128):
            cols = slice(j * 128, (j + 1) * 128)
            xv = proj[:, cols]
            if j < (AW + KVW) // 128:
                xv = _rope(xv, c_ref[...], sa_ref[...], sb_ref[...], 1.0)
            qkv_ref[:, cols] = xv.astype(BF16)
        u_ref[...] = proj[:, qkw:]

    return pl.pallas_call(
        body, name="in_proj", grid=(L // TL,),
        in_specs=[_rows(D), _whole((1, D)), _whole((INW, D)), _rows(128), _rows(128), _rows(128)],
        out_specs=[_rows(D), _rows(qkw), _rows(SW)],
        out_shape=[jax.ShapeDtypeStruct((L, D), BF16), jax.ShapeDtypeStruct((L, qkw), BF16),
                   jax.ShapeDtypeStruct((L, SW), F32)],
        compiler_params=_cp(("parallel",)),
    )(x, g, w_in_t, *tabs)


def _in_proj_dx(dproj, w_in_t, x, g, dres):
    def body(dp_ref, w_ref, x_ref, g_ref, dres_ref, dx_ref, dg_ref):
        dh = jnp.dot(dp_ref[...], w_ref[...], preferred_element_type=F32)
        dx, dg = _rms_bwd_tile(x_ref[...], g_ref[...], dh)
        dx_ref[...] = dx + dres_ref[...]

        @pl.when(pl.program_id(0) == 0)
        def _():
            dg_ref[...] = jnp.zeros_like(dg_ref)

        dg_ref[...] += dg

    return pl.pallas_call(
        body, name="in_proj_dx", grid=(L // TL,),
        in_specs=[_rows(INW), _whole((INW, D)), _rows(D), _whole((1, D)), _rows(D)],
        out_specs=[_rows(D), _whole((1, D))],
        out_shape=[jax.ShapeDtypeStruct((L, D), F32), jax.ShapeDtypeStruct((1, D), F32)],
        compiler_params=_cp(("arbitrary",)),
    )(dproj, w_in_t, x, g, dres)


def _rms_bwd_tile(xv, gv, dh):
    r = lax.rsqrt(jnp.mean(xv * xv, axis=-1, keepdims=True) + EPS)
    a = dh * gv
    dx = r * a - xv * (r * r * r) * jnp.mean(a * xv, axis=-1, keepdims=True)
    dg = jnp.sum(dh * xv * r, axis=0, keepdims=True)
    return dx, dg


def _rms_bwd_post(dh, i, ins, outs):
    x_ref, dres_ref, g_ref = ins
    dx_ref, dxb_ref, dg_ref = outs
    dx, dg = _rms_bwd_tile(x_ref[...], g_ref[...], dh)
    dx = dx + dres_ref[...]
    dx_ref[...] = dx
    dxb_ref[...] = dx.astype(BF16)

    @pl.when(i == 0)
    def _():
        dg_ref[...] = jnp.zeros_like(dg_ref)

    dg_ref[...] += dg


def _final_post(xv, i, ins, outs):
    t_ref, g_ref = ins
    loss_ref, dx_ref, dxb_ref, dg_ref = outs

    @pl.when(i == 0)
    def _():
        loss_ref[...] = jnp.zeros_like(loss_ref)
        dg_ref[...] = jnp.zeros_like(dg_ref)

    gv = g_ref[...]
    r = lax.rsqrt(jnp.mean(xv * xv, axis=-1, keepdims=True) + EPS)
    e = xv * r * gv - t_ref[...]
    loss_ref[...] += 0.5 * jnp.sum(jnp.mean(e * e, axis=-1, keepdims=True), axis=0, keepdims=True)
    dy = e * (1.0 / D)
    a = dy * gv
    dx = r * a - xv * (r * r * r) * jnp.mean(a * xv, axis=-1, keepdims=True)
    dx_ref[...] = dx
    dxb_ref[...] = dx.astype(BF16)
    dg_ref[...] += jnp.sum(dy * xv * r, axis=0, keepdims=True)


def _out_proj(attn, ysg, ga, gs, w_out, x, gf):
    def body(a_ref, s_ref, ga_ref, gs_ref, w_ref, x_ref, gf_ref, m_ref, x1_ref, h2_ref):
        m_ref[:, 0:AW] = _rms(a_ref[...], ga_ref[...]).astype(BF16)
        m_ref[:, AW:AW + SW] = _rms(s_ref[...], gs_ref[...]).astype(BF16)
        x1 = jnp.dot(m_ref[...], w_ref[...], preferred_element_type=F32) + x_ref[...]
        x1_ref[...] = x1
        h2_ref[...] = _rms(x1, gf_ref[...]).astype(BF16)

    return pl.pallas_call(
        body, name="out_proj", grid=(L // TL,),
        in_specs=[_rows(AW), _rows(SW), _whole((1, AW)), _whole((1, SW)), _whole((D, D)), _rows(D), _whole((1, D))],
        out_specs=[_rows(D), _rows(D), _rows(D)],
        out_shape=[jax.ShapeDtypeStruct((L, D), BF16), jax.ShapeDtypeStruct((L, D), F32),
                   jax.ShapeDtypeStruct((L, D), BF16)],
        compiler_params=_cp(("parallel",)),
    )(attn, ysg, ga, gs, w_out, x, gf)


def _out_proj_dx(dx1, w_out, attn, ysg, ga, gs):
    def body(dx_ref, w_ref, a_ref, s_ref, ga_ref, gs_ref, da_ref, ds_ref, dga_ref, dgs_ref):
        @pl.when(pl.program_id(0) == 0)
        def _():
            dga_ref[...] = jnp.zeros_like(dga_ref)
            dgs_ref[...] = jnp.zeros_like(dgs_ref)

        dm = lax.dot_general(dx_ref[...].astype(BF16), w_ref[...], _NT, preferred_element_type=F32)
        dxa, dga = _rms_bwd_tile(a_ref[...], ga_ref[...], dm[:, 0:AW])
        da_ref[...] = dxa
        dga_ref[...] += dga
        dxs, dgs = _rms_bwd_tile(s_ref[...], gs_ref[...], dm[:, AW:AW + SW])
        ds_ref[...] = dxs
        dgs_ref[...] += dgs

    return pl.pallas_call(
        body, name="out_proj_dx", grid=(L // TL,),
        in_specs=[_rows(D), _whole((D, D)), _rows(AW), _rows(SW), _whole((1, AW)), _whole((1, SW))],
        out_specs=[_rows(AW), _rows(SW), _whole((1, AW)), _whole((1, SW))],
        out_shape=[jax.ShapeDtypeStruct((L, AW), F32), jax.ShapeDtypeStruct((L, SW), F32),
                   jax.ShapeDtypeStruct((1, AW), F32), jax.ShapeDtypeStruct((1, SW), F32)],
        compiler_params=_cp(("arbitrary",)),
    )(dx1, w_out, attn, ysg, ga, gs)


def _rope_tables():
    half = HD // 8
    inv_freq = jnp.power(ROPE_THETA, -jnp.arange(half, dtype=F32) / half)
    ang = jnp.arange(L, dtype=F32)[:, None] * inv_freq[None, :]
    cos, sin = jnp.cos(ang), jnp.sin(ang)
    one = jnp.ones((L, HD - 2 * half), F32)
    zero = jnp.zeros((L, HD - 2 * half), F32)
    zh = jnp.zeros((L, half), F32)
    cos64 = jnp.concatenate([cos, cos, one], axis=1)
    sa64 = jnp.concatenate([-sin, zh, zero], axis=1)
    sb64 = jnp.concatenate([zh, sin, zero], axis=1)
    return [jnp.tile(t, (1, 2)) for t in (cos64, sa64, sb64)]


def _rope(xv, cosv, sav, sbv, sign):
    return xv * cosv + sign * (pltpu.roll(xv, 120, 1) * sav + pltpu.roll(xv, 8, 1) * sbv)


def _rope_bwd(dq, dk, dv, du, tabs):
    def body(dq_ref, dk_ref, dv_ref, du_ref, c_ref, sa_ref, sb_ref, o_ref):
        for j in range(AW // 128):
            cols = slice(j * 128, (j + 1) * 128)
            o_ref[:, cols] = _rope(dq_ref[:, cols], c_ref[...], sa_ref[...], sb_ref[...], -1.0).astype(BF16)
        o_ref[:, AW:AW + KVW] = _rope(dk_ref[...], c_ref[...], sa_ref[...], sb_ref[...], -1.0).astype(BF16)
        o_ref[:, AW + KVW:AW + 2 * KVW] = dv_ref[...].astype(BF16)
        o_ref[:, AW + 2 * KVW:] = du_ref[...].astype(BF16)

    def row(width):
        return pl.BlockSpec((TL, width), lambda i: (i, 0))

    return pl.pallas_call(
        body, name="rope_bwd", grid=(L // TL,),
        in_specs=[row(AW), row(KVW), row(KVW), row(SW), row(128), row(128), row(128)],
        out_specs=row(INW), out_shape=jax.ShapeDtypeStruct((L, INW), BF16),
        compiler_params=_cp(("parallel",)),
    )(dq, dk, dv, du, *tabs)


def _attn_window(n):
    start = pl.multiple_of(jnp.clip((n - 1) * BLK, 0, L - WIN), BLK)
    qpos = n * BLK + lax.broadcasted_iota(jnp.int32, (BLK, WIN), 0)
    kpos = start + lax.broadcasted_iota(jnp.int32, (BLK, WIN), 1)
    return start, jnp.abs(kpos - qpos) <= BLK


_NT = (((1,), (1,)), ((), ()))
_TN = (((0,), (0,)), ((), ()))
NEG = -1e30


def _attn_fwd(qkv, sink, gather=()):
    ng = len(gather)

    def body(sink_ref, q_ref, k_ref, v_ref, *rest):
        o_ref, lse_ref = rest[ng], rest[ng + 1]
        s_scr, p_scr = rest[2 * ng + 2], rest[2 * ng + 3]
        n = pl.program_id(0)
        if ng:
            travellers = (rest[:ng], rest[ng + 2:2 * ng + 2], *rest[2 * ng + 4:])
            _gather_under(n, L // BLK, travellers)

        start, valid = _attn_window(n)
        kw = k_ref[pl.ds(start, WIN), :]
        vw = v_ref[pl.ds(start, WIN), :]
        for h in range(NQ):
            kv = h // (NQ // NKV)
            s_scr[h] = lax.dot_general(q_ref[:, h * HD:(h + 1) * HD], kw[:, kv * HD:(kv + 1) * HD], _NT,
                                       preferred_element_type=F32)
        for h in range(NQ):
            s = jnp.where(valid, s_scr[h] * (HD ** -0.5), NEG)
            sk = sink_ref[h]
            m = jnp.maximum(jnp.max(s, axis=-1, keepdims=True), sk)
            p = jnp.exp(s - m)
            den = jnp.sum(p, axis=-1, keepdims=True) + jnp.exp(sk - m)
            p_scr[h] = (p / den).astype(BF16)
            lse_ref[:, h:h + 1] = m + jnp.log(den)
        for h in range(NQ):
            kv = h // (NQ // NKV)
            o_ref[:, h * HD:(h + 1) * HD] = jnp.dot(p_scr[h], vw[:, kv * HD:(kv + 1) * HD],
                                                    preferred_element_type=F32)
        if ng:
            _gather_done(n, L // BLK, travellers)

    outs = pl.pallas_call(
        body, name="attn_fwd", grid=(L // BLK,),
        in_specs=[pl.BlockSpec(memory_space=pltpu.SMEM),
                  pl.BlockSpec((BLK, AW), lambda n: (n, 0)),
                  pl.BlockSpec((L, KVW), lambda n: (0, AW // KVW)),
                  pl.BlockSpec((L, KVW), lambda n: (0, AW // KVW + 1))] + [ANY] * ng,
        out_specs=[pl.BlockSpec((BLK, AW), lambda n: (n, 0)), pl.BlockSpec((BLK, NQ), lambda n: (n, 0))] + [ANY] * ng,
        out_shape=[jax.ShapeDtypeStruct((L, AW), F32), jax.ShapeDtypeStruct((L, NQ), F32)] + _gather_shapes(gather),
        scratch_shapes=[pltpu.VMEM((NQ, BLK, WIN), F32), pltpu.VMEM((NQ, BLK, WIN), BF16)]
        + (_gather_sems(ng) if ng else []),
        compiler_params=_cp(("arbitrary",) if ng else ("parallel",)),
    )(sink, qkv, qkv, qkv, *gather)
    return outs[0], outs[1], list(outs[2:])


def _attn_bwd(qkv, sink, attn, lse, dattn, ride=()):
    kind, riding = ride if ride else (None, ())
    nr = len(riding)

    def body(*refs):
        if nr:
            riders = (refs[7:7 + nr], refs[11 + nr:11 + 2 * nr], *refs[15 + 2 * nr:])

            @pl.when(pl.program_id(0) == 0)
            def _():
                _start_all(_exchange_copies(kind, *riders))

        compute(*refs[:7], *refs[7 + nr:11 + nr], *refs[11 + 2 * nr:15 + 2 * nr])
        if nr:
            @pl.when(pl.program_id(0) == L // BLK - 1)
            def _():
                _wait_all(_exchange_copies(kind, *riders))

    def compute(sink_ref, q_ref, k_ref, v_ref, o_ref, lse_ref, do_ref, dq_ref, dk_ref, dv_ref, dsink_ref,
                s_scr, dp_scr, p_scr, ds_scr):
        n = pl.program_id(0)

        @pl.when(n == 0)
        def _():
            dk_ref[...] = jnp.zeros_like(dk_ref)
            dv_ref[...] = jnp.zeros_like(dv_ref)
            dsink_ref[...] = jnp.zeros_like(dsink_ref)

        start, valid = _attn_window(n)
        kw = k_ref[pl.ds(start, WIN), :]
        vw = v_ref[pl.ds(start, WIN), :]
        group = NQ // NKV
        for h in range(NQ):
            kv = h // group
            s_scr[h] = lax.dot_general(q_ref[:, h * HD:(h + 1) * HD], kw[:, kv * HD:(kv + 1) * HD], _NT,
                                       preferred_element_type=F32)
            dp_scr[h] = lax.dot_general(do_ref[:, h * HD:(h + 1) * HD].astype(BF16), vw[:, kv * HD:(kv + 1) * HD],
                                        _NT, preferred_element_type=F32)
        for h in range(NQ):
            dd = jnp.sum(do_ref[:, h * HD:(h + 1) * HD] * o_ref[:, h * HD:(h + 1) * HD], axis=-1, keepdims=True)
            lse_h = lse_ref[:, h:h + 1]
            p = jnp.where(valid, jnp.exp(s_scr[h] * (HD ** -0.5) - lse_h), 0.0)
            p_scr[h] = p.astype(BF16)
            ds_scr[h] = (p * (dp_scr[h] - dd) * (HD ** -0.5)).astype(BF16)
            dsk = -jnp.sum(jnp.exp(sink_ref[h] - lse_h) * dd, axis=0, keepdims=True)
            dsink_ref[h:h + 1, :] += jnp.broadcast_to(dsk, (1, 128))
        for kv in range(NKV):
            kh = kw[:, kv * HD:(kv + 1) * HD]
            dk_acc = jnp.zeros((WIN, HD), F32)
            dv_acc = jnp.zeros((WIN, HD), F32)
            for h in range(kv * group, (kv + 1) * group):
                dq_ref[:, h * HD:(h + 1) * HD] = jnp.dot(ds_scr[h], kh, preferred_element_type=F32)
                dk_acc += lax.dot_general(ds_scr[h], q_ref[:, h * HD:(h + 1) * HD], _TN, preferred_element_type=F32)
                dv_acc += lax.dot_general(p_scr[h], do_ref[:, h * HD:(h + 1) * HD].astype(BF16), _TN,
                                          preferred_element_type=F32)
            dk_ref[pl.ds(start, WIN), kv * HD:(kv + 1) * HD] += dk_acc
            dv_ref[pl.ds(start, WIN), kv * HD:(kv + 1) * HD] += dv_acc

    qblk = pl.BlockSpec((BLK, AW), lambda n: (n, 0))
    full = pl.BlockSpec((L, KVW), lambda n: (0, 0))
    outs = pl.pallas_call(
        body, name="attn_bwd", grid=(L // BLK,),
        in_specs=[pl.BlockSpec(memory_space=pltpu.SMEM), qblk,
                  pl.BlockSpec((L, KVW), lambda n: (0, AW // KVW)),
                  pl.BlockSpec((L, KVW), lambda n: (0, AW // KVW + 1)),
                  qblk, pl.BlockSpec((BLK, NQ), lambda n: (n, 0)), qblk] + [ANY] * nr,
        out_specs=[qblk, full, full, pl.BlockSpec((NQ, 128), lambda n: (0, 0))] + [ANY] * nr,
        out_shape=[jax.ShapeDtypeStruct((L, AW), F32), jax.ShapeDtypeStruct((L, KVW), F32),
                   jax.ShapeDtypeStruct((L, KVW), F32), jax.ShapeDtypeStruct((NQ, 128), F32)]
        + (_exchange_shapes(kind, riding) if nr else []),
        scratch_shapes=[pltpu.VMEM((NQ, BLK, WIN), F32), pltpu.VMEM((NQ, BLK, WIN), F32),
                        pltpu.VMEM((NQ, BLK, WIN), BF16), pltpu.VMEM((NQ, BLK, WIN), BF16)]
        + (_exchange_sems(kind, nr) if nr else []),
        compiler_params=_cp(("arbitrary",)),
    )(sink, qkv, qkv, qkv, attn, lse, dattn, *riding)
    return list(outs[:4]), list(outs[4:])


def _perm(a):
    return a.reshape(NSEG, TSEG, a.shape[1]).transpose(1, 0, 2).reshape(L, a.shape[1])


def _unperm(a):
    return a.reshape(TSEG, NSEG, a.shape[1]).transpose(1, 0, 2).reshape(L, a.shape[1])


def _cmul(ar, ai, br, bi):
    return ar * br - ai * bi, ar * bi + ai * br


def _scan_inplace(s_ref, lr, li, rev, visit=None, carried=(), out_ref=None):
    n = lr.shape[1]
    lr8 = jnp.broadcast_to(lr, (NSEG, n))
    li8 = jnp.broadcast_to(li, (NSEG, n))

    def rows(k):
        return pl.ds(pl.multiple_of(jnp.where(rev, TSEG - 1 - k, k) * NSEG, NSEG), NSEG)

    def step(k, c, store):
        sr, si = c
        rs = rows(k)
        pr, pi = _cmul(lr8, li8, sr, si)
        nr = pr + s_ref[rs, 0:n]
        ni = pi + s_ref[rs, n:2 * n]
        if store:
            dst = s_ref if out_ref is None else out_ref
            dst[rs, 0:n] = nr.astype(dst.dtype)
            dst[rs, n:2 * n] = ni.astype(dst.dtype)
        return nr, ni

    z = jnp.zeros((NSEG, n), F32)
    er, ei = lax.fori_loop(0, TSEG, functools.partial(step, store=False), (z, z))
    pr, pi = lr, li
    for _ in range(int(math.log2(TSEG))):
        pr, pi = _cmul(pr, pi, pr, pi)

    seg = lax.broadcasted_iota(jnp.int32, (NSEG, n), 0)

    def chain(order):
        cr = jnp.zeros((1, n), F32)
        ci = jnp.zeros((1, n), F32)
        outr = jnp.zeros((NSEG, n), F32)
        outi = jnp.zeros((NSEG, n), F32)
        for s in order:
            outr = jnp.where(seg == s, cr, outr)
            outi = jnp.where(seg == s, ci, outi)
            mr, mi = _cmul(pr, pi, cr, ci)
            cr, ci = mr + er[s:s + 1], mi + ei[s:s + 1]
        return outr, outi

    fr, fi = chain(range(NSEG))
    rr, ri = chain(range(NSEG - 1, -1, -1))
    cin_r = jnp.where(rev, rr, fr)
    cin_i = jnp.where(rev, ri, fi)
    if visit is None:
        lax.fori_loop(0, TSEG, functools.partial(step, store=True), (cin_r, cin_i))
        return cin_r, cin_i

    def visited(k, c):
        nr, ni = step(k, c[:2], True)
        return (nr, ni) + tuple(visit(k, nr, ni, c[2:]))

    fin = lax.fori_loop(0, TSEG - 1, visited, (cin_r, cin_i) + tuple(carried))
    last_r, last_i = step(TSEG - 1, fin[:2], True)
    return last_r, last_i, fin[2:]


S5_RC = 512


def _s5_specs():
    u_spec = pl.BlockSpec((L, 128), lambda cb, h, d: (0, cb))
    b_spec = pl.BlockSpec((None, None, 128, 2 * SBW), lambda cb, h, d: (d, cb * 2 + h, 0, 0))
    c_spec = pl.BlockSpec((None, None, 2 * SBW, 128), lambda cb, h, d: (d, cb * 2 + h, 0, 0))
    l_spec = pl.BlockSpec((None, None, 1, SBW), lambda cb, h, d: (d, cb * 2 + h, 0, 0))
    d_spec = pl.BlockSpec((1, 128), lambda cb, h, d: (0, cb))
    return u_spec, b_spec, c_spec, l_spec, d_spec


def _s5_fwd(u_p, bcat, ccat, lam_re, lam_im, dskip, gather=()):
    ng = len(gather)
    grid = (SW // 128, 2, 2)

    def body(*refs):
        if ng:
            step = (pl.program_id(0) * grid[1] + pl.program_id(1)) * grid[2] + pl.program_id(2)
            travellers = (refs[6:6 + ng], refs[8 + ng:8 + 2 * ng], *refs[9 + 2 * ng:])
            _gather_under(step, math.prod(grid), travellers)
        compute(*refs[:6], *refs[6 + ng:8 + ng], refs[8 + 2 * ng])
        if ng:
            _gather_done(step, math.prod(grid), travellers)

    def compute(u_ref, b_ref, c_ref, lr_ref, li_ref, d_ref, y_ref, sb_ref, s_scr):
        first = (pl.program_id(1) == 0) & (pl.program_id(2) == 0)

        def proj(i, _):
            rs = pl.ds(pl.multiple_of(i * S5_RC, S5_RC), S5_RC)
            s_scr[rs, :] = jnp.dot(u_ref[rs, :].astype(BF16), b_ref[...], preferred_element_type=F32)
            return 0

        lax.fori_loop(0, L // S5_RC, proj, 0)
        _scan_inplace(s_scr, lr_ref[...], li_ref[...], pl.program_id(2) == 1, out_ref=sb_ref)

        def out(i, _):
            rs = pl.ds(pl.multiple_of(i * S5_RC, S5_RC), S5_RC)
            yv = jnp.dot(sb_ref[rs, :], c_ref[...], preferred_element_type=F32)

            @pl.when(first)
            def _():
                y_ref[rs, :] = d_ref[...] * u_ref[rs, :] + yv

            @pl.when(jnp.logical_not(first))
            def _():
                y_ref[rs, :] += yv

            return 0

        lax.fori_loop(0, L // S5_RC, out, 0)

    u_spec, b_spec, c_spec, l_spec, d_spec = _s5_specs()
    outs = pl.pallas_call(
        body, name="s5_fwd", grid=grid,
        in_specs=[u_spec, b_spec, c_spec, l_spec, l_spec, d_spec] + [ANY] * ng,
        out_specs=[u_spec, _s5_state_spec()] + [ANY] * ng,
        out_shape=[jax.ShapeDtypeStruct((L, SW), F32), jax.ShapeDtypeStruct((2, NSB, L, 2 * SBW), BF16)]
        + _gather_shapes(gather),
        scratch_shapes=[pltpu.VMEM((L, 2 * SBW), F32)] + (_gather_sems(ng) if ng else []),
        compiler_params=_cp(("arbitrary",) * 3 if ng else ("parallel", "arbitrary", "arbitrary")),
    )(u_p, bcat, ccat, lam_re, lam_im, dskip, *gather)
    return outs[0], outs[1], list(outs[2:])


def _s5_state_spec():
    return pl.BlockSpec((None, None, L, 2 * SBW), lambda cb, h, d: (d, cb * 2 + h, 0, 0))


def _s5_bwd(u_p, dy_p, states, bcat, ccat, lam_re, lam_im, dskip, ride=()):
    kind, riding = ride if ride else (None, ())
    nr = len(riding)
    grid = (SW // 128, 2, 2)

    def body(*refs):
        work = refs[:8] + refs[8 + nr:14 + nr] + refs[14 + 2 * nr:16 + 2 * nr]
        if nr:
            riders = (refs[8:8 + nr], refs[14 + nr:14 + 2 * nr], *refs[16 + 2 * nr:])
            step = [pl.program_id(d) for d in range(3)]

            @pl.when((step[0] == 0) & (step[1] == 0) & (step[2] == 0))
            def _():
                _start_all(_exchange_copies(kind, *riders))

        compute(*work)
        if nr:
            @pl.when((step[0] == grid[0] - 1) & (step[1] == grid[1] - 1) & (step[2] == grid[2] - 1))
            def _():
                _wait_all(_exchange_copies(kind, *riders))

    def compute(u_ref, dy_ref, s_ref, b_ref, c_ref, lr_ref, li_ref, d_ref,
                du_ref, db_ref, dc_ref, dlr_ref, dli_ref, dd_ref, g_scr, gb_scr):
        first = (pl.program_id(1) == 0) & (pl.program_id(2) == 0)
        rev = pl.program_id(2) == 1

        def dstate(i, _):
            rs = pl.ds(pl.multiple_of(i * S5_RC, S5_RC), S5_RC)
            g_scr[rs, :] = lax.dot_general(dy_ref[rs, :].astype(BF16), c_ref[...], _NT, preferred_element_type=F32)
            return 0

        lax.fori_loop(0, L // S5_RC, dstate, 0)

        def before(rows):
            sv = s_ref[rows, :].astype(F32)
            return sv[:, 0:SBW], sv[:, SBW:2 * SBW]

        def dlam(gr, gi, sr, si, ar, ai):
            return ar + gr * sr + gi * si, ai + gi * sr - gr * si

        def visit(k, gr, gi, acc):
            ts = jnp.where(rev, k + 1, TSEG - 2 - k)
            sr, si = before(pl.ds(pl.multiple_of(ts * NSEG, NSEG), NSEG))
            return dlam(gr, gi, sr, si, *acc)

        z = jnp.zeros((NSEG, SBW), F32)
        gr, gi, acc = _scan_inplace(g_scr, lr_ref[...], -li_ref[...], jnp.logical_not(rev), visit, (z, z), gb_scr)
        edge_r, edge_i = before(pl.ds(pl.multiple_of(jnp.where(rev, 0, TSEG - 1) * NSEG, NSEG), NSEG))
        seg = lax.broadcasted_iota(jnp.int32, (NSEG, SBW), 0)
        keep = seg != jnp.where(rev, NSEG - 1, 0)

        def neighbour(e):
            return jnp.where(keep, jnp.where(rev, pltpu.roll(e, NSEG - 1, 0), pltpu.roll(e, 1, 0)), 0.0)

        ar, ai = dlam(gr, gi, neighbour(edge_r), neighbour(edge_i), *acc)
        dlr_ref[...] = jnp.sum(ar, axis=0, keepdims=True)
        dli_ref[...] = jnp.sum(ai, axis=0, keepdims=True)

        db_ref[...] = jnp.zeros_like(db_ref)
        dc_ref[...] = jnp.zeros_like(dc_ref)

        @pl.when(first)
        def _():
            dd_ref[...] = jnp.zeros_like(dd_ref)

        def grads(i, _):
            rs = pl.ds(pl.multiple_of(i * S5_RC, S5_RC), S5_RC)
            uv = u_ref[rs, :]
            dyv = dy_ref[rs, :]
            gb = gb_scr[rs, :]
            db_ref[...] += lax.dot_general(uv.astype(BF16), gb, _TN, preferred_element_type=F32)
            dc_ref[...] += lax.dot_general(dyv.astype(BF16), s_ref[rs, :], _TN, preferred_element_type=F32)
            duv = lax.dot_general(gb, b_ref[...], _NT, preferred_element_type=F32)

            @pl.when(first)
            def _():
                du_ref[rs, :] = d_ref[...] * dyv + duv
                dd_ref[...] += jnp.sum(dyv * uv, axis=0, keepdims=True)

            @pl.when(jnp.logical_not(first))
            def _():
                du_ref[rs, :] += duv

            return 0

        lax.fori_loop(0, L // S5_RC, grads, 0)

    u_spec, b_spec, c_spec, l_spec, d_spec = _s5_specs()
    outs = pl.pallas_call(
        body, name="s5_bwd", grid=grid,
        in_specs=[u_spec, u_spec, _s5_state_spec(), b_spec, c_spec, l_spec, l_spec, d_spec] + [ANY] * nr,
        out_specs=[u_spec, b_spec, b_spec, l_spec, l_spec, d_spec] + [ANY] * nr,
        out_shape=[jax.ShapeDtypeStruct((L, SW), F32),
                   jax.ShapeDtypeStruct((2, NSB, 128, 2 * SBW), F32), jax.ShapeDtypeStruct((2, NSB, 128, 2 * SBW), F32),
                   jax.ShapeDtypeStruct((2, NSB, 1, SBW), F32), jax.ShapeDtypeStruct((2, NSB, 1, SBW), F32),
                   jax.ShapeDtypeStruct((1, SW), F32)] + (_exchange_shapes(kind, riding) if nr else []),
        scratch_shapes=[pltpu.VMEM((L, 2 * SBW), F32), pltpu.VMEM((L, 2 * SBW), BF16)]
        + (_exchange_sems(kind, nr) if nr else []),
        compiler_params=_cp(("arbitrary",) * 3 if nr else ("parallel", "arbitrary", "arbitrary")),
    )(u_p, dy_p, states, bcat, ccat, lam_re, lam_im, dskip, *riding)
    return list(outs[:6]), list(outs[6:])


def _s5_params(a_re, a_im, log_step, bt_re, bt_im):
    lam = lax.complex(a_re, a_im)
    step = jnp.exp(log_step)[..., None]
    lam_bar = jnp.exp(lam * step)
    b_bar = ((lam_bar - 1.0) / lam)[..., None, :] * lax.complex(bt_re, bt_im)
    return jnp.real(lam_bar), jnp.imag(lam_bar), jnp.real(b_bar), jnp.imag(b_bar)


def _sel():
    i = jnp.arange(8)[None, :, None]
    j = jnp.arange(4)[None, None, :]
    r = jnp.arange(2)[:, None, None]
    return (i == r * 4 + j).astype(F32)


def _to_bcat(bt_re, bt_im):
    def one(bt):
        return jnp.einsum('dkrjcp,rij->dkricjp', bt.reshape(2, 4, 2, 4, GC, NP), _sel()).reshape(2, NSB, 128, SBW)
    return jnp.concatenate([one(bt_re), one(bt_im)], axis=-1)


def _from_bcat(dbcat):
    def one(dbbd):
        return jnp.einsum('dkricjp,rij->dkrjcp', dbbd.reshape(2, 4, 2, 8, GC, 4, NP), _sel()).reshape(2, NG, GC, NP)
    return one(dbcat[..., :SBW]), one(dbcat[..., SBW:])


def _to_ccat(c_re, c_im):
    def one(cc):
        return jnp.einsum('dkrjcp,rij->dkrjpic', cc.reshape(2, 4, 2, 4, GC, NP), _sel()).reshape(2, NSB, SBW, 128)
    return jnp.concatenate([one(c_re), -one(c_im)], axis=-2)


def _from_ccat(dccat):
    def one(dcbd):
        return jnp.einsum('dkrjpic,rij->dkrjcp', dcbd.reshape(2, 4, 2, 4, NP, 8, GC), _sel()).reshape(2, NG, GC, NP)
    return one(dccat[:, :, :SBW]), -one(dccat[:, :, SBW:])


def _gelu(y):
    return 0.5 * y * (1.0 + lax.erf(y * (2.0 ** -0.5)))


def _gelu_grad(y):
    return 0.5 * (1.0 + lax.erf(y * (2.0 ** -0.5))) + y * jnp.exp(-0.5 * y * y) * ((2.0 * math.pi) ** -0.5)


def _sigmoid(z):
    return 0.5 * jnp.tanh(0.5 * z) + 0.5


def _glu_fwd(y, wg):
    def body(y_ref, w_ref, o_ref, z_ref):
        ys = _gelu(y_ref[...])
        z = jnp.dot(ys.astype(BF16), w_ref[...], preferred_element_type=F32)
        z_ref[...] = z
        o_ref[...] = ys * _sigmoid(z)

    row = pl.BlockSpec((TL, SW), lambda i: (i, 0))
    return pl.pallas_call(
        body, name="glu_fwd", grid=(L // TL,),
        in_specs=[row, pl.BlockSpec((SW, SW), lambda i: (0, 0))], out_specs=[row, row],
        out_shape=[jax.ShapeDtypeStruct((L, SW), F32), jax.ShapeDtypeStruct((L, SW), F32)],
        compiler_params=_cp(("parallel",)),
    )(y, wg)


def _glu_bwd(y, z, dout, wg):
    def body(y_ref, z_ref, do_ref, w_ref, dy_ref, dw_ref):
        @pl.when(pl.program_id(0) == 0)
        def _():
            dw_ref[...] = jnp.zeros_like(dw_ref)

        yv = y_ref[...]
        ys = _gelu(yv)
        sg = _sigmoid(z_ref[...])
        dov = do_ref[...]
        dz = (dov * ys * sg * (1.0 - sg)).astype(BF16)
        dys = dov * sg + lax.dot_general(dz, w_ref[...], _NT, preferred_element_type=F32)
        dy_ref[...] = dys * _gelu_grad(yv)
        dw_ref[...] += lax.dot_general(ys.astype(BF16), dz, _TN, preferred_element_type=F32)

    row = pl.BlockSpec((TL, SW), lambda i: (i, 0))
    wsp = pl.BlockSpec((SW, SW), lambda i: (0, 0))
    return pl.pallas_call(
        body, name="glu_bwd", grid=(L // TL,),
        in_specs=[row, row, row, wsp], out_specs=[row, wsp],
        out_shape=[jax.ShapeDtypeStruct((L, SW), F32), jax.ShapeDtypeStruct((SW, SW), F32)],
        compiler_params=_cp(("arbitrary",)),
    )(y, z, dout, wg)


CT = 256
CR = 128
NCT = DFF // CT


def _shifted(ref, r):
    h = 8 * (4 // ref.dtype.itemsize)
    cur = ref[pl.ds(r, CR), :].astype(F32)
    before = ref[pl.ds(pl.multiple_of(jnp.maximum(r - h, 0), h), h), :][h - 1:h, :].astype(F32)
    after = ref[pl.ds(pl.multiple_of(jnp.minimum(r + CR, L - h), h), h), :][0:1, :].astype(F32)
    before = jnp.where(r > 0, before, 0.0)
    after = jnp.where(r + CR < L, after, 0.0)
    row = lax.broadcasted_iota(jnp.int32, cur.shape, 0)
    prev = jnp.where(row == 0, before, pltpu.roll(cur, 1, 0))
    nxt = jnp.where(row == CR - 1, after, pltpu.roll(cur, CR - 1, 0))
    return prev, cur, nxt


def _conv3(ref, r, w_ref, b_ref):
    prev, cur, nxt = _shifted(ref, r)
    return w_ref[0:1, :] * prev + w_ref[1:2, :] * cur + w_ref[2:3, :] * nxt + b_ref[...]


def _convact_fwd(up, conv_w, conv_b):
    def body(ug_ref, uv_ref, wg_ref, wv_ref, bg_ref, bv_ref, o_ref, g_ref, v_ref):
        def chunk(i, _):
            r = pl.multiple_of(i * CR, CR)
            rs = pl.ds(r, CR)
            g = _conv3(ug_ref, r, wg_ref, bg_ref)
            v = _conv3(uv_ref, r, wv_ref, bv_ref)
            o_ref[rs, :] = (g * _sigmoid(g) * v).astype(BF16)
            g_ref[rs, :] = g.astype(BF16)
            v_ref[rs, :] = v.astype(BF16)
            return 0

        lax.fori_loop(0, L // CR, chunk, 0)

    gcol = pl.BlockSpec((L, CT), lambda j: (0, j))
    vcol = pl.BlockSpec((L, CT), lambda j: (0, j + NCT))
    return pl.pallas_call(
        body, name="convact_fwd", grid=(NCT,),
        in_specs=[gcol, vcol,
                  pl.BlockSpec((3, CT), lambda j: (0, j)), pl.BlockSpec((3, CT), lambda j: (0, j + NCT)),
                  pl.BlockSpec((1, CT), lambda j: (0, j)), pl.BlockSpec((1, CT), lambda j: (0, j + NCT))],
        out_specs=[gcol, gcol, gcol], out_shape=[jax.ShapeDtypeStruct((L, DFF), BF16)] * 3,
        compiler_params=_cp(("parallel",)),
    )(up, up, conv_w, conv_w, conv_b, conv_b)


def _convact_bwd(up, gq, vq, dact, conv_w):
    def body(ug_ref, uv_ref, g_ref, v_ref, da_ref, wg_ref, wv_ref, du_ref, dw_ref, db_ref, dgs, dvs, dbv):
        half = pl.program_id(1)

        def transpose_conv(src, u_ref, w_ref):
            dw_ref[...] = jnp.zeros_like(dw_ref)

            def chunk(i, _):
                r = pl.multiple_of(i * CR, CR)
                rs = pl.ds(r, CR)
                prev, cur, nxt = _shifted(src, r)
                du_ref[rs, :] = (w_ref[0:1, :] * nxt + w_ref[1:2, :] * cur + w_ref[2:3, :] * prev).astype(BF16)
                uv = u_ref[rs, :].astype(F32)
                for k, d in enumerate((nxt, cur, prev)):
                    dw_ref[k:k + 1, :] += jnp.sum(d * uv, axis=0, keepdims=True)
                return 0

            lax.fori_loop(0, L // CR, chunk, 0)

        @pl.when(half == 0)
        def _():
            db_ref[...] = jnp.zeros_like(db_ref)
            dbv[...] = jnp.zeros_like(dbv)

            def chunk1(i, _):
                rs = pl.ds(pl.multiple_of(i * CR, CR), CR)
                g = g_ref[rs, :].astype(F32)
                v = v_ref[rs, :].astype(F32)
                sg = _sigmoid(g)
                da = da_ref[rs, :].astype(F32)
                dv = da * g * sg
                dg = da * v * sg * (1.0 + g * (1.0 - sg))
                dgs[rs, :] = dg
                dvs[rs, :] = dv
                db_ref[...] += jnp.sum(dg, axis=0, keepdims=True)
                dbv[0:1, :] += jnp.sum(dv, axis=0, keepdims=True)
                return 0

            lax.fori_loop(0, L // CR, chunk1, 0)
            transpose_conv(dgs, ug_ref, wg_ref)

        @pl.when(half == 1)
        def _():
            db_ref[...] = dbv[0:1, :]
            transpose_conv(dvs, uv_ref, wv_ref)

    def col(rows, off):
        return pl.BlockSpec((rows, CT), lambda j, h: (0, j + off))

    def out(rows):
        return pl.BlockSpec((rows, CT), lambda j, h: (0, j + h * NCT))

    return pl.pallas_call(
        body, name="convact_bwd", grid=(NCT, 2),
        in_specs=[col(L, 0), col(L, NCT), col(L, 0), col(L, 0), col(L, 0), col(3, 0), col(3, NCT)],
        out_specs=[out(L), out(3), out(1)],
        out_shape=[jax.ShapeDtypeStruct((L, 2 * DFF), BF16), jax.ShapeDtypeStruct((3, 2 * DFF), F32),
                   jax.ShapeDtypeStruct((1, 2 * DFF), F32)],
        scratch_shapes=[pltpu.VMEM((L, CT), F32), pltpu.VMEM((L, CT), F32), pltpu.VMEM((8, CT), F32)],
        compiler_params=_cp(("parallel", "arbitrary")),
    )(up, up, gq, vq, dact, conv_w, conv_w)


def _local_step(x, tgt, w_in_t, p, attend, scan, stage):
    tabs = _rope_tables()
    lam_re, lam_im, bb_re, bb_im = _s5_params(p["a_re"], p["a_im"], p["log_step"], p["bt_re"], p["bt_im"])
    bcat = _to_bcat(bb_re, bb_im).astype(BF16)
    ccat = _to_ccat(p["c_re"], p["c_im"]).astype(BF16)
    lam_re4, lam_im4 = lam_re.reshape(2, NSB, 1, SBW), lam_im.reshape(2, NSB, 1, SBW)
    dskip = p["d_skip"].reshape(1, SW)
    g_mix, g_ffn, g_fin = p["norm_mix_g"].reshape(1, D), p["norm_ffn_g"].reshape(1, D), p["norm_final_g"].reshape(1, D)
    g_attn, g_ssm = p["norm_attn_g"].reshape(1, AW), p["norm_ssm_g"].reshape(1, SW)
    sink = p["sink"].reshape(NQ)
    conv_b = p["conv_b"].reshape(1, 2 * DFF)

    rows, gain = jax.ShapeDtypeStruct((L, D), F32), jax.ShapeDtypeStruct((1, D), F32)
    rows16 = jax.ShapeDtypeStruct((L, D), BF16)
    h1, qkv, u = _in_proj(x, g_mix, w_in_t, tabs)
    attn, lse, wts = attend(qkv, sink)
    u_p = _perm(u)
    y_p, states, more = scan(u_p, bcat, ccat, lam_re4, lam_im4, dskip)
    wts = dict(wts, **more)
    w_glu, w_out, w_up_t, w_down, conv_w = (wts[k] for k in ("w_glu", "w_out", "w_up_t", "w_down", "conv_w"))
    ysg_p, z_p = _glu_fwd(y_p, w_glu)
    ysg = _unperm(ysg_p)
    mixed, x1, h2 = _out_proj(attn, ysg, g_attn, g_ssm, w_out, x, g_ffn)
    up = _mm(h2, w_up_t, tb=True, name="ffn_up", tn=1408, out_dtype=BF16)
    act, gq, vq = _convact_fwd(up, conv_w, conv_b)
    loss, dx2, dx2b, dg_fin = _mm(
        act, w_down, add=x1, name="ffn_down", tm=512, tk=DFF,
        post=(_final_post, [tgt, g_fin], [jax.ShapeDtypeStruct((1, 1), F32), rows, rows16, gain]))

    def riding(res, ride):
        return res if ride else (res, None)

    dw_down = _mm(act, dx2b, ta=True, name="ffn_down_dw", tm=256, tk=L)
    ride = stage(("w_down",), "cores", [dw_down])
    dact, got = riding(_mm(dx2b, w_down, tb=True, name="ffn_down_dx", tn=1408, out_dtype=BF16, ride=ride), ride)
    ride = stage(("w_down",), "chips", got)
    dup, dconv_w, dconv_b = _convact_bwd(up, gq, vq, dact, conv_w)
    dw_up_t, got = riding(_mm(dup, h2, ta=True, name="ffn_up_dw", tm=512, tk=L, ride=ride), ride)
    stage(("w_down",), "done", got)
    ride = stage(("w_up_t",), "cores", [dw_up_t])
    (dx1, dx1b, dg_ffn), got = riding(
        _mm(dup, w_up_t, name="ffn_up_dx", tm=512, tk=2 * DFF, ride=ride,
            post=(_rms_bwd_post, [x1, dx2, g_ffn], [rows, rows16, gain])), ride)
    ride = stage(("w_up_t",), "chips", got)
    dattn, dysg, dg_attn, dg_ssm = _out_proj_dx(dx1b, w_out, attn, ysg, g_attn, g_ssm)
    dw_out = _mm(mixed, dx1b, ta=True, name="out_proj_dw", tm=512, tk=L)
    dy_p, dw_glu = _glu_bwd(y_p, z_p, _perm(dysg), w_glu)
    (du_p, dbcat, dccat, dlam_re, dlam_im, dd), got = _s5_bwd(u_p, dy_p, states, bcat, ccat, lam_re4, lam_im4, dskip,
                                                              ride=ride)
    stage(("w_up_t",), "done", got)
    dbb_re, dbb_im = _from_bcat(dbcat)
    dc_re, dc_im = _from_ccat(_swap(dccat))
    mix = ("w_out", "w_glu")
    ride = stage(mix, "cores", [dw_out, dw_glu])
    (dq, dk, dv, dsink), got = _attn_bwd(qkv, sink, attn, lse, dattn, ride=ride)
    ride = stage(mix, "chips", got)
    dproj = _rope_bwd(dq, dk, dv, _unperm(du_p), tabs)
    dw_in_t, got = riding(_mm(dproj, h1, ta=True, name="in_proj_dw", tm=640, tk=L, ride=ride), ride)
    stage(mix, "done", got)
    grad_x, dg_mix = _in_proj_dx(dproj, w_in_t, x, g_mix, dx1)

    big = dict(w_in_t=dw_in_t)
    small = dict(norm_mix_g=dg_mix, norm_attn_g=dg_attn, norm_ssm_g=dg_ssm, norm_ffn_g=dg_ffn, norm_final_g=dg_fin,
                 sink=dsink[:, 0], conv_b=dconv_b, d_skip=dd, conv_w=dconv_w,
                 lam_re=dlam_re.reshape(2, NG, NP), lam_im=dlam_im.reshape(2, NG, NP),
                 bb_re=dbb_re, bb_im=dbb_im, c_re=dc_re, c_im=dc_im, loss=loss.reshape(1))
    return grad_x, big, small


ANY = pl.BlockSpec(memory_space=pl.ANY)


def _coords():
    return lax.axis_index("x"), lax.axis_index("y"), lax.axis_index("c")


def _flip(v, b):
    return v + b - 2 * v * b if b else v


def _all_gather(shards, name):
    n = len(shards)

    def body(*refs):
        _gather_start(refs[:n], refs[n:2 * n], *refs[2 * n:])
        _gather_finish(refs[:n], refs[n:2 * n], *refs[2 * n:])

    return pl.pallas_call(
        body, name=name,
        in_specs=[ANY] * n, out_specs=[ANY] * n,
        out_shape=_gather_shapes(shards), scratch_shapes=_gather_sems(n),
    )(*shards)


def _gather_shapes(shards):
    return [jax.ShapeDtypeStruct((NDEV * s.shape[0], s.shape[1]), s.dtype) for s in shards]


def _gather_sems(n):
    return [pltpu.SemaphoreType.DMA((7 * n,)), pltpu.SemaphoreType.DMA((7 * n,)), pltpu.SemaphoreType.DMA((n,))]


def _gather_copies(ins, outs, send_sems, recv_sems, local_sems, a):
    x, y, c = _coords()
    me, sibling = (x, y, c), (x, y, 1 - c)
    chips = [(1 - x, y), (x, 1 - y), (1 - x, 1 - y)]
    r = ins[a].shape[0]

    def rows(px, py, pc):
        return outs[a].at[pl.ds(pl.multiple_of((4 * px + 2 * py + pc) * r, 8), r), :]

    def copy(k, block, to, src=None):
        return pltpu.make_async_remote_copy(
            src_ref=rows(*block) if src is None else src, dst_ref=rows(*block),
            send_sem=send_sems.at[a * 7 + k], recv_sem=recv_sems.at[a * 7 + k],
            device_id=to, device_id_type=pl.DeviceIdType.MESH)

    mine = pltpu.make_async_copy(ins[a], rows(*me), local_sems.at[a])
    first = [copy(0, me, sibling, src=ins[a])]
    first += [copy(1 + j, me, (*chip, c), src=ins[a]) for j, chip in enumerate(chips)]
    passed = [copy(4 + j, (*chip, c), sibling) for j, chip in enumerate(chips)]
    arrivals = [copy(1 + j, (*chip, c), me) for j, chip in enumerate(chips)]
    from_sibling = [copy(0, sibling, me)] + [copy(4 + j, (*chip, 1 - c), me) for j, chip in enumerate(chips)]
    return mine, first, passed, arrivals, from_sibling


def _gather_start(ins, outs, send_sems, recv_sems, local_sems):
    for a in range(len(ins)):
        mine, first, _, _, _ = _gather_copies(ins, outs, send_sems, recv_sems, local_sems, a)
        mine.start()
        for cp in first:
            cp.start()


def _gather_forward(ins, outs, send_sems, recv_sems, local_sems):
    for a in range(len(ins)):
        _, _, passed, arrivals, _ = _gather_copies(ins, outs, send_sems, recv_sems, local_sems, a)
        for arrived, onward in zip(arrivals, passed):
            arrived.wait_recv()
            onward.start()


def _gather_wait(ins, outs, send_sems, recv_sems, local_sems):
    for a in range(len(ins)):
        mine, first, passed, _, from_sibling = _gather_copies(ins, outs, send_sems, recv_sems, local_sems, a)
        for cp in from_sibling:
            cp.wait_recv()
        for cp in first + passed:
            cp.wait_send()
        mine.wait()


def _gather_finish(*refs):
    _gather_forward(*refs)
    _gather_wait(*refs)


def _gather_under(step, steps, travellers):
    @pl.when(step == 0)
    def _():
        _gather_start(*travellers)

    @pl.when(step == (3 * steps) // 4)
    def _():
        _gather_forward(*travellers)


def _gather_done(step, steps, travellers):
    @pl.when(step == steps - 1)
    def _():
        _gather_wait(*travellers)


NCHIP = 4
CHIP_FLIPS = ((1, 0), (0, 1), (1, 1))


def _planned_copies(ins, outs, send_sems, recv_sems, plan):
    return [pltpu.make_async_remote_copy(
        src_ref=ins[a].at[src], dst_ref=outs[a].at[dst], send_sem=send_sems.at[k], recv_sem=recv_sems.at[k],
        device_id=to, device_id_type=pl.DeviceIdType.MESH) for k, (a, src, dst, to) in enumerate(plan)]


def _start_all(copies):
    for cp in copies:
        cp.start()


def _wait_all(copies):
    for cp in copies:
        cp.wait_recv()
    for cp in copies:
        cp.wait_send()


SLOTS = {"cores": NCHIP, "chips": 3}


def _exchange_copies(kind, ins, outs, send_sems, recv_sems):
    x, y, c = _coords()
    plan = []
    for a in range(len(ins)):
        if kind == "cores":
            plan += [(a, 2 * q + 1 - c, q, (x, y, 1 - c)) for q in range(NCHIP)]
        else:
            for j, (fx, fy) in enumerate(CHIP_FLIPS):
                px, py = _flip(x, fx), _flip(y, fy)
                plan.append((a, 2 * px + py, j, (px, py, c)))
    return _planned_copies(ins, outs, send_sems, recv_sems, plan)


def _exchange_shapes(kind, parts):
    return [jax.ShapeDtypeStruct((SLOTS[kind],) + s.shape[1:], s.dtype) for s in parts]


def _exchange_sems(kind, n):
    return [pltpu.SemaphoreType.DMA((SLOTS[kind] * n,)), pltpu.SemaphoreType.DMA((SLOTS[kind] * n,))]


def _exchange(kind, parts, name):
    n = len(parts)

    def body(*refs):
        copies = _exchange_copies(kind, refs[:n], refs[n:2 * n], *refs[2 * n:])
        _start_all(copies)
        _wait_all(copies)

    return pl.pallas_call(
        body, name=name, in_specs=[ANY] * n, out_specs=[ANY] * n,
        out_shape=_exchange_shapes(kind, parts), scratch_shapes=_exchange_sems(kind, n),
    )(*parts)


def _pair_sum(where, part, recv, wire_dtype, name):
    _, r, c = part.shape
    tr = _pick(r, 256, 16)

    def body(w_ref, p_ref, r_ref, pb_ref, own_ref):
        s = p_ref[...] + r_ref[...]
        pb_ref[...] = s.astype(wire_dtype)

        @pl.when(pl.program_id(1) == w_ref[1])
        def _():
            own_ref[...] = s

    return pl.pallas_call(
        body, name=name,
        grid_spec=pltpu.PrefetchScalarGridSpec(
            num_scalar_prefetch=1, grid=(r // tr, NCHIP),
            in_specs=[pl.BlockSpec((None, tr, c), lambda i, q, w: (2 * q + w[0], i, 0)),
                      pl.BlockSpec((None, tr, c), lambda i, q, w: (q, i, 0))],
            out_specs=[pl.BlockSpec((None, tr, c), lambda i, q, w: (q, i, 0)),
                       pl.BlockSpec((tr, c), lambda i, q, w: (i, 0))]),
        out_shape=[jax.ShapeDtypeStruct((NCHIP, r, c), wire_dtype), jax.ShapeDtypeStruct((r, c), F32)],
        compiler_params=_cp(("parallel", "arbitrary")),
    )(where, part, recv)


def _chip_sum(own, recv, name):
    r, c = own.shape
    tr = _pick(r, 256, 16)

    def body(o_ref, r_ref, out_ref):
        acc = o_ref[...]
        for j in range(3):
            acc = acc + r_ref[j].astype(F32)
        out_ref[...] = acc

    return pl.pallas_call(
        body, name=name, grid=(r // tr,),
        in_specs=[pl.BlockSpec((tr, c), lambda i: (i, 0)), pl.BlockSpec((3, tr, c), lambda i: (0, i, 0))],
        out_specs=pl.BlockSpec((tr, c), lambda i: (i, 0)),
        out_shape=jax.ShapeDtypeStruct((r, c), F32),
        compiler_params=_cp(("parallel",)),
    )(own, recv)


def _adamw(w, own, recv, m, v, name):
    r, c = w.shape
    tr = _pick(r, 256, 16)

    def body(w_ref, o_ref, r_ref, m_ref, v_ref, g_ref, d_ref, nm_ref, nv_ref):
        acc = o_ref[...]
        for j in range(3):
            acc = acc + r_ref[j].astype(F32)
        g_ref[...] = acc
        _adamw_refs(w_ref, g_ref, m_ref, v_ref, d_ref, nm_ref, nv_ref)

    blk = pl.BlockSpec((tr, c), lambda i: (i, 0))
    return pl.pallas_call(
        body, name=name, grid=(r // tr,),
        in_specs=[blk, blk, pl.BlockSpec((3, tr, c), lambda i: (0, i, 0)), blk, blk], out_specs=[blk] * 4,
        out_shape=[jax.ShapeDtypeStruct((r, c), F32)] * 4,
        compiler_params=_cp(("parallel",)),
    )(w, own, recv, m, v)


def _adamw_refs(w_ref, g_ref, m_ref, v_ref, d_ref, nm_ref, nv_ref):
    gv = g_ref[...]
    nm = B1 * m_ref[...] + (1.0 - B1) * gv
    nv = B2 * v_ref[...] + (1.0 - B2) * (gv * gv)
    nm_ref[...] = nm
    nv_ref[...] = nv
    d_ref[...] = -LR * ((nm / C1) / (jnp.sqrt(nv / C2) + AEPS) + WD * w_ref[...])


def _adamw_small(ws, gs, ms, vs, name):
    n = len(ws)

    def body(*refs):
        groups = [refs[i * n:(i + 1) * n] for i in range(7)]
        for per_param in zip(*groups):
            _adamw_refs(*per_param)

    vm = pl.BlockSpec(memory_space=pltpu.VMEM)
    outs = pl.pallas_call(
        body, name=name, in_specs=[vm] * (4 * n), out_specs=[vm] * (3 * n),
        out_shape=[jax.ShapeDtypeStruct(a.shape, F32) for a in ws] * 3,
    )(*ws, *gs, *ms, *vs)
    return outs[:n], outs[n:2 * n], outs[2 * n:]


def _swap(a):
    return jnp.swapaxes(a, -1, -2)


VIEWS = {
    "w_in": (lambda a: a[0].T, lambda u: u.T[None]),
    "w_up": (lambda a: a[0].T, lambda u: u.T[None]),
    "w_glu": (lambda a: a[0], lambda u: u[None]),
    "w_out": (lambda a: a[0], lambda u: u[None]),
    "w_down": (lambda a: a[0], lambda u: u[None]),
    "conv_w": (lambda a: a[0], lambda u: u[None]),
    "norm_mix_g": (lambda a: a, lambda u: u),
    "norm_attn_g": (lambda a: a, lambda u: u),
    "norm_ssm_g": (lambda a: a, lambda u: u),
    "norm_ffn_g": (lambda a: a, lambda u: u),
    "norm_final_g": (lambda a: a[None], lambda u: u[0]),
    "conv_b": (lambda a: a, lambda u: u),
    "sink": (lambda a: a, lambda u: u),
    "a_re": (lambda a: a.reshape(2 * NG, NP), lambda u: u.reshape(1, 2, NG, NP)),
    "a_im": (lambda a: a.reshape(2 * NG, NP), lambda u: u.reshape(1, 2, NG, NP)),
    "log_step": (lambda a: a[0], lambda u: u[None]),
    "b_re": (lambda a: _swap(a[0]).reshape(2 * NG * GC, NP), lambda u: _swap(u.reshape(2, NG, GC, NP))[None]),
    "b_im": (lambda a: _swap(a[0]).reshape(2 * NG * GC, NP), lambda u: _swap(u.reshape(2, NG, GC, NP))[None]),
    "c_re": (lambda a: a.reshape(2 * NG * GC, NP), lambda u: u.reshape(1, 2, NG, GC, NP)),
    "c_im": (lambda a: a.reshape(2 * NG * GC, NP), lambda u: u.reshape(1, 2, NG, GC, NP)),
    "d_skip": (lambda a: a[0].T, lambda u: u.T[None]),
}
BIG = ["w_in", "w_glu", "w_out", "w_up", "w_down"]
PACK_W = 1024


def _pack(arrs, rows):
    flat = jnp.concatenate([a.reshape(-1).astype(F32) for a in arrs])
    return jnp.pad(flat, (0, rows * PACK_W - flat.shape[0])).reshape(rows, PACK_W)


def _unpack(packed, shapes):
    flat = packed.reshape(-1)
    out, off = [], 0
    for s in shapes:
        size = math.prod(s)
        out.append(flat[off:off + size].reshape(s))
        off += size
    return out


def kernel(x, norm_mix_g, w_in, a_re, a_im, log_step, b_re, b_im, c_re, c_im, d_skip, w_glu, sink, norm_attn_g, norm_ssm_g, w_out, norm_ffn_g, w_up, conv_w, conv_b, w_down, norm_final_g, loss_target, m_norm_mix_g, m_w_in, m_a_re, m_a_im, m_log_step, m_b_re, m_b_im, m_c_re, m_c_im, m_d_skip, m_w_glu, m_sink, m_norm_attn_g, m_norm_ssm_g, m_w_out, m_norm_ffn_g, m_w_up, m_conv_w, m_conv_b, m_w_down, m_norm_final_g, v_norm_mix_g, v_w_in, v_a_re, v_a_im, v_log_step, v_b_re, v_b_im, v_c_re, v_c_im, v_d_skip, v_w_glu, v_sink, v_norm_attn_g, v_norm_ssm_g, v_w_out, v_norm_ffn_g, v_w_up, v_conv_w, v_conv_b, v_w_down, v_norm_final_g):
    args = dict(locals())
    names = ["norm_mix_g", "w_in", "a_re", "a_im", "log_step", "b_re", "b_im", "c_re", "c_im", "d_skip", "w_glu",
             "sink", "norm_attn_g", "norm_ssm_g", "w_out", "norm_ffn_g", "w_up", "conv_w", "conv_b", "w_down",
             "norm_final_g"]
    w = {k: args[k] for k in names}
    m = {k: args["m_" + k] for k in names}
    v = {k: args["v_" + k] for k in names}

    (w_in_t,) = _all_gather([w_in[0].T.astype(BF16)], "gather_w_in")
    under_attn = dict(w_glu=w_glu[0].astype(BF16), w_out=w_out[0].astype(BF16),
                      conv_w=jnp.pad(conv_w[0], ((0, 5), (0, 0))))
    under_scan = dict(w_up_t=w_up[0].T.astype(BF16), w_down=w_down[0].astype(BF16))

    ax, ay, ac = _coords()
    me = 4 * ax + 2 * ay + ac
    where = jnp.stack([ac, 2 * ax + ay]).astype(jnp.int32)
    parts, own, got = {}, {}, {}

    def split8(g):
        return g.reshape(NDEV, g.shape[0] // NDEV, g.shape[1])

    def attend(qkv, sink_):
        attn, lse, gathered = _attn_fwd(qkv, sink_, gather=list(under_attn.values()))
        wts = dict(zip(under_attn.keys(), gathered))
        wts["conv_w"] = (wts["conv_w"].reshape(NDEV, 8, 2 * DFF // NDEV)[:, :3].transpose(1, 0, 2)
                         .reshape(3, 2 * DFF))
        return attn, lse, wts

    def scan(*operands):
        y_p, states, gathered = _s5_fwd(*operands, gather=list(under_scan.values()))
        return y_p, states, dict(zip(under_scan.keys(), gathered))

    def pair_sum(k, from_core):
        per_chip, own[k] = _pair_sum(where, parts[k], from_core, F32 if k == "small" else BF16, "pair_sum_" + k)
        return per_chip

    def stage(ks, phase, payload):
        if phase == "cores":
            parts.update({k: split8(g) for k, g in zip(ks, payload)})
            return ("cores", [parts[k] for k in ks])
        if phase == "chips":
            return ("chips", [pair_sum(k, fc) for k, fc in zip(ks, payload)])
        got.update(zip(ks, payload))
        return ()

    p = {k: w[k][0] for k in ("norm_mix_g", "a_re", "a_im", "log_step", "c_re", "c_im", "d_skip", "sink",
                              "norm_attn_g", "norm_ssm_g", "norm_ffn_g", "conv_b")}
    p["norm_final_g"] = norm_final_g
    p["bt_re"], p["bt_im"] = _swap(b_re[0]), _swap(b_im[0])
    grad_x, big, small = _local_step(x[0], loss_target[0], w_in_t, p, attend, scan, stage)

    small_names = list(small.keys())
    small_shapes = [small[k].shape for k in small_names]
    n_small = sum(math.prod(s) for s in small_shapes)
    rows_dev = -(-n_small // (PACK_W * NDEV * 16)) * 16
    spack = _pack([small[k] for k in small_names], rows_dev * NDEV)
    late = ["w_in_t", "small"]
    parts.update(w_in_t=split8(big["w_in_t"]), small=spack.reshape(NDEV, rows_dev, PACK_W))
    from_cores = _exchange("cores", [parts[k] for k in late], "exchange_cores")
    from_chips = _exchange("chips", [pair_sum(k, fc) for k, fc in zip(late, from_cores)], "exchange_chips")
    got.update(zip(late, from_chips))
    (small_full,) = _all_gather([_chip_sum(own["small"], got["small"], "chip_sum_small")], "gather_small")
    sm = dict(zip(small_names, _unpack(small_full, small_shapes)))

    _, s5_vjp = jax.vjp(_s5_params, a_re[0], a_im[0], log_step[0], p["bt_re"], p["bt_im"])
    da_re, da_im, dlog_step, dbt_re, dbt_im = s5_vjp((sm["lam_re"], sm["lam_im"], sm["bb_re"], sm["bb_im"]))
    gview = {
        "norm_mix_g": sm["norm_mix_g"], "norm_attn_g": sm["norm_attn_g"], "norm_ssm_g": sm["norm_ssm_g"],
        "norm_ffn_g": sm["norm_ffn_g"], "norm_final_g": sm["norm_final_g"], "conv_b": sm["conv_b"],
        "sink": sm["sink"][None], "a_re": da_re.reshape(2 * NG, NP), "a_im": da_im.reshape(2 * NG, NP),
        "log_step": dlog_step, "b_re": dbt_re.reshape(2 * NG * GC, NP), "b_im": dbt_im.reshape(2 * NG * GC, NP),
        "c_re": sm["c_re"].reshape(2 * NG * GC, NP), "c_im": sm["c_im"].reshape(2 * NG * GC, NP),
        "d_skip": sm["d_skip"].reshape(NG, GC).T,
        "conv_w": lax.dynamic_slice_in_dim(sm["conv_w"], me * (2 * DFF // NDEV), 2 * DFF // NDEV, axis=1),
    }

    dview, mview, vview = {}, {}, {}
    for k, kg in zip(BIG, ("w_in_t", "w_glu", "w_out", "w_up_t", "w_down")):
        to = VIEWS[k][0]
        gview[k], dview[k], mview[k], vview[k] = _adamw(to(w[k]), own[kg], got[kg], to(m[k]), to(v[k]), "adamw_" + k)
    rest = [k for k in names if k not in BIG]
    outs = _adamw_small([VIEWS[k][0](w[k]) for k in rest], [gview[k] for k in rest],
                        [VIEWS[k][0](m[k]) for k in rest], [VIEWS[k][0](v[k]) for k in rest], "adamw_small")
    for dst, vals in zip((dview, mview, vview), outs):
        dst.update(dict(zip(rest, vals)))

    def back(views):
        return [VIEWS[k][1](views[k]) for k in names]

    return (sm["loss"][0], grad_x[None], *back(gview), *back(dview), *back(mview), *back(vview))
```

```python
import functools
import math

import jax
import jax.numpy as jnp
from jax import lax
from jax.experimental import pallas as pl
from jax.experimental.pallas import tpu as pltpu

F32 = jnp.float32
BF16 = jnp.bfloat16

L = 4096
D = 1024
NQ, NKV, HD = 8, 2, 64
AW = NQ * HD
KVW = NKV * HD
SW = 512
NG, GC, NP = 32, 16, 64
INW = AW + 2 * KVW + SW
DFF = 2816
BLK = 128
WIN = 3 * BLK
EPS = 1e-6
ROPE_THETA = 500000.0
NSEG = 32
TSEG = L // NSEG
SBW = 256
NSB = NG * NP // SBW
NDEV = 8
MESH_AXES = ("x", "y", "c")

LR, B1, B2, AEPS, WD, STEP = 0.001, 0.9, 0.999, 1e-08, 0.01, 10
C1 = 1.0 - B1 ** STEP
C2 = 1.0 - B2 ** STEP

VMEM_LIMIT = 56 * 1024 * 1024


def _pick(n, target, mult):
    best = None
    for t in range(mult, min(n, target) + 1, mult):
        if n % t == 0:
            best = t
    return best if best is not None else n


def _cp(sem):
    return pltpu.CompilerParams(dimension_semantics=sem, vmem_limit_bytes=VMEM_LIMIT)


def _mm(a, b, *, ta=False, tb=False, out_dtype=F32, add=None, ride=(), post=None, name, tm=1024, tn=1024, tk=1024):
    m, k = (a.shape[1], a.shape[0]) if ta else a.shape
    n = b.shape[0] if tb else b.shape[1]
    assert k == (b.shape[1] if tb else b.shape[0])
    tm, tn, tk = _pick(m, tm, 128), _pick(n, tn, 128), _pick(k, tk, 128)
    grid = (m // tm, n // tn, k // tk)
    nk = grid[2]
    dn = (((0 if ta else 1,), (1 if tb else 0,)), ((), ()))
    n_in = 2 + (add is not None)
    kind, riding = ride if ride else (None, ())
    nr = len(riding)
    post_fn, post_ins, post_outs = post if post is not None else (None, (), ())
    n_pi = len(post_ins)
    n_out = len(post_outs) if post is not None else 1
    assert post is None or grid[1] == 1

    def body(*refs):
        a_ref, b_ref = refs[0], refs[1]
        pin = refs[n_in:n_in + n_pi]
        base = n_in + n_pi + nr
        o_refs = refs[base:base + n_out]
        acc_ref = refs[base + n_out + nr]
        step = [pl.program_id(d) for d in range(3)]
        kk = step[2]
        if nr:
            riders = (refs[n_in + n_pi:base], refs[base + n_out:base + n_out + nr], *refs[base + n_out + nr + 1:])

            @pl.when((step[0] == 0) & (step[1] == 0) & (kk == 0))
            def _():
                _start_all(_exchange_copies(kind, *riders))

        prod = lax.dot_general(a_ref[...].astype(BF16), b_ref[...].astype(BF16), dn, preferred_element_type=F32)

        def finish(r):
            if add is not None:
                r = r + refs[2][...]
            if post_fn is None:
                o_refs[0][...] = r.astype(out_dtype)
            else:
                post_fn(r, step[0], pin, o_refs)

        if nk == 1:
            finish(prod)
        else:
            @pl.when(kk == 0)
            def _():
                acc_ref[...] = prod

            @pl.when((kk > 0) & (kk < nk - 1))
            def _():
                acc_ref[...] += prod

            @pl.when(kk == nk - 1)
            def _():
                finish(acc_ref[...] + prod)

        if nr:
            @pl.when((step[0] == grid[0] - 1) & (step[1] == grid[1] - 1) & (kk == nk - 1))
            def _():
                _wait_all(_exchange_copies(kind, *riders))

    a_spec = pl.BlockSpec((tk, tm), lambda i, j, kk: (kk, i)) if ta else pl.BlockSpec((tm, tk), lambda i, j, kk: (i, kk))
    b_spec = pl.BlockSpec((tn, tk), lambda i, j, kk: (j, kk)) if tb else pl.BlockSpec((tk, tn), lambda i, j, kk: (kk, j))
    def row_spec(shape):
        return pl.BlockSpec((tm if shape[0] == m else shape[0], shape[1]),
                            (lambda i, j, kk: (i, 0)) if shape[0] == m else (lambda i, j, kk: (0, 0)))

    in_specs = [a_spec, b_spec]
    args = [a, b]
    if add is not None:
        in_specs.append(pl.BlockSpec((tm, tn), lambda i, j, kk: (i, j)))
        args.append(add)
    if post is None:
        main_specs = [pl.BlockSpec((tm, tn), lambda i, j, kk: (i, j))]
        main_shapes = [jax.ShapeDtypeStruct((m, n), out_dtype)]
    else:
        main_specs = [row_spec(s.shape) for s in post_outs]
        main_shapes = list(post_outs)
    outs = pl.pallas_call(
        body, name=name, grid=grid,
        in_specs=in_specs + [row_spec(p.shape) for p in post_ins] + [ANY] * nr,
        out_specs=main_specs + [ANY] * nr,
        out_shape=main_shapes + (_exchange_shapes(kind, riding) if nr else []),
        scratch_shapes=[pltpu.VMEM((tm, tn) if nk > 1 else (8, 128), F32)] + (_exchange_sems(kind, nr) if nr else []),
        compiler_params=_cp(("arbitrary",) * 3 if (nr or post is not None) else ("parallel", "parallel", "arbitrary")),
    )(*args, *post_ins, *riding)
    main = outs[0] if post is None else list(outs[:n_out])
    return (main, list(outs[n_out:])) if nr else main


TL = 512


def _rms(xv, gv):
    return xv * lax.rsqrt(jnp.mean(xv * xv, axis=-1, keepdims=True) + EPS) * gv


def _rows(width):
    return pl.BlockSpec((TL, width), lambda i: (i, 0))


def _whole(shape):
    return pl.BlockSpec(shape, lambda i: (0,) * len(shape))


def _in_proj(x, g, w_in_t, tabs):
    qkw = AW + 2 * KVW

    def body(x_ref, g_ref, w_ref, c_ref, sa_ref, sb_ref, h_ref, qkv_ref, u_ref):
        h = _rms(x_ref[...], g_ref[...]).astype(BF16)
        h_ref[...] = h
        proj = lax.dot_general(h, w_ref[...], _NT, preferred_element_type=F32)
        for j in range(qkw // 128):
            cols = slice(j * 128, (j + 1) * 128)
            xv = proj[:, cols]
            if j < (AW + KVW) // 128:
                xv = _rope(xv, c_ref[...], sa_ref[...], sb_ref[...], 1.0)
            qkv_ref[:, cols] = xv.astype(BF16)
        u_ref[...] = proj[:, qkw:]

    return pl.pallas_call(
        body, name="in_proj", grid=(L // TL,),
        in_specs=[_rows(D), _whole((1, D)), _whole((INW, D)), _rows(128), _rows(128), _rows(128)],
        out_specs=[_rows(D), _rows(qkw), _rows(SW)],
        out_shape=[jax.ShapeDtypeStruct((L, D), BF16), jax.ShapeDtypeStruct((L, qkw), BF16),
                   jax.ShapeDtypeStruct((L, SW), F32)],
        compiler_params=_cp(("parallel",)),
    )(x, g, w_in_t, *tabs)


def _in_proj_dx(dproj, w_in_t, x, g, dres):
    def body(dp_ref, w_ref, x_ref, g_ref, dres_ref, dx_ref, dg_ref):
        dh = jnp.dot(dp_ref[...], w_ref[...], preferred_element_type=F32)
        dx, dg = _rms_bwd_tile(x_ref[...], g_ref[...], dh)
        dx_ref[...] = dx + dres_ref[...]

        @pl.when(pl.program_id(0) == 0)
        def _():
            dg_ref[...] = jnp.zeros_like(dg_ref)

        dg_ref[...] += dg

    return pl.pallas_call(
        body, name="in_proj_dx", grid=(L // TL,),
        in_specs=[_rows(INW), _whole((INW, D)), _rows(D), _whole((1, D)), _rows(D)],
        out_specs=[_rows(D), _whole((1, D))],
        out_shape=[jax.ShapeDtypeStruct((L, D), F32), jax.ShapeDtypeStruct((1, D), F32)],
        compiler_params=_cp(("arbitrary",)),
    )(dproj, w_in_t, x, g, dres)


def _rms_bwd_tile(xv, gv, dh):
    r = lax.rsqrt(jnp.mean(xv * xv, axis=-1, keepdims=True) + EPS)
    a = dh * gv
    dx = r * a - xv * (r * r * r) * jnp.mean(a * xv, axis=-1, keepdims=True)
    dg = jnp.sum(dh * xv * r, axis=0, keepdims=True)
    return dx, dg


def _rms_bwd_post(dh, i, ins, outs):
    x_ref, dres_ref, g_ref = ins
    dx_ref, dxb_ref, dg_ref = outs
    dx, dg = _rms_bwd_tile(x_ref[...], g_ref[...], dh)
    dx = dx + dres_ref[...]
    dx_ref[...] = dx
    dxb_ref[...] = dx.astype(BF16)

    @pl.when(i == 0)
    def _():
        dg_ref[...] = jnp.zeros_like(dg_ref)

    dg_ref[...] += dg


def _final_post(xv, i, ins, outs):
    t_ref, g_ref = ins
    loss_ref, dx_ref, dxb_ref, dg_ref = outs

    @pl.when(i == 0)
    def _():
        loss_ref[...] = jnp.zeros_like(loss_ref)
        dg_ref[...] = jnp.zeros_like(dg_ref)

    gv = g_ref[...]
    r = lax.rsqrt(jnp.mean(xv * xv, axis=-1, keepdims=True) + EPS)
    e = xv * r * gv - t_ref[...]
    loss_ref[...] += 0.5 * jnp.sum(jnp.mean(e * e, axis=-1, keepdims=True), axis=0, keepdims=True)
    dy = e * (1.0 / D)
    a = dy * gv
    dx = r * a - xv * (r * r * r) * jnp.mean(a * xv, axis=-1, keepdims=True)
    dx_ref[...] = dx
    dxb_ref[...] = dx.astype(BF16)
    dg_ref[...] += jnp.sum(dy * xv * r, axis=0, keepdims=True)


def _out_proj(attn, ysg, ga, gs, w_out, x, gf):
    def body(a_ref, s_ref, ga_ref, gs_ref, w_ref, x_ref, gf_ref, m_ref, x1_ref, h2_ref):
        m_ref[:, 0:AW] = _rms(a_ref[...], ga_ref[...]).astype(BF16)
        m_ref[:, AW:AW + SW] = _rms(s_ref[...], gs_ref[...]).astype(BF16)
        x1 = jnp.dot(m_ref[...], w_ref[...], preferred_element_type=F32) + x_ref[...]
        x1_ref[...] = x1
        h2_ref[...] = _rms(x1, gf_ref[...]).astype(BF16)

    return pl.pallas_call(
        body, name="out_proj", grid=(L // TL,),
        in_specs=[_rows(AW), _rows(SW), _whole((1, AW)), _whole((1, SW)), _whole((D, D)), _rows(D), _whole((1, D))],
        out_specs=[_rows(D), _rows(D), _rows(D)],
        out_shape=[jax.ShapeDtypeStruct((L, D), BF16), jax.ShapeDtypeStruct((L, D), F32),
                   jax.ShapeDtypeStruct((L, D), BF16)],
        compiler_params=_cp(("parallel",)),
    )(attn, ysg, ga, gs, w_out, x, gf)


def _out_proj_dx(dx1, w_out, attn, ysg, ga, gs):
    def body(dx_ref, w_ref, a_ref, s_ref, ga_ref, gs_ref, da_ref, ds_ref, dga_ref, dgs_ref):
        @pl.when(pl.program_id(0) == 0)
        def _():
            dga_ref[...] = jnp.zeros_like(dga_ref)
            dgs_ref[...] = jnp.zeros_like(dgs_ref)

        dm = lax.dot_general(dx_ref[...].astype(BF16), w_ref[...], _NT, preferred_element_type=F32)
        dxa, dga = _rms_bwd_tile(a_ref[...], ga_ref[...], dm[:, 0:AW])
        da_ref[...] = dxa
        dga_ref[...] += dga
        dxs, dgs = _rms_bwd_tile(s_ref[...], gs_ref[...], dm[:, AW:AW + SW])
        ds_ref[...] = dxs
        dgs_ref[...] += dgs

    return pl.pallas_call(
        body, name="out_proj_dx", grid=(L // TL,),
        in_specs=[_rows(D), _whole((D, D)), _rows(AW), _rows(SW), _whole((1, AW)), _whole((1, SW))],
        out_specs=[_rows(AW), _rows(SW), _whole((1, AW)), _whole((1, SW))],
        out_shape=[jax.ShapeDtypeStruct((L, AW), F32), jax.ShapeDtypeStruct((L, SW), F32),
                   jax.ShapeDtypeStruct((1, AW), F32), jax.ShapeDtypeStruct((1, SW), F32)],
        compiler_params=_cp(("arbitrary",)),
    )(dx1, w_out, attn, ysg, ga, gs)


def _rope_tables():
    half = HD // 8
    inv_freq = jnp.power(ROPE_THETA, -jnp.arange(half, dtype=F32) / half)
    ang = jnp.arange(L, dtype=F32)[:, None] * inv_freq[None, :]
    cos, sin = jnp.cos(ang), jnp.sin(ang)
    one = jnp.ones((L, HD - 2 * half), F32)
    zero = jnp.zeros((L, HD - 2 * half), F32)
    zh = jnp.zeros((L, half), F32)
    cos64 = jnp.concatenate([cos, cos, one], axis=1)
    sa64 = jnp.concatenate([-sin, zh, zero], axis=1)
    sb64 = jnp.concatenate([zh, sin, zero], axis=1)
    return [jnp.tile(t, (1, 2)) for t in (cos64, sa64, sb64)]


def _rope(xv, cosv, sav, sbv, sign):
    return xv * cosv + sign * (pltpu.roll(xv, 120, 1) * sav + pltpu.roll(xv, 8, 1) * sbv)


def _rope_bwd(dq, dk, dv, du, tabs):
    def body(dq_ref, dk_ref, dv_ref, du_ref, c_ref, sa_ref, sb_ref, o_ref):
        for j in range(AW // 128):
            cols = slice(j * 128, (j + 1) * 128)
            o_ref[:, cols] = _rope(dq_ref[:, cols], c_ref[...], sa_ref[...], sb_ref[...], -1.0).astype(BF16)
        o_ref[:, AW:AW + KVW] = _rope(dk_ref[...], c_ref[...], sa_ref[...], sb_ref[...], -1.0).astype(BF16)
        o_ref[:, AW + KVW:AW + 2 * KVW] = dv_ref[...].astype(BF16)
        o_ref[:, AW + 2 * KVW:] = du_ref[...].astype(BF16)

    def row(width):
        return pl.BlockSpec((TL, width), lambda i: (i, 0))

    return pl.pallas_call(
        body, name="rope_bwd", grid=(L // TL,),
        in_specs=[row(AW), row(KVW), row(KVW), row(SW), row(128), row(128), row(128)],
        out_specs=row(INW), out_shape=jax.ShapeDtypeStruct((L, INW), BF16),
        compiler_params=_cp(("parallel",)),
    )(dq, dk, dv, du, *tabs)


def _attn_window(n):
    start = pl.multiple_of(jnp.clip((n - 1) * BLK, 0, L - WIN), BLK)
    qpos = n * BLK + lax.broadcasted_iota(jnp.int32, (BLK, WIN), 0)
    kpos = start + lax.broadcasted_iota(jnp.int32, (BLK, WIN), 1)
    return start, jnp.abs(kpos - qpos) <= BLK


_NT = (((1,), (1,)), ((), ()))
_TN = (((0,), (0,)), ((), ()))
NEG = -1e30


def _attn_fwd(qkv, sink, gather=()):
    ng = len(gather)

    def body(sink_ref, q_ref, k_ref, v_ref, *rest):
        o_ref, lse_ref = rest[ng], rest[ng + 1]
        s_scr, p_scr = rest[2 * ng + 2], rest[2 * ng + 3]
        n = pl.program_id(0)
        if ng:
            travellers = (rest[:ng], rest[ng + 2:2 * ng + 2], *rest[2 * ng + 4:])
            _gather_under(n, L // BLK, travellers)

        start, valid = _attn_window(n)
        kw = k_ref[pl.ds(start, WIN), :]
        vw = v_ref[pl.ds(start, WIN), :]
        for h in range(NQ):
            kv = h // (NQ // NKV)
            s_scr[h] = lax.dot_general(q_ref[:, h * HD:(h + 1) * HD], kw[:, kv * HD:(kv + 1) * HD], _NT,
                                       preferred_element_type=F32)
        for h in range(NQ):
            s = jnp.where(valid, s_scr[h] * (HD ** -0.5), NEG)
            sk = sink_ref[h]
            m = jnp.maximum(jnp.max(s, axis=-1, keepdims=True), sk)
            p = jnp.exp(s - m)
            den = jnp.sum(p, axis=-1, keepdims=True) + jnp.exp(sk - m)
            p_scr[h] = (p / den).astype(BF16)
            lse_ref[:, h:h + 1] = m + jnp.log(den)
        for h in range(NQ):
            kv = h // (NQ // NKV)
            o_ref[:, h * HD:(h + 1) * HD] = jnp.dot(p_scr[h], vw[:, kv * HD:(kv + 1) * HD],
                                                    preferred_element_type=F32)
        if ng:
            _gather_done(n, L // BLK, travellers)

    outs = pl.pallas_call(
        body, name="attn_fwd", grid=(L // BLK,),
        in_specs=[pl.BlockSpec(memory_space=pltpu.SMEM),
                  pl.BlockSpec((BLK, AW), lambda n: (n, 0)),
                  pl.BlockSpec((L, KVW), lambda n: (0, AW // KVW)),
                  pl.BlockSpec((L, KVW), lambda n: (0, AW // KVW + 1))] + [ANY] * ng,
        out_specs=[pl.BlockSpec((BLK, AW), lambda n: (n, 0)), pl.BlockSpec((BLK, NQ), lambda n: (n, 0))] + [ANY] * ng,
        out_shape=[jax.ShapeDtypeStruct((L, AW), F32), jax.ShapeDtypeStruct((L, NQ), F32)] + _gather_shapes(gather),
        scratch_shapes=[pltpu.VMEM((NQ, BLK, WIN), F32), pltpu.VMEM((NQ, BLK, WIN), BF16)]
        + (_gather_sems(ng) if ng else []),
        compiler_params=_cp(("arbitrary",) if ng else ("parallel",)),
    )(sink, qkv, qkv, qkv, *gather)
    return outs[0], outs[1], list(outs[2:])


def _attn_bwd(qkv, sink, attn, lse, dattn, ride=()):
    kind, riding = ride if ride else (None, ())
    nr = len(riding)

    def body(*refs):
        if nr:
            riders = (refs[7:7 + nr], refs[11 + nr:11 + 2 * nr], *refs[15 + 2 * nr:])

            @pl.when(pl.program_id(0) == 0)
            def _():
                _start_all(_exchange_copies(kind, *riders))

        compute(*refs[:7], *refs[7 + nr:11 + nr], *refs[11 + 2 * nr:15 + 2 * nr])
        if nr:
            @pl.when(pl.program_id(0) == L // BLK - 1)
            def _():
                _wait_all(_exchange_copies(kind, *riders))

    def compute(sink_ref, q_ref, k_ref, v_ref, o_ref, lse_ref, do_ref, dq_ref, dk_ref, dv_ref, dsink_ref,
                s_scr, dp_scr, p_scr, ds_scr):
        n = pl.program_id(0)

        @pl.when(n == 0)
        def _():
            dk_ref[...] = jnp.zeros_like(dk_ref)
            dv_ref[...] = jnp.zeros_like(dv_ref)
            dsink_ref[...] = jnp.zeros_like(dsink_ref)

        start, valid = _attn_window(n)
        kw = k_ref[pl.ds(start, WIN), :]
        vw = v_ref[pl.ds(start, WIN), :]
        group = NQ // NKV
        for h in range(NQ):
            kv = h // group
            s_scr[h] = lax.dot_general(q_ref[:, h * HD:(h + 1) * HD], kw[:, kv * HD:(kv + 1) * HD], _NT,
                                       preferred_element_type=F32)
            dp_scr[h] = lax.dot_general(do_ref[:, h * HD:(h + 1) * HD].astype(BF16), vw[:, kv * HD:(kv + 1) * HD],
                                        _NT, preferred_element_type=F32)
        for h in range(NQ):
            dd = jnp.sum(do_ref[:, h * HD:(h + 1) * HD] * o_ref[:, h * HD:(h + 1) * HD], axis=-1, keepdims=True)
            lse_h = lse_ref[:, h:h + 1]
            p = jnp.where(valid, jnp.exp(s_scr[h] * (HD ** -0.5) - lse_h), 0.0)
            p_scr[h] = p.astype(BF16)
            ds_scr[h] = (p * (dp_scr[h] - dd) * (HD ** -0.5)).astype(BF16)
            dsk = -jnp.sum(jnp.exp(sink_ref[h] - lse_h) * dd, axis=0, keepdims=True)
            dsink_ref[h:h + 1, :] += jnp.broadcast_to(dsk, (1, 128))
        for kv in range(NKV):
            kh = kw[:, kv * HD:(kv + 1) * HD]
            dk_acc = jnp.zeros((WIN, HD), F32)
            dv_acc = jnp.zeros((WIN, HD), F32)
            for h in range(kv * group, (kv + 1) * group):
                dq_ref[:, h * HD:(h + 1) * HD] = jnp.dot(ds_scr[h], kh, preferred_element_type=F32)
                dk_acc += lax.dot_general(ds_scr[h], q_ref[:, h * HD:(h + 1) * HD], _TN, preferred_element_type=F32)
                dv_acc += lax.dot_general(p_scr[h], do_ref[:, h * HD:(h + 1) * HD].astype(BF16), _TN,
                                          preferred_element_type=F32)
            dk_ref[pl.ds(start, WIN), kv * HD:(kv + 1) * HD] += dk_acc
            dv_ref[pl.ds(start, WIN), kv * HD:(kv + 1) * HD] += dv_acc

    qblk = pl.BlockSpec((BLK, AW), lambda n: (n, 0))
    full = pl.BlockSpec((L, KVW), lambda n: (0, 0))
    outs = pl.pallas_call(
        body, name="attn_bwd", grid=(L // BLK,),
        in_specs=[pl.BlockSpec(memory_space=pltpu.SMEM), qblk,
                  pl.BlockSpec((L, KVW), lambda n: (0, AW // KVW)),
                  pl.BlockSpec((L, KVW), lambda n: (0, AW // KVW + 1)),
                  qblk, pl.BlockSpec((BLK, NQ), lambda n: (n, 0)), qblk] + [ANY] * nr,
        out_specs=[qblk, full, full, pl.BlockSpec((NQ, 128), lambda n: (0, 0))] + [ANY] * nr,
        out_shape=[jax.ShapeDtypeStruct((L, AW), F32), jax.ShapeDtypeStruct((L, KVW), F32),
                   jax.ShapeDtypeStruct((L, KVW), F32), jax.ShapeDtypeStruct((NQ, 128), F32)]
        + (_exchange_shapes(kind, riding) if nr else []),
        scratch_shapes=[pltpu.VMEM((NQ, BLK, WIN), F32), pltpu.VMEM((NQ, BLK, WIN), F32),
                        pltpu.VMEM((NQ, BLK, WIN), BF16), pltpu.VMEM((NQ, BLK, WIN), BF16)]
        + (_exchange_sems(kind, nr) if nr else []),
        compiler_params=_cp(("arbitrary",)),
    )(sink, qkv, qkv, qkv, attn, lse, dattn, *riding)
    return list(outs[:4]), list(outs[4:])


def _perm(a):
    return a.reshape(NSEG, TSEG, a.shape[1]).transpose(1, 0, 2).reshape(L, a.shape[1])


def _unperm(a):
    return a.reshape(TSEG, NSEG, a.shape[1]).transpose(1, 0, 2).reshape(L, a.shape[1])


def _cmul(ar, ai, br, bi):
    return ar * br - ai * bi, ar * bi + ai * br


def _scan_inplace(s_ref, lr, li, rev, visit=None, carried=(), out_ref=None):
    n = lr.shape[1]
    lr8 = jnp.broadcast_to(lr, (NSEG, n))
    li8 = jnp.broadcast_to(li, (NSEG, n))

    def rows(k):
        return pl.ds(pl.multiple_of(jnp.where(rev, TSEG - 1 - k, k) * NSEG, NSEG), NSEG)

    def step(k, c, store):
        sr, si = c
        rs = rows(k)
        pr, pi = _cmul(lr8, li8, sr, si)
        nr = pr + s_ref[rs, 0:n]
        ni = pi + s_ref[rs, n:2 * n]
        if store:
            dst = s_ref if out_ref is None else out_ref
            dst[rs, 0:n] = nr.astype(dst.dtype)
            dst[rs, n:2 * n] = ni.astype(dst.dtype)
        return nr, ni

    z = jnp.zeros((NSEG, n), F32)
    er, ei = lax.fori_loop(0, TSEG, functools.partial(step, store=False), (z, z))
    pr, pi = lr, li
    for _ in range(int(math.log2(TSEG))):
        pr, pi = _cmul(pr, pi, pr, pi)

    seg = lax.broadcasted_iota(jnp.int32, (NSEG, n), 0)

    def moved(val, k):
        down = jnp.where(seg >= k, pltpu.roll(val, k, 0), 0.0)
        up = jnp.where(seg < NSEG - k, pltpu.roll(val, NSEG - k, 0), 0.0)
        return jnp.where(rev, up, down)

    k = 1
    while k < NSEG:
        mr, mi = _cmul(pr, pi, moved(er, k), moved(ei, k))
        er, ei = er + mr, ei + mi
        pr, pi = _cmul(pr, pi, pr, pi)
        k *= 2
    cin_r, cin_i = moved(er, 1), moved(ei, 1)
    if visit is None:
        lax.fori_loop(0, TSEG, functools.partial(step, store=True), (cin_r, cin_i))
        return cin_r, cin_i

    def visited(k, c):
        nr, ni = step(k, c[:2], True)
        return (nr, ni) + tuple(visit(k, nr, ni, c[2:]))

    fin = lax.fori_loop(0, TSEG - 1, visited, (cin_r, cin_i) + tuple(carried))
    last_r, last_i = step(TSEG - 1, fin[:2], True)
    return last_r, last_i, fin[2:]


S5_RC = 512


def _s5_specs():
    u_spec = pl.BlockSpec((L, 128), lambda cb, h, d: (0, cb))
    b_spec = pl.BlockSpec((None, None, 128, 2 * SBW), lambda cb, h, d: (d, cb * 2 + h, 0, 0))
    c_spec = pl.BlockSpec((None, None, 2 * SBW, 128), lambda cb, h, d: (d, cb * 2 + h, 0, 0))
    l_spec = pl.BlockSpec((None, None, 1, SBW), lambda cb, h, d: (d, cb * 2 + h, 0, 0))
    d_spec = pl.BlockSpec((1, 128), lambda cb, h, d: (0, cb))
    return u_spec, b_spec, c_spec, l_spec, d_spec


def _s5_fwd(u_p, bcat, ccat, lam_re, lam_im, dskip, gather=()):
    ng = len(gather)
    grid = (SW // 128, 2, 2)

    def body(*refs):
        if ng:
            step = (pl.program_id(0) * grid[1] + pl.program_id(1)) * grid[2] + pl.program_id(2)
            travellers = (refs[6:6 + ng], refs[8 + ng:8 + 2 * ng], *refs[9 + 2 * ng:])
            _gather_under(step, math.prod(grid), travellers)
        compute(*refs[:6], *refs[6 + ng:8 + ng], refs[8 + 2 * ng])
        if ng:
            _gather_done(step, math.prod(grid), travellers)

    def compute(u_ref, b_ref, c_ref, lr_ref, li_ref, d_ref, y_ref, sb_ref, s_scr):
        first = (pl.program_id(1) == 0) & (pl.program_id(2) == 0)

        def proj(i, _):
            rs = pl.ds(pl.multiple_of(i * S5_RC, S5_RC), S5_RC)
            s_scr[rs, :] = jnp.dot(u_ref[rs, :].astype(BF16), b_ref[...], preferred_element_type=F32)
            return 0

        lax.fori_loop(0, L // S5_RC, proj, 0)
        _scan_inplace(s_scr, lr_ref[...], li_ref[...], pl.program_id(2) == 1, out_ref=sb_ref)

        def out(i, _, opening):
            rs = pl.ds(pl.multiple_of(i * S5_RC, S5_RC), S5_RC)
            yv = jnp.dot(sb_ref[rs, :], c_ref[...], preferred_element_type=F32)
            if opening:
                y_ref[rs, :] = d_ref[...] * u_ref[rs, :] + yv
            else:
                y_ref[rs, :] += yv
            return 0

        @pl.when(first)
        def _():
            lax.fori_loop(0, L // S5_RC, functools.partial(out, opening=True), 0)

        @pl.when(jnp.logical_not(first))
        def _():
            lax.fori_loop(0, L // S5_RC, functools.partial(out, opening=False), 0)

    u_spec, b_spec, c_spec, l_spec, d_spec = _s5_specs()
    outs = pl.pallas_call(
        body, name="s5_fwd", grid=grid,
        in_specs=[u_spec, b_spec, c_spec, l_spec, l_spec, d_spec] + [ANY] * ng,
        out_specs=[u_spec, _s5_state_spec()] + [ANY] * ng,
        out_shape=[jax.ShapeDtypeStruct((L, SW), F32), jax.ShapeDtypeStruct((2, NSB, L, 2 * SBW), BF16)]
        + _gather_shapes(gather),
        scratch_shapes=[pltpu.VMEM((L, 2 * SBW), F32)] + (_gather_sems(ng) if ng else []),
        compiler_params=_cp(("arbitrary",) * 3 if ng else ("parallel", "arbitrary", "arbitrary")),
    )(u_p, bcat, ccat, lam_re, lam_im, dskip, *gather)
    return outs[0], outs[1], list(outs[2:])


def _s5_state_spec():
    return pl.BlockSpec((None, None, L, 2 * SBW), lambda cb, h, d: (d, cb * 2 + h, 0, 0))


def _s5_bwd(u_p, dy_p, states, bcat, ccat, lam_re, lam_im, dskip, ride=()):
    kind, riding = ride if ride else (None, ())
    nr = len(riding)
    grid = (SW // 128, 2, 2)

    def body(*refs):
        work = refs[:8] + refs[8 + nr:14 + nr] + refs[14 + 2 * nr:16 + 2 * nr]
        if nr:
            riders = (refs[8:8 + nr], refs[14 + nr:14 + 2 * nr], *refs[16 + 2 * nr:])
            step = [pl.program_id(d) for d in range(3)]

            @pl.when((step[0] == 0) & (step[1] == 0) & (step[2] == 0))
            def _():
                _start_all(_exchange_copies(kind, *riders))

        compute(*work)
        if nr:
            @pl.when((step[0] == grid[0] - 1) & (step[1] == grid[1] - 1) & (step[2] == grid[2] - 1))
            def _():
                _wait_all(_exchange_copies(kind, *riders))

    def compute(u_ref, dy_ref, s_ref, b_ref, c_ref, lr_ref, li_ref, d_ref,
                du_ref, db_ref, dc_ref, dlr_ref, dli_ref, dd_ref, g_scr, gb_scr):
        first = (pl.program_id(1) == 0) & (pl.program_id(2) == 0)
        rev = pl.program_id(2) == 1

        def dstate(i, _):
            rs = pl.ds(pl.multiple_of(i * S5_RC, S5_RC), S5_RC)
            g_scr[rs, :] = lax.dot_general(dy_ref[rs, :].astype(BF16), c_ref[...], _NT, preferred_element_type=F32)
            return 0

        lax.fori_loop(0, L // S5_RC, dstate, 0)

        def before(rows):
            sv = s_ref[rows, :].astype(F32)
            return sv[:, 0:SBW], sv[:, SBW:2 * SBW]

        def dlam(gr, gi, sr, si, ar, ai):
            return ar + gr * sr + gi * si, ai + gi * sr - gr * si

        def visit(k, gr, gi, acc):
            ts = jnp.where(rev, k + 1, TSEG - 2 - k)
            sr, si = before(pl.ds(pl.multiple_of(ts * NSEG, NSEG), NSEG))
            return dlam(gr, gi, sr, si, *acc)

        z = jnp.zeros((NSEG, SBW), F32)
        gr, gi, acc = _scan_inplace(g_scr, lr_ref[...], -li_ref[...], jnp.logical_not(rev), visit, (z, z), gb_scr)
        edge_r, edge_i = before(pl.ds(pl.multiple_of(jnp.where(rev, 0, TSEG - 1) * NSEG, NSEG), NSEG))
        seg = lax.broadcasted_iota(jnp.int32, (NSEG, SBW), 0)
        keep = seg != jnp.where(rev, NSEG - 1, 0)

        def neighbour(e):
            return jnp.where(keep, jnp.where(rev, pltpu.roll(e, NSEG - 1, 0), pltpu.roll(e, 1, 0)), 0.0)

        ar, ai = dlam(gr, gi, neighbour(edge_r), neighbour(edge_i), *acc)
        dlr_ref[...] = jnp.sum(ar, axis=0, keepdims=True)
        dli_ref[...] = jnp.sum(ai, axis=0, keepdims=True)

        db_ref[...] = jnp.zeros_like(db_ref)
        dc_ref[...] = jnp.zeros_like(dc_ref)

        @pl.when(first)
        def _():
            dd_ref[...] = jnp.zeros_like(dd_ref)

        def grads(i, _, opening):
            rs = pl.ds(pl.multiple_of(i * S5_RC, S5_RC), S5_RC)
            uv = u_ref[rs, :]
            dyv = dy_ref[rs, :]
            gb = gb_scr[rs, :]
            db_ref[...] += lax.dot_general(uv.astype(BF16), gb, _TN, preferred_element_type=F32)
            dc_ref[...] += lax.dot_general(dyv.astype(BF16), s_ref[rs, :], _TN, preferred_element_type=F32)
            duv = lax.dot_general(gb, b_ref[...], _NT, preferred_element_type=F32)
            if opening:
                du_ref[rs, :] = d_ref[...] * dyv + duv
                dd_ref[...] += jnp.sum(dyv * uv, axis=0, keepdims=True)
            else:
                du_ref[rs, :] += duv
            return 0

        @pl.when(first)
        def _():
            lax.fori_loop(0, L // S5_RC, functools.partial(grads, opening=True), 0)

        @pl.when(jnp.logical_not(first))
        def _():
            lax.fori_loop(0, L // S5_RC, functools.partial(grads, opening=False), 0)

    u_spec, b_spec, c_spec, l_spec, d_spec = _s5_specs()
    outs = pl.pallas_call(
        body, name="s5_bwd", grid=grid,
        in_specs=[u_spec, u_spec, _s5_state_spec(), b_spec, c_spec, l_spec, l_spec, d_spec] + [ANY] * nr,
        out_specs=[u_spec, b_spec, b_spec, l_spec, l_spec, d_spec] + [ANY] * nr,
        out_shape=[jax.ShapeDtypeStruct((L, SW), F32),
                   jax.ShapeDtypeStruct((2, NSB, 128, 2 * SBW), F32), jax.ShapeDtypeStruct((2, NSB, 128, 2 * SBW), F32),
                   jax.ShapeDtypeStruct((2, NSB, 1, SBW), F32), jax.ShapeDtypeStruct((2, NSB, 1, SBW), F32),
                   jax.ShapeDtypeStruct((1, SW), F32)] + (_exchange_shapes(kind, riding) if nr else []),
        scratch_shapes=[pltpu.VMEM((L, 2 * SBW), F32), pltpu.VMEM((L, 2 * SBW), BF16)]
        + (_exchange_sems(kind, nr) if nr else []),
        compiler_params=_cp(("arbitrary",) * 3 if nr else ("parallel", "arbitrary", "arbitrary")),
    )(u_p, dy_p, states, bcat, ccat, lam_re, lam_im, dskip, *riding)
    return list(outs[:6]), list(outs[6:])


def _s5_params(a_re, a_im, log_step, bt_re, bt_im):
    lam = lax.complex(a_re, a_im)
    step = jnp.exp(log_step)[..., None]
    lam_bar = jnp.exp(lam * step)
    b_bar = ((lam_bar - 1.0) / lam)[..., None, :] * lax.complex(bt_re, bt_im)
    return jnp.real(lam_bar), jnp.imag(lam_bar), jnp.real(b_bar), jnp.imag(b_bar)


def _sel():
    i = jnp.arange(8)[None, :, None]
    j = jnp.arange(4)[None, None, :]
    r = jnp.arange(2)[:, None, None]
    return (i == r * 4 + j).astype(F32)


def _to_bcat(bt_re, bt_im):
    def one(bt):
        return jnp.einsum('dkrjcp,rij->dkricjp', bt.reshape(2, 4, 2, 4, GC, NP), _sel()).reshape(2, NSB, 128, SBW)
    return jnp.concatenate([one(bt_re), one(bt_im)], axis=-1)


def _from_bcat(dbcat):
    def one(dbbd):
        return jnp.einsum('dkricjp,rij->dkrjcp', dbbd.reshape(2, 4, 2, 8, GC, 4, NP), _sel()).reshape(2, NG, GC, NP)
    return one(dbcat[..., :SBW]), one(dbcat[..., SBW:])


def _to_ccat(c_re, c_im):
    def one(cc):
        return jnp.einsum('dkrjcp,rij->dkrjpic', cc.reshape(2, 4, 2, 4, GC, NP), _sel()).reshape(2, NSB, SBW, 128)
    return jnp.concatenate([one(c_re), -one(c_im)], axis=-2)


def _from_ccat(dccat):
    def one(dcbd):
        return jnp.einsum('dkrjpic,rij->dkrjcp', dcbd.reshape(2, 4, 2, 4, NP, 8, GC), _sel()).reshape(2, NG, GC, NP)
    return one(dccat[:, :, :SBW]), -one(dccat[:, :, SBW:])


def _gelu(y):
    return 0.5 * y * (1.0 + lax.erf(y * (2.0 ** -0.5)))


def _gelu_grad(y):
    return 0.5 * (1.0 + lax.erf(y * (2.0 ** -0.5))) + y * jnp.exp(-0.5 * y * y) * ((2.0 * math.pi) ** -0.5)


def _sigmoid(z):
    return 0.5 * jnp.tanh(0.5 * z) + 0.5


def _glu_fwd(y, wg):
    def body(y_ref, w_ref, o_ref, z_ref):
        ys = _gelu(y_ref[...])
        z = jnp.dot(ys.astype(BF16), w_ref[...], preferred_element_type=F32)
        z_ref[...] = z
        o_ref[...] = ys * _sigmoid(z)

    row = pl.BlockSpec((TL, SW), lambda i: (i, 0))
    return pl.pallas_call(
        body, name="glu_fwd", grid=(L // TL,),
        in_specs=[row, pl.BlockSpec((SW, SW), lambda i: (0, 0))], out_specs=[row, row],
        out_shape=[jax.ShapeDtypeStruct((L, SW), F32), jax.ShapeDtypeStruct((L, SW), F32)],
        compiler_params=_cp(("parallel",)),
    )(y, wg)


def _glu_bwd(y, z, dout, wg):
    def body(y_ref, z_ref, do_ref, w_ref, dy_ref, dw_ref):
        @pl.when(pl.program_id(0) == 0)
        def _():
            dw_ref[...] = jnp.zeros_like(dw_ref)

        yv = y_ref[...]
        ys = _gelu(yv)
        sg = _sigmoid(z_ref[...])
        dov = do_ref[...]
        dz = (dov * ys * sg * (1.0 - sg)).astype(BF16)
        dys = dov * sg + lax.dot_general(dz, w_ref[...], _NT, preferred_element_type=F32)
        dy_ref[...] = dys * _gelu_grad(yv)
        dw_ref[...] += lax.dot_general(ys.astype(BF16), dz, _TN, preferred_element_type=F32)

    row = pl.BlockSpec((TL, SW), lambda i: (i, 0))
    wsp = pl.BlockSpec((SW, SW), lambda i: (0, 0))
    return pl.pallas_call(
        body, name="glu_bwd", grid=(L // TL,),
        in_specs=[row, row, row, wsp], out_specs=[row, wsp],
        out_shape=[jax.ShapeDtypeStruct((L, SW), F32), jax.ShapeDtypeStruct((SW, SW), F32)],
        compiler_params=_cp(("arbitrary",)),
    )(y, z, dout, wg)


CT = 256
CR = 128
NCT = DFF // CT


def _shifted(ref, r):
    h = 8 * (4 // ref.dtype.itemsize)
    cur = ref[pl.ds(r, CR), :].astype(F32)
    before = ref[pl.ds(pl.multiple_of(jnp.maximum(r - h, 0), h), h), :][h - 1:h, :].astype(F32)
    after = ref[pl.ds(pl.multiple_of(jnp.minimum(r + CR, L - h), h), h), :][0:1, :].astype(F32)
    before = jnp.where(r > 0, before, 0.0)
    after = jnp.where(r + CR < L, after, 0.0)
    row = lax.broadcasted_iota(jnp.int32, cur.shape, 0)
    prev = jnp.where(row == 0, before, pltpu.roll(cur, 1, 0))
    nxt = jnp.where(row == CR - 1, after, pltpu.roll(cur, CR - 1, 0))
    return prev, cur, nxt


def _conv3(ref, r, w_ref, b_ref):
    prev, cur, nxt = _shifted(ref, r)
    return w_ref[0:1, :] * prev + w_ref[1:2, :] * cur + w_ref[2:3, :] * nxt + b_ref[...]


def _convact_fwd(up, conv_w, conv_b):
    def body(ug_ref, uv_ref, wg_ref, wv_ref, bg_ref, bv_ref, o_ref, g_ref, v_ref):
        def chunk(i, _):
            r = pl.multiple_of(i * CR, CR)
            rs = pl.ds(r, CR)
            g = _conv3(ug_ref, r, wg_ref, bg_ref)
            v = _conv3(uv_ref, r, wv_ref, bv_ref)
            o_ref[rs, :] = (g * _sigmoid(g) * v).astype(BF16)
            g_ref[rs, :] = g.astype(BF16)
            v_ref[rs, :] = v.astype(BF16)
            return 0

        lax.fori_loop(0, L // CR, chunk, 0)

    gcol = pl.BlockSpec((L, CT), lambda j: (0, j))
    vcol = pl.BlockSpec((L, CT), lambda j: (0, j + NCT))
    return pl.pallas_call(
        body, name="convact_fwd", grid=(NCT,),
        in_specs=[gcol, vcol,
                  pl.BlockSpec((3, CT), lambda j: (0, j)), pl.BlockSpec((3, CT), lambda j: (0, j + NCT)),
                  pl.BlockSpec((1, CT), lambda j: (0, j)), pl.BlockSpec((1, CT), lambda j: (0, j + NCT))],
        out_specs=[gcol, gcol, gcol], out_shape=[jax.ShapeDtypeStruct((L, DFF), BF16)] * 3,
        compiler_params=_cp(("parallel",)),
    )(up, up, conv_w, conv_w, conv_b, conv_b)


def _convact_bwd(up, gq, vq, dact, conv_w):
    def body(ug_ref, uv_ref, g_ref, v_ref, da_ref, wg_ref, wv_ref, du_ref, dw_ref, db_ref, dgs, dvs, dbv):
        half = pl.program_id(1)

        def transpose_conv(src, u_ref, w_ref):
            dw_ref[...] = jnp.zeros_like(dw_ref)

            def chunk(i, _):
                r = pl.multiple_of(i * CR, CR)
                rs = pl.ds(r, CR)
                prev, cur, nxt = _shifted(src, r)
                du_ref[rs, :] = (w_ref[0:1, :] * nxt + w_ref[1:2, :] * cur + w_ref[2:3, :] * prev).astype(BF16)
                uv = u_ref[rs, :].astype(F32)
                for k, d in enumerate((nxt, cur, prev)):
                    dw_ref[k:k + 1, :] += jnp.sum(d * uv, axis=0, keepdims=True)
                return 0

            lax.fori_loop(0, L // CR, chunk, 0)

        @pl.when(half == 0)
        def _():
            db_ref[...] = jnp.zeros_like(db_ref)
            dbv[...] = jnp.zeros_like(dbv)

            def chunk1(i, _):
                rs = pl.ds(pl.multiple_of(i * CR, CR), CR)
                g = g_ref[rs, :].astype(F32)
                v = v_ref[rs, :].astype(F32)
                sg = _sigmoid(g)
                da = da_ref[rs, :].astype(F32)
                dv = da * g * sg
                dg = da * v * sg * (1.0 + g * (1.0 - sg))
                dgs[rs, :] = dg
                dvs[rs, :] = dv
                db_ref[...] += jnp.sum(dg, axis=0, keepdims=True)
                dbv[0:1, :] += jnp.sum(dv, axis=0, keepdims=True)
                return 0

            lax.fori_loop(0, L // CR, chunk1, 0)
            transpose_conv(dgs, ug_ref, wg_ref)

        @pl.when(half == 1)
        def _():
            db_ref[...] = dbv[0:1, :]
            transpose_conv(dvs, uv_ref, wv_ref)

    def col(rows, off):
        return pl.BlockSpec((rows, CT), lambda j, h: (0, j + off))

    def out(rows):
        return pl.BlockSpec((rows, CT), lambda j, h: (0, j + h * NCT))

    return pl.pallas_call(
        body, name="convact_bwd", grid=(NCT, 2),
        in_specs=[col(L, 0), col(L, NCT), col(L, 0), col(L, 0), col(L, 0), col(3, 0), col(3, NCT)],
        out_specs=[out(L), out(3), out(1)],
        out_shape=[jax.ShapeDtypeStruct((L, 2 * DFF), BF16), jax.ShapeDtypeStruct((3, 2 * DFF), F32),
                   jax.ShapeDtypeStruct((1, 2 * DFF), F32)],
        scratch_shapes=[pltpu.VMEM((L, CT), F32), pltpu.VMEM((L, CT), F32), pltpu.VMEM((8, CT), F32)],
        compiler_params=_cp(("parallel", "arbitrary")),
    )(up, up, gq, vq, dact, conv_w, conv_w)


def _local_step(x, tgt, w_in_t, p, attend, scan, stage):
    tabs = _rope_tables()
    lam_re, lam_im, bb_re, bb_im = _s5_params(p["a_re"], p["a_im"], p["log_step"], p["bt_re"], p["bt_im"])
    bcat = _to_bcat(bb_re, bb_im).astype(BF16)
    ccat = _to_ccat(p["c_re"], p["c_im"]).astype(BF16)
    lam_re4, lam_im4 = lam_re.reshape(2, NSB, 1, SBW), lam_im.reshape(2, NSB, 1, SBW)
    dskip = p["d_skip"].reshape(1, SW)
    g_mix, g_ffn, g_fin = p["norm_mix_g"].reshape(1, D), p["norm_ffn_g"].reshape(1, D), p["norm_final_g"].reshape(1, D)
    g_attn, g_ssm = p["norm_attn_g"].reshape(1, AW), p["norm_ssm_g"].reshape(1, SW)
    sink = p["sink"].reshape(NQ)
    conv_b = p["conv_b"].reshape(1, 2 * DFF)

    rows, gain = jax.ShapeDtypeStruct((L, D), F32), jax.ShapeDtypeStruct((1, D), F32)
    rows16 = jax.ShapeDtypeStruct((L, D), BF16)
    h1, qkv, u = _in_proj(x, g_mix, w_in_t, tabs)
    attn, lse, wts = attend(qkv, sink)
    u_p = _perm(u)
    y_p, states, more = scan(u_p, bcat, ccat, lam_re4, lam_im4, dskip)
    wts = dict(wts, **more)
    w_glu, w_out, w_up_t, w_down, conv_w = (wts[k] for k in ("w_glu", "w_out", "w_up_t", "w_down", "conv_w"))
    ysg_p, z_p = _glu_fwd(y_p, w_glu)
    ysg = _unperm(ysg_p)
    mixed, x1, h2 = _out_proj(attn, ysg, g_attn, g_ssm, w_out, x, g_ffn)
    up = _mm(h2, w_up_t, tb=True, name="ffn_up", tn=1408, out_dtype=BF16)
    act, gq, vq = _convact_fwd(up, conv_w, conv_b)
    loss, dx2, dx2b, dg_fin = _mm(
        act, w_down, add=x1, name="ffn_down", tm=512, tk=DFF,
        post=(_final_post, [tgt, g_fin], [jax.ShapeDtypeStruct((1, 1), F32), rows, rows16, gain]))

    def riding(res, ride):
        return res if ride else (res, None)

    dw_down = _mm(act, dx2b, ta=True, name="ffn_down_dw", tm=256, tk=L)
    ride = stage(("w_down",), "cores", [dw_down])
    dact, got = riding(_mm(dx2b, w_down, tb=True, name="ffn_down_dx", tn=1408, out_dtype=BF16, ride=ride), ride)
    ride = stage(("w_down",), "chips", got)
    dup, dconv_w, dconv_b = _convact_bwd(up, gq, vq, dact, conv_w)
    dw_up_t, got = riding(_mm(dup, h2, ta=True, name="ffn_up_dw", tm=512, tk=L, ride=ride), ride)
    stage(("w_down",), "done", got)
    ride = stage(("w_up_t",), "cores", [dw_up_t])
    (dx1, dx1b, dg_ffn), got = riding(
        _mm(dup, w_up_t, name="ffn_up_dx", tm=512, tk=2 * DFF, ride=ride,
            post=(_rms_bwd_post, [x1, dx2, g_ffn], [rows, rows16, gain])), ride)
    ride = stage(("w_up_t",), "chips", got)
    dattn, dysg, dg_attn, dg_ssm = _out_proj_dx(dx1b, w_out, attn, ysg, g_attn, g_ssm)
    dw_out = _mm(mixed, dx1b, ta=True, name="out_proj_dw", tm=512, tk=L)
    dy_p, dw_glu = _glu_bwd(y_p, z_p, _perm(dysg), w_glu)
    (du_p, dbcat, dccat, dlam_re, dlam_im, dd), got = _s5_bwd(u_p, dy_p, states, bcat, ccat, lam_re4, lam_im4, dskip,
                                                              ride=ride)
    stage(("w_up_t",), "done", got)
    dbb_re, dbb_im = _from_bcat(dbcat)
    dc_re, dc_im = _from_ccat(_swap(dccat))
    mix = ("w_out", "w_glu")
    ride = stage(mix, "cores", [dw_out, dw_glu])
    (dq, dk, dv, dsink), got = _attn_bwd(qkv, sink, attn, lse, dattn, ride=ride)
    ride = stage(mix, "chips", got)
    dproj = _rope_bwd(dq, dk, dv, _unperm(du_p), tabs)
    dw_in_t, got = riding(_mm(dproj, h1, ta=True, name="in_proj_dw", tm=640, tk=L, ride=ride), ride)
    stage(mix, "done", got)
    grad_x, dg_mix = _in_proj_dx(dproj, w_in_t, x, g_mix, dx1)

    big = dict(w_in_t=dw_in_t)
    small = dict(norm_mix_g=dg_mix, norm_attn_g=dg_attn, norm_ssm_g=dg_ssm, norm_ffn_g=dg_ffn, norm_final_g=dg_fin,
                 sink=dsink[:, 0], conv_b=dconv_b, d_skip=dd, conv_w=dconv_w,
                 lam_re=dlam_re.reshape(2, NG, NP), lam_im=dlam_im.reshape(2, NG, NP),
                 bb_re=dbb_re, bb_im=dbb_im, c_re=dc_re, c_im=dc_im, loss=loss.reshape(1))
    return grad_x, big, small


ANY = pl.BlockSpec(memory_space=pl.ANY)


def _coords():
    return lax.axis_index("x"), lax.axis_index("y"), lax.axis_index("c")


def _flip(v, b):
    return v + b - 2 * v * b if b else v


def _all_gather(shards, name):
    n = len(shards)

    def body(*refs):
        _gather_start(refs[:n], refs[n:2 * n], *refs[2 * n:])
        _gather_finish(refs[:n], refs[n:2 * n], *refs[2 * n:])

    return pl.pallas_call(
        body, name=name,
        in_specs=[ANY] * n, out_specs=[ANY] * n,
        out_shape=_gather_shapes(shards), scratch_shapes=_gather_sems(n),
    )(*shards)


def _gather_shapes(shards):
    return [jax.ShapeDtypeStruct((NDEV * s.shape[0], s.shape[1]), s.dtype) for s in shards]


def _gather_sems(n):
    return [pltpu.SemaphoreType.DMA((7 * n,)), pltpu.SemaphoreType.DMA((7 * n,)), pltpu.SemaphoreType.DMA((n,))]


def _gather_copies(ins, outs, send_sems, recv_sems, local_sems, a):
    x, y, c = _coords()
    me, sibling = (x, y, c), (x, y, 1 - c)
    chips = [(1 - x, y), (x, 1 - y), (1 - x, 1 - y)]
    r = ins[a].shape[0]

    def rows(px, py, pc):
        return outs[a].at[pl.ds(pl.multiple_of((4 * px + 2 * py + pc) * r, 8), r), :]

    def copy(k, block, to, src=None):
        return pltpu.make_async_remote_copy(
            src_ref=rows(*block) if src is None else src, dst_ref=rows(*block),
            send_sem=send_sems.at[a * 7 + k], recv_sem=recv_sems.at[a * 7 + k],
            device_id=to, device_id_type=pl.DeviceIdType.MESH)

    mine = pltpu.make_async_copy(ins[a], rows(*me), local_sems.at[a])
    first = [copy(0, me, sibling, src=ins[a])]
    first += [copy(1 + j, me, (*chip, c), src=ins[a]) for j, chip in enumerate(chips)]
    passed = [copy(4 + j, (*chip, c), sibling) for j, chip in enumerate(chips)]
    arrivals = [copy(1 + j, (*chip, c), me) for j, chip in enumerate(chips)]
    from_sibling = [copy(0, sibling, me)] + [copy(4 + j, (*chip, 1 - c), me) for j, chip in enumerate(chips)]
    return mine, first, passed, arrivals, from_sibling


def _gather_start(ins, outs, send_sems, recv_sems, local_sems):
    for a in range(len(ins)):
        mine, first, _, _, _ = _gather_copies(ins, outs, send_sems, recv_sems, local_sems, a)
        mine.start()
        for cp in first:
            cp.start()


def _gather_forward(ins, outs, send_sems, recv_sems, local_sems):
    for a in range(len(ins)):
        _, _, passed, arrivals, _ = _gather_copies(ins, outs, send_sems, recv_sems, local_sems, a)
        for arrived, onward in zip(arrivals, passed):
            arrived.wait_recv()
            onward.start()


def _gather_wait(ins, outs, send_sems, recv_sems, local_sems):
    for a in range(len(ins)):
        mine, first, passed, _, from_sibling = _gather_copies(ins, outs, send_sems, recv_sems, local_sems, a)
        for cp in from_sibling:
            cp.wait_recv()
        for cp in first + passed:
            cp.wait_send()
        mine.wait()


def _gather_finish(*refs):
    _gather_forward(*refs)
    _gather_wait(*refs)


def _gather_under(step, steps, travellers):
    @pl.when(step == 0)
    def _():
        _gather_start(*travellers)

    @pl.when(step == (3 * steps) // 4)
    def _():
        _gather_forward(*travellers)


def _gather_done(step, steps, travellers):
    @pl.when(step == steps - 1)
    def _():
        _gather_wait(*travellers)


NCHIP = 4
CHIP_FLIPS = ((1, 0), (0, 1), (1, 1))


def _planned_copies(ins, outs, send_sems, recv_sems, plan):
    return [pltpu.make_async_remote_copy(
        src_ref=ins[a].at[src], dst_ref=outs[a].at[dst], send_sem=send_sems.at[k], recv_sem=recv_sems.at[k],
        device_id=to, device_id_type=pl.DeviceIdType.MESH) for k, (a, src, dst, to) in enumerate(plan)]


def _start_all(copies):
    for cp in copies:
        cp.start()


def _wait_all(copies):
    for cp in copies:
        cp.wait_recv()
    for cp in copies:
        cp.wait_send()


SLOTS = {"cores": NCHIP, "chips": 3}


def _exchange_copies(kind, ins, outs, send_sems, recv_sems):
    x, y, c = _coords()
    plan = []
    for a in range(len(ins)):
        if kind == "cores":
            plan += [(a, 2 * q + 1 - c, q, (x, y, 1 - c)) for q in range(NCHIP)]
        else:
            for j, (fx, fy) in enumerate(CHIP_FLIPS):
                px, py = _flip(x, fx), _flip(y, fy)
                plan.append((a, 2 * px + py, j, (px, py, c)))
    return _planned_copies(ins, outs, send_sems, recv_sems, plan)


def _exchange_shapes(kind, parts):
    return [jax.ShapeDtypeStruct((SLOTS[kind],) + s.shape[1:], s.dtype) for s in parts]


def _exchange_sems(kind, n):
    return [pltpu.SemaphoreType.DMA((SLOTS[kind] * n,)), pltpu.SemaphoreType.DMA((SLOTS[kind] * n,))]


def _exchange(kind, parts, name):
    n = len(parts)

    def body(*refs):
        copies = _exchange_copies(kind, refs[:n], refs[n:2 * n], *refs[2 * n:])
        _start_all(copies)
        _wait_all(copies)

    return pl.pallas_call(
        body, name=name, in_specs=[ANY] * n, out_specs=[ANY] * n,
        out_shape=_exchange_shapes(kind, parts), scratch_shapes=_exchange_sems(kind, n),
    )(*parts)


def _pair_sum(where, part, recv, wire_dtype, name):
    _, r, c = part.shape
    tr = _pick(r, 256, 16)

    def body(w_ref, p_ref, r_ref, pb_ref, own_ref):
        s = p_ref[...] + r_ref[...]
        pb_ref[...] = s.astype(wire_dtype)

        @pl.when(pl.program_id(1) == w_ref[1])
        def _():
            own_ref[...] = s

    return pl.pallas_call(
        body, name=name,
        grid_spec=pltpu.PrefetchScalarGridSpec(
            num_scalar_prefetch=1, grid=(r // tr, NCHIP),
            in_specs=[pl.BlockSpec((None, tr, c), lambda i, q, w: (2 * q + w[0], i, 0)),
                      pl.BlockSpec((None, tr, c), lambda i, q, w: (q, i, 0))],
            out_specs=[pl.BlockSpec((None, tr, c), lambda i, q, w: (q, i, 0)),
                       pl.BlockSpec((tr, c), lambda i, q, w: (i, 0))]),
        out_shape=[jax.ShapeDtypeStruct((NCHIP, r, c), wire_dtype), jax.ShapeDtypeStruct((r, c), F32)],
        compiler_params=_cp(("parallel", "arbitrary")),
    )(where, part, recv)


def _chip_sum(own, recv, name):
    r, c = own.shape
    tr = _pick(r, 256, 16)

    def body(o_ref, r_ref, out_ref):
        acc = o_ref[...]
        for j in range(3):
            acc = acc + r_ref[j].astype(F32)
        out_ref[...] = acc

    return pl.pallas_call(
        body, name=name, grid=(r // tr,),
        in_specs=[pl.BlockSpec((tr, c), lambda i: (i, 0)), pl.BlockSpec((3, tr, c), lambda i: (0, i, 0))],
        out_specs=pl.BlockSpec((tr, c), lambda i: (i, 0)),
        out_shape=jax.ShapeDtypeStruct((r, c), F32),
        compiler_params=_cp(("parallel",)),
    )(own, recv)


def _adamw(w, own, recv, m, v, name):
    r, c = w.shape
    tr = _pick(r, 256, 16)

    def body(w_ref, o_ref, r_ref, m_ref, v_ref, g_ref, d_ref, nm_ref, nv_ref):
        acc = o_ref[...]
        for j in range(3):
            acc = acc + r_ref[j].astype(F32)
        g_ref[...] = acc
        _adamw_refs(w_ref, g_ref, m_ref, v_ref, d_ref, nm_ref, nv_ref)

    blk = pl.BlockSpec((tr, c), lambda i: (i, 0))
    return pl.pallas_call(
        body, name=name, grid=(r // tr,),
        in_specs=[blk, blk, pl.BlockSpec((3, tr, c), lambda i: (0, i, 0)), blk, blk], out_specs=[blk] * 4,
        out_shape=[jax.ShapeDtypeStruct((r, c), F32)] * 4,
        compiler_params=_cp(("parallel",)),
    )(w, own, recv, m, v)


def _adamw_refs(w_ref, g_ref, m_ref, v_ref, d_ref, nm_ref, nv_ref):
    gv = g_ref[...]
    nm = B1 * m_ref[...] + (1.0 - B1) * gv
    nv = B2 * v_ref[...] + (1.0 - B2) * (gv * gv)
    nm_ref[...] = nm
    nv_ref[...] = nv
    d_ref[...] = -LR * ((nm / C1) / (jnp.sqrt(nv / C2) + AEPS) + WD * w_ref[...])


def _adamw_small(ws, gs, ms, vs, name):
    n = len(ws)

    def body(*refs):
        groups = [refs[i * n:(i + 1) * n] for i in range(7)]
        for per_param in zip(*groups):
            _adamw_refs(*per_param)

    vm = pl.BlockSpec(memory_space=pltpu.VMEM)
    outs = pl.pallas_call(
        body, name=name, in_specs=[vm] * (4 * n), out_specs=[vm] * (3 * n),
        out_shape=[jax.ShapeDtypeStruct(a.shape, F32) for a in ws] * 3,
    )(*ws, *gs, *ms, *vs)
    return outs[:n], outs[n:2 * n], outs[2 * n:]


def _swap(a):
    return jnp.swapaxes(a, -1, -2)


VIEWS = {
    "w_in": (lambda a: a[0].T, lambda u: u.T[None]),
    "w_up": (lambda a: a[0].T, lambda u: u.T[None]),
    "w_glu": (lambda a: a[0], lambda u: u[None]),
    "w_out": (lambda a: a[0], lambda u: u[None]),
    "w_down": (lambda a: a[0], lambda u: u[None]),
    "conv_w": (lambda a: a[0], lambda u: u[None]),
    "norm_mix_g": (lambda a: a, lambda u: u),
    "norm_attn_g": (lambda a: a, lambda u: u),
    "norm_ssm_g": (lambda a: a, lambda u: u),
    "norm_ffn_g": (lambda a: a, lambda u: u),
    "norm_final_g": (lambda a: a[None], lambda u: u[0]),
    "conv_b": (lambda a: a, lambda u: u),
    "sink": (lambda a: a, lambda u: u),
    "a_re": (lambda a: a.reshape(2 * NG, NP), lambda u: u.reshape(1, 2, NG, NP)),
    "a_im": (lambda a: a.reshape(2 * NG, NP), lambda u: u.reshape(1, 2, NG, NP)),
    "log_step": (lambda a: a[0], lambda u: u[None]),
    "b_re": (lambda a: _swap(a[0]).reshape(2 * NG * GC, NP), lambda u: _swap(u.reshape(2, NG, GC, NP))[None]),
    "b_im": (lambda a: _swap(a[0]).reshape(2 * NG * GC, NP), lambda u: _swap(u.reshape(2, NG, GC, NP))[None]),
    "c_re": (lambda a: a.reshape(2 * NG * GC, NP), lambda u: u.reshape(1, 2, NG, GC, NP)),
    "c_im": (lambda a: a.reshape(2 * NG * GC, NP), lambda u: u.reshape(1, 2, NG, GC, NP)),
    "d_skip": (lambda a: a[0].T, lambda u: u.T[None]),
}
BIG = ["w_in", "w_glu", "w_out", "w_up", "w_down"]
PACK_W = 1024


def _pack(arrs, rows):
    flat = jnp.concatenate([a.reshape(-1).astype(F32) for a in arrs])
    return jnp.pad(flat, (0, rows * PACK_W - flat.shape[0])).reshape(rows, PACK_W)


def _unpack(packed, shapes):
    flat = packed.reshape(-1)
    out, off = [], 0
    for s in shapes:
        size = math.prod(s)
        out.append(flat[off:off + size].reshape(s))
        off += size
    return out


def kernel(x, norm_mix_g, w_in, a_re, a_im, log_step, b_re, b_im, c_re, c_im, d_skip, w_glu, sink, norm_attn_g, norm_ssm_g, w_out, norm_ffn_g, w_up, conv_w, conv_b, w_down, norm_final_g, loss_target, m_norm_mix_g, m_w_in, m_a_re, m_a_im, m_log_step, m_b_re, m_b_im, m_c_re, m_c_im, m_d_skip, m_w_glu, m_sink, m_norm_attn_g, m_norm_ssm_g, m_w_out, m_norm_ffn_g, m_w_up, m_conv_w, m_conv_b, m_w_down, m_norm_final_g, v_norm_mix_g, v_w_in, v_a_re, v_a_im, v_log_step, v_b_re, v_b_im, v_c_re, v_c_im, v_d_skip, v_w_glu, v_sink, v_norm_attn_g, v_norm_ssm_g, v_w_out, v_norm_ffn_g, v_w_up, v_conv_w, v_conv_b, v_w_down, v_norm_final_g):
    args = dict(locals())
    names = ["norm_mix_g", "w_in", "a_re", "a_im", "log_step", "b_re", "b_im", "c_re", "c_im", "d_skip", "w_glu",
             "sink", "norm_attn_g", "norm_ssm_g", "w_out", "norm_ffn_g", "w_up", "conv_w", "conv_b", "w_down",
             "norm_final_g"]
    w = {k: args[k] for k in names}
    m = {k: args["m_" + k] for k in names}
    v = {k: args["v_" + k] for k in names}

    (w_in_t,) = _all_gather([w_in[0].T.astype(BF16)], "gather_w_in")
    under_attn = dict(w_glu=w_glu[0].astype(BF16), w_out=w_out[0].astype(BF16),
                      conv_w=jnp.pad(conv_w[0], ((0, 5), (0, 0))))
    under_scan = dict(w_up_t=w_up[0].T.astype(BF16), w_down=w_down[0].astype(BF16))

    ax, ay, ac = _coords()
    me = 4 * ax + 2 * ay + ac
    where = jnp.stack([ac, 2 * ax + ay]).astype(jnp.int32)
    parts, own, got = {}, {}, {}

    def split8(g):
        return g.reshape(NDEV, g.shape[0] // NDEV, g.shape[1])

    def attend(qkv, sink_):
        attn, lse, gathered = _attn_fwd(qkv, sink_, gather=list(under_attn.values()))
        wts = dict(zip(under_attn.keys(), gathered))
        wts["conv_w"] = (wts["conv_w"].reshape(NDEV, 8, 2 * DFF // NDEV)[:, :3].transpose(1, 0, 2)
                         .reshape(3, 2 * DFF))
        return attn, lse, wts

    def scan(*operands):
        y_p, states, gathered = _s5_fwd(*operands, gather=list(under_scan.values()))
        return y_p, states, dict(zip(under_scan.keys(), gathered))

    def pair_sum(k, from_core):
        per_chip, own[k] = _pair_sum(where, parts[k], from_core, F32 if k == "small" else BF16, "pair_sum_" + k)
        return per_chip

    def stage(ks, phase, payload):
        if phase == "cores":
            parts.update({k: split8(g) for k, g in zip(ks, payload)})
            return ("cores", [parts[k] for k in ks])
        if phase == "chips":
            return ("chips", [pair_sum(k, fc) for k, fc in zip(ks, payload)])
        got.update(zip(ks, payload))
        return ()

    p = {k: w[k][0] for k in ("norm_mix_g", "a_re", "a_im", "log_step", "c_re", "c_im", "d_skip", "sink",
                              "norm_attn_g", "norm_ssm_g", "norm_ffn_g", "conv_b")}
    p["norm_final_g"] = norm_final_g
    p["bt_re"], p["bt_im"] = _swap(b_re[0]), _swap(b_im[0])
    grad_x, big, small = _local_step(x[0], loss_target[0], w_in_t, p, attend, scan, stage)

    small_names = list(small.keys())
    small_shapes = [small[k].shape for k in small_names]
    n_small = sum(math.prod(s) for s in small_shapes)
    rows_dev = -(-n_small // (PACK_W * NDEV * 16)) * 16
    spack = _pack([small[k] for k in small_names], rows_dev * NDEV)
    late = ["w_in_t", "small"]
    parts.update(w_in_t=split8(big["w_in_t"]), small=spack.reshape(NDEV, rows_dev, PACK_W))
    from_cores = _exchange("cores", [parts[k] for k in late], "exchange_cores")
    from_chips = _exchange("chips", [pair_sum(k, fc) for k, fc in zip(late, from_cores)], "exchange_chips")
    got.update(zip(late, from_chips))
    (small_full,) = _all_gather([_chip_sum(own["small"], got["small"], "chip_sum_small")], "gather_small")
    sm = dict(zip(small_names, _unpack(small_full, small_shapes)))

    _, s5_vjp = jax.vjp(_s5_params, a_re[0], a_im[0], log_step[0], p["bt_re"], p["bt_im"])
    da_re, da_im, dlog_step, dbt_re, dbt_im = s5_vjp((sm["lam_re"], sm["lam_im"], sm["bb_re"], sm["bb_im"]))
    gview = {
        "norm_mix_g": sm["norm_mix_g"], "norm_attn_g": sm["norm_attn_g"], "norm_ssm_g": sm["norm_ssm_g"],
        "norm_ffn_g": sm["norm_ffn_g"], "norm_final_g": sm["norm_final_g"], "conv_b": sm["conv_b"],
        "sink": sm["sink"][None], "a_re": da_re.reshape(2 * NG, NP), "a_im": da_im.reshape(2 * NG, NP),
        "log_step": dlog_step, "b_re": dbt_re.reshape(2 * NG * GC, NP), "b_im": dbt_im.reshape(2 * NG * GC, NP),
        "c_re": sm["c_re"].reshape(2 * NG * GC, NP), "c_im": sm["c_im"].reshape(2 * NG * GC, NP),
        "d_skip": sm["d_skip"].reshape(NG, GC).T,
        "conv_w": lax.dynamic_slice_in_dim(sm["conv_w"], me * (2 * DFF // NDEV), 2 * DFF // NDEV, axis=1),
    }

    dview, mview, vview = {}, {}, {}
    for k, kg in zip(BIG, ("w_in_t", "w_glu", "w_out", "w_up_t", "w_down")):
        to = VIEWS[k][0]
        gview[k], dview[k], mview[k], vview[k] = _adamw(to(w[k]), own[kg], got[kg], to(m[k]), to(v[k]), "adamw_" + k)
    rest = [k for k in names if k not in BIG]
    outs = _adamw_small([VIEWS[k][0](w[k]) for k in rest], [gview[k] for k in rest],
                        [VIEWS[k][0](m[k]) for k in rest], [VIEWS[k][0](v[k]) for k in rest], "adamw_small")
    for dst, vals in zip((dview, mview, vview), outs):
        dst.update(dict(zip(rest, vals)))

    def back(views):
        return [VIEWS[k][1](views[k]) for k in names]

    return (sm["loss"][0], grad_x[None], *back(gview), *back(dview), *back(mview), *back(vview))
```

```python
import functools
import math

import jax
import jax.numpy as jnp
from jax import lax
from jax.experimental import pallas as pl
from jax.experimental.pallas import tpu as pltpu

F32 = jnp.float32
BF16 = jnp.bfloat16

L = 4096
D = 1024
NQ, NKV, HD = 8, 2, 64
AW = NQ * HD
KVW = NKV * HD
SW = 512
NG, GC, NP = 32, 16, 64
INW = AW + 2 * KVW + SW
DFF = 2816
BLK = 128
WIN = 3 * BLK
EPS = 1e-6
ROPE_THETA = 500000.0
NSEG = 32
TSEG = L // NSEG
SBW = 256
NSB = NG * NP // SBW
NDEV = 8
MESH_AXES = ("x", "y", "c")

LR, B1, B2, AEPS, WD, STEP = 0.001, 0.9, 0.999, 1e-08, 0.01, 10
C1 = 1.0 - B1 ** STEP
C2 = 1.0 - B2 ** STEP

VMEM_LIMIT = 56 * 1024 * 1024


def _pick(n, target, mult):
    best = None
    for t in range(mult, min(n, target) + 1, mult):
        if n % t == 0:
            best = t
    return best if best is not None else n


def _cp(sem):
    return pltpu.CompilerParams(dimension_semantics=sem, vmem_limit_bytes=VMEM_LIMIT)


def _mm(a, b, *, ta=False, tb=False, out_dtype=F32, add=None, ride=(), post=None, name, tm=1024, tn=1024, tk=1024):
    m, k = (a.shape[1], a.shape[0]) if ta else a.shape
    n = b.shape[0] if tb else b.shape[1]
    assert k == (b.shape[1] if tb else b.shape[0])
    tm, tn, tk = _pick(m, tm, 128), _pick(n, tn, 128), _pick(k, tk, 128)
    grid = (m // tm, n // tn, k // tk)
    nk = grid[2]
    dn = (((0 if ta else 1,), (1 if tb else 0,)), ((), ()))
    n_in = 2 + (add is not None)
    kind, riding = ride if ride else (None, ())
    nr = len(riding)
    post_fn, post_ins, post_outs = post if post is not None else (None, (), ())
    n_pi = len(post_ins)
    n_out = len(post_outs) if post is not None else 1
    assert post is None or grid[1] == 1

    def body(*refs):
        a_ref, b_ref = refs[0], refs[1]
        pin = refs[n_in:n_in + n_pi]
        base = n_in + n_pi + nr
        o_refs = refs[base:base + n_out]
        acc_ref = refs[base + n_out + nr]
        step = [pl.program_id(d) for d in range(3)]
        kk = step[2]
        if nr:
            riders = (refs[n_in + n_pi:base], refs[base + n_out:base + n_out + nr], *refs[base + n_out + nr + 1:])

            @pl.when((step[0] == 0) & (step[1] == 0) & (kk == 0))
            def _():
                _start_all(_exchange_copies(kind, *riders))

        prod = lax.dot_general(a_ref[...].astype(BF16), b_ref[...].astype(BF16), dn, preferred_element_type=F32)

        def finish(r):
            if add is not None:
                r = r + refs[2][...]
            if post_fn is None:
                o_refs[0][...] = r.astype(out_dtype)
            else:
                post_fn(r, step[0], pin, o_refs)

        if nk == 1:
            finish(prod)
        else:
            @pl.when(kk == 0)
            def _():
                acc_ref[...] = prod

            @pl.when((kk > 0) & (kk < nk - 1))
            def _():
                acc_ref[...] += prod

            @pl.when(kk == nk - 1)
            def _():
                finish(acc_ref[...] + prod)

        if nr:
            @pl.when((step[0] == grid[0] - 1) & (step[1] == grid[1] - 1) & (kk == nk - 1))
            def _():
                _wait_all(_exchange_copies(kind, *riders))

    a_spec = pl.BlockSpec((tk, tm), lambda i, j, kk: (kk, i)) if ta else pl.BlockSpec((tm, tk), lambda i, j, kk: (i, kk))
    b_spec = pl.BlockSpec((tn, tk), lambda i, j, kk: (j, kk)) if tb else pl.BlockSpec((tk, tn), lambda i, j, kk: (kk, j))
    def row_spec(shape):
        return pl.BlockSpec((tm if shape[0] == m else shape[0], shape[1]),
                            (lambda i, j, kk: (i, 0)) if shape[0] == m else (lambda i, j, kk: (0, 0)))

    in_specs = [a_spec, b_spec]
    args = [a, b]
    if add is not None:
        in_specs.append(pl.BlockSpec((tm, tn), lambda i, j, kk: (i, j)))
        args.append(add)
    if post is None:
        main_specs = [pl.BlockSpec((tm, tn), lambda i, j, kk: (i, j))]
        main_shapes = [jax.ShapeDtypeStruct((m, n), out_dtype)]
    else:
        main_specs = [row_spec(s.shape) for s in post_outs]
        main_shapes = list(post_outs)
    outs = pl.pallas_call(
        body, name=name, grid=grid,
        in_specs=in_specs + [row_spec(p.shape) for p in post_ins] + [ANY] * nr,
        out_specs=main_specs + [ANY] * nr,
        out_shape=main_shapes + (_exchange_shapes(kind, riding) if nr else []),
        scratch_shapes=[pltpu.VMEM((tm, tn) if nk > 1 else (8, 128), F32)] + (_exchange_sems(kind, nr) if nr else []),
        compiler_params=_cp(("arbitrary",) * 3 if (nr or post is not None) else ("parallel", "parallel", "arbitrary")),
    )(*args, *post_ins, *riding)
    main = outs[0] if post is None else list(outs[:n_out])
    return (main, list(outs[n_out:])) if nr else main


TL = 512


def _rms(xv, gv):
    return xv * lax.rsqrt(jnp.mean(xv * xv, axis=-1, keepdims=True) + EPS) * gv


def _rows(width):
    return pl.BlockSpec((TL, width), lambda i: (i, 0))


def _whole(shape):
    return pl.BlockSpec(shape, lambda i: (0,) * len(shape))


def _in_proj(x, g, w_in_t, tabs):
    qkw = AW + 2 * KVW

    def body(x_ref, g_ref, w_ref, c_ref, sa_ref, sb_ref, h_ref, qkv_ref, u_ref):
        h = _rms(x_ref[...], g_ref[...]).astype(BF16)
        h_ref[...] = h
        proj = lax.dot_general(h, w_ref[...], _NT, preferred_element_type=F32)
        for j in range(qkw // 128):
            cols = slice(j * 128, (j + 1) * 128)
            xv = proj[:, cols]
            if j < (AW + KVW) // 128:
                xv = _rope(xv, c_ref[...], sa_ref[...], sb_ref[...], 1.0)
            qkv_ref[:, cols] = xv.astype(BF16)
        u_ref[...] = proj[:, qkw:]

    return pl.pallas_call(
        body, name="in_proj", grid=(L // TL,),
        in_specs=[_rows(D), _whole((1, D)), _whole((INW, D)), _rows(128), _rows(128), _rows(128)],
        out_specs=[_rows(D), _rows(qkw), _rows(SW)],
        out_shape=[jax.ShapeDtypeStruct((L, D), BF16), jax.ShapeDtypeStruct((L, qkw), BF16),
                   jax.ShapeDtypeStruct((L, SW), F32)],
        compiler_params=_cp(("parallel",)),
    )(x, g, w_in_t, *tabs)


def _in_proj_dx(dproj, w_in_t, x, g, dres):
    def body(dp_ref, w_ref, x_ref, g_ref, dres_ref, dx_ref, dg_ref):
        dh = jnp.dot(dp_ref[...], w_ref[...], preferred_element_type=F32)
        dx, dg = _rms_bwd_tile(x_ref[...], g_ref[...], dh)
        dx_ref[...] = dx + dres_ref[...]

        @pl.when(pl.program_id(0) == 0)
        def _():
            dg_ref[...] = jnp.zeros_like(dg_ref)

        dg_ref[...] += dg

    return pl.pallas_call(
        body, name="in_proj_dx", grid=(L // TL,),
        in_specs=[_rows(INW), _whole((INW, D)), _rows(D), _whole((1, D)), _rows(D)],
        out_specs=[_rows(D), _whole((1, D))],
        out_shape=[jax.ShapeDtypeStruct((L, D), F32), jax.ShapeDtypeStruct((1, D), F32)],
        compiler_params=_cp(("arbitrary",)),
    )(dproj, w_in_t, x, g, dres)


def _rms_bwd_tile(xv, gv, dh):
    r = lax.rsqrt(jnp.mean(xv * xv, axis=-1, keepdims=True) + EPS)
    a = dh * gv
    dx = r * a - xv * (r * r * r) * jnp.mean(a * xv, axis=-1, keepdims=True)
    dg = jnp.sum(dh * xv * r, axis=0, keepdims=True)
    return dx, dg


def _rms_bwd_post(dh, i, ins, outs):
    x_ref, dres_ref, g_ref = ins
    dx_ref, dxb_ref, dg_ref = outs
    dx, dg = _rms_bwd_tile(x_ref[...], g_ref[...], dh)
    dx = dx + dres_ref[...]
    dx_ref[...] = dx
    dxb_ref[...] = dx.astype(BF16)

    @pl.when(i == 0)
    def _():
        dg_ref[...] = jnp.zeros_like(dg_ref)

    dg_ref[...] += dg


def _final_post(xv, i, ins, outs):
    t_ref, g_ref = ins
    loss_ref, dx_ref, dxb_ref, dg_ref = outs

    @pl.when(i == 0)
    def _():
        loss_ref[...] = jnp.zeros_like(loss_ref)
        dg_ref[...] = jnp.zeros_like(dg_ref)

    gv = g_ref[...]
    r = lax.rsqrt(jnp.mean(xv * xv, axis=-1, keepdims=True) + EPS)
    e = xv * r * gv - t_ref[...]
    loss_ref[...] += 0.5 * jnp.sum(jnp.mean(e * e, axis=-1, keepdims=True), axis=0, keepdims=True)
    dy = e * (1.0 / D)
    a = dy * gv
    dx = r * a - xv * (r * r * r) * jnp.mean(a * xv, axis=-1, keepdims=True)
    dx_ref[...] = dx
    dxb_ref[...] = dx.astype(BF16)
    dg_ref[...] += jnp.sum(dy * xv * r, axis=0, keepdims=True)


def _out_proj(attn, ysg, ga, gs, w_out, x, gf):
    def body(a_ref, s_ref, ga_ref, gs_ref, w_ref, x_ref, gf_ref, m_ref, x1_ref, h2_ref):
        m_ref[:, 0:AW] = _rms(a_ref[...], ga_ref[...]).astype(BF16)
        m_ref[:, AW:AW + SW] = _rms(s_ref[...], gs_ref[...]).astype(BF16)
        x1 = jnp.dot(m_ref[...], w_ref[...], preferred_element_type=F32) + x_ref[...]
        x1_ref[...] = x1
        h2_ref[...] = _rms(x1, gf_ref[...]).astype(BF16)

    return pl.pallas_call(
        body, name="out_proj", grid=(L // TL,),
        in_specs=[_rows(AW), _rows(SW), _whole((1, AW)), _whole((1, SW)), _whole((D, D)), _rows(D), _whole((1, D))],
        out_specs=[_rows(D), _rows(D), _rows(D)],
        out_shape=[jax.ShapeDtypeStruct((L, D), BF16), jax.ShapeDtypeStruct((L, D), F32),
                   jax.ShapeDtypeStruct((L, D), BF16)],
        compiler_params=_cp(("parallel",)),
    )(attn, ysg, ga, gs, w_out, x, gf)


def _out_proj_dx(dx1, w_out, attn, ysg, ga, gs):
    def body(dx_ref, w_ref, a_ref, s_ref, ga_ref, gs_ref, da_ref, ds_ref, dga_ref, dgs_ref):
        @pl.when(pl.program_id(0) == 0)
        def _():
            dga_ref[...] = jnp.zeros_like(dga_ref)
            dgs_ref[...] = jnp.zeros_like(dgs_ref)

        dm = lax.dot_general(dx_ref[...].astype(BF16), w_ref[...], _NT, preferred_element_type=F32)
        dxa, dga = _rms_bwd_tile(a_ref[...], ga_ref[...], dm[:, 0:AW])
        da_ref[...] = dxa
        dga_ref[...] += dga
        dxs, dgs = _rms_bwd_tile(s_ref[...], gs_ref[...], dm[:, AW:AW + SW])
        ds_ref[...] = dxs
        dgs_ref[...] += dgs

    return pl.pallas_call(
        body, name="out_proj_dx", grid=(L // TL,),
        in_specs=[_rows(D), _whole((D, D)), _rows(AW), _rows(SW), _whole((1, AW)), _whole((1, SW))],
        out_specs=[_rows(AW), _rows(SW), _whole((1, AW)), _whole((1, SW))],
        out_shape=[jax.ShapeDtypeStruct((L, AW), F32), jax.ShapeDtypeStruct((L, SW), F32),
                   jax.ShapeDtypeStruct((1, AW), F32), jax.ShapeDtypeStruct((1, SW), F32)],
        compiler_params=_cp(("arbitrary",)),
    )(dx1, w_out, attn, ysg, ga, gs)


def _rope_tables():
    half = HD // 8
    inv_freq = jnp.power(ROPE_THETA, -jnp.arange(half, dtype=F32) / half)
    ang = jnp.arange(L, dtype=F32)[:, None] * inv_freq[None, :]
    cos, sin = jnp.cos(ang), jnp.sin(ang)
    one = jnp.ones((L, HD - 2 * half), F32)
    zero = jnp.zeros((L, HD - 2 * half), F32)
    zh = jnp.zeros((L, half), F32)
    cos64 = jnp.concatenate([cos, cos, one], axis=1)
    sa64 = jnp.concatenate([-sin, zh, zero], axis=1)
    sb64 = jnp.concatenate([zh, sin, zero], axis=1)
    return [jnp.tile(t, (1, 2)) for t in (cos64, sa64, sb64)]


def _rope(xv, cosv, sav, sbv, sign):
    return xv * cosv + sign * (pltpu.roll(xv, 120, 1) * sav + pltpu.roll(xv, 8, 1) * sbv)


def _rope_bwd(dq, dk, dv, du, tabs):
    def body(dq_ref, dk_ref, dv_ref, du_ref, c_ref, sa_ref, sb_ref, o_ref):
        for j in range(AW // 128):
            cols = slice(j * 128, (j + 1) * 128)
            o_ref[:, cols] = _rope(dq_ref[:, cols], c_ref[...], sa_ref[...], sb_ref[...], -1.0).astype(BF16)
        o_ref[:, AW:AW + KVW] = _rope(dk_ref[...], c_ref[...], sa_ref[...], sb_ref[...], -1.0).astype(BF16)
        o_ref[:, AW + KVW:AW + 2 * KVW] = dv_ref[...].astype(BF16)
        o_ref[:, AW + 2 * KVW:] = du_ref[...].astype(BF16)

    def row(width):
        return pl.BlockSpec((TL, width), lambda i: (i, 0))

    return pl.pallas_call(
        body, name="rope_bwd", grid=(L // TL,),
        in_specs=[row(AW), row(KVW), row(KVW), row(SW), row(128), row(128), row(128)],
        out_specs=row(INW), out_shape=jax.ShapeDtypeStruct((L, INW), BF16),
        compiler_params=_cp(("parallel",)),
    )(dq, dk, dv, du, *tabs)


def _attn_window(n):
    start = pl.multiple_of(jnp.clip((n - 1) * BLK, 0, L - WIN), BLK)
    qpos = n * BLK + lax.broadcasted_iota(jnp.int32, (BLK, WIN), 0)
    kpos = start + lax.broadcasted_iota(jnp.int32, (BLK, WIN), 1)
    return start, jnp.abs(kpos - qpos) <= BLK


_NT = (((1,), (1,)), ((), ()))
_TN = (((0,), (0,)), ((), ()))
NEG = -1e30


def _attn_fwd(qkv, sink, gather=()):
    ng = len(gather)

    def body(sink_ref, q_ref, k_ref, v_ref, *rest):
        o_ref, lse_ref = rest[ng], rest[ng + 1]
        s_scr, p_scr = rest[2 * ng + 2], rest[2 * ng + 3]
        n = pl.program_id(0)
        if ng:
            travellers = (rest[:ng], rest[ng + 2:2 * ng + 2], *rest[2 * ng + 4:])
            _gather_under(n, L // BLK, travellers)

        start, valid = _attn_window(n)
        kw = k_ref[pl.ds(start, WIN), :]
        vw = v_ref[pl.ds(start, WIN), :]
        for h in range(NQ):
            kv = h // (NQ // NKV)
            s_scr[h] = lax.dot_general(q_ref[:, h * HD:(h + 1) * HD], kw[:, kv * HD:(kv + 1) * HD], _NT,
                                       preferred_element_type=F32)
        for h in range(NQ):
            s = jnp.where(valid, s_scr[h] * (HD ** -0.5), NEG)
            sk = sink_ref[h]
            m = jnp.maximum(jnp.max(s, axis=-1, keepdims=True), sk)
            p = jnp.exp(s - m)
            den = jnp.sum(p, axis=-1, keepdims=True) + jnp.exp(sk - m)
            p_scr[h] = (p / den).astype(BF16)
            lse_ref[:, h:h + 1] = m + jnp.log(den)
        for h in range(NQ):
            kv = h // (NQ // NKV)
            o_ref[:, h * HD:(h + 1) * HD] = jnp.dot(p_scr[h], vw[:, kv * HD:(kv + 1) * HD],
                                                    preferred_element_type=F32)
        if ng:
            _gather_done(n, L // BLK, travellers)

    outs = pl.pallas_call(
        body, name="attn_fwd", grid=(L // BLK,),
        in_specs=[pl.BlockSpec(memory_space=pltpu.SMEM),
                  pl.BlockSpec((BLK, AW), lambda n: (n, 0)),
                  pl.BlockSpec((L, KVW), lambda n: (0, AW // KVW)),
                  pl.BlockSpec((L, KVW), lambda n: (0, AW // KVW + 1))] + [ANY] * ng,
        out_specs=[pl.BlockSpec((BLK, AW), lambda n: (n, 0)), pl.BlockSpec((BLK, NQ), lambda n: (n, 0))] + [ANY] * ng,
        out_shape=[jax.ShapeDtypeStruct((L, AW), F32), jax.ShapeDtypeStruct((L, NQ), F32)] + _gather_shapes(gather),
        scratch_shapes=[pltpu.VMEM((NQ, BLK, WIN), F32), pltpu.VMEM((NQ, BLK, WIN), BF16)]
        + (_gather_sems(ng) if ng else []),
        compiler_params=_cp(("arbitrary",) if ng else ("parallel",)),
    )(sink, qkv, qkv, qkv, *gather)
    return outs[0], outs[1], list(outs[2:])


def _attn_bwd(qkv, sink, attn, lse, dattn, ride=()):
    kind, riding = ride if ride else (None, ())
    nr = len(riding)

    def body(*refs):
        if nr:
            riders = (refs[7:7 + nr], refs[11 + nr:11 + 2 * nr], *refs[15 + 2 * nr:])

            @pl.when(pl.program_id(0) == 0)
            def _():
                _start_all(_exchange_copies(kind, *riders))

        compute(*refs[:7], *refs[7 + nr:11 + nr], *refs[11 + 2 * nr:15 + 2 * nr])
        if nr:
            @pl.when(pl.program_id(0) == L // BLK - 1)
            def _():
                _wait_all(_exchange_copies(kind, *riders))

    def compute(sink_ref, q_ref, k_ref, v_ref, o_ref, lse_ref, do_ref, dq_ref, dk_ref, dv_ref, dsink_ref,
                s_scr, dp_scr, p_scr, ds_scr):
        n = pl.program_id(0)

        @pl.when(n == 0)
        def _():
            dk_ref[...] = jnp.zeros_like(dk_ref)
            dv_ref[...] = jnp.zeros_like(dv_ref)
            dsink_ref[...] = jnp.zeros_like(dsink_ref)

        start, valid = _attn_window(n)
        kw = k_ref[pl.ds(start, WIN), :]
        vw = v_ref[pl.ds(start, WIN), :]
        group = NQ // NKV
        for h in range(NQ):
            kv = h // group
            s_scr[h] = lax.dot_general(q_ref[:, h * HD:(h + 1) * HD], kw[:, kv * HD:(kv + 1) * HD], _NT,
                                       preferred_element_type=F32)
            dp_scr[h] = lax.dot_general(do_ref[:, h * HD:(h + 1) * HD].astype(BF16), vw[:, kv * HD:(kv + 1) * HD],
                                        _NT, preferred_element_type=F32)
        for h in range(NQ):
            dd = jnp.sum(do_ref[:, h * HD:(h + 1) * HD] * o_ref[:, h * HD:(h + 1) * HD], axis=-1, keepdims=True)
            lse_h = lse_ref[:, h:h + 1]
            p = jnp.where(valid, jnp.exp(s_scr[h] * (HD ** -0.5) - lse_h), 0.0)
            p_scr[h] = p.astype(BF16)
            ds_scr[h] = (p * (dp_scr[h] - dd) * (HD ** -0.5)).astype(BF16)
            dsk = -jnp.sum(jnp.exp(sink_ref[h] - lse_h) * dd, axis=0, keepdims=True)
            dsink_ref[h:h + 1, :] += jnp.broadcast_to(dsk, (1, 128))
        for kv in range(NKV):
            kh = kw[:, kv * HD:(kv + 1) * HD]
            dk_acc = jnp.zeros((WIN, HD), F32)
            dv_acc = jnp.zeros((WIN, HD), F32)
            for h in range(kv * group, (kv + 1) * group):
                dq_ref[:, h * HD:(h + 1) * HD] = jnp.dot(ds_scr[h], kh, preferred_element_type=F32)
                dk_acc += lax.dot_general(ds_scr[h], q_ref[:, h * HD:(h + 1) * HD], _TN, preferred_element_type=F32)
                dv_acc += lax.dot_general(p_scr[h], do_ref[:, h * HD:(h + 1) * HD].astype(BF16), _TN,
                                          preferred_element_type=F32)
            dk_ref[pl.ds(start, WIN), kv * HD:(kv + 1) * HD] += dk_acc
            dv_ref[pl.ds(start, WIN), kv * HD:(kv + 1) * HD] += dv_acc

    qblk = pl.BlockSpec((BLK, AW), lambda n: (n, 0))
    full = pl.BlockSpec((L, KVW), lambda n: (0, 0))
    outs = pl.pallas_call(
        body, name="attn_bwd", grid=(L // BLK,),
        in_specs=[pl.BlockSpec(memory_space=pltpu.SMEM), qblk,
                  pl.BlockSpec((L, KVW), lambda n: (0, AW // KVW)),
                  pl.BlockSpec((L, KVW), lambda n: (0, AW // KVW + 1)),
                  qblk, pl.BlockSpec((BLK, NQ), lambda n: (n, 0)), qblk] + [ANY] * nr,
        out_specs=[qblk, full, full, pl.BlockSpec((NQ, 128), lambda n: (0, 0))] + [ANY] * nr,
        out_shape=[jax.ShapeDtypeStruct((L, AW), F32), jax.ShapeDtypeStruct((L, KVW), F32),
                   jax.ShapeDtypeStruct((L, KVW), F32), jax.ShapeDtypeStruct((NQ, 128), F32)]
        + (_exchange_shapes(kind, riding) if nr else []),
        scratch_shapes=[pltpu.VMEM((NQ, BLK, WIN), F32), pltpu.VMEM((NQ, BLK, WIN), F32),
                        pltpu.VMEM((NQ, BLK, WIN), BF16), pltpu.VMEM((NQ, BLK, WIN), BF16)]
        + (_exchange_sems(kind, nr) if nr else []),
        compiler_params=_cp(("arbitrary",)),
    )(sink, qkv, qkv, qkv, attn, lse, dattn, *riding)
    return list(outs[:4]), list(outs[4:])


def _perm(a):
    return a.reshape(NSEG, TSEG, a.shape[1]).transpose(1, 0, 2).reshape(L, a.shape[1])


def _unperm(a):
    return a.reshape(TSEG, NSEG, a.shape[1]).transpose(1, 0, 2).reshape(L, a.shape[1])


def _cmul(ar, ai, br, bi):
    return ar * br - ai * bi, ar * bi + ai * br


def _scan_inplace(s_ref, lr, li, rev, visit=None, carried=(), out_ref=None):
    n = lr.shape[1]
    lr8 = jnp.broadcast_to(lr, (NSEG, n))
    li8 = jnp.broadcast_to(li, (NSEG, n))

    def rows(k):
        return pl.ds(pl.multiple_of(jnp.where(rev, TSEG - 1 - k, k) * NSEG, NSEG), NSEG)

    def step(k, c, store):
        sr, si = c
        rs = rows(k)
        pr, pi = _cmul(lr8, li8, sr, si)
        nr = pr + s_ref[rs, 0:n]
        ni = pi + s_ref[rs, n:2 * n]
        if store:
            dst = s_ref if out_ref is None else out_ref
            dst[rs, 0:n] = nr.astype(dst.dtype)
            dst[rs, n:2 * n] = ni.astype(dst.dtype)
        return nr, ni

    z = jnp.zeros((NSEG, n), F32)
    er, ei = lax.fori_loop(0, TSEG, functools.partial(step, store=False), (z, z))
    pr, pi = lr, li
    for _ in range(int(math.log2(TSEG))):
        pr, pi = _cmul(pr, pi, pr, pi)

    seg = lax.broadcasted_iota(jnp.int32, (NSEG, n), 0)

    def moved(val, k):
        down = jnp.where(seg >= k, pltpu.roll(val, k, 0), 0.0)
        up = jnp.where(seg < NSEG - k, pltpu.roll(val, NSEG - k, 0), 0.0)
        return jnp.where(rev, up, down)

    k = 1
    while k < NSEG:
        mr, mi = _cmul(pr, pi, moved(er, k), moved(ei, k))
        er, ei = er + mr, ei + mi
        pr, pi = _cmul(pr, pi, pr, pi)
        k *= 2
    cin_r, cin_i = moved(er, 1), moved(ei, 1)
    if visit is None:
        lax.fori_loop(0, TSEG, functools.partial(step, store=True), (cin_r, cin_i))
        return cin_r, cin_i

    def visited(k, c):
        nr, ni = step(k, c[:2], True)
        return (nr, ni) + tuple(visit(k, nr, ni, c[2:]))

    fin = lax.fori_loop(0, TSEG - 1, visited, (cin_r, cin_i) + tuple(carried))
    last_r, last_i = step(TSEG - 1, fin[:2], True)
    return last_r, last_i, fin[2:]


S5_RC = 512


def _s5_specs():
    u_spec = pl.BlockSpec((L, 128), lambda cb, h, d: (0, cb))
    b_spec = pl.BlockSpec((None, None, 128, 2 * SBW), lambda cb, h, d: (d, cb * 2 + h, 0, 0))
    c_spec = pl.BlockSpec((None, None, 2 * SBW, 128), lambda cb, h, d: (d, cb * 2 + h, 0, 0))
    l_spec = pl.BlockSpec((None, None, 1, SBW), lambda cb, h, d: (d, cb * 2 + h, 0, 0))
    d_spec = pl.BlockSpec((1, 128), lambda cb, h, d: (0, cb))
    return u_spec, b_spec, c_spec, l_spec, d_spec


def _s5_fwd(u_p, bcat, ccat, lam_re, lam_im, dskip, gather=()):
    ng = len(gather)
    grid = (SW // 128, 2, 2)

    def body(*refs):
        if ng:
            step = (pl.program_id(0) * grid[1] + pl.program_id(1)) * grid[2] + pl.program_id(2)
            travellers = (refs[6:6 + ng], refs[8 + ng:8 + 2 * ng], *refs[9 + 2 * ng:])
            _gather_under(step, math.prod(grid), travellers)
        compute(*refs[:6], *refs[6 + ng:8 + ng], refs[8 + 2 * ng])
        if ng:
            _gather_done(step, math.prod(grid), travellers)

    def compute(u_ref, b_ref, c_ref, lr_ref, li_ref, d_ref, y_ref, sb_ref, s_scr):
        first = (pl.program_id(1) == 0) & (pl.program_id(2) == 0)

        def proj(i, _):
            rs = pl.ds(pl.multiple_of(i * S5_RC, S5_RC), S5_RC)
            s_scr[rs, :] = jnp.dot(u_ref[rs, :].astype(BF16), b_ref[...], preferred_element_type=F32)
            return 0

        lax.fori_loop(0, L // S5_RC, proj, 0)
        _scan_inplace(s_scr, lr_ref[...], li_ref[...], pl.program_id(2) == 1, out_ref=sb_ref)

        def out(i, _, opening):
            rs = pl.ds(pl.multiple_of(i * S5_RC, S5_RC), S5_RC)
            yv = jnp.dot(sb_ref[rs, :], c_ref[...], preferred_element_type=F32)
            if opening:
                y_ref[rs, :] = d_ref[...] * u_ref[rs, :] + yv
            else:
                y_ref[rs, :] += yv
            return 0

        @pl.when(first)
        def _():
            lax.fori_loop(0, L // S5_RC, functools.partial(out, opening=True), 0)

        @pl.when(jnp.logical_not(first))
        def _():
            lax.fori_loop(0, L // S5_RC, functools.partial(out, opening=False), 0)

    u_spec, b_spec, c_spec, l_spec, d_spec = _s5_specs()
    outs = pl.pallas_call(
        body, name="s5_fwd", grid=grid,
        in_specs=[u_spec, b_spec, c_spec, l_spec, l_spec, d_spec] + [ANY] * ng,
        out_specs=[u_spec, _s5_state_spec()] + [ANY] * ng,
        out_shape=[jax.ShapeDtypeStruct((L, SW), F32), jax.ShapeDtypeStruct((2, NSB, L, 2 * SBW), BF16)]
        + _gather_shapes(gather),
        scratch_shapes=[pltpu.VMEM((L, 2 * SBW), F32)] + (_gather_sems(ng) if ng else []),
        compiler_params=_cp(("arbitrary",) * 3 if ng else ("parallel", "arbitrary", "arbitrary")),
    )(u_p, bcat, ccat, lam_re, lam_im, dskip, *gather)
    return outs[0], outs[1], list(outs[2:])


def _s5_state_spec():
    return pl.BlockSpec((None, None, L, 2 * SBW), lambda cb, h, d: (d, cb * 2 + h, 0, 0))


def _s5_bwd(u_p, dy_p, states, bcat, ccat, lam_re, lam_im, dskip, ride=()):
    kind, riding = ride if ride else (None, ())
    nr = len(riding)
    grid = (SW // 128, 2, 2)

    def body(*refs):
        work = refs[:8] + refs[8 + nr:14 + nr] + refs[14 + 2 * nr:16 + 2 * nr]
        if nr:
            riders = (refs[8:8 + nr], refs[14 + nr:14 + 2 * nr], *refs[16 + 2 * nr:])
            step = [pl.program_id(d) for d in range(3)]

            @pl.when((step[0] == 0) & (step[1] == 0) & (step[2] == 0))
            def _():
                _start_all(_exchange_copies(kind, *riders))

        compute(*work)
        if nr:
            @pl.when((step[0] == grid[0] - 1) & (step[1] == grid[1] - 1) & (step[2] == grid[2] - 1))
            def _():
                _wait_all(_exchange_copies(kind, *riders))

    def compute(u_ref, dy_ref, s_ref, b_ref, c_ref, lr_ref, li_ref, d_ref,
                du_ref, db_ref, dc_ref, dlr_ref, dli_ref, dd_ref, g_scr, gb_scr):
        first = (pl.program_id(1) == 0) & (pl.program_id(2) == 0)
        rev = pl.program_id(2) == 1

        def dstate(i, _):
            rs = pl.ds(pl.multiple_of(i * S5_RC, S5_RC), S5_RC)
            g_scr[rs, :] = lax.dot_general(dy_ref[rs, :].astype(BF16), c_ref[...], _NT, preferred_element_type=F32)
            return 0

        lax.fori_loop(0, L // S5_RC, dstate, 0)

        def before(rows):
            sv = s_ref[rows, :].astype(F32)
            return sv[:, 0:SBW], sv[:, SBW:2 * SBW]

        def dlam(gr, gi, sr, si, ar, ai):
            return ar + gr * sr + gi * si, ai + gi * sr - gr * si

        db_ref[0:NSEG, :] = jnp.zeros((NSEG, 2 * SBW), F32)

        def visit(k, gr, gi, _):
            ts = jnp.where(rev, k + 1, TSEG - 2 - k)
            sr, si = before(pl.ds(pl.multiple_of(ts * NSEG, NSEG), NSEG))
            ar, ai = dlam(gr, gi, sr, si, db_ref[0:NSEG, 0:SBW], db_ref[0:NSEG, SBW:2 * SBW])
            db_ref[0:NSEG, 0:SBW] = ar
            db_ref[0:NSEG, SBW:2 * SBW] = ai
            return ()

        gr, gi, _ = _scan_inplace(g_scr, lr_ref[...], -li_ref[...], jnp.logical_not(rev), visit, (), gb_scr)
        acc = (db_ref[0:NSEG, 0:SBW], db_ref[0:NSEG, SBW:2 * SBW])
        edge_r, edge_i = before(pl.ds(pl.multiple_of(jnp.where(rev, 0, TSEG - 1) * NSEG, NSEG), NSEG))
        seg = lax.broadcasted_iota(jnp.int32, (NSEG, SBW), 0)
        keep = seg != jnp.where(rev, NSEG - 1, 0)

        def neighbour(e):
            return jnp.where(keep, jnp.where(rev, pltpu.roll(e, NSEG - 1, 0), pltpu.roll(e, 1, 0)), 0.0)

        ar, ai = dlam(gr, gi, neighbour(edge_r), neighbour(edge_i), *acc)
        dlr_ref[...] = jnp.sum(ar, axis=0, keepdims=True)
        dli_ref[...] = jnp.sum(ai, axis=0, keepdims=True)

        db_ref[...] = jnp.zeros_like(db_ref)
        dc_ref[...] = jnp.zeros_like(dc_ref)

        @pl.when(first)
        def _():
            dd_ref[...] = jnp.zeros_like(dd_ref)

        def grads(i, _, opening):
            rs = pl.ds(pl.multiple_of(i * S5_RC, S5_RC), S5_RC)
            uv = u_ref[rs, :]
            dyv = dy_ref[rs, :]
            gb = gb_scr[rs, :]
            db_ref[...] += lax.dot_general(uv.astype(BF16), gb, _TN, preferred_element_type=F32)
            dc_ref[...] += lax.dot_general(dyv.astype(BF16), s_ref[rs, :], _TN, preferred_element_type=F32)
            duv = lax.dot_general(gb, b_ref[...], _NT, preferred_element_type=F32)
            if opening:
                du_ref[rs, :] = d_ref[...] * dyv + duv
                dd_ref[...] += jnp.sum(dyv * uv, axis=0, keepdims=True)
            else:
                du_ref[rs, :] += duv
            return 0

        @pl.when(first)
        def _():
            lax.fori_loop(0, L // S5_RC, functools.partial(grads, opening=True), 0)

        @pl.when(jnp.logical_not(first))
        def _():
            lax.fori_loop(0, L // S5_RC, functools.partial(grads, opening=False), 0)

    u_spec, b_spec, c_spec, l_spec, d_spec = _s5_specs()
    outs = pl.pallas_call(
        body, name="s5_bwd", grid=grid,
        in_specs=[u_spec, u_spec, _s5_state_spec(), b_spec, c_spec, l_spec, l_spec, d_spec] + [ANY] * nr,
        out_specs=[u_spec, b_spec, b_spec, l_spec, l_spec, d_spec] + [ANY] * nr,
        out_shape=[jax.ShapeDtypeStruct((L, SW), F32),
                   jax.ShapeDtypeStruct((2, NSB, 128, 2 * SBW), F32), jax.ShapeDtypeStruct((2, NSB, 128, 2 * SBW), F32),
                   jax.ShapeDtypeStruct((2, NSB, 1, SBW), F32), jax.ShapeDtypeStruct((2, NSB, 1, SBW), F32),
                   jax.ShapeDtypeStruct((1, SW), F32)] + (_exchange_shapes(kind, riding) if nr else []),
        scratch_shapes=[pltpu.VMEM((L, 2 * SBW), F32), pltpu.VMEM((L, 2 * SBW), BF16)]
        + (_exchange_sems(kind, nr) if nr else []),
        compiler_params=_cp(("arbitrary",) * 3 if nr else ("parallel", "arbitrary", "arbitrary")),
    )(u_p, dy_p, states, bcat, ccat, lam_re, lam_im, dskip, *riding)
    return list(outs[:6]), list(outs[6:])


def _s5_params(a_re, a_im, log_step, bt_re, bt_im):
    lam = lax.complex(a_re, a_im)
    step = jnp.exp(log_step)[..., None]
    lam_bar = jnp.exp(lam * step)
    b_bar = ((lam_bar - 1.0) / lam)[..., None, :] * lax.complex(bt_re, bt_im)
    return jnp.real(lam_bar), jnp.imag(lam_bar), jnp.real(b_bar), jnp.imag(b_bar)


def _sel():
    i = jnp.arange(8)[None, :, None]
    j = jnp.arange(4)[None, None, :]
    r = jnp.arange(2)[:, None, None]
    return (i == r * 4 + j).astype(F32)


def _to_bcat(bt_re, bt_im):
    def one(bt):
        return jnp.einsum('dkrjcp,rij->dkricjp', bt.reshape(2, 4, 2, 4, GC, NP), _sel()).reshape(2, NSB, 128, SBW)
    return jnp.concatenate([one(bt_re), one(bt_im)], axis=-1)


def _from_bcat(dbcat):
    def one(dbbd):
        return jnp.einsum('dkricjp,rij->dkrjcp', dbbd.reshape(2, 4, 2, 8, GC, 4, NP), _sel()).reshape(2, NG, GC, NP)
    return one(dbcat[..., :SBW]), one(dbcat[..., SBW:])


def _to_ccat(c_re, c_im):
    def one(cc):
        return jnp.einsum('dkrjcp,rij->dkrjpic', cc.reshape(2, 4, 2, 4, GC, NP), _sel()).reshape(2, NSB, SBW, 128)
    return jnp.concatenate([one(c_re), -one(c_im)], axis=-2)


def _from_ccat(dccat):
    def one(dcbd):
        return jnp.einsum('dkrjpic,rij->dkrjcp', dcbd.reshape(2, 4, 2, 4, NP, 8, GC), _sel()).reshape(2, NG, GC, NP)
    return one(dccat[:, :, :SBW]), -one(dccat[:, :, SBW:])


def _gelu(y):
    return 0.5 * y * (1.0 + lax.erf(y * (2.0 ** -0.5)))


def _gelu_grad(y):
    return 0.5 * (1.0 + lax.erf(y * (2.0 ** -0.5))) + y * jnp.exp(-0.5 * y * y) * ((2.0 * math.pi) ** -0.5)


def _sigmoid(z):
    return 0.5 * jnp.tanh(0.5 * z) + 0.5


def _glu_fwd(y, wg):
    def body(y_ref, w_ref, o_ref, z_ref):
        ys = _gelu(y_ref[...])
        z = jnp.dot(ys.astype(BF16), w_ref[...], preferred_element_type=F32)
        z_ref[...] = z
        o_ref[...] = ys * _sigmoid(z)

    row = pl.BlockSpec((TL, SW), lambda i: (i, 0))
    return pl.pallas_call(
        body, name="glu_fwd", grid=(L // TL,),
        in_specs=[row, pl.BlockSpec((SW, SW), lambda i: (0, 0))], out_specs=[row, row],
        out_shape=[jax.ShapeDtypeStruct((L, SW), F32), jax.ShapeDtypeStruct((L, SW), F32)],
        compiler_params=_cp(("parallel",)),
    )(y, wg)


def _glu_bwd(y, z, dout, wg):
    def body(y_ref, z_ref, do_ref, w_ref, dy_ref, dw_ref):
        @pl.when(pl.program_id(0) == 0)
        def _():
            dw_ref[...] = jnp.zeros_like(dw_ref)

        yv = y_ref[...]
        ys = _gelu(yv)
        sg = _sigmoid(z_ref[...])
        dov = do_ref[...]
        dz = (dov * ys * sg * (1.0 - sg)).astype(BF16)
        dys = dov * sg + lax.dot_general(dz, w_ref[...], _NT, preferred_element_type=F32)
        dy_ref[...] = dys * _gelu_grad(yv)
        dw_ref[...] += lax.dot_general(ys.astype(BF16), dz, _TN, preferred_element_type=F32)

    row = pl.BlockSpec((TL, SW), lambda i: (i, 0))
    wsp = pl.BlockSpec((SW, SW), lambda i: (0, 0))
    return pl.pallas_call(
        body, name="glu_bwd", grid=(L // TL,),
        in_specs=[row, row, row, wsp], out_specs=[row, wsp],
        out_shape=[jax.ShapeDtypeStruct((L, SW), F32), jax.ShapeDtypeStruct((SW, SW), F32)],
        compiler_params=_cp(("arbitrary",)),
    )(y, z, dout, wg)


CT = 256
CR = 128
NCT = DFF // CT


def _shifted(ref, r):
    h = 8 * (4 // ref.dtype.itemsize)
    cur = ref[pl.ds(r, CR), :].astype(F32)
    before = ref[pl.ds(pl.multiple_of(jnp.maximum(r - h, 0), h), h), :][h - 1:h, :].astype(F32)
    after = ref[pl.ds(pl.multiple_of(jnp.minimum(r + CR, L - h), h), h), :][0:1, :].astype(F32)
    before = jnp.where(r > 0, before, 0.0)
    after = jnp.where(r + CR < L, after, 0.0)
    row = lax.broadcasted_iota(jnp.int32, cur.shape, 0)
    prev = jnp.where(row == 0, before, pltpu.roll(cur, 1, 0))
    nxt = jnp.where(row == CR - 1, after, pltpu.roll(cur, CR - 1, 0))
    return prev, cur, nxt


def _conv3(ref, r, w_ref, b_ref):
    prev, cur, nxt = _shifted(ref, r)
    return w_ref[0:1, :] * prev + w_ref[1:2, :] * cur + w_ref[2:3, :] * nxt + b_ref[...]


def _convact_fwd(up, conv_w, conv_b):
    def body(ug_ref, uv_ref, wg_ref, wv_ref, bg_ref, bv_ref, o_ref, g_ref, v_ref):
        def chunk(i, _):
            r = pl.multiple_of(i * CR, CR)
            rs = pl.ds(r, CR)
            g = _conv3(ug_ref, r, wg_ref, bg_ref)
            v = _conv3(uv_ref, r, wv_ref, bv_ref)
            o_ref[rs, :] = (g * _sigmoid(g) * v).astype(BF16)
            g_ref[rs, :] = g.astype(BF16)
            v_ref[rs, :] = v.astype(BF16)
            return 0

        lax.fori_loop(0, L // CR, chunk, 0)

    gcol = pl.BlockSpec((L, CT), lambda j: (0, j))
    vcol = pl.BlockSpec((L, CT), lambda j: (0, j + NCT))
    return pl.pallas_call(
        body, name="convact_fwd", grid=(NCT,),
        in_specs=[gcol, vcol,
                  pl.BlockSpec((3, CT), lambda j: (0, j)), pl.BlockSpec((3, CT), lambda j: (0, j + NCT)),
                  pl.BlockSpec((1, CT), lambda j: (0, j)), pl.BlockSpec((1, CT), lambda j: (0, j + NCT))],
        out_specs=[gcol, gcol, gcol], out_shape=[jax.ShapeDtypeStruct((L, DFF), BF16)] * 3,
        compiler_params=_cp(("parallel",)),
    )(up, up, conv_w, conv_w, conv_b, conv_b)


def _convact_bwd(up, gq, vq, dact, conv_w):
    def body(ug_ref, uv_ref, g_ref, v_ref, da_ref, wg_ref, wv_ref, du_ref, dw_ref, db_ref, dgs, dvs, dbv):
        half = pl.program_id(1)

        def transpose_conv(src, u_ref, w_ref):
            dw_ref[...] = jnp.zeros_like(dw_ref)

            def chunk(i, _):
                r = pl.multiple_of(i * CR, CR)
                rs = pl.ds(r, CR)
                prev, cur, nxt = _shifted(src, r)
                du_ref[rs, :] = (w_ref[0:1, :] * nxt + w_ref[1:2, :] * cur + w_ref[2:3, :] * prev).astype(BF16)
                uv = u_ref[rs, :].astype(F32)
                for k, d in enumerate((nxt, cur, prev)):
                    dw_ref[k:k + 1, :] += jnp.sum(d * uv, axis=0, keepdims=True)
                return 0

            lax.fori_loop(0, L // CR, chunk, 0)

        @pl.when(half == 0)
        def _():
            db_ref[...] = jnp.zeros_like(db_ref)
            dbv[...] = jnp.zeros_like(dbv)

            def chunk1(i, _):
                rs = pl.ds(pl.multiple_of(i * CR, CR), CR)
                g = g_ref[rs, :].astype(F32)
                v = v_ref[rs, :].astype(F32)
                sg = _sigmoid(g)
                da = da_ref[rs, :].astype(F32)
                dv = da * g * sg
                dg = da * v * sg * (1.0 + g * (1.0 - sg))
                dgs[rs, :] = dg
                dvs[rs, :] = dv
                db_ref[...] += jnp.sum(dg, axis=0, keepdims=True)
                dbv[0:1, :] += jnp.sum(dv, axis=0, keepdims=True)
                return 0

            lax.fori_loop(0, L // CR, chunk1, 0)
            transpose_conv(dgs, ug_ref, wg_ref)

        @pl.when(half == 1)
        def _():
            db_ref[...] = dbv[0:1, :]
            transpose_conv(dvs, uv_ref, wv_ref)

    def col(rows, off):
        return pl.BlockSpec((rows, CT), lambda j, h: (0, j + off))

    def out(rows):
        return pl.BlockSpec((rows, CT), lambda j, h: (0, j + h * NCT))

    return pl.pallas_call(
        body, name="convact_bwd", grid=(NCT, 2),
        in_specs=[col(L, 0), col(L, NCT), col(L, 0), col(L, 0), col(L, 0), col(3, 0), col(3, NCT)],
        out_specs=[out(L), out(3), out(1)],
        out_shape=[jax.ShapeDtypeStruct((L, 2 * DFF), BF16), jax.ShapeDtypeStruct((3, 2 * DFF), F32),
                   jax.ShapeDtypeStruct((1, 2 * DFF), F32)],
        scratch_shapes=[pltpu.VMEM((L, CT), F32), pltpu.VMEM((L, CT), F32), pltpu.VMEM((8, CT), F32)],
        compiler_params=_cp(("parallel", "arbitrary")),
    )(up, up, gq, vq, dact, conv_w, conv_w)


def _local_step(x, tgt, w_in_t, p, attend, scan, stage):
    tabs = _rope_tables()
    lam_re, lam_im, bb_re, bb_im = _s5_params(p["a_re"], p["a_im"], p["log_step"], p["bt_re"], p["bt_im"])
    bcat = _to_bcat(bb_re, bb_im).astype(BF16)
    ccat = _to_ccat(p["c_re"], p["c_im"]).astype(BF16)
    lam_re4, lam_im4 = lam_re.reshape(2, NSB, 1, SBW), lam_im.reshape(2, NSB, 1, SBW)
    dskip = p["d_skip"].reshape(1, SW)
    g_mix, g_ffn, g_fin = p["norm_mix_g"].reshape(1, D), p["norm_ffn_g"].reshape(1, D), p["norm_final_g"].reshape(1, D)
    g_attn, g_ssm = p["norm_attn_g"].reshape(1, AW), p["norm_ssm_g"].reshape(1, SW)
    sink = p["sink"].reshape(NQ)
    conv_b = p["conv_b"].reshape(1, 2 * DFF)

    rows, gain = jax.ShapeDtypeStruct((L, D), F32), jax.ShapeDtypeStruct((1, D), F32)
    rows16 = jax.ShapeDtypeStruct((L, D), BF16)
    h1, qkv, u = _in_proj(x, g_mix, w_in_t, tabs)
    attn, lse, wts = attend(qkv, sink)
    u_p = _perm(u)
    y_p, states, more = scan(u_p, bcat, ccat, lam_re4, lam_im4, dskip)
    wts = dict(wts, **more)
    w_glu, w_out, w_up_t, w_down, conv_w = (wts[k] for k in ("w_glu", "w_out", "w_up_t", "w_down", "conv_w"))
    ysg_p, z_p = _glu_fwd(y_p, w_glu)
    ysg = _unperm(ysg_p)
    mixed, x1, h2 = _out_proj(attn, ysg, g_attn, g_ssm, w_out, x, g_ffn)
    up = _mm(h2, w_up_t, tb=True, name="ffn_up", tn=1408, out_dtype=BF16)
    act, gq, vq = _convact_fwd(up, conv_w, conv_b)
    loss, dx2, dx2b, dg_fin = _mm(
        act, w_down, add=x1, name="ffn_down", tm=512, tk=DFF,
        post=(_final_post, [tgt, g_fin], [jax.ShapeDtypeStruct((1, 1), F32), rows, rows16, gain]))

    def riding(res, ride):
        return res if ride else (res, None)

    dw_down = _mm(act, dx2b, ta=True, name="ffn_down_dw", tm=256, tk=L)
    ride = stage(("w_down",), "cores", [dw_down])
    dact, got = riding(_mm(dx2b, w_down, tb=True, name="ffn_down_dx", tn=1408, out_dtype=BF16, ride=ride), ride)
    ride = stage(("w_down",), "chips", got)
    dup, dconv_w, dconv_b = _convact_bwd(up, gq, vq, dact, conv_w)
    dw_up_t, got = riding(_mm(dup, h2, ta=True, name="ffn_up_dw", tm=512, tk=L, ride=ride), ride)
    stage(("w_down",), "done", got)
    ride = stage(("w_up_t",), "cores", [dw_up_t])
    (dx1, dx1b, dg_ffn), got = riding(
        _mm(dup, w_up_t, name="ffn_up_dx", tm=512, tk=2 * DFF, ride=ride,
            post=(_rms_bwd_post, [x1, dx2, g_ffn], [rows, rows16, gain])), ride)
    ride = stage(("w_up_t",), "chips", got)
    dattn, dysg, dg_attn, dg_ssm = _out_proj_dx(dx1b, w_out, attn, ysg, g_attn, g_ssm)
    dw_out = _mm(mixed, dx1b, ta=True, name="out_proj_dw", tm=512, tk=L)
    dy_p, dw_glu = _glu_bwd(y_p, z_p, _perm(dysg), w_glu)
    (du_p, dbcat, dccat, dlam_re, dlam_im, dd), got = _s5_bwd(u_p, dy_p, states, bcat, ccat, lam_re4, lam_im4, dskip,
                                                              ride=ride)
    stage(("w_up_t",), "done", got)
    dbb_re, dbb_im = _from_bcat(dbcat)
    dc_re, dc_im = _from_ccat(_swap(dccat))
    mix = ("w_out", "w_glu")
    ride = stage(mix, "cores", [dw_out, dw_glu])
    (dq, dk, dv, dsink), got = _attn_bwd(qkv, sink, attn, lse, dattn, ride=ride)
    ride = stage(mix, "chips", got)
    dproj = _rope_bwd(dq, dk, dv, _unperm(du_p), tabs)
    dw_in_t, got = riding(_mm(dproj, h1, ta=True, name="in_proj_dw", tm=640, tk=L, ride=ride), ride)
    stage(mix, "done", got)
    grad_x, dg_mix = _in_proj_dx(dproj, w_in_t, x, g_mix, dx1)

    big = dict(w_in_t=dw_in_t)
    small = dict(norm_mix_g=dg_mix, norm_attn_g=dg_attn, norm_ssm_g=dg_ssm, norm_ffn_g=dg_ffn, norm_final_g=dg_fin,
                 sink=dsink[:, 0], conv_b=dconv_b, d_skip=dd, conv_w=dconv_w,
                 lam_re=dlam_re.reshape(2, NG, NP), lam_im=dlam_im.reshape(2, NG, NP),
                 bb_re=dbb_re, bb_im=dbb_im, c_re=dc_re, c_im=dc_im, loss=loss.reshape(1))
    return grad_x, big, small


ANY = pl.BlockSpec(memory_space=pl.ANY)


def _coords():
    return lax.axis_index("x"), lax.axis_index("y"), lax.axis_index("c")


def _flip(v, b):
    return v + b - 2 * v * b if b else v


def _all_gather(shards, name):
    n = len(shards)

    def body(*refs):
        _gather_start(refs[:n], refs[n:2 * n], *refs[2 * n:])
        _gather_finish(refs[:n], refs[n:2 * n], *refs[2 * n:])

    return pl.pallas_call(
        body, name=name,
        in_specs=[ANY] * n, out_specs=[ANY] * n,
        out_shape=_gather_shapes(shards), scratch_shapes=_gather_sems(n),
    )(*shards)


def _gather_shapes(shards):
    return [jax.ShapeDtypeStruct((NDEV * s.shape[0], s.shape[1]), s.dtype) for s in shards]


def _gather_sems(n):
    return [pltpu.SemaphoreType.DMA((7 * n,)), pltpu.SemaphoreType.DMA((7 * n,)), pltpu.SemaphoreType.DMA((n,))]


def _gather_copies(ins, outs, send_sems, recv_sems, local_sems, a):
    x, y, c = _coords()
    me, sibling = (x, y, c), (x, y, 1 - c)
    chips = [(1 - x, y), (x, 1 - y), (1 - x, 1 - y)]
    r = ins[a].shape[0]

    def rows(px, py, pc):
        return outs[a].at[pl.ds(pl.multiple_of((4 * px + 2 * py + pc) * r, 8), r), :]

    def copy(k, block, to, src=None):
        return pltpu.make_async_remote_copy(
            src_ref=rows(*block) if src is None else src, dst_ref=rows(*block),
            send_sem=send_sems.at[a * 7 + k], recv_sem=recv_sems.at[a * 7 + k],
            device_id=to, device_id_type=pl.DeviceIdType.MESH)

    mine = pltpu.make_async_copy(ins[a], rows(*me), local_sems.at[a])
    first = [copy(0, me, sibling, src=ins[a])]
    first += [copy(1 + j, me, (*chip, c), src=ins[a]) for j, chip in enumerate(chips)]
    passed = [copy(4 + j, (*chip, c), sibling) for j, chip in enumerate(chips)]
    arrivals = [copy(1 + j, (*chip, c), me) for j, chip in enumerate(chips)]
    from_sibling = [copy(0, sibling, me)] + [copy(4 + j, (*chip, 1 - c), me) for j, chip in enumerate(chips)]
    return mine, first, passed, arrivals, from_sibling


def _gather_start(ins, outs, send_sems, recv_sems, local_sems):
    for a in range(len(ins)):
        mine, first, _, _, _ = _gather_copies(ins, outs, send_sems, recv_sems, local_sems, a)
        mine.start()
        for cp in first:
            cp.start()


def _gather_forward(ins, outs, send_sems, recv_sems, local_sems):
    for a in range(len(ins)):
        _, _, passed, arrivals, _ = _gather_copies(ins, outs, send_sems, recv_sems, local_sems, a)
        for arrived, onward in zip(arrivals, passed):
            arrived.wait_recv()
            onward.start()


def _gather_wait(ins, outs, send_sems, recv_sems, local_sems):
    for a in range(len(ins)):
        mine, first, passed, _, from_sibling = _gather_copies(ins, outs, send_sems, recv_sems, local_sems, a)
        for cp in from_sibling:
            cp.wait_recv()
        for cp in first + passed:
            cp.wait_send()
        mine.wait()


def _gather_finish(*refs):
    _gather_forward(*refs)
    _gather_wait(*refs)


def _gather_under(step, steps, travellers):
    @pl.when(step == 0)
    def _():
        _gather_start(*travellers)

    @pl.when(step == (3 * steps) // 4)
    def _():
        _gather_forward(*travellers)


def _gather_done(step, steps, travellers):
    @pl.when(step == steps - 1)
    def _():
        _gather_wait(*travellers)


NCHIP = 4
CHIP_FLIPS = ((1, 0), (0, 1), (1, 1))


def _planned_copies(ins, outs, send_sems, recv_sems, plan):
    return [pltpu.make_async_remote_copy(
        src_ref=ins[a].at[src], dst_ref=outs[a].at[dst], send_sem=send_sems.at[k], recv_sem=recv_sems.at[k],
        device_id=to, device_id_type=pl.DeviceIdType.MESH) for k, (a, src, dst, to) in enumerate(plan)]


def _start_all(copies):
    for cp in copies:
        cp.start()


def _wait_all(copies):
    for cp in copies:
        cp.wait_recv()
    for cp in copies:
        cp.wait_send()


SLOTS = {"cores": NCHIP, "chips": 3}


def _exchange_copies(kind, ins, outs, send_sems, recv_sems):
    x, y, c = _coords()
    plan = []
    for a in range(len(ins)):
        if kind == "cores":
            plan += [(a, 2 * q + 1 - c, q, (x, y, 1 - c)) for q in range(NCHIP)]
        else:
            for j, (fx, fy) in enumerate(CHIP_FLIPS):
                px, py = _flip(x, fx), _flip(y, fy)
                plan.append((a, 2 * px + py, j, (px, py, c)))
    return _planned_copies(ins, outs, send_sems, recv_sems, plan)


def _exchange_shapes(kind, parts):
    return [jax.ShapeDtypeStruct((SLOTS[kind],) + s.shape[1:], s.dtype) for s in parts]


def _exchange_sems(kind, n):
    return [pltpu.SemaphoreType.DMA((SLOTS[kind] * n,)), pltpu.SemaphoreType.DMA((SLOTS[kind] * n,))]


def _exchange(kind, parts, name):
    n = len(parts)

    def body(*refs):
        copies = _exchange_copies(kind, refs[:n], refs[n:2 * n], *refs[2 * n:])
        _start_all(copies)
        _wait_all(copies)

    return pl.pallas_call(
        body, name=name, in_specs=[ANY] * n, out_specs=[ANY] * n,
        out_shape=_exchange_shapes(kind, parts), scratch_shapes=_exchange_sems(kind, n),
    )(*parts)


def _pair_sum(where, part, recv, wire_dtype, name):
    _, r, c = part.shape
    tr = _pick(r, 256, 16)

    def body(w_ref, p_ref, r_ref, pb_ref, own_ref):
        s = p_ref[...] + r_ref[...]
        pb_ref[...] = s.astype(wire_dtype)

        @pl.when(pl.program_id(1) == w_ref[1])
        def _():
            own_ref[...] = s

    return pl.pallas_call(
        body, name=name,
        grid_spec=pltpu.PrefetchScalarGridSpec(
            num_scalar_prefetch=1, grid=(r // tr, NCHIP),
            in_specs=[pl.BlockSpec((None, tr, c), lambda i, q, w: (2 * q + w[0], i, 0)),
                      pl.BlockSpec((None, tr, c), lambda i, q, w: (q, i, 0))],
            out_specs=[pl.BlockSpec((None, tr, c), lambda i, q, w: (q, i, 0)),
                       pl.BlockSpec((tr, c), lambda i, q, w: (i, 0))]),
        out_shape=[jax.ShapeDtypeStruct((NCHIP, r, c), wire_dtype), jax.ShapeDtypeStruct((r, c), F32)],
        compiler_params=_cp(("parallel", "arbitrary")),
    )(where, part, recv)


def _chip_sum(own, recv, name):
    r, c = own.shape
    tr = _pick(r, 256, 16)

    def body(o_ref, r_ref, out_ref):
        acc = o_ref[...]
        for j in range(3):
            acc = acc + r_ref[j].astype(F32)
        out_ref[...] = acc

    return pl.pallas_call(
        body, name=name, grid=(r // tr,),
        in_specs=[pl.BlockSpec((tr, c), lambda i: (i, 0)), pl.BlockSpec((3, tr, c), lambda i: (0, i, 0))],
        out_specs=pl.BlockSpec((tr, c), lambda i: (i, 0)),
        out_shape=jax.ShapeDtypeStruct((r, c), F32),
        compiler_params=_cp(("parallel",)),
    )(own, recv)


def _adamw(w, own, recv, m, v, name):
    r, c = w.shape
    tr = _pick(r, 256, 16)

    def body(w_ref, o_ref, r_ref, m_ref, v_ref, g_ref, d_ref, nm_ref, nv_ref):
        acc = o_ref[...]
        for j in range(3):
            acc = acc + r_ref[j].astype(F32)
        g_ref[...] = acc
        _adamw_refs(w_ref, g_ref, m_ref, v_ref, d_ref, nm_ref, nv_ref)

    blk = pl.BlockSpec((tr, c), lambda i: (i, 0))
    return pl.pallas_call(
        body, name=name, grid=(r // tr,),
        in_specs=[blk, blk, pl.BlockSpec((3, tr, c), lambda i: (0, i, 0)), blk, blk], out_specs=[blk] * 4,
        out_shape=[jax.ShapeDtypeStruct((r, c), F32)] * 4,
        compiler_params=_cp(("parallel",)),
    )(w, own, recv, m, v)


def _adamw_refs(w_ref, g_ref, m_ref, v_ref, d_ref, nm_ref, nv_ref):
    gv = g_ref[...]
    nm = B1 * m_ref[...] + (1.0 - B1) * gv
    nv = B2 * v_ref[...] + (1.0 - B2) * (gv * gv)
    nm_ref[...] = nm
    nv_ref[...] = nv
    d_ref[...] = -LR * ((nm / C1) / (jnp.sqrt(nv / C2) + AEPS) + WD * w_ref[...])


def _adamw_small(ws, gs, ms, vs, name):
    n = len(ws)

    def body(*refs):
        groups = [refs[i * n:(i + 1) * n] for i in range(7)]
        for per_param in zip(*groups):
            _adamw_refs(*per_param)

    vm = pl.BlockSpec(memory_space=pltpu.VMEM)
    outs = pl.pallas_call(
        body, name=name, in_specs=[vm] * (4 * n), out_specs=[vm] * (3 * n),
        out_shape=[jax.ShapeDtypeStruct(a.shape, F32) for a in ws] * 3,
    )(*ws, *gs, *ms, *vs)
    return outs[:n], outs[n:2 * n], outs[2 * n:]


def _swap(a):
    return jnp.swapaxes(a, -1, -2)


VIEWS = {
    "w_in": (lambda a: a[0].T, lambda u: u.T[None]),
    "w_up": (lambda a: a[0].T, lambda u: u.T[None]),
    "w_glu": (lambda a: a[0], lambda u: u[None]),
    "w_out": (lambda a: a[0], lambda u: u[None]),
    "w_down": (lambda a: a[0], lambda u: u[None]),
    "conv_w": (lambda a: a[0], lambda u: u[None]),
    "norm_mix_g": (lambda a: a, lambda u: u),
    "norm_attn_g": (lambda a: a, lambda u: u),
    "norm_ssm_g": (lambda a: a, lambda u: u),
    "norm_ffn_g": (lambda a: a, lambda u: u),
    "norm_final_g": (lambda a: a[None], lambda u: u[0]),
    "conv_b": (lambda a: a, lambda u: u),
    "sink": (lambda a: a, lambda u: u),
    "a_re": (lambda a: a.reshape(2 * NG, NP), lambda u: u.reshape(1, 2, NG, NP)),
    "a_im": (lambda a: a.reshape(2 * NG, NP), lambda u: u.reshape(1, 2, NG, NP)),
    "log_step": (lambda a: a[0], lambda u: u[None]),
    "b_re": (lambda a: _swap(a[0]).reshape(2 * NG * GC, NP), lambda u: _swap(u.reshape(2, NG, GC, NP))[None]),
    "b_im": (lambda a: _swap(a[0]).reshape(2 * NG * GC, NP), lambda u: _swap(u.reshape(2, NG, GC, NP))[None]),
    "c_re": (lambda a: a.reshape(2 * NG * GC, NP), lambda u: u.reshape(1, 2, NG, GC, NP)),
    "c_im": (lambda a: a.reshape(2 * NG * GC, NP), lambda u: u.reshape(1, 2, NG, GC, NP)),
    "d_skip": (lambda a: a[0].T, lambda u: u.T[None]),
}
BIG = ["w_in", "w_glu", "w_out", "w_up", "w_down"]
PACK_W = 1024


def _pack(arrs, rows):
    flat = jnp.concatenate([a.reshape(-1).astype(F32) for a in arrs])
    return jnp.pad(flat, (0, rows * PACK_W - flat.shape[0])).reshape(rows, PACK_W)


def _unpack(packed, shapes):
    flat = packed.reshape(-1)
    out, off = [], 0
    for s in shapes:
        size = math.prod(s)
        out.append(flat[off:off + size].reshape(s))
        off += size
    return out


def kernel(x, norm_mix_g, w_in, a_re, a_im, log_step, b_re, b_im, c_re, c_im, d_skip, w_glu, sink, norm_attn_g, norm_ssm_g, w_out, norm_ffn_g, w_up, conv_w, conv_b, w_down, norm_final_g, loss_target, m_norm_mix_g, m_w_in, m_a_re, m_a_im, m_log_step, m_b_re, m_b_im, m_c_re, m_c_im, m_d_skip, m_w_glu, m_sink, m_norm_attn_g, m_norm_ssm_g, m_w_out, m_norm_ffn_g, m_w_up, m_conv_w, m_conv_b, m_w_down, m_norm_final_g, v_norm_mix_g, v_w_in, v_a_re, v_a_im, v_log_step, v_b_re, v_b_im, v_c_re, v_c_im, v_d_skip, v_w_glu, v_sink, v_norm_attn_g, v_norm_ssm_g, v_w_out, v_norm_ffn_g, v_w_up, v_conv_w, v_conv_b, v_w_down, v_norm_final_g):
    args = dict(locals())
    names = ["norm_mix_g", "w_in", "a_re", "a_im", "log_step", "b_re", "b_im", "c_re", "c_im", "d_skip", "w_glu",
             "sink", "norm_attn_g", "norm_ssm_g", "w_out", "norm_ffn_g", "w_up", "conv_w", "conv_b", "w_down",
             "norm_final_g"]
    w = {k: args[k] for k in names}
    m = {k: args["m_" + k] for k in names}
    v = {k: args["v_" + k] for k in names}

    (w_in_t,) = _all_gather([w_in[0].T.astype(BF16)], "gather_w_in")
    under_attn = dict(w_glu=w_glu[0].astype(BF16), w_out=w_out[0].astype(BF16),
                      conv_w=jnp.pad(conv_w[0], ((0, 5), (0, 0))))
    under_scan = dict(w_up_t=w_up[0].T.astype(BF16), w_down=w_down[0].astype(BF16))

    ax, ay, ac = _coords()
    me = 4 * ax + 2 * ay + ac
    where = jnp.stack([ac, 2 * ax + ay]).astype(jnp.int32)
    parts, own, got = {}, {}, {}

    def split8(g):
        return g.reshape(NDEV, g.shape[0] // NDEV, g.shape[1])

    def attend(qkv, sink_):
        attn, lse, gathered = _attn_fwd(qkv, sink_, gather=list(under_attn.values()))
        wts = dict(zip(under_attn.keys(), gathered))
        wts["conv_w"] = (wts["conv_w"].reshape(NDEV, 8, 2 * DFF // NDEV)[:, :3].transpose(1, 0, 2)
                         .reshape(3, 2 * DFF))
        return attn, lse, wts

    def scan(*operands):
        y_p, states, gathered = _s5_fwd(*operands, gather=list(under_scan.values()))
        return y_p, states, dict(zip(under_scan.keys(), gathered))

    def pair_sum(k, from_core):
        per_chip, own[k] = _pair_sum(where, parts[k], from_core, F32 if k == "small" else BF16, "pair_sum_" + k)
        return per_chip

    def stage(ks, phase, payload):
        if phase == "cores":
            parts.update({k: split8(g) for k, g in zip(ks, payload)})
            return ("cores", [parts[k] for k in ks])
        if phase == "chips":
            return ("chips", [pair_sum(k, fc) for k, fc in zip(ks, payload)])
        got.update(zip(ks, payload))
        return ()

    p = {k: w[k][0] for k in ("norm_mix_g", "a_re", "a_im", "log_step", "c_re", "c_im", "d_skip", "sink",
                              "norm_attn_g", "norm_ssm_g", "norm_ffn_g", "conv_b")}
    p["norm_final_g"] = norm_final_g
    p["bt_re"], p["bt_im"] = _swap(b_re[0]), _swap(b_im[0])
    grad_x, big, small = _local_step(x[0], loss_target[0], w_in_t, p, attend, scan, stage)

    small_names = list(small.keys())
    small_shapes = [small[k].shape for k in small_names]
    n_small = sum(math.prod(s) for s in small_shapes)
    rows_dev = -(-n_small // (PACK_W * NDEV * 16)) * 16
    spack = _pack([small[k] for k in small_names], rows_dev * NDEV)
    late = ["w_in_t", "small"]
    parts.update(w_in_t=split8(big["w_in_t"]), small=spack.reshape(NDEV, rows_dev, PACK_W))
    from_cores = _exchange("cores", [parts[k] for k in late], "exchange_cores")
    from_chips = _exchange("chips", [pair_sum(k, fc) for k, fc in zip(late, from_cores)], "exchange_chips")
    got.update(zip(late, from_chips))
    (small_full,) = _all_gather([_chip_sum(own["small"], got["small"], "chip_sum_small")], "gather_small")
    sm = dict(zip(small_names, _unpack(small_full, small_shapes)))

    _, s5_vjp = jax.vjp(_s5_params, a_re[0], a_im[0], log_step[0], p["bt_re"], p["bt_im"])
    da_re, da_im, dlog_step, dbt_re, dbt_im = s5_vjp((sm["lam_re"], sm["lam_im"], sm["bb_re"], sm["bb_im"]))
    gview = {
        "norm_mix_g": sm["norm_mix_g"], "norm_attn_g": sm["norm_attn_g"], "norm_ssm_g": sm["norm_ssm_g"],
        "norm_ffn_g": sm["norm_ffn_g"], "norm_final_g": sm["norm_final_g"], "conv_b": sm["conv_b"],
        "sink": sm["sink"][None], "a_re": da_re.reshape(2 * NG, NP), "a_im": da_im.reshape(2 * NG, NP),
        "log_step": dlog_step, "b_re": dbt_re.reshape(2 * NG * GC, NP), "b_im": dbt_im.reshape(2 * NG * GC, NP),
        "c_re": sm["c_re"].reshape(2 * NG * GC, NP), "c_im": sm["c_im"].reshape(2 * NG * GC, NP),
        "d_skip": sm["d_skip"].reshape(NG, GC).T,
        "conv_w": lax.dynamic_slice_in_dim(sm["conv_w"], me * (2 * DFF // NDEV), 2 * DFF // NDEV, axis=1),
    }

    dview, mview, vview = {}, {}, {}
    for k, kg in zip(BIG, ("w_in_t", "w_glu", "w_out", "w_up_t", "w_down")):
        to = VIEWS[k][0]
        gview[k], dview[k], mview[k], vview[k] = _adamw(to(w[k]), own[kg], got[kg], to(m[k]), to(v[k]), "adamw_" + k)
    rest = [k for k in names if k not in BIG]
    outs = _adamw_small([VIEWS[k][0](w[k]) for k in rest], [gview[k] for k in rest],
                        [VIEWS[k][0](m[k]) for k in rest], [VIEWS[k][0](v[k]) for k in rest], "adamw_small")
    for dst, vals in zip((dview, mview, vview), outs):
        dst.update(dict(zip(rest, vals)))

    def back(views):
        return [VIEWS[k][1](views[k]) for k in names]

    return (sm["loss"][0], grad_x[None], *back(gview), *back(dview), *back(mview), *back(vview))
```

```python
import functools
import math

import jax
import jax.numpy as jnp
from jax import lax
from jax.experimental import pallas as pl
from jax.experimental.pallas import tpu as pltpu

F32 = jnp.float32
BF16 = jnp.bfloat16

L = 4096
D = 1024
NQ, NKV, HD = 8, 2, 64
AW = NQ * HD
KVW = NKV * HD
SW = 512
NG, GC, NP = 32, 16, 64
INW = AW + 2 * KVW + SW
DFF = 2816
BLK = 128
WIN = 3 * BLK
EPS = 1e-6
ROPE_THETA = 500000.0
NSEG = 32
TSEG = L // NSEG
SBW = 256
NSB = NG * NP // SBW
NDEV = 8
MESH_AXES = ("x", "y", "c")

LR, B1, B2, AEPS, WD, STEP = 0.001, 0.9, 0.999, 1e-08, 0.01, 10
C1 = 1.0 - B1 ** STEP
C2 = 1.0 - B2 ** STEP

VMEM_LIMIT = 56 * 1024 * 1024


def _pick(n, target, mult):
    best = None
    for t in range(mult, min(n, target) + 1, mult):
        if n % t == 0:
            best = t
    return best if best is not None else n


def _cp(sem):
    return pltpu.CompilerParams(dimension_semantics=sem, vmem_limit_bytes=VMEM_LIMIT)


def _mm(a, b, *, ta=False, tb=False, out_dtype=F32, add=None, ride=(), post=None, name, tm=1024, tn=1024, tk=1024):
    m, k = (a.shape[1], a.shape[0]) if ta else a.shape
    n = b.shape[0] if tb else b.shape[1]
    assert k == (b.shape[1] if tb else b.shape[0])
    tm, tn, tk = _pick(m, tm, 128), _pick(n, tn, 128), _pick(k, tk, 128)
    grid = (m // tm, n // tn, k // tk)
    nk = grid[2]
    dn = (((0 if ta else 1,), (1 if tb else 0,)), ((), ()))
    n_in = 2 + (add is not None)
    kind, riding = ride if ride else (None, ())
    nr = len(riding)
    post_fn, post_ins, post_outs = post if post is not None else (None, (), ())
    n_pi = len(post_ins)
    n_out = len(post_outs) if post is not None else 1
    assert post is None or grid[1] == 1

    def body(*refs):
        a_ref, b_ref = refs[0], refs[1]
        pin = refs[n_in:n_in + n_pi]
        base = n_in + n_pi + nr
        o_refs = refs[base:base + n_out]
        acc_ref = refs[base + n_out + nr]
        step = [pl.program_id(d) for d in range(3)]
        kk = step[2]
        if nr:
            riders = (refs[n_in + n_pi:base], refs[base + n_out:base + n_out + nr], *refs[base + n_out + nr + 1:])

            @pl.when((step[0] == 0) & (step[1] == 0) & (kk == 0))
            def _():
                _start_all(_exchange_copies(kind, *riders))

        prod = lax.dot_general(a_ref[...].astype(BF16), b_ref[...].astype(BF16), dn, preferred_element_type=F32)

        def finish(r):
            if add is not None:
                r = r + refs[2][...]
            if post_fn is None:
                o_refs[0][...] = r.astype(out_dtype)
            else:
                post_fn(r, step[0], pin, o_refs)

        if nk == 1:
            finish(prod)
        else:
            @pl.when(kk == 0)
            def _():
                acc_ref[...] = prod

            @pl.when((kk > 0) & (kk < nk - 1))
            def _():
                acc_ref[...] += prod

            @pl.when(kk == nk - 1)
            def _():
                finish(acc_ref[...] + prod)

        if nr:
            @pl.when((step[0] == grid[0] - 1) & (step[1] == grid[1] - 1) & (kk == nk - 1))
            def _():
                _wait_all(_exchange_copies(kind, *riders))

    a_spec = pl.BlockSpec((tk, tm), lambda i, j, kk: (kk, i)) if ta else pl.BlockSpec((tm, tk), lambda i, j, kk: (i, kk))
    b_spec = pl.BlockSpec((tn, tk), lambda i, j, kk: (j, kk)) if tb else pl.BlockSpec((tk, tn), lambda i, j, kk: (kk, j))
    def row_spec(shape):
        return pl.BlockSpec((tm if shape[0] == m else shape[0], shape[1]),
                            (lambda i, j, kk: (i, 0)) if shape[0] == m else (lambda i, j, kk: (0, 0)))

    in_specs = [a_spec, b_spec]
    args = [a, b]
    if add is not None:
        in_specs.append(pl.BlockSpec((tm, tn), lambda i, j, kk: (i, j)))
        args.append(add)
    if post is None:
        main_specs = [pl.BlockSpec((tm, tn), lambda i, j, kk: (i, j))]
        main_shapes = [jax.ShapeDtypeStruct((m, n), out_dtype)]
    else:
        main_specs = [row_spec(s.shape) for s in post_outs]
        main_shapes = list(post_outs)
    outs = pl.pallas_call(
        body, name=name, grid=grid,
        in_specs=in_specs + [row_spec(p.shape) for p in post_ins] + [ANY] * nr,
        out_specs=main_specs + [ANY] * nr,
        out_shape=main_shapes + (_exchange_shapes(kind, riding) if nr else []),
        scratch_shapes=[pltpu.VMEM((tm, tn) if nk > 1 else (8, 128), F32)] + (_exchange_sems(kind, nr) if nr else []),
        compiler_params=_cp(("arbitrary",) * 3 if (nr or post is not None) else ("parallel", "parallel", "arbitrary")),
    )(*args, *post_ins, *riding)
    main = outs[0] if post is None else list(outs[:n_out])
    return (main, list(outs[n_out:])) if nr else main


TL = 512


def _rms(xv, gv):
    return xv * lax.rsqrt(jnp.mean(xv * xv, axis=-1, keepdims=True) + EPS) * gv


def _rows(width):
    return pl.BlockSpec((TL, width), lambda i: (i, 0))


def _whole(shape):
    return pl.BlockSpec(shape, lambda i: (0,) * len(shape))


def _in_proj(x, g, w_in_t, tabs):
    qkw = AW + 2 * KVW

    def body(x_ref, g_ref, w_ref, c_ref, sa_ref, sb_ref, h_ref, qkv_ref, u_ref):
        h = _rms(x_ref[...], g_ref[...]).astype(BF16)
        h_ref[...] = h
        proj = lax.dot_general(h, w_ref[...], _NT, preferred_element_type=F32)
        for j in range(qkw // 128):
            cols = slice(j * 128, (j + 1) * 128)
            xv = proj[:, cols]
            if j < (AW + KVW) // 128:
                xv = _rope(xv, c_ref[...], sa_ref[...], sb_ref[...], 1.0)
            qkv_ref[:, cols] = xv.astype(BF16)
        u_ref[...] = proj[:, qkw:]

    return pl.pallas_call(
        body, name="in_proj", grid=(L // TL,),
        in_specs=[_rows(D), _whole((1, D)), _whole((INW, D)), _rows(128), _rows(128), _rows(128)],
        out_specs=[_rows(D), _rows(qkw), _rows(SW)],
        out_shape=[jax.ShapeDtypeStruct((L, D), BF16), jax.ShapeDtypeStruct((L, qkw), BF16),
                   jax.ShapeDtypeStruct((L, SW), F32)],
        compiler_params=_cp(("parallel",)),
    )(x, g, w_in_t, *tabs)


def _in_proj_dx(dproj, w_in_t, x, g, dres):
    def body(dp_ref, w_ref, x_ref, g_ref, dres_ref, dx_ref, dg_ref):
        dh = jnp.dot(dp_ref[...], w_ref[...], preferred_element_type=F32)
        dx, dg = _rms_bwd_tile(x_ref[...], g_ref[...], dh)
        dx_ref[...] = dx + dres_ref[...]

        @pl.when(pl.program_id(0) == 0)
        def _():
            dg_ref[...] = jnp.zeros_like(dg_ref)

        dg_ref[...] += dg

    return pl.pallas_call(
        body, name="in_proj_dx", grid=(L // TL,),
        in_specs=[_rows(INW), _whole((INW, D)), _rows(D), _whole((1, D)), _rows(D)],
        out_specs=[_rows(D), _whole((1, D))],
        out_shape=[jax.ShapeDtypeStruct((L, D), F32), jax.ShapeDtypeStruct((1, D), F32)],
        compiler_params=_cp(("arbitrary",)),
    )(dproj, w_in_t, x, g, dres)


def _rms_bwd_tile(xv, gv, dh):
    r = lax.rsqrt(jnp.mean(xv * xv, axis=-1, keepdims=True) + EPS)
    a = dh * gv
    dx = r * a - xv * (r * r * r) * jnp.mean(a * xv, axis=-1, keepdims=True)
    dg = jnp.sum(dh * xv * r, axis=0, keepdims=True)
    return dx, dg


def _rms_bwd_post(dh, i, ins, outs):
    x_ref, dres_ref, g_ref = ins
    dx_ref, dxb_ref, dg_ref = outs
    dx, dg = _rms_bwd_tile(x_ref[...], g_ref[...], dh)
    dx = dx + dres_ref[...]
    dx_ref[...] = dx
    dxb_ref[...] = dx.astype(BF16)

    @pl.when(i == 0)
    def _():
        dg_ref[...] = jnp.zeros_like(dg_ref)

    dg_ref[...] += dg


def _final_post(xv, i, ins, outs):
    t_ref, g_ref = ins
    loss_ref, dx_ref, dxb_ref, dg_ref = outs

    @pl.when(i == 0)
    def _():
        loss_ref[...] = jnp.zeros_like(loss_ref)
        dg_ref[...] = jnp.zeros_like(dg_ref)

    gv = g_ref[...]
    r = lax.rsqrt(jnp.mean(xv * xv, axis=-1, keepdims=True) + EPS)
    e = xv * r * gv - t_ref[...]
    loss_ref[...] += 0.5 * jnp.sum(jnp.mean(e * e, axis=-1, keepdims=True), axis=0, keepdims=True)
    dy = e * (1.0 / D)
    a = dy * gv
    dx = r * a - xv * (r * r * r) * jnp.mean(a * xv, axis=-1, keepdims=True)
    dx_ref[...] = dx
    dxb_ref[...] = dx.astype(BF16)
    dg_ref[...] += jnp.sum(dy * xv * r, axis=0, keepdims=True)


def _out_proj(attn, ysg, ga, gs, w_out, x, gf):
    def body(a_ref, s_ref, ga_ref, gs_ref, w_ref, x_ref, gf_ref, m_ref, x1_ref, h2_ref):
        m_ref[:, 0:AW] = _rms(a_ref[...], ga_ref[...]).astype(BF16)
        m_ref[:, AW:AW + SW] = _rms(s_ref[...], gs_ref[...]).astype(BF16)
        x1 = jnp.dot(m_ref[...], w_ref[...], preferred_element_type=F32) + x_ref[...]
        x1_ref[...] = x1
        h2_ref[...] = _rms(x1, gf_ref[...]).astype(BF16)

    return pl.pallas_call(
        body, name="out_proj", grid=(L // TL,),
        in_specs=[_rows(AW), _rows(SW), _whole((1, AW)), _whole((1, SW)), _whole((D, D)), _rows(D), _whole((1, D))],
        out_specs=[_rows(D), _rows(D), _rows(D)],
        out_shape=[jax.ShapeDtypeStruct((L, D), BF16), jax.ShapeDtypeStruct((L, D), F32),
                   jax.ShapeDtypeStruct((L, D), BF16)],
        compiler_params=_cp(("parallel",)),
    )(attn, ysg, ga, gs, w_out, x, gf)


def _out_proj_dx(dx1, w_out, attn, ysg, ga, gs):
    def body(dx_ref, w_ref, a_ref, s_ref, ga_ref, gs_ref, da_ref, ds_ref, dga_ref, dgs_ref):
        @pl.when(pl.program_id(0) == 0)
        def _():
            dga_ref[...] = jnp.zeros_like(dga_ref)
            dgs_ref[...] = jnp.zeros_like(dgs_ref)

        dm = lax.dot_general(dx_ref[...].astype(BF16), w_ref[...], _NT, preferred_element_type=F32)
        dxa, dga = _rms_bwd_tile(a_ref[...], ga_ref[...], dm[:, 0:AW])
        da_ref[...] = dxa
        dga_ref[...] += dga
        dxs, dgs = _rms_bwd_tile(s_ref[...], gs_ref[...], dm[:, AW:AW + SW])
        ds_ref[...] = dxs
        dgs_ref[...] += dgs

    return pl.pallas_call(
        body, name="out_proj_dx", grid=(L // TL,),
        in_specs=[_rows(D), _whole((D, D)), _rows(AW), _rows(SW), _whole((1, AW)), _whole((1, SW))],
        out_specs=[_rows(AW), _rows(SW), _whole((1, AW)), _whole((1, SW))],
        out_shape=[jax.ShapeDtypeStruct((L, AW), F32), jax.ShapeDtypeStruct((L, SW), F32),
                   jax.ShapeDtypeStruct((1, AW), F32), jax.ShapeDtypeStruct((1, SW), F32)],
        compiler_params=_cp(("arbitrary",)),
    )(dx1, w_out, attn, ysg, ga, gs)


def _rope_tables():
    half = HD // 8
    inv_freq = jnp.power(ROPE_THETA, -jnp.arange(half, dtype=F32) / half)
    ang = jnp.arange(L, dtype=F32)[:, None] * inv_freq[None, :]
    cos, sin = jnp.cos(ang), jnp.sin(ang)
    one = jnp.ones((L, HD - 2 * half), F32)
    zero = jnp.zeros((L, HD - 2 * half), F32)
    zh = jnp.zeros((L, half), F32)
    cos64 = jnp.concatenate([cos, cos, one], axis=1)
    sa64 = jnp.concatenate([-sin, zh, zero], axis=1)
    sb64 = jnp.concatenate([zh, sin, zero], axis=1)
    return [jnp.tile(t, (1, 2)) for t in (cos64, sa64, sb64)]


def _rope(xv, cosv, sav, sbv, sign):
    return xv * cosv + sign * (pltpu.roll(xv, 120, 1) * sav + pltpu.roll(xv, 8, 1) * sbv)


def _rope_bwd(dq, dk, dv, du, tabs):
    def body(dq_ref, dk_ref, dv_ref, du_ref, c_ref, sa_ref, sb_ref, o_ref):
        for j in range(AW // 128):
            cols = slice(j * 128, (j + 1) * 128)
            o_ref[:, cols] = _rope(dq_ref[:, cols], c_ref[...], sa_ref[...], sb_ref[...], -1.0).astype(BF16)
        o_ref[:, AW:AW + KVW] = _rope(dk_ref[...], c_ref[...], sa_ref[...], sb_ref[...], -1.0).astype(BF16)
        o_ref[:, AW + KVW:AW + 2 * KVW] = dv_ref[...].astype(BF16)
        o_ref[:, AW + 2 * KVW:] = du_ref[...].astype(BF16)

    def row(width):
        return pl.BlockSpec((TL, width), lambda i: (i, 0))

    return pl.pallas_call(
        body, name="rope_bwd", grid=(L // TL,),
        in_specs=[row(AW), row(KVW), row(KVW), row(SW), row(128), row(128), row(128)],
        out_specs=row(INW), out_shape=jax.ShapeDtypeStruct((L, INW), BF16),
        compiler_params=_cp(("parallel",)),
    )(dq, dk, dv, du, *tabs)


def _attn_window(n):
    start = pl.multiple_of(jnp.clip((n - 1) * BLK, 0, L - WIN), BLK)
    qpos = n * BLK + lax.broadcasted_iota(jnp.int32, (BLK, WIN), 0)
    kpos = start + lax.broadcasted_iota(jnp.int32, (BLK, WIN), 1)
    return start, jnp.abs(kpos - qpos) <= BLK


_NT = (((1,), (1,)), ((), ()))
_TN = (((0,), (0,)), ((), ()))
NEG = -1e30


def _attn_fwd(qkv, sink, gather=()):
    ng = len(gather)

    def body(sink_ref, q_ref, k_ref, v_ref, *rest):
        o_ref, lse_ref = rest[ng], rest[ng + 1]
        s_scr, p_scr = rest[2 * ng + 2], rest[2 * ng + 3]
        n = pl.program_id(0)
        if ng:
            travellers = (rest[:ng], rest[ng + 2:2 * ng + 2], *rest[2 * ng + 4:])
            _gather_under(n, L // BLK, travellers)

        start, valid = _attn_window(n)
        kw = k_ref[pl.ds(start, WIN), :]
        vw = v_ref[pl.ds(start, WIN), :]
        for h in range(NQ):
            kv = h // (NQ // NKV)
            s_scr[h] = lax.dot_general(q_ref[:, h * HD:(h + 1) * HD], kw[:, kv * HD:(kv + 1) * HD], _NT,
                                       preferred_element_type=F32)
        for h in range(NQ):
            s = jnp.where(valid, s_scr[h] * (HD ** -0.5), NEG)
            sk = sink_ref[h]
            m = jnp.maximum(jnp.max(s, axis=-1, keepdims=True), sk)
            p = jnp.exp(s - m)
            den = jnp.sum(p, axis=-1, keepdims=True) + jnp.exp(sk - m)
            p_scr[h] = (p / den).astype(BF16)
            lse_ref[:, h:h + 1] = m + jnp.log(den)
        for h in range(NQ):
            kv = h // (NQ // NKV)
            o_ref[:, h * HD:(h + 1) * HD] = jnp.dot(p_scr[h], vw[:, kv * HD:(kv + 1) * HD],
                                                    preferred_element_type=F32)
        if ng:
            _gather_done(n, L // BLK, travellers)

    outs = pl.pallas_call(
        body, name="attn_fwd", grid=(L // BLK,),
        in_specs=[pl.BlockSpec(memory_space=pltpu.SMEM),
                  pl.BlockSpec((BLK, AW), lambda n: (n, 0)),
                  pl.BlockSpec((L, KVW), lambda n: (0, AW // KVW)),
                  pl.BlockSpec((L, KVW), lambda n: (0, AW // KVW + 1))] + [ANY] * ng,
        out_specs=[pl.BlockSpec((BLK, AW), lambda n: (n, 0)), pl.BlockSpec((BLK, NQ), lambda n: (n, 0))] + [ANY] * ng,
        out_shape=[jax.ShapeDtypeStruct((L, AW), F32), jax.ShapeDtypeStruct((L, NQ), F32)] + _gather_shapes(gather),
        scratch_shapes=[pltpu.VMEM((NQ, BLK, WIN), F32), pltpu.VMEM((NQ, BLK, WIN), BF16)]
        + (_gather_sems(ng) if ng else []),
        compiler_params=_cp(("arbitrary",) if ng else ("parallel",)),
    )(sink, qkv, qkv, qkv, *gather)
    return outs[0], outs[1], list(outs[2:])


def _attn_bwd(qkv, sink, attn, lse, dattn, ride=()):
    kind, riding = ride if ride else (None, ())
    nr = len(riding)

    def body(*refs):
        if nr:
            riders = (refs[7:7 + nr], refs[11 + nr:11 + 2 * nr], *refs[15 + 2 * nr:])

            @pl.when(pl.program_id(0) == 0)
            def _():
                _start_all(_exchange_copies(kind, *riders))

        compute(*refs[:7], *refs[7 + nr:11 + nr], *refs[11 + 2 * nr:15 + 2 * nr])
        if nr:
            @pl.when(pl.program_id(0) == L // BLK - 1)
            def _():
                _wait_all(_exchange_copies(kind, *riders))

    def compute(sink_ref, q_ref, k_ref, v_ref, o_ref, lse_ref, do_ref, dq_ref, dk_ref, dv_ref, dsink_ref,
                s_scr, dp_scr, p_scr, ds_scr):
        n = pl.program_id(0)

        @pl.when(n == 0)
        def _():
            dk_ref[...] = jnp.zeros_like(dk_ref)
            dv_ref[...] = jnp.zeros_like(dv_ref)
            dsink_ref[...] = jnp.zeros_like(dsink_ref)

        start, valid = _attn_window(n)
        kw = k_ref[pl.ds(start, WIN), :]
        vw = v_ref[pl.ds(start, WIN), :]
        group = NQ // NKV
        for h in range(NQ):
            kv = h // group
            s_scr[h] = lax.dot_general(q_ref[:, h * HD:(h + 1) * HD], kw[:, kv * HD:(kv + 1) * HD], _NT,
                                       preferred_element_type=F32)
            dp_scr[h] = lax.dot_general(do_ref[:, h * HD:(h + 1) * HD].astype(BF16), vw[:, kv * HD:(kv + 1) * HD],
                                        _NT, preferred_element_type=F32)
        for h in range(NQ):
            dd = jnp.sum(do_ref[:, h * HD:(h + 1) * HD] * o_ref[:, h * HD:(h + 1) * HD], axis=-1, keepdims=True)
            lse_h = lse_ref[:, h:h + 1]
            p = jnp.where(valid, jnp.exp(s_scr[h] * (HD ** -0.5) - lse_h), 0.0)
            p_scr[h] = p.astype(BF16)
            ds_scr[h] = (p * (dp_scr[h] - dd) * (HD ** -0.5)).astype(BF16)
            dsk = -jnp.sum(jnp.exp(sink_ref[h] - lse_h) * dd, axis=0, keepdims=True)
            dsink_ref[h:h + 1, :] += jnp.broadcast_to(dsk, (1, 128))
        for kv in range(NKV):
            kh = kw[:, kv * HD:(kv + 1) * HD]
            dk_acc = jnp.zeros((WIN, HD), F32)
            dv_acc = jnp.zeros((WIN, HD), F32)
            for h in range(kv * group, (kv + 1) * group):
                dq_ref[:, h * HD:(h + 1) * HD] = jnp.dot(ds_scr[h], kh, preferred_element_type=F32)
                dk_acc += lax.dot_general(ds_scr[h], q_ref[:, h * HD:(h + 1) * HD], _TN, preferred_element_type=F32)
                dv_acc += lax.dot_general(p_scr[h], do_ref[:, h * HD:(h + 1) * HD].astype(BF16), _TN,
                                          preferred_element_type=F32)
            dk_ref[pl.ds(start, WIN), kv * HD:(kv + 1) * HD] += dk_acc
            dv_ref[pl.ds(start, WIN), kv * HD:(kv + 1) * HD] += dv_acc

    qblk = pl.BlockSpec((BLK, AW), lambda n: (n, 0))
    full = pl.BlockSpec((L, KVW), lambda n: (0, 0))
    outs = pl.pallas_call(
        body, name="attn_bwd", grid=(L // BLK,),
        in_specs=[pl.BlockSpec(memory_space=pltpu.SMEM), qblk,
                  pl.BlockSpec((L, KVW), lambda n: (0, AW // KVW)),
                  pl.BlockSpec((L, KVW), lambda n: (0, AW // KVW + 1)),
                  qblk, pl.BlockSpec((BLK, NQ), lambda n: (n, 0)), qblk] + [ANY] * nr,
        out_specs=[qblk, full, full, pl.BlockSpec((NQ, 128), lambda n: (0, 0))] + [ANY] * nr,
        out_shape=[jax.ShapeDtypeStruct((L, AW), F32), jax.ShapeDtypeStruct((L, KVW), F32),
                   jax.ShapeDtypeStruct((L, KVW), F32), jax.ShapeDtypeStruct((NQ, 128), F32)]
        + (_exchange_shapes(kind, riding) if nr else []),
        scratch_shapes=[pltpu.VMEM((NQ, BLK, WIN), F32), pltpu.VMEM((NQ, BLK, WIN), F32),
                        pltpu.VMEM((NQ, BLK, WIN), BF16), pltpu.VMEM((NQ, BLK, WIN), BF16)]
        + (_exchange_sems(kind, nr) if nr else []),
        compiler_params=_cp(("arbitrary",)),
    )(sink, qkv, qkv, qkv, attn, lse, dattn, *riding)
    return list(outs[:4]), list(outs[4:])


def _perm(a):
    return a.reshape(NSEG, TSEG, a.shape[1]).transpose(1, 0, 2).reshape(L, a.shape[1])


def _unperm(a):
    return a.reshape(TSEG, NSEG, a.shape[1]).transpose(1, 0, 2).reshape(L, a.shape[1])


def _cmul(ar, ai, br, bi):
    return ar * br - ai * bi, ar * bi + ai * br


def _scan_inplace(s_ref, lr, li, rev, visit=None, carried=(), out_ref=None):
    n = lr.shape[1]
    lr8 = jnp.broadcast_to(lr, (NSEG, n))
    li8 = jnp.broadcast_to(li, (NSEG, n))

    def rows(k):
        return pl.ds(pl.multiple_of(jnp.where(rev, TSEG - 1 - k, k) * NSEG, NSEG), NSEG)

    def step(k, c, store):
        sr, si = c
        rs = rows(k)
        pr, pi = _cmul(lr8, li8, sr, si)
        nr = pr + s_ref[rs, 0:n]
        ni = pi + s_ref[rs, n:2 * n]
        if store:
            dst = s_ref if out_ref is None else out_ref
            dst[rs, 0:n] = nr.astype(dst.dtype)
            dst[rs, n:2 * n] = ni.astype(dst.dtype)
        return nr, ni

    z = jnp.zeros((NSEG, n), F32)
    er, ei = lax.fori_loop(0, TSEG, functools.partial(step, store=False), (z, z))
    pr, pi = lr, li
    for _ in range(int(math.log2(TSEG))):
        pr, pi = _cmul(pr, pi, pr, pi)

    seg = lax.broadcasted_iota(jnp.int32, (NSEG, n), 0)

    def moved(val, k):
        down = jnp.where(seg >= k, pltpu.roll(val, k, 0), 0.0)
        up = jnp.where(seg < NSEG - k, pltpu.roll(val, NSEG - k, 0), 0.0)
        return jnp.where(rev, up, down)

    k = 1
    while k < NSEG:
        mr, mi = _cmul(pr, pi, moved(er, k), moved(ei, k))
        er, ei = er + mr, ei + mi
        pr, pi = _cmul(pr, pi, pr, pi)
        k *= 2
    cin_r, cin_i = moved(er, 1), moved(ei, 1)
    if visit is None:
        lax.fori_loop(0, TSEG, functools.partial(step, store=True), (cin_r, cin_i))
        return cin_r, cin_i

    def visited(k, c):
        nr, ni = step(k, c[:2], True)
        return (nr, ni) + tuple(visit(k, nr, ni, c[2:]))

    fin = lax.fori_loop(0, TSEG - 1, visited, (cin_r, cin_i) + tuple(carried))
    last_r, last_i = step(TSEG - 1, fin[:2], True)
    return last_r, last_i, fin[2:]


S5_RC = 512


def _s5_specs():
    u_spec = pl.BlockSpec((L, 128), lambda cb, h, d: (0, cb))
    b_spec = pl.BlockSpec((None, None, 128, 2 * SBW), lambda cb, h, d: (d, cb * 2 + h, 0, 0))
    c_spec = pl.BlockSpec((None, None, 2 * SBW, 128), lambda cb, h, d: (d, cb * 2 + h, 0, 0))
    l_spec = pl.BlockSpec((None, None, 1, SBW), lambda cb, h, d: (d, cb * 2 + h, 0, 0))
    d_spec = pl.BlockSpec((1, 128), lambda cb, h, d: (0, cb))
    return u_spec, b_spec, c_spec, l_spec, d_spec


def _s5_fwd(u_p, bcat, ccat, lam_re, lam_im, dskip, gather=()):
    ng = len(gather)
    grid = (SW // 128, 2, 2)

    def body(*refs):
        if ng:
            step = (pl.program_id(0) * grid[1] + pl.program_id(1)) * grid[2] + pl.program_id(2)
            travellers = (refs[6:6 + ng], refs[8 + ng:8 + 2 * ng], *refs[9 + 2 * ng:])
            _gather_under(step, math.prod(grid), travellers)
        compute(*refs[:6], *refs[6 + ng:8 + ng], refs[8 + 2 * ng])
        if ng:
            _gather_done(step, math.prod(grid), travellers)

    def compute(u_ref, b_ref, c_ref, lr_ref, li_ref, d_ref, y_ref, sb_ref, s_scr):
        first = (pl.program_id(1) == 0) & (pl.program_id(2) == 0)

        def proj(i, _):
            rs = pl.ds(pl.multiple_of(i * S5_RC, S5_RC), S5_RC)
            s_scr[rs, :] = jnp.dot(u_ref[rs, :].astype(BF16), b_ref[...], preferred_element_type=F32)
            return 0

        lax.fori_loop(0, L // S5_RC, proj, 0)
        _scan_inplace(s_scr, lr_ref[...], li_ref[...], pl.program_id(2) == 1, out_ref=sb_ref)

        def out(i, _, opening):
            rs = pl.ds(pl.multiple_of(i * S5_RC, S5_RC), S5_RC)
            yv = jnp.dot(sb_ref[rs, :], c_ref[...], preferred_element_type=F32)
            if opening:
                y_ref[rs, :] = d_ref[...] * u_ref[rs, :] + yv
            else:
                y_ref[rs, :] += yv
            return 0

        @pl.when(first)
        def _():
            lax.fori_loop(0, L // S5_RC, functools.partial(out, opening=True), 0)

        @pl.when(jnp.logical_not(first))
        def _():
            lax.fori_loop(0, L // S5_RC, functools.partial(out, opening=False), 0)

    u_spec, b_spec, c_spec, l_spec, d_spec = _s5_specs()
    outs = pl.pallas_call(
        body, name="s5_fwd", grid=grid,
        in_specs=[u_spec, b_spec, c_spec, l_spec, l_spec, d_spec] + [ANY] * ng,
        out_specs=[u_spec, _s5_state_spec()] + [ANY] * ng,
        out_shape=[jax.ShapeDtypeStruct((L, SW), F32), jax.ShapeDtypeStruct((2, NSB, L, 2 * SBW), BF16)]
        + _gather_shapes(gather),
        scratch_shapes=[pltpu.VMEM((L, 2 * SBW), F32)] + (_gather_sems(ng) if ng else []),
        compiler_params=_cp(("arbitrary",) * 3 if ng else ("parallel", "arbitrary", "arbitrary")),
    )(u_p, bcat, ccat, lam_re, lam_im, dskip, *gather)
    return outs[0], outs[1], list(outs[2:])


def _s5_state_spec():
    return pl.BlockSpec((None, None, L, 2 * SBW), lambda cb, h, d: (d, cb * 2 + h, 0, 0))


def _s5_bwd(u_p, dy_p, states, bcat, ccat, lam_re, lam_im, dskip, ride=()):
    kind, riding = ride if ride else (None, ())
    nr = len(riding)
    grid = (SW // 128, 2, 2)

    def body(*refs):
        work = refs[:8] + refs[8 + nr:14 + nr] + refs[14 + 2 * nr:16 + 2 * nr]
        if nr:
            riders = (refs[8:8 + nr], refs[14 + nr:14 + 2 * nr], *refs[16 + 2 * nr:])
            step = [pl.program_id(d) for d in range(3)]

            @pl.when((step[0] == 0) & (step[1] == 0) & (step[2] == 0))
            def _():
                _start_all(_exchange_copies(kind, *riders))

        compute(*work)
        if nr:
            @pl.when((step[0] == grid[0] - 1) & (step[1] == grid[1] - 1) & (step[2] == grid[2] - 1))
            def _():
                _wait_all(_exchange_copies(kind, *riders))

    def compute(u_ref, dy_ref, s_ref, b_ref, c_ref, lr_ref, li_ref, d_ref,
                du_ref, db_ref, dc_ref, dlr_ref, dli_ref, dd_ref, g_scr, gb_scr):
        first = (pl.program_id(1) == 0) & (pl.program_id(2) == 0)
        rev = pl.program_id(2) == 1

        def dstate(i, _):
            rs = pl.ds(pl.multiple_of(i * S5_RC, S5_RC), S5_RC)
            g_scr[rs, :] = lax.dot_general(dy_ref[rs, :].astype(BF16), c_ref[...], _NT, preferred_element_type=F32)
            return 0

        lax.fori_loop(0, L // S5_RC, dstate, 0)

        def before(rows):
            sv = s_ref[rows, :].astype(F32)
            return sv[:, 0:SBW], sv[:, SBW:2 * SBW]

        def dlam(gr, gi, sr, si, ar, ai):
            return ar + gr * sr + gi * si, ai + gi * sr - gr * si

        db_ref[0:NSEG, :] = jnp.zeros((NSEG, 2 * SBW), F32)

        def visit(k, gr, gi, _):
            ts = jnp.where(rev, k + 1, TSEG - 2 - k)
            sr, si = before(pl.ds(pl.multiple_of(ts * NSEG, NSEG), NSEG))
            ar, ai = dlam(gr, gi, sr, si, db_ref[0:NSEG, 0:SBW], db_ref[0:NSEG, SBW:2 * SBW])
            db_ref[0:NSEG, 0:SBW] = ar
            db_ref[0:NSEG, SBW:2 * SBW] = ai
            return ()

        gr, gi, _ = _scan_inplace(g_scr, lr_ref[...], -li_ref[...], jnp.logical_not(rev), visit, (), gb_scr)
        acc = (db_ref[0:NSEG, 0:SBW], db_ref[0:NSEG, SBW:2 * SBW])
        edge_r, edge_i = before(pl.ds(pl.multiple_of(jnp.where(rev, 0, TSEG - 1) * NSEG, NSEG), NSEG))
        seg = lax.broadcasted_iota(jnp.int32, (NSEG, SBW), 0)
        keep = seg != jnp.where(rev, NSEG - 1, 0)

        def neighbour(e):
            return jnp.where(keep, jnp.where(rev, pltpu.roll(e, NSEG - 1, 0), pltpu.roll(e, 1, 0)), 0.0)

        ar, ai = dlam(gr, gi, neighbour(edge_r), neighbour(edge_i), *acc)
        dlr_ref[...] = jnp.sum(ar, axis=0, keepdims=True)
        dli_ref[...] = jnp.sum(ai, axis=0, keepdims=True)

        db_ref[...] = jnp.zeros_like(db_ref)
        dc_ref[...] = jnp.zeros_like(dc_ref)

        @pl.when(first)
        def _():
            dd_ref[...] = jnp.zeros_like(dd_ref)

        def grads(i, _, opening):
            rs = pl.ds(pl.multiple_of(i * S5_RC, S5_RC), S5_RC)
            uv = u_ref[rs, :]
            dyv = dy_ref[rs, :]
            gb = gb_scr[rs, :]
            db_ref[...] += lax.dot_general(uv.astype(BF16), gb, _TN, preferred_element_type=F32)
            dc_ref[...] += lax.dot_general(dyv.astype(BF16), s_ref[rs, :], _TN, preferred_element_type=F32)
            duv = lax.dot_general(gb, b_ref[...], _NT, preferred_element_type=F32)
            if opening:
                du_ref[rs, :] = d_ref[...] * dyv + duv
                dd_ref[...] += jnp.sum(dyv * uv, axis=0, keepdims=True)
            else:
                du_ref[rs, :] += duv
            return 0

        @pl.when(first)
        def _():
            lax.fori_loop(0, L // S5_RC, functools.partial(grads, opening=True), 0)

        @pl.when(jnp.logical_not(first))
        def _():
            lax.fori_loop(0, L // S5_RC, functools.partial(grads, opening=False), 0)

    u_spec, b_spec, c_spec, l_spec, d_spec = _s5_specs()
    outs = pl.pallas_call(
        body, name="s5_bwd", grid=grid,
        in_specs=[u_spec, u_spec, _s5_state_spec(), b_spec, c_spec, l_spec, l_spec, d_spec] + [ANY] * nr,
        out_specs=[u_spec, b_spec, b_spec, l_spec, l_spec, d_spec] + [ANY] * nr,
        out_shape=[jax.ShapeDtypeStruct((L, SW), F32),
                   jax.ShapeDtypeStruct((2, NSB, 128, 2 * SBW), F32), jax.ShapeDtypeStruct((2, NSB, 128, 2 * SBW), F32),
                   jax.ShapeDtypeStruct((2, NSB, 1, SBW), F32), jax.ShapeDtypeStruct((2, NSB, 1, SBW), F32),
                   jax.ShapeDtypeStruct((1, SW), F32)] + (_exchange_shapes(kind, riding) if nr else []),
        scratch_shapes=[pltpu.VMEM((L, 2 * SBW), F32), pltpu.VMEM((L, 2 * SBW), BF16)]
        + (_exchange_sems(kind, nr) if nr else []),
        compiler_params=_cp(("arbitrary",) * 3 if nr else ("parallel", "arbitrary", "arbitrary")),
    )(u_p, dy_p, states, bcat, ccat, lam_re, lam_im, dskip, *riding)
    return list(outs[:6]), list(outs[6:])


def _s5_params(a_re, a_im, log_step, bt_re, bt_im):
    lam = lax.complex(a_re, a_im)
    step = jnp.exp(log_step)[..., None]
    lam_bar = jnp.exp(lam * step)
    b_bar = ((lam_bar - 1.0) / lam)[..., None, :] * lax.complex(bt_re, bt_im)
    return jnp.real(lam_bar), jnp.imag(lam_bar), jnp.real(b_bar), jnp.imag(b_bar)


def _sel():
    i = jnp.arange(8)[None, :, None]
    j = jnp.arange(4)[None, None, :]
    r = jnp.arange(2)[:, None, None]
    return (i == r * 4 + j).astype(F32)


def _to_bcat(bt_re, bt_im):
    def one(bt):
        return jnp.einsum('dkrjcp,rij->dkricjp', bt.reshape(2, 4, 2, 4, GC, NP), _sel()).reshape(2, NSB, 128, SBW)
    return jnp.concatenate([one(bt_re), one(bt_im)], axis=-1)


def _from_bcat(dbcat):
    def one(dbbd):
        return jnp.einsum('dkricjp,rij->dkrjcp', dbbd.reshape(2, 4, 2, 8, GC, 4, NP), _sel()).reshape(2, NG, GC, NP)
    return one(dbcat[..., :SBW]), one(dbcat[..., SBW:])


def _to_ccat(c_re, c_im):
    def one(cc):
        return jnp.einsum('dkrjcp,rij->dkrjpic', cc.reshape(2, 4, 2, 4, GC, NP), _sel()).reshape(2, NSB, SBW, 128)
    return jnp.concatenate([one(c_re), -one(c_im)], axis=-2)


def _from_ccat(dccat):
    def one(dcbd):
        return jnp.einsum('dkrjpic,rij->dkrjcp', dcbd.reshape(2, 4, 2, 4, NP, 8, GC), _sel()).reshape(2, NG, GC, NP)
    return one(dccat[:, :, :SBW]), -one(dccat[:, :, SBW:])


def _gelu(y):
    return 0.5 * y * (1.0 + lax.erf(y * (2.0 ** -0.5)))


def _gelu_grad(y):
    return 0.5 * (1.0 + lax.erf(y * (2.0 ** -0.5))) + y * jnp.exp(-0.5 * y * y) * ((2.0 * math.pi) ** -0.5)


def _sigmoid(z):
    return 0.5 * jnp.tanh(0.5 * z) + 0.5


def _glu_fwd(y, wg):
    def body(y_ref, w_ref, o_ref, z_ref):
        ys = _gelu(y_ref[...])
        z = jnp.dot(ys.astype(BF16), w_ref[...], preferred_element_type=F32)
        z_ref[...] = z
        o_ref[...] = ys * _sigmoid(z)

    row = pl.BlockSpec((TL, SW), lambda i: (i, 0))
    return pl.pallas_call(
        body, name="glu_fwd", grid=(L // TL,),
        in_specs=[row, pl.BlockSpec((SW, SW), lambda i: (0, 0))], out_specs=[row, row],
        out_shape=[jax.ShapeDtypeStruct((L, SW), F32), jax.ShapeDtypeStruct((L, SW), F32)],
        compiler_params=_cp(("parallel",)),
    )(y, wg)


def _glu_bwd(y, z, dout, wg):
    def body(y_ref, z_ref, do_ref, w_ref, dy_ref, dw_ref):
        @pl.when(pl.program_id(0) == 0)
        def _():
            dw_ref[...] = jnp.zeros_like(dw_ref)

        yv = y_ref[...]
        ys = _gelu(yv)
        sg = _sigmoid(z_ref[...])
        dov = do_ref[...]
        dz = (dov * ys * sg * (1.0 - sg)).astype(BF16)
        dys = dov * sg + lax.dot_general(dz, w_ref[...], _NT, preferred_element_type=F32)
        dy_ref[...] = dys * _gelu_grad(yv)
        dw_ref[...] += lax.dot_general(ys.astype(BF16), dz, _TN, preferred_element_type=F32)

    row = pl.BlockSpec((TL, SW), lambda i: (i, 0))
    wsp = pl.BlockSpec((SW, SW), lambda i: (0, 0))
    return pl.pallas_call(
        body, name="glu_bwd", grid=(L // TL,),
        in_specs=[row, row, row, wsp], out_specs=[row, wsp],
        out_shape=[jax.ShapeDtypeStruct((L, SW), F32), jax.ShapeDtypeStruct((SW, SW), F32)],
        compiler_params=_cp(("arbitrary",)),
    )(y, z, dout, wg)


CT = 256
CR = 128
NCT = DFF // CT


def _shifted(ref, r):
    h = 8 * (4 // ref.dtype.itemsize)
    cur = ref[pl.ds(r, CR), :].astype(F32)
    before = ref[pl.ds(pl.multiple_of(jnp.maximum(r - h, 0), h), h), :][h - 1:h, :].astype(F32)
    after = ref[pl.ds(pl.multiple_of(jnp.minimum(r + CR, L - h), h), h), :][0:1, :].astype(F32)
    before = jnp.where(r > 0, before, 0.0)
    after = jnp.where(r + CR < L, after, 0.0)
    row = lax.broadcasted_iota(jnp.int32, cur.shape, 0)
    prev = jnp.where(row == 0, before, pltpu.roll(cur, 1, 0))
    nxt = jnp.where(row == CR - 1, after, pltpu.roll(cur, CR - 1, 0))
    return prev, cur, nxt


def _conv3(ref, r, w_ref, b_ref):
    prev, cur, nxt = _shifted(ref, r)
    return w_ref[0:1, :] * prev + w_ref[1:2, :] * cur + w_ref[2:3, :] * nxt + b_ref[...]


def _convact_fwd(up, conv_w, conv_b):
    def body(ug_ref, uv_ref, wg_ref, wv_ref, bg_ref, bv_ref, o_ref, g_ref, v_ref):
        def chunk(i, _):
            r = pl.multiple_of(i * CR, CR)
            rs = pl.ds(r, CR)
            g = _conv3(ug_ref, r, wg_ref, bg_ref)
            v = _conv3(uv_ref, r, wv_ref, bv_ref)
            o_ref[rs, :] = (g * _sigmoid(g) * v).astype(BF16)
            g_ref[rs, :] = g.astype(BF16)
            v_ref[rs, :] = v.astype(BF16)
            return 0

        lax.fori_loop(0, L // CR, chunk, 0)

    gcol = pl.BlockSpec((L, CT), lambda j: (0, j))
    vcol = pl.BlockSpec((L, CT), lambda j: (0, j + NCT))
    return pl.pallas_call(
        body, name="convact_fwd", grid=(NCT,),
        in_specs=[gcol, vcol,
                  pl.BlockSpec((3, CT), lambda j: (0, j)), pl.BlockSpec((3, CT), lambda j: (0, j + NCT)),
                  pl.BlockSpec((1, CT), lambda j: (0, j)), pl.BlockSpec((1, CT), lambda j: (0, j + NCT))],
        out_specs=[gcol, gcol, gcol], out_shape=[jax.ShapeDtypeStruct((L, DFF), BF16)] * 3,
        compiler_params=_cp(("parallel",)),
    )(up, up, conv_w, conv_w, conv_b, conv_b)


def _convact_bwd(up, gq, vq, dact, conv_w):
    def body(ug_ref, uv_ref, g_ref, v_ref, da_ref, wg_ref, wv_ref, du_ref, dw_ref, db_ref, d_scr):
        def half_step(gate, u_ref, w_ref):
            db_ref[...] = jnp.zeros_like(db_ref)
            dw_ref[...] = jnp.zeros_like(dw_ref)

            def elementwise(i, _):
                rs = pl.ds(pl.multiple_of(i * CR, CR), CR)
                g = g_ref[rs, :].astype(F32)
                sg = _sigmoid(g)
                da = da_ref[rs, :].astype(F32)
                if gate:
                    d = da * v_ref[rs, :].astype(F32) * sg * (1.0 + g * (1.0 - sg))
                else:
                    d = da * g * sg
                d_scr[rs, :] = d
                db_ref[...] += jnp.sum(d, axis=0, keepdims=True)
                return 0

            lax.fori_loop(0, L // CR, elementwise, 0)

            def transpose_conv(i, _):
                r = pl.multiple_of(i * CR, CR)
                rs = pl.ds(r, CR)
                prev, cur, nxt = _shifted(d_scr, r)
                du_ref[rs, :] = (w_ref[0:1, :] * nxt + w_ref[1:2, :] * cur + w_ref[2:3, :] * prev).astype(BF16)
                uv = u_ref[rs, :].astype(F32)
                for k, d in enumerate((nxt, cur, prev)):
                    dw_ref[k:k + 1, :] += jnp.sum(d * uv, axis=0, keepdims=True)
                return 0

            lax.fori_loop(0, L // CR, transpose_conv, 0)

        @pl.when(pl.program_id(1) == 0)
        def _():
            half_step(True, ug_ref, wg_ref)

        @pl.when(pl.program_id(1) == 1)
        def _():
            half_step(False, uv_ref, wv_ref)

    def col(rows, off):
        return pl.BlockSpec((rows, CT), lambda j, h: (0, j + off))

    def out(rows):
        return pl.BlockSpec((rows, CT), lambda j, h: (0, j + h * NCT))

    return pl.pallas_call(
        body, name="convact_bwd", grid=(NCT, 2),
        in_specs=[col(L, 0), col(L, NCT), col(L, 0), col(L, 0), col(L, 0), col(3, 0), col(3, NCT)],
        out_specs=[out(L), out(3), out(1)],
        out_shape=[jax.ShapeDtypeStruct((L, 2 * DFF), BF16), jax.ShapeDtypeStruct((3, 2 * DFF), F32),
                   jax.ShapeDtypeStruct((1, 2 * DFF), F32)],
        scratch_shapes=[pltpu.VMEM((L, CT), F32)],
        compiler_params=_cp(("parallel", "arbitrary")),
    )(up, up, gq, vq, dact, conv_w, conv_w)


def _local_step(x, tgt, w_in_t, p, attend, scan, stage):
    tabs = _rope_tables()
    lam_re, lam_im, bb_re, bb_im = _s5_params(p["a_re"], p["a_im"], p["log_step"], p["bt_re"], p["bt_im"])
    bcat = _to_bcat(bb_re, bb_im).astype(BF16)
    ccat = _to_ccat(p["c_re"], p["c_im"]).astype(BF16)
    lam_re4, lam_im4 = lam_re.reshape(2, NSB, 1, SBW), lam_im.reshape(2, NSB, 1, SBW)
    dskip = p["d_skip"].reshape(1, SW)
    g_mix, g_ffn, g_fin = p["norm_mix_g"].reshape(1, D), p["norm_ffn_g"].reshape(1, D), p["norm_final_g"].reshape(1, D)
    g_attn, g_ssm = p["norm_attn_g"].reshape(1, AW), p["norm_ssm_g"].reshape(1, SW)
    sink = p["sink"].reshape(NQ)
    conv_b = p["conv_b"].reshape(1, 2 * DFF)

    rows, gain = jax.ShapeDtypeStruct((L, D), F32), jax.ShapeDtypeStruct((1, D), F32)
    rows16 = jax.ShapeDtypeStruct((L, D), BF16)
    h1, qkv, u = _in_proj(x, g_mix, w_in_t, tabs)
    attn, lse, wts = attend(qkv, sink)
    u_p = _perm(u)
    y_p, states, more = scan(u_p, bcat, ccat, lam_re4, lam_im4, dskip)
    wts = dict(wts, **more)
    w_glu, w_out, w_up_t, w_down, conv_w = (wts[k] for k in ("w_glu", "w_out", "w_up_t", "w_down", "conv_w"))
    ysg_p, z_p = _glu_fwd(y_p, w_glu)
    ysg = _unperm(ysg_p)
    mixed, x1, h2 = _out_proj(attn, ysg, g_attn, g_ssm, w_out, x, g_ffn)
    up = _mm(h2, w_up_t, tb=True, name="ffn_up", tn=1408, out_dtype=BF16)
    act, gq, vq = _convact_fwd(up, conv_w, conv_b)
    loss, dx2, dx2b, dg_fin = _mm(
        act, w_down, add=x1, name="ffn_down", tm=512, tk=DFF,
        post=(_final_post, [tgt, g_fin], [jax.ShapeDtypeStruct((1, 1), F32), rows, rows16, gain]))

    def riding(res, ride):
        return res if ride else (res, None)

    dw_down = _mm(act, dx2b, ta=True, name="ffn_down_dw", tm=256, tk=L)
    ride = stage(("w_down",), "cores", [dw_down])
    dact, got = riding(_mm(dx2b, w_down, tb=True, name="ffn_down_dx", tn=1408, out_dtype=BF16, ride=ride), ride)
    ride = stage(("w_down",), "chips", got)
    dup, dconv_w, dconv_b = _convact_bwd(up, gq, vq, dact, conv_w)
    dw_up_t, got = riding(_mm(dup, h2, ta=True, name="ffn_up_dw", tm=512, tk=L, ride=ride), ride)
    stage(("w_down",), "done", got)
    ride = stage(("w_up_t",), "cores", [dw_up_t])
    (dx1, dx1b, dg_ffn), got = riding(
        _mm(dup, w_up_t, name="ffn_up_dx", tm=512, tk=2 * DFF, ride=ride,
            post=(_rms_bwd_post, [x1, dx2, g_ffn], [rows, rows16, gain])), ride)
    ride = stage(("w_up_t",), "chips", got)
    dattn, dysg, dg_attn, dg_ssm = _out_proj_dx(dx1b, w_out, attn, ysg, g_attn, g_ssm)
    dw_out = _mm(mixed, dx1b, ta=True, name="out_proj_dw", tm=512, tk=L)
    dy_p, dw_glu = _glu_bwd(y_p, z_p, _perm(dysg), w_glu)
    (du_p, dbcat, dccat, dlam_re, dlam_im, dd), got = _s5_bwd(u_p, dy_p, states, bcat, ccat, lam_re4, lam_im4, dskip,
                                                              ride=ride)
    stage(("w_up_t",), "done", got)
    dbb_re, dbb_im = _from_bcat(dbcat)
    dc_re, dc_im = _from_ccat(_swap(dccat))
    mix = ("w_out", "w_glu")
    ride = stage(mix, "cores", [dw_out, dw_glu])
    (dq, dk, dv, dsink), got = _attn_bwd(qkv, sink, attn, lse, dattn, ride=ride)
    ride = stage(mix, "chips", got)
    dproj = _rope_bwd(dq, dk, dv, _unperm(du_p), tabs)
    dw_in_t, got = riding(_mm(dproj, h1, ta=True, name="in_proj_dw", tm=640, tk=L, ride=ride), ride)
    stage(mix, "done", got)
    grad_x, dg_mix = _in_proj_dx(dproj, w_in_t, x, g_mix, dx1)

    big = dict(w_in_t=dw_in_t)
    small = dict(norm_mix_g=dg_mix, norm_attn_g=dg_attn, norm_ssm_g=dg_ssm, norm_ffn_g=dg_ffn, norm_final_g=dg_fin,
                 sink=dsink[:, 0], conv_b=dconv_b, d_skip=dd, conv_w=dconv_w,
                 lam_re=dlam_re.reshape(2, NG, NP), lam_im=dlam_im.reshape(2, NG, NP),
                 bb_re=dbb_re, bb_im=dbb_im, c_re=dc_re, c_im=dc_im, loss=loss.reshape(1))
    return grad_x, big, small


ANY = pl.BlockSpec(memory_space=pl.ANY)


def _coords():
    return lax.axis_index("x"), lax.axis_index("y"), lax.axis_index("c")


def _flip(v, b):
    return v + b - 2 * v * b if b else v


def _all_gather(shards, name):
    n = len(shards)

    def body(*refs):
        _gather_start(refs[:n], refs[n:2 * n], *refs[2 * n:])
        _gather_finish(refs[:n], refs[n:2 * n], *refs[2 * n:])

    return pl.pallas_call(
        body, name=name,
        in_specs=[ANY] * n, out_specs=[ANY] * n,
        out_shape=_gather_shapes(shards), scratch_shapes=_gather_sems(n),
    )(*shards)


def _gather_shapes(shards):
    return [jax.ShapeDtypeStruct((NDEV * s.shape[0], s.shape[1]), s.dtype) for s in shards]


def _gather_sems(n):
    return [pltpu.SemaphoreType.DMA((7 * n,)), pltpu.SemaphoreType.DMA((7 * n,)), pltpu.SemaphoreType.DMA((n,))]


def _gather_copies(ins, outs, send_sems, recv_sems, local_sems, a):
    x, y, c = _coords()
    me, sibling = (x, y, c), (x, y, 1 - c)
    chips = [(1 - x, y), (x, 1 - y), (1 - x, 1 - y)]
    r = ins[a].shape[0]

    def rows(px, py, pc):
        return outs[a].at[pl.ds(pl.multiple_of((4 * px + 2 * py + pc) * r, 8), r), :]

    def copy(k, block, to, src=None):
        return pltpu.make_async_remote_copy(
            src_ref=rows(*block) if src is None else src, dst_ref=rows(*block),
            send_sem=send_sems.at[a * 7 + k], recv_sem=recv_sems.at[a * 7 + k],
            device_id=to, device_id_type=pl.DeviceIdType.MESH)

    mine = pltpu.make_async_copy(ins[a], rows(*me), local_sems.at[a])
    first = [copy(0, me, sibling, src=ins[a])]
    first += [copy(1 + j, me, (*chip, c), src=ins[a]) for j, chip in enumerate(chips)]
    passed = [copy(4 + j, (*chip, c), sibling) for j, chip in enumerate(chips)]
    arrivals = [copy(1 + j, (*chip, c), me) for j, chip in enumerate(chips)]
    from_sibling = [copy(0, sibling, me)] + [copy(4 + j, (*chip, 1 - c), me) for j, chip in enumerate(chips)]
    return mine, first, passed, arrivals, from_sibling


def _gather_start(ins, outs, send_sems, recv_sems, local_sems):
    for a in range(len(ins)):
        mine, first, _, _, _ = _gather_copies(ins, outs, send_sems, recv_sems, local_sems, a)
        mine.start()
        for cp in first:
            cp.start()


def _gather_forward(ins, outs, send_sems, recv_sems, local_sems):
    for a in range(len(ins)):
        _, _, passed, arrivals, _ = _gather_copies(ins, outs, send_sems, recv_sems, local_sems, a)
        for arrived, onward in zip(arrivals, passed):
            arrived.wait_recv()
            onward.start()


def _gather_wait(ins, outs, send_sems, recv_sems, local_sems):
    for a in range(len(ins)):
        mine, first, passed, _, from_sibling = _gather_copies(ins, outs, send_sems, recv_sems, local_sems, a)
        for cp in from_sibling:
            cp.wait_recv()
        for cp in first + passed:
            cp.wait_send()
        mine.wait()


def _gather_finish(*refs):
    _gather_forward(*refs)
    _gather_wait(*refs)


def _gather_under(step, steps, travellers):
    @pl.when(step == 0)
    def _():
        _gather_start(*travellers)

    @pl.when(step == (3 * steps) // 4)
    def _():
        _gather_forward(*travellers)


def _gather_done(step, steps, travellers):
    @pl.when(step == steps - 1)
    def _():
        _gather_wait(*travellers)


NCHIP = 4
CHIP_FLIPS = ((1, 0), (0, 1), (1, 1))


def _planned_copies(ins, outs, send_sems, recv_sems, plan):
    return [pltpu.make_async_remote_copy(
        src_ref=ins[a].at[src], dst_ref=outs[a].at[dst], send_sem=send_sems.at[k], recv_sem=recv_sems.at[k],
        device_id=to, device_id_type=pl.DeviceIdType.MESH) for k, (a, src, dst, to) in enumerate(plan)]


def _start_all(copies):
    for cp in copies:
        cp.start()


def _wait_all(copies):
    for cp in copies:
        cp.wait_recv()
    for cp in copies:
        cp.wait_send()


SLOTS = {"cores": NCHIP, "chips": 3}


def _exchange_copies(kind, ins, outs, send_sems, recv_sems):
    x, y, c = _coords()
    plan = []
    for a in range(len(ins)):
        if kind == "cores":
            plan += [(a, 2 * q + 1 - c, q, (x, y, 1 - c)) for q in range(NCHIP)]
        else:
            for j, (fx, fy) in enumerate(CHIP_FLIPS):
                px, py = _flip(x, fx), _flip(y, fy)
                plan.append((a, 2 * px + py, j, (px, py, c)))
    return _planned_copies(ins, outs, send_sems, recv_sems, plan)


def _exchange_shapes(kind, parts):
    return [jax.ShapeDtypeStruct((SLOTS[kind],) + s.shape[1:], s.dtype) for s in parts]


def _exchange_sems(kind, n):
    return [pltpu.SemaphoreType.DMA((SLOTS[kind] * n,)), pltpu.SemaphoreType.DMA((SLOTS[kind] * n,))]


def _exchange(kind, parts, name):
    n = len(parts)

    def body(*refs):
        copies = _exchange_copies(kind, refs[:n], refs[n:2 * n], *refs[2 * n:])
        _start_all(copies)
        _wait_all(copies)

    return pl.pallas_call(
        body, name=name, in_specs=[ANY] * n, out_specs=[ANY] * n,
        out_shape=_exchange_shapes(kind, parts), scratch_shapes=_exchange_sems(kind, n),
    )(*parts)


def _pair_sum(where, part, recv, wire_dtype, name):
    _, r, c = part.shape
    tr = _pick(r, 256, 16)

    def body(w_ref, p_ref, r_ref, pb_ref, own_ref):
        s = p_ref[...] + r_ref[...]
        pb_ref[...] = s.astype(wire_dtype)

        @pl.when(pl.program_id(1) == w_ref[1])
        def _():
            own_ref[...] = s

    return pl.pallas_call(
        body, name=name,
        grid_spec=pltpu.PrefetchScalarGridSpec(
            num_scalar_prefetch=1, grid=(r // tr, NCHIP),
            in_specs=[pl.BlockSpec((None, tr, c), lambda i, q, w: (2 * q + w[0], i, 0)),
                      pl.BlockSpec((None, tr, c), lambda i, q, w: (q, i, 0))],
            out_specs=[pl.BlockSpec((None, tr, c), lambda i, q, w: (q, i, 0)),
                       pl.BlockSpec((tr, c), lambda i, q, w: (i, 0))]),
        out_shape=[jax.ShapeDtypeStruct((NCHIP, r, c), wire_dtype), jax.ShapeDtypeStruct((r, c), F32)],
        compiler_params=_cp(("parallel", "arbitrary")),
    )(where, part, recv)


def _chip_sum(own, recv, name):
    r, c = own.shape
    tr = _pick(r, 256, 16)

    def body(o_ref, r_ref, out_ref):
        acc = o_ref[...]
        for j in range(3):
            acc = acc + r_ref[j].astype(F32)
        out_ref[...] = acc

    return pl.pallas_call(
        body, name=name, grid=(r // tr,),
        in_specs=[pl.BlockSpec((tr, c), lambda i: (i, 0)), pl.BlockSpec((3, tr, c), lambda i: (0, i, 0))],
        out_specs=pl.BlockSpec((tr, c), lambda i: (i, 0)),
        out_shape=jax.ShapeDtypeStruct((r, c), F32),
        compiler_params=_cp(("parallel",)),
    )(own, recv)


def _adamw(w, own, recv, m, v, name):
    r, c = w.shape
    tr = _pick(r, 256, 16)

    def body(w_ref, o_ref, r_ref, m_ref, v_ref, g_ref, d_ref, nm_ref, nv_ref):
        acc = o_ref[...]
        for j in range(3):
            acc = acc + r_ref[j].astype(F32)
        g_ref[...] = acc
        _adamw_refs(w_ref, g_ref, m_ref, v_ref, d_ref, nm_ref, nv_ref)

    blk = pl.BlockSpec((tr, c), lambda i: (i, 0))
    return pl.pallas_call(
        body, name=name, grid=(r // tr,),
        in_specs=[blk, blk, pl.BlockSpec((3, tr, c), lambda i: (0, i, 0)), blk, blk], out_specs=[blk] * 4,
        out_shape=[jax.ShapeDtypeStruct((r, c), F32)] * 4,
        compiler_params=_cp(("parallel",)),
    )(w, own, recv, m, v)


def _adamw_refs(w_ref, g_ref, m_ref, v_ref, d_ref, nm_ref, nv_ref):
    gv = g_ref[...]
    nm = B1 * m_ref[...] + (1.0 - B1) * gv
    nv = B2 * v_ref[...] + (1.0 - B2) * (gv * gv)
    nm_ref[...] = nm
    nv_ref[...] = nv
    d_ref[...] = -LR * ((nm / C1) / (jnp.sqrt(nv / C2) + AEPS) + WD * w_ref[...])


def _adamw_small(ws, gs, ms, vs, name):
    n = len(ws)

    def body(*refs):
        groups = [refs[i * n:(i + 1) * n] for i in range(7)]
        for per_param in zip(*groups):
            _adamw_refs(*per_param)

    vm = pl.BlockSpec(memory_space=pltpu.VMEM)
    outs = pl.pallas_call(
        body, name=name, in_specs=[vm] * (4 * n), out_specs=[vm] * (3 * n),
        out_shape=[jax.ShapeDtypeStruct(a.shape, F32) for a in ws] * 3,
    )(*ws, *gs, *ms, *vs)
    return outs[:n], outs[n:2 * n], outs[2 * n:]


def _swap(a):
    return jnp.swapaxes(a, -1, -2)


VIEWS = {
    "w_in": (lambda a: a[0].T, lambda u: u.T[None]),
    "w_up": (lambda a: a[0].T, lambda u: u.T[None]),
    "w_glu": (lambda a: a[0], lambda u: u[None]),
    "w_out": (lambda a: a[0], lambda u: u[None]),
    "w_down": (lambda a: a[0], lambda u: u[None]),
    "conv_w": (lambda a: a[0], lambda u: u[None]),
    "norm_mix_g": (lambda a: a, lambda u: u),
    "norm_attn_g": (lambda a: a, lambda u: u),
    "norm_ssm_g": (lambda a: a, lambda u: u),
    "norm_ffn_g": (lambda a: a, lambda u: u),
    "norm_final_g": (lambda a: a[None], lambda u: u[0]),
    "conv_b": (lambda a: a, lambda u: u),
    "sink": (lambda a: a, lambda u: u),
    "a_re": (lambda a: a.reshape(2 * NG, NP), lambda u: u.reshape(1, 2, NG, NP)),
    "a_im": (lambda a: a.reshape(2 * NG, NP), lambda u: u.reshape(1, 2, NG, NP)),
    "log_step": (lambda a: a[0], lambda u: u[None]),
    "b_re": (lambda a: _swap(a[0]).reshape(2 * NG * GC, NP), lambda u: _swap(u.reshape(2, NG, GC, NP))[None]),
    "b_im": (lambda a: _swap(a[0]).reshape(2 * NG * GC, NP), lambda u: _swap(u.reshape(2, NG, GC, NP))[None]),
    "c_re": (lambda a: a.reshape(2 * NG * GC, NP), lambda u: u.reshape(1, 2, NG, GC, NP)),
    "c_im": (lambda a: a.reshape(2 * NG * GC, NP), lambda u: u.reshape(1, 2, NG, GC, NP)),
    "d_skip": (lambda a: a[0].T, lambda u: u.T[None]),
}
BIG = ["w_in", "w_glu", "w_out", "w_up", "w_down"]
PACK_W = 1024


def _pack(arrs, rows):
    flat = jnp.concatenate([a.reshape(-1).astype(F32) for a in arrs])
    return jnp.pad(flat, (0, rows * PACK_W - flat.shape[0])).reshape(rows, PACK_W)


def _unpack(packed, shapes):
    flat = packed.reshape(-1)
    out, off = [], 0
    for s in shapes:
        size = math.prod(s)
        out.append(flat[off:off + size].reshape(s))
        off += size
    return out


def kernel(x, norm_mix_g, w_in, a_re, a_im, log_step, b_re, b_im, c_re, c_im, d_skip, w_glu, sink, norm_attn_g, norm_ssm_g, w_out, norm_ffn_g, w_up, conv_w, conv_b, w_down, norm_final_g, loss_target, m_norm_mix_g, m_w_in, m_a_re, m_a_im, m_log_step, m_b_re, m_b_im, m_c_re, m_c_im, m_d_skip, m_w_glu, m_sink, m_norm_attn_g, m_norm_ssm_g, m_w_out, m_norm_ffn_g, m_w_up, m_conv_w, m_conv_b, m_w_down, m_norm_final_g, v_norm_mix_g, v_w_in, v_a_re, v_a_im, v_log_step, v_b_re, v_b_im, v_c_re, v_c_im, v_d_skip, v_w_glu, v_sink, v_norm_attn_g, v_norm_ssm_g, v_w_out, v_norm_ffn_g, v_w_up, v_conv_w, v_conv_b, v_w_down, v_norm_final_g):
    args = dict(locals())
    names = ["norm_mix_g", "w_in", "a_re", "a_im", "log_step", "b_re", "b_im", "c_re", "c_im", "d_skip", "w_glu",
             "sink", "norm_attn_g", "norm_ssm_g", "w_out", "norm_ffn_g", "w_up", "conv_w", "conv_b", "w_down",
             "norm_final_g"]
    w = {k: args[k] for k in names}
    m = {k: args["m_" + k] for k in names}
    v = {k: args["v_" + k] for k in names}

    (w_in_t,) = _all_gather([w_in[0].T.astype(BF16)], "gather_w_in")
    under_attn = dict(w_glu=w_glu[0].astype(BF16), w_out=w_out[0].astype(BF16),
                      conv_w=jnp.pad(conv_w[0], ((0, 5), (0, 0))))
    under_scan = dict(w_up_t=w_up[0].T.astype(BF16), w_down=w_down[0].astype(BF16))

    ax, ay, ac = _coords()
    me = 4 * ax + 2 * ay + ac
    where = jnp.stack([ac, 2 * ax + ay]).astype(jnp.int32)
    parts, own, got = {}, {}, {}

    def split8(g):
        return g.reshape(NDEV, g.shape[0] // NDEV, g.shape[1])

    def attend(qkv, sink_):
        attn, lse, gathered = _attn_fwd(qkv, sink_, gather=list(under_attn.values()))
        wts = dict(zip(under_attn.keys(), gathered))
        wts["conv_w"] = (wts["conv_w"].reshape(NDEV, 8, 2 * DFF // NDEV)[:, :3].transpose(1, 0, 2)
                         .reshape(3, 2 * DFF))
        return attn, lse, wts

    def scan(*operands):
        y_p, states, gathered = _s5_fwd(*operands, gather=list(under_scan.values()))
        return y_p, states, dict(zip(under_scan.keys(), gathered))

    def pair_sum(k, from_core):
        per_chip, own[k] = _pair_sum(where, parts[k], from_core, F32 if k == "small" else BF16, "pair_sum_" + k)
        return per_chip

    def stage(ks, phase, payload):
        if phase == "cores":
            parts.update({k: split8(g) for k, g in zip(ks, payload)})
            return ("cores", [parts[k] for k in ks])
        if phase == "chips":
            return ("chips", [pair_sum(k, fc) for k, fc in zip(ks, payload)])
        got.update(zip(ks, payload))
        return ()

    p = {k: w[k][0] for k in ("norm_mix_g", "a_re", "a_im", "log_step", "c_re", "c_im", "d_skip", "sink",
                              "norm_attn_g", "norm_ssm_g", "norm_ffn_g", "conv_b")}
    p["norm_final_g"] = norm_final_g
    p["bt_re"], p["bt_im"] = _swap(b_re[0]), _swap(b_im[0])
    grad_x, big, small = _local_step(x[0], loss_target[0], w_in_t, p, attend, scan, stage)

    small_names = list(small.keys())
    small_shapes = [small[k].shape for k in small_names]
    n_small = sum(math.prod(s) for s in small_shapes)
    rows_dev = -(-n_small // (PACK_W * NDEV * 16)) * 16
    spack = _pack([small[k] for k in small_names], rows_dev * NDEV)
    late = ["w_in_t", "small"]
    parts.update(w_in_t=split8(big["w_in_t"]), small=spack.reshape(NDEV, rows_dev, PACK_W))
    from_cores = _exchange("cores", [parts[k] for k in late], "exchange_cores")
    from_chips = _exchange("chips", [pair_sum(k, fc) for k, fc in zip(late, from_cores)], "exchange_chips")
    got.update(zip(late, from_chips))
    (small_full,) = _all_gather([_chip_sum(own["small"], got["small"], "chip_sum_small")], "gather_small")
    sm = dict(zip(small_names, _unpack(small_full, small_shapes)))

    _, s5_vjp = jax.vjp(_s5_params, a_re[0], a_im[0], log_step[0], p["bt_re"], p["bt_im"])
    da_re, da_im, dlog_step, dbt_re, dbt_im = s5_vjp((sm["lam_re"], sm["lam_im"], sm["bb_re"], sm["bb_im"]))
    gview = {
        "norm_mix_g": sm["norm_mix_g"], "norm_attn_g": sm["norm_attn_g"], "norm_ssm_g": sm["norm_ssm_g"],
        "norm_ffn_g": sm["norm_ffn_g"], "norm_final_g": sm["norm_final_g"], "conv_b": sm["conv_b"],
        "sink": sm["sink"][None], "a_re": da_re.reshape(2 * NG, NP), "a_im": da_im.reshape(2 * NG, NP),
        "log_step": dlog_step, "b_re": dbt_re.reshape(2 * NG * GC, NP), "b_im": dbt_im.reshape(2 * NG * GC, NP),
        "c_re": sm["c_re"].reshape(2 * NG * GC, NP), "c_im": sm["c_im"].reshape(2 * NG * GC, NP),
        "d_skip": sm["d_skip"].reshape(NG, GC).T,
        "conv_w": lax.dynamic_slice_in_dim(sm["conv_w"], me * (2 * DFF // NDEV), 2 * DFF // NDEV, axis=1),
    }

    dview, mview, vview = {}, {}, {}
    for k, kg in zip(BIG, ("w_in_t", "w_glu", "w_out", "w_up_t", "w_down")):
        to = VIEWS[k][0]
        gview[k], dview[k], mview[k], vview[k] = _adamw(to(w[k]), own[kg], got[kg], to(m[k]), to(v[k]), "adamw_" + k)
    rest = [k for k in names if k not in BIG]
    outs = _adamw_small([VIEWS[k][0](w[k]) for k in rest], [gview[k] for k in rest],
                        [VIEWS[k][0](m[k]) for k in rest], [VIEWS[k][0](v[k]) for k in rest], "adamw_small")
    for dst, vals in zip((dview, mview, vview), outs):
        dst.update(dict(zip(rest, vals)))

    def back(views):
        return [VIEWS[k][1](views[k]) for k in names]

    return (sm["loss"][0], grad_x[None], *back(gview), *back(dview), *back(mview), *back(vview))
```

```python
import functools
import math

import jax
import jax.numpy as jnp
from jax import lax
from jax.experimental import pallas as pl
from jax.experimental.pallas import tpu as pltpu

F32 = jnp.float32
BF16 = jnp.bfloat16

L = 4096
D = 1024
NQ, NKV, HD = 8, 2, 64
AW = NQ * HD
KVW = NKV * HD
SW = 512
NG, GC, NP = 32, 16, 64
INW = AW + 2 * KVW + SW
DFF = 2816
BLK = 128
WIN = 3 * BLK
EPS = 1e-6
ROPE_THETA = 500000.0
NSEG = 32
TSEG = L // NSEG
SBW = 256
NSB = NG * NP // SBW
NDEV = 8
MESH_AXES = ("x", "y", "c")

LR, B1, B2, AEPS, WD, STEP = 0.001, 0.9, 0.999, 1e-08, 0.01, 10
C1 = 1.0 - B1 ** STEP
C2 = 1.0 - B2 ** STEP

VMEM_LIMIT = 56 * 1024 * 1024


def _pick(n, target, mult):
    best = None
    for t in range(mult, min(n, target) + 1, mult):
        if n % t == 0:
            best = t
    return best if best is not None else n


def _cp(sem):
    return pltpu.CompilerParams(dimension_semantics=sem, vmem_limit_bytes=VMEM_LIMIT)


def _mm(a, b, *, ta=False, tb=False, out_dtype=F32, add=None, ride=(), post=None, name, tm=1024, tn=1024, tk=1024):
    m, k = (a.shape[1], a.shape[0]) if ta else a.shape
    n = b.shape[0] if tb else b.shape[1]
    assert k == (b.shape[1] if tb else b.shape[0])
    tm, tn, tk = _pick(m, tm, 128), _pick(n, tn, 128), _pick(k, tk, 128)
    grid = (m // tm, n // tn, k // tk)
    nk = grid[2]
    dn = (((0 if ta else 1,), (1 if tb else 0,)), ((), ()))
    n_in = 2 + (add is not None)
    kind, riding = ride if ride else (None, ())
    nr = len(riding)
    post_fn, post_ins, post_outs = post if post is not None else (None, (), ())
    n_pi = len(post_ins)
    n_out = len(post_outs) if post is not None else 1
    assert post is None or grid[1] == 1

    def body(*refs):
        a_ref, b_ref = refs[0], refs[1]
        pin = refs[n_in:n_in + n_pi]
        base = n_in + n_pi + nr
        o_refs = refs[base:base + n_out]
        acc_ref = refs[base + n_out + nr]
        step = [pl.program_id(d) for d in range(3)]
        kk = step[2]
        if nr:
            riders = (refs[n_in + n_pi:base], refs[base + n_out:base + n_out + nr], *refs[base + n_out + nr + 1:])

            @pl.when((step[0] == 0) & (step[1] == 0) & (kk == 0))
            def _():
                _start_all(_exchange_copies(kind, *riders))

        prod = lax.dot_general(a_ref[...].astype(BF16), b_ref[...].astype(BF16), dn, preferred_element_type=F32)

        def finish(r):
            if add is not None:
                r = r + refs[2][...]
            if post_fn is None:
                o_refs[0][...] = r.astype(out_dtype)
            else:
                post_fn(r, step[0], pin, o_refs)

        if nk == 1:
            finish(prod)
        else:
            @pl.when(kk == 0)
            def _():
                acc_ref[...] = prod

            @pl.when((kk > 0) & (kk < nk - 1))
            def _():
                acc_ref[...] += prod

            @pl.when(kk == nk - 1)
            def _():
                finish(acc_ref[...] + prod)

        if nr:
            @pl.when((step[0] == grid[0] - 1) & (step[1] == grid[1] - 1) & (kk == nk - 1))
            def _():
                _wait_all(_exchange_copies(kind, *riders))

    a_spec = pl.BlockSpec((tk, tm), lambda i, j, kk: (kk, i)) if ta else pl.BlockSpec((tm, tk), lambda i, j, kk: (i, kk))
    b_spec = pl.BlockSpec((tn, tk), lambda i, j, kk: (j, kk)) if tb else pl.BlockSpec((tk, tn), lambda i, j, kk: (kk, j))
    def row_spec(shape):
        return pl.BlockSpec((tm if shape[0] == m else shape[0], shape[1]),
                            (lambda i, j, kk: (i, 0)) if shape[0] == m else (lambda i, j, kk: (0, 0)))

    in_specs = [a_spec, b_spec]
    args = [a, b]
    if add is not None:
        in_specs.append(pl.BlockSpec((tm, tn), lambda i, j, kk: (i, j)))
        args.append(add)
    if post is None:
        main_specs = [pl.BlockSpec((tm, tn), lambda i, j, kk: (i, j))]
        main_shapes = [jax.ShapeDtypeStruct((m, n), out_dtype)]
    else:
        main_specs = [row_spec(s.shape) for s in post_outs]
        main_shapes = list(post_outs)
    outs = pl.pallas_call(
        body, name=name, grid=grid,
        in_specs=in_specs + [row_spec(p.shape) for p in post_ins] + [ANY] * nr,
        out_specs=main_specs + [ANY] * nr,
        out_shape=main_shapes + (_exchange_shapes(kind, riding) if nr else []),
        scratch_shapes=[pltpu.VMEM((tm, tn) if nk > 1 else (8, 128), F32)] + (_exchange_sems(kind, nr) if nr else []),
        compiler_params=_cp(("arbitrary",) * 3 if (nr or post is not None) else ("parallel", "parallel", "arbitrary")),
    )(*args, *post_ins, *riding)
    main = outs[0] if post is None else list(outs[:n_out])
    return (main, list(outs[n_out:])) if nr else main


TL = 512


def _rms(xv, gv):
    return xv * lax.rsqrt(jnp.mean(xv * xv, axis=-1, keepdims=True) + EPS) * gv


def _rows(width):
    return pl.BlockSpec((TL, width), lambda i: (i, 0))


def _whole(shape):
    return pl.BlockSpec(shape, lambda i: (0,) * len(shape))


def _in_proj(x, g, w_in_t, tabs):
    qkw = AW + 2 * KVW

    def body(x_ref, g_ref, w_ref, c_ref, sa_ref, sb_ref, h_ref, qkv_ref, u_ref):
        h = _rms(x_ref[...], g_ref[...]).astype(BF16)
        h_ref[...] = h
        proj = lax.dot_general(h, w_ref[...], _NT, preferred_element_type=F32)
        for j in range(qkw // 128):
            cols = slice(j * 128, (j + 1) * 128)
            xv = proj[:, cols]
            if j < (AW + KVW) // 128:
                xv = _rope(xv, c_ref[...], sa_ref[...], sb_ref[...], 1.0)
            qkv_ref[:, cols] = xv.astype(BF16)
        u_ref[...] = proj[:, qkw:]

    return pl.pallas_call(
        body, name="in_proj", grid=(L // TL,),
        in_specs=[_rows(D), _whole((1, D)), _whole((INW, D)), _rows(128), _rows(128), _rows(128)],
        out_specs=[_rows(D), _rows(qkw), _rows(SW)],
        out_shape=[jax.ShapeDtypeStruct((L, D), BF16), jax.ShapeDtypeStruct((L, qkw), BF16),
                   jax.ShapeDtypeStruct((L, SW), F32)],
        compiler_params=_cp(("parallel",)),
    )(x, g, w_in_t, *tabs)


def _in_proj_dx(dproj, w_in_t, x, g, dres):
    def body(dp_ref, w_ref, x_ref, g_ref, dres_ref, dx_ref, dg_ref):
        dh = jnp.dot(dp_ref[...], w_ref[...], preferred_element_type=F32)
        dx, dg = _rms_bwd_tile(x_ref[...], g_ref[...], dh)
        dx_ref[...] = dx + dres_ref[...]

        @pl.when(pl.program_id(0) == 0)
        def _():
            dg_ref[...] = jnp.zeros_like(dg_ref)

        dg_ref[...] += dg

    return pl.pallas_call(
        body, name="in_proj_dx", grid=(L // TL,),
        in_specs=[_rows(INW), _whole((INW, D)), _rows(D), _whole((1, D)), _rows(D)],
        out_specs=[_rows(D), _whole((1, D))],
        out_shape=[jax.ShapeDtypeStruct((L, D), F32), jax.ShapeDtypeStruct((1, D), F32)],
        compiler_params=_cp(("arbitrary",)),
    )(dproj, w_in_t, x, g, dres)


def _rms_bwd_tile(xv, gv, dh):
    r = lax.rsqrt(jnp.mean(xv * xv, axis=-1, keepdims=True) + EPS)
    a = dh * gv
    dx = r * a - xv * (r * r * r) * jnp.mean(a * xv, axis=-1, keepdims=True)
    dg = jnp.sum(dh * xv * r, axis=0, keepdims=True)
    return dx, dg


def _rms_bwd_post(dh, i, ins, outs):
    x_ref, dres_ref, g_ref = ins
    dx_ref, dxb_ref, dg_ref = outs
    dx, dg = _rms_bwd_tile(x_ref[...], g_ref[...], dh)
    dx = dx + dres_ref[...]
    dx_ref[...] = dx
    dxb_ref[...] = dx.astype(BF16)

    @pl.when(i == 0)
    def _():
        dg_ref[...] = jnp.zeros_like(dg_ref)

    dg_ref[...] += dg


def _final_post(xv, i, ins, outs):
    t_ref, g_ref = ins
    loss_ref, dx_ref, dxb_ref, dg_ref = outs

    @pl.when(i == 0)
    def _():
        loss_ref[...] = jnp.zeros_like(loss_ref)
        dg_ref[...] = jnp.zeros_like(dg_ref)

    gv = g_ref[...]
    r = lax.rsqrt(jnp.mean(xv * xv, axis=-1, keepdims=True) + EPS)
    e = xv * r * gv - t_ref[...]
    loss_ref[...] += 0.5 * jnp.sum(jnp.mean(e * e, axis=-1, keepdims=True), axis=0, keepdims=True)
    dy = e * (1.0 / D)
    a = dy * gv
    dx = r * a - xv * (r * r * r) * jnp.mean(a * xv, axis=-1, keepdims=True)
    dx_ref[...] = dx
    dxb_ref[...] = dx.astype(BF16)
    dg_ref[...] += jnp.sum(dy * xv * r, axis=0, keepdims=True)


def _out_proj(attn, ysg, ga, gs, w_out, x, gf):
    def body(a_ref, s_ref, ga_ref, gs_ref, w_ref, x_ref, gf_ref, m_ref, x1_ref, h2_ref):
        m_ref[:, 0:AW] = _rms(a_ref[...], ga_ref[...]).astype(BF16)
        m_ref[:, AW:AW + SW] = _rms(s_ref[...], gs_ref[...]).astype(BF16)
        x1 = jnp.dot(m_ref[...], w_ref[...], preferred_element_type=F32) + x_ref[...]
        x1_ref[...] = x1
        h2_ref[...] = _rms(x1, gf_ref[...]).astype(BF16)

    return pl.pallas_call(
        body, name="out_proj", grid=(L // TL,),
        in_specs=[_rows(AW), _rows(SW), _whole((1, AW)), _whole((1, SW)), _whole((D, D)), _rows(D), _whole((1, D))],
        out_specs=[_rows(D), _rows(D), _rows(D)],
        out_shape=[jax.ShapeDtypeStruct((L, D), BF16), jax.ShapeDtypeStruct((L, D), F32),
                   jax.ShapeDtypeStruct((L, D), BF16)],
        compiler_params=_cp(("parallel",)),
    )(attn, ysg, ga, gs, w_out, x, gf)


def _out_proj_dx(dx1, w_out, attn, ysg, ga, gs):
    def body(dx_ref, w_ref, a_ref, s_ref, ga_ref, gs_ref, da_ref, ds_ref, dga_ref, dgs_ref):
        @pl.when(pl.program_id(0) == 0)
        def _():
            dga_ref[...] = jnp.zeros_like(dga_ref)
            dgs_ref[...] = jnp.zeros_like(dgs_ref)

        dm = lax.dot_general(dx_ref[...].astype(BF16), w_ref[...], _NT, preferred_element_type=F32)
        dxa, dga = _rms_bwd_tile(a_ref[...], ga_ref[...], dm[:, 0:AW])
        da_ref[...] = dxa
        dga_ref[...] += dga
        dxs, dgs = _rms_bwd_tile(s_ref[...], gs_ref[...], dm[:, AW:AW + SW])
        ds_ref[...] = dxs
        dgs_ref[...] += dgs

    return pl.pallas_call(
        body, name="out_proj_dx", grid=(L // TL,),
        in_specs=[_rows(D), _whole((D, D)), _rows(AW), _rows(SW), _whole((1, AW)), _whole((1, SW))],
        out_specs=[_rows(AW), _rows(SW), _whole((1, AW)), _whole((1, SW))],
        out_shape=[jax.ShapeDtypeStruct((L, AW), F32), jax.ShapeDtypeStruct((L, SW), F32),
                   jax.ShapeDtypeStruct((1, AW), F32), jax.ShapeDtypeStruct((1, SW), F32)],
        compiler_params=_cp(("arbitrary",)),
    )(dx1, w_out, attn, ysg, ga, gs)


def _rope_tables():
    half = HD // 8
    inv_freq = jnp.power(ROPE_THETA, -jnp.arange(half, dtype=F32) / half)
    ang = jnp.arange(L, dtype=F32)[:, None] * inv_freq[None, :]
    cos, sin = jnp.cos(ang), jnp.sin(ang)
    one = jnp.ones((L, HD - 2 * half), F32)
    zero = jnp.zeros((L, HD - 2 * half), F32)
    zh = jnp.zeros((L, half), F32)
    cos64 = jnp.concatenate([cos, cos, one], axis=1)
    sa64 = jnp.concatenate([-sin, zh, zero], axis=1)
    sb64 = jnp.concatenate([zh, sin, zero], axis=1)
    return [jnp.tile(t, (1, 2)) for t in (cos64, sa64, sb64)]


def _rope(xv, cosv, sav, sbv, sign):
    return xv * cosv + sign * (pltpu.roll(xv, 120, 1) * sav + pltpu.roll(xv, 8, 1) * sbv)


def _rope_bwd(dq, dk, dv, du, tabs):
    def body(dq_ref, dk_ref, dv_ref, du_ref, c_ref, sa_ref, sb_ref, o_ref):
        for j in range(AW // 128):
            cols = slice(j * 128, (j + 1) * 128)
            o_ref[:, cols] = _rope(dq_ref[:, cols], c_ref[...], sa_ref[...], sb_ref[...], -1.0).astype(BF16)
        o_ref[:, AW:AW + KVW] = _rope(dk_ref[...], c_ref[...], sa_ref[...], sb_ref[...], -1.0).astype(BF16)
        o_ref[:, AW + KVW:AW + 2 * KVW] = dv_ref[...].astype(BF16)
        o_ref[:, AW + 2 * KVW:] = du_ref[...].astype(BF16)

    def row(width):
        return pl.BlockSpec((TL, width), lambda i: (i, 0))

    return pl.pallas_call(
        body, name="rope_bwd", grid=(L // TL,),
        in_specs=[row(AW), row(KVW), row(KVW), row(SW), row(128), row(128), row(128)],
        out_specs=row(INW), out_shape=jax.ShapeDtypeStruct((L, INW), BF16),
        compiler_params=_cp(("parallel",)),
    )(dq, dk, dv, du, *tabs)


def _attn_window(n):
    start = pl.multiple_of(jnp.clip((n - 1) * BLK, 0, L - WIN), BLK)
    qpos = n * BLK + lax.broadcasted_iota(jnp.int32, (BLK, WIN), 0)
    kpos = start + lax.broadcasted_iota(jnp.int32, (BLK, WIN), 1)
    return start, jnp.abs(kpos - qpos) <= BLK


_NT = (((1,), (1,)), ((), ()))
_TN = (((0,), (0,)), ((), ()))
NEG = -1e30


def _attn_fwd(qkv, sink, gather=()):
    ng = len(gather)

    def body(sink_ref, q_ref, k_ref, v_ref, *rest):
        o_ref, lse_ref = rest[ng], rest[ng + 1]
        s_scr, p_scr = rest[2 * ng + 2], rest[2 * ng + 3]
        n = pl.program_id(0)
        if ng:
            travellers = (rest[:ng], rest[ng + 2:2 * ng + 2], *rest[2 * ng + 4:])
            _gather_under(n, L // BLK, travellers)

        start, valid = _attn_window(n)
        kw = k_ref[pl.ds(start, WIN), :]
        vw = v_ref[pl.ds(start, WIN), :]
        for h in range(NQ):
            kv = h // (NQ // NKV)
            s_scr[h] = lax.dot_general(q_ref[:, h * HD:(h + 1) * HD], kw[:, kv * HD:(kv + 1) * HD], _NT,
                                       preferred_element_type=F32)
        for h in range(NQ):
            s = jnp.where(valid, s_scr[h] * (HD ** -0.5), NEG)
            sk = sink_ref[h]
            m = jnp.maximum(jnp.max(s, axis=-1, keepdims=True), sk)
            p = jnp.exp(s - m)
            den = jnp.sum(p, axis=-1, keepdims=True) + jnp.exp(sk - m)
            p_scr[h] = (p * (1.0 / den)).astype(BF16)
            lse_ref[:, h:h + 1] = m + jnp.log(den)
        for h in range(NQ):
            kv = h // (NQ // NKV)
            o_ref[:, h * HD:(h + 1) * HD] = jnp.dot(p_scr[h], vw[:, kv * HD:(kv + 1) * HD],
                                                    preferred_element_type=F32)
        if ng:
            _gather_done(n, L // BLK, travellers)

    outs = pl.pallas_call(
        body, name="attn_fwd", grid=(L // BLK,),
        in_specs=[pl.BlockSpec(memory_space=pltpu.SMEM),
                  pl.BlockSpec((BLK, AW), lambda n: (n, 0)),
                  pl.BlockSpec((L, KVW), lambda n: (0, AW // KVW)),
                  pl.BlockSpec((L, KVW), lambda n: (0, AW // KVW + 1))] + [ANY] * ng,
        out_specs=[pl.BlockSpec((BLK, AW), lambda n: (n, 0)), pl.BlockSpec((BLK, NQ), lambda n: (n, 0))] + [ANY] * ng,
        out_shape=[jax.ShapeDtypeStruct((L, AW), F32), jax.ShapeDtypeStruct((L, NQ), F32)] + _gather_shapes(gather),
        scratch_shapes=[pltpu.VMEM((NQ, BLK, WIN), F32), pltpu.VMEM((NQ, BLK, WIN), BF16)]
        + (_gather_sems(ng) if ng else []),
        compiler_params=_cp(("arbitrary",) if ng else ("parallel",)),
    )(sink, qkv, qkv, qkv, *gather)
    return outs[0], outs[1], list(outs[2:])


def _attn_bwd(qkv, sink, attn, lse, dattn, ride=()):
    kind, riding = ride if ride else (None, ())
    nr = len(riding)

    def body(*refs):
        if nr:
            riders = (refs[7:7 + nr], refs[11 + nr:11 + 2 * nr], *refs[15 + 2 * nr:])

            @pl.when(pl.program_id(0) == 0)
            def _():
                _start_all(_exchange_copies(kind, *riders))

        compute(*refs[:7], *refs[7 + nr:11 + nr], *refs[11 + 2 * nr:15 + 2 * nr])
        if nr:
            @pl.when(pl.program_id(0) == L // BLK - 1)
            def _():
                _wait_all(_exchange_copies(kind, *riders))

    def compute(sink_ref, q_ref, k_ref, v_ref, o_ref, lse_ref, do_ref, dq_ref, dk_ref, dv_ref, dsink_ref,
                s_scr, dp_scr, p_scr, ds_scr):
        n = pl.program_id(0)

        @pl.when(n == 0)
        def _():
            dk_ref[...] = jnp.zeros_like(dk_ref)
            dv_ref[...] = jnp.zeros_like(dv_ref)
            dsink_ref[...] = jnp.zeros_like(dsink_ref)

        start, valid = _attn_window(n)
        kw = k_ref[pl.ds(start, WIN), :]
        vw = v_ref[pl.ds(start, WIN), :]
        group = NQ // NKV
        for h in range(NQ):
            kv = h // group
            s_scr[h] = lax.dot_general(q_ref[:, h * HD:(h + 1) * HD], kw[:, kv * HD:(kv + 1) * HD], _NT,
                                       preferred_element_type=F32)
            dp_scr[h] = lax.dot_general(do_ref[:, h * HD:(h + 1) * HD].astype(BF16), vw[:, kv * HD:(kv + 1) * HD],
                                        _NT, preferred_element_type=F32)
        for h in range(NQ):
            dd = jnp.sum(do_ref[:, h * HD:(h + 1) * HD] * o_ref[:, h * HD:(h + 1) * HD], axis=-1, keepdims=True)
            lse_h = lse_ref[:, h:h + 1]
            p = jnp.where(valid, jnp.exp(s_scr[h] * (HD ** -0.5) - lse_h), 0.0)
            p_scr[h] = p.astype(BF16)
            ds_scr[h] = (p * (dp_scr[h] - dd) * (HD ** -0.5)).astype(BF16)
            dsk = -jnp.sum(jnp.exp(sink_ref[h] - lse_h) * dd, axis=0, keepdims=True)
            dsink_ref[h:h + 1, :] += jnp.broadcast_to(dsk, (1, 128))
        for kv in range(NKV):
            kh = kw[:, kv * HD:(kv + 1) * HD]
            dk_acc = jnp.zeros((WIN, HD), F32)
            dv_acc = jnp.zeros((WIN, HD), F32)
            for h in range(kv * group, (kv + 1) * group):
                dq_ref[:, h * HD:(h + 1) * HD] = jnp.dot(ds_scr[h], kh, preferred_element_type=F32)
                dk_acc += lax.dot_general(ds_scr[h], q_ref[:, h * HD:(h + 1) * HD], _TN, preferred_element_type=F32)
                dv_acc += lax.dot_general(p_scr[h], do_ref[:, h * HD:(h + 1) * HD].astype(BF16), _TN,
                                          preferred_element_type=F32)
            dk_ref[pl.ds(start, WIN), kv * HD:(kv + 1) * HD] += dk_acc
            dv_ref[pl.ds(start, WIN), kv * HD:(kv + 1) * HD] += dv_acc

    qblk = pl.BlockSpec((BLK, AW), lambda n: (n, 0))
    full = pl.BlockSpec((L, KVW), lambda n: (0, 0))
    outs = pl.pallas_call(
        body, name="attn_bwd", grid=(L // BLK,),
        in_specs=[pl.BlockSpec(memory_space=pltpu.SMEM), qblk,
                  pl.BlockSpec((L, KVW), lambda n: (0, AW // KVW)),
                  pl.BlockSpec((L, KVW), lambda n: (0, AW // KVW + 1)),
                  qblk, pl.BlockSpec((BLK, NQ), lambda n: (n, 0)), qblk] + [ANY] * nr,
        out_specs=[qblk, full, full, pl.BlockSpec((NQ, 128), lambda n: (0, 0))] + [ANY] * nr,
        out_shape=[jax.ShapeDtypeStruct((L, AW), F32), jax.ShapeDtypeStruct((L, KVW), F32),
                   jax.ShapeDtypeStruct((L, KVW), F32), jax.ShapeDtypeStruct((NQ, 128), F32)]
        + (_exchange_shapes(kind, riding) if nr else []),
        scratch_shapes=[pltpu.VMEM((NQ, BLK, WIN), F32), pltpu.VMEM((NQ, BLK, WIN), F32),
                        pltpu.VMEM((NQ, BLK, WIN), BF16), pltpu.VMEM((NQ, BLK, WIN), BF16)]
        + (_exchange_sems(kind, nr) if nr else []),
        compiler_params=_cp(("arbitrary",)),
    )(sink, qkv, qkv, qkv, attn, lse, dattn, *riding)
    return list(outs[:4]), list(outs[4:])


def _perm(a):
    return a.reshape(NSEG, TSEG, a.shape[1]).transpose(1, 0, 2).reshape(L, a.shape[1])


def _unperm(a):
    return a.reshape(TSEG, NSEG, a.shape[1]).transpose(1, 0, 2).reshape(L, a.shape[1])


def _cmul(ar, ai, br, bi):
    return ar * br - ai * bi, ar * bi + ai * br


def _scan_inplace(s_ref, lr, li, rev, visit=None, carried=(), out_ref=None):
    n = lr.shape[1]
    lr8 = jnp.broadcast_to(lr, (NSEG, n))
    li8 = jnp.broadcast_to(li, (NSEG, n))

    def rows(k):
        return pl.ds(pl.multiple_of(jnp.where(rev, TSEG - 1 - k, k) * NSEG, NSEG), NSEG)

    def step(k, c, store):
        sr, si = c
        rs = rows(k)
        pr, pi = _cmul(lr8, li8, sr, si)
        nr = pr + s_ref[rs, 0:n]
        ni = pi + s_ref[rs, n:2 * n]
        if store:
            dst = s_ref if out_ref is None else out_ref
            dst[rs, 0:n] = nr.astype(dst.dtype)
            dst[rs, n:2 * n] = ni.astype(dst.dtype)
        return nr, ni

    z = jnp.zeros((NSEG, n), F32)
    er, ei = lax.fori_loop(0, TSEG, functools.partial(step, store=False), (z, z))
    pr, pi = lr, li
    for _ in range(int(math.log2(TSEG))):
        pr, pi = _cmul(pr, pi, pr, pi)

    seg = lax.broadcasted_iota(jnp.int32, (NSEG, n), 0)

    def moved(val, k):
        down = jnp.where(seg >= k, pltpu.roll(val, k, 0), 0.0)
        up = jnp.where(seg < NSEG - k, pltpu.roll(val, NSEG - k, 0), 0.0)
        return jnp.where(rev, up, down)

    k = 1
    while k < NSEG:
        mr, mi = _cmul(pr, pi, moved(er, k), moved(ei, k))
        er, ei = er + mr, ei + mi
        pr, pi = _cmul(pr, pi, pr, pi)
        k *= 2
    cin_r, cin_i = moved(er, 1), moved(ei, 1)
    if visit is None:
        lax.fori_loop(0, TSEG, functools.partial(step, store=True), (cin_r, cin_i))
        return cin_r, cin_i

    def visited(k, c):
        nr, ni = step(k, c[:2], True)
        return (nr, ni) + tuple(visit(k, nr, ni, c[2:]))

    fin = lax.fori_loop(0, TSEG - 1, visited, (cin_r, cin_i) + tuple(carried))
    last_r, last_i = step(TSEG - 1, fin[:2], True)
    return last_r, last_i, fin[2:]


S5_RC = 512


def _s5_specs():
    u_spec = pl.BlockSpec((L, 128), lambda cb, h, d: (0, cb))
    b_spec = pl.BlockSpec((None, None, 128, 2 * SBW), lambda cb, h, d: (d, cb * 2 + h, 0, 0))
    c_spec = pl.BlockSpec((None, None, 2 * SBW, 128), lambda cb, h, d: (d, cb * 2 + h, 0, 0))
    l_spec = pl.BlockSpec((None, None, 1, SBW), lambda cb, h, d: (d, cb * 2 + h, 0, 0))
    d_spec = pl.BlockSpec((1, 128), lambda cb, h, d: (0, cb))
    return u_spec, b_spec, c_spec, l_spec, d_spec


def _s5_fwd(u_p, bcat, ccat, lam_re, lam_im, dskip, gather=()):
    ng = len(gather)
    grid = (SW // 128, 2, 2)

    def body(*refs):
        if ng:
            step = (pl.program_id(0) * grid[1] + pl.program_id(1)) * grid[2] + pl.program_id(2)
            travellers = (refs[6:6 + ng], refs[8 + ng:8 + 2 * ng], *refs[9 + 2 * ng:])
            _gather_under(step, math.prod(grid), travellers)
        compute(*refs[:6], *refs[6 + ng:8 + ng], refs[8 + 2 * ng])
        if ng:
            _gather_done(step, math.prod(grid), travellers)

    def compute(u_ref, b_ref, c_ref, lr_ref, li_ref, d_ref, y_ref, sb_ref, s_scr):
        first = (pl.program_id(1) == 0) & (pl.program_id(2) == 0)

        def proj(i, _):
            rs = pl.ds(pl.multiple_of(i * S5_RC, S5_RC), S5_RC)
            s_scr[rs, :] = jnp.dot(u_ref[rs, :].astype(BF16), b_ref[...], preferred_element_type=F32)
            return 0

        lax.fori_loop(0, L // S5_RC, proj, 0)
        _scan_inplace(s_scr, lr_ref[...], li_ref[...], pl.program_id(2) == 1, out_ref=sb_ref)

        def out(i, _, opening):
            rs = pl.ds(pl.multiple_of(i * S5_RC, S5_RC), S5_RC)
            yv = jnp.dot(sb_ref[rs, :], c_ref[...], preferred_element_type=F32)
            if opening:
                y_ref[rs, :] = d_ref[...] * u_ref[rs, :] + yv
            else:
                y_ref[rs, :] += yv
            return 0

        @pl.when(first)
        def _():
            lax.fori_loop(0, L // S5_RC, functools.partial(out, opening=True), 0)

        @pl.when(jnp.logical_not(first))
        def _():
            lax.fori_loop(0, L // S5_RC, functools.partial(out, opening=False), 0)

    u_spec, b_spec, c_spec, l_spec, d_spec = _s5_specs()
    outs = pl.pallas_call(
        body, name="s5_fwd", grid=grid,
        in_specs=[u_spec, b_spec, c_spec, l_spec, l_spec, d_spec] + [ANY] * ng,
        out_specs=[u_spec, _s5_state_spec()] + [ANY] * ng,
        out_shape=[jax.ShapeDtypeStruct((L, SW), F32), jax.ShapeDtypeStruct((2, NSB, L, 2 * SBW), BF16)]
        + _gather_shapes(gather),
        scratch_shapes=[pltpu.VMEM((L, 2 * SBW), F32)] + (_gather_sems(ng) if ng else []),
        compiler_params=_cp(("arbitrary",) * 3 if ng else ("parallel", "arbitrary", "arbitrary")),
    )(u_p, bcat, ccat, lam_re, lam_im, dskip, *gather)
    return outs[0], outs[1], list(outs[2:])


def _s5_state_spec():
    return pl.BlockSpec((None, None, L, 2 * SBW), lambda cb, h, d: (d, cb * 2 + h, 0, 0))


def _s5_bwd(u_p, dy_p, states, bcat, ccat, lam_re, lam_im, dskip, ride=()):
    kind, riding = ride if ride else (None, ())
    nr = len(riding)
    grid = (SW // 128, 2, 2)

    def body(*refs):
        work = refs[:8] + refs[8 + nr:14 + nr] + refs[14 + 2 * nr:16 + 2 * nr]
        if nr:
            riders = (refs[8:8 + nr], refs[14 + nr:14 + 2 * nr], *refs[16 + 2 * nr:])
            step = [pl.program_id(d) for d in range(3)]

            @pl.when((step[0] == 0) & (step[1] == 0) & (step[2] == 0))
            def _():
                _start_all(_exchange_copies(kind, *riders))

        compute(*work)
        if nr:
            @pl.when((step[0] == grid[0] - 1) & (step[1] == grid[1] - 1) & (step[2] == grid[2] - 1))
            def _():
                _wait_all(_exchange_copies(kind, *riders))

    def compute(u_ref, dy_ref, s_ref, b_ref, c_ref, lr_ref, li_ref, d_ref,
                du_ref, db_ref, dc_ref, dlr_ref, dli_ref, dd_ref, g_scr, gb_scr):
        first = (pl.program_id(1) == 0) & (pl.program_id(2) == 0)
        rev = pl.program_id(2) == 1

        def dstate(i, _):
            rs = pl.ds(pl.multiple_of(i * S5_RC, S5_RC), S5_RC)
            g_scr[rs, :] = lax.dot_general(dy_ref[rs, :].astype(BF16), c_ref[...], _NT, preferred_element_type=F32)
            return 0

        lax.fori_loop(0, L // S5_RC, dstate, 0)

        def before(rows):
            sv = s_ref[rows, :].astype(F32)
            return sv[:, 0:SBW], sv[:, SBW:2 * SBW]

        def dlam(gr, gi, sr, si, ar, ai):
            return ar + gr * sr + gi * si, ai + gi * sr - gr * si

        db_ref[0:NSEG, :] = jnp.zeros((NSEG, 2 * SBW), F32)

        def visit(k, gr, gi, _):
            ts = jnp.where(rev, k + 1, TSEG - 2 - k)
            sr, si = before(pl.ds(pl.multiple_of(ts * NSEG, NSEG), NSEG))
            ar, ai = dlam(gr, gi, sr, si, db_ref[0:NSEG, 0:SBW], db_ref[0:NSEG, SBW:2 * SBW])
            db_ref[0:NSEG, 0:SBW] = ar
            db_ref[0:NSEG, SBW:2 * SBW] = ai
            return ()

        gr, gi, _ = _scan_inplace(g_scr, lr_ref[...], -li_ref[...], jnp.logical_not(rev), visit, (), gb_scr)
        acc = (db_ref[0:NSEG, 0:SBW], db_ref[0:NSEG, SBW:2 * SBW])
        edge_r, edge_i = before(pl.ds(pl.multiple_of(jnp.where(rev, 0, TSEG - 1) * NSEG, NSEG), NSEG))
        seg = lax.broadcasted_iota(jnp.int32, (NSEG, SBW), 0)
        keep = seg != jnp.where(rev, NSEG - 1, 0)

        def neighbour(e):
            return jnp.where(keep, jnp.where(rev, pltpu.roll(e, NSEG - 1, 0), pltpu.roll(e, 1, 0)), 0.0)

        ar, ai = dlam(gr, gi, neighbour(edge_r), neighbour(edge_i), *acc)
        dlr_ref[...] = jnp.sum(ar, axis=0, keepdims=True)
        dli_ref[...] = jnp.sum(ai, axis=0, keepdims=True)

        db_ref[...] = jnp.zeros_like(db_ref)
        dc_ref[...] = jnp.zeros_like(dc_ref)

        @pl.when(first)
        def _():
            dd_ref[...] = jnp.zeros_like(dd_ref)

        def grads(i, _, opening):
            rs = pl.ds(pl.multiple_of(i * S5_RC, S5_RC), S5_RC)
            uv = u_ref[rs, :]
            dyv = dy_ref[rs, :]
            gb = gb_scr[rs, :]
            db_ref[...] += lax.dot_general(uv.astype(BF16), gb, _TN, preferred_element_type=F32)
            dc_ref[...] += lax.dot_general(dyv.astype(BF16), s_ref[rs, :], _TN, preferred_element_type=F32)
            duv = lax.dot_general(gb, b_ref[...], _NT, preferred_element_type=F32)
            if opening:
                du_ref[rs, :] = d_ref[...] * dyv + duv
                dd_ref[...] += jnp.sum(dyv * uv, axis=0, keepdims=True)
            else:
                du_ref[rs, :] += duv
            return 0

        @pl.when(first)
        def _():
            lax.fori_loop(0, L // S5_RC, functools.partial(grads, opening=True), 0)

        @pl.when(jnp.logical_not(first))
        def _():
            lax.fori_loop(0, L // S5_RC, functools.partial(grads, opening=False), 0)

    u_spec, b_spec, c_spec, l_spec, d_spec = _s5_specs()
    outs = pl.pallas_call(
        body, name="s5_bwd", grid=grid,
        in_specs=[u_spec, u_spec, _s5_state_spec(), b_spec, c_spec, l_spec, l_spec, d_spec] + [ANY] * nr,
        out_specs=[u_spec, b_spec, b_spec, l_spec, l_spec, d_spec] + [ANY] * nr,
        out_shape=[jax.ShapeDtypeStruct((L, SW), F32),
                   jax.ShapeDtypeStruct((2, NSB, 128, 2 * SBW), F32), jax.ShapeDtypeStruct((2, NSB, 128, 2 * SBW), F32),
                   jax.ShapeDtypeStruct((2, NSB, 1, SBW), F32), jax.ShapeDtypeStruct((2, NSB, 1, SBW), F32),
                   jax.ShapeDtypeStruct((1, SW), F32)] + (_exchange_shapes(kind, riding) if nr else []),
        scratch_shapes=[pltpu.VMEM((L, 2 * SBW), F32), pltpu.VMEM((L, 2 * SBW), BF16)]
        + (_exchange_sems(kind, nr) if nr else []),
        compiler_params=_cp(("arbitrary",) * 3 if nr else ("parallel", "arbitrary", "arbitrary")),
    )(u_p, dy_p, states, bcat, ccat, lam_re, lam_im, dskip, *riding)
    return list(outs[:6]), list(outs[6:])


def _s5_params(a_re, a_im, log_step, bt_re, bt_im):
    lam = lax.complex(a_re, a_im)
    step = jnp.exp(log_step)[..., None]
    lam_bar = jnp.exp(lam * step)
    b_bar = ((lam_bar - 1.0) / lam)[..., None, :] * lax.complex(bt_re, bt_im)
    return jnp.real(lam_bar), jnp.imag(lam_bar), jnp.real(b_bar), jnp.imag(b_bar)


def _sel():
    i = jnp.arange(8)[None, :, None]
    j = jnp.arange(4)[None, None, :]
    r = jnp.arange(2)[:, None, None]
    return (i == r * 4 + j).astype(F32)


def _to_bcat(bt_re, bt_im):
    def one(bt):
        return jnp.einsum('dkrjcp,rij->dkricjp', bt.reshape(2, 4, 2, 4, GC, NP), _sel()).reshape(2, NSB, 128, SBW)
    return jnp.concatenate([one(bt_re), one(bt_im)], axis=-1)


def _from_bcat(dbcat):
    def one(dbbd):
        return jnp.einsum('dkricjp,rij->dkrjcp', dbbd.reshape(2, 4, 2, 8, GC, 4, NP), _sel()).reshape(2, NG, GC, NP)
    return one(dbcat[..., :SBW]), one(dbcat[..., SBW:])


def _to_ccat(c_re, c_im):
    def one(cc):
        return jnp.einsum('dkrjcp,rij->dkrjpic', cc.reshape(2, 4, 2, 4, GC, NP), _sel()).reshape(2, NSB, SBW, 128)
    return jnp.concatenate([one(c_re), -one(c_im)], axis=-2)


def _from_ccat(dccat):
    def one(dcbd):
        return jnp.einsum('dkrjpic,rij->dkrjcp', dcbd.reshape(2, 4, 2, 4, NP, 8, GC), _sel()).reshape(2, NG, GC, NP)
    return one(dccat[:, :, :SBW]), -one(dccat[:, :, SBW:])


def _gelu(y):
    return 0.5 * y * (1.0 + lax.erf(y * (2.0 ** -0.5)))


def _gelu_grad(y):
    return 0.5 * (1.0 + lax.erf(y * (2.0 ** -0.5))) + y * jnp.exp(-0.5 * y * y) * ((2.0 * math.pi) ** -0.5)


def _sigmoid(z):
    return 0.5 * jnp.tanh(0.5 * z) + 0.5


def _glu_fwd(y, wg):
    def body(y_ref, w_ref, o_ref, z_ref):
        ys = _gelu(y_ref[...])
        z = jnp.dot(ys.astype(BF16), w_ref[...], preferred_element_type=F32)
        z_ref[...] = z
        o_ref[...] = ys * _sigmoid(z)

    row = pl.BlockSpec((TL, SW), lambda i: (i, 0))
    return pl.pallas_call(
        body, name="glu_fwd", grid=(L // TL,),
        in_specs=[row, pl.BlockSpec((SW, SW), lambda i: (0, 0))], out_specs=[row, row],
        out_shape=[jax.ShapeDtypeStruct((L, SW), F32), jax.ShapeDtypeStruct((L, SW), F32)],
        compiler_params=_cp(("parallel",)),
    )(y, wg)


def _glu_bwd(y, z, dout, wg):
    def body(y_ref, z_ref, do_ref, w_ref, dy_ref, dw_ref):
        @pl.when(pl.program_id(0) == 0)
        def _():
            dw_ref[...] = jnp.zeros_like(dw_ref)

        yv = y_ref[...]
        ys = _gelu(yv)
        sg = _sigmoid(z_ref[...])
        dov = do_ref[...]
        dz = (dov * ys * sg * (1.0 - sg)).astype(BF16)
        dys = dov * sg + lax.dot_general(dz, w_ref[...], _NT, preferred_element_type=F32)
        dy_ref[...] = dys * _gelu_grad(yv)
        dw_ref[...] += lax.dot_general(ys.astype(BF16), dz, _TN, preferred_element_type=F32)

    row = pl.BlockSpec((TL, SW), lambda i: (i, 0))
    wsp = pl.BlockSpec((SW, SW), lambda i: (0, 0))
    return pl.pallas_call(
        body, name="glu_bwd", grid=(L // TL,),
        in_specs=[row, row, row, wsp], out_specs=[row, wsp],
        out_shape=[jax.ShapeDtypeStruct((L, SW), F32), jax.ShapeDtypeStruct((SW, SW), F32)],
        compiler_params=_cp(("arbitrary",)),
    )(y, z, dout, wg)


CT = 256
CR = 128
NCT = DFF // CT


def _shifted(ref, r):
    h = 8 * (4 // ref.dtype.itemsize)
    cur = ref[pl.ds(r, CR), :].astype(F32)
    before = ref[pl.ds(pl.multiple_of(jnp.maximum(r - h, 0), h), h), :][h - 1:h, :].astype(F32)
    after = ref[pl.ds(pl.multiple_of(jnp.minimum(r + CR, L - h), h), h), :][0:1, :].astype(F32)
    before = jnp.where(r > 0, before, 0.0)
    after = jnp.where(r + CR < L, after, 0.0)
    row = lax.broadcasted_iota(jnp.int32, cur.shape, 0)
    prev = jnp.where(row == 0, before, pltpu.roll(cur, 1, 0))
    nxt = jnp.where(row == CR - 1, after, pltpu.roll(cur, CR - 1, 0))
    return prev, cur, nxt


def _conv3(ref, r, w_ref, b_ref):
    prev, cur, nxt = _shifted(ref, r)
    return w_ref[0:1, :] * prev + w_ref[1:2, :] * cur + w_ref[2:3, :] * nxt + b_ref[...]


def _convact_fwd(up, conv_w, conv_b):
    def body(ug_ref, uv_ref, wg_ref, wv_ref, bg_ref, bv_ref, o_ref, g_ref, v_ref):
        def chunk(i, _):
            r = pl.multiple_of(i * CR, CR)
            rs = pl.ds(r, CR)
            g = _conv3(ug_ref, r, wg_ref, bg_ref)
            v = _conv3(uv_ref, r, wv_ref, bv_ref)
            o_ref[rs, :] = (g * _sigmoid(g) * v).astype(BF16)
            g_ref[rs, :] = g.astype(BF16)
            v_ref[rs, :] = v.astype(BF16)
            return 0

        lax.fori_loop(0, L // CR, chunk, 0)

    gcol = pl.BlockSpec((L, CT), lambda j: (0, j))
    vcol = pl.BlockSpec((L, CT), lambda j: (0, j + NCT))
    return pl.pallas_call(
        body, name="convact_fwd", grid=(NCT,),
        in_specs=[gcol, vcol,
                  pl.BlockSpec((3, CT), lambda j: (0, j)), pl.BlockSpec((3, CT), lambda j: (0, j + NCT)),
                  pl.BlockSpec((1, CT), lambda j: (0, j)), pl.BlockSpec((1, CT), lambda j: (0, j + NCT))],
        out_specs=[gcol, gcol, gcol], out_shape=[jax.ShapeDtypeStruct((L, DFF), BF16)] * 3,
        compiler_params=_cp(("parallel",)),
    )(up, up, conv_w, conv_w, conv_b, conv_b)


def _convact_bwd(up, gq, vq, dact, conv_w):
    def body(ug_ref, uv_ref, g_ref, v_ref, da_ref, wg_ref, wv_ref, du_ref, dw_ref, db_ref, d_scr):
        def half_step(gate, u_ref, w_ref):
            db_ref[...] = jnp.zeros_like(db_ref)
            dw_ref[...] = jnp.zeros_like(dw_ref)

            def elementwise(i, _):
                rs = pl.ds(pl.multiple_of(i * CR, CR), CR)
                g = g_ref[rs, :].astype(F32)
                sg = _sigmoid(g)
                da = da_ref[rs, :].astype(F32)
                if gate:
                    d = da * v_ref[rs, :].astype(F32) * sg * (1.0 + g * (1.0 - sg))
                else:
                    d = da * g * sg
                d_scr[rs, :] = d
                db_ref[...] += jnp.sum(d, axis=0, keepdims=True)
                return 0

            lax.fori_loop(0, L // CR, elementwise, 0)

            def transpose_conv(i, _):
                r = pl.multiple_of(i * CR, CR)
                rs = pl.ds(r, CR)
                prev, cur, nxt = _shifted(d_scr, r)
                du_ref[rs, :] = (w_ref[0:1, :] * nxt + w_ref[1:2, :] * cur + w_ref[2:3, :] * prev).astype(BF16)
                uv = u_ref[rs, :].astype(F32)
                for k, d in enumerate((nxt, cur, prev)):
                    dw_ref[k:k + 1, :] += jnp.sum(d * uv, axis=0, keepdims=True)
                return 0

            lax.fori_loop(0, L // CR, transpose_conv, 0)

        @pl.when(pl.program_id(1) == 0)
        def _():
            half_step(True, ug_ref, wg_ref)

        @pl.when(pl.program_id(1) == 1)
        def _():
            half_step(False, uv_ref, wv_ref)

    def col(rows, off):
        return pl.BlockSpec((rows, CT), lambda j, h: (0, j + off))

    def out(rows):
        return pl.BlockSpec((rows, CT), lambda j, h: (0, j + h * NCT))

    return pl.pallas_call(
        body, name="convact_bwd", grid=(NCT, 2),
        in_specs=[col(L, 0), col(L, NCT), col(L, 0), col(L, 0), col(L, 0), col(3, 0), col(3, NCT)],
        out_specs=[out(L), out(3), out(1)],
        out_shape=[jax.ShapeDtypeStruct((L, 2 * DFF), BF16), jax.ShapeDtypeStruct((3, 2 * DFF), F32),
                   jax.ShapeDtypeStruct((1, 2 * DFF), F32)],
        scratch_shapes=[pltpu.VMEM((L, CT), F32)],
        compiler_params=_cp(("parallel", "arbitrary")),
    )(up, up, gq, vq, dact, conv_w, conv_w)


def _local_step(x, tgt, w_in_t, p, attend, scan, stage):
    tabs = _rope_tables()
    lam_re, lam_im, bb_re, bb_im = _s5_params(p["a_re"], p["a_im"], p["log_step"], p["bt_re"], p["bt_im"])
    bcat = _to_bcat(bb_re, bb_im).astype(BF16)
    ccat = _to_ccat(p["c_re"], p["c_im"]).astype(BF16)
    lam_re4, lam_im4 = lam_re.reshape(2, NSB, 1, SBW), lam_im.reshape(2, NSB, 1, SBW)
    dskip = p["d_skip"].reshape(1, SW)
    g_mix, g_ffn, g_fin = p["norm_mix_g"].reshape(1, D), p["norm_ffn_g"].reshape(1, D), p["norm_final_g"].reshape(1, D)
    g_attn, g_ssm = p["norm_attn_g"].reshape(1, AW), p["norm_ssm_g"].reshape(1, SW)
    sink = p["sink"].reshape(NQ)
    conv_b = p["conv_b"].reshape(1, 2 * DFF)

    rows, gain = jax.ShapeDtypeStruct((L, D), F32), jax.ShapeDtypeStruct((1, D), F32)
    rows16 = jax.ShapeDtypeStruct((L, D), BF16)
    h1, qkv, u = _in_proj(x, g_mix, w_in_t, tabs)
    attn, lse, wts = attend(qkv, sink)
    u_p = _perm(u)
    y_p, states, more = scan(u_p, bcat, ccat, lam_re4, lam_im4, dskip)
    wts = dict(wts, **more)
    w_glu, w_out, w_up_t, w_down, conv_w = (wts[k] for k in ("w_glu", "w_out", "w_up_t", "w_down", "conv_w"))
    ysg_p, z_p = _glu_fwd(y_p, w_glu)
    ysg = _unperm(ysg_p)
    mixed, x1, h2 = _out_proj(attn, ysg, g_attn, g_ssm, w_out, x, g_ffn)
    up = _mm(h2, w_up_t, tb=True, name="ffn_up", tn=1408, out_dtype=BF16)
    act, gq, vq = _convact_fwd(up, conv_w, conv_b)
    loss, dx2, dx2b, dg_fin = _mm(
        act, w_down, add=x1, name="ffn_down", tm=512, tk=DFF,
        post=(_final_post, [tgt, g_fin], [jax.ShapeDtypeStruct((1, 1), F32), rows, rows16, gain]))

    def riding(res, ride):
        return res if ride else (res, None)

    dw_down = _mm(act, dx2b, ta=True, name="ffn_down_dw", tm=256, tk=L)
    ride = stage(("w_down",), "cores", [dw_down])
    dact, got = riding(_mm(dx2b, w_down, tb=True, name="ffn_down_dx", tn=1408, out_dtype=BF16, ride=ride), ride)
    ride = stage(("w_down",), "chips", got)
    dup, dconv_w, dconv_b = _convact_bwd(up, gq, vq, dact, conv_w)
    dw_up_t, got = riding(_mm(dup, h2, ta=True, name="ffn_up_dw", tm=512, tk=L, ride=ride), ride)
    stage(("w_down",), "done", got)
    ride = stage(("w_up_t",), "cores", [dw_up_t])
    (dx1, dx1b, dg_ffn), got = riding(
        _mm(dup, w_up_t, name="ffn_up_dx", tm=512, tk=2 * DFF, ride=ride,
            post=(_rms_bwd_post, [x1, dx2, g_ffn], [rows, rows16, gain])), ride)
    ride = stage(("w_up_t",), "chips", got)
    dattn, dysg, dg_attn, dg_ssm = _out_proj_dx(dx1b, w_out, attn, ysg, g_attn, g_ssm)
    dw_out = _mm(mixed, dx1b, ta=True, name="out_proj_dw", tm=512, tk=L)
    dy_p, dw_glu = _glu_bwd(y_p, z_p, _perm(dysg), w_glu)
    (du_p, dbcat, dccat, dlam_re, dlam_im, dd), got = _s5_bwd(u_p, dy_p, states, bcat, ccat, lam_re4, lam_im4, dskip,
                                                              ride=ride)
    stage(("w_up_t",), "done", got)
    dbb_re, dbb_im = _from_bcat(dbcat)
    dc_re, dc_im = _from_ccat(_swap(dccat))
    mix = ("w_out", "w_glu")
    ride = stage(mix, "cores", [dw_out, dw_glu])
    (dq, dk, dv, dsink), got = _attn_bwd(qkv, sink, attn, lse, dattn, ride=ride)
    ride = stage(mix, "chips", got)
    dproj = _rope_bwd(dq, dk, dv, _unperm(du_p), tabs)
    dw_in_t, got = riding(_mm(dproj, h1, ta=True, name="in_proj_dw", tm=640, tk=L, ride=ride), ride)
    stage(mix, "done", got)
    grad_x, dg_mix = _in_proj_dx(dproj, w_in_t, x, g_mix, dx1)

    big = dict(w_in_t=dw_in_t)
    small = dict(norm_mix_g=dg_mix, norm_attn_g=dg_attn, norm_ssm_g=dg_ssm, norm_ffn_g=dg_ffn, norm_final_g=dg_fin,
                 sink=dsink[:, 0], conv_b=dconv_b, d_skip=dd, conv_w=dconv_w,
                 lam_re=dlam_re.reshape(2, NG, NP), lam_im=dlam_im.reshape(2, NG, NP),
                 bb_re=dbb_re, bb_im=dbb_im, c_re=dc_re, c_im=dc_im, loss=loss.reshape(1))
    return grad_x, big, small


ANY = pl.BlockSpec(memory_space=pl.ANY)


def _coords():
    return lax.axis_index("x"), lax.axis_index("y"), lax.axis_index("c")


def _flip(v, b):
    return v + b - 2 * v * b if b else v


def _all_gather(shards, name):
    n = len(shards)

    def body(*refs):
        _gather_start(refs[:n], refs[n:2 * n], *refs[2 * n:])
        _gather_finish(refs[:n], refs[n:2 * n], *refs[2 * n:])

    return pl.pallas_call(
        body, name=name,
        in_specs=[ANY] * n, out_specs=[ANY] * n,
        out_shape=_gather_shapes(shards), scratch_shapes=_gather_sems(n),
    )(*shards)


def _gather_shapes(shards):
    return [jax.ShapeDtypeStruct((NDEV * s.shape[0], s.shape[1]), s.dtype) for s in shards]


def _gather_sems(n):
    return [pltpu.SemaphoreType.DMA((7 * n,)), pltpu.SemaphoreType.DMA((7 * n,)), pltpu.SemaphoreType.DMA((n,))]


def _gather_copies(ins, outs, send_sems, recv_sems, local_sems, a):
    x, y, c = _coords()
    me, sibling = (x, y, c), (x, y, 1 - c)
    chips = [(1 - x, y), (x, 1 - y), (1 - x, 1 - y)]
    r = ins[a].shape[0]

    def rows(px, py, pc):
        return outs[a].at[pl.ds(pl.multiple_of((4 * px + 2 * py + pc) * r, 8), r), :]

    def copy(k, block, to, src=None):
        return pltpu.make_async_remote_copy(
            src_ref=rows(*block) if src is None else src, dst_ref=rows(*block),
            send_sem=send_sems.at[a * 7 + k], recv_sem=recv_sems.at[a * 7 + k],
            device_id=to, device_id_type=pl.DeviceIdType.MESH)

    mine = pltpu.make_async_copy(ins[a], rows(*me), local_sems.at[a])
    first = [copy(0, me, sibling, src=ins[a])]
    first += [copy(1 + j, me, (*chip, c), src=ins[a]) for j, chip in enumerate(chips)]
    passed = [copy(4 + j, (*chip, c), sibling) for j, chip in enumerate(chips)]
    arrivals = [copy(1 + j, (*chip, c), me) for j, chip in enumerate(chips)]
    from_sibling = [copy(0, sibling, me)] + [copy(4 + j, (*chip, 1 - c), me) for j, chip in enumerate(chips)]
    return mine, first, passed, arrivals, from_sibling


def _gather_start(ins, outs, send_sems, recv_sems, local_sems):
    for a in range(len(ins)):
        mine, first, _, _, _ = _gather_copies(ins, outs, send_sems, recv_sems, local_sems, a)
        mine.start()
        for cp in first:
            cp.start()


def _gather_forward(ins, outs, send_sems, recv_sems, local_sems):
    for a in range(len(ins)):
        _, _, passed, arrivals, _ = _gather_copies(ins, outs, send_sems, recv_sems, local_sems, a)
        for arrived, onward in zip(arrivals, passed):
            arrived.wait_recv()
            onward.start()


def _gather_wait(ins, outs, send_sems, recv_sems, local_sems):
    for a in range(len(ins)):
        mine, first, passed, _, from_sibling = _gather_copies(ins, outs, send_sems, recv_sems, local_sems, a)
        for cp in from_sibling:
            cp.wait_recv()
        for cp in first + passed:
            cp.wait_send()
        mine.wait()


def _gather_finish(*refs):
    _gather_forward(*refs)
    _gather_wait(*refs)


def _gather_under(step, steps, travellers):
    @pl.when(step == 0)
    def _():
        _gather_start(*travellers)

    @pl.when(step == (3 * steps) // 4)
    def _():
        _gather_forward(*travellers)


def _gather_done(step, steps, travellers):
    @pl.when(step == steps - 1)
    def _():
        _gather_wait(*travellers)


NCHIP = 4
CHIP_FLIPS = ((1, 0), (0, 1), (1, 1))


def _planned_copies(ins, outs, send_sems, recv_sems, plan):
    return [pltpu.make_async_remote_copy(
        src_ref=ins[a].at[src], dst_ref=outs[a].at[dst], send_sem=send_sems.at[k], recv_sem=recv_sems.at[k],
        device_id=to, device_id_type=pl.DeviceIdType.MESH) for k, (a, src, dst, to) in enumerate(plan)]


def _start_all(copies):
    for cp in copies:
        cp.start()


def _wait_all(copies):
    for cp in copies:
        cp.wait_recv()
    for cp in copies:
        cp.wait_send()


SLOTS = {"cores": NCHIP, "chips": 3}


def _exchange_copies(kind, ins, outs, send_sems, recv_sems):
    x, y, c = _coords()
    plan = []
    for a in range(len(ins)):
        if kind == "cores":
            plan += [(a, 2 * q + 1 - c, q, (x, y, 1 - c)) for q in range(NCHIP)]
        else:
            for j, (fx, fy) in enumerate(CHIP_FLIPS):
                px, py = _flip(x, fx), _flip(y, fy)
                plan.append((a, 2 * px + py, j, (px, py, c)))
    return _planned_copies(ins, outs, send_sems, recv_sems, plan)


def _exchange_shapes(kind, parts):
    return [jax.ShapeDtypeStruct((SLOTS[kind],) + s.shape[1:], s.dtype) for s in parts]


def _exchange_sems(kind, n):
    return [pltpu.SemaphoreType.DMA((SLOTS[kind] * n,)), pltpu.SemaphoreType.DMA((SLOTS[kind] * n,))]


def _exchange(kind, parts, name):
    n = len(parts)

    def body(*refs):
        copies = _exchange_copies(kind, refs[:n], refs[n:2 * n], *refs[2 * n:])
        _start_all(copies)
        _wait_all(copies)

    return pl.pallas_call(
        body, name=name, in_specs=[ANY] * n, out_specs=[ANY] * n,
        out_shape=_exchange_shapes(kind, parts), scratch_shapes=_exchange_sems(kind, n),
    )(*parts)


def _pair_sum(where, part, recv, wire_dtype, name):
    _, r, c = part.shape
    tr = _pick(r, 256, 16)

    def body(w_ref, p_ref, r_ref, pb_ref, own_ref):
        s = p_ref[...] + r_ref[...]
        pb_ref[...] = s.astype(wire_dtype)

        @pl.when(pl.program_id(1) == w_ref[1])
        def _():
            own_ref[...] = s

    return pl.pallas_call(
        body, name=name,
        grid_spec=pltpu.PrefetchScalarGridSpec(
            num_scalar_prefetch=1, grid=(r // tr, NCHIP),
            in_specs=[pl.BlockSpec((None, tr, c), lambda i, q, w: (2 * q + w[0], i, 0)),
                      pl.BlockSpec((None, tr, c), lambda i, q, w: (q, i, 0))],
            out_specs=[pl.BlockSpec((None, tr, c), lambda i, q, w: (q, i, 0)),
                       pl.BlockSpec((tr, c), lambda i, q, w: (i, 0))]),
        out_shape=[jax.ShapeDtypeStruct((NCHIP, r, c), wire_dtype), jax.ShapeDtypeStruct((r, c), F32)],
        compiler_params=_cp(("parallel", "arbitrary")),
    )(where, part, recv)


def _chip_sum(own, recv, name):
    r, c = own.shape
    tr = _pick(r, 256, 16)

    def body(o_ref, r_ref, out_ref):
        acc = o_ref[...]
        for j in range(3):
            acc = acc + r_ref[j].astype(F32)
        out_ref[...] = acc

    return pl.pallas_call(
        body, name=name, grid=(r // tr,),
        in_specs=[pl.BlockSpec((tr, c), lambda i: (i, 0)), pl.BlockSpec((3, tr, c), lambda i: (0, i, 0))],
        out_specs=pl.BlockSpec((tr, c), lambda i: (i, 0)),
        out_shape=jax.ShapeDtypeStruct((r, c), F32),
        compiler_params=_cp(("parallel",)),
    )(own, recv)


def _adamw(w, own, recv, m, v, name):
    r, c = w.shape
    tr = _pick(r, 256, 16)

    def body(w_ref, o_ref, r_ref, m_ref, v_ref, g_ref, d_ref, nm_ref, nv_ref):
        acc = o_ref[...]
        for j in range(3):
            acc = acc + r_ref[j].astype(F32)
        g_ref[...] = acc
        _adamw_refs(w_ref, g_ref, m_ref, v_ref, d_ref, nm_ref, nv_ref)

    blk = pl.BlockSpec((tr, c), lambda i: (i, 0))
    return pl.pallas_call(
        body, name=name, grid=(r // tr,),
        in_specs=[blk, blk, pl.BlockSpec((3, tr, c), lambda i: (0, i, 0)), blk, blk], out_specs=[blk] * 4,
        out_shape=[jax.ShapeDtypeStruct((r, c), F32)] * 4,
        compiler_params=_cp(("parallel",)),
    )(w, own, recv, m, v)


def _adamw_refs(w_ref, g_ref, m_ref, v_ref, d_ref, nm_ref, nv_ref):
    gv = g_ref[...]
    nm = B1 * m_ref[...] + (1.0 - B1) * gv
    nv = B2 * v_ref[...] + (1.0 - B2) * (gv * gv)
    nm_ref[...] = nm
    nv_ref[...] = nv
    d_ref[...] = -LR * ((nm / C1) / (jnp.sqrt(nv / C2) + AEPS) + WD * w_ref[...])


def _adamw_small(ws, gs, ms, vs, name):
    n = len(ws)

    def body(*refs):
        groups = [refs[i * n:(i + 1) * n] for i in range(7)]
        for per_param in zip(*groups):
            _adamw_refs(*per_param)

    vm = pl.BlockSpec(memory_space=pltpu.VMEM)
    outs = pl.pallas_call(
        body, name=name, in_specs=[vm] * (4 * n), out_specs=[vm] * (3 * n),
        out_shape=[jax.ShapeDtypeStruct(a.shape, F32) for a in ws] * 3,
    )(*ws, *gs, *ms, *vs)
    return outs[:n], outs[n:2 * n], outs[2 * n:]


def _swap(a):
    return jnp.swapaxes(a, -1, -2)


VIEWS = {
    "w_in": (lambda a: a[0].T, lambda u: u.T[None]),
    "w_up": (lambda a: a[0].T, lambda u: u.T[None]),
    "w_glu": (lambda a: a[0], lambda u: u[None]),
    "w_out": (lambda a: a[0], lambda u: u[None]),
    "w_down": (lambda a: a[0], lambda u: u[None]),
    "conv_w": (lambda a: a[0], lambda u: u[None]),
    "norm_mix_g": (lambda a: a, lambda u: u),
    "norm_attn_g": (lambda a: a, lambda u: u),
    "norm_ssm_g": (lambda a: a, lambda u: u),
    "norm_ffn_g": (lambda a: a, lambda u: u),
    "norm_final_g": (lambda a: a[None], lambda u: u[0]),
    "conv_b": (lambda a: a, lambda u: u),
    "sink": (lambda a: a, lambda u: u),
    "a_re": (lambda a: a.reshape(2 * NG, NP), lambda u: u.reshape(1, 2, NG, NP)),
    "a_im": (lambda a: a.reshape(2 * NG, NP), lambda u: u.reshape(1, 2, NG, NP)),
    "log_step": (lambda a: a[0], lambda u: u[None]),
    "b_re": (lambda a: _swap(a[0]).reshape(2 * NG * GC, NP), lambda u: _swap(u.reshape(2, NG, GC, NP))[None]),
    "b_im": (lambda a: _swap(a[0]).reshape(2 * NG * GC, NP), lambda u: _swap(u.reshape(2, NG, GC, NP))[None]),
    "c_re": (lambda a: a.reshape(2 * NG * GC, NP), lambda u: u.reshape(1, 2, NG, GC, NP)),
    "c_im": (lambda a: a.reshape(2 * NG * GC, NP), lambda u: u.reshape(1, 2, NG, GC, NP)),
    "d_skip": (lambda a: a[0].T, lambda u: u.T[None]),
}
BIG = ["w_in", "w_glu", "w_out", "w_up", "w_down"]
PACK_W = 1024


def _pack(arrs, rows):
    flat = jnp.concatenate([a.reshape(-1).astype(F32) for a in arrs])
    return jnp.pad(flat, (0, rows * PACK_W - flat.shape[0])).reshape(rows, PACK_W)


def _unpack(packed, shapes):
    flat = packed.reshape(-1)
    out, off = [], 0
    for s in shapes:
        size = math.prod(s)
        out.append(flat[off:off + size].reshape(s))
        off += size
    return out


def kernel(x, norm_mix_g, w_in, a_re, a_im, log_step, b_re, b_im, c_re, c_im, d_skip, w_glu, sink, norm_attn_g, norm_ssm_g, w_out, norm_ffn_g, w_up, conv_w, conv_b, w_down, norm_final_g, loss_target, m_norm_mix_g, m_w_in, m_a_re, m_a_im, m_log_step, m_b_re, m_b_im, m_c_re, m_c_im, m_d_skip, m_w_glu, m_sink, m_norm_attn_g, m_norm_ssm_g, m_w_out, m_norm_ffn_g, m_w_up, m_conv_w, m_conv_b, m_w_down, m_norm_final_g, v_norm_mix_g, v_w_in, v_a_re, v_a_im, v_log_step, v_b_re, v_b_im, v_c_re, v_c_im, v_d_skip, v_w_glu, v_sink, v_norm_attn_g, v_norm_ssm_g, v_w_out, v_norm_ffn_g, v_w_up, v_conv_w, v_conv_b, v_w_down, v_norm_final_g):
    args = dict(locals())
    names = ["norm_mix_g", "w_in", "a_re", "a_im", "log_step", "b_re", "b_im", "c_re", "c_im", "d_skip", "w_glu",
             "sink", "norm_attn_g", "norm_ssm_g", "w_out", "norm_ffn_g", "w_up", "conv_w", "conv_b", "w_down",
             "norm_final_g"]
    w = {k: args[k] for k in names}
    m = {k: args["m_" + k] for k in names}
    v = {k: args["v_" + k] for k in names}

    (w_in_t,) = _all_gather([w_in[0].T.astype(BF16)], "gather_w_in")
    under_attn = dict(w_glu=w_glu[0].astype(BF16), w_out=w_out[0].astype(BF16),
                      conv_w=jnp.pad(conv_w[0], ((0, 5), (0, 0))))
    under_scan = dict(w_up_t=w_up[0].T.astype(BF16), w_down=w_down[0].astype(BF16))

    ax, ay, ac = _coords()
    me = 4 * ax + 2 * ay + ac
    where = jnp.stack([ac, 2 * ax + ay]).astype(jnp.int32)
    parts, own, got = {}, {}, {}

    def split8(g):
        return g.reshape(NDEV, g.shape[0] // NDEV, g.shape[1])

    def attend(qkv, sink_):
        attn, lse, gathered = _attn_fwd(qkv, sink_, gather=list(under_attn.values()))
        wts = dict(zip(under_attn.keys(), gathered))
        wts["conv_w"] = (wts["conv_w"].reshape(NDEV, 8, 2 * DFF // NDEV)[:, :3].transpose(1, 0, 2)
                         .reshape(3, 2 * DFF))
        return attn, lse, wts

    def scan(*operands):
        y_p, states, gathered = _s5_fwd(*operands, gather=list(under_scan.values()))
        return y_p, states, dict(zip(under_scan.keys(), gathered))

    def pair_sum(k, from_core):
        per_chip, own[k] = _pair_sum(where, parts[k], from_core, F32 if k == "small" else BF16, "pair_sum_" + k)
        return per_chip

    def stage(ks, phase, payload):
        if phase == "cores":
            parts.update({k: split8(g) for k, g in zip(ks, payload)})
            return ("cores", [parts[k] for k in ks])
        if phase == "chips":
            return ("chips", [pair_sum(k, fc) for k, fc in zip(ks, payload)])
        got.update(zip(ks, payload))
        return ()

    p = {k: w[k][0] for k in ("norm_mix_g", "a_re", "a_im", "log_step", "c_re", "c_im", "d_skip", "sink",
                              "norm_attn_g", "norm_ssm_g", "norm_ffn_g", "conv_b")}
    p["norm_final_g"] = norm_final_g
    p["bt_re"], p["bt_im"] = _swap(b_re[0]), _swap(b_im[0])
    grad_x, big, small = _local_step(x[0], loss_target[0], w_in_t, p, attend, scan, stage)

    small_names = list(small.keys())
    small_shapes = [small[k].shape for k in small_names]
    n_small = sum(math.prod(s) for s in small_shapes)
    rows_dev = -(-n_small // (PACK_W * NDEV * 16)) * 16
    spack = _pack([small[k] for k in small_names], rows_dev * NDEV)
    late = ["w_in_t", "small"]
    parts.update(w_in_t=split8(big["w_in_t"]), small=spack.reshape(NDEV, rows_dev, PACK_W))
    from_cores = _exchange("cores", [parts[k] for k in late], "exchange_cores")
    from_chips = _exchange("chips", [pair_sum(k, fc) for k, fc in zip(late, from_cores)], "exchange_chips")
    got.update(zip(late, from_chips))
    (small_full,) = _all_gather([_chip_sum(own["small"], got["small"], "chip_sum_small")], "gather_small")
    sm = dict(zip(small_names, _unpack(small_full, small_shapes)))

    _, s5_vjp = jax.vjp(_s5_params, a_re[0], a_im[0], log_step[0], p["bt_re"], p["bt_im"])
    da_re, da_im, dlog_step, dbt_re, dbt_im = s5_vjp((sm["lam_re"], sm["lam_im"], sm["bb_re"], sm["bb_im"]))
    gview = {
        "norm_mix_g": sm["norm_mix_g"], "norm_attn_g": sm["norm_attn_g"], "norm_ssm_g": sm["norm_ssm_g"],
        "norm_ffn_g": sm["norm_ffn_g"], "norm_final_g": sm["norm_final_g"], "conv_b": sm["conv_b"],
        "sink": sm["sink"][None], "a_re": da_re.reshape(2 * NG, NP), "a_im": da_im.reshape(2 * NG, NP),
        "log_step": dlog_step, "b_re": dbt_re.reshape(2 * NG * GC, NP), "b_im": dbt_im.reshape(2 * NG * GC, NP),
        "c_re": sm["c_re"].reshape(2 * NG * GC, NP), "c_im": sm["c_im"].reshape(2 * NG * GC, NP),
        "d_skip": sm["d_skip"].reshape(NG, GC).T,
        "conv_w": lax.dynamic_slice_in_dim(sm["conv_w"], me * (2 * DFF // NDEV), 2 * DFF // NDEV, axis=1),
    }

    dview, mview, vview = {}, {}, {}
    for k, kg in zip(BIG, ("w_in_t", "w_glu", "w_out", "w_up_t", "w_down")):
        to = VIEWS[k][0]
        gview[k], dview[k], mview[k], vview[k] = _adamw(to(w[k]), own[kg], got[kg], to(m[k]), to(v[k]), "adamw_" + k)
    rest = [k for k in names if k not in BIG]
    outs = _adamw_small([VIEWS[k][0](w[k]) for k in rest], [gview[k] for k in rest],
                        [VIEWS[k][0](m[k]) for k in rest], [VIEWS[k][0](v[k]) for k in rest], "adamw_small")
    for dst, vals in zip((dview, mview, vview), outs):
        dst.update(dict(zip(rest, vals)))

    def back(views):
        return [VIEWS[k][1](views[k]) for k in names]

    return (sm["loss"][0], grad_x[None], *back(gview), *back(dview), *back(mview), *back(vview))
```
